```python
import math
import jax, jax.numpy as jnp
from jax import lax
import numpy as np

D_MODEL = 1024
BATCH = 8
SEQ = 4096
DEPTH = 4

GRID_W = 64
EPS = 1e-6
NA_HEADS = 8
NA_HEAD_DIM = 64
NA_WIN_R = 8
NA_WIN_C = 16
NA_QCOLS = 16
NA_KCOLS = 2 * NA_WIN_C
GLA_HEADS = 4
GLA_DK = 64
GLA_DV = 128
GLA_GATE_RANK = 16
GLA_GATE_TAU = 16.0
GLA_CHUNK = 64
MLA_HEADS = 4
MLA_Q_RANK = 256
MLA_KV_RANK = 256
MLA_NOPE = 128
MLA_ROPE = 64
MLA_V = 128
MLA_QBLOCK = 128
ROPE_THETA = 10000.0
D_FF = 2816
N_BRANCH = 3
NA_W = NA_HEADS * NA_HEAD_DIM
GLA_QK_W = GLA_HEADS * GLA_DK
GLA_V_W = GLA_HEADS * GLA_DV
MLA_QK_HEAD = MLA_NOPE + MLA_ROPE
MLA_V_W = MLA_HEADS * MLA_V
IN_SPLITS = (NA_W, NA_W, NA_W,
             GLA_QK_W, GLA_QK_W, GLA_V_W, GLA_V_W, GLA_GATE_RANK, GLA_GATE_RANK,
             MLA_Q_RANK, MLA_KV_RANK, MLA_ROPE,
             N_BRANCH * D_MODEL)
D_IN = 6752

kernel_name = "hybrid_na_gla_mla_macaron_encoder"


def rms_norm(x, g):
    xf = x.astype(jnp.float32)
    y = xf * lax.rsqrt(jnp.mean(xf * xf, axis=-1, keepdims=True) + EPS)
    return (y * g.astype(jnp.float32)).astype(x.dtype)


def swiglu(h, w1, w3, w2):
    return (jax.nn.silu(h @ w1) * (h @ w3)) @ w2


def split_cols(z, sizes):
    out, start = [], 0
    for n in sizes:
        out.append(z[..., start:start + n])
        start += n
    return out


def neighborhood_attention(q, k, v, rpb):
    B, S, H, d = q.shape
    rows = S // GRID_W
    win_r = min(NA_WIN_R, rows)
    nj = GRID_W // NA_QCOLS
    r = np.arange(rows)
    r0 = np.clip(r - win_r // 2, 0, rows - win_r)
    ridx = r0[:, None] + np.arange(win_r)
    j = np.arange(nj)
    k0 = np.clip(j * NA_QCOLS - NA_WIN_C // 2, 0, GRID_W - NA_KCOLS)
    cidx = k0[:, None] + np.arange(NA_KCOLS)
    qc = j[:, None] * NA_QCOLS + np.arange(NA_QCOLS)
    c0 = np.clip(qc - NA_WIN_C // 2, 0, GRID_W - NA_WIN_C)
    col_ok = (cidx[:, None, :] >= c0[..., None]) & (cidx[:, None, :] < c0[..., None] + NA_WIN_C)
    mask = np.broadcast_to(col_ok[:, :, None, :], (nj, NA_QCOLS, win_r, NA_KCOLS)).reshape(nj, NA_QCOLS, win_r * NA_KCOLS)
    dr = ridx - r[:, None] + (NA_WIN_R - 1)
    dc = np.clip(cidx[:, None, :] - qc[:, :, None] + (NA_WIN_C - 1), 0, 2 * NA_WIN_C - 2)
    bias = rpb[:, dr[:, None, None, :, None], dc[None, :, :, None, :]]
    bias = bias.reshape(H, rows, nj, NA_QCOLS, win_r * NA_KCOLS)

    ri = ridx[:, None, :, None]
    ci = cidx[None, :, None, :]
    kg = k.reshape(B, rows, GRID_W, H, d)[:, ri, ci].reshape(B, rows, nj, win_r * NA_KCOLS, H, d)
    vg = v.reshape(B, rows, GRID_W, H, d)[:, ri, ci].reshape(B, rows, nj, win_r * NA_KCOLS, H, d)
    qg = q.reshape(B, rows, nj, NA_QCOLS, H, d)
    s = jnp.einsum('brjqhd,brjkhd->bhrjqk', qg, kg).astype(jnp.float32) * (d ** -0.5)
    s = jnp.where(mask, s + bias.astype(jnp.float32), -1e30)
    p = jax.nn.softmax(s, axis=-1).astype(v.dtype)
    o = jnp.einsum('bhrjqk,brjkhd->brjqhd', p, vg)
    return o.reshape(B, S, H * d)


def gla_chunked(q, k, v, g, strict):
    B, H, S, dk = q.shape
    dv = v.shape[-1]
    n = S // GLA_CHUNK
    q = q.reshape(B, H, n, GLA_CHUNK, dk)
    k = k.reshape(B, H, n, GLA_CHUNK, dk)
    g = g.reshape(B, H, n, GLA_CHUNK, dk)
    v = v.reshape(B, H, n, GLA_CHUNK, dv)
    b = jnp.cumsum(g, axis=3)
    b_last = b[:, :, :, -1:, :]
    qe = q * jnp.exp(b)
    ke = k * jnp.exp(-b)
    k_end = k * jnp.exp(b_last - b)
    tri = np.tril(np.ones((GLA_CHUNK, GLA_CHUNK), dtype=bool), -1 if strict else 0)
    a = jnp.where(tri, jnp.einsum('bhnid,bhnjd->bhnij', qe, ke), 0.0)
    o_intra = jnp.einsum('bhnij,bhnjv->bhniv', a, v)
    upd = jnp.einsum('bhncd,bhncv->bhndv', k_end, v)
    decay = jnp.exp(b_last[:, :, :, 0, :])

    def step(state, inp):
        dec, u = inp
        return dec[..., None] * state + u, state

    init = jnp.zeros((B, H, dk, dv), q.dtype)
    _, s_prev = lax.scan(step, init, (jnp.moveaxis(decay, 2, 0), jnp.moveaxis(upd, 2, 0)))
    s_prev = jnp.moveaxis(s_prev, 0, 2)
    o = o_intra + jnp.einsum('bhnid,bhndv->bhniv', qe, s_prev)
    return o.reshape(B, H, S, dv)


def gla_bidirectional(q, k, v, g_fwd, g_bwd):
    t = lambda a: jnp.swapaxes(a.astype(jnp.float32), 1, 2)
    q, k, v, g_fwd, g_bwd = t(q), t(k), t(v), t(g_fwd), t(g_bwd)
    q = q * (GLA_DK ** -0.5)
    flip = lambda a: a[:, :, ::-1]
    o_f = gla_chunked(q, k, v, g_fwd, False)
    o_b = flip(gla_chunked(flip(q), flip(k), flip(v), flip(g_bwd), True))
    return jnp.swapaxes(o_f + o_b, 1, 2)


def rope_tables(S):
    half = MLA_ROPE // 2
    inv = ROPE_THETA ** (-jnp.arange(half, dtype=jnp.float32) / half)
    ang = jnp.arange(S, dtype=jnp.float32)[:, None] * inv[None, :]
    return jnp.cos(ang), jnp.sin(ang)


def apply_rope(x, cos, sin):
    xf = x.astype(jnp.float32)
    x1, x2 = xf[..., :MLA_ROPE // 2], xf[..., MLA_ROPE // 2:]
    c, s = cos[None, :, None, :], sin[None, :, None, :]
    return jnp.concatenate([x1 * c - x2 * s, x1 * s + x2 * c], axis=-1).astype(x.dtype)


def blocked_softmax_attention(q, k, v, scale):
    B, S, H, dq = q.shape
    nb = S // MLA_QBLOCK
    qb = jnp.moveaxis(q.reshape(B, nb, MLA_QBLOCK, H, dq), 1, 0)

    def one(qblk):
        s = jnp.einsum('bqhd,bkhd->bhqk', qblk, k).astype(jnp.float32) * scale
        p = jax.nn.softmax(s, axis=-1).astype(v.dtype)
        return jnp.einsum('bhqk,bkhd->bqhd', p, v)

    o = lax.map(one, qb)
    return jnp.moveaxis(o, 0, 1).reshape(B, S, H * v.shape[-1])


def mla_attention(c_q, c_kv, k_rope, cq_norm, ckv_norm, w_uq, w_ukv, q_norm, k_norm, cos, sin):
    B, S, _ = c_q.shape
    q = (rms_norm(c_q, cq_norm) @ w_uq).reshape(B, S, MLA_HEADS, MLA_QK_HEAD)
    kv = (rms_norm(c_kv, ckv_norm) @ w_ukv).reshape(B, S, MLA_HEADS, MLA_NOPE + MLA_V)
    k_nope, v = kv[..., :MLA_NOPE], kv[..., MLA_NOPE:]
    k_r = jnp.broadcast_to(k_rope[:, :, None, :], (B, S, MLA_HEADS, MLA_ROPE))
    k = jnp.concatenate([k_nope, k_r], axis=-1)
    q = rms_norm(q, q_norm)
    k = rms_norm(k, k_norm)
    q = jnp.concatenate([q[..., :MLA_NOPE], apply_rope(q[..., MLA_NOPE:], cos, sin)], axis=-1)
    k = jnp.concatenate([k[..., :MLA_NOPE], apply_rope(k[..., MLA_NOPE:], cos, sin)], axis=-1)
    return blocked_softmax_attention(q, k, v, MLA_QK_HEAD ** -0.5)


def _fwd_setup_inputs(seed: int = 0) -> dict:
    key = jax.random.key(seed)
    ks = iter(jax.random.split(key, 32))
    L = DEPTH
    f32 = jnp.float32

    def w(shape, fan_in):
        return jax.random.normal(next(ks), shape, f32) * (fan_in ** -0.5)

    def gain(shape):
        return 1.0 + 0.05 * jax.random.normal(next(ks), shape, f32)

    def small(shape, scale, offset=0.0):
        return offset + scale * jax.random.normal(next(ks), shape, f32)

    return {
        "x": jax.random.normal(next(ks), (BATCH, SEQ, D_MODEL), f32),
        "ffn1_norm": gain((L, D_MODEL)),
        "ffn1_w1": w((L, D_MODEL, D_FF), D_MODEL),
        "ffn1_w3": w((L, D_MODEL, D_FF), D_MODEL),
        "ffn1_w2": w((L, D_FF, D_MODEL), D_FF),
        "mix_norm": gain((L, D_MODEL)),
        "w_in": w((L, D_MODEL, D_IN), D_MODEL),
        "na_q_norm": gain((L, NA_HEAD_DIM)),
        "na_k_norm": gain((L, NA_HEAD_DIM)),
        "na_rpb": small((L, NA_HEADS, 2 * NA_WIN_R - 1, 2 * NA_WIN_C - 1), 0.1),
        "gla_gf_up": w((L, GLA_GATE_RANK, GLA_QK_W), GLA_GATE_RANK),
        "gla_gf_bias": small((L, GLA_QK_W), 0.1, 2.0),
        "gla_gb_up": w((L, GLA_GATE_RANK, GLA_QK_W), GLA_GATE_RANK),
        "gla_gb_bias": small((L, GLA_QK_W), 0.1, 2.0),
        "gla_out_norm": gain((L, GLA_DV)),
        "mla_cq_norm": gain((L, MLA_Q_RANK)),
        "mla_ckv_norm": gain((L, MLA_KV_RANK)),
        "mla_w_uq": w((L, MLA_Q_RANK, MLA_HEADS * MLA_QK_HEAD), MLA_Q_RANK),
        "mla_w_ukv": w((L, MLA_KV_RANK, MLA_HEADS * (MLA_NOPE + MLA_V)), MLA_KV_RANK),
        "mla_q_norm": gain((L, MLA_QK_HEAD)),
        "mla_k_norm": gain((L, MLA_QK_HEAD)),
        "w_br_na": w((L, NA_W, D_MODEL), NA_W),
        "w_br_gla": w((L, GLA_V_W, D_MODEL), GLA_V_W),
        "w_br_mla": w((L, MLA_V_W, D_MODEL), MLA_V_W),
        "w_out": w((L, D_MODEL, D_MODEL), D_MODEL),
        "ffn2_norm": gain((L, D_MODEL)),
        "ffn2_w1": w((L, D_MODEL, D_FF), D_MODEL),
        "ffn2_w3": w((L, D_MODEL, D_FF), D_MODEL),
        "ffn2_w2": w((L, D_FF, D_MODEL), D_FF),
    }


def _fwd_reference(x, ffn1_norm, ffn1_w1, ffn1_w3, ffn1_w2, mix_norm, w_in,
              na_q_norm, na_k_norm, na_rpb,
              gla_gf_up, gla_gf_bias, gla_gb_up, gla_gb_bias, gla_out_norm,
              mla_cq_norm, mla_ckv_norm, mla_w_uq, mla_w_ukv, mla_q_norm, mla_k_norm,
              w_br_na, w_br_gla, w_br_mla, w_out,
              ffn2_norm, ffn2_w1, ffn2_w3, ffn2_w2):
    B, S, D = x.shape
    cos, sin = rope_tables(S)
    for l in range(DEPTH):
        x = x + 0.5 * swiglu(rms_norm(x, ffn1_norm[l]), ffn1_w1[l], ffn1_w3[l], ffn1_w2[l])

        h = rms_norm(x, mix_norm[l])
        z = h @ w_in[l]
        (na_q, na_k, na_v, gq, gk, gv, gr, gfl, gbl, c_q, c_kv, k_rope, gates) = split_cols(z, IN_SPLITS)

        qa = rms_norm(na_q.reshape(B, S, NA_HEADS, NA_HEAD_DIM), na_q_norm[l])
        ka = rms_norm(na_k.reshape(B, S, NA_HEADS, NA_HEAD_DIM), na_k_norm[l])
        va = na_v.reshape(B, S, NA_HEADS, NA_HEAD_DIM)
        y_na = neighborhood_attention(qa, ka, va, na_rpb[l])

        g_f = jax.nn.log_sigmoid((gfl @ gla_gf_up[l] + gla_gf_bias[l]).astype(jnp.float32)) / GLA_GATE_TAU
        g_b = jax.nn.log_sigmoid((gbl @ gla_gb_up[l] + gla_gb_bias[l]).astype(jnp.float32)) / GLA_GATE_TAU
        o_gla = gla_bidirectional(gq.reshape(B, S, GLA_HEADS, GLA_DK), gk.reshape(B, S, GLA_HEADS, GLA_DK),
                                  gv.reshape(B, S, GLA_HEADS, GLA_DV),
                                  g_f.reshape(B, S, GLA_HEADS, GLA_DK), g_b.reshape(B, S, GLA_HEADS, GLA_DK))
        o_gla = rms_norm(o_gla, gla_out_norm[l]).astype(x.dtype).reshape(B, S, GLA_V_W)
        y_gla = o_gla * jax.nn.silu(gr)

        y_mla = mla_attention(c_q, c_kv, k_rope, mla_cq_norm[l], mla_ckv_norm[l], mla_w_uq[l], mla_w_ukv[l],
                              mla_q_norm[l], mla_k_norm[l], cos, sin)

        gt = jax.nn.sigmoid(gates.reshape(B, S, N_BRANCH, D))
        mixed = (gt[:, :, 0] * (y_na @ w_br_na[l])
                 + gt[:, :, 1] * (y_gla @ w_br_gla[l])
                 + gt[:, :, 2] * (y_mla @ w_br_mla[l]))
        x = x + mixed @ w_out[l]

        x = x + 0.5 * swiglu(rms_norm(x, ffn2_norm[l]), ffn2_w1[l], ffn2_w3[l], ffn2_w2[l])
    return x


import jax as _jax
import jax.numpy as _jnp

TWIN_FORMAT = 'train_step'
FWD_PARAMS = ['x', 'ffn1_norm', 'ffn1_w1', 'ffn1_w3', 'ffn1_w2', 'mix_norm', 'w_in', 'na_q_norm', 'na_k_norm', 'na_rpb', 'gla_gf_up', 'gla_gf_bias', 'gla_gb_up', 'gla_gb_bias', 'gla_out_norm', 'mla_cq_norm', 'mla_ckv_norm', 'mla_w_uq', 'mla_w_ukv', 'mla_q_norm', 'mla_k_norm', 'w_br_na', 'w_br_gla', 'w_br_mla', 'w_out', 'ffn2_norm', 'ffn2_w1', 'ffn2_w3', 'ffn2_w2']
TWIN_WEIGHTS = ['ffn1_norm', 'ffn1_w1', 'ffn1_w3', 'ffn1_w2', 'mix_norm', 'w_in', 'na_q_norm', 'na_k_norm', 'na_rpb', 'gla_gf_up', 'gla_gf_bias', 'gla_gb_up', 'gla_gb_bias', 'gla_out_norm', 'mla_cq_norm', 'mla_ckv_norm', 'mla_w_uq', 'mla_w_ukv', 'mla_q_norm', 'mla_k_norm', 'w_br_na', 'w_br_gla', 'w_br_mla', 'w_out', 'ffn2_norm', 'ffn2_w1', 'ffn2_w3', 'ffn2_w2']
TWIN_DIFF_INPUT = 'x'
TWIN_INPUTS = ['x', 'ffn1_norm', 'ffn1_w1', 'ffn1_w3', 'ffn1_w2', 'mix_norm', 'w_in', 'na_q_norm', 'na_k_norm', 'na_rpb', 'gla_gf_up', 'gla_gf_bias', 'gla_gb_up', 'gla_gb_bias', 'gla_out_norm', 'mla_cq_norm', 'mla_ckv_norm', 'mla_w_uq', 'mla_w_ukv', 'mla_q_norm', 'mla_k_norm', 'w_br_na', 'w_br_gla', 'w_br_mla', 'w_out', 'ffn2_norm', 'ffn2_w1', 'ffn2_w3', 'ffn2_w2', 'loss_target', 'm_ffn1_norm', 'm_ffn1_w1', 'm_ffn1_w3', 'm_ffn1_w2', 'm_mix_norm', 'm_w_in', 'm_na_q_norm', 'm_na_k_norm', 'm_na_rpb', 'm_gla_gf_up', 'm_gla_gf_bias', 'm_gla_gb_up', 'm_gla_gb_bias', 'm_gla_out_norm', 'm_mla_cq_norm', 'm_mla_ckv_norm', 'm_mla_w_uq', 'm_mla_w_ukv', 'm_mla_q_norm', 'm_mla_k_norm', 'm_w_br_na', 'm_w_br_gla', 'm_w_br_mla', 'm_w_out', 'm_ffn2_norm', 'm_ffn2_w1', 'm_ffn2_w3', 'm_ffn2_w2', 'v_ffn1_norm', 'v_ffn1_w1', 'v_ffn1_w3', 'v_ffn1_w2', 'v_mix_norm', 'v_w_in', 'v_na_q_norm', 'v_na_k_norm', 'v_na_rpb', 'v_gla_gf_up', 'v_gla_gf_bias', 'v_gla_gb_up', 'v_gla_gb_bias', 'v_gla_out_norm', 'v_mla_cq_norm', 'v_mla_ckv_norm', 'v_mla_w_uq', 'v_mla_w_ukv', 'v_mla_q_norm', 'v_mla_k_norm', 'v_w_br_na', 'v_w_br_gla', 'v_w_br_mla', 'v_w_out', 'v_ffn2_norm', 'v_ffn2_w1', 'v_ffn2_w3', 'v_ffn2_w2']
TWIN_OUTPUTS = ['loss', 'grad_x', 'grad_ffn1_norm', 'grad_ffn1_w1', 'grad_ffn1_w3', 'grad_ffn1_w2', 'grad_mix_norm', 'grad_w_in', 'grad_na_q_norm', 'grad_na_k_norm', 'grad_na_rpb', 'grad_gla_gf_up', 'grad_gla_gf_bias', 'grad_gla_gb_up', 'grad_gla_gb_bias', 'grad_gla_out_norm', 'grad_mla_cq_norm', 'grad_mla_ckv_norm', 'grad_mla_w_uq', 'grad_mla_w_ukv', 'grad_mla_q_norm', 'grad_mla_k_norm', 'grad_w_br_na', 'grad_w_br_gla', 'grad_w_br_mla', 'grad_w_out', 'grad_ffn2_norm', 'grad_ffn2_w1', 'grad_ffn2_w3', 'grad_ffn2_w2', 'delta_ffn1_norm', 'delta_ffn1_w1', 'delta_ffn1_w3', 'delta_ffn1_w2', 'delta_mix_norm', 'delta_w_in', 'delta_na_q_norm', 'delta_na_k_norm', 'delta_na_rpb', 'delta_gla_gf_up', 'delta_gla_gf_bias', 'delta_gla_gb_up', 'delta_gla_gb_bias', 'delta_gla_out_norm', 'delta_mla_cq_norm', 'delta_mla_ckv_norm', 'delta_mla_w_uq', 'delta_mla_w_ukv', 'delta_mla_q_norm', 'delta_mla_k_norm', 'delta_w_br_na', 'delta_w_br_gla', 'delta_w_br_mla', 'delta_w_out', 'delta_ffn2_norm', 'delta_ffn2_w1', 'delta_ffn2_w3', 'delta_ffn2_w2', 'new_m_ffn1_norm', 'new_m_ffn1_w1', 'new_m_ffn1_w3', 'new_m_ffn1_w2', 'new_m_mix_norm', 'new_m_w_in', 'new_m_na_q_norm', 'new_m_na_k_norm', 'new_m_na_rpb', 'new_m_gla_gf_up', 'new_m_gla_gf_bias', 'new_m_gla_gb_up', 'new_m_gla_gb_bias', 'new_m_gla_out_norm', 'new_m_mla_cq_norm', 'new_m_mla_ckv_norm', 'new_m_mla_w_uq', 'new_m_mla_w_ukv', 'new_m_mla_q_norm', 'new_m_mla_k_norm', 'new_m_w_br_na', 'new_m_w_br_gla', 'new_m_w_br_mla', 'new_m_w_out', 'new_m_ffn2_norm', 'new_m_ffn2_w1', 'new_m_ffn2_w3', 'new_m_ffn2_w2', 'new_v_ffn1_norm', 'new_v_ffn1_w1', 'new_v_ffn1_w3', 'new_v_ffn1_w2', 'new_v_mix_norm', 'new_v_w_in', 'new_v_na_q_norm', 'new_v_na_k_norm', 'new_v_na_rpb', 'new_v_gla_gf_up', 'new_v_gla_gf_bias', 'new_v_gla_gb_up', 'new_v_gla_gb_bias', 'new_v_gla_out_norm', 'new_v_mla_cq_norm', 'new_v_mla_ckv_norm', 'new_v_mla_w_uq', 'new_v_mla_w_ukv', 'new_v_mla_q_norm', 'new_v_mla_k_norm', 'new_v_w_br_na', 'new_v_w_br_gla', 'new_v_w_br_mla', 'new_v_w_out', 'new_v_ffn2_norm', 'new_v_ffn2_w1', 'new_v_ffn2_w3', 'new_v_ffn2_w2']
TWIN_LEAF_KINDS = {'loss': 'loss', 'grad_x': 'grad_x', 'grad_ffn1_norm': 'grad_w', 'grad_ffn1_w1': 'grad_w', 'grad_ffn1_w3': 'grad_w', 'grad_ffn1_w2': 'grad_w', 'grad_mix_norm': 'grad_w', 'grad_w_in': 'grad_w', 'grad_na_q_norm': 'grad_w', 'grad_na_k_norm': 'grad_w', 'grad_na_rpb': 'grad_w', 'grad_gla_gf_up': 'grad_w', 'grad_gla_gf_bias': 'grad_w', 'grad_gla_gb_up': 'grad_w', 'grad_gla_gb_bias': 'grad_w', 'grad_gla_out_norm': 'grad_w', 'grad_mla_cq_norm': 'grad_w', 'grad_mla_ckv_norm': 'grad_w', 'grad_mla_w_uq': 'grad_w', 'grad_mla_w_ukv': 'grad_w', 'grad_mla_q_norm': 'grad_w', 'grad_mla_k_norm': 'grad_w', 'grad_w_br_na': 'grad_w', 'grad_w_br_gla': 'grad_w', 'grad_w_br_mla': 'grad_w', 'grad_w_out': 'grad_w', 'grad_ffn2_norm': 'grad_w', 'grad_ffn2_w1': 'grad_w', 'grad_ffn2_w3': 'grad_w', 'grad_ffn2_w2': 'grad_w', 'delta_ffn1_norm': 'delta_w', 'delta_ffn1_w1': 'delta_w', 'delta_ffn1_w3': 'delta_w', 'delta_ffn1_w2': 'delta_w', 'delta_mix_norm': 'delta_w', 'delta_w_in': 'delta_w', 'delta_na_q_norm': 'delta_w', 'delta_na_k_norm': 'delta_w', 'delta_na_rpb': 'delta_w', 'delta_gla_gf_up': 'delta_w', 'delta_gla_gf_bias': 'delta_w', 'delta_gla_gb_up': 'delta_w', 'delta_gla_gb_bias': 'delta_w', 'delta_gla_out_norm': 'delta_w', 'delta_mla_cq_norm': 'delta_w', 'delta_mla_ckv_norm': 'delta_w', 'delta_mla_w_uq': 'delta_w', 'delta_mla_w_ukv': 'delta_w', 'delta_mla_q_norm': 'delta_w', 'delta_mla_k_norm': 'delta_w', 'delta_w_br_na': 'delta_w', 'delta_w_br_gla': 'delta_w', 'delta_w_br_mla': 'delta_w', 'delta_w_out': 'delta_w', 'delta_ffn2_norm': 'delta_w', 'delta_ffn2_w1': 'delta_w', 'delta_ffn2_w3': 'delta_w', 'delta_ffn2_w2': 'delta_w', 'new_m_ffn1_norm': 'new_m', 'new_m_ffn1_w1': 'new_m', 'new_m_ffn1_w3': 'new_m', 'new_m_ffn1_w2': 'new_m', 'new_m_mix_norm': 'new_m', 'new_m_w_in': 'new_m', 'new_m_na_q_norm': 'new_m', 'new_m_na_k_norm': 'new_m', 'new_m_na_rpb': 'new_m', 'new_m_gla_gf_up': 'new_m', 'new_m_gla_gf_bias': 'new_m', 'new_m_gla_gb_up': 'new_m', 'new_m_gla_gb_bias': 'new_m', 'new_m_gla_out_norm': 'new_m', 'new_m_mla_cq_norm': 'new_m', 'new_m_mla_ckv_norm': 'new_m', 'new_m_mla_w_uq': 'new_m', 'new_m_mla_w_ukv': 'new_m', 'new_m_mla_q_norm': 'new_m', 'new_m_mla_k_norm': 'new_m', 'new_m_w_br_na': 'new_m', 'new_m_w_br_gla': 'new_m', 'new_m_w_br_mla': 'new_m', 'new_m_w_out': 'new_m', 'new_m_ffn2_norm': 'new_m', 'new_m_ffn2_w1': 'new_m', 'new_m_ffn2_w3': 'new_m', 'new_m_ffn2_w2': 'new_m', 'new_v_ffn1_norm': 'new_v', 'new_v_ffn1_w1': 'new_v', 'new_v_ffn1_w3': 'new_v', 'new_v_ffn1_w2': 'new_v', 'new_v_mix_norm': 'new_v', 'new_v_w_in': 'new_v', 'new_v_na_q_norm': 'new_v', 'new_v_na_k_norm': 'new_v', 'new_v_na_rpb': 'new_v', 'new_v_gla_gf_up': 'new_v', 'new_v_gla_gf_bias': 'new_v', 'new_v_gla_gb_up': 'new_v', 'new_v_gla_gb_bias': 'new_v', 'new_v_gla_out_norm': 'new_v', 'new_v_mla_cq_norm': 'new_v', 'new_v_mla_ckv_norm': 'new_v', 'new_v_mla_w_uq': 'new_v', 'new_v_mla_w_ukv': 'new_v', 'new_v_mla_q_norm': 'new_v', 'new_v_mla_k_norm': 'new_v', 'new_v_w_br_na': 'new_v', 'new_v_w_br_gla': 'new_v', 'new_v_w_br_mla': 'new_v', 'new_v_w_out': 'new_v', 'new_v_ffn2_norm': 'new_v', 'new_v_ffn2_w1': 'new_v', 'new_v_ffn2_w3': 'new_v', 'new_v_ffn2_w2': 'new_v'}


def _forward(args):
    return _fwd_reference(*[args[k] for k in FWD_PARAMS])


def _output_shape():
    def fwd():
        inp = _fwd_setup_inputs(0)
        return _fwd_reference(*[inp[k] for k in FWD_PARAMS])
    out = _jax.eval_shape(fwd)
    return out.shape, out.dtype

N_MICROBATCH = 1
ADAM_LR = 0.001
ADAM_B1 = 0.9
ADAM_B2 = 0.999
ADAM_EPS = 1e-08
ADAM_WD = 0.01
ADAM_STEP = 10
PER_EXAMPLE_BATCH_AXIS = {'x': 0, 'loss_target': 0}
SHARED_INPUTS = []
_WEIGHT_DTYPES = {'ffn1_norm': _jnp.float32, 'ffn1_w1': _jnp.float32, 'ffn1_w3': _jnp.float32, 'ffn1_w2': _jnp.float32, 'mix_norm': _jnp.float32, 'w_in': _jnp.float32, 'na_q_norm': _jnp.float32, 'na_k_norm': _jnp.float32, 'na_rpb': _jnp.float32, 'gla_gf_up': _jnp.float32, 'gla_gf_bias': _jnp.float32, 'gla_gb_up': _jnp.float32, 'gla_gb_bias': _jnp.float32, 'gla_out_norm': _jnp.float32, 'mla_cq_norm': _jnp.float32, 'mla_ckv_norm': _jnp.float32, 'mla_w_uq': _jnp.float32, 'mla_w_ukv': _jnp.float32, 'mla_q_norm': _jnp.float32, 'mla_k_norm': _jnp.float32, 'w_br_na': _jnp.float32, 'w_br_gla': _jnp.float32, 'w_br_mla': _jnp.float32, 'w_out': _jnp.float32, 'ffn2_norm': _jnp.float32, 'ffn2_w1': _jnp.float32, 'ffn2_w3': _jnp.float32, 'ffn2_w2': _jnp.float32}
MOMENT_SCALE = {'ffn1_norm': 6.200143e+00, 'ffn1_w1': 1.209667e-01, 'ffn1_w3': 1.255217e-01, 'ffn1_w2': 2.066763e-01, 'mix_norm': 4.213263e+00, 'w_in': 1.624779e-01, 'na_q_norm': 1.436578e+00, 'na_k_norm': 1.426041e+00, 'na_rpb': 2.841670e-02, 'gla_gf_up': 3.742050e-02, 'gla_gf_bias': 1.848276e-01, 'gla_gb_up': 3.840525e-02, 'gla_gb_bias': 1.845145e-01, 'gla_out_norm': 2.701047e+01, 'mla_cq_norm': 6.932987e-02, 'mla_ckv_norm': 2.274555e-01, 'mla_w_uq': 4.081521e-02, 'mla_w_ukv': 6.097135e-02, 'mla_q_norm': 1.776025e-01, 'mla_k_norm': 1.759913e-01, 'w_br_na': 7.196196e-02, 'w_br_gla': 2.244229e-01, 'w_br_mla': 5.045219e-02, 'w_out': 2.149724e-01, 'ffn2_norm': 6.165667e+00, 'ffn2_w1': 9.693867e-02, 'ffn2_w3': 1.096560e-01, 'ffn2_w2': 1.802314e-01}


def _to_microbatches(a, axis):
    t = _jnp.moveaxis(a, axis, 0)
    t = t.reshape((N_MICROBATCH, t.shape[0] // N_MICROBATCH) + t.shape[1:])
    return _jnp.moveaxis(t, 1, axis + 1)


def setup_inputs(seed: int = 0) -> dict:
    inp = _fwd_setup_inputs(seed)
    key = _jax.random.fold_in(_jax.random.key(seed), 7919)
    shape, _ = _output_shape()
    out = dict(inp)
    out["loss_target"] = _jax.random.normal(_jax.random.fold_in(key, 0), shape, _jnp.float32)
    for i, name in enumerate(TWIN_WEIGHTS):
        w = inp[name].astype(_jnp.float32)
        if MOMENT_SCALE is None:
            s = _jnp.sqrt(_jnp.mean(_jnp.square(w)) + 1e-30)
        else:
            s = MOMENT_SCALE[name]
        km, kv = _jax.random.split(_jax.random.fold_in(key, i + 1))
        out[name] = w
        out["m_" + name] = s * _jax.random.normal(km, w.shape, _jnp.float32)
        out["v_" + name] = (s * s) * _jax.random.uniform(kv, w.shape, _jnp.float32, 0.5, 1.5)
    if N_MICROBATCH > 1:
        for name, axis in PER_EXAMPLE_BATCH_AXIS.items():
            out[name] = _to_microbatches(out[name], axis)
    return {'x': out['x'], 'ffn1_norm': out['ffn1_norm'], 'ffn1_w1': out['ffn1_w1'], 'ffn1_w3': out['ffn1_w3'], 'ffn1_w2': out['ffn1_w2'], 'mix_norm': out['mix_norm'], 'w_in': out['w_in'], 'na_q_norm': out['na_q_norm'], 'na_k_norm': out['na_k_norm'], 'na_rpb': out['na_rpb'], 'gla_gf_up': out['gla_gf_up'], 'gla_gf_bias': out['gla_gf_bias'], 'gla_gb_up': out['gla_gb_up'], 'gla_gb_bias': out['gla_gb_bias'], 'gla_out_norm': out['gla_out_norm'], 'mla_cq_norm': out['mla_cq_norm'], 'mla_ckv_norm': out['mla_ckv_norm'], 'mla_w_uq': out['mla_w_uq'], 'mla_w_ukv': out['mla_w_ukv'], 'mla_q_norm': out['mla_q_norm'], 'mla_k_norm': out['mla_k_norm'], 'w_br_na': out['w_br_na'], 'w_br_gla': out['w_br_gla'], 'w_br_mla': out['w_br_mla'], 'w_out': out['w_out'], 'ffn2_norm': out['ffn2_norm'], 'ffn2_w1': out['ffn2_w1'], 'ffn2_w3': out['ffn2_w3'], 'ffn2_w2': out['ffn2_w2'], 'loss_target': out['loss_target'], 'm_ffn1_norm': out['m_ffn1_norm'], 'm_ffn1_w1': out['m_ffn1_w1'], 'm_ffn1_w3': out['m_ffn1_w3'], 'm_ffn1_w2': out['m_ffn1_w2'], 'm_mix_norm': out['m_mix_norm'], 'm_w_in': out['m_w_in'], 'm_na_q_norm': out['m_na_q_norm'], 'm_na_k_norm': out['m_na_k_norm'], 'm_na_rpb': out['m_na_rpb'], 'm_gla_gf_up': out['m_gla_gf_up'], 'm_gla_gf_bias': out['m_gla_gf_bias'], 'm_gla_gb_up': out['m_gla_gb_up'], 'm_gla_gb_bias': out['m_gla_gb_bias'], 'm_gla_out_norm': out['m_gla_out_norm'], 'm_mla_cq_norm': out['m_mla_cq_norm'], 'm_mla_ckv_norm': out['m_mla_ckv_norm'], 'm_mla_w_uq': out['m_mla_w_uq'], 'm_mla_w_ukv': out['m_mla_w_ukv'], 'm_mla_q_norm': out['m_mla_q_norm'], 'm_mla_k_norm': out['m_mla_k_norm'], 'm_w_br_na': out['m_w_br_na'], 'm_w_br_gla': out['m_w_br_gla'], 'm_w_br_mla': out['m_w_br_mla'], 'm_w_out': out['m_w_out'], 'm_ffn2_norm': out['m_ffn2_norm'], 'm_ffn2_w1': out['m_ffn2_w1'], 'm_ffn2_w3': out['m_ffn2_w3'], 'm_ffn2_w2': out['m_ffn2_w2'], 'v_ffn1_norm': out['v_ffn1_norm'], 'v_ffn1_w1': out['v_ffn1_w1'], 'v_ffn1_w3': out['v_ffn1_w3'], 'v_ffn1_w2': out['v_ffn1_w2'], 'v_mix_norm': out['v_mix_norm'], 'v_w_in': out['v_w_in'], 'v_na_q_norm': out['v_na_q_norm'], 'v_na_k_norm': out['v_na_k_norm'], 'v_na_rpb': out['v_na_rpb'], 'v_gla_gf_up': out['v_gla_gf_up'], 'v_gla_gf_bias': out['v_gla_gf_bias'], 'v_gla_gb_up': out['v_gla_gb_up'], 'v_gla_gb_bias': out['v_gla_gb_bias'], 'v_gla_out_norm': out['v_gla_out_norm'], 'v_mla_cq_norm': out['v_mla_cq_norm'], 'v_mla_ckv_norm': out['v_mla_ckv_norm'], 'v_mla_w_uq': out['v_mla_w_uq'], 'v_mla_w_ukv': out['v_mla_w_ukv'], 'v_mla_q_norm': out['v_mla_q_norm'], 'v_mla_k_norm': out['v_mla_k_norm'], 'v_w_br_na': out['v_w_br_na'], 'v_w_br_gla': out['v_w_br_gla'], 'v_w_br_mla': out['v_w_br_mla'], 'v_w_out': out['v_w_out'], 'v_ffn2_norm': out['v_ffn2_norm'], 'v_ffn2_w1': out['v_ffn2_w1'], 'v_ffn2_w3': out['v_ffn2_w3'], 'v_ffn2_w2': out['v_ffn2_w2']}


def _loss(weights, diff, rest, loss_target):
    with _jax.named_scope("forward"):
        args = {**rest, TWIN_DIFF_INPUT: diff, **{k: w.astype(_WEIGHT_DTYPES[k]) for k, w in weights.items()}}
        y = _forward(args)
    with _jax.named_scope("loss_head"):
        err = _jnp.square(y.astype(_jnp.float32) - loss_target)
        return 0.5 * _jnp.sum(_jnp.mean(err, axis=-1)) if err.ndim else 0.5 * err


def _adamw(w, g, m, v):
    m = ADAM_B1 * m + (1.0 - ADAM_B1) * g
    v = ADAM_B2 * v + (1.0 - ADAM_B2) * _jnp.square(g)
    m_hat = m / (1.0 - ADAM_B1 ** ADAM_STEP)
    v_hat = v / (1.0 - ADAM_B2 ** ADAM_STEP)
    delta = -ADAM_LR * (m_hat / (_jnp.sqrt(v_hat) + ADAM_EPS) + ADAM_WD * w)
    return delta, m, v


def reference(x, ffn1_norm, ffn1_w1, ffn1_w3, ffn1_w2, mix_norm, w_in, na_q_norm, na_k_norm, na_rpb, gla_gf_up, gla_gf_bias, gla_gb_up, gla_gb_bias, gla_out_norm, mla_cq_norm, mla_ckv_norm, mla_w_uq, mla_w_ukv, mla_q_norm, mla_k_norm, w_br_na, w_br_gla, w_br_mla, w_out, ffn2_norm, ffn2_w1, ffn2_w3, ffn2_w2, loss_target, m_ffn1_norm, m_ffn1_w1, m_ffn1_w3, m_ffn1_w2, m_mix_norm, m_w_in, m_na_q_norm, m_na_k_norm, m_na_rpb, m_gla_gf_up, m_gla_gf_bias, m_gla_gb_up, m_gla_gb_bias, m_gla_out_norm, m_mla_cq_norm, m_mla_ckv_norm, m_mla_w_uq, m_mla_w_ukv, m_mla_q_norm, m_mla_k_norm, m_w_br_na, m_w_br_gla, m_w_br_mla, m_w_out, m_ffn2_norm, m_ffn2_w1, m_ffn2_w3, m_ffn2_w2, v_ffn1_norm, v_ffn1_w1, v_ffn1_w3, v_ffn1_w2, v_mix_norm, v_w_in, v_na_q_norm, v_na_k_norm, v_na_rpb, v_gla_gf_up, v_gla_gf_bias, v_gla_gb_up, v_gla_gb_bias, v_gla_out_norm, v_mla_cq_norm, v_mla_ckv_norm, v_mla_w_uq, v_mla_w_ukv, v_mla_q_norm, v_mla_k_norm, v_w_br_na, v_w_br_gla, v_w_br_mla, v_w_out, v_ffn2_norm, v_ffn2_w1, v_ffn2_w3, v_ffn2_w2):
    given = dict(x=x, ffn1_norm=ffn1_norm, ffn1_w1=ffn1_w1, ffn1_w3=ffn1_w3, ffn1_w2=ffn1_w2, mix_norm=mix_norm, w_in=w_in, na_q_norm=na_q_norm, na_k_norm=na_k_norm, na_rpb=na_rpb, gla_gf_up=gla_gf_up, gla_gf_bias=gla_gf_bias, gla_gb_up=gla_gb_up, gla_gb_bias=gla_gb_bias, gla_out_norm=gla_out_norm, mla_cq_norm=mla_cq_norm, mla_ckv_norm=mla_ckv_norm, mla_w_uq=mla_w_uq, mla_w_ukv=mla_w_ukv, mla_q_norm=mla_q_norm, mla_k_norm=mla_k_norm, w_br_na=w_br_na, w_br_gla=w_br_gla, w_br_mla=w_br_mla, w_out=w_out, ffn2_norm=ffn2_norm, ffn2_w1=ffn2_w1, ffn2_w3=ffn2_w3, ffn2_w2=ffn2_w2, loss_target=loss_target, m_ffn1_norm=m_ffn1_norm, m_ffn1_w1=m_ffn1_w1, m_ffn1_w3=m_ffn1_w3, m_ffn1_w2=m_ffn1_w2, m_mix_norm=m_mix_norm, m_w_in=m_w_in, m_na_q_norm=m_na_q_norm, m_na_k_norm=m_na_k_norm, m_na_rpb=m_na_rpb, m_gla_gf_up=m_gla_gf_up, m_gla_gf_bias=m_gla_gf_bias, m_gla_gb_up=m_gla_gb_up, m_gla_gb_bias=m_gla_gb_bias, m_gla_out_norm=m_gla_out_norm, m_mla_cq_norm=m_mla_cq_norm, m_mla_ckv_norm=m_mla_ckv_norm, m_mla_w_uq=m_mla_w_uq, m_mla_w_ukv=m_mla_w_ukv, m_mla_q_norm=m_mla_q_norm, m_mla_k_norm=m_mla_k_norm, m_w_br_na=m_w_br_na, m_w_br_gla=m_w_br_gla, m_w_br_mla=m_w_br_mla, m_w_out=m_w_out, m_ffn2_norm=m_ffn2_norm, m_ffn2_w1=m_ffn2_w1, m_ffn2_w3=m_ffn2_w3, m_ffn2_w2=m_ffn2_w2, v_ffn1_norm=v_ffn1_norm, v_ffn1_w1=v_ffn1_w1, v_ffn1_w3=v_ffn1_w3, v_ffn1_w2=v_ffn1_w2, v_mix_norm=v_mix_norm, v_w_in=v_w_in, v_na_q_norm=v_na_q_norm, v_na_k_norm=v_na_k_norm, v_na_rpb=v_na_rpb, v_gla_gf_up=v_gla_gf_up, v_gla_gf_bias=v_gla_gf_bias, v_gla_gb_up=v_gla_gb_up, v_gla_gb_bias=v_gla_gb_bias, v_gla_out_norm=v_gla_out_norm, v_mla_cq_norm=v_mla_cq_norm, v_mla_ckv_norm=v_mla_ckv_norm, v_mla_w_uq=v_mla_w_uq, v_mla_w_ukv=v_mla_w_ukv, v_mla_q_norm=v_mla_q_norm, v_mla_k_norm=v_mla_k_norm, v_w_br_na=v_w_br_na, v_w_br_gla=v_w_br_gla, v_w_br_mla=v_w_br_mla, v_w_out=v_w_out, v_ffn2_norm=v_ffn2_norm, v_ffn2_w1=v_ffn2_w1, v_ffn2_w3=v_ffn2_w3, v_ffn2_w2=v_ffn2_w2)
    weights = {n: given[n] for n in TWIN_WEIGHTS}
    shared = {n: given[n] for n in SHARED_INPUTS}
    per_example = {n: given[n] for n in ['x']}
    grad_fn = _jax.value_and_grad(_loss, argnums=(0, 1))

    def one_microbatch(ex, loss_target):
        ex = dict(ex)
        diff = ex.pop(TWIN_DIFF_INPUT)
        return grad_fn(weights, diff, {**shared, **ex}, loss_target)

    if N_MICROBATCH == 1:
        loss, (grad_w, grad_x) = one_microbatch(per_example, given["loss_target"])
    else:
        def body(carry, xs):
            loss_sum, grad_sum = carry
            l_k, (gw_k, gx_k) = one_microbatch(xs[0], xs[1])
            with _jax.named_scope("update"):
                return (loss_sum + l_k, _jax.tree.map(_jnp.add, grad_sum, gw_k)), gx_k

        init = (_jnp.zeros((), _jnp.float32), _jax.tree.map(_jnp.zeros_like, weights))
        (loss, grad_w), grad_x = _jax.lax.scan(body, init, (per_example, given["loss_target"]))
    with _jax.named_scope("update"):
        delta_w, new_m, new_v = {}, {}, {}
        for n in TWIN_WEIGHTS:
            delta_w[n], new_m[n], new_v[n] = _adamw(weights[n], grad_w[n], given["m_" + n], given["v_" + n])
    return (loss, grad_x, *[grad_w[n] for n in TWIN_WEIGHTS], *[delta_w[n] for n in TWIN_WEIGHTS],
            *[new_m[n] for n in TWIN_WEIGHTS], *[new_v[n] for n in TWIN_WEIGHTS])
```

```python
import functools

import numpy as np
import jax
import jax.numpy as jnp
from jax import lax
from jax.experimental import pallas as pl
from jax.experimental.pallas import tpu as pltpu

F32 = jnp.float32
BF16 = jnp.bfloat16
SDS = jax.ShapeDtypeStruct
HI = lax.Precision.HIGHEST

N_DEV = 8
DEPTH = 4
EPS = 1e-6
LANE = 128
VMEM_LIMIT = 56 * 1024 * 1024

GRID_W = 64
NA_HEADS, NA_HD, NA_WIN_R, NA_WIN_C = 8, 64, 8, 16
GLA_HEADS, GLA_DK, GLA_DV, GLA_RANK, GLA_TAU, GLA_CHUNK = 4, 64, 128, 16, 16.0, 64
MLA_HEADS, MLA_QR, MLA_KVR, MLA_NOPE, MLA_ROPE, MLA_V = 4, 256, 256, 128, 64, 128
MLA_QK = MLA_NOPE + MLA_ROPE
ROPE_THETA = 10000.0
NA_W = NA_HEADS * NA_HD
GLA_QK_W = GLA_HEADS * GLA_DK
GLA_V_W = GLA_HEADS * GLA_DV
MLA_V_W = MLA_HEADS * MLA_V

ADAM_LR, ADAM_B1, ADAM_B2, ADAM_EPS, ADAM_WD, ADAM_STEP = 0.001, 0.9, 0.999, 1e-08, 0.01, 10

WEIGHTS = ['ffn1_norm', 'ffn1_w1', 'ffn1_w3', 'ffn1_w2', 'mix_norm', 'w_in', 'na_q_norm', 'na_k_norm', 'na_rpb',
           'gla_gf_up', 'gla_gf_bias', 'gla_gb_up', 'gla_gb_bias', 'gla_out_norm', 'mla_cq_norm', 'mla_ckv_norm',
           'mla_w_uq', 'mla_w_ukv', 'mla_q_norm', 'mla_k_norm', 'w_br_na', 'w_br_gla', 'w_br_mla', 'w_out',
           'ffn2_norm', 'ffn2_w1', 'ffn2_w3', 'ffn2_w2']
COL_SHARDED = ['ffn1_w1', 'ffn1_w3', 'w_in', 'gla_gf_up', 'gla_gb_up', 'mla_w_uq', 'mla_w_ukv', 'w_br_na', 'w_br_gla',
               'w_br_mla', 'ffn2_w1', 'ffn2_w3']
ROW_SHARDED = ['ffn1_w2', 'w_out', 'ffn2_w2']
SHARDED = [n for n in WEIGHTS if n in COL_SHARDED or n in ROW_SHARDED]
REPLICATED = [n for n in WEIGHTS if n not in SHARDED]

NT = (((1,), (1,)), ((), ()))
TN = (((0,), (0,)), ((), ()))


def _tile(n, pref, mult=8):
    best = None
    for t in range(mult, min(n, pref) + 1, mult):
        if n % t == 0:
            best = t
    return best if best is not None else n


def _params(n_axes):
    return pltpu.CompilerParams(dimension_semantics=("arbitrary",) * n_axes, vmem_limit_bytes=VMEM_LIMIT)


def _dot(a, b, dims=None, precision=None):
    if dims is None:
        return jnp.dot(a, b, preferred_element_type=F32, precision=precision)
    return lax.dot_general(a, b, dims, preferred_element_type=F32, precision=precision)


def _sigmoid(x):
    return 1.0 / (1.0 + jnp.exp(-x))


def _rstd(x):
    return lax.rsqrt(jnp.mean(x * x, axis=-1, keepdims=True) + EPS)


def _rms_bwd(dy, x, g, r):
    xh = x * r
    dyg = dy * g
    dx = r * (dyg - xh * jnp.mean(dyg * xh, axis=-1, keepdims=True))
    return dx, jnp.sum(dy * xh, axis=0, keepdims=True)


def _z_layout(d_model):
    own = {}
    off = 0
    for name, w in [('na_q', NA_W), ('na_k', NA_W), ('na_v', NA_W), ('gq', GLA_QK_W), ('gk', GLA_QK_W),
                    ('gv', GLA_V_W), ('gr', GLA_V_W), ('gfl', GLA_RANK), ('gbl', GLA_RANK), ('c_q', MLA_QR),
                    ('c_kv', MLA_KVR), ('k_rope', MLA_ROPE), ('gates', 3 * d_model)]:
        own[name] = (off, w)
        off += w
    order = [('gates', 3 * d_model), ('na_q', NA_W), ('na_k', NA_W), ('na_v', NA_W), ('gv', GLA_V_W), ('gr', GLA_V_W),
             ('gq', GLA_QK_W), ('gk', GLA_QK_W), ('c_q', MLA_QR), ('c_kv', MLA_KVR), ('glow', LANE), ('k_rope', LANE)]
    lay = {}
    z = 0
    for name, w in order:
        assert z % w == 0, (name, z, w)
        if name == 'glow':
            lay[name] = (z, w, own['gfl'][0], 2 * GLA_RANK)
        else:
            lay[name] = (z, w, own[name][0], own[name][1])
        z += w
    return lay, z, off


def _to_z_cols(w, lay, zp):
    parts = []
    for name, (zo, zw, oo, ow) in lay.items():
        parts.append(w[:, oo:oo + ow])
        if zw > ow:
            parts.append(jnp.zeros((w.shape[0], zw - ow), w.dtype))
    return jnp.concatenate(parts, axis=1)


def _from_z_cols(wz, lay, d_in):
    own = sorted(lay.values(), key=lambda t: t[2])
    return jnp.concatenate([wz[:, zo:zo + ow] for (zo, zw, oo, ow) in own], axis=1)


def _ffn_fwd(x, g, w1, w3, w2, name):
    T, D = x.shape
    Fd = w1.shape[1]
    tm, tf = _tile(T, 1024), _tile(Fd, 256, LANE)
    nj = Fd // tf

    def body(x_ref, g_ref, w1_ref, w3_ref, w2_ref, o_ref, a_ref, b_ref, h_scr, acc):
        j = pl.program_id(1)

        @pl.when(j == 0)
        def _():
            xv = x_ref[...]
            h_scr[...] = (xv * _rstd(xv) * g_ref[...]).astype(BF16)
            acc[...] = jnp.zeros_like(acc)

        h = h_scr[...]
        a = _dot(h, w1_ref[...])
        b = _dot(h, w3_ref[...])
        a_ref[...] = a.astype(BF16)
        b_ref[...] = b.astype(BF16)
        u = a * _sigmoid(a) * b
        acc[...] += _dot(u.astype(BF16), w2_ref[...])

        @pl.when(j == nj - 1)
        def _():
            o_ref[...] = x_ref[...] + 0.5 * acc[...]

    return pl.pallas_call(
        body, name=name, grid=(T // tm, nj),
        in_specs=[pl.BlockSpec((tm, D), lambda i, j: (i, 0)), pl.BlockSpec((1, D), lambda i, j: (0, 0)),
                  pl.BlockSpec((D, tf), lambda i, j: (0, j)), pl.BlockSpec((D, tf), lambda i, j: (0, j)),
                  pl.BlockSpec((tf, D), lambda i, j: (j, 0))],
        out_specs=[pl.BlockSpec((tm, D), lambda i, j: (i, 0)), pl.BlockSpec((tm, tf), lambda i, j: (i, j)),
                   pl.BlockSpec((tm, tf), lambda i, j: (i, j))],
        out_shape=[SDS((T, D), F32), SDS((T, Fd), BF16), SDS((T, Fd), BF16)],
        scratch_shapes=[pltpu.VMEM((tm, D), BF16), pltpu.VMEM((tm, D), F32)],
        compiler_params=_params(2),
    )(x, g, w1, w3, w2)


def _ffn_bwd(x, g, dy, a, b, w1, w3, w2, name):
    T, D = x.shape
    Fd = w1.shape[1]
    tm, tf = _tile(T, 512), _tile(Fd, 256, LANE)
    nj = Fd // tf

    def body(x_ref, g_ref, dy_ref, a_ref, b_ref, w1_ref, w3_ref, w2_ref,
             dx_ref, da_ref, db_ref, u_ref, hb_ref, dyb_ref, dg_ref, dh_acc):
        i, j = pl.program_id(0), pl.program_id(1)

        @pl.when(j == 0)
        def _():
            xv = x_ref[...]
            hb_ref[...] = (xv * _rstd(xv) * g_ref[...]).astype(BF16)
            dyb_ref[...] = (0.5 * dy_ref[...]).astype(BF16)
            dh_acc[...] = jnp.zeros_like(dh_acc)

        @pl.when((i == 0) & (j == 0))
        def _():
            dg_ref[...] = jnp.zeros_like(dg_ref)

        du = _dot(dyb_ref[...], w2_ref[...], NT)
        av = a_ref[...].astype(F32)
        bv = b_ref[...].astype(F32)
        s = _sigmoid(av)
        sl = av * s
        da = (du * bv * (s * (1.0 + av * (1.0 - s)))).astype(BF16)
        db = (du * sl).astype(BF16)
        da_ref[...] = da
        db_ref[...] = db
        u_ref[...] = (sl * bv).astype(BF16)
        dh_acc[...] += _dot(da, w1_ref[...], NT) + _dot(db, w3_ref[...], NT)

        @pl.when(j == nj - 1)
        def _():
            xv = x_ref[...]
            dxn, dg = _rms_bwd(dh_acc[...], xv, g_ref[...], _rstd(xv))
            dx_ref[...] = dy_ref[...] + dxn
            dg_ref[...] += dg

    row = lambda i, j: (i, 0)
    return pl.pallas_call(
        body, name=name, grid=(T // tm, nj),
        in_specs=[pl.BlockSpec((tm, D), row), pl.BlockSpec((1, D), lambda i, j: (0, 0)), pl.BlockSpec((tm, D), row),
                  pl.BlockSpec((tm, tf), lambda i, j: (i, j)), pl.BlockSpec((tm, tf), lambda i, j: (i, j)),
                  pl.BlockSpec((D, tf), lambda i, j: (0, j)), pl.BlockSpec((D, tf), lambda i, j: (0, j)),
                  pl.BlockSpec((tf, D), lambda i, j: (j, 0))],
        out_specs=[pl.BlockSpec((tm, D), row), pl.BlockSpec((tm, tf), lambda i, j: (i, j)),
                   pl.BlockSpec((tm, tf), lambda i, j: (i, j)), pl.BlockSpec((tm, tf), lambda i, j: (i, j)),
                   pl.BlockSpec((tm, D), row), pl.BlockSpec((tm, D), row), pl.BlockSpec((1, D), lambda i, j: (0, 0))],
        out_shape=[SDS((T, D), F32), SDS((T, Fd), BF16), SDS((T, Fd), BF16), SDS((T, Fd), BF16),
                   SDS((T, D), BF16), SDS((T, D), BF16), SDS((1, D), F32)],
        scratch_shapes=[pltpu.VMEM((tm, D), F32)],
        compiler_params=_params(2),
    )(x, g, dy, a, b, w1, w3, w2)


def _matmul_tn(a, b, name, out_dtype=BF16):
    T, K = a.shape
    N = b.shape[1]
    tk = _tile(K, 1408, LANE)
    tn = _tile(N, 1408, LANE)
    tt = _tile(T, 1024)
    nt = T // tt

    def body(a_ref, b_ref, o_ref, acc):
        t = pl.program_id(2)

        @pl.when(t == 0)
        def _():
            acc[...] = jnp.zeros_like(acc)

        acc[...] += _dot(a_ref[...], b_ref[...], TN)

        @pl.when(t == nt - 1)
        def _():
            o_ref[...] = acc[...].astype(out_dtype)

    return pl.pallas_call(
        body, name=name, grid=(K // tk, N // tn, nt),
        in_specs=[pl.BlockSpec((tt, tk), lambda k, n, t: (t, k)), pl.BlockSpec((tt, tn), lambda k, n, t: (t, n))],
        out_specs=pl.BlockSpec((tk, tn), lambda k, n, t: (k, n)),
        out_shape=SDS((K, N), out_dtype),
        scratch_shapes=[pltpu.VMEM((tk, tn), F32)],
        compiler_params=_params(3),
    )(a, b)


def _norm_matmul(x, xcol, g, w, name):
    T = x.shape[0]
    D = g.shape[1]
    N = w.shape[1]
    tm, tn = _tile(T, 1024), _tile(N, 768, LANE)

    def body(x_ref, g_ref, w_ref, z_ref, hb_ref):
        @pl.when(pl.program_id(1) == 0)
        def _():
            xv = x_ref[...]
            hb_ref[...] = (xv * _rstd(xv) * g_ref[...]).astype(BF16)

        z_ref[...] = _dot(hb_ref[...], w_ref[...])

    return pl.pallas_call(
        body, name=name, grid=(T // tm, N // tn),
        in_specs=[pl.BlockSpec((tm, D), lambda i, j: (i, xcol)), pl.BlockSpec((1, D), lambda i, j: (0, 0)),
                  pl.BlockSpec((D, tn), lambda i, j: (0, j))],
        out_specs=[pl.BlockSpec((tm, tn), lambda i, j: (i, j)), pl.BlockSpec((tm, D), lambda i, j: (i, 0))],
        out_shape=[SDS((T, N), F32), SDS((T, D), BF16)],
        compiler_params=_params(2),
    )(x, g, w)


def _norm_matmul_bwd(dz, w, x, xcol, g, resid, name):
    T, N = dz.shape
    D = g.shape[1]
    tm, tn = _tile(T, 1024), _tile(N, 768, LANE)
    nj = N // tn
    has_resid = resid is not None

    def body(*refs):
        if has_resid:
            dz_ref, w_ref, x_ref, g_ref, r_ref, dx_ref, dg_ref, acc = refs
        else:
            dz_ref, w_ref, x_ref, g_ref, dx_ref, dg_ref, acc = refs
        i, j = pl.program_id(0), pl.program_id(1)

        @pl.when(j == 0)
        def _():
            acc[...] = jnp.zeros_like(acc)

        @pl.when((i == 0) & (j == 0))
        def _():
            dg_ref[...] = jnp.zeros_like(dg_ref)

        acc[...] += _dot(dz_ref[...], w_ref[...], NT)

        @pl.when(j == nj - 1)
        def _():
            xv = x_ref[...]
            dxn, dg = _rms_bwd(acc[...], xv, g_ref[...], _rstd(xv))
            dx_ref[...] = dxn + r_ref[...] if has_resid else dxn
            dg_ref[...] += dg

    in_specs = [pl.BlockSpec((tm, tn), lambda i, j: (i, j)), pl.BlockSpec((D, tn), lambda i, j: (0, j)),
                pl.BlockSpec((tm, D), lambda i, j: (i, xcol)), pl.BlockSpec((1, D), lambda i, j: (0, 0))]
    args = [dz, w, x, g]
    if has_resid:
        in_specs.append(pl.BlockSpec((tm, D), lambda i, j: (i, 0)))
        args.append(resid)
    return pl.pallas_call(
        body, name=name, grid=(T // tm, nj), in_specs=in_specs,
        out_specs=[pl.BlockSpec((tm, D), lambda i, j: (i, 0)), pl.BlockSpec((1, D), lambda i, j: (0, 0))],
        out_shape=[SDS((T, D), F32), SDS((1, D), F32)],
        scratch_shapes=[pltpu.VMEM((tm, D), F32)],
        compiler_params=_params(2),
    )(*args)


def _na_bias_tables(rpb):
    L = rpb.shape[0]
    qc = np.arange(GRID_W)[:, None]
    kc = np.arange(GRID_W)[None, :]
    dc = np.clip(kc - qc + NA_WIN_C - 1, 0, 2 * NA_WIN_C - 2)
    c0 = np.clip(qc - NA_WIN_C // 2, 0, GRID_W - NA_WIN_C)
    ok = (kc >= c0) & (kc < c0 + NA_WIN_C)
    onehot = (dc.reshape(-1)[None, :] == np.arange(2 * NA_WIN_C - 1)[:, None]).astype(np.float32)
    t1 = jnp.stack([rpb[:, :, c:c + NA_WIN_R, :] for c in range(NA_WIN_R)], axis=2)
    full = jnp.dot(t1.reshape(-1, 2 * NA_WIN_C - 1), jnp.asarray(onehot), precision=HI)
    full = full.reshape(L, NA_HEADS, NA_WIN_R, NA_WIN_R, GRID_W, GRID_W)
    full = jnp.where(jnp.asarray(ok)[None, None, None, None], full, -1e30)
    return jnp.transpose(full, (0, 1, 2, 4, 3, 5)).reshape(L, NA_HEADS, NA_WIN_R, GRID_W, NA_WIN_R * GRID_W), onehot


def _na_bias_tables_bwd(db, onehot):
    d = db.reshape(NA_HEADS, NA_WIN_R, GRID_W, NA_WIN_R, GRID_W)
    d = jnp.transpose(d, (0, 1, 3, 2, 4)).reshape(-1, GRID_W * GRID_W)
    t1 = jnp.dot(d, jnp.asarray(onehot.T), precision=HI).reshape(NA_HEADS, NA_WIN_R, NA_WIN_R, 2 * NA_WIN_C - 1)
    out = jnp.zeros((NA_HEADS, 2 * NA_WIN_R - 1, 2 * NA_WIN_C - 1), F32)
    for c in range(NA_WIN_R):
        out = out.at[:, c:c + NA_WIN_R, :].add(t1[:, c])
    return out


def _na_fwd(q, k, v, bias, gq, gk, name):
    H, T, d = q.shape
    rows = T // GRID_W
    win = NA_WIN_R * GRID_W
    scale = d ** -0.5

    def body(q_ref, k_ref, v_ref, b_ref, gq_ref, gk_ref, o_ref, qn, kn, vb):
        qv, kv = q_ref[0], k_ref[0]
        qn[...] = (qv * _rstd(qv) * gq_ref[...] * scale).astype(BF16)
        kn[...] = (kv * _rstd(kv) * gk_ref[...]).astype(BF16)
        vb[...] = v_ref[0].astype(BF16)

        def row(r, c):
            r0 = jnp.clip(r - NA_WIN_R // 2, 0, rows - NA_WIN_R)
            qsl = pl.ds(pl.multiple_of(r * GRID_W, GRID_W), GRID_W)
            wsl = pl.ds(pl.multiple_of(r0 * GRID_W, GRID_W), win)
            s = _dot(qn[qsl, :], kn[wsl, :], NT) + b_ref[0, r0 - r + NA_WIN_R - 1]
            p = jnp.exp(s - jnp.max(s, axis=-1, keepdims=True))
            p = p / jnp.sum(p, axis=-1, keepdims=True)
            o_ref[0, qsl, :] = _dot(p.astype(BF16), vb[wsl, :])
            return c

        lax.fori_loop(0, rows, row, 0)

    hm = pl.BlockSpec((1, T, d), lambda h: (h, 0, 0))
    gs = pl.BlockSpec((1, d), lambda h: (0, 0))
    return pl.pallas_call(
        body, name=name, grid=(H,),
        in_specs=[hm, hm, hm, pl.BlockSpec((1, NA_WIN_R, GRID_W, win), lambda h: (h, 0, 0, 0)), gs, gs],
        out_specs=hm, out_shape=SDS((H, T, d), F32),
        scratch_shapes=[pltpu.VMEM((T, d), BF16)] * 3,
        compiler_params=_params(1),
    )(q, k, v, bias, gq, gk)


def _na_bwd(q, k, v, do, bias, gq, gk, name):
    H, T, d = q.shape
    rows = T // GRID_W
    win = NA_WIN_R * GRID_W
    scale = d ** -0.5

    def body(q_ref, k_ref, v_ref, do_ref, b_ref, gq_ref, gk_ref,
             dq_ref, dk_ref, dv_ref, db_ref, dgq_ref, dgk_ref, qn, kn, vb, dob, dqn, dkn):
        h = pl.program_id(0)
        qv, kv = q_ref[0], k_ref[0]
        rq, rk = _rstd(qv), _rstd(kv)
        qn[...] = (qv * rq * gq_ref[...] * scale).astype(BF16)
        kn[...] = (kv * rk * gk_ref[...]).astype(BF16)
        vb[...] = v_ref[0].astype(BF16)
        dob[...] = do_ref[0].astype(BF16)
        dkn[...] = jnp.zeros_like(dkn)
        dv_ref[...] = jnp.zeros_like(dv_ref)
        db_ref[...] = jnp.zeros_like(db_ref)

        @pl.when(h == 0)
        def _():
            dgq_ref[...] = jnp.zeros_like(dgq_ref)
            dgk_ref[...] = jnp.zeros_like(dgk_ref)

        def row(r, c):
            r0 = jnp.clip(r - NA_WIN_R // 2, 0, rows - NA_WIN_R)
            cls = r0 - r + NA_WIN_R - 1
            qsl = pl.ds(pl.multiple_of(r * GRID_W, GRID_W), GRID_W)
            wsl = pl.ds(pl.multiple_of(r0 * GRID_W, GRID_W), win)
            qs, kw, vw, dos = qn[qsl, :], kn[wsl, :], vb[wsl, :], dob[qsl, :]
            s = _dot(qs, kw, NT) + b_ref[0, cls]
            p = jnp.exp(s - jnp.max(s, axis=-1, keepdims=True))
            p = p / jnp.sum(p, axis=-1, keepdims=True)
            dp = _dot(dos, vw, NT)
            ds = p * (dp - jnp.sum(dp * p, axis=-1, keepdims=True))
            db_ref[0, cls] += ds
            dsb = ds.astype(BF16)
            dv_ref[0, wsl, :] += _dot(p.astype(BF16), dos, TN)
            dqn[qsl, :] = _dot(dsb, kw)
            dkn[wsl, :] += _dot(dsb, qs, TN)
            return c

        lax.fori_loop(0, rows, row, 0)
        dq, dgq = _rms_bwd(dqn[...] * scale, qv, gq_ref[...], rq)
        dk, dgk = _rms_bwd(dkn[...], kv, gk_ref[...], rk)
        dq_ref[0] = dq
        dk_ref[0] = dk
        dgq_ref[...] += dgq
        dgk_ref[...] += dgk

    hm = pl.BlockSpec((1, T, d), lambda h: (h, 0, 0))
    gs = pl.BlockSpec((1, d), lambda h: (0, 0))
    bs = pl.BlockSpec((1, NA_WIN_R, GRID_W, win), lambda h: (h, 0, 0, 0))
    return pl.pallas_call(
        body, name=name, grid=(H,),
        in_specs=[hm, hm, hm, hm, bs, gs, gs],
        out_specs=[hm, hm, hm, bs, gs, gs],
        out_shape=[SDS((H, T, d), F32)] * 3 + [SDS((H, NA_WIN_R, GRID_W, win), F32), SDS((1, d), F32), SDS((1, d), F32)],
        scratch_shapes=[pltpu.VMEM((T, d), BF16)] * 4 + [pltpu.VMEM((T, d), F32)] * 2,
        compiler_params=_params(1),
    )(q, k, v, do, bias, gq, gk)


def _gla_gate_fwd(z, col, upf, upb, bf, bb, name):
    T = z.shape[0]
    W = upf.shape[1]
    tm = _tile(T, 1024)

    def body(z_ref, upf_ref, upb_ref, bf_ref, bb_ref, gf_ref, gb_ref):
        seg = z_ref[...].astype(BF16)
        for up_ref, b_ref, o_ref in ((upf_ref, bf_ref, gf_ref), (upb_ref, bb_ref, gb_ref)):
            pre = _dot(seg, up_ref[...]) + b_ref[...]
            o_ref[...] = (jnp.minimum(pre, 0.0) - jnp.log(1.0 + jnp.exp(-jnp.abs(pre)))) * (1.0 / GLA_TAU)

    full = lambda shape: pl.BlockSpec(shape, lambda i: (0, 0))
    return pl.pallas_call(
        body, name=name, grid=(T // tm,),
        in_specs=[pl.BlockSpec((tm, LANE), lambda i: (i, col)), full((LANE, W)), full((LANE, W)), full((1, W)), full((1, W))],
        out_specs=[pl.BlockSpec((tm, W), lambda i: (i, 0))] * 2,
        out_shape=[SDS((T, W), F32)] * 2,
        compiler_params=_params(1),
    )(z, upf, upb, bf, bb)


def _gla_gate_bwd(z, col, upf, upb, bf, bb, dgf, dgb, name):
    T = z.shape[0]
    W = upf.shape[1]
    tm = _tile(T, 1024)

    def body(z_ref, upf_ref, upb_ref, bf_ref, bb_ref, dgf_ref, dgb_ref, dseg_ref, dupf_ref, dupb_ref, dbf_ref, dbb_ref):
        @pl.when(pl.program_id(0) == 0)
        def _():
            for r in (dupf_ref, dupb_ref, dbf_ref, dbb_ref):
                r[...] = jnp.zeros_like(r)

        seg = z_ref[...].astype(BF16)
        dseg = jnp.zeros(dseg_ref.shape, F32)
        for up_ref, b_ref, dg_ref, dup_ref, db_ref in ((upf_ref, bf_ref, dgf_ref, dupf_ref, dbf_ref),
                                                      (upb_ref, bb_ref, dgb_ref, dupb_ref, dbb_ref)):
            pre = _dot(seg, up_ref[...]) + b_ref[...]
            dpre = dg_ref[...] * (1.0 / GLA_TAU) * (1.0 - _sigmoid(pre))
            dpb = dpre.astype(BF16)
            dseg = dseg + _dot(dpb, up_ref[...], NT)
            dup_ref[...] += _dot(seg, dpb, TN)
            db_ref[...] += jnp.sum(dpre, axis=0, keepdims=True)
        dseg_ref[...] = dseg

    full = lambda shape: pl.BlockSpec(shape, lambda i: (0, 0))
    tok = pl.BlockSpec((tm, W), lambda i: (i, 0))
    return pl.pallas_call(
        body, name=name, grid=(T // tm,),
        in_specs=[pl.BlockSpec((tm, LANE), lambda i: (i, col)), full((LANE, W)), full((LANE, W)), full((1, W)), full((1, W)),
                  tok, tok],
        out_specs=[pl.BlockSpec((tm, LANE), lambda i: (i, 0)), full((LANE, W)), full((LANE, W)), full((1, W)), full((1, W))],
        out_shape=[SDS((T, LANE), F32), SDS((LANE, W), F32), SDS((LANE, W), F32), SDS((1, W), F32), SDS((1, W), F32)],
        compiler_params=_params(1),
    )(z, upf, upb, bf, bb, dgf, dgb)


def _bdot(a, b, ca, cb, precision=None):
    return lax.dot_general(a, b, (((ca,), (cb,)), ((0,), (0,))), preferred_element_type=F32, precision=precision)


def _gla_chunk_terms(q_ref, k_ref, v_ref, g_ref, tok, n, C):
    dk, dv = q_ref.shape[-1], v_ref.shape[-1]
    q = q_ref[0, 0, tok, :].reshape(n, C, dk) * (dk ** -0.5)
    k = k_ref[0, 0, tok, :].reshape(n, C, dk)
    v = v_ref[0, 0, tok, :].reshape(n, C, dv)
    g = g_ref[0, 0, tok, :].reshape(n, C, dk)
    ii = lax.broadcasted_iota(jnp.int32, (C, C), 0)
    jj = lax.broadcasted_iota(jnp.int32, (C, C), 1)
    tri = jnp.broadcast_to((ii >= jj).astype(F32), (n, C, C))
    amask = ((ii - jj) >= pl.program_id(0))[None]
    b = _bdot(tri, g, 2, 1, HI)
    bl = jnp.sum(g, axis=1, keepdims=True)
    eb = jnp.exp(b)
    enb = jnp.exp(-b)
    eend = jnp.exp(bl - b)
    return q, k, v, tri, amask, bl, eb, enb, eend


def _gla_fwd(q, k, v, g, name):
    _, H, T, dk = q.shape
    dv = v.shape[-1]
    C = GLA_CHUNK
    n = T // C

    def body(q_ref, k_ref, v_ref, g_ref, o_ref, upd_scr, dec_scr, st_scr):
        q, k, v, tri, amask, bl, eb, enb, eend = _gla_chunk_terms(q_ref, k_ref, v_ref, g_ref, slice(None), n, C)
        qe = (q * eb).astype(BF16)
        ke = (k * enb).astype(BF16)
        kend = (k * eend).astype(BF16)
        vb = v.astype(BF16)
        a = jnp.where(amask, _bdot(qe, ke, 2, 2), 0.0).astype(BF16)
        o = _bdot(a, vb, 2, 1)
        upd_scr[...] = _bdot(vb, kend, 1, 1)
        dec_scr[...] = jnp.exp(bl)

        def step(i, st):
            st_scr[i] = st
            return st * dec_scr[i] + upd_scr[i]

        lax.fori_loop(0, n, step, jnp.zeros((dv, dk), F32))
        o = o + _bdot(qe, st_scr[...].astype(BF16), 2, 2)
        o_ref[0, 0] = o.reshape(T, dv)

    sk = pl.BlockSpec((1, 1, T, dk), lambda d, h: (d, h, 0, 0))
    sv = pl.BlockSpec((1, 1, T, dv), lambda d, h: (d, h, 0, 0))
    return pl.pallas_call(
        body, name=name, grid=(2, H), in_specs=[sk, sk, sv, sk], out_specs=sv,
        out_shape=SDS((2, H, T, dv), F32),
        scratch_shapes=[pltpu.VMEM((n, dv, dk), F32), pltpu.VMEM((n, 1, dk), F32), pltpu.VMEM((n, dv, dk), F32)],
        compiler_params=_params(2),
    )(q, k, v, g)


def _gla_bwd(q, k, v, g, do, name):
    _, H, T, dk = q.shape
    dv = v.shape[-1]
    C = GLA_CHUNK
    n = T // C
    ns = _tile(n, 16, 1)
    nseg = n // ns

    def body(q_ref, k_ref, v_ref, g_ref, do_ref, dq_ref, dk_ref, dv_ref, dg_ref, upd_scr, dec_scr, st_scr, gn_scr, rn_scr):
        def seg_tok(s):
            return pl.ds(pl.multiple_of(s * (ns * C), ns * C), ns * C)

        def seg_chunks(s):
            return pl.ds(pl.multiple_of(s * ns, ns), ns)

        def first(s, c):
            tok, ch = seg_tok(s), seg_chunks(s)
            q, k, v, tri, amask, bl, eb, enb, eend = _gla_chunk_terms(q_ref, k_ref, v_ref, g_ref, tok, ns, C)
            dob = do_ref[0, 0, tok, :].reshape(ns, C, dv).astype(BF16)
            upd_scr[ch] = _bdot(v.astype(BF16), (k * eend).astype(BF16), 1, 1)
            dec_scr[ch] = jnp.exp(bl)
            gn_scr[ch] = _bdot(dob, (q * eb).astype(BF16), 1, 1)
            return c

        lax.fori_loop(0, nseg, first, 0)

        def fstep(i, st):
            st_scr[i] = st
            return st * dec_scr[i] + upd_scr[i]

        lax.fori_loop(0, n, fstep, jnp.zeros((dv, dk), F32))

        def bstep(t, r):
            i = n - 1 - t
            rn_scr[i] = r
            return gn_scr[i] + r * dec_scr[i]

        lax.fori_loop(0, n, bstep, jnp.zeros((dv, dk), F32))

        def second(s, c):
            tok, ch = seg_tok(s), seg_chunks(s)
            q, k, v, tri, amask, bl, eb, enb, eend = _gla_chunk_terms(q_ref, k_ref, v_ref, g_ref, tok, ns, C)
            qe, ke, kend = q * eb, k * enb, k * eend
            qeb, keb, kendb, vb = qe.astype(BF16), ke.astype(BF16), kend.astype(BF16), v.astype(BF16)
            dob = do_ref[0, 0, tok, :].reshape(ns, C, dv).astype(BF16)
            a = jnp.where(amask, _bdot(qeb, keb, 2, 2), 0.0).astype(BF16)
            st = st_scr[ch]
            rn = rn_scr[ch]
            rnb = rn.astype(BF16)
            da = jnp.where(amask, _bdot(dob, vb, 2, 2), 0.0).astype(BF16)
            dvv = _bdot(a, dob, 1, 1) + _bdot(kendb, rnb, 2, 2)
            dqe = _bdot(da, keb, 2, 1) + _bdot(dob, st.astype(BF16), 2, 1)
            dke = _bdot(da, qeb, 1, 1)
            dkend = _bdot(vb, rnb, 2, 1)
            ddec = jnp.sum(rn * st, axis=1, keepdims=True)
            dbl = ddec * jnp.exp(bl) + jnp.sum(dkend * kend, axis=1, keepdims=True)
            db = dqe * qe - dke * ke - dkend * kend
            dg = _bdot(tri, db, 1, 1, HI) + dbl
            dq_ref[0, 0, tok, :] = (dqe * eb * (dk ** -0.5)).reshape(ns * C, dk)
            dk_ref[0, 0, tok, :] = (dke * enb + dkend * eend).reshape(ns * C, dk)
            dv_ref[0, 0, tok, :] = dvv.reshape(ns * C, dv)
            dg_ref[0, 0, tok, :] = dg.reshape(ns * C, dk)
            return c

        lax.fori_loop(0, nseg, second, 0)

    sk = pl.BlockSpec((1, 1, T, dk), lambda d, h: (d, h, 0, 0), pipeline_mode=pl.Buffered(1))
    sv = pl.BlockSpec((1, 1, T, dv), lambda d, h: (d, h, 0, 0), pipeline_mode=pl.Buffered(1))
    st_shape = pltpu.VMEM((n, dv, dk), F32)
    return pl.pallas_call(
        body, name=name, grid=(2, H), in_specs=[sk, sk, sv, sk, sv], out_specs=[sk, sk, sv, sk],
        out_shape=[SDS((2, H, T, dk), F32), SDS((2, H, T, dk), F32), SDS((2, H, T, dv), F32), SDS((2, H, T, dk), F32)],
        scratch_shapes=[st_shape, pltpu.VMEM((n, 1, dk), F32), st_shape, st_shape, st_shape],
        compiler_params=_params(2),
    )(q, k, v, g, do)


def _rope_tables(T):
    half = MLA_ROPE // 2
    inv = ROPE_THETA ** (-jnp.arange(half, dtype=F32) / half)
    ang = jnp.arange(T, dtype=F32)[:, None] * inv[None, :]
    cos, sin = jnp.cos(ang), jnp.sin(ang)
    ctab = jnp.concatenate([jnp.ones((T, MLA_NOPE), F32), cos, cos], axis=1)
    stab = jnp.concatenate([jnp.zeros((T, MLA_NOPE), F32), -sin, sin], axis=1)
    idx = np.arange(MLA_QK)
    swap = np.where(idx < MLA_NOPE, idx, np.where(idx < MLA_NOPE + half, idx + half, idx - half))
    perm = (idx[:, None] == swap[None, :]).astype(np.float32)
    return ctab, stab, jnp.asarray(perm)


def _mla_prep_fwd(x, g, ctab, stab, perm, name):
    H, T, d = x.shape
    tm = _tile(T, 1024)

    def body(x_ref, g_ref, c_ref, s_ref, p_ref, o_ref):
        xv = x_ref[0]
        xn = xv * _rstd(xv) * g_ref[...]
        o_ref[0] = (xn * c_ref[...] + _dot(xn, p_ref[...], precision=HI) * s_ref[...]).astype(BF16)

    tab = pl.BlockSpec((tm, d), lambda h, i: (i, 0))
    return pl.pallas_call(
        body, name=name, grid=(H, T // tm),
        in_specs=[pl.BlockSpec((1, tm, d), lambda h, i: (h, i, 0)), pl.BlockSpec((1, d), lambda h, i: (0, 0)), tab, tab,
                  pl.BlockSpec((d, d), lambda h, i: (0, 0))],
        out_specs=pl.BlockSpec((1, tm, d), lambda h, i: (h, i, 0)),
        out_shape=SDS((H, T, d), BF16),
        compiler_params=_params(2),
    )(x, g, ctab, stab, perm)


def _mla_prep_bwd(x, g, ctab, stab, perm, dy, name):
    H, T, d = x.shape
    tm = _tile(T, 1024)

    def body(x_ref, g_ref, c_ref, s_ref, p_ref, dy_ref, dx_ref, dg_ref):
        @pl.when((pl.program_id(0) == 0) & (pl.program_id(1) == 0))
        def _():
            dg_ref[...] = jnp.zeros_like(dg_ref)

        xv = x_ref[0]
        dyv = dy_ref[0]
        dxn = dyv * c_ref[...] + _dot(dyv * s_ref[...], p_ref[...], precision=HI)
        dx, dg = _rms_bwd(dxn, xv, g_ref[...], _rstd(xv))
        dx_ref[0] = dx
        dg_ref[...] += dg

    tab = pl.BlockSpec((tm, d), lambda h, i: (i, 0))
    blk = pl.BlockSpec((1, tm, d), lambda h, i: (h, i, 0))
    gs = pl.BlockSpec((1, d), lambda h, i: (0, 0))
    return pl.pallas_call(
        body, name=name, grid=(H, T // tm),
        in_specs=[blk, gs, tab, tab, pl.BlockSpec((d, d), lambda h, i: (0, 0)), blk],
        out_specs=[blk, gs], out_shape=[SDS((H, T, d), F32), SDS((1, d), F32)],
        compiler_params=_params(2),
    )(x, g, ctab, stab, perm, dy)


def _mla_attn_fwd(q, k, v, name):
    H, T, d = q.shape
    dv = v.shape[-1]
    tq = _tile(T, 256)
    scale = d ** -0.5

    def body(q_ref, k_ref, v_ref, o_ref):
        s = _dot(q_ref[0], k_ref[0], NT) * scale
        p = jnp.exp(s - jnp.max(s, axis=-1, keepdims=True))
        p = p / jnp.sum(p, axis=-1, keepdims=True)
        o_ref[0] = _dot(p.astype(BF16), v_ref[0])

    return pl.pallas_call(
        body, name=name, grid=(H, T // tq),
        in_specs=[pl.BlockSpec((1, tq, d), lambda h, i: (h, i, 0)), pl.BlockSpec((1, T, d), lambda h, i: (h, 0, 0)),
                  pl.BlockSpec((1, T, dv), lambda h, i: (h, 0, 0))],
        out_specs=pl.BlockSpec((1, tq, dv), lambda h, i: (h, i, 0)),
        out_shape=SDS((H, T, dv), F32),
        compiler_params=_params(2),
    )(q, k, v)


def _mla_attn_bwd(q, k, v, do, name):
    H, T, d = q.shape
    dv = v.shape[-1]
    tq = _tile(T, 256)
    scale = d ** -0.5

    def body(q_ref, k_ref, v_ref, do_ref, dq_ref, dk_ref, dv_ref):
        @pl.when(pl.program_id(1) == 0)
        def _():
            dk_ref[...] = jnp.zeros_like(dk_ref)
            dv_ref[...] = jnp.zeros_like(dv_ref)

        qv, kv = q_ref[0], k_ref[0]
        dob = do_ref[0].astype(BF16)
        st = _dot(kv, qv, NT) * scale
        pt = jnp.exp(st - jnp.max(st, axis=0, keepdims=True))
        pt = pt / jnp.sum(pt, axis=0, keepdims=True)
        dpt = _dot(v_ref[0], dob, NT)
        dst = (pt * (dpt - jnp.sum(pt * dpt, axis=0, keepdims=True)) * scale).astype(BF16)
        dv_ref[0] += _dot(pt.astype(BF16), dob)
        dk_ref[0] += _dot(dst, qv)
        dq_ref[0] = _dot(dst, kv, TN)

    qb = pl.BlockSpec((1, tq, d), lambda h, i: (h, i, 0))
    kb = pl.BlockSpec((1, T, d), lambda h, i: (h, 0, 0))
    vb = pl.BlockSpec((1, T, dv), lambda h, i: (h, 0, 0))
    return pl.pallas_call(
        body, name=name, grid=(H, T // tq),
        in_specs=[qb, kb, vb, pl.BlockSpec((1, tq, dv), lambda h, i: (h, i, 0))],
        out_specs=[qb, kb, vb],
        out_shape=[SDS((H, T, d), F32), SDS((H, T, d), F32), SDS((H, T, dv), F32)],
        compiler_params=_params(2),
    )(q, k, v, do)


def _silu_parts(x):
    s = _sigmoid(x)
    return x * s, s * (1.0 + x * (1.0 - s))


def _mix_out_fwd(x1, y_na, o_gla, y_mla, z, gr_col, gnorm, wbr, w_out, name):
    T, D = x1.shape
    W = y_na.shape[1]
    tm = _tile(T, 512)

    def body(x_ref, na_ref, gla_ref, mla_ref, gt_ref, gr_ref, gn_ref, w0, w1, w2, wo_ref, o_ref):
        og = gla_ref[...]
        gn = gn_ref[...]
        nrm = jnp.concatenate([og[:, h * GLA_DV:(h + 1) * GLA_DV] * _rstd(og[:, h * GLA_DV:(h + 1) * GLA_DV]) * gn
                               for h in range(GLA_HEADS)], axis=1)
        y_gla = nrm * _silu_parts(gr_ref[...])[0]
        mixed = jnp.zeros((tm, D), F32)
        for i, (y, w_ref) in enumerate(((na_ref[...], w0), (y_gla, w1), (mla_ref[...], w2))):
            mixed = mixed + _sigmoid(gt_ref[:, i * D:(i + 1) * D]) * _dot(y.astype(BF16), w_ref[...])
        o_ref[...] = x_ref[...] + _dot(mixed.astype(BF16), wo_ref[...])

    tok = lambda w: pl.BlockSpec((tm, w), lambda i: (i, 0))
    full = lambda shape: pl.BlockSpec(shape, lambda i: (0, 0))
    return pl.pallas_call(
        body, name=name, grid=(T // tm,),
        in_specs=[tok(D), tok(W), tok(W), tok(W), tok(3 * D), pl.BlockSpec((tm, W), lambda i: (i, gr_col)),
                  full((1, GLA_DV)), full((W, D)), full((W, D)), full((W, D)), full((D, D))],
        out_specs=tok(D), out_shape=SDS((T, D), F32),
        compiler_params=_params(1),
    )(x1, y_na, o_gla, y_mla, z, z, gnorm, wbr[0], wbr[1], wbr[2], w_out)


def _mix_out_bwd(dx2, y_na, o_gla, y_mla, z, gr_col, gnorm, wbr, w_out, name):
    T, D = dx2.shape
    W = y_na.shape[1]
    tm = _tile(T, 256)

    def body(dx_ref, na_ref, gla_ref, mla_ref, gt_ref, gr_ref, gn_ref, w0, w1, w2, wo_ref,
             dna_ref, dgla_ref, dmla_ref, dgr_ref, dgt_ref, dw0, dw1, dw2, dwo_ref, dgn_ref):
        @pl.when(pl.program_id(0) == 0)
        def _():
            for r in (dw0, dw1, dw2, dwo_ref, dgn_ref):
                r[...] = jnp.zeros_like(r)

        og = gla_ref[...]
        gn = gn_ref[...]
        grv = gr_ref[...]
        sil, dsil = _silu_parts(grv)
        rs = [_rstd(og[:, h * GLA_DV:(h + 1) * GLA_DV]) for h in range(GLA_HEADS)]
        nrm = jnp.concatenate([og[:, h * GLA_DV:(h + 1) * GLA_DV] * rs[h] * gn for h in range(GLA_HEADS)], axis=1)
        ys = (na_ref[...].astype(BF16), (nrm * sil).astype(BF16), mla_ref[...].astype(BF16))
        dxb = dx_ref[...].astype(BF16)
        dmixed = _dot(dxb, wo_ref[...], NT)
        mixed = jnp.zeros((tm, D), F32)
        dys = []
        for i, (y, w_ref, dw_ref) in enumerate(zip(ys, (w0, w1, w2), (dw0, dw1, dw2))):
            gt = _sigmoid(gt_ref[:, i * D:(i + 1) * D])
            p = _dot(y, w_ref[...])
            mixed = mixed + gt * p
            dp = (dmixed * gt).astype(BF16)
            dgt_ref[:, i * D:(i + 1) * D] = (dmixed * p * gt * (1.0 - gt)).astype(BF16)
            dys.append(_dot(dp, w_ref[...], NT))
            dw_ref[...] += _dot(y, dp, TN)
        dwo_ref[...] += _dot(mixed.astype(BF16), dxb, TN)
        dna_ref[...] = dys[0]
        dmla_ref[...] = dys[2]
        dgr_ref[...] = dys[1] * nrm * dsil
        dn = dys[1] * sil
        dgn = jnp.zeros((1, GLA_DV), F32)
        for h in range(GLA_HEADS):
            sl = slice(h * GLA_DV, (h + 1) * GLA_DV)
            dxh, dgh = _rms_bwd(dn[:, sl], og[:, sl], gn, rs[h])
            dgla_ref[:, sl] = dxh
            dgn = dgn + dgh
        dgn_ref[...] += dgn

    tok = lambda w: pl.BlockSpec((tm, w), lambda i: (i, 0))
    full = lambda shape: pl.BlockSpec(shape, lambda i: (0, 0))
    return pl.pallas_call(
        body, name=name, grid=(T // tm,),
        in_specs=[tok(D), tok(W), tok(W), tok(W), tok(3 * D), pl.BlockSpec((tm, W), lambda i: (i, gr_col)),
                  full((1, GLA_DV)), full((W, D)), full((W, D)), full((W, D)), full((D, D))],
        out_specs=[tok(W), tok(W), tok(W), tok(W), tok(3 * D), full((W, D)), full((W, D)), full((W, D)), full((D, D)),
                   full((1, GLA_DV))],
        out_shape=[SDS((T, W), F32)] * 4 + [SDS((T, 3 * D), BF16)] + [SDS((W, D), F32)] * 3 + [SDS((D, D), F32),
                                                                                              SDS((1, GLA_DV), F32)],
        compiler_params=_params(1),
    )(dx2, y_na, o_gla, y_mla, z, z, gnorm, wbr[0], wbr[1], wbr[2], w_out)


def _loss_head(y, target, name):
    T, D = y.shape
    tm = _tile(T, 1024)

    def body(y_ref, t_ref, dy_ref, l_ref):
        @pl.when(pl.program_id(0) == 0)
        def _():
            l_ref[...] = jnp.zeros_like(l_ref)

        e = y_ref[...] - t_ref[...]
        dy_ref[...] = e * (1.0 / D)
        l_ref[...] += 0.5 * jnp.sum(jnp.mean(e * e, axis=-1, keepdims=True), axis=0, keepdims=True)

    tok = pl.BlockSpec((tm, D), lambda i: (i, 0))
    return pl.pallas_call(
        body, name=name, grid=(T // tm,), in_specs=[tok, tok],
        out_specs=[tok, pl.BlockSpec((1, 1), lambda i: (0, 0))],
        out_shape=[SDS((T, D), F32), SDS((1, 1), F32)],
        compiler_params=_params(1),
    )(y, target)


def _exchange(srcs, broadcast, name):
    n = len(srcs)
    out_shape = [SDS((N_DEV,) + (tuple(s.shape) if broadcast else tuple(s.shape[1:])), s.dtype) for s in srcs]

    def body(*refs):
        src, dst = refs[:n], refs[n:2 * n]
        send_sems, recv_sems, local_sems = refs[2 * n:]
        x, y, c = lax.axis_index("x"), lax.axis_index("y"), lax.axis_index("c")
        me = 4 * x + 2 * y + c
        local = []
        for a in range(n):
            cp = pltpu.make_async_copy(src[a] if broadcast else src[a].at[me], dst[a].at[me], local_sems.at[a])
            cp.start()
            local.append(cp)
        sends, recvs = [], []
        for d in range(1, N_DEV):
            px = 1 - x if d & 4 else x
            py = 1 - y if d & 2 else y
            pc = 1 - c if d & 1 else c
            peer = 4 * px + 2 * py + pc
            for a in range(n):
                cp = pltpu.make_async_remote_copy(
                    src_ref=src[a] if broadcast else src[a].at[peer], dst_ref=dst[a].at[me],
                    send_sem=send_sems.at[a, d - 1], recv_sem=recv_sems.at[a, d - 1],
                    device_id=(px, py, pc), device_id_type=pl.DeviceIdType.MESH)
                cp.start()
                sends.append(cp)
                recvs.append(pltpu.make_async_remote_copy(
                    src_ref=src[a] if broadcast else src[a].at[peer], dst_ref=dst[a].at[peer],
                    send_sem=send_sems.at[a, d - 1], recv_sem=recv_sems.at[a, d - 1],
                    device_id=(px, py, pc), device_id_type=pl.DeviceIdType.MESH))
        for cp in recvs:
            cp.wait_recv()
        for cp in sends:
            cp.wait_send()
        for cp in local:
            cp.wait()

    hbm = pl.BlockSpec(memory_space=pltpu.HBM)
    return pl.pallas_call(
        body, name=name, in_specs=[hbm] * n, out_specs=[hbm] * n, out_shape=out_shape,
        scratch_shapes=[pltpu.SemaphoreType.DMA((n, N_DEV - 1)), pltpu.SemaphoreType.DMA((n, N_DEV - 1)),
                        pltpu.SemaphoreType.DMA((n,))],
    )(*srcs)


def _adam_math(w, g, m, v):
    m = ADAM_B1 * m + (1.0 - ADAM_B1) * g
    v = ADAM_B2 * v + (1.0 - ADAM_B2) * (g * g)
    m_hat = m / (1.0 - ADAM_B1 ** ADAM_STEP)
    v_hat = v / (1.0 - ADAM_B2 ** ADAM_STEP)
    delta = -ADAM_LR * (m_hat / (jnp.sqrt(v_hat) + ADAM_EPS) + ADAM_WD * w)
    return delta, m, v


def _adamw_sharded(landed, goff, w, m, v, name):
    L, r, c = w.shape
    rb = _tile(r, 512)

    def body(l_ref, w_ref, m_ref, v_ref, g_ref, d_ref, nm_ref, nv_ref):
        g = l_ref[0, 0].astype(F32)
        for s in range(1, N_DEV):
            g = g + l_ref[s, 0].astype(F32)
        delta, nm, nv = _adam_math(w_ref[0], g, m_ref[0], v_ref[0])
        g_ref[0] = g
        d_ref[0] = delta
        nm_ref[0] = nm
        nv_ref[0] = nv

    blk = pl.BlockSpec((1, rb, c), lambda l, i: (l, i, 0))
    return pl.pallas_call(
        body, name=name, grid=(L, r // rb),
        in_specs=[pl.BlockSpec((N_DEV, 1, rb, c), lambda l, i: (0, goff + l, i, 0)), blk, blk, blk],
        out_specs=[blk] * 4, out_shape=[SDS((L, r, c), F32)] * 4,
        compiler_params=_params(2),
    )(landed, w, m, v)


def _adamw_replicated(landed, w, m, v, name):
    R = w.shape[0]

    def body(l_ref, w_ref, m_ref, v_ref, g_ref, d_ref, nm_ref, nv_ref):
        g = l_ref[0]
        for s in range(1, N_DEV):
            g = g + l_ref[s]
        delta, nm, nv = _adam_math(w_ref[...], g, m_ref[...], v_ref[...])
        g_ref[...] = g
        d_ref[...] = delta
        nm_ref[...] = nm
        nv_ref[...] = nv

    blk = pl.BlockSpec((R, LANE), lambda i: (0, 0))
    return pl.pallas_call(
        body, name=name, grid=(1,),
        in_specs=[pl.BlockSpec((N_DEV, R, LANE), lambda i: (0, 0, 0)), blk, blk, blk],
        out_specs=[blk] * 4, out_shape=[SDS((R, LANE), F32)] * 4,
        compiler_params=_params(1),
    )(landed, w, m, v)


def _heads(x, h):
    T = x.shape[0]
    return jnp.transpose(x.reshape(T, h, -1), (1, 0, 2))


def _unheads(x):
    h, T, d = x.shape
    return jnp.transpose(x, (1, 0, 2)).reshape(T, h * d)


def _pack(parts):
    flat = jnp.concatenate([p.reshape(-1) for p in parts])
    rows = -(-flat.shape[0] // (8 * LANE)) * 8
    return jnp.pad(flat, (0, rows * LANE - flat.shape[0])).reshape(rows, LANE)


def _unpack(packed, shapes):
    flat = packed.reshape(-1)
    out, off = [], 0
    for s in shapes:
        size = int(np.prod(s))
        out.append(flat[off:off + size].reshape(s))
        off += size
    return out


def _layer_fwd(x0, w, l, lay, tabs):
    T, D = x0.shape
    ctab, stab, perm, na_bias = tabs
    col = lambda name: lay[name][0] // lay[name][1]
    sv = {'x0': x0}
    x1, sv['a1'], sv['b1'] = _ffn_fwd(x0, w['ffn1_norm'], w['ffn1_w1'], w['ffn1_w3'], w['ffn1_w2'], f"ffn1_fwd_{l}")
    z, sv['hb'] = _norm_matmul(x1, 0, w['mix_norm'], w['w_in_z'], f"mix_in_{l}")
    seg = lambda name: z[:, lay[name][0]:lay[name][0] + lay[name][3]]
    sv['na_q'], sv['na_k'], sv['na_v'] = (_heads(seg(n), NA_HEADS) for n in ('na_q', 'na_k', 'na_v'))
    y_na = _unheads(_na_fwd(sv['na_q'], sv['na_k'], sv['na_v'], na_bias, w['na_q_norm'], w['na_k_norm'], f"na_fwd_{l}"))
    gf, gb = _gla_gate_fwd(z, col('glow'), w['up_f'], w['up_b'], w['gla_gf_bias'], w['gla_gb_bias'], f"gla_gate_{l}")
    both = lambda a, h: jnp.stack([_heads(a, h), _heads(a, h)[:, ::-1]])
    sv['gq'], sv['gk'], sv['gv'] = both(seg('gq'), GLA_HEADS), both(seg('gk'), GLA_HEADS), both(seg('gv'), GLA_HEADS)
    sv['gg'] = jnp.stack([_heads(gf, GLA_HEADS), _heads(gb, GLA_HEADS)[:, ::-1]])
    o2 = _gla_fwd(sv['gq'], sv['gk'], sv['gv'], sv['gg'], f"gla_fwd_{l}")
    o_gla = _unheads(o2[0] + o2[1][:, ::-1])
    q_raw, sv['hq'] = _norm_matmul(z, col('c_q'), w['mla_cq_norm'], w['mla_w_uq'], f"mla_uq_{l}")
    kv_raw, sv['hkv'] = _norm_matmul(z, col('c_kv'), w['mla_ckv_norm'], w['mla_w_ukv'], f"mla_ukv_{l}")
    kv_h = _heads(kv_raw, MLA_HEADS)
    k_rope = jnp.broadcast_to(seg('k_rope')[None], (MLA_HEADS, T, MLA_ROPE))
    sv['mq_raw'] = _heads(q_raw, MLA_HEADS)
    sv['mk_raw'] = jnp.concatenate([kv_h[..., :MLA_NOPE], k_rope], axis=-1)
    sv['mq'] = _mla_prep_fwd(sv['mq_raw'], w['mla_q_norm'], ctab, stab, perm, f"mla_qprep_{l}")
    sv['mk'] = _mla_prep_fwd(sv['mk_raw'], w['mla_k_norm'], ctab, stab, perm, f"mla_kprep_{l}")
    sv['mv'] = kv_h[..., MLA_NOPE:].astype(BF16)
    y_mla = _unheads(_mla_attn_fwd(sv['mq'], sv['mk'], sv['mv'], f"mla_attn_{l}"))
    x2 = _mix_out_fwd(x1, y_na, o_gla, y_mla, z, col('gr'), w['gla_out_norm'], w['w_br'], w['w_out'], f"mix_out_{l}")
    sv.update(x1=x1, z=z, y_na=y_na, o_gla=o_gla, y_mla=y_mla, x2=x2)
    x3, sv['a2'], sv['b2'] = _ffn_fwd(x2, w['ffn2_norm'], w['ffn2_w1'], w['ffn2_w3'], w['ffn2_w2'], f"ffn2_fwd_{l}")
    return x3, sv


def _ffn_grads(x, dy, a, b, w, pre, l):
    dx, da, db, u, hb, dyb, dg = _ffn_bwd(x, w[pre + '_norm'], dy, a, b, w[pre + '_w1'], w[pre + '_w3'], w[pre + '_w2'],
                                          f"{pre}_bwd_{l}")
    grads = {pre + '_norm': dg,
             pre + '_w1': _matmul_tn(hb, da, f"{pre}_dw1_{l}"),
             pre + '_w3': _matmul_tn(hb, db, f"{pre}_dw3_{l}"),
             pre + '_w2': _matmul_tn(u, dyb, f"{pre}_dw2_{l}")}
    return dx, grads


def _layer_bwd(dx3, sv, w, l, lay, zp, tabs):
    T, D = dx3.shape
    ctab, stab, perm, na_bias, onehot = tabs
    col = lambda name: lay[name][0] // lay[name][1]
    z = sv['z']
    dx2, grads = _ffn_grads(sv['x2'], dx3, sv['a2'], sv['b2'], w, 'ffn2', l)
    (dy_na, do_gla, dy_mla, dgr, dgates, dw0, dw1, dw2, dwo, dgn) = _mix_out_bwd(
        dx2, sv['y_na'], sv['o_gla'], sv['y_mla'], z, col('gr'), w['gla_out_norm'], w['w_br'], w['w_out'], f"mix_out_bwd_{l}")
    grads.update(w_br_na=dw0, w_br_gla=dw1, w_br_mla=dw2, w_out=dwo, gla_out_norm=dgn)
    dq, dk, dv, dbias, dgq, dgk = _na_bwd(sv['na_q'], sv['na_k'], sv['na_v'], _heads(dy_na, NA_HEADS), na_bias,
                                          w['na_q_norm'], w['na_k_norm'], f"na_bwd_{l}")
    grads.update(na_q_norm=dgq, na_k_norm=dgk, na_rpb=_na_bias_tables_bwd(dbias, onehot))
    do_h = _heads(do_gla, GLA_HEADS)
    gdq, gdk, gdv, gdg = _gla_bwd(sv['gq'], sv['gk'], sv['gv'], sv['gg'], jnp.stack([do_h, do_h[:, ::-1]]), f"gla_bwd_{l}")
    fold = lambda a: _unheads(a[0] + a[1][:, ::-1])
    dglow, dupf, dupb, dbf, dbb = _gla_gate_bwd(z, col('glow'), w['up_f'], w['up_b'], w['gla_gf_bias'], w['gla_gb_bias'],
                                                _unheads(gdg[0]), _unheads(gdg[1][:, ::-1]), f"gla_gate_bwd_{l}")
    grads.update(gla_gf_up=dupf[:GLA_RANK], gla_gb_up=dupb[GLA_RANK:2 * GLA_RANK], gla_gf_bias=dbf, gla_gb_bias=dbb)
    mdq, mdk, mdv = _mla_attn_bwd(sv['mq'], sv['mk'], sv['mv'], _heads(dy_mla, MLA_HEADS), f"mla_attn_bwd_{l}")
    dq_raw, dgqn = _mla_prep_bwd(sv['mq_raw'], w['mla_q_norm'], ctab, stab, perm, mdq, f"mla_qprep_bwd_{l}")
    dk_raw, dgkn = _mla_prep_bwd(sv['mk_raw'], w['mla_k_norm'], ctab, stab, perm, mdk, f"mla_kprep_bwd_{l}")
    dq_raw = _unheads(dq_raw).astype(BF16)
    dkv_raw = _unheads(jnp.concatenate([dk_raw[..., :MLA_NOPE], mdv], axis=-1)).astype(BF16)
    dk_rope = jnp.sum(dk_raw[..., MLA_NOPE:], axis=0)
    dcq, dgcq = _norm_matmul_bwd(dq_raw, w['mla_w_uq'], z, col('c_q'), w['mla_cq_norm'], None, f"mla_uq_bwd_{l}")
    dckv, dgckv = _norm_matmul_bwd(dkv_raw, w['mla_w_ukv'], z, col('c_kv'), w['mla_ckv_norm'], None, f"mla_ukv_bwd_{l}")
    grads.update(mla_q_norm=dgqn, mla_k_norm=dgkn, mla_cq_norm=dgcq, mla_ckv_norm=dgckv,
                 mla_w_uq=_matmul_tn(sv['hq'], dq_raw, f"mla_duq_{l}", F32),
                 mla_w_ukv=_matmul_tn(sv['hkv'], dkv_raw, f"mla_dukv_{l}", F32))
    pieces = {'gates': dgates, 'na_q': _unheads(dq), 'na_k': _unheads(dk), 'na_v': _unheads(dv), 'gv': fold(gdv), 'gr': dgr,
              'gq': fold(gdq), 'gk': fold(gdk), 'c_q': dcq, 'c_kv': dckv, 'glow': dglow,
              'k_rope': jnp.pad(dk_rope, ((0, 0), (0, LANE - MLA_ROPE)))}
    dz = jnp.concatenate([pieces[name].astype(BF16) for name in lay], axis=1)
    dx1, dgmix = _norm_matmul_bwd(dz, w['w_in_z'], sv['x1'], 0, w['mix_norm'], dx2, f"mix_in_bwd_{l}")
    grads.update(mix_norm=dgmix, w_in_z=_matmul_tn(sv['hb'], dz, f"mix_dw_in_{l}"))
    dx0, g1 = _ffn_grads(sv['x0'], dx1, sv['a1'], sv['b1'], w, 'ffn1', l)
    grads.update(g1)
    return dx0, grads


def _groups(shards):
    by_shape = {}
    for n in SHARDED:
        by_shape.setdefault(tuple(shards[n].shape), []).append(n)
    return list(by_shape.values())


def kernel(x, ffn1_norm, ffn1_w1, ffn1_w3, ffn1_w2, mix_norm, w_in, na_q_norm, na_k_norm, na_rpb, gla_gf_up, gla_gf_bias, gla_gb_up, gla_gb_bias, gla_out_norm, mla_cq_norm, mla_ckv_norm, mla_w_uq, mla_w_ukv, mla_q_norm, mla_k_norm, w_br_na, w_br_gla, w_br_mla, w_out, ffn2_norm, ffn2_w1, ffn2_w3, ffn2_w2, loss_target, m_ffn1_norm, m_ffn1_w1, m_ffn1_w3, m_ffn1_w2, m_mix_norm, m_w_in, m_na_q_norm, m_na_k_norm, m_na_rpb, m_gla_gf_up, m_gla_gf_bias, m_gla_gb_up, m_gla_gb_bias, m_gla_out_norm, m_mla_cq_norm, m_mla_ckv_norm, m_mla_w_uq, m_mla_w_ukv, m_mla_q_norm, m_mla_k_norm, m_w_br_na, m_w_br_gla, m_w_br_mla, m_w_out, m_ffn2_norm, m_ffn2_w1, m_ffn2_w3, m_ffn2_w2, v_ffn1_norm, v_ffn1_w1, v_ffn1_w3, v_ffn1_w2, v_mix_norm, v_w_in, v_na_q_norm, v_na_k_norm, v_na_rpb, v_gla_gf_up, v_gla_gf_bias, v_gla_gb_up, v_gla_gb_bias, v_gla_out_norm, v_mla_cq_norm, v_mla_ckv_norm, v_mla_w_uq, v_mla_w_ukv, v_mla_q_norm, v_mla_k_norm, v_w_br_na, v_w_br_gla, v_w_br_mla, v_w_out, v_ffn2_norm, v_ffn2_w1, v_ffn2_w3, v_ffn2_w2):
    given = dict(locals())
    P = {n: given[n] for n in WEIGHTS}
    M = {n: given['m_' + n] for n in WEIGHTS}
    V = {n: given['v_' + n] for n in WEIGHTS}
    xs = x[0]
    T, D = xs.shape
    L = ffn1_norm.shape[0]
    lay, zp, d_in = _z_layout(D)

    groups = _groups(P)
    stacked = [jnp.concatenate([P[n].astype(BF16) for n in names], axis=0) for names in groups]
    gathered = _exchange(stacked, True, "gather_weights")
    full = {}
    for names, g in zip(groups, gathered):
        for k, n in enumerate(names):
            blk = g[:, k * L:(k + 1) * L]
            if n in COL_SHARDED:
                full[n] = jnp.transpose(blk, (1, 2, 0, 3)).reshape(L, blk.shape[2], N_DEV * blk.shape[3])
            else:
                full[n] = jnp.transpose(blk, (1, 0, 2, 3)).reshape(L, N_DEV * blk.shape[2], blk.shape[3])

    ctab, stab, perm = _rope_tables(T)
    na_bias, onehot = _na_bias_tables(na_rpb)

    def layer_weights(l):
        w = {n: full[n][l] for n in SHARDED if n not in ('w_in', 'gla_gf_up', 'gla_gb_up')}
        w['w_in_z'] = _to_z_cols(full['w_in'][l], lay, zp)
        w['up_f'] = jnp.zeros((LANE, GLA_QK_W), BF16).at[:GLA_RANK].set(full['gla_gf_up'][l])
        w['up_b'] = jnp.zeros((LANE, GLA_QK_W), BF16).at[GLA_RANK:2 * GLA_RANK].set(full['gla_gb_up'][l])
        w['w_br'] = (w['w_br_na'], w['w_br_gla'], w['w_br_mla'])
        for n in REPLICATED:
            if n != 'na_rpb':
                w[n] = P[n][l][None, :]
        return w

    ws = [layer_weights(l) for l in range(L)]
    saved = []
    h = xs
    for l in range(L):
        h, sv = _layer_fwd(h, ws[l], l, lay, (ctab, stab, perm, na_bias[l]))
        saved.append(sv)
    dh, loss_local = _loss_head(h, loss_target[0], "loss_head")
    loss = lax.psum(loss_local[0, 0], ("x", "y", "c"))

    layer_grads = [None] * L
    for l in reversed(range(L)):
        dh, layer_grads[l] = _layer_bwd(dh, saved[l], ws[l], l, lay, zp, (ctab, stab, perm, na_bias[l], onehot))
        layer_grads[l]['w_in'] = _from_z_cols(layer_grads[l].pop('w_in_z'), lay, d_in)
    grad_x = dh[None]

    def blocks(n):
        g = jnp.stack([layer_grads[l][n] for l in range(L)])
        if n in COL_SHARDED:
            return jnp.transpose(g.reshape(L, g.shape[1], N_DEV, -1), (2, 0, 1, 3)).astype(BF16)
        return jnp.transpose(g.reshape(L, N_DEV, -1, g.shape[2]), (1, 0, 2, 3)).astype(BF16)

    partial_blocks = [jnp.concatenate([blocks(n) for n in names], axis=1) for names in groups]
    landed = _exchange(partial_blocks, False, "scatter_grads")
    out = {}
    for names, land in zip(groups, landed):
        for k, n in enumerate(names):
            out[n] = _adamw_sharded(land, k * L, P[n], M[n], V[n], f"adamw_{n}")

    rep = [n for n in REPLICATED]
    rep_shapes = [tuple(P[n].shape) for n in rep]
    rep_partial = _pack([jnp.stack([layer_grads[l][n].reshape(P[n].shape[1:]) for l in range(L)]) for n in rep])
    (rep_landed,) = _exchange([rep_partial], True, "gather_small_grads")
    packed = _adamw_replicated(rep_landed, _pack([P[n] for n in rep]), _pack([M[n] for n in rep]),
                               _pack([V[n] for n in rep]), "adamw_replicated")
    for kind, pk in enumerate(packed):
        for n, arr in zip(rep, _unpack(pk, rep_shapes)):
            out.setdefault(n, [None] * 4)[kind] = arr

    res = [loss, grad_x]
    for kind in range(4):
        res += [out[n][kind] for n in WEIGHTS]
    return tuple(res)
```

```python
import functools

import numpy as np
import jax
import jax.numpy as jnp
from jax import lax
from jax.experimental import pallas as pl
from jax.experimental.pallas import tpu as pltpu

F32 = jnp.float32
BF16 = jnp.bfloat16
SDS = jax.ShapeDtypeStruct
HI = lax.Precision.HIGHEST

N_DEV = 8
DEPTH = 4
EPS = 1e-6
LANE = 128
VMEM_LIMIT = 56 * 1024 * 1024

GRID_W = 64
NA_HEADS, NA_HD, NA_WIN_R, NA_WIN_C = 8, 64, 8, 16
GLA_HEADS, GLA_DK, GLA_DV, GLA_RANK, GLA_TAU, GLA_CHUNK = 4, 64, 128, 16, 16.0, 64
MLA_HEADS, MLA_QR, MLA_KVR, MLA_NOPE, MLA_ROPE, MLA_V = 4, 256, 256, 128, 64, 128
MLA_QK = MLA_NOPE + MLA_ROPE
ROPE_THETA = 10000.0
NA_W = NA_HEADS * NA_HD
GLA_QK_W = GLA_HEADS * GLA_DK
GLA_V_W = GLA_HEADS * GLA_DV
MLA_V_W = MLA_HEADS * MLA_V

ADAM_LR, ADAM_B1, ADAM_B2, ADAM_EPS, ADAM_WD, ADAM_STEP = 0.001, 0.9, 0.999, 1e-08, 0.01, 10

WEIGHTS = ['ffn1_norm', 'ffn1_w1', 'ffn1_w3', 'ffn1_w2', 'mix_norm', 'w_in', 'na_q_norm', 'na_k_norm', 'na_rpb',
           'gla_gf_up', 'gla_gf_bias', 'gla_gb_up', 'gla_gb_bias', 'gla_out_norm', 'mla_cq_norm', 'mla_ckv_norm',
           'mla_w_uq', 'mla_w_ukv', 'mla_q_norm', 'mla_k_norm', 'w_br_na', 'w_br_gla', 'w_br_mla', 'w_out',
           'ffn2_norm', 'ffn2_w1', 'ffn2_w3', 'ffn2_w2']
COL_SHARDED = ['ffn1_w1', 'ffn1_w3', 'w_in', 'gla_gf_up', 'gla_gb_up', 'mla_w_uq', 'mla_w_ukv', 'w_br_na', 'w_br_gla',
               'w_br_mla', 'ffn2_w1', 'ffn2_w3']
ROW_SHARDED = ['ffn1_w2', 'w_out', 'ffn2_w2']
SHARDED = [n for n in WEIGHTS if n in COL_SHARDED or n in ROW_SHARDED]
REPLICATED = [n for n in WEIGHTS if n not in SHARDED]

NT = (((1,), (1,)), ((), ()))
TN = (((0,), (0,)), ((), ()))


def _tile(n, pref, mult=8):
    best = None
    for t in range(mult, min(n, pref) + 1, mult):
        if n % t == 0:
            best = t
    return best if best is not None else n


def _params(n_axes):
    return pltpu.CompilerParams(dimension_semantics=("arbitrary",) * n_axes, vmem_limit_bytes=VMEM_LIMIT)


def _dot(a, b, dims=None, precision=None):
    if dims is None:
        return jnp.dot(a, b, preferred_element_type=F32, precision=precision)
    return lax.dot_general(a, b, dims, preferred_element_type=F32, precision=precision)


def _sigmoid(x):
    return 1.0 / (1.0 + jnp.exp(-x))


def _rstd(x):
    return lax.rsqrt(jnp.mean(x * x, axis=-1, keepdims=True) + EPS)


def _rms_bwd(dy, x, g, r):
    xh = x * r
    dyg = dy * g
    dx = r * (dyg - xh * jnp.mean(dyg * xh, axis=-1, keepdims=True))
    return dx, jnp.sum(dy * xh, axis=0, keepdims=True)


def _z_layout(d_model):
    own = {}
    off = 0
    for name, w in [('na_q', NA_W), ('na_k', NA_W), ('na_v', NA_W), ('gq', GLA_QK_W), ('gk', GLA_QK_W),
                    ('gv', GLA_V_W), ('gr', GLA_V_W), ('gfl', GLA_RANK), ('gbl', GLA_RANK), ('c_q', MLA_QR),
                    ('c_kv', MLA_KVR), ('k_rope', MLA_ROPE), ('gates', 3 * d_model)]:
        own[name] = (off, w)
        off += w
    order = [('gates', 3 * d_model), ('na_q', NA_W), ('na_k', NA_W), ('na_v', NA_W), ('gv', GLA_V_W), ('gr', GLA_V_W),
             ('gq', GLA_QK_W), ('gk', GLA_QK_W), ('c_q', MLA_QR), ('c_kv', MLA_KVR), ('glow', LANE), ('k_rope', LANE)]
    lay = {}
    z = 0
    for name, w in order:
        assert z % w == 0, (name, z, w)
        if name == 'glow':
            lay[name] = (z, w, own['gfl'][0], 2 * GLA_RANK)
        else:
            lay[name] = (z, w, own[name][0], own[name][1])
        z += w
    return lay, z, off


def _to_z_cols(w, lay, zp):
    parts = []
    for name, (zo, zw, oo, ow) in lay.items():
        parts.append(w[:, oo:oo + ow])
        if zw > ow:
            parts.append(jnp.zeros((w.shape[0], zw - ow), w.dtype))
    return jnp.concatenate(parts, axis=1)


def _from_z_cols(wz, lay, d_in):
    own = sorted(lay.values(), key=lambda t: t[2])
    return jnp.concatenate([wz[:, zo:zo + ow] for (zo, zw, oo, ow) in own], axis=1)


def _ffn_fwd(x, g, w1, w3, w2, name, exch=None):
    T, D = x.shape
    Fd = w1.shape[1]
    tm, tf = _tile(T, 1024), _tile(Fd, 256, LANE)
    nj = Fd // tf

    def body(x_ref, g_ref, w1_ref, w3_ref, w2_ref, o_ref, a_ref, b_ref, h_scr, acc):
        j = pl.program_id(1)

        @pl.when(j == 0)
        def _():
            xv = x_ref[...]
            h_scr[...] = (xv * _rstd(xv) * g_ref[...]).astype(BF16)
            acc[...] = jnp.zeros_like(acc)

        h = h_scr[...]
        a = _dot(h, w1_ref[...])
        b = _dot(h, w3_ref[...])
        a_ref[...] = a.astype(BF16)
        b_ref[...] = b.astype(BF16)
        u = a * _sigmoid(a) * b
        acc[...] += _dot(u.astype(BF16), w2_ref[...])

        @pl.when(j == nj - 1)
        def _():
            o_ref[...] = x_ref[...] + 0.5 * acc[...]

    return _call(
        body, name=name, grid=(T // tm, nj),
        in_specs=[pl.BlockSpec((tm, D), lambda i, j: (i, 0)), pl.BlockSpec((1, D), lambda i, j: (0, 0)),
                  pl.BlockSpec((D, tf), lambda i, j: (0, j)), pl.BlockSpec((D, tf), lambda i, j: (0, j)),
                  pl.BlockSpec((tf, D), lambda i, j: (j, 0))],
        out_specs=[pl.BlockSpec((tm, D), lambda i, j: (i, 0)), pl.BlockSpec((tm, tf), lambda i, j: (i, j)),
                   pl.BlockSpec((tm, tf), lambda i, j: (i, j))],
        out_shape=[SDS((T, D), F32), SDS((T, Fd), BF16), SDS((T, Fd), BF16)],
        scratch_shapes=[pltpu.VMEM((tm, D), BF16), pltpu.VMEM((tm, D), F32)],
        args=(x, g, w1, w3, w2), exch=exch)


def _ffn_bwd(x, g, dy, a, b, w1, w3, w2, name, exch=None):
    T, D = x.shape
    Fd = w1.shape[1]
    tm, tf = _tile(T, 512), _tile(Fd, 256, LANE)
    nj = Fd // tf

    def body(x_ref, g_ref, dy_ref, a_ref, b_ref, w1_ref, w3_ref, w2_ref,
             dx_ref, da_ref, db_ref, u_ref, hb_ref, dyb_ref, dg_ref, dh_acc):
        i, j = pl.program_id(0), pl.program_id(1)

        @pl.when(j == 0)
        def _():
            xv = x_ref[...]
            hb_ref[...] = (xv * _rstd(xv) * g_ref[...]).astype(BF16)
            dyb_ref[...] = (0.5 * dy_ref[...]).astype(BF16)
            dh_acc[...] = jnp.zeros_like(dh_acc)

        @pl.when((i == 0) & (j == 0))
        def _():
            dg_ref[...] = jnp.zeros_like(dg_ref)

        du = _dot(dyb_ref[...], w2_ref[...], NT)
        av = a_ref[...].astype(F32)
        bv = b_ref[...].astype(F32)
        s = _sigmoid(av)
        sl = av * s
        da = (du * bv * (s * (1.0 + av * (1.0 - s)))).astype(BF16)
        db = (du * sl).astype(BF16)
        da_ref[...] = da
        db_ref[...] = db
        u_ref[...] = (sl * bv).astype(BF16)
        dh_acc[...] += _dot(da, w1_ref[...], NT) + _dot(db, w3_ref[...], NT)

        @pl.when(j == nj - 1)
        def _():
            xv = x_ref[...]
            dxn, dg = _rms_bwd(dh_acc[...], xv, g_ref[...], _rstd(xv))
            dx_ref[...] = dy_ref[...] + dxn
            dg_ref[...] += dg

    row = lambda i, j: (i, 0)
    return _call(
        body, name=name, grid=(T // tm, nj),
        in_specs=[pl.BlockSpec((tm, D), row), pl.BlockSpec((1, D), lambda i, j: (0, 0)), pl.BlockSpec((tm, D), row),
                  pl.BlockSpec((tm, tf), lambda i, j: (i, j)), pl.BlockSpec((tm, tf), lambda i, j: (i, j)),
                  pl.BlockSpec((D, tf), lambda i, j: (0, j)), pl.BlockSpec((D, tf), lambda i, j: (0, j)),
                  pl.BlockSpec((tf, D), lambda i, j: (j, 0))],
        out_specs=[pl.BlockSpec((tm, D), row), pl.BlockSpec((tm, tf), lambda i, j: (i, j)),
                   pl.BlockSpec((tm, tf), lambda i, j: (i, j)), pl.BlockSpec((tm, tf), lambda i, j: (i, j)),
                   pl.BlockSpec((tm, D), row), pl.BlockSpec((tm, D), row), pl.BlockSpec((1, D), lambda i, j: (0, 0))],
        out_shape=[SDS((T, D), F32), SDS((T, Fd), BF16), SDS((T, Fd), BF16), SDS((T, Fd), BF16),
                   SDS((T, D), BF16), SDS((T, D), BF16), SDS((1, D), F32)],
        scratch_shapes=[pltpu.VMEM((tm, D), F32)],
        args=(x, g, dy, a, b, w1, w3, w2), exch=exch)


def _matmul_tn(a, b, name, out_dtype=BF16):
    T, K = a.shape
    N = b.shape[1]
    tk = _tile(K, 1408, LANE)
    tn = _tile(N, 1408, LANE)
    tt = _tile(T, 1024)
    nt = T // tt

    def body(a_ref, b_ref, o_ref, acc):
        t = pl.program_id(2)

        @pl.when(t == 0)
        def _():
            acc[...] = jnp.zeros_like(acc)

        acc[...] += _dot(a_ref[...], b_ref[...], TN)

        @pl.when(t == nt - 1)
        def _():
            o_ref[...] = acc[...].astype(out_dtype)

    return pl.pallas_call(
        body, name=name, grid=(K // tk, N // tn, nt),
        in_specs=[pl.BlockSpec((tt, tk), lambda k, n, t: (t, k)), pl.BlockSpec((tt, tn), lambda k, n, t: (t, n))],
        out_specs=pl.BlockSpec((tk, tn), lambda k, n, t: (k, n)),
        out_shape=SDS((K, N), out_dtype),
        scratch_shapes=[pltpu.VMEM((tk, tn), F32)],
        compiler_params=_params(3),
    )(a, b)


def _norm_matmul(x, xcol, g, w, name, exch=None):
    T = x.shape[0]
    D = g.shape[1]
    N = w.shape[1]
    tm, tn = _tile(T, 1024), _tile(N, 768, LANE)

    def body(x_ref, g_ref, w_ref, z_ref, hb_ref):
        @pl.when(pl.program_id(1) == 0)
        def _():
            xv = x_ref[...]
            hb_ref[...] = (xv * _rstd(xv) * g_ref[...]).astype(BF16)

        z_ref[...] = _dot(hb_ref[...], w_ref[...])

    return _call(
        body, name=name, grid=(T // tm, N // tn),
        in_specs=[pl.BlockSpec((tm, D), lambda i, j: (i, xcol)), pl.BlockSpec((1, D), lambda i, j: (0, 0)),
                  pl.BlockSpec((D, tn), lambda i, j: (0, j))],
        out_specs=[pl.BlockSpec((tm, tn), lambda i, j: (i, j)), pl.BlockSpec((tm, D), lambda i, j: (i, 0))],
        out_shape=[SDS((T, N), F32), SDS((T, D), BF16)],
        args=(x, g, w), exch=exch)


def _norm_matmul_bwd(dz, w, x, xcol, g, resid, name):
    T, N = dz.shape
    D = g.shape[1]
    tm, tn = _tile(T, 1024), _tile(N, 768, LANE)
    nj = N // tn
    has_resid = resid is not None

    def body(*refs):
        if has_resid:
            dz_ref, w_ref, x_ref, g_ref, r_ref, dx_ref, dg_ref, acc = refs
        else:
            dz_ref, w_ref, x_ref, g_ref, dx_ref, dg_ref, acc = refs
        i, j = pl.program_id(0), pl.program_id(1)

        @pl.when(j == 0)
        def _():
            acc[...] = jnp.zeros_like(acc)

        @pl.when((i == 0) & (j == 0))
        def _():
            dg_ref[...] = jnp.zeros_like(dg_ref)

        acc[...] += _dot(dz_ref[...], w_ref[...], NT)

        @pl.when(j == nj - 1)
        def _():
            xv = x_ref[...]
            dxn, dg = _rms_bwd(acc[...], xv, g_ref[...], _rstd(xv))
            dx_ref[...] = dxn + r_ref[...] if has_resid else dxn
            dg_ref[...] += dg

    in_specs = [pl.BlockSpec((tm, tn), lambda i, j: (i, j)), pl.BlockSpec((D, tn), lambda i, j: (0, j)),
                pl.BlockSpec((tm, D), lambda i, j: (i, xcol)), pl.BlockSpec((1, D), lambda i, j: (0, 0))]
    args = [dz, w, x, g]
    if has_resid:
        in_specs.append(pl.BlockSpec((tm, D), lambda i, j: (i, 0)))
        args.append(resid)
    return pl.pallas_call(
        body, name=name, grid=(T // tm, nj), in_specs=in_specs,
        out_specs=[pl.BlockSpec((tm, D), lambda i, j: (i, 0)), pl.BlockSpec((1, D), lambda i, j: (0, 0))],
        out_shape=[SDS((T, D), F32), SDS((1, D), F32)],
        scratch_shapes=[pltpu.VMEM((tm, D), F32)],
        compiler_params=_params(2),
    )(*args)


def _na_bias_tables(rpb):
    L = rpb.shape[0]
    qc = np.arange(GRID_W)[:, None]
    kc = np.arange(GRID_W)[None, :]
    dc = np.clip(kc - qc + NA_WIN_C - 1, 0, 2 * NA_WIN_C - 2)
    c0 = np.clip(qc - NA_WIN_C // 2, 0, GRID_W - NA_WIN_C)
    ok = (kc >= c0) & (kc < c0 + NA_WIN_C)
    onehot = (dc.reshape(-1)[None, :] == np.arange(2 * NA_WIN_C - 1)[:, None]).astype(np.float32)
    t1 = jnp.stack([rpb[:, :, c:c + NA_WIN_R, :] for c in range(NA_WIN_R)], axis=2)
    full = jnp.dot(t1.reshape(-1, 2 * NA_WIN_C - 1), jnp.asarray(onehot), precision=HI)
    full = full.reshape(L, NA_HEADS, NA_WIN_R, NA_WIN_R, GRID_W, GRID_W)
    full = jnp.where(jnp.asarray(ok)[None, None, None, None], full, -1e30)
    return jnp.transpose(full, (0, 1, 2, 4, 3, 5)).reshape(L, NA_HEADS, NA_WIN_R, GRID_W, NA_WIN_R * GRID_W), onehot


def _na_bias_tables_bwd(db, onehot):
    d = db.reshape(NA_HEADS, NA_WIN_R, GRID_W, NA_WIN_R, GRID_W)
    d = jnp.transpose(d, (0, 1, 3, 2, 4)).reshape(-1, GRID_W * GRID_W)
    t1 = jnp.dot(d, jnp.asarray(onehot.T), precision=HI).reshape(NA_HEADS, NA_WIN_R, NA_WIN_R, 2 * NA_WIN_C - 1)
    out = jnp.zeros((NA_HEADS, 2 * NA_WIN_R - 1, 2 * NA_WIN_C - 1), F32)
    for c in range(NA_WIN_R):
        out = out.at[:, c:c + NA_WIN_R, :].add(t1[:, c])
    return out


def _na_windows(bi, rb, rows):
    wsl, cls = [], []
    for i in range(rb):
        r = bi * rb + i
        r0 = jnp.clip(r - NA_WIN_R // 2, 0, rows - NA_WIN_R)
        wsl.append(pl.ds(pl.multiple_of(r0 * GRID_W, GRID_W), NA_WIN_R * GRID_W))
        cls.append(r0 - r + NA_WIN_R - 1)
    return wsl, cls


def _na_fwd(q, k, v, bias, gq, gk, name, exch=None):
    H, T, d = q.shape
    rows = T // GRID_W
    rb = _tile(rows, 8, 1)
    win = NA_WIN_R * GRID_W
    scale = d ** -0.5

    def body(q_ref, k_ref, v_ref, b_ref, gq_ref, gk_ref, o_ref, qn, kn, vb):
        qv, kv = q_ref[0], k_ref[0]
        qn[...] = (qv * _rstd(qv) * gq_ref[...] * scale).astype(BF16)
        kn[...] = (kv * _rstd(kv) * gk_ref[...]).astype(BF16)
        vb[...] = v_ref[0].astype(BF16)

        def block(bi, c):
            wsl, cls = _na_windows(bi, rb, rows)
            qsl = pl.ds(pl.multiple_of(bi * (rb * GRID_W), rb * GRID_W), rb * GRID_W)
            kw = jnp.stack([kn[w, :] for w in wsl])
            vw = jnp.stack([vb[w, :] for w in wsl])
            s = _bdot(qn[qsl, :].reshape(rb, GRID_W, d), kw, 2, 2) + jnp.stack([b_ref[0, c_] for c_ in cls])
            p = jnp.exp(s - jnp.max(s, axis=-1, keepdims=True))
            p = p / jnp.sum(p, axis=-1, keepdims=True)
            o_ref[0, qsl, :] = _bdot(p.astype(BF16), vw, 2, 1).reshape(rb * GRID_W, d)
            return c

        lax.fori_loop(0, rows // rb, block, 0)

    hm = pl.BlockSpec((1, T, d), lambda h: (h, 0, 0))
    gs = pl.BlockSpec((1, d), lambda h: (0, 0))
    return _call(
        body, name=name, grid=(H,),
        in_specs=[hm, hm, hm, pl.BlockSpec((1, NA_WIN_R, GRID_W, win), lambda h: (h, 0, 0, 0)), gs, gs],
        out_specs=[hm], out_shape=[SDS((H, T, d), F32)],
        scratch_shapes=[pltpu.VMEM((T, d), BF16)] * 3,
        args=(q, k, v, bias, gq, gk), exch=exch)


def _na_bwd(q, k, v, do, bias, gq, gk, name):
    H, T, d = q.shape
    rows = T // GRID_W
    rb = _tile(rows, 8, 1)
    win = NA_WIN_R * GRID_W
    scale = d ** -0.5

    def body(q_ref, k_ref, v_ref, do_ref, b_ref, gq_ref, gk_ref,
             dq_ref, dk_ref, dv_ref, db_ref, dgq_ref, dgk_ref, qn, kn, vb, dob, dqn, dkn):
        h = pl.program_id(0)
        qv, kv = q_ref[0], k_ref[0]
        rq, rk = _rstd(qv), _rstd(kv)
        qn[...] = (qv * rq * gq_ref[...] * scale).astype(BF16)
        kn[...] = (kv * rk * gk_ref[...]).astype(BF16)
        vb[...] = v_ref[0].astype(BF16)
        dob[...] = do_ref[0].astype(BF16)
        dkn[...] = jnp.zeros_like(dkn)
        dv_ref[...] = jnp.zeros_like(dv_ref)
        db_ref[...] = jnp.zeros_like(db_ref)

        @pl.when(h == 0)
        def _():
            dgq_ref[...] = jnp.zeros_like(dgq_ref)
            dgk_ref[...] = jnp.zeros_like(dgk_ref)

        def block(bi, c):
            wsl, cls = _na_windows(bi, rb, rows)
            qsl = pl.ds(pl.multiple_of(bi * (rb * GRID_W), rb * GRID_W), rb * GRID_W)
            qs = qn[qsl, :].reshape(rb, GRID_W, d)
            dos = dob[qsl, :].reshape(rb, GRID_W, d)
            kw = jnp.stack([kn[w, :] for w in wsl])
            vw = jnp.stack([vb[w, :] for w in wsl])
            s = _bdot(qs, kw, 2, 2) + jnp.stack([b_ref[0, c_] for c_ in cls])
            p = jnp.exp(s - jnp.max(s, axis=-1, keepdims=True))
            p = p / jnp.sum(p, axis=-1, keepdims=True)
            dp = _bdot(dos, vw, 2, 2)
            ds = p * (dp - jnp.sum(dp * p, axis=-1, keepdims=True))
            dsb = ds.astype(BF16)
            dqn[qsl, :] = _bdot(dsb, kw, 2, 1).reshape(rb * GRID_W, d)
            dvw = _bdot(p.astype(BF16), dos, 1, 1)
            dkw = _bdot(dsb, qs, 1, 1)
            for i in range(rb):
                db_ref[0, cls[i]] += ds[i]
                dv_ref[0, wsl[i], :] += dvw[i]
                dkn[wsl[i], :] += dkw[i]
            return c

        lax.fori_loop(0, rows // rb, block, 0)
        dq, dgq = _rms_bwd(dqn[...] * scale, qv, gq_ref[...], rq)
        dk, dgk = _rms_bwd(dkn[...], kv, gk_ref[...], rk)
        dq_ref[0] = dq
        dk_ref[0] = dk
        dgq_ref[...] += dgq
        dgk_ref[...] += dgk

    hm = pl.BlockSpec((1, T, d), lambda h: (h, 0, 0))
    gs = pl.BlockSpec((1, d), lambda h: (0, 0))
    bs = pl.BlockSpec((1, NA_WIN_R, GRID_W, win), lambda h: (h, 0, 0, 0))
    return pl.pallas_call(
        body, name=name, grid=(H,),
        in_specs=[hm, hm, hm, hm, bs, gs, gs],
        out_specs=[hm, hm, hm, bs, gs, gs],
        out_shape=[SDS((H, T, d), F32)] * 3 + [SDS((H, NA_WIN_R, GRID_W, win), F32), SDS((1, d), F32), SDS((1, d), F32)],
        scratch_shapes=[pltpu.VMEM((T, d), BF16)] * 4 + [pltpu.VMEM((T, d), F32)] * 2,
        compiler_params=_params(1),
    )(q, k, v, do, bias, gq, gk)


def _gla_gate_fwd(z, col, upf, upb, bf, bb, name):
    T = z.shape[0]
    W = upf.shape[1]
    tm = _tile(T, 1024)

    def body(z_ref, upf_ref, upb_ref, bf_ref, bb_ref, gf_ref, gb_ref):
        seg = z_ref[...].astype(BF16)
        for up_ref, b_ref, o_ref in ((upf_ref, bf_ref, gf_ref), (upb_ref, bb_ref, gb_ref)):
            pre = _dot(seg, up_ref[...]) + b_ref[...]
            o_ref[...] = (jnp.minimum(pre, 0.0) - jnp.log(1.0 + jnp.exp(-jnp.abs(pre)))) * (1.0 / GLA_TAU)

    full = lambda shape: pl.BlockSpec(shape, lambda i: (0, 0))
    return pl.pallas_call(
        body, name=name, grid=(T // tm,),
        in_specs=[pl.BlockSpec((tm, LANE), lambda i: (i, col)), full((LANE, W)), full((LANE, W)), full((1, W)), full((1, W))],
        out_specs=[pl.BlockSpec((tm, W), lambda i: (i, 0))] * 2,
        out_shape=[SDS((T, W), F32)] * 2,
        compiler_params=_params(1),
    )(z, upf, upb, bf, bb)


def _gla_gate_bwd(z, col, upf, upb, bf, bb, dgf, dgb, name):
    T = z.shape[0]
    W = upf.shape[1]
    tm = _tile(T, 1024)

    def body(z_ref, upf_ref, upb_ref, bf_ref, bb_ref, dgf_ref, dgb_ref, dseg_ref, dupf_ref, dupb_ref, dbf_ref, dbb_ref):
        @pl.when(pl.program_id(0) == 0)
        def _():
            for r in (dupf_ref, dupb_ref, dbf_ref, dbb_ref):
                r[...] = jnp.zeros_like(r)

        seg = z_ref[...].astype(BF16)
        dseg = jnp.zeros(dseg_ref.shape, F32)
        for up_ref, b_ref, dg_ref, dup_ref, db_ref in ((upf_ref, bf_ref, dgf_ref, dupf_ref, dbf_ref),
                                                      (upb_ref, bb_ref, dgb_ref, dupb_ref, dbb_ref)):
            pre = _dot(seg, up_ref[...]) + b_ref[...]
            dpre = dg_ref[...] * (1.0 / GLA_TAU) * (1.0 - _sigmoid(pre))
            dpb = dpre.astype(BF16)
            dseg = dseg + _dot(dpb, up_ref[...], NT)
            dup_ref[...] += _dot(seg, dpb, TN)
            db_ref[...] += jnp.sum(dpre, axis=0, keepdims=True)
        dseg_ref[...] = dseg

    full = lambda shape: pl.BlockSpec(shape, lambda i: (0, 0))
    tok = pl.BlockSpec((tm, W), lambda i: (i, 0))
    return pl.pallas_call(
        body, name=name, grid=(T // tm,),
        in_specs=[pl.BlockSpec((tm, LANE), lambda i: (i, col)), full((LANE, W)), full((LANE, W)), full((1, W)), full((1, W)),
                  tok, tok],
        out_specs=[pl.BlockSpec((tm, LANE), lambda i: (i, 0)), full((LANE, W)), full((LANE, W)), full((1, W)), full((1, W))],
        out_shape=[SDS((T, LANE), F32), SDS((LANE, W), F32), SDS((LANE, W), F32), SDS((1, W), F32), SDS((1, W), F32)],
        compiler_params=_params(1),
    )(z, upf, upb, bf, bb, dgf, dgb)


def _bdot(a, b, ca, cb, precision=None):
    return lax.dot_general(a, b, (((ca,), (cb,)), ((0,), (0,))), preferred_element_type=F32, precision=precision)


def _gla_chunk_terms(q_ref, k_ref, v_ref, g_ref, rev, tok, n, C):
    dk, dv = q_ref.shape[-1], v_ref.shape[-1]
    q = q_ref[0, tok, :].reshape(n, C, dk) * (dk ** -0.5)
    k = k_ref[0, tok, :].reshape(n, C, dk)
    v = v_ref[0, tok, :].reshape(n, C, dv)
    g = g_ref[rev, 0, tok, :].reshape(n, C, dk)
    ii = lax.broadcasted_iota(jnp.int32, (C, C), 0)
    jj = lax.broadcasted_iota(jnp.int32, (C, C), 1)
    tri = jnp.broadcast_to(((ii <= jj) if rev else (ii >= jj)).astype(F32), (n, C, C))
    amask = ((jj > ii) if rev else (ii >= jj))[None]
    b = _bdot(tri, g, 2, 1, HI)
    bl = jnp.sum(g, axis=1, keepdims=True)
    eb = jnp.exp(b)
    enb = jnp.exp(-b)
    eend = jnp.exp(bl - b)
    return q, k, v, tri, amask, bl, eb, enb, eend


def _gla_segments(T, C):
    n = T // C
    ns = _tile(n, 16, 1)

    def tok(s):
        return pl.ds(pl.multiple_of(s * (ns * C), ns * C), ns * C)

    def chunks(s):
        return pl.ds(pl.multiple_of(s * ns, ns), ns)

    return n, ns, n // ns, tok, chunks


def _gla_fwd(q, k, v, g, name):
    H, T, dk = q.shape
    dv = v.shape[-1]
    C = GLA_CHUNK
    n, ns, nseg, seg_tok, seg_chunks = _gla_segments(T, C)

    def body(q_ref, k_ref, v_ref, g_ref, o_ref, upd_scr, dec_scr, st_scr):
        for rev in (0, 1):
            def intra(s, c):
                tok, ch = seg_tok(s), seg_chunks(s)
                q, k, v, tri, amask, bl, eb, enb, eend = _gla_chunk_terms(q_ref, k_ref, v_ref, g_ref, rev, tok, ns, C)
                vb = v.astype(BF16)
                a = jnp.where(amask, _bdot((q * eb).astype(BF16), (k * enb).astype(BF16), 2, 2), 0.0).astype(BF16)
                o = _bdot(a, vb, 2, 1).reshape(ns * C, dv)
                if rev:
                    o_ref[0, tok, :] += o
                else:
                    o_ref[0, tok, :] = o
                upd_scr[ch] = _bdot(vb, (k * eend).astype(BF16), 1, 1)
                dec_scr[ch] = jnp.exp(bl)
                return c

            lax.fori_loop(0, nseg, intra, 0)

            def step(t, st):
                i = n - 1 - t if rev else t
                st_scr[i] = st
                return st * dec_scr[i] + upd_scr[i]

            lax.fori_loop(0, n, step, jnp.zeros((dv, dk), F32))

            def inter(s, c):
                tok, ch = seg_tok(s), seg_chunks(s)
                q, k, v, tri, amask, bl, eb, enb, eend = _gla_chunk_terms(q_ref, k_ref, v_ref, g_ref, rev, tok, ns, C)
                o_ref[0, tok, :] += _bdot((q * eb).astype(BF16), st_scr[ch].astype(BF16), 2, 2).reshape(ns * C, dv)
                return c

            lax.fori_loop(0, nseg, inter, 0)

    sk = pl.BlockSpec((1, T, dk), lambda h: (h, 0, 0))
    sv = pl.BlockSpec((1, T, dv), lambda h: (h, 0, 0))
    sg = pl.BlockSpec((2, 1, T, dk), lambda h: (0, h, 0, 0))
    return pl.pallas_call(
        body, name=name, grid=(H,), in_specs=[sk, sk, sv, sg], out_specs=sv,
        out_shape=SDS((H, T, dv), F32),
        scratch_shapes=[pltpu.VMEM((n, dv, dk), F32), pltpu.VMEM((n, 1, dk), F32), pltpu.VMEM((n, dv, dk), F32)],
        compiler_params=_params(1),
    )(q, k, v, g)


def _gla_bwd(q, k, v, g, do, name):
    H, T, dk = q.shape
    dv = v.shape[-1]
    C = GLA_CHUNK
    n, ns, nseg, seg_tok, seg_chunks = _gla_segments(T, C)

    def one_scan(rev, q_ref, k_ref, v_ref, g_ref, do_ref, dq_ref, dk_ref, dv_ref, dg_ref, upd_scr, dec_scr, st_scr, gn_scr,
                 rn_scr):
        def first(s, c):
            tok, ch = seg_tok(s), seg_chunks(s)
            q, k, v, tri, amask, bl, eb, enb, eend = _gla_chunk_terms(q_ref, k_ref, v_ref, g_ref, rev, tok, ns, C)
            dob = do_ref[0, tok, :].reshape(ns, C, dv).astype(BF16)
            upd_scr[ch] = _bdot(v.astype(BF16), (k * eend).astype(BF16), 1, 1)
            dec_scr[ch] = jnp.exp(bl)
            gn_scr[ch] = _bdot(dob, (q * eb).astype(BF16), 1, 1)
            return c

        lax.fori_loop(0, nseg, first, 0)

        def fstep(t, st):
            i = n - 1 - t if rev else t
            st_scr[i] = st
            return st * dec_scr[i] + upd_scr[i]

        lax.fori_loop(0, n, fstep, jnp.zeros((dv, dk), F32))

        def bstep(t, r):
            i = t if rev else n - 1 - t
            rn_scr[i] = r
            return gn_scr[i] + r * dec_scr[i]

        lax.fori_loop(0, n, bstep, jnp.zeros((dv, dk), F32))

        def second(s, c):
            tok, ch = seg_tok(s), seg_chunks(s)
            q, k, v, tri, amask, bl, eb, enb, eend = _gla_chunk_terms(q_ref, k_ref, v_ref, g_ref, rev, tok, ns, C)
            qe, ke, kend = q * eb, k * enb, k * eend
            qeb, keb, kendb, vb = qe.astype(BF16), ke.astype(BF16), kend.astype(BF16), v.astype(BF16)
            dob = do_ref[0, tok, :].reshape(ns, C, dv).astype(BF16)
            a = jnp.where(amask, _bdot(qeb, keb, 2, 2), 0.0).astype(BF16)
            st = st_scr[ch]
            rn = rn_scr[ch]
            rnb = rn.astype(BF16)
            da = jnp.where(amask, _bdot(dob, vb, 2, 2), 0.0).astype(BF16)
            dvv = _bdot(a, dob, 1, 1) + _bdot(kendb, rnb, 2, 2)
            dqe = _bdot(da, keb, 2, 1) + _bdot(dob, st.astype(BF16), 2, 1)
            dke = _bdot(da, qeb, 1, 1)
            dkend = _bdot(vb, rnb, 2, 1)
            ddec = jnp.sum(rn * st, axis=1, keepdims=True)
            dbl = ddec * jnp.exp(bl) + jnp.sum(dkend * kend, axis=1, keepdims=True)
            db = dqe * qe - dke * ke - dkend * kend
            dg = _bdot(tri, db, 1, 1, HI) + dbl
            dqv = (dqe * eb * (dk ** -0.5)).reshape(ns * C, dk)
            dkv = (dke * enb + dkend * eend).reshape(ns * C, dk)
            if rev:
                dq_ref[0, tok, :] += dqv
                dk_ref[0, tok, :] += dkv
                dv_ref[0, tok, :] += dvv.reshape(ns * C, dv)
            else:
                dq_ref[0, tok, :] = dqv
                dk_ref[0, tok, :] = dkv
                dv_ref[0, tok, :] = dvv.reshape(ns * C, dv)
            dg_ref[rev, 0, tok, :] = dg.reshape(ns * C, dk)
            return c

        lax.fori_loop(0, nseg, second, 0)

    def body(*refs):
        one_scan(0, *refs)
        one_scan(1, *refs)

    sk = pl.BlockSpec((1, T, dk), lambda h: (h, 0, 0), pipeline_mode=pl.Buffered(1))
    sv = pl.BlockSpec((1, T, dv), lambda h: (h, 0, 0), pipeline_mode=pl.Buffered(1))
    sg = pl.BlockSpec((2, 1, T, dk), lambda h: (0, h, 0, 0), pipeline_mode=pl.Buffered(1))
    st_shape = pltpu.VMEM((n, dv, dk), F32)
    return pl.pallas_call(
        body, name=name, grid=(H,), in_specs=[sk, sk, sv, sg, sv], out_specs=[sk, sk, sv, sg],
        out_shape=[SDS((H, T, dk), F32), SDS((H, T, dk), F32), SDS((H, T, dv), F32), SDS((2, H, T, dk), F32)],
        scratch_shapes=[st_shape, pltpu.VMEM((n, 1, dk), F32), st_shape, st_shape, st_shape],
        compiler_params=_params(1),
    )(q, k, v, g, do)


def _rope_tables(T):
    half = MLA_ROPE // 2
    inv = ROPE_THETA ** (-jnp.arange(half, dtype=F32) / half)
    ang = jnp.arange(T, dtype=F32)[:, None] * inv[None, :]
    cos, sin = jnp.cos(ang), jnp.sin(ang)
    ctab = jnp.concatenate([jnp.ones((T, MLA_NOPE), F32), cos, cos], axis=1)
    stab = jnp.concatenate([jnp.zeros((T, MLA_NOPE), F32), -sin, sin], axis=1)
    idx = np.arange(MLA_QK)
    swap = np.where(idx < MLA_NOPE, idx, np.where(idx < MLA_NOPE + half, idx + half, idx - half))
    perm = (idx[:, None] == swap[None, :]).astype(np.float32)
    return ctab, stab, jnp.asarray(perm)


def _mla_prep_fwd(x, g, ctab, stab, perm, name):
    H, T, d = x.shape
    tm = _tile(T, 1024)

    def body(x_ref, g_ref, c_ref, s_ref, p_ref, o_ref):
        xv = x_ref[0]
        xn = xv * _rstd(xv) * g_ref[...]
        o_ref[0] = (xn * c_ref[...] + _dot(xn, p_ref[...], precision=HI) * s_ref[...]).astype(BF16)

    tab = pl.BlockSpec((tm, d), lambda h, i: (i, 0))
    return pl.pallas_call(
        body, name=name, grid=(H, T // tm),
        in_specs=[pl.BlockSpec((1, tm, d), lambda h, i: (h, i, 0)), pl.BlockSpec((1, d), lambda h, i: (0, 0)), tab, tab,
                  pl.BlockSpec((d, d), lambda h, i: (0, 0))],
        out_specs=pl.BlockSpec((1, tm, d), lambda h, i: (h, i, 0)),
        out_shape=SDS((H, T, d), BF16),
        compiler_params=_params(2),
    )(x, g, ctab, stab, perm)


def _mla_prep_bwd(x, g, ctab, stab, perm, dy, name):
    H, T, d = x.shape
    tm = _tile(T, 1024)

    def body(x_ref, g_ref, c_ref, s_ref, p_ref, dy_ref, dx_ref, dg_ref):
        @pl.when((pl.program_id(0) == 0) & (pl.program_id(1) == 0))
        def _():
            dg_ref[...] = jnp.zeros_like(dg_ref)

        xv = x_ref[0]
        dyv = dy_ref[0]
        dxn = dyv * c_ref[...] + _dot(dyv * s_ref[...], p_ref[...], precision=HI)
        dx, dg = _rms_bwd(dxn, xv, g_ref[...], _rstd(xv))
        dx_ref[0] = dx
        dg_ref[...] += dg

    tab = pl.BlockSpec((tm, d), lambda h, i: (i, 0))
    blk = pl.BlockSpec((1, tm, d), lambda h, i: (h, i, 0))
    gs = pl.BlockSpec((1, d), lambda h, i: (0, 0))
    return pl.pallas_call(
        body, name=name, grid=(H, T // tm),
        in_specs=[blk, gs, tab, tab, pl.BlockSpec((d, d), lambda h, i: (0, 0)), blk],
        out_specs=[blk, gs], out_shape=[SDS((H, T, d), F32), SDS((1, d), F32)],
        compiler_params=_params(2),
    )(x, g, ctab, stab, perm, dy)


def _mla_attn_fwd(q, k, v, name, exch=None):
    H, T, d = q.shape
    dv = v.shape[-1]
    tq = _tile(T, 256)
    scale = d ** -0.5

    def body(q_ref, k_ref, v_ref, o_ref):
        s = _dot(q_ref[0], k_ref[0], NT) * scale
        p = jnp.exp(s - jnp.max(s, axis=-1, keepdims=True))
        p = p / jnp.sum(p, axis=-1, keepdims=True)
        o_ref[0] = _dot(p.astype(BF16), v_ref[0])

    return _call(
        body, name=name, grid=(H, T // tq),
        in_specs=[pl.BlockSpec((1, tq, d), lambda h, i: (h, i, 0)), pl.BlockSpec((1, T, d), lambda h, i: (h, 0, 0)),
                  pl.BlockSpec((1, T, dv), lambda h, i: (h, 0, 0))],
        out_specs=[pl.BlockSpec((1, tq, dv), lambda h, i: (h, i, 0))],
        out_shape=[SDS((H, T, dv), F32)],
        args=(q, k, v), exch=exch)


def _mla_attn_bwd(q, k, v, do, name, exch=None):
    H, T, d = q.shape
    dv = v.shape[-1]
    tq = _tile(T, 256)
    scale = d ** -0.5

    def body(q_ref, k_ref, v_ref, do_ref, dq_ref, dk_ref, dv_ref):
        @pl.when(pl.program_id(1) == 0)
        def _():
            dk_ref[...] = jnp.zeros_like(dk_ref)
            dv_ref[...] = jnp.zeros_like(dv_ref)

        qv, kv = q_ref[0], k_ref[0]
        dob = do_ref[0].astype(BF16)
        st = _dot(kv, qv, NT) * scale
        pt = jnp.exp(st - jnp.max(st, axis=0, keepdims=True))
        pt = pt / jnp.sum(pt, axis=0, keepdims=True)
        dpt = _dot(v_ref[0], dob, NT)
        dst = (pt * (dpt - jnp.sum(pt * dpt, axis=0, keepdims=True)) * scale).astype(BF16)
        dv_ref[0] += _dot(pt.astype(BF16), dob)
        dk_ref[0] += _dot(dst, qv)
        dq_ref[0] = _dot(dst, kv, TN)

    qb = pl.BlockSpec((1, tq, d), lambda h, i: (h, i, 0))
    kb = pl.BlockSpec((1, T, d), lambda h, i: (h, 0, 0))
    vb = pl.BlockSpec((1, T, dv), lambda h, i: (h, 0, 0))
    return _call(
        body, name=name, grid=(H, T // tq),
        in_specs=[qb, kb, vb, pl.BlockSpec((1, tq, dv), lambda h, i: (h, i, 0))],
        out_specs=[qb, kb, vb],
        out_shape=[SDS((H, T, d), F32), SDS((H, T, d), F32), SDS((H, T, dv), F32)],
        args=(q, k, v, do), exch=exch)


def _silu_parts(x):
    s = _sigmoid(x)
    return x * s, s * (1.0 + x * (1.0 - s))


def _mix_out_fwd(x1, y_na, o_gla, y_mla, z, gr_col, gnorm, wbr, w_out, name):
    T, D = x1.shape
    W = y_na.shape[1]
    tm = _tile(T, 512)

    def body(x_ref, na_ref, gla_ref, mla_ref, gt_ref, gr_ref, gn_ref, w0, w1, w2, wo_ref, o_ref):
        og = gla_ref[...]
        gn = gn_ref[...]
        nrm = jnp.concatenate([og[:, h * GLA_DV:(h + 1) * GLA_DV] * _rstd(og[:, h * GLA_DV:(h + 1) * GLA_DV]) * gn
                               for h in range(GLA_HEADS)], axis=1)
        y_gla = nrm * _silu_parts(gr_ref[...])[0]
        mixed = jnp.zeros((tm, D), F32)
        for i, (y, w_ref) in enumerate(((na_ref[...], w0), (y_gla, w1), (mla_ref[...], w2))):
            mixed = mixed + _sigmoid(gt_ref[:, i * D:(i + 1) * D]) * _dot(y.astype(BF16), w_ref[...])
        o_ref[...] = x_ref[...] + _dot(mixed.astype(BF16), wo_ref[...])

    tok = lambda w: pl.BlockSpec((tm, w), lambda i: (i, 0))
    full = lambda shape: pl.BlockSpec(shape, lambda i: (0, 0))
    return pl.pallas_call(
        body, name=name, grid=(T // tm,),
        in_specs=[tok(D), tok(W), tok(W), tok(W), tok(3 * D), pl.BlockSpec((tm, W), lambda i: (i, gr_col)),
                  full((1, GLA_DV)), full((W, D)), full((W, D)), full((W, D)), full((D, D))],
        out_specs=tok(D), out_shape=SDS((T, D), F32),
        compiler_params=_params(1),
    )(x1, y_na, o_gla, y_mla, z, z, gnorm, wbr[0], wbr[1], wbr[2], w_out)


def _mix_out_bwd(dx2, y_na, o_gla, y_mla, z, gr_col, gnorm, wbr, w_out, name):
    T, D = dx2.shape
    W = y_na.shape[1]
    tm = _tile(T, 256)

    def body(dx_ref, na_ref, gla_ref, mla_ref, gt_ref, gr_ref, gn_ref, w0, w1, w2, wo_ref,
             dna_ref, dgla_ref, dmla_ref, dgr_ref, dgt_ref, dw0, dw1, dw2, dwo_ref, dgn_ref):
        @pl.when(pl.program_id(0) == 0)
        def _():
            for r in (dw0, dw1, dw2, dwo_ref, dgn_ref):
                r[...] = jnp.zeros_like(r)

        og = gla_ref[...]
        gn = gn_ref[...]
        grv = gr_ref[...]
        sil, dsil = _silu_parts(grv)
        rs = [_rstd(og[:, h * GLA_DV:(h + 1) * GLA_DV]) for h in range(GLA_HEADS)]
        nrm = jnp.concatenate([og[:, h * GLA_DV:(h + 1) * GLA_DV] * rs[h] * gn for h in range(GLA_HEADS)], axis=1)
        ys = (na_ref[...].astype(BF16), (nrm * sil).astype(BF16), mla_ref[...].astype(BF16))
        dxb = dx_ref[...].astype(BF16)
        dmixed = _dot(dxb, wo_ref[...], NT)
        mixed = jnp.zeros((tm, D), F32)
        dys = []
        for i, (y, w_ref, dw_ref) in enumerate(zip(ys, (w0, w1, w2), (dw0, dw1, dw2))):
            gt = _sigmoid(gt_ref[:, i * D:(i + 1) * D])
            p = _dot(y, w_ref[...])
            mixed = mixed + gt * p
            dp = (dmixed * gt).astype(BF16)
            dgt_ref[:, i * D:(i + 1) * D] = (dmixed * p * gt * (1.0 - gt)).astype(BF16)
            dys.append(_dot(dp, w_ref[...], NT))
            dw_ref[...] += _dot(y, dp, TN)
        dwo_ref[...] += _dot(mixed.astype(BF16), dxb, TN)
        dna_ref[...] = dys[0]
        dmla_ref[...] = dys[2]
        dgr_ref[...] = dys[1] * nrm * dsil
        dn = dys[1] * sil
        dgn = jnp.zeros((1, GLA_DV), F32)
        for h in range(GLA_HEADS):
            sl = slice(h * GLA_DV, (h + 1) * GLA_DV)
            dxh, dgh = _rms_bwd(dn[:, sl], og[:, sl], gn, rs[h])
            dgla_ref[:, sl] = dxh
            dgn = dgn + dgh
        dgn_ref[...] += dgn

    tok = lambda w: pl.BlockSpec((tm, w), lambda i: (i, 0))
    full = lambda shape: pl.BlockSpec(shape, lambda i: (0, 0))
    return pl.pallas_call(
        body, name=name, grid=(T // tm,),
        in_specs=[tok(D), tok(W), tok(W), tok(W), tok(3 * D), pl.BlockSpec((tm, W), lambda i: (i, gr_col)),
                  full((1, GLA_DV)), full((W, D)), full((W, D)), full((W, D)), full((D, D))],
        out_specs=[tok(W), tok(W), tok(W), tok(W), tok(3 * D), full((W, D)), full((W, D)), full((W, D)), full((D, D)),
                   full((1, GLA_DV))],
        out_shape=[SDS((T, W), F32)] * 4 + [SDS((T, 3 * D), BF16)] + [SDS((W, D), F32)] * 3 + [SDS((D, D), F32),
                                                                                              SDS((1, GLA_DV), F32)],
        compiler_params=_params(1),
    )(dx2, y_na, o_gla, y_mla, z, z, gnorm, wbr[0], wbr[1], wbr[2], w_out)


def _loss_head(y, target, name):
    T, D = y.shape
    tm = _tile(T, 1024)

    def body(y_ref, t_ref, dy_ref, l_ref):
        @pl.when(pl.program_id(0) == 0)
        def _():
            l_ref[...] = jnp.zeros_like(l_ref)

        e = y_ref[...] - t_ref[...]
        dy_ref[...] = e * (1.0 / D)
        l_ref[...] += 0.5 * jnp.sum(jnp.mean(e * e, axis=-1, keepdims=True), axis=0, keepdims=True)

    tok = pl.BlockSpec((tm, D), lambda i: (i, 0))
    return pl.pallas_call(
        body, name=name, grid=(T // tm,), in_specs=[tok, tok],
        out_specs=[tok, pl.BlockSpec((1, 1), lambda i: (0, 0))],
        out_shape=[SDS((T, D), F32), SDS((1, 1), F32)],
        compiler_params=_params(1),
    )(y, target)


def _exchange_copies(src, dst, send_sems, recv_sems, local_sems, broadcast, with_recvs=True):
    n = len(src)
    x, y, c = lax.axis_index("x"), lax.axis_index("y"), lax.axis_index("c")
    me = 4 * x + 2 * y + c
    local = [pltpu.make_async_copy(src[a] if broadcast else src[a].at[me], dst[a].at[me], local_sems.at[a])
             for a in range(n)]
    sends, recvs = [], []
    for d in range(1, N_DEV):
        px = 1 - x if d & 4 else x
        py = 1 - y if d & 2 else y
        pc = 1 - c if d & 1 else c
        peer = 4 * px + 2 * py + pc
        for a in range(n):
            for out, slot in ((sends, me), (recvs, peer)) if with_recvs else ((sends, me),):
                out.append(pltpu.make_async_remote_copy(
                    src_ref=src[a] if broadcast else src[a].at[peer], dst_ref=dst[a].at[slot],
                    send_sem=send_sems.at[a, d - 1], recv_sem=recv_sems.at[a, d - 1],
                    device_id=(px, py, pc), device_id_type=pl.DeviceIdType.MESH))
    return local, sends, recvs


def _exchange_start(*refs_and_mode):
    local, sends, _ = _exchange_copies(*refs_and_mode, with_recvs=False)
    for cp in local + sends:
        cp.start()


def _exchange_wait(*refs_and_mode):
    local, sends, recvs = _exchange_copies(*refs_and_mode)
    for cp in recvs:
        cp.wait_recv()
    for cp in sends:
        cp.wait_send()
    for cp in local:
        cp.wait()


def _exchange_shapes(srcs, broadcast):
    n = len(srcs)
    out_shape = [SDS((N_DEV,) + (tuple(s.shape) if broadcast else tuple(s.shape[1:])), s.dtype) for s in srcs]
    sems = [pltpu.SemaphoreType.DMA((n, N_DEV - 1)), pltpu.SemaphoreType.DMA((n, N_DEV - 1)),
            pltpu.SemaphoreType.DMA((n,))]
    return out_shape, sems


def _exchange(srcs, broadcast, name):
    n = len(srcs)
    out_shape, sems = _exchange_shapes(srcs, broadcast)

    def body(*refs):
        mode = (refs[:n], refs[n:2 * n]) + tuple(refs[2 * n:]) + (broadcast,)
        _exchange_start(*mode)
        _exchange_wait(*mode)

    hbm = pl.BlockSpec(memory_space=pltpu.HBM)
    return pl.pallas_call(body, name=name, in_specs=[hbm] * n, out_specs=[hbm] * n, out_shape=out_shape,
                          scratch_shapes=sems)(*srcs)


def _call(body, *, name, grid, in_specs, out_specs, out_shape, args, scratch_shapes=(), exch=None):
    params = _params(len(grid))
    if exch is None:
        return pl.pallas_call(body, name=name, grid=grid, in_specs=in_specs, out_specs=out_specs, out_shape=out_shape,
                              scratch_shapes=list(scratch_shapes), compiler_params=params)(*args)
    srcs, broadcast = exch
    n, n_in, n_out, n_scr = len(srcs), len(args), len(out_shape), len(scratch_shapes)
    x_shape, sems = _exchange_shapes(srcs, broadcast)

    def carrier(*refs):
        ins, x_src = refs[:n_in], refs[n_in:n_in + n]
        outs, x_dst = refs[n_in + n:n_in + n + n_out], refs[n_in + n + n_out:n_in + 2 * n + n_out]
        scr = refs[n_in + 2 * n + n_out:n_in + 2 * n + n_out + n_scr]
        mode = (x_src, x_dst) + tuple(refs[n_in + 2 * n + n_out + n_scr:]) + (broadcast,)
        ids = [pl.program_id(a) for a in range(len(grid))]
        first = functools.reduce(jnp.logical_and, [i == 0 for i in ids])
        last = functools.reduce(jnp.logical_and, [i == g - 1 for i, g in zip(ids, grid)])
        pl.when(first)(lambda: _exchange_start(*mode))
        body(*ins, *outs, *scr)
        pl.when(last)(lambda: _exchange_wait(*mode))

    hbm = pl.BlockSpec(memory_space=pltpu.HBM)
    res = pl.pallas_call(carrier, name=name, grid=grid, in_specs=list(in_specs) + [hbm] * n,
                         out_specs=list(out_specs) + [hbm] * n, out_shape=list(out_shape) + x_shape,
                         scratch_shapes=list(scratch_shapes) + sems, compiler_params=params)(*args, *srcs)
    return res[:n_out], res[n_out:]


def _adam_math(w, g, m, v):
    m = ADAM_B1 * m + (1.0 - ADAM_B1) * g
    v = ADAM_B2 * v + (1.0 - ADAM_B2) * (g * g)
    m_hat = m / (1.0 - ADAM_B1 ** ADAM_STEP)
    v_hat = v / (1.0 - ADAM_B2 ** ADAM_STEP)
    delta = -ADAM_LR * (m_hat / (jnp.sqrt(v_hat) + ADAM_EPS) + ADAM_WD * w)
    return delta, m, v


def _adamw_sharded(landed, k, w, m, v, name):
    L, r, c = w.shape
    lanes = -(-c // LANE) * LANE
    rb = _tile(r, max(16, (1 << 20) // (N_DEV * lanes * 2)), 16)

    def body(*refs):
        l_refs = refs[:L]
        w_ref, m_ref, v_ref, g_ref, d_ref, nm_ref, nv_ref = refs[L:]
        for j in range(L):
            @pl.when(pl.program_id(0) == j)
            def _(j=j):
                g = l_refs[j][0, 0].astype(F32)
                for s in range(1, N_DEV):
                    g = g + l_refs[j][s, 0].astype(F32)
                delta, nm, nv = _adam_math(w_ref[0], g, m_ref[0], v_ref[0])
                g_ref[0] = g
                d_ref[0] = delta
                nm_ref[0] = nm
                nv_ref[0] = nv

    blk = pl.BlockSpec((1, rb, c), lambda l, i: (l, i, 0))
    land = [pl.BlockSpec((N_DEV, 1, rb, c), lambda l, i, j=j: (0, k, jnp.where(l == j, i, 0), 0)) for j in range(L)]
    return pl.pallas_call(
        body, name=name, grid=(L, r // rb),
        in_specs=land + [blk, blk, blk],
        out_specs=[blk] * 4, out_shape=[SDS((L, r, c), F32)] * 4,
        compiler_params=_params(2),
    )(*landed, w, m, v)


def _adamw_replicated(landed, w, m, v, name):
    R = w.shape[0]

    def body(l_ref, w_ref, m_ref, v_ref, g_ref, d_ref, nm_ref, nv_ref):
        g = l_ref[0]
        for s in range(1, N_DEV):
            g = g + l_ref[s]
        delta, nm, nv = _adam_math(w_ref[...], g, m_ref[...], v_ref[...])
        g_ref[...] = g
        d_ref[...] = delta
        nm_ref[...] = nm
        nv_ref[...] = nv

    blk = pl.BlockSpec((R, LANE), lambda i: (0, 0))
    return pl.pallas_call(
        body, name=name, grid=(1,),
        in_specs=[pl.BlockSpec((N_DEV, R, LANE), lambda i: (0, 0, 0)), blk, blk, blk],
        out_specs=[blk] * 4, out_shape=[SDS((R, LANE), F32)] * 4,
        compiler_params=_params(1),
    )(landed, w, m, v)


def _heads(x, h):
    T = x.shape[0]
    return jnp.transpose(x.reshape(T, h, -1), (1, 0, 2))


def _unheads(x):
    h, T, d = x.shape
    return jnp.transpose(x, (1, 0, 2)).reshape(T, h * d)


def _pack(parts):
    flat = jnp.concatenate([p.reshape(-1) for p in parts])
    rows = -(-flat.shape[0] // (8 * LANE)) * 8
    return jnp.pad(flat, (0, rows * LANE - flat.shape[0])).reshape(rows, LANE)


def _unpack(packed, shapes):
    flat = packed.reshape(-1)
    out, off = [], 0
    for s in shapes:
        size = int(np.prod(s))
        out.append(flat[off:off + size].reshape(s))
        off += size
    return out


def _layer_fwd(x0, w, l, lay, tabs, carry, got):
    T, D = x0.shape
    ctab, stab, perm, na_bias = tabs
    col = lambda name: lay[name][0] // lay[name][1]
    sv = {'x0': x0}
    x1, sv['a1'], sv['b1'] = _carried(_ffn_fwd, carry, got, 'ffn1', x0, w['ffn1_norm'], w['ffn1_w1'], w['ffn1_w3'],
                                      w['ffn1_w2'], f"ffn1_fwd_{l}")
    z, sv['hb'] = _carried(_norm_matmul, carry, got, 'mix_in', x1, 0, w['mix_norm'], w['w_in_z'], f"mix_in_{l}")
    seg = lambda name: z[:, lay[name][0]:lay[name][0] + lay[name][3]]
    sv['na_q'], sv['na_k'], sv['na_v'] = (_heads(seg(n), NA_HEADS) for n in ('na_q', 'na_k', 'na_v'))
    y_na = _unheads(_carried(_na_fwd, carry, got, 'na', sv['na_q'], sv['na_k'], sv['na_v'], na_bias, w['na_q_norm'],
                             w['na_k_norm'], f"na_fwd_{l}")[0])
    gf, gb = _gla_gate_fwd(z, col('glow'), w['up_f'], w['up_b'], w['gla_gf_bias'], w['gla_gb_bias'], f"gla_gate_{l}")
    sv['gq'], sv['gk'], sv['gv'] = (_heads(seg(n), GLA_HEADS) for n in ('gq', 'gk', 'gv'))
    sv['gg'] = jnp.stack([_heads(gf, GLA_HEADS), _heads(gb, GLA_HEADS)])
    o_gla = _unheads(_gla_fwd(sv['gq'], sv['gk'], sv['gv'], sv['gg'], f"gla_fwd_{l}"))
    q_raw, sv['hq'] = _norm_matmul(z, col('c_q'), w['mla_cq_norm'], w['mla_w_uq'], f"mla_uq_{l}")
    kv_raw, sv['hkv'] = _norm_matmul(z, col('c_kv'), w['mla_ckv_norm'], w['mla_w_ukv'], f"mla_ukv_{l}")
    kv_h = _heads(kv_raw, MLA_HEADS)
    k_rope = jnp.broadcast_to(seg('k_rope')[None], (MLA_HEADS, T, MLA_ROPE))
    sv['mq_raw'] = _heads(q_raw, MLA_HEADS)
    sv['mk_raw'] = jnp.concatenate([kv_h[..., :MLA_NOPE], k_rope], axis=-1)
    sv['mq'] = _mla_prep_fwd(sv['mq_raw'], w['mla_q_norm'], ctab, stab, perm, f"mla_qprep_{l}")
    sv['mk'] = _mla_prep_fwd(sv['mk_raw'], w['mla_k_norm'], ctab, stab, perm, f"mla_kprep_{l}")
    sv['mv'] = kv_h[..., MLA_NOPE:].astype(BF16)
    y_mla = _unheads(_carried(_mla_attn_fwd, carry, got, 'mla', sv['mq'], sv['mk'], sv['mv'], f"mla_attn_{l}")[0])
    x2 = _mix_out_fwd(x1, y_na, o_gla, y_mla, z, col('gr'), w['gla_out_norm'], w['w_br'], w['w_out'], f"mix_out_{l}")
    sv.update(x1=x1, z=z, y_na=y_na, o_gla=o_gla, y_mla=y_mla, x2=x2)
    x3, sv['a2'], sv['b2'] = _carried(_ffn_fwd, carry, got, 'ffn2', x2, w['ffn2_norm'], w['ffn2_w1'], w['ffn2_w3'],
                                      w['ffn2_w2'], f"ffn2_fwd_{l}")
    return x3, sv


def _ffn_grads(x, dy, a, b, w, pre, l, carry, got):
    dx, da, db, u, hb, dyb, dg = _carried(_ffn_bwd, carry, got, pre, x, w[pre + '_norm'], dy, a, b, w[pre + '_w1'],
                                          w[pre + '_w3'], w[pre + '_w2'], f"{pre}_bwd_{l}")
    grads = {pre + '_norm': dg,
             pre + '_w1': _matmul_tn(hb, da, f"{pre}_dw1_{l}"),
             pre + '_w3': _matmul_tn(hb, db, f"{pre}_dw3_{l}"),
             pre + '_w2': _matmul_tn(u, dyb, f"{pre}_dw2_{l}")}
    return dx, grads


def _layer_bwd(dx3, sv, w, l, lay, zp, tabs, carry, got):
    T, D = dx3.shape
    ctab, stab, perm, na_bias, onehot = tabs
    col = lambda name: lay[name][0] // lay[name][1]
    z = sv['z']
    dx2, grads = _ffn_grads(sv['x2'], dx3, sv['a2'], sv['b2'], w, 'ffn2', l, carry, got)
    (dy_na, do_gla, dy_mla, dgr, dgates, dw0, dw1, dw2, dwo, dgn) = _mix_out_bwd(
        dx2, sv['y_na'], sv['o_gla'], sv['y_mla'], z, col('gr'), w['gla_out_norm'], w['w_br'], w['w_out'], f"mix_out_bwd_{l}")
    grads.update(w_br_na=dw0, w_br_gla=dw1, w_br_mla=dw2, w_out=dwo, gla_out_norm=dgn)
    dq, dk, dv, dbias, dgq, dgk = _na_bwd(sv['na_q'], sv['na_k'], sv['na_v'], _heads(dy_na, NA_HEADS), na_bias,
                                          w['na_q_norm'], w['na_k_norm'], f"na_bwd_{l}")
    grads.update(na_q_norm=dgq, na_k_norm=dgk, na_rpb=_na_bias_tables_bwd(dbias, onehot))
    gdq, gdk, gdv, gdg = _gla_bwd(sv['gq'], sv['gk'], sv['gv'], sv['gg'], _heads(do_gla, GLA_HEADS), f"gla_bwd_{l}")
    fold = _unheads
    dglow, dupf, dupb, dbf, dbb = _gla_gate_bwd(z, col('glow'), w['up_f'], w['up_b'], w['gla_gf_bias'], w['gla_gb_bias'],
                                                _unheads(gdg[0]), _unheads(gdg[1]), f"gla_gate_bwd_{l}")
    grads.update(gla_gf_up=dupf[:GLA_RANK], gla_gb_up=dupb[GLA_RANK:2 * GLA_RANK], gla_gf_bias=dbf, gla_gb_bias=dbb)
    mdq, mdk, mdv = _carried(_mla_attn_bwd, carry, got, 'mla', sv['mq'], sv['mk'], sv['mv'], _heads(dy_mla, MLA_HEADS),
                             f"mla_attn_bwd_{l}")
    dq_raw, dgqn = _mla_prep_bwd(sv['mq_raw'], w['mla_q_norm'], ctab, stab, perm, mdq, f"mla_qprep_bwd_{l}")
    dk_raw, dgkn = _mla_prep_bwd(sv['mk_raw'], w['mla_k_norm'], ctab, stab, perm, mdk, f"mla_kprep_bwd_{l}")
    dq_raw = _unheads(dq_raw).astype(BF16)
    dkv_raw = _unheads(jnp.concatenate([dk_raw[..., :MLA_NOPE], mdv], axis=-1)).astype(BF16)
    dk_rope = jnp.sum(dk_raw[..., MLA_NOPE:], axis=0)
    dcq, dgcq = _norm_matmul_bwd(dq_raw, w['mla_w_uq'], z, col('c_q'), w['mla_cq_norm'], None, f"mla_uq_bwd_{l}")
    dckv, dgckv = _norm_matmul_bwd(dkv_raw, w['mla_w_ukv'], z, col('c_kv'), w['mla_ckv_norm'], None, f"mla_ukv_bwd_{l}")
    grads.update(mla_q_norm=dgqn, mla_k_norm=dgkn, mla_cq_norm=dgcq, mla_ckv_norm=dgckv,
                 mla_w_uq=_matmul_tn(sv['hq'], dq_raw, f"mla_duq_{l}", F32),
                 mla_w_ukv=_matmul_tn(sv['hkv'], dkv_raw, f"mla_dukv_{l}", F32))
    pieces = {'gates': dgates, 'na_q': _unheads(dq), 'na_k': _unheads(dk), 'na_v': _unheads(dv), 'gv': fold(gdv), 'gr': dgr,
              'gq': fold(gdq), 'gk': fold(gdk), 'c_q': dcq, 'c_kv': dckv, 'glow': dglow,
              'k_rope': jnp.pad(dk_rope, ((0, 0), (0, LANE - MLA_ROPE)))}
    dz = jnp.concatenate([pieces[name].astype(BF16) for name in lay], axis=1)
    dx1, dgmix = _norm_matmul_bwd(dz, w['w_in_z'], sv['x1'], 0, w['mix_norm'], dx2, f"mix_in_bwd_{l}")
    grads.update(mix_norm=dgmix, w_in_z=_matmul_tn(sv['hb'], dz, f"mix_dw_in_{l}"))
    dx0, g1 = _ffn_grads(sv['x0'], dx1, sv['a1'], sv['b1'], w, 'ffn1', l, carry, got)
    grads.update(g1)
    return dx0, grads


EXCHANGE_PARTS = {
    'A1': [['ffn1_w1', 'ffn1_w3']],
    'A2': [['ffn2_w1', 'ffn2_w3']],
    'B': [['ffn1_w2', 'ffn2_w2']],
    'C': [['w_in']],
    'S': [['w_br_na', 'w_br_gla', 'w_br_mla'], ['w_out'], ['mla_w_uq'], ['mla_w_ukv'], ['gla_gf_up', 'gla_gb_up']],
}
FWD_CARRIERS = {'ffn1': ['B'], 'mix_in': ['S'], 'na': ['A2'], 'mla': ['C'], 'ffn2': ['A1']}
BWD_CARRIERS = {'mla': ['A1', 'A2', 'B'], 'ffn2': ['C'], 'ffn1': ['S']}


def _carried(fn, carry, got, key, *args):
    if key in carry:
        outs, got[key] = fn(*args, exch=carry[key])
        return outs
    return fn(*args)


def kernel(x, ffn1_norm, ffn1_w1, ffn1_w3, ffn1_w2, mix_norm, w_in, na_q_norm, na_k_norm, na_rpb, gla_gf_up, gla_gf_bias, gla_gb_up, gla_gb_bias, gla_out_norm, mla_cq_norm, mla_ckv_norm, mla_w_uq, mla_w_ukv, mla_q_norm, mla_k_norm, w_br_na, w_br_gla, w_br_mla, w_out, ffn2_norm, ffn2_w1, ffn2_w3, ffn2_w2, loss_target, m_ffn1_norm, m_ffn1_w1, m_ffn1_w3, m_ffn1_w2, m_mix_norm, m_w_in, m_na_q_norm, m_na_k_norm, m_na_rpb, m_gla_gf_up, m_gla_gf_bias, m_gla_gb_up, m_gla_gb_bias, m_gla_out_norm, m_mla_cq_norm, m_mla_ckv_norm, m_mla_w_uq, m_mla_w_ukv, m_mla_q_norm, m_mla_k_norm, m_w_br_na, m_w_br_gla, m_w_br_mla, m_w_out, m_ffn2_norm, m_ffn2_w1, m_ffn2_w3, m_ffn2_w2, v_ffn1_norm, v_ffn1_w1, v_ffn1_w3, v_ffn1_w2, v_mix_norm, v_w_in, v_na_q_norm, v_na_k_norm, v_na_rpb, v_gla_gf_up, v_gla_gf_bias, v_gla_gb_up, v_gla_gb_bias, v_gla_out_norm, v_mla_cq_norm, v_mla_ckv_norm, v_mla_w_uq, v_mla_w_ukv, v_mla_q_norm, v_mla_k_norm, v_w_br_na, v_w_br_gla, v_w_br_mla, v_w_out, v_ffn2_norm, v_ffn2_w1, v_ffn2_w3, v_ffn2_w2):
    given = dict(locals())
    P = {n: given[n] for n in WEIGHTS}
    M = {n: given['m_' + n] for n in WEIGHTS}
    V = {n: given['v_' + n] for n in WEIGHTS}
    xs = x[0]
    T, D = xs.shape
    L = ffn1_norm.shape[0]
    lay, zp, d_in = _z_layout(D)

    def arrays_of(parts):
        return [names for p in parts for names in EXCHANGE_PARTS[p]]

    def gather_srcs(l, parts):
        return [jnp.stack([P[n][l].astype(BF16) for n in names]) for names in arrays_of(parts)]

    full = [dict() for _ in range(L)]

    def take_gathered(l, parts, res):
        for names, g in zip(arrays_of(parts), res):
            for k, n in enumerate(names):
                blk = g[:, k]
                if n in COL_SHARDED:
                    full[l][n] = jnp.transpose(blk, (1, 0, 2)).reshape(blk.shape[1], N_DEV * blk.shape[2])
                else:
                    full[l][n] = blk.reshape(N_DEV * blk.shape[1], blk.shape[2])

    ctab, stab, perm = _rope_tables(T)
    na_bias, onehot = _na_bias_tables(na_rpb)

    def layer_weights(l):
        w = {n: full[l][n] for n in SHARDED if n not in ('w_in', 'gla_gf_up', 'gla_gb_up')}
        w['w_in_z'] = _to_z_cols(full[l]['w_in'], lay, zp)
        w['up_f'] = jnp.zeros((LANE, GLA_QK_W), BF16).at[:GLA_RANK].set(full[l]['gla_gf_up'])
        w['up_b'] = jnp.zeros((LANE, GLA_QK_W), BF16).at[GLA_RANK:2 * GLA_RANK].set(full[l]['gla_gb_up'])
        w['w_br'] = (w['w_br_na'], w['w_br_gla'], w['w_br_mla'])
        for n in REPLICATED:
            if n != 'na_rpb':
                w[n] = P[n][l][None, :]
        return w

    every = list(EXCHANGE_PARTS)
    take_gathered(0, every, _exchange(gather_srcs(0, every), True, "gather_weights_0"))
    ws, saved = [], []
    h = xs
    for l in range(L):
        ws.append(layer_weights(l))
        carry = {c: (gather_srcs(l + 1, parts), True) for c, parts in FWD_CARRIERS.items()} if l + 1 < L else {}
        got = {}
        h, sv = _layer_fwd(h, ws[l], l, lay, (ctab, stab, perm, na_bias[l]), carry, got)
        saved.append(sv)
        for c, res in got.items():
            take_gathered(l + 1, FWD_CARRIERS[c], res)
    dh, loss_local = _loss_head(h, loss_target[0], "loss_head")
    loss = lax.psum(loss_local[0, 0], ("x", "y", "c"))

    layer_grads = [None] * L

    def scatter_srcs(l, parts):
        def blocks(n):
            g = layer_grads[l][n]
            if n in COL_SHARDED:
                return jnp.transpose(g.reshape(g.shape[0], N_DEV, -1), (1, 0, 2)).astype(BF16)
            return g.reshape(N_DEV, -1, g.shape[1]).astype(BF16)
        return [jnp.stack([blocks(n) for n in names], axis=1) for names in arrays_of(parts)]

    landed = [dict() for _ in range(L)]

    def take_landed(l, parts, res):
        for names, r in zip(arrays_of(parts), res):
            for k, n in enumerate(names):
                landed[l][n] = (r, k)

    for l in reversed(range(L)):
        carry = {c: (scatter_srcs(l + 1, parts), False) for c, parts in BWD_CARRIERS.items()} if l + 1 < L else {}
        got = {}
        dh, layer_grads[l] = _layer_bwd(dh, saved[l], ws[l], l, lay, zp, (ctab, stab, perm, na_bias[l], onehot), carry, got)
        layer_grads[l]['w_in'] = _from_z_cols(layer_grads[l].pop('w_in_z'), lay, d_in)
        for c, res in got.items():
            take_landed(l + 1, BWD_CARRIERS[c], res)
    grad_x = dh[None]
    take_landed(0, every, _exchange(scatter_srcs(0, every), False, "scatter_grads_0"))

    out = {}
    for n in SHARDED:
        out[n] = _adamw_sharded([landed[l][n][0] for l in range(L)], landed[0][n][1], P[n], M[n], V[n], f"adamw_{n}")

    rep = [n for n in REPLICATED]
    rep_shapes = [tuple(P[n].shape) for n in rep]
    rep_partial = _pack([jnp.stack([layer_grads[l][n].reshape(P[n].shape[1:]) for l in range(L)]) for n in rep])
    (rep_landed,) = _exchange([rep_partial], True, "gather_small_grads")
    packed = _adamw_replicated(rep_landed, _pack([P[n] for n in rep]), _pack([M[n] for n in rep]),
                               _pack([V[n] for n in rep]), "adamw_replicated")
    for kind, pk in enumerate(packed):
        for n, arr in zip(rep, _unpack(pk, rep_shapes)):
            out.setdefault(n, [None] * 4)[kind] = arr

    res = [loss, grad_x]
    for kind in range(4):
        res += [out[n][kind] for n in WEIGHTS]
    return tuple(res)
```

```python
import functools

import numpy as np
import jax
import jax.numpy as jnp
from jax import lax
from jax.experimental import pallas as pl
from jax.experimental.pallas import tpu as pltpu

F32 = jnp.float32
BF16 = jnp.bfloat16
SDS = jax.ShapeDtypeStruct
HI = lax.Precision.HIGHEST

N_DEV = 8
DEPTH = 4
EPS = 1e-6
LANE = 128
VMEM_LIMIT = 56 * 1024 * 1024

GRID_W = 64
NA_HEADS, NA_HD, NA_WIN_R, NA_WIN_C = 8, 64, 8, 16
GLA_HEADS, GLA_DK, GLA_DV, GLA_RANK, GLA_TAU, GLA_CHUNK = 4, 64, 128, 16, 16.0, 64
MLA_HEADS, MLA_QR, MLA_KVR, MLA_NOPE, MLA_ROPE, MLA_V = 4, 256, 256, 128, 64, 128
MLA_QK = MLA_NOPE + MLA_ROPE
ROPE_THETA = 10000.0
NA_W = NA_HEADS * NA_HD
GLA_QK_W = GLA_HEADS * GLA_DK
GLA_V_W = GLA_HEADS * GLA_DV
MLA_V_W = MLA_HEADS * MLA_V

ADAM_LR, ADAM_B1, ADAM_B2, ADAM_EPS, ADAM_WD, ADAM_STEP = 0.001, 0.9, 0.999, 1e-08, 0.01, 10

WEIGHTS = ['ffn1_norm', 'ffn1_w1', 'ffn1_w3', 'ffn1_w2', 'mix_norm', 'w_in', 'na_q_norm', 'na_k_norm', 'na_rpb',
           'gla_gf_up', 'gla_gf_bias', 'gla_gb_up', 'gla_gb_bias', 'gla_out_norm', 'mla_cq_norm', 'mla_ckv_norm',
           'mla_w_uq', 'mla_w_ukv', 'mla_q_norm', 'mla_k_norm', 'w_br_na', 'w_br_gla', 'w_br_mla', 'w_out',
           'ffn2_norm', 'ffn2_w1', 'ffn2_w3', 'ffn2_w2']
COL_SHARDED = ['ffn1_w1', 'ffn1_w3', 'w_in', 'gla_gf_up', 'gla_gb_up', 'mla_w_uq', 'mla_w_ukv', 'w_br_na', 'w_br_gla',
               'w_br_mla', 'ffn2_w1', 'ffn2_w3']
ROW_SHARDED = ['ffn1_w2', 'w_out', 'ffn2_w2']
SHARDED = [n for n in WEIGHTS if n in COL_SHARDED or n in ROW_SHARDED]
REPLICATED = [n for n in WEIGHTS if n not in SHARDED]

NT = (((1,), (1,)), ((), ()))
TN = (((0,), (0,)), ((), ()))


def _tile(n, pref, mult=8):
    best = None
    for t in range(mult, min(n, pref) + 1, mult):
        if n % t == 0:
            best = t
    return best if best is not None else n


def _params(n_axes):
    return pltpu.CompilerParams(dimension_semantics=("arbitrary",) * n_axes, vmem_limit_bytes=VMEM_LIMIT)


def _dot(a, b, dims=None, precision=None):
    if dims is None:
        return jnp.dot(a, b, preferred_element_type=F32, precision=precision)
    return lax.dot_general(a, b, dims, preferred_element_type=F32, precision=precision)


def _sigmoid(x):
    return 1.0 / (1.0 + jnp.exp(-x))


def _rstd(x):
    return lax.rsqrt(jnp.mean(x * x, axis=-1, keepdims=True) + EPS)


def _rms_bwd(dy, x, g, r):
    xh = x * r
    dyg = dy * g
    dx = r * (dyg - xh * jnp.mean(dyg * xh, axis=-1, keepdims=True))
    return dx, jnp.sum(dy * xh, axis=0, keepdims=True)


def _z_layout(d_model):
    own = {}
    off = 0
    for name, w in [('na_q', NA_W), ('na_k', NA_W), ('na_v', NA_W), ('gq', GLA_QK_W), ('gk', GLA_QK_W),
                    ('gv', GLA_V_W), ('gr', GLA_V_W), ('gfl', GLA_RANK), ('gbl', GLA_RANK), ('c_q', MLA_QR),
                    ('c_kv', MLA_KVR), ('k_rope', MLA_ROPE), ('gates', 3 * d_model)]:
        own[name] = (off, w)
        off += w
    order = [('gates', 3 * d_model), ('na_q', NA_W), ('na_k', NA_W), ('na_v', NA_W), ('gv', GLA_V_W), ('gr', GLA_V_W),
             ('gq', GLA_QK_W), ('gk', GLA_QK_W), ('c_q', MLA_QR), ('c_kv', MLA_KVR), ('glow', LANE), ('k_rope', LANE)]
    lay = {}
    z = 0
    for name, w in order:
        assert z % w == 0, (name, z, w)
        if name == 'glow':
            lay[name] = (z, w, own['gfl'][0], 2 * GLA_RANK)
        else:
            lay[name] = (z, w, own[name][0], own[name][1])
        z += w
    return lay, z, off


def _to_z_cols(w, lay, zp):
    parts = []
    for name, (zo, zw, oo, ow) in lay.items():
        parts.append(w[:, oo:oo + ow])
        if zw > ow:
            parts.append(jnp.zeros((w.shape[0], zw - ow), w.dtype))
    return jnp.concatenate(parts, axis=1)


def _from_z_cols(wz, lay, d_in):
    own = sorted(lay.values(), key=lambda t: t[2])
    return jnp.concatenate([wz[:, zo:zo + ow] for (zo, zw, oo, ow) in own], axis=1)


def _ffn_fwd(x, g, w1, w3, w2, name, exch=None):
    T, D = x.shape
    Fd = w1.shape[1]
    tm, tf = _tile(T, 1024), _tile(Fd, 256, LANE)
    nj = Fd // tf

    def body(x_ref, g_ref, w1_ref, w3_ref, w2_ref, o_ref, a_ref, b_ref, h_scr, acc):
        j = pl.program_id(1)

        @pl.when(j == 0)
        def _():
            xv = x_ref[...]
            h_scr[...] = (xv * _rstd(xv) * g_ref[...]).astype(BF16)
            acc[...] = jnp.zeros_like(acc)

        h = h_scr[...]
        a = _dot(h, w1_ref[...])
        b = _dot(h, w3_ref[...])
        a_ref[...] = a.astype(BF16)
        b_ref[...] = b.astype(BF16)
        u = a * _sigmoid(a) * b
        acc[...] += _dot(u.astype(BF16), w2_ref[...])

        @pl.when(j == nj - 1)
        def _():
            o_ref[...] = x_ref[...] + 0.5 * acc[...]

    return _call(
        body, name=name, grid=(T // tm, nj),
        in_specs=[pl.BlockSpec((tm, D), lambda i, j: (i, 0)), pl.BlockSpec((1, D), lambda i, j: (0, 0)),
                  pl.BlockSpec((D, tf), lambda i, j: (0, j)), pl.BlockSpec((D, tf), lambda i, j: (0, j)),
                  pl.BlockSpec((tf, D), lambda i, j: (j, 0))],
        out_specs=[pl.BlockSpec((tm, D), lambda i, j: (i, 0)), pl.BlockSpec((tm, tf), lambda i, j: (i, j)),
                   pl.BlockSpec((tm, tf), lambda i, j: (i, j))],
        out_shape=[SDS((T, D), F32), SDS((T, Fd), BF16), SDS((T, Fd), BF16)],
        scratch_shapes=[pltpu.VMEM((tm, D), BF16), pltpu.VMEM((tm, D), F32)],
        args=(x, g, w1, w3, w2), exch=exch)


def _ffn_bwd(x, g, dy, a, b, w1, w3, w2, name, exch=None):
    T, D = x.shape
    Fd = w1.shape[1]
    tm, tf = _tile(T, 512), _tile(Fd, 256, LANE)
    nj = Fd // tf

    def body(x_ref, g_ref, dy_ref, a_ref, b_ref, w1_ref, w3_ref, w2_ref,
             dx_ref, da_ref, db_ref, u_ref, hb_ref, dyb_ref, dg_ref, dh_acc):
        i, j = pl.program_id(0), pl.program_id(1)

        @pl.when(j == 0)
        def _():
            xv = x_ref[...]
            hb_ref[...] = (xv * _rstd(xv) * g_ref[...]).astype(BF16)
            dyb_ref[...] = (0.5 * dy_ref[...]).astype(BF16)
            dh_acc[...] = jnp.zeros_like(dh_acc)

        @pl.when((i == 0) & (j == 0))
        def _():
            dg_ref[...] = jnp.zeros_like(dg_ref)

        du = _dot(dyb_ref[...], w2_ref[...], NT)
        av = a_ref[...].astype(F32)
        bv = b_ref[...].astype(F32)
        s = _sigmoid(av)
        sl = av * s
        da = (du * bv * (s * (1.0 + av * (1.0 - s)))).astype(BF16)
        db = (du * sl).astype(BF16)
        da_ref[...] = da
        db_ref[...] = db
        u_ref[...] = (sl * bv).astype(BF16)
        dh_acc[...] += _dot(da, w1_ref[...], NT) + _dot(db, w3_ref[...], NT)

        @pl.when(j == nj - 1)
        def _():
            xv = x_ref[...]
            dxn, dg = _rms_bwd(dh_acc[...], xv, g_ref[...], _rstd(xv))
            dx_ref[...] = dy_ref[...] + dxn
            dg_ref[...] += dg

    row = lambda i, j: (i, 0)
    return _call(
        body, name=name, grid=(T // tm, nj),
        in_specs=[pl.BlockSpec((tm, D), row), pl.BlockSpec((1, D), lambda i, j: (0, 0)), pl.BlockSpec((tm, D), row),
                  pl.BlockSpec((tm, tf), lambda i, j: (i, j)), pl.BlockSpec((tm, tf), lambda i, j: (i, j)),
                  pl.BlockSpec((D, tf), lambda i, j: (0, j)), pl.BlockSpec((D, tf), lambda i, j: (0, j)),
                  pl.BlockSpec((tf, D), lambda i, j: (j, 0))],
        out_specs=[pl.BlockSpec((tm, D), row), pl.BlockSpec((tm, tf), lambda i, j: (i, j)),
                   pl.BlockSpec((tm, tf), lambda i, j: (i, j)), pl.BlockSpec((tm, tf), lambda i, j: (i, j)),
                   pl.BlockSpec((tm, D), row), pl.BlockSpec((tm, D), row), pl.BlockSpec((1, D), lambda i, j: (0, 0))],
        out_shape=[SDS((T, D), F32), SDS((T, Fd), BF16), SDS((T, Fd), BF16), SDS((T, Fd), BF16),
                   SDS((T, D), BF16), SDS((T, D), BF16), SDS((1, D), F32)],
        scratch_shapes=[pltpu.VMEM((tm, D), F32)],
        args=(x, g, dy, a, b, w1, w3, w2), exch=exch)


def _matmul_tn(a, b, name, out_dtype=BF16):
    T, K = a.shape
    N = b.shape[1]
    tk = _tile(K, 1408, LANE)
    tn = _tile(N, 1408, LANE)
    tt = _tile(T, 1024)
    nt = T // tt

    def body(a_ref, b_ref, o_ref, acc):
        t = pl.program_id(2)

        @pl.when(t == 0)
        def _():
            acc[...] = jnp.zeros_like(acc)

        acc[...] += _dot(a_ref[...], b_ref[...], TN)

        @pl.when(t == nt - 1)
        def _():
            o_ref[...] = acc[...].astype(out_dtype)

    return pl.pallas_call(
        body, name=name, grid=(K // tk, N // tn, nt),
        in_specs=[pl.BlockSpec((tt, tk), lambda k, n, t: (t, k)), pl.BlockSpec((tt, tn), lambda k, n, t: (t, n))],
        out_specs=pl.BlockSpec((tk, tn), lambda k, n, t: (k, n)),
        out_shape=SDS((K, N), out_dtype),
        scratch_shapes=[pltpu.VMEM((tk, tn), F32)],
        compiler_params=_params(3),
    )(a, b)


def _norm_matmul(x, xcol, g, w, name, exch=None):
    T = x.shape[0]
    D = g.shape[1]
    N = w.shape[1]
    tm, tn = _tile(T, 1024), _tile(N, 768, LANE)

    def body(x_ref, g_ref, w_ref, z_ref, hb_ref):
        @pl.when(pl.program_id(1) == 0)
        def _():
            xv = x_ref[...]
            hb_ref[...] = (xv * _rstd(xv) * g_ref[...]).astype(BF16)

        z_ref[...] = _dot(hb_ref[...], w_ref[...])

    return _call(
        body, name=name, grid=(T // tm, N // tn),
        in_specs=[pl.BlockSpec((tm, D), lambda i, j: (i, xcol)), pl.BlockSpec((1, D), lambda i, j: (0, 0)),
                  pl.BlockSpec((D, tn), lambda i, j: (0, j))],
        out_specs=[pl.BlockSpec((tm, tn), lambda i, j: (i, j)), pl.BlockSpec((tm, D), lambda i, j: (i, 0))],
        out_shape=[SDS((T, N), F32), SDS((T, D), BF16)],
        args=(x, g, w), exch=exch)


def _norm_matmul_bwd(dz, w, x, xcol, g, resid, name, exch=None):
    T, N = dz.shape
    D = g.shape[1]
    tm, tn = _tile(T, 1024), _tile(N, 768, LANE)
    nj = N // tn
    has_resid = resid is not None

    def body(*refs):
        if has_resid:
            dz_ref, w_ref, x_ref, g_ref, r_ref, dx_ref, dg_ref, acc = refs
        else:
            dz_ref, w_ref, x_ref, g_ref, dx_ref, dg_ref, acc = refs
        i, j = pl.program_id(0), pl.program_id(1)

        @pl.when(j == 0)
        def _():
            acc[...] = jnp.zeros_like(acc)

        @pl.when((i == 0) & (j == 0))
        def _():
            dg_ref[...] = jnp.zeros_like(dg_ref)

        acc[...] += _dot(dz_ref[...], w_ref[...], NT)

        @pl.when(j == nj - 1)
        def _():
            xv = x_ref[...]
            dxn, dg = _rms_bwd(acc[...], xv, g_ref[...], _rstd(xv))
            dx_ref[...] = dxn + r_ref[...] if has_resid else dxn
            dg_ref[...] += dg

    in_specs = [pl.BlockSpec((tm, tn), lambda i, j: (i, j)), pl.BlockSpec((D, tn), lambda i, j: (0, j)),
                pl.BlockSpec((tm, D), lambda i, j: (i, xcol)), pl.BlockSpec((1, D), lambda i, j: (0, 0))]
    args = [dz, w, x, g]
    if has_resid:
        in_specs.append(pl.BlockSpec((tm, D), lambda i, j: (i, 0)))
        args.append(resid)
    return _call(
        body, name=name, grid=(T // tm, nj), in_specs=in_specs,
        out_specs=[pl.BlockSpec((tm, D), lambda i, j: (i, 0)), pl.BlockSpec((1, D), lambda i, j: (0, 0))],
        out_shape=[SDS((T, D), F32), SDS((1, D), F32)],
        scratch_shapes=[pltpu.VMEM((tm, D), F32)],
        args=args, exch=exch)


def _na_bias_tables(rpb):
    L = rpb.shape[0]
    qc = np.arange(GRID_W)[:, None]
    kc = np.arange(GRID_W)[None, :]
    dc = np.clip(kc - qc + NA_WIN_C - 1, 0, 2 * NA_WIN_C - 2)
    c0 = np.clip(qc - NA_WIN_C // 2, 0, GRID_W - NA_WIN_C)
    ok = (kc >= c0) & (kc < c0 + NA_WIN_C)
    onehot = (dc.reshape(-1)[None, :] == np.arange(2 * NA_WIN_C - 1)[:, None]).astype(np.float32)
    t1 = jnp.stack([rpb[:, :, c:c + NA_WIN_R, :] for c in range(NA_WIN_R)], axis=2)
    full = jnp.dot(t1.reshape(-1, 2 * NA_WIN_C - 1), jnp.asarray(onehot), precision=HI)
    full = full.reshape(L, NA_HEADS, NA_WIN_R, NA_WIN_R, GRID_W, GRID_W)
    full = jnp.where(jnp.asarray(ok)[None, None, None, None], full, -1e30)
    return jnp.transpose(full, (0, 1, 2, 4, 3, 5)).reshape(L, NA_HEADS, NA_WIN_R, GRID_W, NA_WIN_R * GRID_W), onehot


def _na_bias_tables_bwd(db, onehot):
    d = db.reshape(NA_HEADS, NA_WIN_R, GRID_W, NA_WIN_R, GRID_W)
    d = jnp.transpose(d, (0, 1, 3, 2, 4)).reshape(-1, GRID_W * GRID_W)
    t1 = jnp.dot(d, jnp.asarray(onehot.T), precision=HI).reshape(NA_HEADS, NA_WIN_R, NA_WIN_R, 2 * NA_WIN_C - 1)
    out = jnp.zeros((NA_HEADS, 2 * NA_WIN_R - 1, 2 * NA_WIN_C - 1), F32)
    for c in range(NA_WIN_R):
        out = out.at[:, c:c + NA_WIN_R, :].add(t1[:, c])
    return out


def _na_windows(bi, rb, rows):
    wsl, cls = [], []
    for i in range(rb):
        r = bi * rb + i
        r0 = jnp.clip(r - NA_WIN_R // 2, 0, rows - NA_WIN_R)
        wsl.append(pl.ds(pl.multiple_of(r0 * GRID_W, GRID_W), NA_WIN_R * GRID_W))
        cls.append(r0 - r + NA_WIN_R - 1)
    return wsl, cls


def _na_fwd(q, k, v, bias, gq, gk, name, exch=None):
    H, T, d = q.shape
    rows = T // GRID_W
    rb = _tile(rows, 8, 1)
    win = NA_WIN_R * GRID_W
    scale = d ** -0.5

    def body(q_ref, k_ref, v_ref, b_ref, gq_ref, gk_ref, o_ref, qn, kn, vb):
        qv, kv = q_ref[0], k_ref[0]
        qn[...] = (qv * _rstd(qv) * gq_ref[...] * scale).astype(BF16)
        kn[...] = (kv * _rstd(kv) * gk_ref[...]).astype(BF16)
        vb[...] = v_ref[0].astype(BF16)

        def block(bi, c):
            wsl, cls = _na_windows(bi, rb, rows)
            qsl = pl.ds(pl.multiple_of(bi * (rb * GRID_W), rb * GRID_W), rb * GRID_W)
            kw = jnp.stack([kn[w, :] for w in wsl])
            vw = jnp.stack([vb[w, :] for w in wsl])
            s = _bdot(qn[qsl, :].reshape(rb, GRID_W, d), kw, 2, 2) + jnp.stack([b_ref[0, c_] for c_ in cls])
            p = jnp.exp(s - jnp.max(s, axis=-1, keepdims=True))
            p = p / jnp.sum(p, axis=-1, keepdims=True)
            o_ref[0, qsl, :] = _bdot(p.astype(BF16), vw, 2, 1).reshape(rb * GRID_W, d)
            return c

        lax.fori_loop(0, rows // rb, block, 0)

    hm = pl.BlockSpec((1, T, d), lambda h: (h, 0, 0))
    gs = pl.BlockSpec((1, d), lambda h: (0, 0))
    return _call(
        body, name=name, grid=(H,),
        in_specs=[hm, hm, hm, pl.BlockSpec((1, NA_WIN_R, GRID_W, win), lambda h: (h, 0, 0, 0)), gs, gs],
        out_specs=[hm], out_shape=[SDS((H, T, d), F32)],
        scratch_shapes=[pltpu.VMEM((T, d), BF16)] * 3,
        args=(q, k, v, bias, gq, gk), exch=exch)


def _na_bwd(q, k, v, do, bias, gq, gk, name):
    H, T, d = q.shape
    rows = T // GRID_W
    rb = _tile(rows, 8, 1)
    win = NA_WIN_R * GRID_W
    scale = d ** -0.5

    def body(q_ref, k_ref, v_ref, do_ref, b_ref, gq_ref, gk_ref,
             dq_ref, dk_ref, dv_ref, db_ref, dgq_ref, dgk_ref, qn, kn, vb, dob, dqn, dkn):
        h = pl.program_id(0)
        qv, kv = q_ref[0], k_ref[0]
        rq, rk = _rstd(qv), _rstd(kv)
        qn[...] = (qv * rq * gq_ref[...] * scale).astype(BF16)
        kn[...] = (kv * rk * gk_ref[...]).astype(BF16)
        vb[...] = v_ref[0].astype(BF16)
        dob[...] = do_ref[0].astype(BF16)
        dkn[...] = jnp.zeros_like(dkn)
        dv_ref[...] = jnp.zeros_like(dv_ref)
        db_ref[...] = jnp.zeros_like(db_ref)

        @pl.when(h == 0)
        def _():
            dgq_ref[...] = jnp.zeros_like(dgq_ref)
            dgk_ref[...] = jnp.zeros_like(dgk_ref)

        def block(bi, c):
            wsl, cls = _na_windows(bi, rb, rows)
            qsl = pl.ds(pl.multiple_of(bi * (rb * GRID_W), rb * GRID_W), rb * GRID_W)
            qs = qn[qsl, :].reshape(rb, GRID_W, d)
            dos = dob[qsl, :].reshape(rb, GRID_W, d)
            kw = jnp.stack([kn[w, :] for w in wsl])
            vw = jnp.stack([vb[w, :] for w in wsl])
            s = _bdot(qs, kw, 2, 2) + jnp.stack([b_ref[0, c_] for c_ in cls])
            p = jnp.exp(s - jnp.max(s, axis=-1, keepdims=True))
            p = p / jnp.sum(p, axis=-1, keepdims=True)
            dp = _bdot(dos, vw, 2, 2)
            ds = p * (dp - jnp.sum(dp * p, axis=-1, keepdims=True))
            dsb = ds.astype(BF16)
            dqn[qsl, :] = _bdot(dsb, kw, 2, 1).reshape(rb * GRID_W, d)
            dvw = _bdot(p.astype(BF16), dos, 1, 1)
            dkw = _bdot(dsb, qs, 1, 1)
            for i in range(rb):
                db_ref[0, cls[i]] += ds[i]
                dv_ref[0, wsl[i], :] += dvw[i]
                dkn[wsl[i], :] += dkw[i]
            return c

        lax.fori_loop(0, rows // rb, block, 0)
        dq, dgq = _rms_bwd(dqn[...] * scale, qv, gq_ref[...], rq)
        dk, dgk = _rms_bwd(dkn[...], kv, gk_ref[...], rk)
        dq_ref[0] = dq
        dk_ref[0] = dk
        dgq_ref[...] += dgq
        dgk_ref[...] += dgk

    hm = pl.BlockSpec((1, T, d), lambda h: (h, 0, 0))
    gs = pl.BlockSpec((1, d), lambda h: (0, 0))
    bs = pl.BlockSpec((1, NA_WIN_R, GRID_W, win), lambda h: (h, 0, 0, 0))
    return pl.pallas_call(
        body, name=name, grid=(H,),
        in_specs=[hm, hm, hm, hm, bs, gs, gs],
        out_specs=[hm, hm, hm, bs, gs, gs],
        out_shape=[SDS((H, T, d), F32)] * 3 + [SDS((H, NA_WIN_R, GRID_W, win), F32), SDS((1, d), F32), SDS((1, d), F32)],
        scratch_shapes=[pltpu.VMEM((T, d), BF16)] * 4 + [pltpu.VMEM((T, d), F32)] * 2,
        compiler_params=_params(1),
    )(q, k, v, do, bias, gq, gk)


def _gla_gate_fwd(z, col, upf, upb, bf, bb, name):
    T = z.shape[0]
    W = upf.shape[1]
    tm = _tile(T, 1024)

    def body(z_ref, upf_ref, upb_ref, bf_ref, bb_ref, gf_ref, gb_ref):
        seg = z_ref[...].astype(BF16)
        for up_ref, b_ref, o_ref in ((upf_ref, bf_ref, gf_ref), (upb_ref, bb_ref, gb_ref)):
            pre = _dot(seg, up_ref[...]) + b_ref[...]
            o_ref[...] = (jnp.minimum(pre, 0.0) - jnp.log(1.0 + jnp.exp(-jnp.abs(pre)))) * (1.0 / GLA_TAU)

    full = lambda shape: pl.BlockSpec(shape, lambda i: (0, 0))
    return pl.pallas_call(
        body, name=name, grid=(T // tm,),
        in_specs=[pl.BlockSpec((tm, LANE), lambda i: (i, col)), full((LANE, W)), full((LANE, W)), full((1, W)), full((1, W))],
        out_specs=[pl.BlockSpec((tm, W), lambda i: (i, 0))] * 2,
        out_shape=[SDS((T, W), F32)] * 2,
        compiler_params=_params(1),
    )(z, upf, upb, bf, bb)


def _gla_gate_bwd(z, col, upf, upb, bf, bb, dgf, dgb, name):
    T = z.shape[0]
    W = upf.shape[1]
    tm = _tile(T, 1024)

    def body(z_ref, upf_ref, upb_ref, bf_ref, bb_ref, dgf_ref, dgb_ref, dseg_ref, dupf_ref, dupb_ref, dbf_ref, dbb_ref):
        @pl.when(pl.program_id(0) == 0)
        def _():
            for r in (dupf_ref, dupb_ref, dbf_ref, dbb_ref):
                r[...] = jnp.zeros_like(r)

        seg = z_ref[...].astype(BF16)
        dseg = jnp.zeros(dseg_ref.shape, F32)
        for up_ref, b_ref, dg_ref, dup_ref, db_ref in ((upf_ref, bf_ref, dgf_ref, dupf_ref, dbf_ref),
                                                      (upb_ref, bb_ref, dgb_ref, dupb_ref, dbb_ref)):
            pre = _dot(seg, up_ref[...]) + b_ref[...]
            dpre = dg_ref[...] * (1.0 / GLA_TAU) * (1.0 - _sigmoid(pre))
            dpb = dpre.astype(BF16)
            dseg = dseg + _dot(dpb, up_ref[...], NT)
            dup_ref[...] += _dot(seg, dpb, TN)
            db_ref[...] += jnp.sum(dpre, axis=0, keepdims=True)
        dseg_ref[...] = dseg

    full = lambda shape: pl.BlockSpec(shape, lambda i: (0, 0))
    tok = pl.BlockSpec((tm, W), lambda i: (i, 0))
    return pl.pallas_call(
        body, name=name, grid=(T // tm,),
        in_specs=[pl.BlockSpec((tm, LANE), lambda i: (i, col)), full((LANE, W)), full((LANE, W)), full((1, W)), full((1, W)),
                  tok, tok],
        out_specs=[pl.BlockSpec((tm, LANE), lambda i: (i, 0)), full((LANE, W)), full((LANE, W)), full((1, W)), full((1, W))],
        out_shape=[SDS((T, LANE), F32), SDS((LANE, W), F32), SDS((LANE, W), F32), SDS((1, W), F32), SDS((1, W), F32)],
        compiler_params=_params(1),
    )(z, upf, upb, bf, bb, dgf, dgb)


def _bdot(a, b, ca, cb, precision=None):
    return lax.dot_general(a, b, (((ca,), (cb,)), ((0,), (0,))), preferred_element_type=F32, precision=precision)


def _gla_chunk_terms(q_ref, k_ref, v_ref, g_ref, rev, tok, n, C):
    dk, dv = q_ref.shape[-1], v_ref.shape[-1]
    q = q_ref[0, tok, :].reshape(n, C, dk) * (dk ** -0.5)
    k = k_ref[0, tok, :].reshape(n, C, dk)
    v = v_ref[0, tok, :].reshape(n, C, dv)
    g = g_ref[rev, 0, tok, :].reshape(n, C, dk)
    ii = lax.broadcasted_iota(jnp.int32, (C, C), 0)
    jj = lax.broadcasted_iota(jnp.int32, (C, C), 1)
    tri = jnp.broadcast_to(((ii <= jj) if rev else (ii >= jj)).astype(F32), (n, C, C))
    amask = ((jj > ii) if rev else (ii >= jj))[None]
    b = _bdot(tri, g, 2, 1, HI)
    bl = jnp.sum(g, axis=1, keepdims=True)
    eb = jnp.exp(b)
    enb = jnp.exp(-b)
    eend = jnp.exp(bl - b)
    return q, k, v, tri, amask, bl, eb, enb, eend


def _gla_segments(T, C):
    n = T // C
    ns = _tile(n, 16, 1)

    def tok(s):
        return pl.ds(pl.multiple_of(s * (ns * C), ns * C), ns * C)

    def chunks(s):
        return pl.ds(pl.multiple_of(s * ns, ns), ns)

    return n, ns, n // ns, tok, chunks


def _gla_fwd(q, k, v, g, name):
    H, T, dk = q.shape
    dv = v.shape[-1]
    C = GLA_CHUNK
    n, ns, nseg, seg_tok, seg_chunks = _gla_segments(T, C)

    def body(q_ref, k_ref, v_ref, g_ref, o_ref, upd_scr, dec_scr, st_scr):
        for rev in (0, 1):
            def intra(s, c):
                tok, ch = seg_tok(s), seg_chunks(s)
                q, k, v, tri, amask, bl, eb, enb, eend = _gla_chunk_terms(q_ref, k_ref, v_ref, g_ref, rev, tok, ns, C)
                vb = v.astype(BF16)
                a = jnp.where(amask, _bdot((q * eb).astype(BF16), (k * enb).astype(BF16), 2, 2), 0.0).astype(BF16)
                o = _bdot(a, vb, 2, 1).reshape(ns * C, dv)
                if rev:
                    o_ref[0, tok, :] += o
                else:
                    o_ref[0, tok, :] = o
                upd_scr[ch] = _bdot(vb, (k * eend).astype(BF16), 1, 1)
                dec_scr[ch] = jnp.exp(bl)
                return c

            lax.fori_loop(0, nseg, intra, 0)

            def step(t, st):
                i = n - 1 - t if rev else t
                st_scr[i] = st
                return st * dec_scr[i] + upd_scr[i]

            lax.fori_loop(0, n, step, jnp.zeros((dv, dk), F32))

            def inter(s, c):
                tok, ch = seg_tok(s), seg_chunks(s)
                q, k, v, tri, amask, bl, eb, enb, eend = _gla_chunk_terms(q_ref, k_ref, v_ref, g_ref, rev, tok, ns, C)
                o_ref[0, tok, :] += _bdot((q * eb).astype(BF16), st_scr[ch].astype(BF16), 2, 2).reshape(ns * C, dv)
                return c

            lax.fori_loop(0, nseg, inter, 0)

    sk = pl.BlockSpec((1, T, dk), lambda h: (h, 0, 0))
    sv = pl.BlockSpec((1, T, dv), lambda h: (h, 0, 0))
    sg = pl.BlockSpec((2, 1, T, dk), lambda h: (0, h, 0, 0))
    return pl.pallas_call(
        body, name=name, grid=(H,), in_specs=[sk, sk, sv, sg], out_specs=sv,
        out_shape=SDS((H, T, dv), F32),
        scratch_shapes=[pltpu.VMEM((n, dv, dk), F32), pltpu.VMEM((n, 1, dk), F32), pltpu.VMEM((n, dv, dk), F32)],
        compiler_params=_params(1),
    )(q, k, v, g)


def _gla_bwd(q, k, v, g, do, name):
    H, T, dk = q.shape
    dv = v.shape[-1]
    C = GLA_CHUNK
    n, ns, nseg, seg_tok, seg_chunks = _gla_segments(T, C)

    def one_scan(rev, q_ref, k_ref, v_ref, g_ref, do_ref, dq_ref, dk_ref, dv_ref, dg_ref, upd_scr, dec_scr, st_scr, gn_scr,
                 rn_scr):
        def first(s, c):
            tok, ch = seg_tok(s), seg_chunks(s)
            q, k, v, tri, amask, bl, eb, enb, eend = _gla_chunk_terms(q_ref, k_ref, v_ref, g_ref, rev, tok, ns, C)
            dob = do_ref[0, tok, :].reshape(ns, C, dv).astype(BF16)
            upd_scr[ch] = _bdot(v.astype(BF16), (k * eend).astype(BF16), 1, 1)
            dec_scr[ch] = jnp.exp(bl)
            gn_scr[ch] = _bdot(dob, (q * eb).astype(BF16), 1, 1)
            return c

        lax.fori_loop(0, nseg, first, 0)

        def fstep(t, st):
            i = n - 1 - t if rev else t
            st_scr[i] = st
            return st * dec_scr[i] + upd_scr[i]

        lax.fori_loop(0, n, fstep, jnp.zeros((dv, dk), F32))

        def bstep(t, r):
            i = t if rev else n - 1 - t
            rn_scr[i] = r
            return gn_scr[i] + r * dec_scr[i]

        lax.fori_loop(0, n, bstep, jnp.zeros((dv, dk), F32))

        def second(s, c):
            tok, ch = seg_tok(s), seg_chunks(s)
            q, k, v, tri, amask, bl, eb, enb, eend = _gla_chunk_terms(q_ref, k_ref, v_ref, g_ref, rev, tok, ns, C)
            qe, ke, kend = q * eb, k * enb, k * eend
            qeb, keb, kendb, vb = qe.astype(BF16), ke.astype(BF16), kend.astype(BF16), v.astype(BF16)
            dob = do_ref[0, tok, :].reshape(ns, C, dv).astype(BF16)
            a = jnp.where(amask, _bdot(qeb, keb, 2, 2), 0.0).astype(BF16)
            st = st_scr[ch]
            rn = rn_scr[ch]
            rnb = rn.astype(BF16)
            da = jnp.where(amask, _bdot(dob, vb, 2, 2), 0.0).astype(BF16)
            dvv = _bdot(a, dob, 1, 1) + _bdot(kendb, rnb, 2, 2)
            dqe = _bdot(da, keb, 2, 1) + _bdot(dob, st.astype(BF16), 2, 1)
            dke = _bdot(da, qeb, 1, 1)
            dkend = _bdot(vb, rnb, 2, 1)
            ddec = jnp.sum(rn * st, axis=1, keepdims=True)
            dbl = ddec * jnp.exp(bl) + jnp.sum(dkend * kend, axis=1, keepdims=True)
            db = dqe * qe - dke * ke - dkend * kend
            dg = _bdot(tri, db, 1, 1, HI) + dbl
            dqv = (dqe * eb * (dk ** -0.5)).reshape(ns * C, dk)
            dkv = (dke * enb + dkend * eend).reshape(ns * C, dk)
            if rev:
                dq_ref[0, tok, :] += dqv
                dk_ref[0, tok, :] += dkv
                dv_ref[0, tok, :] += dvv.reshape(ns * C, dv)
            else:
                dq_ref[0, tok, :] = dqv
                dk_ref[0, tok, :] = dkv
                dv_ref[0, tok, :] = dvv.reshape(ns * C, dv)
            dg_ref[rev, 0, tok, :] = dg.reshape(ns * C, dk)
            return c

        lax.fori_loop(0, nseg, second, 0)

    def body(*refs):
        one_scan(0, *refs)
        one_scan(1, *refs)

    sk = pl.BlockSpec((1, T, dk), lambda h: (h, 0, 0), pipeline_mode=pl.Buffered(1))
    sv = pl.BlockSpec((1, T, dv), lambda h: (h, 0, 0), pipeline_mode=pl.Buffered(1))
    sg = pl.BlockSpec((2, 1, T, dk), lambda h: (0, h, 0, 0), pipeline_mode=pl.Buffered(1))
    st_shape = pltpu.VMEM((n, dv, dk), F32)
    return pl.pallas_call(
        body, name=name, grid=(H,), in_specs=[sk, sk, sv, sg, sv], out_specs=[sk, sk, sv, sg],
        out_shape=[SDS((H, T, dk), F32), SDS((H, T, dk), F32), SDS((H, T, dv), F32), SDS((2, H, T, dk), F32)],
        scratch_shapes=[st_shape, pltpu.VMEM((n, 1, dk), F32), st_shape, st_shape, st_shape],
        compiler_params=_params(1),
    )(q, k, v, g, do)


def _rope_tables(T):
    half = MLA_ROPE // 2
    inv = ROPE_THETA ** (-jnp.arange(half, dtype=F32) / half)
    ang = jnp.arange(T, dtype=F32)[:, None] * inv[None, :]
    cos, sin = jnp.cos(ang), jnp.sin(ang)
    ctab = jnp.concatenate([jnp.ones((T, MLA_NOPE), F32), cos, cos], axis=1)
    stab = jnp.concatenate([jnp.zeros((T, MLA_NOPE), F32), -sin, sin], axis=1)
    return ctab, stab


def _swap_rope_halves(x):
    half = MLA_ROPE // 2
    return jnp.concatenate([x[:, :MLA_NOPE], x[:, MLA_NOPE + half:], x[:, MLA_NOPE:MLA_NOPE + half]], axis=1)


def _mla_prep_fwd(x, g, ctab, stab, name):
    H, T, d = x.shape
    tm = _tile(T, 1024)

    def body(x_ref, g_ref, c_ref, s_ref, o_ref):
        xv = x_ref[0]
        xn = xv * _rstd(xv) * g_ref[...]
        o_ref[0] = (xn * c_ref[...] + _swap_rope_halves(xn) * s_ref[...]).astype(BF16)

    tab = pl.BlockSpec((tm, d), lambda h, i: (i, 0))
    return pl.pallas_call(
        body, name=name, grid=(H, T // tm),
        in_specs=[pl.BlockSpec((1, tm, d), lambda h, i: (h, i, 0)), pl.BlockSpec((1, d), lambda h, i: (0, 0)), tab, tab],
        out_specs=pl.BlockSpec((1, tm, d), lambda h, i: (h, i, 0)),
        out_shape=SDS((H, T, d), BF16),
        compiler_params=_params(2),
    )(x, g, ctab, stab)


def _mla_prep_bwd(x, g, ctab, stab, dy, name):
    H, T, d = x.shape
    tm = _tile(T, 1024)

    def body(x_ref, g_ref, c_ref, s_ref, dy_ref, dx_ref, dg_ref):
        @pl.when((pl.program_id(0) == 0) & (pl.program_id(1) == 0))
        def _():
            dg_ref[...] = jnp.zeros_like(dg_ref)

        xv = x_ref[0]
        dyv = dy_ref[0]
        dxn = dyv * c_ref[...] + _swap_rope_halves(dyv * s_ref[...])
        dx, dg = _rms_bwd(dxn, xv, g_ref[...], _rstd(xv))
        dx_ref[0] = dx
        dg_ref[...] += dg

    tab = pl.BlockSpec((tm, d), lambda h, i: (i, 0))
    blk = pl.BlockSpec((1, tm, d), lambda h, i: (h, i, 0))
    gs = pl.BlockSpec((1, d), lambda h, i: (0, 0))
    return pl.pallas_call(
        body, name=name, grid=(H, T // tm),
        in_specs=[blk, gs, tab, tab, blk],
        out_specs=[blk, gs], out_shape=[SDS((H, T, d), F32), SDS((1, d), F32)],
        compiler_params=_params(2),
    )(x, g, ctab, stab, dy)


def _mla_attn_fwd(q, k, v, name, exch=None):
    H, T, d = q.shape
    dv = v.shape[-1]
    tq = _tile(T, 256)
    scale = d ** -0.5

    def body(q_ref, k_ref, v_ref, o_ref):
        s = _dot(q_ref[0], k_ref[0], NT) * scale
        p = jnp.exp(s - jnp.max(s, axis=-1, keepdims=True))
        p = p / jnp.sum(p, axis=-1, keepdims=True)
        o_ref[0] = _dot(p.astype(BF16), v_ref[0])

    return _call(
        body, name=name, grid=(H, T // tq),
        in_specs=[pl.BlockSpec((1, tq, d), lambda h, i: (h, i, 0)), pl.BlockSpec((1, T, d), lambda h, i: (h, 0, 0)),
                  pl.BlockSpec((1, T, dv), lambda h, i: (h, 0, 0))],
        out_specs=[pl.BlockSpec((1, tq, dv), lambda h, i: (h, i, 0))],
        out_shape=[SDS((H, T, dv), F32)],
        args=(q, k, v), exch=exch)


def _mla_attn_bwd(q, k, v, do, name, exch=None):
    H, T, d = q.shape
    dv = v.shape[-1]
    tq = _tile(T, 256)
    scale = d ** -0.5

    def body(q_ref, k_ref, v_ref, do_ref, dq_ref, dk_ref, dv_ref):
        @pl.when(pl.program_id(1) == 0)
        def _():
            dk_ref[...] = jnp.zeros_like(dk_ref)
            dv_ref[...] = jnp.zeros_like(dv_ref)

        qv, kv = q_ref[0], k_ref[0]
        dob = do_ref[0].astype(BF16)
        st = _dot(kv, qv, NT) * scale
        pt = jnp.exp(st - jnp.max(st, axis=0, keepdims=True))
        pt = pt / jnp.sum(pt, axis=0, keepdims=True)
        dpt = _dot(v_ref[0], dob, NT)
        dst = (pt * (dpt - jnp.sum(pt * dpt, axis=0, keepdims=True)) * scale).astype(BF16)
        dv_ref[0] += _dot(pt.astype(BF16), dob)
        dk_ref[0] += _dot(dst, qv)
        dq_ref[0] = _dot(dst, kv, TN)

    qb = pl.BlockSpec((1, tq, d), lambda h, i: (h, i, 0))
    kb = pl.BlockSpec((1, T, d), lambda h, i: (h, 0, 0))
    vb = pl.BlockSpec((1, T, dv), lambda h, i: (h, 0, 0))
    return _call(
        body, name=name, grid=(H, T // tq),
        in_specs=[qb, kb, vb, pl.BlockSpec((1, tq, dv), lambda h, i: (h, i, 0))],
        out_specs=[qb, kb, vb],
        out_shape=[SDS((H, T, d), F32), SDS((H, T, d), F32), SDS((H, T, dv), F32)],
        args=(q, k, v, do), exch=exch)


def _silu_parts(x):
    s = _sigmoid(x)
    return x * s, s * (1.0 + x * (1.0 - s))


def _mix_out_fwd(x1, y_na, o_gla, y_mla, z, gr_col, gnorm, wbr, w_out, name):
    T, D = x1.shape
    W = y_na.shape[0] * y_na.shape[2]
    tm = _tile(T, 512)

    def body(x_ref, na_ref, gla_ref, mla_ref, gt_ref, gr_ref, gn_ref, w0, w1, w2, wo_ref, o_ref):
        gn = gn_ref[...]
        nrm = jnp.concatenate([gla_ref[h] * _rstd(gla_ref[h]) * gn for h in range(GLA_HEADS)], axis=1)
        y_gla = nrm * _silu_parts(gr_ref[...])[0]
        y_na = jnp.concatenate([na_ref[h] for h in range(NA_HEADS)], axis=1)
        y_mla = jnp.concatenate([mla_ref[h] for h in range(MLA_HEADS)], axis=1)
        mixed = jnp.zeros((tm, D), F32)
        for i, (y, w_ref) in enumerate(((y_na, w0), (y_gla, w1), (y_mla, w2))):
            mixed = mixed + _sigmoid(gt_ref[:, i * D:(i + 1) * D]) * _dot(y.astype(BF16), w_ref[...])
        o_ref[...] = x_ref[...] + _dot(mixed.astype(BF16), wo_ref[...])

    tok = lambda w: pl.BlockSpec((tm, w), lambda i: (i, 0))
    hm = lambda a: pl.BlockSpec((a.shape[0], tm, a.shape[2]), lambda i: (0, i, 0))
    full = lambda shape: pl.BlockSpec(shape, lambda i: (0, 0))
    return pl.pallas_call(
        body, name=name, grid=(T // tm,),
        in_specs=[tok(D), hm(y_na), hm(o_gla), hm(y_mla), tok(3 * D), pl.BlockSpec((tm, W), lambda i: (i, gr_col)),
                  full((1, GLA_DV)), full((W, D)), full((W, D)), full((W, D)), full((D, D))],
        out_specs=tok(D), out_shape=SDS((T, D), F32),
        compiler_params=_params(1),
    )(x1, y_na, o_gla, y_mla, z, z, gnorm, wbr[0], wbr[1], wbr[2], w_out)


def _mix_out_bwd(dx2, y_na, o_gla, y_mla, z, gr_col, gnorm, wbr, w_out, name):
    T, D = dx2.shape
    W = y_na.shape[0] * y_na.shape[2]
    tm = _tile(T, 256)

    def body(dx_ref, na_ref, gla_ref, mla_ref, gt_ref, gr_ref, gn_ref, w0, w1, w2, wo_ref,
             dna_ref, dgla_ref, dmla_ref, dgr_ref, dgt_ref, dw0, dw1, dw2, dwo_ref, dgn_ref):
        @pl.when(pl.program_id(0) == 0)
        def _():
            for r in (dw0, dw1, dw2, dwo_ref, dgn_ref):
                r[...] = jnp.zeros_like(r)

        gn = gn_ref[...]
        grv = gr_ref[...]
        sil, dsil = _silu_parts(grv)
        rs = [_rstd(gla_ref[h]) for h in range(GLA_HEADS)]
        nrm = jnp.concatenate([gla_ref[h] * rs[h] * gn for h in range(GLA_HEADS)], axis=1)
        ys = (jnp.concatenate([na_ref[h] for h in range(NA_HEADS)], axis=1).astype(BF16), (nrm * sil).astype(BF16),
              jnp.concatenate([mla_ref[h] for h in range(MLA_HEADS)], axis=1).astype(BF16))
        dxb = dx_ref[...].astype(BF16)
        dmixed = _dot(dxb, wo_ref[...], NT)
        mixed = jnp.zeros((tm, D), F32)
        dys = []
        for i, (y, w_ref, dw_ref) in enumerate(zip(ys, (w0, w1, w2), (dw0, dw1, dw2))):
            gt = _sigmoid(gt_ref[:, i * D:(i + 1) * D])
            p = _dot(y, w_ref[...])
            mixed = mixed + gt * p
            dp = (dmixed * gt).astype(BF16)
            dgt_ref[:, i * D:(i + 1) * D] = (dmixed * p * gt * (1.0 - gt)).astype(BF16)
            dys.append(_dot(dp, w_ref[...], NT))
            dw_ref[...] += _dot(y, dp, TN)
        dwo_ref[...] += _dot(mixed.astype(BF16), dxb, TN)
        for h in range(NA_HEADS):
            dna_ref[h] = dys[0][:, h * NA_HD:(h + 1) * NA_HD]
        for h in range(MLA_HEADS):
            dmla_ref[h] = dys[2][:, h * MLA_V:(h + 1) * MLA_V]
        dgr_ref[...] = dys[1] * nrm * dsil
        dn = dys[1] * sil
        dgn = jnp.zeros((1, GLA_DV), F32)
        for h in range(GLA_HEADS):
            dxh, dgh = _rms_bwd(dn[:, h * GLA_DV:(h + 1) * GLA_DV], gla_ref[h], gn, rs[h])
            dgla_ref[h] = dxh
            dgn = dgn + dgh
        dgn_ref[...] += dgn

    tok = lambda w: pl.BlockSpec((tm, w), lambda i: (i, 0))
    hm = lambda a: pl.BlockSpec((a.shape[0], tm, a.shape[2]), lambda i: (0, i, 0))
    full = lambda shape: pl.BlockSpec(shape, lambda i: (0, 0))
    return pl.pallas_call(
        body, name=name, grid=(T // tm,),
        in_specs=[tok(D), hm(y_na), hm(o_gla), hm(y_mla), tok(3 * D), pl.BlockSpec((tm, W), lambda i: (i, gr_col)),
                  full((1, GLA_DV)), full((W, D)), full((W, D)), full((W, D)), full((D, D))],
        out_specs=[hm(y_na), hm(o_gla), hm(y_mla), tok(W), tok(3 * D), full((W, D)), full((W, D)), full((W, D)),
                   full((D, D)), full((1, GLA_DV))],
        out_shape=[SDS(y_na.shape, F32), SDS(o_gla.shape, F32), SDS(y_mla.shape, F32), SDS((T, W), F32),
                   SDS((T, 3 * D), BF16)] + [SDS((W, D), F32)] * 3 + [SDS((D, D), F32), SDS((1, GLA_DV), F32)],
        compiler_params=_params(1),
    )(dx2, y_na, o_gla, y_mla, z, z, gnorm, wbr[0], wbr[1], wbr[2], w_out)


def _loss_head(y, target, name):
    T, D = y.shape
    tm = _tile(T, 1024)

    def body(y_ref, t_ref, dy_ref, l_ref):
        @pl.when(pl.program_id(0) == 0)
        def _():
            l_ref[...] = jnp.zeros_like(l_ref)

        e = y_ref[...] - t_ref[...]
        dy_ref[...] = e * (1.0 / D)
        l_ref[...] += 0.5 * jnp.sum(jnp.mean(e * e, axis=-1, keepdims=True), axis=0, keepdims=True)

    tok = pl.BlockSpec((tm, D), lambda i: (i, 0))
    return pl.pallas_call(
        body, name=name, grid=(T // tm,), in_specs=[tok, tok],
        out_specs=[tok, pl.BlockSpec((1, 1), lambda i: (0, 0))],
        out_shape=[SDS((T, D), F32), SDS((1, 1), F32)],
        compiler_params=_params(1),
    )(y, target)


def _exchange_copies(src, dst, send_sems, recv_sems, local_sems, broadcast, with_recvs=True):
    n = len(src)
    x, y, c = lax.axis_index("x"), lax.axis_index("y"), lax.axis_index("c")
    me = 4 * x + 2 * y + c
    local = [pltpu.make_async_copy(src[a] if broadcast else src[a].at[me], dst[a].at[me], local_sems.at[a])
             for a in range(n)]
    sends, recvs = [], []
    for d in range(1, N_DEV):
        px = 1 - x if d & 4 else x
        py = 1 - y if d & 2 else y
        pc = 1 - c if d & 1 else c
        peer = 4 * px + 2 * py + pc
        for a in range(n):
            for out, slot in ((sends, me), (recvs, peer)) if with_recvs else ((sends, me),):
                out.append(pltpu.make_async_remote_copy(
                    src_ref=src[a] if broadcast else src[a].at[peer], dst_ref=dst[a].at[slot],
                    send_sem=send_sems.at[a, d - 1], recv_sem=recv_sems.at[a, d - 1],
                    device_id=(px, py, pc), device_id_type=pl.DeviceIdType.MESH))
    return local, sends, recvs


def _exchange_start(*refs_and_mode):
    local, sends, _ = _exchange_copies(*refs_and_mode, with_recvs=False)
    for cp in local + sends:
        cp.start()


def _exchange_wait(*refs_and_mode):
    local, sends, recvs = _exchange_copies(*refs_and_mode)
    for cp in recvs:
        cp.wait_recv()
    for cp in sends:
        cp.wait_send()
    for cp in local:
        cp.wait()


def _exchange_shapes(srcs, broadcast):
    n = len(srcs)
    out_shape = [SDS((N_DEV,) + (tuple(s.shape) if broadcast else tuple(s.shape[1:])), s.dtype) for s in srcs]
    sems = [pltpu.SemaphoreType.DMA((n, N_DEV - 1)), pltpu.SemaphoreType.DMA((n, N_DEV - 1)),
            pltpu.SemaphoreType.DMA((n,))]
    return out_shape, sems


def _exchange(srcs, broadcast, name):
    n = len(srcs)
    out_shape, sems = _exchange_shapes(srcs, broadcast)

    def body(*refs):
        mode = (refs[:n], refs[n:2 * n]) + tuple(refs[2 * n:]) + (broadcast,)
        _exchange_start(*mode)
        _exchange_wait(*mode)

    hbm = pl.BlockSpec(memory_space=pltpu.HBM)
    return pl.pallas_call(body, name=name, in_specs=[hbm] * n, out_specs=[hbm] * n, out_shape=out_shape,
                          scratch_shapes=sems)(*srcs)


def _call(body, *, name, grid, in_specs, out_specs, out_shape, args, scratch_shapes=(), exch=None):
    params = _params(len(grid))
    if exch is None:
        return pl.pallas_call(body, name=name, grid=grid, in_specs=in_specs, out_specs=out_specs, out_shape=out_shape,
                              scratch_shapes=list(scratch_shapes), compiler_params=params)(*args)
    srcs, broadcast = exch
    n, n_in, n_out, n_scr = len(srcs), len(args), len(out_shape), len(scratch_shapes)
    x_shape, sems = _exchange_shapes(srcs, broadcast)

    def carrier(*refs):
        ins, x_src = refs[:n_in], refs[n_in:n_in + n]
        outs, x_dst = refs[n_in + n:n_in + n + n_out], refs[n_in + n + n_out:n_in + 2 * n + n_out]
        scr = refs[n_in + 2 * n + n_out:n_in + 2 * n + n_out + n_scr]
        mode = (x_src, x_dst) + tuple(refs[n_in + 2 * n + n_out + n_scr:]) + (broadcast,)
        ids = [pl.program_id(a) for a in range(len(grid))]
        first = functools.reduce(jnp.logical_and, [i == 0 for i in ids])
        last = functools.reduce(jnp.logical_and, [i == g - 1 for i, g in zip(ids, grid)])
        pl.when(first)(lambda: _exchange_start(*mode))
        body(*ins, *outs, *scr)
        pl.when(last)(lambda: _exchange_wait(*mode))

    hbm = pl.BlockSpec(memory_space=pltpu.HBM)
    res = pl.pallas_call(carrier, name=name, grid=grid, in_specs=list(in_specs) + [hbm] * n,
                         out_specs=list(out_specs) + [hbm] * n, out_shape=list(out_shape) + x_shape,
                         scratch_shapes=list(scratch_shapes) + sems, compiler_params=params)(*args, *srcs)
    return res[:n_out], res[n_out:]


def _adam_math(w, g, m, v):
    m = ADAM_B1 * m + (1.0 - ADAM_B1) * g
    v = ADAM_B2 * v + (1.0 - ADAM_B2) * (g * g)
    m_hat = m / (1.0 - ADAM_B1 ** ADAM_STEP)
    v_hat = v / (1.0 - ADAM_B2 ** ADAM_STEP)
    delta = -ADAM_LR * (m_hat / (jnp.sqrt(v_hat) + ADAM_EPS) + ADAM_WD * w)
    return delta, m, v


def _adamw_sharded(landed, k, w, m, v, name):
    L, r, c = w.shape
    lanes = -(-c // LANE) * LANE
    rb = _tile(r, max(16, (1 << 20) // (N_DEV * lanes * 2)), 16)

    def body(*refs):
        l_refs = refs[:L]
        w_ref, m_ref, v_ref, g_ref, d_ref, nm_ref, nv_ref = refs[L:]
        for j in range(L):
            @pl.when(pl.program_id(0) == j)
            def _(j=j):
                g = l_refs[j][0, 0].astype(F32)
                for s in range(1, N_DEV):
                    g = g + l_refs[j][s, 0].astype(F32)
                delta, nm, nv = _adam_math(w_ref[0], g, m_ref[0], v_ref[0])
                g_ref[0] = g
                d_ref[0] = delta
                nm_ref[0] = nm
                nv_ref[0] = nv

    blk = pl.BlockSpec((1, rb, c), lambda l, i: (l, i, 0))
    land = [pl.BlockSpec((N_DEV, 1, rb, c), lambda l, i, j=j: (0, k, jnp.where(l == j, i, 0), 0)) for j in range(L)]
    return pl.pallas_call(
        body, name=name, grid=(L, r // rb),
        in_specs=land + [blk, blk, blk],
        out_specs=[blk] * 4, out_shape=[SDS((L, r, c), F32)] * 4,
        compiler_params=_params(2),
    )(*landed, w, m, v)


def _adamw_replicated(landed, w, m, v, name):
    R = w.shape[0]

    def body(l_ref, w_ref, m_ref, v_ref, g_ref, d_ref, nm_ref, nv_ref):
        g = l_ref[0]
        for s in range(1, N_DEV):
            g = g + l_ref[s]
        delta, nm, nv = _adam_math(w_ref[...], g, m_ref[...], v_ref[...])
        g_ref[...] = g
        d_ref[...] = delta
        nm_ref[...] = nm
        nv_ref[...] = nv

    blk = pl.BlockSpec((R, LANE), lambda i: (0, 0))
    return pl.pallas_call(
        body, name=name, grid=(1,),
        in_specs=[pl.BlockSpec((N_DEV, R, LANE), lambda i: (0, 0, 0)), blk, blk, blk],
        out_specs=[blk] * 4, out_shape=[SDS((R, LANE), F32)] * 4,
        compiler_params=_params(1),
    )(landed, w, m, v)


def _heads(x, h):
    T = x.shape[0]
    return jnp.transpose(x.reshape(T, h, -1), (1, 0, 2))


def _unheads(x):
    h, T, d = x.shape
    return jnp.transpose(x, (1, 0, 2)).reshape(T, h * d)


def _pack(parts):
    flat = jnp.concatenate([p.reshape(-1) for p in parts])
    rows = -(-flat.shape[0] // (8 * LANE)) * 8
    return jnp.pad(flat, (0, rows * LANE - flat.shape[0])).reshape(rows, LANE)


def _unpack(packed, shapes):
    flat = packed.reshape(-1)
    out, off = [], 0
    for s in shapes:
        size = int(np.prod(s))
        out.append(flat[off:off + size].reshape(s))
        off += size
    return out


def _layer_fwd(x0, w, l, lay, tabs, carry):
    T, D = x0.shape
    ctab, stab, na_bias = tabs
    col = lambda name: lay[name][0] // lay[name][1]
    sv = {'x0': x0}
    x1, sv['a1'], sv['b1'] = _carried(_ffn_fwd, carry, 'ffn1', x0, w['ffn1_norm'], w['ffn1_w1'], w['ffn1_w3'],
                                      w['ffn1_w2'], f"ffn1_fwd_{l}")
    z, sv['hb'] = _carried(_norm_matmul, carry, 'mix_in', x1, 0, w['mix_norm'], w['w_in_z'], f"mix_in_{l}")
    seg = lambda name: z[:, lay[name][0]:lay[name][0] + lay[name][3]]
    sv['na_q'], sv['na_k'], sv['na_v'] = (_heads(seg(n), NA_HEADS) for n in ('na_q', 'na_k', 'na_v'))
    y_na = _carried(_na_fwd, carry, 'na', sv['na_q'], sv['na_k'], sv['na_v'], na_bias, w['na_q_norm'], w['na_k_norm'],
                    f"na_fwd_{l}")[0]
    gf, gb = _gla_gate_fwd(z, col('glow'), w['up_f'], w['up_b'], w['gla_gf_bias'], w['gla_gb_bias'], f"gla_gate_{l}")
    sv['gq'], sv['gk'], sv['gv'] = (_heads(seg(n), GLA_HEADS) for n in ('gq', 'gk', 'gv'))
    sv['gg'] = jnp.stack([_heads(gf, GLA_HEADS), _heads(gb, GLA_HEADS)])
    o_gla = _gla_fwd(sv['gq'], sv['gk'], sv['gv'], sv['gg'], f"gla_fwd_{l}")
    q_raw, sv['hq'] = _norm_matmul(z, col('c_q'), w['mla_cq_norm'], w['mla_w_uq'], f"mla_uq_{l}")
    kv_raw, sv['hkv'] = _norm_matmul(z, col('c_kv'), w['mla_ckv_norm'], w['mla_w_ukv'], f"mla_ukv_{l}")
    kv_h = _heads(kv_raw, MLA_HEADS)
    k_rope = jnp.broadcast_to(seg('k_rope')[None], (MLA_HEADS, T, MLA_ROPE))
    sv['mq_raw'] = _heads(q_raw, MLA_HEADS)
    sv['mk_raw'] = jnp.concatenate([kv_h[..., :MLA_NOPE], k_rope], axis=-1)
    sv['mq'] = _mla_prep_fwd(sv['mq_raw'], w['mla_q_norm'], ctab, stab, f"mla_qprep_{l}")
    sv['mk'] = _mla_prep_fwd(sv['mk_raw'], w['mla_k_norm'], ctab, stab, f"mla_kprep_{l}")
    sv['mv'] = kv_h[..., MLA_NOPE:].astype(BF16)
    y_mla = _carried(_mla_attn_fwd, carry, 'mla', sv['mq'], sv['mk'], sv['mv'], f"mla_attn_{l}")[0]
    w_br = (w['w_br_na'], w['w_br_gla'], w['w_br_mla'])
    x2 = _mix_out_fwd(x1, y_na, o_gla, y_mla, z, col('gr'), w['gla_out_norm'], w_br, w['w_out'], f"mix_out_{l}")
    sv.update(x1=x1, z=z, y_na=y_na, o_gla=o_gla, y_mla=y_mla, x2=x2)
    x3, sv['a2'], sv['b2'] = _carried(_ffn_fwd, carry, 'ffn2', x2, w['ffn2_norm'], w['ffn2_w1'], w['ffn2_w3'],
                                      w['ffn2_w2'], f"ffn2_fwd_{l}")
    return x3, sv


def _ffn_grads(grads, x, dy, a, b, w, pre, l, carry):
    dx, da, db, u, hb, dyb, dg = _carried(_ffn_bwd, carry, pre, x, w[pre + '_norm'], dy, a, b, w[pre + '_w1'],
                                          w[pre + '_w3'], w[pre + '_w2'], f"{pre}_bwd_{l}")
    grads.update({pre + '_norm': dg,
                  pre + '_w1': _matmul_tn(hb, da, f"{pre}_dw1_{l}"),
                  pre + '_w3': _matmul_tn(hb, db, f"{pre}_dw3_{l}"),
                  pre + '_w2': _matmul_tn(u, dyb, f"{pre}_dw2_{l}")})
    return dx


def _layer_bwd(grads, dx3, sv, w, l, lay, d_in, tabs, carry):
    T, D = dx3.shape
    ctab, stab, na_bias, onehot = tabs
    col = lambda name: lay[name][0] // lay[name][1]
    z = sv['z']
    dx2 = _ffn_grads(grads, sv['x2'], dx3, sv['a2'], sv['b2'], w, 'ffn2', l, carry)
    w_br = (w['w_br_na'], w['w_br_gla'], w['w_br_mla'])
    (dy_na, do_gla, dy_mla, dgr, dgates, dw0, dw1, dw2, dwo, dgn) = _mix_out_bwd(
        dx2, sv['y_na'], sv['o_gla'], sv['y_mla'], z, col('gr'), w['gla_out_norm'], w_br, w['w_out'], f"mix_out_bwd_{l}")
    grads.update(w_br_na=dw0, w_br_gla=dw1, w_br_mla=dw2, w_out=dwo, gla_out_norm=dgn)
    dq, dk, dv, dbias, dgq, dgk = _na_bwd(sv['na_q'], sv['na_k'], sv['na_v'], dy_na, na_bias,
                                          w['na_q_norm'], w['na_k_norm'], f"na_bwd_{l}")
    grads.update(na_q_norm=dgq, na_k_norm=dgk, na_rpb=_na_bias_tables_bwd(dbias, onehot))
    gdq, gdk, gdv, gdg = _gla_bwd(sv['gq'], sv['gk'], sv['gv'], sv['gg'], do_gla, f"gla_bwd_{l}")
    fold = _unheads
    dglow, dupf, dupb, dbf, dbb = _gla_gate_bwd(z, col('glow'), w['up_f'], w['up_b'], w['gla_gf_bias'], w['gla_gb_bias'],
                                                _unheads(gdg[0]), _unheads(gdg[1]), f"gla_gate_bwd_{l}")
    grads.update(gla_gf_up=dupf[:GLA_RANK], gla_gb_up=dupb[GLA_RANK:2 * GLA_RANK], gla_gf_bias=dbf, gla_gb_bias=dbb)
    mdq, mdk, mdv = _carried(_mla_attn_bwd, carry, 'mla', sv['mq'], sv['mk'], sv['mv'], dy_mla, f"mla_attn_bwd_{l}")
    dq_raw, dgqn = _mla_prep_bwd(sv['mq_raw'], w['mla_q_norm'], ctab, stab, mdq, f"mla_qprep_bwd_{l}")
    dk_raw, dgkn = _mla_prep_bwd(sv['mk_raw'], w['mla_k_norm'], ctab, stab, mdk, f"mla_kprep_bwd_{l}")
    dq_raw = _unheads(dq_raw).astype(BF16)
    dkv_raw = _unheads(jnp.concatenate([dk_raw[..., :MLA_NOPE], mdv], axis=-1)).astype(BF16)
    dk_rope = jnp.sum(dk_raw[..., MLA_NOPE:], axis=0)
    dcq, dgcq = _norm_matmul_bwd(dq_raw, w['mla_w_uq'], z, col('c_q'), w['mla_cq_norm'], None, f"mla_uq_bwd_{l}")
    dckv, dgckv = _norm_matmul_bwd(dkv_raw, w['mla_w_ukv'], z, col('c_kv'), w['mla_ckv_norm'], None, f"mla_ukv_bwd_{l}")
    grads.update(mla_q_norm=dgqn, mla_k_norm=dgkn, mla_cq_norm=dgcq, mla_ckv_norm=dgckv,
                 mla_w_uq=_matmul_tn(sv['hq'], dq_raw, f"mla_duq_{l}", F32),
                 mla_w_ukv=_matmul_tn(sv['hkv'], dkv_raw, f"mla_dukv_{l}", F32))
    pieces = {'gates': dgates, 'na_q': _unheads(dq), 'na_k': _unheads(dk), 'na_v': _unheads(dv), 'gv': fold(gdv), 'gr': dgr,
              'gq': fold(gdq), 'gk': fold(gdk), 'c_q': dcq, 'c_kv': dckv, 'glow': dglow,
              'k_rope': jnp.pad(dk_rope, ((0, 0), (0, LANE - MLA_ROPE)))}
    dz = jnp.concatenate([pieces[name].astype(BF16) for name in lay], axis=1)
    dx1, dgmix = _carried(_norm_matmul_bwd, carry, 'mix_in', dz, w['w_in_z'], sv['x1'], 0, w['mix_norm'], dx2,
                          f"mix_in_bwd_{l}")
    grads.update(mix_norm=dgmix, w_in=_from_z_cols(_matmul_tn(sv['hb'], dz, f"mix_dw_in_{l}"), lay, d_in))
    return _ffn_grads(grads, sv['x0'], dx1, sv['a1'], sv['b1'], w, 'ffn1', l, carry)


EXCHANGE_PARTS = {
    'F1a': [['ffn1_w1', 'ffn1_w3']],
    'F1b': [['ffn1_w2']],
    'F2': [['ffn2_w1', 'ffn2_w3'], ['ffn2_w2']],
    'C': [['w_in']],
    'S': [['w_br_na', 'w_br_gla', 'w_br_mla'], ['w_out'], ['mla_w_uq'], ['mla_w_ukv'], ['gla_gf_up', 'gla_gb_up']],
}
FWD_PLAN = {'ffn1': [(0, 'C')], 'mix_in': [(0, 'S')], 'mla': [(0, 'F2')], 'na': [(1, 'F1b')], 'ffn2': [(1, 'F1a')]}
BWD_PLAN = {'mla': [(0, 'F2')], 'mix_in': [(0, 'S')], 'ffn1': [(0, 'C')], 'ffn2': [(1, 'F1a'), (1, 'F1b')]}
EDGE_PARTS = ['F1a', 'F1b']


def _carried(fn, carry, key, *args):
    if key in carry:
        make_srcs, broadcast, deliver = carry[key]
        outs, landed = fn(*args, exch=(make_srcs(), broadcast))
        deliver(landed)
        return outs
    return fn(*args)


def kernel(x, ffn1_norm, ffn1_w1, ffn1_w3, ffn1_w2, mix_norm, w_in, na_q_norm, na_k_norm, na_rpb, gla_gf_up, gla_gf_bias, gla_gb_up, gla_gb_bias, gla_out_norm, mla_cq_norm, mla_ckv_norm, mla_w_uq, mla_w_ukv, mla_q_norm, mla_k_norm, w_br_na, w_br_gla, w_br_mla, w_out, ffn2_norm, ffn2_w1, ffn2_w3, ffn2_w2, loss_target, m_ffn1_norm, m_ffn1_w1, m_ffn1_w3, m_ffn1_w2, m_mix_norm, m_w_in, m_na_q_norm, m_na_k_norm, m_na_rpb, m_gla_gf_up, m_gla_gf_bias, m_gla_gb_up, m_gla_gb_bias, m_gla_out_norm, m_mla_cq_norm, m_mla_ckv_norm, m_mla_w_uq, m_mla_w_ukv, m_mla_q_norm, m_mla_k_norm, m_w_br_na, m_w_br_gla, m_w_br_mla, m_w_out, m_ffn2_norm, m_ffn2_w1, m_ffn2_w3, m_ffn2_w2, v_ffn1_norm, v_ffn1_w1, v_ffn1_w3, v_ffn1_w2, v_mix_norm, v_w_in, v_na_q_norm, v_na_k_norm, v_na_rpb, v_gla_gf_up, v_gla_gf_bias, v_gla_gb_up, v_gla_gb_bias, v_gla_out_norm, v_mla_cq_norm, v_mla_ckv_norm, v_mla_w_uq, v_mla_w_ukv, v_mla_q_norm, v_mla_k_norm, v_w_br_na, v_w_br_gla, v_w_br_mla, v_w_out, v_ffn2_norm, v_ffn2_w1, v_ffn2_w3, v_ffn2_w2):
    given = dict(locals())
    P = {n: given[n] for n in WEIGHTS}
    M = {n: given['m_' + n] for n in WEIGHTS}
    V = {n: given['v_' + n] for n in WEIGHTS}
    xs = x[0]
    T, D = xs.shape
    L = ffn1_norm.shape[0]
    lay, zp, d_in = _z_layout(D)

    def arrays_of(items):
        return [(l, names) for l, p in items for names in EXCHANGE_PARTS[p]]

    ws = [{n: P[n][l][None, :] for n in REPLICATED if n != 'na_rpb'} for l in range(L)]

    def gather_srcs(items):
        return [jnp.stack([P[n][l].astype(BF16) for n in names]) for l, names in arrays_of(items)]

    def take_gathered(items, res):
        for (l, names), g in zip(arrays_of(items), res):
            for k, n in enumerate(names):
                blk = g[:, k]
                if n in COL_SHARDED:
                    full = jnp.transpose(blk, (1, 0, 2)).reshape(blk.shape[1], N_DEV * blk.shape[2])
                else:
                    full = blk.reshape(N_DEV * blk.shape[1], blk.shape[2])
                if n == 'w_in':
                    ws[l]['w_in_z'] = _to_z_cols(full, lay, zp)
                elif n == 'gla_gf_up':
                    ws[l]['up_f'] = jnp.zeros((LANE, GLA_QK_W), BF16).at[:GLA_RANK].set(full)
                elif n == 'gla_gb_up':
                    ws[l]['up_b'] = jnp.zeros((LANE, GLA_QK_W), BF16).at[GLA_RANK:2 * GLA_RANK].set(full)
                else:
                    ws[l][n] = full

    def plan(table, l, make_srcs, broadcast, deliver):
        carry = {}
        for key, items in table.items():
            items = [(l + off, p) for off, p in items if l + off < L]
            if items:
                carry[key] = (functools.partial(make_srcs, items), broadcast, functools.partial(deliver, items))
        return carry

    ctab, stab = _rope_tables(T)
    na_bias, onehot = _na_bias_tables(na_rpb)

    edge = [(0, p) for p in EDGE_PARTS]
    take_gathered(edge, _exchange(gather_srcs(edge), True, "gather_weights_edge"))
    saved = []
    h = xs
    for l in range(L):
        h, sv = _layer_fwd(h, ws[l], l, lay, (ctab, stab, na_bias[l]), plan(FWD_PLAN, l, gather_srcs, True, take_gathered))
        saved.append(sv)
    dh, loss_local = _loss_head(h, loss_target[0], "loss_head")
    loss = lax.psum(loss_local[0, 0], ("x", "y", "c"))

    layer_grads = [dict() for _ in range(L)]

    def scatter_srcs(items):
        def blocks(l, n):
            g = layer_grads[l][n]
            if n in COL_SHARDED:
                return jnp.transpose(g.reshape(g.shape[0], N_DEV, -1), (1, 0, 2)).astype(BF16)
            return g.reshape(N_DEV, -1, g.shape[1]).astype(BF16)
        return [jnp.stack([blocks(l, n) for n in names], axis=1) for l, names in arrays_of(items)]

    landed = [dict() for _ in range(L)]

    def take_landed(items, res):
        for (l, names), r in zip(arrays_of(items), res):
            for k, n in enumerate(names):
                landed[l][n] = (r, k)

    for l in reversed(range(L)):
        dh = _layer_bwd(layer_grads[l], dh, saved[l], ws[l], l, lay, d_in, (ctab, stab, na_bias[l], onehot),
                        plan(BWD_PLAN, l, scatter_srcs, False, take_landed))
    grad_x = dh[None]
    take_landed(edge, _exchange(scatter_srcs(edge), False, "scatter_grads_edge"))

    out = {}
    for n in SHARDED:
        out[n] = _adamw_sharded([landed[l][n][0] for l in range(L)], landed[0][n][1], P[n], M[n], V[n], f"adamw_{n}")

    rep = [n for n in REPLICATED]
    rep_shapes = [tuple(P[n].shape) for n in rep]
    rep_partial = _pack([jnp.stack([layer_grads[l][n].reshape(P[n].shape[1:]) for l in range(L)]) for n in rep])
    (rep_landed,) = _exchange([rep_partial], True, "gather_small_grads")
    packed = _adamw_replicated(rep_landed, _pack([P[n] for n in rep]), _pack([M[n] for n in rep]),
                               _pack([V[n] for n in rep]), "adamw_replicated")
    for kind, pk in enumerate(packed):
        for n, arr in zip(rep, _unpack(pk, rep_shapes)):
            out.setdefault(n, [None] * 4)[kind] = arr

    res = [loss, grad_x]
    for kind in range(4):
        res += [out[n][kind] for n in WEIGHTS]
    return tuple(res)
```

```python
import functools

import numpy as np
import jax
import jax.numpy as jnp
from jax import lax
from jax.experimental import pallas as pl
from jax.experimental.pallas import tpu as pltpu

F32 = jnp.float32
BF16 = jnp.bfloat16
SDS = jax.ShapeDtypeStruct
HI = lax.Precision.HIGHEST

N_DEV = 8
DEPTH = 4
EPS = 1e-6
LANE = 128
VMEM_LIMIT = 56 * 1024 * 1024

GRID_W = 64
NA_HEADS, NA_HD, NA_WIN_R, NA_WIN_C = 8, 64, 8, 16
GLA_HEADS, GLA_DK, GLA_DV, GLA_RANK, GLA_TAU, GLA_CHUNK = 4, 64, 128, 16, 16.0, 64
MLA_HEADS, MLA_QR, MLA_KVR, MLA_NOPE, MLA_ROPE, MLA_V = 4, 256, 256, 128, 64, 128
MLA_QK = MLA_NOPE + MLA_ROPE
ROPE_THETA = 10000.0
NA_W = NA_HEADS * NA_HD
GLA_QK_W = GLA_HEADS * GLA_DK
GLA_V_W = GLA_HEADS * GLA_DV
MLA_V_W = MLA_HEADS * MLA_V

ADAM_LR, ADAM_B1, ADAM_B2, ADAM_EPS, ADAM_WD, ADAM_STEP = 0.001, 0.9, 0.999, 1e-08, 0.01, 10

WEIGHTS = ['ffn1_norm', 'ffn1_w1', 'ffn1_w3', 'ffn1_w2', 'mix_norm', 'w_in', 'na_q_norm', 'na_k_norm', 'na_rpb',
           'gla_gf_up', 'gla_gf_bias', 'gla_gb_up', 'gla_gb_bias', 'gla_out_norm', 'mla_cq_norm', 'mla_ckv_norm',
           'mla_w_uq', 'mla_w_ukv', 'mla_q_norm', 'mla_k_norm', 'w_br_na', 'w_br_gla', 'w_br_mla', 'w_out',
           'ffn2_norm', 'ffn2_w1', 'ffn2_w3', 'ffn2_w2']
COL_SHARDED = ['ffn1_w1', 'ffn1_w3', 'w_in', 'gla_gf_up', 'gla_gb_up', 'mla_w_uq', 'mla_w_ukv', 'w_br_na', 'w_br_gla',
               'w_br_mla', 'ffn2_w1', 'ffn2_w3']
ROW_SHARDED = ['ffn1_w2', 'w_out', 'ffn2_w2']
SHARDED = [n for n in WEIGHTS if n in COL_SHARDED or n in ROW_SHARDED]
REPLICATED = [n for n in WEIGHTS if n not in SHARDED]

NT = (((1,), (1,)), ((), ()))
TN = (((0,), (0,)), ((), ()))


def _tile(n, pref, mult=8):
    best = None
    for t in range(mult, min(n, pref) + 1, mult):
        if n % t == 0:
            best = t
    return best if best is not None else n


def _params(n_axes):
    return pltpu.CompilerParams(dimension_semantics=("arbitrary",) * n_axes, vmem_limit_bytes=VMEM_LIMIT)


def _dot(a, b, dims=None, precision=None):
    if dims is None:
        return jnp.dot(a, b, preferred_element_type=F32, precision=precision)
    return lax.dot_general(a, b, dims, preferred_element_type=F32, precision=precision)


def _sigmoid(x):
    return 1.0 / (1.0 + jnp.exp(-x))


def _rstd(x):
    return lax.rsqrt(jnp.mean(x * x, axis=-1, keepdims=True) + EPS)


def _rms_bwd(dy, x, g, r):
    xh = x * r
    dyg = dy * g
    dx = r * (dyg - xh * jnp.mean(dyg * xh, axis=-1, keepdims=True))
    return dx, jnp.sum(dy * xh, axis=0, keepdims=True)


def _z_layout(d_model):
    own = {}
    off = 0
    for name, w in [('na_q', NA_W), ('na_k', NA_W), ('na_v', NA_W), ('gq', GLA_QK_W), ('gk', GLA_QK_W),
                    ('gv', GLA_V_W), ('gr', GLA_V_W), ('gfl', GLA_RANK), ('gbl', GLA_RANK), ('c_q', MLA_QR),
                    ('c_kv', MLA_KVR), ('k_rope', MLA_ROPE), ('gates', 3 * d_model)]:
        own[name] = (off, w)
        off += w
    order = [('gates', 3 * d_model), ('na_q', NA_W), ('na_k', NA_W), ('na_v', NA_W), ('gv', GLA_V_W), ('gr', GLA_V_W),
             ('gq', GLA_QK_W), ('gk', GLA_QK_W), ('c_q', MLA_QR), ('c_kv', MLA_KVR), ('glow', LANE), ('k_rope', LANE)]
    lay = {}
    z = 0
    for name, w in order:
        assert z % w == 0, (name, z, w)
        if name == 'glow':
            lay[name] = (z, w, own['gfl'][0], 2 * GLA_RANK)
        else:
            lay[name] = (z, w, own[name][0], own[name][1])
        z += w
    return lay, z, off


def _z_cols_from_blocks(blocks, lay):
    cs = blocks.shape[2]
    parts = []
    for zo, zw, oo, ow in lay.values():
        a = oo
        while a < oo + ow:
            b = min(oo + ow, (a // cs + 1) * cs)
            parts.append(blocks[a // cs, :, a % cs:a % cs + (b - a)])
            a = b
        if zw > ow:
            parts.append(jnp.zeros((blocks.shape[1], zw - ow), blocks.dtype))
    return jnp.concatenate(parts, axis=1)


def _blocks_from_z_cols(wz, lay, d_in):
    cs = d_in // N_DEV
    own = sorted(lay.values(), key=lambda t: t[2])
    out = []
    for k in range(N_DEV):
        parts = []
        for zo, zw, oo, ow in own:
            a, b = max(oo, k * cs), min(oo + ow, (k + 1) * cs)
            if a < b:
                parts.append(wz[:, zo + a - oo:zo + b - oo])
        out.append(jnp.concatenate(parts, axis=1))
    return jnp.stack(out)


def _ffn_fwd(x, g, w1, w3, w2, name, exch=None):
    T, D = x.shape
    Fd = w1.shape[1]
    tm, tf = _tile(T, 1024), _tile(Fd, 256, LANE)
    nj = Fd // tf

    def body(x_ref, g_ref, w1_ref, w3_ref, w2_ref, o_ref, a_ref, b_ref, h_scr, acc):
        j = pl.program_id(1)

        @pl.when(j == 0)
        def _():
            xv = x_ref[...]
            h_scr[...] = (xv * _rstd(xv) * g_ref[...]).astype(BF16)
            acc[...] = jnp.zeros_like(acc)

        h = h_scr[...]
        a = _dot(h, w1_ref[...])
        b = _dot(h, w3_ref[...])
        a_ref[...] = a.astype(BF16)
        b_ref[...] = b.astype(BF16)
        u = a * _sigmoid(a) * b
        acc[...] += _dot(u.astype(BF16), w2_ref[...])

        @pl.when(j == nj - 1)
        def _():
            o_ref[...] = x_ref[...] + 0.5 * acc[...]

    return _call(
        body, name=name, grid=(T // tm, nj),
        in_specs=[pl.BlockSpec((tm, D), lambda i, j: (i, 0)), pl.BlockSpec((1, D), lambda i, j: (0, 0)),
                  pl.BlockSpec((D, tf), lambda i, j: (0, j)), pl.BlockSpec((D, tf), lambda i, j: (0, j)),
                  pl.BlockSpec((tf, D), lambda i, j: (j, 0))],
        out_specs=[pl.BlockSpec((tm, D), lambda i, j: (i, 0)), pl.BlockSpec((tm, tf), lambda i, j: (i, j)),
                   pl.BlockSpec((tm, tf), lambda i, j: (i, j))],
        out_shape=[SDS((T, D), F32), SDS((T, Fd), BF16), SDS((T, Fd), BF16)],
        scratch_shapes=[pltpu.VMEM((tm, D), BF16), pltpu.VMEM((tm, D), F32)],
        args=(x, g, w1, w3, w2), exch=exch)


def _ffn_bwd(x, g, dy, a, b, w1, w3, w2, name, exch=None):
    T, D = x.shape
    Fd = w1.shape[1]
    tm, tf = _tile(T, 512), _tile(Fd, 256, LANE)
    nj = Fd // tf

    def body(x_ref, g_ref, dy_ref, a_ref, b_ref, w1_ref, w3_ref, w2_ref,
             dx_ref, da_ref, db_ref, u_ref, hb_ref, dyb_ref, dg_ref, dh_acc):
        i, j = pl.program_id(0), pl.program_id(1)

        @pl.when(j == 0)
        def _():
            xv = x_ref[...]
            hb_ref[...] = (xv * _rstd(xv) * g_ref[...]).astype(BF16)
            dyb_ref[...] = (0.5 * dy_ref[...]).astype(BF16)
            dh_acc[...] = jnp.zeros_like(dh_acc)

        @pl.when((i == 0) & (j == 0))
        def _():
            dg_ref[...] = jnp.zeros_like(dg_ref)

        du = _dot(dyb_ref[...], w2_ref[...], NT)
        av = a_ref[...].astype(F32)
        bv = b_ref[...].astype(F32)
        s = _sigmoid(av)
        sl = av * s
        da = (du * bv * (s * (1.0 + av * (1.0 - s)))).astype(BF16)
        db = (du * sl).astype(BF16)
        da_ref[...] = da
        db_ref[...] = db
        u_ref[...] = (sl * bv).astype(BF16)
        dh_acc[...] += _dot(da, w1_ref[...], NT) + _dot(db, w3_ref[...], NT)

        @pl.when(j == nj - 1)
        def _():
            xv = x_ref[...]
            dxn, dg = _rms_bwd(dh_acc[...], xv, g_ref[...], _rstd(xv))
            dx_ref[...] = dy_ref[...] + dxn
            dg_ref[...] += dg

    row = lambda i, j: (i, 0)
    return _call(
        body, name=name, grid=(T // tm, nj),
        in_specs=[pl.BlockSpec((tm, D), row), pl.BlockSpec((1, D), lambda i, j: (0, 0)), pl.BlockSpec((tm, D), row),
                  pl.BlockSpec((tm, tf), lambda i, j: (i, j)), pl.BlockSpec((tm, tf), lambda i, j: (i, j)),
                  pl.BlockSpec((D, tf), lambda i, j: (0, j)), pl.BlockSpec((D, tf), lambda i, j: (0, j)),
                  pl.BlockSpec((tf, D), lambda i, j: (j, 0))],
        out_specs=[pl.BlockSpec((tm, D), row), pl.BlockSpec((tm, tf), lambda i, j: (i, j)),
                   pl.BlockSpec((tm, tf), lambda i, j: (i, j)), pl.BlockSpec((tm, tf), lambda i, j: (i, j)),
                   pl.BlockSpec((tm, D), row), pl.BlockSpec((tm, D), row), pl.BlockSpec((1, D), lambda i, j: (0, 0))],
        out_shape=[SDS((T, D), F32), SDS((T, Fd), BF16), SDS((T, Fd), BF16), SDS((T, Fd), BF16),
                   SDS((T, D), BF16), SDS((T, D), BF16), SDS((1, D), F32)],
        scratch_shapes=[pltpu.VMEM((tm, D), F32)],
        args=(x, g, dy, a, b, w1, w3, w2), exch=exch)


def _matmul_tn(a, b, name, out_dtype=BF16):
    T, K = a.shape
    N = b.shape[1]
    tk = _tile(K, 1408, LANE)
    tn = _tile(N, 1408, LANE)
    tt = _tile(T, 1024)
    nt = T // tt

    def body(a_ref, b_ref, o_ref, acc):
        t = pl.program_id(2)

        @pl.when(t == 0)
        def _():
            acc[...] = jnp.zeros_like(acc)

        acc[...] += _dot(a_ref[...], b_ref[...], TN)

        @pl.when(t == nt - 1)
        def _():
            o_ref[...] = acc[...].astype(out_dtype)

    return pl.pallas_call(
        body, name=name, grid=(K // tk, N // tn, nt),
        in_specs=[pl.BlockSpec((tt, tk), lambda k, n, t: (t, k)), pl.BlockSpec((tt, tn), lambda k, n, t: (t, n))],
        out_specs=pl.BlockSpec((tk, tn), lambda k, n, t: (k, n)),
        out_shape=SDS((K, N), out_dtype),
        scratch_shapes=[pltpu.VMEM((tk, tn), F32)],
        compiler_params=_params(3),
    )(a, b)


def _norm_matmul(x, xcol, g, w, name, exch=None):
    T = x.shape[0]
    D = g.shape[1]
    N = w.shape[1]
    tm, tn = _tile(T, 1024), _tile(N, 768, LANE)

    def body(x_ref, g_ref, w_ref, z_ref, hb_ref):
        @pl.when(pl.program_id(1) == 0)
        def _():
            xv = x_ref[...]
            hb_ref[...] = (xv * _rstd(xv) * g_ref[...]).astype(BF16)

        z_ref[...] = _dot(hb_ref[...], w_ref[...])

    return _call(
        body, name=name, grid=(T // tm, N // tn),
        in_specs=[pl.BlockSpec((tm, D), lambda i, j: (i, xcol)), pl.BlockSpec((1, D), lambda i, j: (0, 0)),
                  pl.BlockSpec((D, tn), lambda i, j: (0, j))],
        out_specs=[pl.BlockSpec((tm, tn), lambda i, j: (i, j)), pl.BlockSpec((tm, D), lambda i, j: (i, 0))],
        out_shape=[SDS((T, N), F32), SDS((T, D), BF16)],
        args=(x, g, w), exch=exch)


def _norm_matmul_bwd(dz, w, x, xcol, g, resid, name, exch=None):
    T, N = dz.shape
    D = g.shape[1]
    tm, tn = _tile(T, 1024), _tile(N, 768, LANE)
    nj = N // tn
    has_resid = resid is not None

    def body(*refs):
        if has_resid:
            dz_ref, w_ref, x_ref, g_ref, r_ref, dx_ref, dg_ref, acc = refs
        else:
            dz_ref, w_ref, x_ref, g_ref, dx_ref, dg_ref, acc = refs
        i, j = pl.program_id(0), pl.program_id(1)

        @pl.when(j == 0)
        def _():
            acc[...] = jnp.zeros_like(acc)

        @pl.when((i == 0) & (j == 0))
        def _():
            dg_ref[...] = jnp.zeros_like(dg_ref)

        acc[...] += _dot(dz_ref[...], w_ref[...], NT)

        @pl.when(j == nj - 1)
        def _():
            xv = x_ref[...]
            dxn, dg = _rms_bwd(acc[...], xv, g_ref[...], _rstd(xv))
            dx_ref[...] = dxn + r_ref[...] if has_resid else dxn
            dg_ref[...] += dg

    in_specs = [pl.BlockSpec((tm, tn), lambda i, j: (i, j)), pl.BlockSpec((D, tn), lambda i, j: (0, j)),
                pl.BlockSpec((tm, D), lambda i, j: (i, xcol)), pl.BlockSpec((1, D), lambda i, j: (0, 0))]
    args = [dz, w, x, g]
    if has_resid:
        in_specs.append(pl.BlockSpec((tm, D), lambda i, j: (i, 0)))
        args.append(resid)
    return _call(
        body, name=name, grid=(T // tm, nj), in_specs=in_specs,
        out_specs=[pl.BlockSpec((tm, D), lambda i, j: (i, 0)), pl.BlockSpec((1, D), lambda i, j: (0, 0))],
        out_shape=[SDS((T, D), F32), SDS((1, D), F32)],
        scratch_shapes=[pltpu.VMEM((tm, D), F32)],
        args=args, exch=exch)


def _na_bias_tables(rpb):
    L = rpb.shape[0]
    qc = np.arange(GRID_W)[:, None]
    kc = np.arange(GRID_W)[None, :]
    dc = np.clip(kc - qc + NA_WIN_C - 1, 0, 2 * NA_WIN_C - 2)
    c0 = np.clip(qc - NA_WIN_C // 2, 0, GRID_W - NA_WIN_C)
    ok = (kc >= c0) & (kc < c0 + NA_WIN_C)
    onehot = (dc.reshape(-1)[None, :] == np.arange(2 * NA_WIN_C - 1)[:, None]).astype(np.float32)
    t1 = jnp.stack([rpb[:, :, c:c + NA_WIN_R, :] for c in range(NA_WIN_R)], axis=2)
    full = jnp.dot(t1.reshape(-1, 2 * NA_WIN_C - 1), jnp.asarray(onehot), precision=HI)
    full = full.reshape(L, NA_HEADS, NA_WIN_R, NA_WIN_R, GRID_W, GRID_W)
    full = jnp.where(jnp.asarray(ok)[None, None, None, None], full, -1e30)
    return jnp.transpose(full, (0, 1, 2, 4, 3, 5)).reshape(L, NA_HEADS, NA_WIN_R, GRID_W, NA_WIN_R * GRID_W), onehot


def _na_bias_tables_bwd(db, onehot):
    d = db.reshape(NA_HEADS, NA_WIN_R, GRID_W, NA_WIN_R, GRID_W)
    d = jnp.transpose(d, (0, 1, 3, 2, 4)).reshape(-1, GRID_W * GRID_W)
    t1 = jnp.dot(d, jnp.asarray(onehot.T), precision=HI).reshape(NA_HEADS, NA_WIN_R, NA_WIN_R, 2 * NA_WIN_C - 1)
    out = jnp.zeros((NA_HEADS, 2 * NA_WIN_R - 1, 2 * NA_WIN_C - 1), F32)
    for c in range(NA_WIN_R):
        out = out.at[:, c:c + NA_WIN_R, :].add(t1[:, c])
    return out


def _na_windows(bi, rb, rows):
    wsl, cls = [], []
    for i in range(rb):
        r = bi * rb + i
        r0 = jnp.clip(r - NA_WIN_R // 2, 0, rows - NA_WIN_R)
        wsl.append(pl.ds(pl.multiple_of(r0 * GRID_W, GRID_W), NA_WIN_R * GRID_W))
        cls.append(r0 - r + NA_WIN_R - 1)
    return wsl, cls


def _na_fwd(q, k, v, bias, gq, gk, name, exch=None):
    H, T, d = q.shape
    rows = T // GRID_W
    rb = _tile(rows, 8, 1)
    win = NA_WIN_R * GRID_W
    scale = d ** -0.5

    def body(q_ref, k_ref, v_ref, b_ref, gq_ref, gk_ref, o_ref, qn, kn, vb):
        qv, kv = q_ref[0], k_ref[0]
        qn[...] = (qv * _rstd(qv) * gq_ref[...] * scale).astype(BF16)
        kn[...] = (kv * _rstd(kv) * gk_ref[...]).astype(BF16)
        vb[...] = v_ref[0].astype(BF16)

        def block(bi, c):
            wsl, cls = _na_windows(bi, rb, rows)
            qsl = pl.ds(pl.multiple_of(bi * (rb * GRID_W), rb * GRID_W), rb * GRID_W)
            kw = jnp.stack([kn[w, :] for w in wsl])
            vw = jnp.stack([vb[w, :] for w in wsl])
            s = _bdot(qn[qsl, :].reshape(rb, GRID_W, d), kw, 2, 2) + jnp.stack([b_ref[0, c_] for c_ in cls])
            p = jnp.exp(s - jnp.max(s, axis=-1, keepdims=True))
            p = p / jnp.sum(p, axis=-1, keepdims=True)
            o_ref[0, qsl, :] = _bdot(p.astype(BF16), vw, 2, 1).reshape(rb * GRID_W, d)
            return c

        lax.fori_loop(0, rows // rb, block, 0)

    hm = pl.BlockSpec((1, T, d), lambda h: (h, 0, 0))
    gs = pl.BlockSpec((1, d), lambda h: (0, 0))
    return _call(
        body, name=name, grid=(H,),
        in_specs=[hm, hm, hm, pl.BlockSpec((1, NA_WIN_R, GRID_W, win), lambda h: (h, 0, 0, 0)), gs, gs],
        out_specs=[hm], out_shape=[SDS((H, T, d), F32)],
        scratch_shapes=[pltpu.VMEM((T, d), BF16)] * 3,
        args=(q, k, v, bias, gq, gk), exch=exch)


def _na_bwd(q, k, v, do, bias, gq, gk, name):
    H, T, d = q.shape
    rows = T // GRID_W
    rb = _tile(rows, 8, 1)
    win = NA_WIN_R * GRID_W
    scale = d ** -0.5

    def body(q_ref, k_ref, v_ref, do_ref, b_ref, gq_ref, gk_ref,
             dq_ref, dk_ref, dv_ref, db_ref, dgq_ref, dgk_ref, qn, kn, vb, dob, dqn, dkn):
        h = pl.program_id(0)
        qv, kv = q_ref[0], k_ref[0]
        rq, rk = _rstd(qv), _rstd(kv)
        qn[...] = (qv * rq * gq_ref[...] * scale).astype(BF16)
        kn[...] = (kv * rk * gk_ref[...]).astype(BF16)
        vb[...] = v_ref[0].astype(BF16)
        dob[...] = do_ref[0].astype(BF16)
        dkn[...] = jnp.zeros_like(dkn)
        dv_ref[...] = jnp.zeros_like(dv_ref)
        db_ref[...] = jnp.zeros_like(db_ref)

        @pl.when(h == 0)
        def _():
            dgq_ref[...] = jnp.zeros_like(dgq_ref)
            dgk_ref[...] = jnp.zeros_like(dgk_ref)

        def block(bi, c):
            wsl, cls = _na_windows(bi, rb, rows)
            qsl = pl.ds(pl.multiple_of(bi * (rb * GRID_W), rb * GRID_W), rb * GRID_W)
            qs = qn[qsl, :].reshape(rb, GRID_W, d)
            dos = dob[qsl, :].reshape(rb, GRID_W, d)
            kw = jnp.stack([kn[w, :] for w in wsl])
            vw = jnp.stack([vb[w, :] for w in wsl])
            s = _bdot(qs, kw, 2, 2) + jnp.stack([b_ref[0, c_] for c_ in cls])
            p = jnp.exp(s - jnp.max(s, axis=-1, keepdims=True))
            p = p / jnp.sum(p, axis=-1, keepdims=True)
            dp = _bdot(dos, vw, 2, 2)
            ds = p * (dp - jnp.sum(dp * p, axis=-1, keepdims=True))
            dsb = ds.astype(BF16)
            dqn[qsl, :] = _bdot(dsb, kw, 2, 1).reshape(rb * GRID_W, d)
            dvw = _bdot(p.astype(BF16), dos, 1, 1)
            dkw = _bdot(dsb, qs, 1, 1)
            for i in range(rb):
                db_ref[0, cls[i]] += ds[i]
                dv_ref[0, wsl[i], :] += dvw[i]
                dkn[wsl[i], :] += dkw[i]
            return c

        lax.fori_loop(0, rows // rb, block, 0)
        dq, dgq = _rms_bwd(dqn[...] * scale, qv, gq_ref[...], rq)
        dk, dgk = _rms_bwd(dkn[...], kv, gk_ref[...], rk)
        dq_ref[0] = dq
        dk_ref[0] = dk
        dgq_ref[...] += dgq
        dgk_ref[...] += dgk

    hm = pl.BlockSpec((1, T, d), lambda h: (h, 0, 0))
    gs = pl.BlockSpec((1, d), lambda h: (0, 0))
    bs = pl.BlockSpec((1, NA_WIN_R, GRID_W, win), lambda h: (h, 0, 0, 0))
    return pl.pallas_call(
        body, name=name, grid=(H,),
        in_specs=[hm, hm, hm, hm, bs, gs, gs],
        out_specs=[hm, hm, hm, bs, gs, gs],
        out_shape=[SDS((H, T, d), F32)] * 3 + [SDS((H, NA_WIN_R, GRID_W, win), F32), SDS((1, d), F32), SDS((1, d), F32)],
        scratch_shapes=[pltpu.VMEM((T, d), BF16)] * 4 + [pltpu.VMEM((T, d), F32)] * 2,
        compiler_params=_params(1),
    )(q, k, v, do, bias, gq, gk)


def _gla_gate_fwd(z, col, upf, upb, bf, bb, name):
    T = z.shape[0]
    W = upf.shape[1]
    tm = _tile(T, 1024)

    def body(z_ref, upf_ref, upb_ref, bf_ref, bb_ref, g_ref):
        seg = z_ref[...].astype(BF16)
        for rev, (up_ref, b_ref) in enumerate(((upf_ref, bf_ref), (upb_ref, bb_ref))):
            pre = _dot(seg, up_ref[...]) + b_ref[...]
            g = (jnp.minimum(pre, 0.0) - jnp.log(1.0 + jnp.exp(-jnp.abs(pre)))) * (1.0 / GLA_TAU)
            for h in range(GLA_HEADS):
                g_ref[rev, h] = g[:, h * GLA_DK:(h + 1) * GLA_DK]

    full = lambda shape: pl.BlockSpec(shape, lambda i: (0, 0))
    return pl.pallas_call(
        body, name=name, grid=(T // tm,),
        in_specs=[pl.BlockSpec((tm, LANE), lambda i: (i, col)), full((LANE, W)), full((LANE, W)), full((1, W)), full((1, W))],
        out_specs=pl.BlockSpec((2, GLA_HEADS, tm, GLA_DK), lambda i: (0, 0, i, 0)),
        out_shape=SDS((2, GLA_HEADS, T, GLA_DK), F32),
        compiler_params=_params(1),
    )(z, upf, upb, bf, bb)


def _gla_gate_bwd(z, col, upf, upb, bf, bb, dg, name):
    T = z.shape[0]
    W = upf.shape[1]
    tm = _tile(T, 1024)

    def body(z_ref, upf_ref, upb_ref, bf_ref, bb_ref, dg_ref, dseg_ref, dupf_ref, dupb_ref, dbf_ref, dbb_ref):
        @pl.when(pl.program_id(0) == 0)
        def _():
            for r in (dupf_ref, dupb_ref, dbf_ref, dbb_ref):
                r[...] = jnp.zeros_like(r)

        seg = z_ref[...].astype(BF16)
        dseg = jnp.zeros(dseg_ref.shape, F32)
        for rev, (up_ref, b_ref, dup_ref, db_ref) in enumerate(((upf_ref, bf_ref, dupf_ref, dbf_ref),
                                                               (upb_ref, bb_ref, dupb_ref, dbb_ref))):
            pre = _dot(seg, up_ref[...]) + b_ref[...]
            dgv = jnp.concatenate([dg_ref[rev, h] for h in range(GLA_HEADS)], axis=1)
            dpre = dgv * (1.0 / GLA_TAU) * (1.0 - _sigmoid(pre))
            dpb = dpre.astype(BF16)
            dseg = dseg + _dot(dpb, up_ref[...], NT)
            dup_ref[...] += _dot(seg, dpb, TN)
            db_ref[...] += jnp.sum(dpre, axis=0, keepdims=True)
        dseg_ref[...] = dseg

    full = lambda shape: pl.BlockSpec(shape, lambda i: (0, 0))
    return pl.pallas_call(
        body, name=name, grid=(T // tm,),
        in_specs=[pl.BlockSpec((tm, LANE), lambda i: (i, col)), full((LANE, W)), full((LANE, W)), full((1, W)), full((1, W)),
                  pl.BlockSpec((2, GLA_HEADS, tm, GLA_DK), lambda i: (0, 0, i, 0))],
        out_specs=[pl.BlockSpec((tm, LANE), lambda i: (i, 0)), full((LANE, W)), full((LANE, W)), full((1, W)), full((1, W))],
        out_shape=[SDS((T, LANE), F32), SDS((LANE, W), F32), SDS((LANE, W), F32), SDS((1, W), F32), SDS((1, W), F32)],
        compiler_params=_params(1),
    )(z, upf, upb, bf, bb, dg)


def _bdot(a, b, ca, cb, precision=None):
    return lax.dot_general(a, b, (((ca,), (cb,)), ((0,), (0,))), preferred_element_type=F32, precision=precision)


def _gla_chunk_terms(q_ref, k_ref, v_ref, g_ref, rev, tok, n, C):
    dk, dv = q_ref.shape[-1], v_ref.shape[-1]
    q = q_ref[0, tok, :].reshape(n, C, dk) * (dk ** -0.5)
    k = k_ref[0, tok, :].reshape(n, C, dk)
    v = v_ref[0, tok, :].reshape(n, C, dv)
    g = g_ref[rev, 0, tok, :].reshape(n, C, dk)
    ii = lax.broadcasted_iota(jnp.int32, (C, C), 0)
    jj = lax.broadcasted_iota(jnp.int32, (C, C), 1)
    tri = jnp.broadcast_to(((ii <= jj) if rev else (ii >= jj)).astype(F32), (n, C, C))
    amask = ((jj > ii) if rev else (ii >= jj))[None]
    b = _bdot(tri, g, 2, 1, HI)
    bl = jnp.sum(g, axis=1, keepdims=True)
    eb = jnp.exp(b)
    enb = jnp.exp(-b)
    eend = jnp.exp(bl - b)
    return q, k, v, tri, amask, bl, eb, enb, eend


def _gla_segments(T, C):
    n = T // C
    ns = _tile(n, 16, 1)

    def tok(s):
        return pl.ds(pl.multiple_of(s * (ns * C), ns * C), ns * C)

    def chunks(s):
        return pl.ds(pl.multiple_of(s * ns, ns), ns)

    return n, ns, n // ns, tok, chunks


def _gla_fwd(q, k, v, g, name):
    H, T, dk = q.shape
    dv = v.shape[-1]
    C = GLA_CHUNK
    n, ns, nseg, seg_tok, seg_chunks = _gla_segments(T, C)

    def body(q_ref, k_ref, v_ref, g_ref, o_ref, upd_scr, dec_scr, st_scr):
        for rev in (0, 1):
            def intra(s, c):
                tok, ch = seg_tok(s), seg_chunks(s)
                q, k, v, tri, amask, bl, eb, enb, eend = _gla_chunk_terms(q_ref, k_ref, v_ref, g_ref, rev, tok, ns, C)
                vb = v.astype(BF16)
                a = jnp.where(amask, _bdot((q * eb).astype(BF16), (k * enb).astype(BF16), 2, 2), 0.0).astype(BF16)
                o = _bdot(a, vb, 2, 1).reshape(ns * C, dv)
                if rev:
                    o_ref[0, tok, :] += o
                else:
                    o_ref[0, tok, :] = o
                upd_scr[ch] = _bdot(vb, (k * eend).astype(BF16), 1, 1)
                dec_scr[ch] = jnp.exp(bl)
                return c

            lax.fori_loop(0, nseg, intra, 0)

            def step(t, st):
                i = n - 1 - t if rev else t
                st_scr[i] = st
                return st * dec_scr[i] + upd_scr[i]

            lax.fori_loop(0, n, step, jnp.zeros((dv, dk), F32))

            def inter(s, c):
                tok, ch = seg_tok(s), seg_chunks(s)
                q, k, v, tri, amask, bl, eb, enb, eend = _gla_chunk_terms(q_ref, k_ref, v_ref, g_ref, rev, tok, ns, C)
                o_ref[0, tok, :] += _bdot((q * eb).astype(BF16), st_scr[ch].astype(BF16), 2, 2).reshape(ns * C, dv)
                return c

            lax.fori_loop(0, nseg, inter, 0)

    sk = pl.BlockSpec((1, T, dk), lambda h: (h, 0, 0))
    sv = pl.BlockSpec((1, T, dv), lambda h: (h, 0, 0))
    sg = pl.BlockSpec((2, 1, T, dk), lambda h: (0, h, 0, 0))
    return pl.pallas_call(
        body, name=name, grid=(H,), in_specs=[sk, sk, sv, sg], out_specs=sv,
        out_shape=SDS((H, T, dv), F32),
        scratch_shapes=[pltpu.VMEM((n, dv, dk), F32), pltpu.VMEM((n, 1, dk), F32), pltpu.VMEM((n, dv, dk), F32)],
        compiler_params=_params(1),
    )(q, k, v, g)


def _gla_bwd(q, k, v, g, do, name):
    H, T, dk = q.shape
    dv = v.shape[-1]
    C = GLA_CHUNK
    n, ns, nseg, seg_tok, seg_chunks = _gla_segments(T, C)

    def one_scan(rev, q_ref, k_ref, v_ref, g_ref, do_ref, dq_ref, dk_ref, dv_ref, dg_ref, upd_scr, dec_scr, st_scr, gn_scr,
                 rn_scr):
        def first(s, c):
            tok, ch = seg_tok(s), seg_chunks(s)
            q, k, v, tri, amask, bl, eb, enb, eend = _gla_chunk_terms(q_ref, k_ref, v_ref, g_ref, rev, tok, ns, C)
            dob = do_ref[0, tok, :].reshape(ns, C, dv).astype(BF16)
            upd_scr[ch] = _bdot(v.astype(BF16), (k * eend).astype(BF16), 1, 1)
            dec_scr[ch] = jnp.exp(bl)
            gn_scr[ch] = _bdot(dob, (q * eb).astype(BF16), 1, 1)
            return c

        lax.fori_loop(0, nseg, first, 0)

        def fstep(t, st):
            i = n - 1 - t if rev else t
            st_scr[i] = st
            return st * dec_scr[i] + upd_scr[i]

        lax.fori_loop(0, n, fstep, jnp.zeros((dv, dk), F32))

        def bstep(t, r):
            i = t if rev else n - 1 - t
            rn_scr[i] = r
            return gn_scr[i] + r * dec_scr[i]

        lax.fori_loop(0, n, bstep, jnp.zeros((dv, dk), F32))

        def second(s, c):
            tok, ch = seg_tok(s), seg_chunks(s)
            q, k, v, tri, amask, bl, eb, enb, eend = _gla_chunk_terms(q_ref, k_ref, v_ref, g_ref, rev, tok, ns, C)
            qe, ke, kend = q * eb, k * enb, k * eend
            qeb, keb, kendb, vb = qe.astype(BF16), ke.astype(BF16), kend.astype(BF16), v.astype(BF16)
            dob = do_ref[0, tok, :].reshape(ns, C, dv).astype(BF16)
            a = jnp.where(amask, _bdot(qeb, keb, 2, 2), 0.0).astype(BF16)
            st = st_scr[ch]
            rn = rn_scr[ch]
            rnb = rn.astype(BF16)
            da = jnp.where(amask, _bdot(dob, vb, 2, 2), 0.0).astype(BF16)
            dvv = _bdot(a, dob, 1, 1) + _bdot(kendb, rnb, 2, 2)
            dqe = _bdot(da, keb, 2, 1) + _bdot(dob, st.astype(BF16), 2, 1)
            dke = _bdot(da, qeb, 1, 1)
            dkend = _bdot(vb, rnb, 2, 1)
            ddec = jnp.sum(rn * st, axis=1, keepdims=True)
            dbl = ddec * jnp.exp(bl) + jnp.sum(dkend * kend, axis=1, keepdims=True)
            db = dqe * qe - dke * ke - dkend * kend
            dg = _bdot(tri, db, 1, 1, HI) + dbl
            dqv = (dqe * eb * (dk ** -0.5)).reshape(ns * C, dk)
            dkv = (dke * enb + dkend * eend).reshape(ns * C, dk)
            if rev:
                dq_ref[0, tok, :] += dqv
                dk_ref[0, tok, :] += dkv
                dv_ref[0, tok, :] += dvv.reshape(ns * C, dv)
            else:
                dq_ref[0, tok, :] = dqv
                dk_ref[0, tok, :] = dkv
                dv_ref[0, tok, :] = dvv.reshape(ns * C, dv)
            dg_ref[rev, 0, tok, :] = dg.reshape(ns * C, dk)
            return c

        lax.fori_loop(0, nseg, second, 0)

    def body(*refs):
        one_scan(0, *refs)
        one_scan(1, *refs)

    sk = pl.BlockSpec((1, T, dk), lambda h: (h, 0, 0), pipeline_mode=pl.Buffered(1))
    sv = pl.BlockSpec((1, T, dv), lambda h: (h, 0, 0), pipeline_mode=pl.Buffered(1))
    sg = pl.BlockSpec((2, 1, T, dk), lambda h: (0, h, 0, 0), pipeline_mode=pl.Buffered(1))
    st_shape = pltpu.VMEM((n, dv, dk), F32)
    return pl.pallas_call(
        body, name=name, grid=(H,), in_specs=[sk, sk, sv, sg, sv], out_specs=[sk, sk, sv, sg],
        out_shape=[SDS((H, T, dk), F32), SDS((H, T, dk), F32), SDS((H, T, dv), F32), SDS((2, H, T, dk), F32)],
        scratch_shapes=[st_shape, pltpu.VMEM((n, 1, dk), F32), st_shape, st_shape, st_shape],
        compiler_params=_params(1),
    )(q, k, v, g, do)


def _rope_tables(T):
    half = MLA_ROPE // 2
    inv = ROPE_THETA ** (-jnp.arange(half, dtype=F32) / half)
    ang = jnp.arange(T, dtype=F32)[:, None] * inv[None, :]
    cos, sin = jnp.cos(ang), jnp.sin(ang)
    ctab = jnp.concatenate([jnp.ones((T, MLA_NOPE), F32), cos, cos], axis=1)
    stab = jnp.concatenate([jnp.zeros((T, MLA_NOPE), F32), -sin, sin], axis=1)
    return ctab, stab


def _swap_rope_halves(x):
    half = MLA_ROPE // 2
    return jnp.concatenate([x[:, :MLA_NOPE], x[:, MLA_NOPE + half:], x[:, MLA_NOPE:MLA_NOPE + half]], axis=1)


def _mla_prep_fwd(x, g, ctab, stab, name):
    H, T, d = x.shape
    tm = _tile(T, 1024)

    def body(x_ref, g_ref, c_ref, s_ref, o_ref):
        xv = x_ref[0]
        xn = xv * _rstd(xv) * g_ref[...]
        o_ref[0] = (xn * c_ref[...] + _swap_rope_halves(xn) * s_ref[...]).astype(BF16)

    tab = pl.BlockSpec((tm, d), lambda h, i: (i, 0))
    return pl.pallas_call(
        body, name=name, grid=(H, T // tm),
        in_specs=[pl.BlockSpec((1, tm, d), lambda h, i: (h, i, 0)), pl.BlockSpec((1, d), lambda h, i: (0, 0)), tab, tab],
        out_specs=pl.BlockSpec((1, tm, d), lambda h, i: (h, i, 0)),
        out_shape=SDS((H, T, d), BF16),
        compiler_params=_params(2),
    )(x, g, ctab, stab)


def _mla_prep_bwd(x, g, ctab, stab, dy, name):
    H, T, d = x.shape
    tm = _tile(T, 1024)

    def body(x_ref, g_ref, c_ref, s_ref, dy_ref, dx_ref, dg_ref):
        @pl.when((pl.program_id(0) == 0) & (pl.program_id(1) == 0))
        def _():
            dg_ref[...] = jnp.zeros_like(dg_ref)

        xv = x_ref[0]
        dyv = dy_ref[0]
        dxn = dyv * c_ref[...] + _swap_rope_halves(dyv * s_ref[...])
        dx, dg = _rms_bwd(dxn, xv, g_ref[...], _rstd(xv))
        dx_ref[0] = dx
        dg_ref[...] += dg

    tab = pl.BlockSpec((tm, d), lambda h, i: (i, 0))
    blk = pl.BlockSpec((1, tm, d), lambda h, i: (h, i, 0))
    gs = pl.BlockSpec((1, d), lambda h, i: (0, 0))
    return pl.pallas_call(
        body, name=name, grid=(H, T // tm),
        in_specs=[blk, gs, tab, tab, blk],
        out_specs=[blk, gs], out_shape=[SDS((H, T, d), F32), SDS((1, d), F32)],
        compiler_params=_params(2),
    )(x, g, ctab, stab, dy)


def _mla_attn_fwd(q, k, v, name, exch=None):
    H, T, d = q.shape
    dv = v.shape[-1]
    tq = _tile(T, 256)
    scale = d ** -0.5

    def body(q_ref, k_ref, v_ref, o_ref, lse_ref):
        s = _dot(q_ref[0], k_ref[0], NT) * scale
        m = jnp.max(s, axis=-1, keepdims=True)
        p = jnp.exp(s - m)
        l = jnp.sum(p, axis=-1, keepdims=True)
        o_ref[0] = _dot((p / l).astype(BF16), v_ref[0])
        lse_ref[0] = _col_to_row(m + jnp.log(l))

    return _call(
        body, name=name, grid=(H, T // tq),
        in_specs=[pl.BlockSpec((1, tq, d), lambda h, i: (h, i, 0)), pl.BlockSpec((1, T, d), lambda h, i: (h, 0, 0)),
                  pl.BlockSpec((1, T, dv), lambda h, i: (h, 0, 0))],
        out_specs=[pl.BlockSpec((1, tq, dv), lambda h, i: (h, i, 0)), pl.BlockSpec((1, 1, tq), lambda h, i: (h, 0, i))],
        out_shape=[SDS((H, T, dv), F32), SDS((H, 1, T), F32)],
        args=(q, k, v), exch=exch)


def _col_to_row(col):
    m = col.shape[0]
    eye = lax.broadcasted_iota(jnp.int32, (m, m), 0) == lax.broadcasted_iota(jnp.int32, (m, m), 1)
    return jnp.sum(jnp.where(eye, col, 0.0), axis=0, keepdims=True)


def _mla_attn_bwd(q, k, v, o, lse, do, name, exch=None):
    H, T, d = q.shape
    dv = v.shape[-1]
    tq = _tile(T, 256)
    tk = _tile(T, 512)
    scale = d ** -0.5

    def body(q_ref, k_ref, v_ref, o_ref, lse_ref, do_ref, dq_ref, dk_ref, dv_ref):
        @pl.when(pl.program_id(1) == 0)
        def _():
            dk_ref[...] = jnp.zeros_like(dk_ref)
            dv_ref[...] = jnp.zeros_like(dv_ref)

        qv = q_ref[0]
        dov = do_ref[0]
        dob = dov.astype(BF16)
        lse_row = lse_ref[0]
        delta_row = _col_to_row(jnp.sum(dov * o_ref[0], axis=-1, keepdims=True))

        def key_block(j, dq):
            ks = pl.ds(pl.multiple_of(j * tk, tk), tk)
            kb, vb = k_ref[0, ks, :], v_ref[0, ks, :]
            pt = jnp.exp(_dot(kb, qv, NT) * scale - lse_row)
            dst = (pt * (_dot(vb, dob, NT) - delta_row) * scale).astype(BF16)
            dv_ref[0, ks, :] += _dot(pt.astype(BF16), dob)
            dk_ref[0, ks, :] += _dot(dst, qv)
            return dq + _dot(dst, kb, TN)

        dq_ref[0] = lax.fori_loop(0, T // tk, key_block, jnp.zeros((tq, d), F32))

    qb = pl.BlockSpec((1, tq, d), lambda h, i: (h, i, 0))
    kb = pl.BlockSpec((1, T, d), lambda h, i: (h, 0, 0))
    vb = pl.BlockSpec((1, T, dv), lambda h, i: (h, 0, 0))
    ob = pl.BlockSpec((1, tq, dv), lambda h, i: (h, i, 0))
    return _call(
        body, name=name, grid=(H, T // tq),
        in_specs=[qb, kb, vb, ob, pl.BlockSpec((1, 1, tq), lambda h, i: (h, 0, i)), ob],
        out_specs=[qb, kb, vb],
        out_shape=[SDS((H, T, d), F32), SDS((H, T, d), F32), SDS((H, T, dv), F32)],
        args=(q, k, v, o, lse, do), exch=exch)


def _silu_parts(x):
    s = _sigmoid(x)
    return x * s, s * (1.0 + x * (1.0 - s))


def _mix_out_fwd(x1, y_na, o_gla, y_mla, z, gr_col, gnorm, wbr, w_out, name):
    T, D = x1.shape
    W = y_na.shape[0] * y_na.shape[2]
    tm = _tile(T, 512)

    def body(x_ref, na_ref, gla_ref, mla_ref, gt_ref, gr_ref, gn_ref, w0, w1, w2, wo_ref, o_ref):
        gn = gn_ref[...]
        nrm = jnp.concatenate([gla_ref[h] * _rstd(gla_ref[h]) * gn for h in range(GLA_HEADS)], axis=1)
        y_gla = nrm * _silu_parts(gr_ref[...])[0]
        y_na = jnp.concatenate([na_ref[h] for h in range(NA_HEADS)], axis=1)
        y_mla = jnp.concatenate([mla_ref[h] for h in range(MLA_HEADS)], axis=1)
        mixed = jnp.zeros((tm, D), F32)
        for i, (y, w_ref) in enumerate(((y_na, w0), (y_gla, w1), (y_mla, w2))):
            mixed = mixed + _sigmoid(gt_ref[:, i * D:(i + 1) * D]) * _dot(y.astype(BF16), w_ref[...])
        o_ref[...] = x_ref[...] + _dot(mixed.astype(BF16), wo_ref[...])

    tok = lambda w: pl.BlockSpec((tm, w), lambda i: (i, 0))
    hm = lambda a: pl.BlockSpec((a.shape[0], tm, a.shape[2]), lambda i: (0, i, 0))
    full = lambda shape: pl.BlockSpec(shape, lambda i: (0, 0))
    return pl.pallas_call(
        body, name=name, grid=(T // tm,),
        in_specs=[tok(D), hm(y_na), hm(o_gla), hm(y_mla), tok(3 * D), pl.BlockSpec((tm, W), lambda i: (i, gr_col)),
                  full((1, GLA_DV)), full((W, D)), full((W, D)), full((W, D)), full((D, D))],
        out_specs=tok(D), out_shape=SDS((T, D), F32),
        compiler_params=_params(1),
    )(x1, y_na, o_gla, y_mla, z, z, gnorm, wbr[0], wbr[1], wbr[2], w_out)


def _mix_out_bwd(dx2, y_na, o_gla, y_mla, z, gr_col, gnorm, wbr, w_out, name):
    T, D = dx2.shape
    W = y_na.shape[0] * y_na.shape[2]
    tm = _tile(T, 256)

    def body(dx_ref, na_ref, gla_ref, mla_ref, gt_ref, gr_ref, gn_ref, w0, w1, w2, wo_ref,
             dna_ref, dgla_ref, dmla_ref, dgr_ref, dgt_ref, dw0, dw1, dw2, dwo_ref, dgn_ref):
        @pl.when(pl.program_id(0) == 0)
        def _():
            for r in (dw0, dw1, dw2, dwo_ref, dgn_ref):
                r[...] = jnp.zeros_like(r)

        gn = gn_ref[...]
        grv = gr_ref[...]
        sil, dsil = _silu_parts(grv)
        rs = [_rstd(gla_ref[h]) for h in range(GLA_HEADS)]
        nrm = jnp.concatenate([gla_ref[h] * rs[h] * gn for h in range(GLA_HEADS)], axis=1)
        ys = (jnp.concatenate([na_ref[h] for h in range(NA_HEADS)], axis=1).astype(BF16), (nrm * sil).astype(BF16),
              jnp.concatenate([mla_ref[h] for h in range(MLA_HEADS)], axis=1).astype(BF16))
        dxb = dx_ref[...].astype(BF16)
        dmixed = _dot(dxb, wo_ref[...], NT)
        mixed = jnp.zeros((tm, D), F32)
        dys = []
        for i, (y, w_ref, dw_ref) in enumerate(zip(ys, (w0, w1, w2), (dw0, dw1, dw2))):
            gt = _sigmoid(gt_ref[:, i * D:(i + 1) * D])
            p = _dot(y, w_ref[...])
            mixed = mixed + gt * p
            dp = (dmixed * gt).astype(BF16)
            dgt_ref[:, i * D:(i + 1) * D] = (dmixed * p * gt * (1.0 - gt)).astype(BF16)
            dys.append(_dot(dp, w_ref[...], NT))
            dw_ref[...] += _dot(y, dp, TN)
        dwo_ref[...] += _dot(mixed.astype(BF16), dxb, TN)
        for h in range(NA_HEADS):
            dna_ref[h] = dys[0][:, h * NA_HD:(h + 1) * NA_HD]
        for h in range(MLA_HEADS):
            dmla_ref[h] = dys[2][:, h * MLA_V:(h + 1) * MLA_V]
        dgr_ref[...] = dys[1] * nrm * dsil
        dn = dys[1] * sil
        dgn = jnp.zeros((1, GLA_DV), F32)
        for h in range(GLA_HEADS):
            dxh, dgh = _rms_bwd(dn[:, h * GLA_DV:(h + 1) * GLA_DV], gla_ref[h], gn, rs[h])
            dgla_ref[h] = dxh
            dgn = dgn + dgh
        dgn_ref[...] += dgn

    tok = lambda w: pl.BlockSpec((tm, w), lambda i: (i, 0))
    hm = lambda a: pl.BlockSpec((a.shape[0], tm, a.shape[2]), lambda i: (0, i, 0))
    full = lambda shape: pl.BlockSpec(shape, lambda i: (0, 0))
    return pl.pallas_call(
        body, name=name, grid=(T // tm,),
        in_specs=[tok(D), hm(y_na), hm(o_gla), hm(y_mla), tok(3 * D), pl.BlockSpec((tm, W), lambda i: (i, gr_col)),
                  full((1, GLA_DV)), full((W, D)), full((W, D)), full((W, D)), full((D, D))],
        out_specs=[hm(y_na), hm(o_gla), hm(y_mla), tok(W), tok(3 * D), full((W, D)), full((W, D)), full((W, D)),
                   full((D, D)), full((1, GLA_DV))],
        out_shape=[SDS(y_na.shape, F32), SDS(o_gla.shape, F32), SDS(y_mla.shape, F32), SDS((T, W), F32),
                   SDS((T, 3 * D), BF16)] + [SDS((W, D), F32)] * 3 + [SDS((D, D), F32), SDS((1, GLA_DV), F32)],
        compiler_params=_params(1),
    )(dx2, y_na, o_gla, y_mla, z, z, gnorm, wbr[0], wbr[1], wbr[2], w_out)


def _split_z(z, lay, name):
    T = z.shape[0]
    tm = _tile(T, 512)
    na0, gv0, gq0 = lay['na_q'][0], lay['gv'][0], lay['gq'][0]
    assert na0 % (3 * NA_W) == 0 and gv0 % GLA_V_W == 0 and gq0 % (2 * GLA_QK_W) == 0
    assert lay['na_k'][0] == na0 + NA_W and lay['na_v'][0] == na0 + 2 * NA_W and lay['gk'][0] == gq0 + GLA_QK_W

    def body(na_ref, gv_ref, gqk_ref, q_ref, k_ref, v_ref, gq_ref, gk_ref, gvo_ref):
        for j, o_ref in enumerate((q_ref, k_ref, v_ref)):
            for h in range(NA_HEADS):
                o_ref[h] = na_ref[:, j * NA_W + h * NA_HD:j * NA_W + (h + 1) * NA_HD]
        for j, o_ref in enumerate((gq_ref, gk_ref)):
            for h in range(GLA_HEADS):
                o_ref[h] = gqk_ref[:, j * GLA_QK_W + h * GLA_DK:j * GLA_QK_W + (h + 1) * GLA_DK]
        for h in range(GLA_HEADS):
            gvo_ref[h] = gv_ref[:, h * GLA_DV:(h + 1) * GLA_DV]

    hm = lambda H, d: pl.BlockSpec((H, tm, d), lambda i: (0, i, 0))
    return pl.pallas_call(
        body, name=name, grid=(T // tm,),
        in_specs=[pl.BlockSpec((tm, 3 * NA_W), lambda i: (i, na0 // (3 * NA_W))),
                  pl.BlockSpec((tm, GLA_V_W), lambda i: (i, gv0 // GLA_V_W)),
                  pl.BlockSpec((tm, 2 * GLA_QK_W), lambda i: (i, gq0 // (2 * GLA_QK_W)))],
        out_specs=[hm(NA_HEADS, NA_HD)] * 3 + [hm(GLA_HEADS, GLA_DK)] * 2 + [hm(GLA_HEADS, GLA_DV)],
        out_shape=[SDS((NA_HEADS, T, NA_HD), F32)] * 3 + [SDS((GLA_HEADS, T, GLA_DK), F32)] * 2
        + [SDS((GLA_HEADS, T, GLA_DV), F32)],
        compiler_params=_params(1),
    )(z, z, z)


def _assemble_dz(dgates, na_dqkv, gla_dqk, gla_dv, dgr, dcq, dckv, dglow, dk_raw, lay, zp, name):
    T = dgr.shape[0]
    tm = _tile(T, 256)

    def body(dgt_ref, nq_ref, nk_ref, nv_ref, gq_ref, gk_ref, gv_ref, dgr_ref, dcq_ref, dckv_ref, dglow_ref, dkr_ref, dz_ref):
        def put(seg, off, val):
            a = lay[seg][0] + off
            dz_ref[:, a:a + val.shape[1]] = val.astype(BF16)

        put('gates', 0, dgt_ref[...])
        for seg, ref in (('na_q', nq_ref), ('na_k', nk_ref), ('na_v', nv_ref)):
            for h in range(NA_HEADS):
                put(seg, h * NA_HD, ref[h])
        for seg, ref in (('gq', gq_ref), ('gk', gk_ref)):
            for h in range(GLA_HEADS):
                put(seg, h * GLA_DK, ref[h])
        for h in range(GLA_HEADS):
            put('gv', h * GLA_DV, gv_ref[h])
        put('gr', 0, dgr_ref[...])
        put('c_q', 0, dcq_ref[...])
        put('c_kv', 0, dckv_ref[...])
        put('glow', 0, dglow_ref[...])
        kr = dkr_ref[0][:, MLA_NOPE:]
        for h in range(1, MLA_HEADS):
            kr = kr + dkr_ref[h][:, MLA_NOPE:]
        put('k_rope', 0, jnp.concatenate([kr, jnp.zeros((tm, LANE - MLA_ROPE), F32)], axis=1))

    tok = lambda a: pl.BlockSpec((tm, a.shape[1]), lambda i: (i, 0))
    hm = lambda a: pl.BlockSpec((a.shape[0], tm, a.shape[2]), lambda i: (0, i, 0))
    args = (dgates, *na_dqkv, *gla_dqk, gla_dv, dgr, dcq, dckv, dglow, dk_raw)
    return pl.pallas_call(
        body, name=name, grid=(T // tm,),
        in_specs=[tok(dgates)] + [hm(a) for a in (*na_dqkv, *gla_dqk, gla_dv)] + [tok(dgr), tok(dcq), tok(dckv), tok(dglow),
                                                                                 hm(dk_raw)],
        out_specs=pl.BlockSpec((tm, zp), lambda i: (i, 0)), out_shape=SDS((T, zp), BF16),
        compiler_params=_params(1),
    )(*args)


def _loss_head(y, target, name):
    T, D = y.shape
    tm = _tile(T, 1024)

    def body(y_ref, t_ref, dy_ref, l_ref):
        @pl.when(pl.program_id(0) == 0)
        def _():
            l_ref[...] = jnp.zeros_like(l_ref)

        e = y_ref[...] - t_ref[...]
        dy_ref[...] = e * (1.0 / D)
        l_ref[...] += 0.5 * jnp.sum(jnp.mean(e * e, axis=-1, keepdims=True), axis=0, keepdims=True)

    tok = pl.BlockSpec((tm, D), lambda i: (i, 0))
    return pl.pallas_call(
        body, name=name, grid=(T // tm,), in_specs=[tok, tok],
        out_specs=[tok, pl.BlockSpec((1, 1), lambda i: (0, 0))],
        out_shape=[SDS((T, D), F32), SDS((1, 1), F32)],
        compiler_params=_params(1),
    )(y, target)


def _exchange_copies(src, dst, send_sems, recv_sems, local_sems, broadcast, with_recvs=True):
    n = len(src)
    x, y, c = lax.axis_index("x"), lax.axis_index("y"), lax.axis_index("c")
    me = 4 * x + 2 * y + c
    local = [pltpu.make_async_copy(src[a] if broadcast else src[a].at[me], dst[a].at[me], local_sems.at[a])
             for a in range(n)]
    sends, recvs = [], []
    for d in range(1, N_DEV):
        px = 1 - x if d & 4 else x
        py = 1 - y if d & 2 else y
        pc = 1 - c if d & 1 else c
        peer = 4 * px + 2 * py + pc
        for a in range(n):
            for out, slot in ((sends, me), (recvs, peer)) if with_recvs else ((sends, me),):
                out.append(pltpu.make_async_remote_copy(
                    src_ref=src[a] if broadcast else src[a].at[peer], dst_ref=dst[a].at[slot],
                    send_sem=send_sems.at[a, d - 1], recv_sem=recv_sems.at[a, d - 1],
                    device_id=(px, py, pc), device_id_type=pl.DeviceIdType.MESH))
    return local, sends, recvs


def _exchange_start(*refs_and_mode):
    local, sends, _ = _exchange_copies(*refs_and_mode, with_recvs=False)
    for cp in local + sends:
        cp.start()


def _exchange_wait(*refs_and_mode):
    local, sends, recvs = _exchange_copies(*refs_and_mode)
    for cp in recvs:
        cp.wait_recv()
    for cp in sends:
        cp.wait_send()
    for cp in local:
        cp.wait()


def _exchange_shapes(srcs, broadcast):
    n = len(srcs)
    out_shape = [SDS((N_DEV,) + (tuple(s.shape) if broadcast else tuple(s.shape[1:])), s.dtype) for s in srcs]
    sems = [pltpu.SemaphoreType.DMA((n, N_DEV - 1)), pltpu.SemaphoreType.DMA((n, N_DEV - 1)),
            pltpu.SemaphoreType.DMA((n,))]
    return out_shape, sems


def _exchange(srcs, broadcast, name):
    n = len(srcs)
    out_shape, sems = _exchange_shapes(srcs, broadcast)

    def body(*refs):
        mode = (refs[:n], refs[n:2 * n]) + tuple(refs[2 * n:]) + (broadcast,)
        _exchange_start(*mode)
        _exchange_wait(*mode)

    hbm = pl.BlockSpec(memory_space=pltpu.HBM)
    return pl.pallas_call(body, name=name, in_specs=[hbm] * n, out_specs=[hbm] * n, out_shape=out_shape,
                          scratch_shapes=sems)(*srcs)


def _call(body, *, name, grid, in_specs, out_specs, out_shape, args, scratch_shapes=(), exch=None):
    params = _params(len(grid))
    if exch is None:
        return pl.pallas_call(body, name=name, grid=grid, in_specs=in_specs, out_specs=out_specs, out_shape=out_shape,
                              scratch_shapes=list(scratch_shapes), compiler_params=params)(*args)
    srcs, broadcast = exch
    n, n_in, n_out, n_scr = len(srcs), len(args), len(out_shape), len(scratch_shapes)
    x_shape, sems = _exchange_shapes(srcs, broadcast)

    def carrier(*refs):
        ins, x_src = refs[:n_in], refs[n_in:n_in + n]
        outs, x_dst = refs[n_in + n:n_in + n + n_out], refs[n_in + n + n_out:n_in + 2 * n + n_out]
        scr = refs[n_in + 2 * n + n_out:n_in + 2 * n + n_out + n_scr]
        mode = (x_src, x_dst) + tuple(refs[n_in + 2 * n + n_out + n_scr:]) + (broadcast,)
        ids = [pl.program_id(a) for a in range(len(grid))]
        first = functools.reduce(jnp.logical_and, [i == 0 for i in ids])
        last = functools.reduce(jnp.logical_and, [i == g - 1 for i, g in zip(ids, grid)])
        pl.when(first)(lambda: _exchange_start(*mode))
        body(*ins, *outs, *scr)
        pl.when(last)(lambda: _exchange_wait(*mode))

    hbm = pl.BlockSpec(memory_space=pltpu.HBM)
    res = pl.pallas_call(carrier, name=name, grid=grid, in_specs=list(in_specs) + [hbm] * n,
                         out_specs=list(out_specs) + [hbm] * n, out_shape=list(out_shape) + x_shape,
                         scratch_shapes=list(scratch_shapes) + sems, compiler_params=params)(*args, *srcs)
    return res[:n_out], res[n_out:]


def _adam_math(w, g, m, v):
    m = ADAM_B1 * m + (1.0 - ADAM_B1) * g
    v = ADAM_B2 * v + (1.0 - ADAM_B2) * (g * g)
    m_hat = m / (1.0 - ADAM_B1 ** ADAM_STEP)
    v_hat = v / (1.0 - ADAM_B2 ** ADAM_STEP)
    delta = -ADAM_LR * (m_hat / (jnp.sqrt(v_hat) + ADAM_EPS) + ADAM_WD * w)
    return delta, m, v


def _adamw_sharded(landed, k, w, m, v, name):
    L, r, c = w.shape
    lanes = -(-c // LANE) * LANE
    rb = _tile(r, max(16, (1 << 20) // (N_DEV * lanes * 2)), 16)

    def body(*refs):
        l_refs = refs[:L]
        w_ref, m_ref, v_ref, g_ref, d_ref, nm_ref, nv_ref = refs[L:]
        for j in range(L):
            @pl.when(pl.program_id(0) == j)
            def _(j=j):
                g = l_refs[j][0, 0].astype(F32)
                for s in range(1, N_DEV):
                    g = g + l_refs[j][s, 0].astype(F32)
                delta, nm, nv = _adam_math(w_ref[0], g, m_ref[0], v_ref[0])
                g_ref[0] = g
                d_ref[0] = delta
                nm_ref[0] = nm
                nv_ref[0] = nv

    blk = pl.BlockSpec((1, rb, c), lambda l, i: (l, i, 0))
    land = [pl.BlockSpec((N_DEV, 1, rb, c), lambda l, i, j=j: (0, k, jnp.where(l == j, i, 0), 0)) for j in range(L)]
    return pl.pallas_call(
        body, name=name, grid=(L, r // rb),
        in_specs=land + [blk, blk, blk],
        out_specs=[blk] * 4, out_shape=[SDS((L, r, c), F32)] * 4,
        compiler_params=_params(2),
    )(*landed, w, m, v)


def _adamw_replicated(landed, w, m, v, name):
    R = w.shape[0]

    def body(l_ref, w_ref, m_ref, v_ref, g_ref, d_ref, nm_ref, nv_ref):
        g = l_ref[0]
        for s in range(1, N_DEV):
            g = g + l_ref[s]
        delta, nm, nv = _adam_math(w_ref[...], g, m_ref[...], v_ref[...])
        g_ref[...] = g
        d_ref[...] = delta
        nm_ref[...] = nm
        nv_ref[...] = nv

    blk = pl.BlockSpec((R, LANE), lambda i: (0, 0))
    return pl.pallas_call(
        body, name=name, grid=(1,),
        in_specs=[pl.BlockSpec((N_DEV, R, LANE), lambda i: (0, 0, 0)), blk, blk, blk],
        out_specs=[blk] * 4, out_shape=[SDS((R, LANE), F32)] * 4,
        compiler_params=_params(1),
    )(landed, w, m, v)


def _heads(x, h):
    T = x.shape[0]
    return jnp.transpose(x.reshape(T, h, -1), (1, 0, 2))


def _unheads(x):
    h, T, d = x.shape
    return jnp.transpose(x, (1, 0, 2)).reshape(T, h * d)


def _pack(parts):
    flat = jnp.concatenate([p.reshape(-1) for p in parts])
    rows = -(-flat.shape[0] // (8 * LANE)) * 8
    return jnp.pad(flat, (0, rows * LANE - flat.shape[0])).reshape(rows, LANE)


def _unpack(packed, shapes):
    flat = packed.reshape(-1)
    out, off = [], 0
    for s in shapes:
        size = int(np.prod(s))
        out.append(flat[off:off + size].reshape(s))
        off += size
    return out


def _layer_fwd(x0, w, l, lay, tabs, carry):
    T, D = x0.shape
    ctab, stab, na_bias = tabs
    col = lambda name: lay[name][0] // lay[name][1]
    sv = {'x0': x0}
    x1, sv['a1'], sv['b1'] = _carried(_ffn_fwd, carry, 'ffn1', x0, w['ffn1_norm'], w['ffn1_w1'], w['ffn1_w3'],
                                      w['ffn1_w2'], f"ffn1_fwd_{l}")
    z, sv['hb'] = _carried(_norm_matmul, carry, 'mix_in', x1, 0, w['mix_norm'], w['w_in_z'], f"mix_in_{l}")
    seg = lambda name: z[:, lay[name][0]:lay[name][0] + lay[name][3]]
    sv['na_q'], sv['na_k'], sv['na_v'], sv['gq'], sv['gk'], sv['gv'] = _split_z(z, lay, f"split_z_{l}")
    y_na = _carried(_na_fwd, carry, 'na', sv['na_q'], sv['na_k'], sv['na_v'], na_bias, w['na_q_norm'], w['na_k_norm'],
                    f"na_fwd_{l}")[0]
    sv['gg'] = _gla_gate_fwd(z, col('glow'), w['up_f'], w['up_b'], w['gla_gf_bias'], w['gla_gb_bias'], f"gla_gate_{l}")
    o_gla = _gla_fwd(sv['gq'], sv['gk'], sv['gv'], sv['gg'], f"gla_fwd_{l}")
    q_raw, sv['hq'] = _norm_matmul(z, col('c_q'), w['mla_cq_norm'], w['mla_w_uq'], f"mla_uq_{l}")
    kv_raw, sv['hkv'] = _norm_matmul(z, col('c_kv'), w['mla_ckv_norm'], w['mla_w_ukv'], f"mla_ukv_{l}")
    kv_h = _heads(kv_raw, MLA_HEADS)
    k_rope = jnp.broadcast_to(seg('k_rope')[None], (MLA_HEADS, T, MLA_ROPE))
    sv['mq_raw'] = _heads(q_raw, MLA_HEADS)
    sv['mk_raw'] = jnp.concatenate([kv_h[..., :MLA_NOPE], k_rope], axis=-1)
    sv['mq'] = _mla_prep_fwd(sv['mq_raw'], w['mla_q_norm'], ctab, stab, f"mla_qprep_{l}")
    sv['mk'] = _mla_prep_fwd(sv['mk_raw'], w['mla_k_norm'], ctab, stab, f"mla_kprep_{l}")
    sv['mv'] = kv_h[..., MLA_NOPE:].astype(BF16)
    y_mla, sv['lse'] = _carried(_mla_attn_fwd, carry, 'mla', sv['mq'], sv['mk'], sv['mv'], f"mla_attn_{l}")
    w_br = (w['w_br_na'], w['w_br_gla'], w['w_br_mla'])
    x2 = _mix_out_fwd(x1, y_na, o_gla, y_mla, z, col('gr'), w['gla_out_norm'], w_br, w['w_out'], f"mix_out_{l}")
    sv.update(x1=x1, z=z, y_na=y_na, o_gla=o_gla, y_mla=y_mla, x2=x2)
    x3, sv['a2'], sv['b2'] = _carried(_ffn_fwd, carry, 'ffn2', x2, w['ffn2_norm'], w['ffn2_w1'], w['ffn2_w3'],
                                      w['ffn2_w2'], f"ffn2_fwd_{l}")
    return x3, sv


def _ffn_grads(grads, x, dy, a, b, w, pre, l, carry):
    dx, da, db, u, hb, dyb, dg = _carried(_ffn_bwd, carry, pre, x, w[pre + '_norm'], dy, a, b, w[pre + '_w1'],
                                          w[pre + '_w3'], w[pre + '_w2'], f"{pre}_bwd_{l}")
    grads.update({pre + '_norm': dg,
                  pre + '_w1': _matmul_tn(hb, da, f"{pre}_dw1_{l}"),
                  pre + '_w3': _matmul_tn(hb, db, f"{pre}_dw3_{l}"),
                  pre + '_w2': _matmul_tn(u, dyb, f"{pre}_dw2_{l}")})
    return dx


def _layer_bwd(grads, dx3, sv, w, l, lay, tabs, carry):
    T, D = dx3.shape
    ctab, stab, na_bias, onehot = tabs
    col = lambda name: lay[name][0] // lay[name][1]
    z = sv['z']
    dx2 = _ffn_grads(grads, sv['x2'], dx3, sv['a2'], sv['b2'], w, 'ffn2', l, carry)
    w_br = (w['w_br_na'], w['w_br_gla'], w['w_br_mla'])
    (dy_na, do_gla, dy_mla, dgr, dgates, dw0, dw1, dw2, dwo, dgn) = _mix_out_bwd(
        dx2, sv['y_na'], sv['o_gla'], sv['y_mla'], z, col('gr'), w['gla_out_norm'], w_br, w['w_out'], f"mix_out_bwd_{l}")
    grads.update(w_br_na=dw0, w_br_gla=dw1, w_br_mla=dw2, w_out=dwo, gla_out_norm=dgn)
    dq, dk, dv, dbias, dgq, dgk = _na_bwd(sv['na_q'], sv['na_k'], sv['na_v'], dy_na, na_bias,
                                          w['na_q_norm'], w['na_k_norm'], f"na_bwd_{l}")
    grads.update(na_q_norm=dgq, na_k_norm=dgk, na_rpb=_na_bias_tables_bwd(dbias, onehot))
    gdq, gdk, gdv, gdg = _gla_bwd(sv['gq'], sv['gk'], sv['gv'], sv['gg'], do_gla, f"gla_bwd_{l}")
    dglow, dupf, dupb, dbf, dbb = _gla_gate_bwd(z, col('glow'), w['up_f'], w['up_b'], w['gla_gf_bias'], w['gla_gb_bias'],
                                                gdg, f"gla_gate_bwd_{l}")
    grads.update(gla_gf_up=dupf[:GLA_RANK], gla_gb_up=dupb[GLA_RANK:2 * GLA_RANK], gla_gf_bias=dbf, gla_gb_bias=dbb)
    mdq, mdk, mdv = _carried(_mla_attn_bwd, carry, 'mla', sv['mq'], sv['mk'], sv['mv'], sv['y_mla'], sv['lse'], dy_mla,
                             f"mla_attn_bwd_{l}")
    dq_raw, dgqn = _mla_prep_bwd(sv['mq_raw'], w['mla_q_norm'], ctab, stab, mdq, f"mla_qprep_bwd_{l}")
    dk_raw, dgkn = _mla_prep_bwd(sv['mk_raw'], w['mla_k_norm'], ctab, stab, mdk, f"mla_kprep_bwd_{l}")
    dq_raw = _unheads(dq_raw).astype(BF16)
    dkv_raw = _unheads(jnp.concatenate([dk_raw[..., :MLA_NOPE], mdv], axis=-1)).astype(BF16)
    dcq, dgcq = _norm_matmul_bwd(dq_raw, w['mla_w_uq'], z, col('c_q'), w['mla_cq_norm'], None, f"mla_uq_bwd_{l}")
    dckv, dgckv = _norm_matmul_bwd(dkv_raw, w['mla_w_ukv'], z, col('c_kv'), w['mla_ckv_norm'], None, f"mla_ukv_bwd_{l}")
    grads.update(mla_q_norm=dgqn, mla_k_norm=dgkn, mla_cq_norm=dgcq, mla_ckv_norm=dgckv,
                 mla_w_uq=_matmul_tn(sv['hq'], dq_raw, f"mla_duq_{l}", F32),
                 mla_w_ukv=_matmul_tn(sv['hkv'], dkv_raw, f"mla_dukv_{l}", F32))
    dz = _assemble_dz(dgates, (dq, dk, dv), (gdq, gdk), gdv, dgr, dcq, dckv, dglow, dk_raw, lay, z.shape[1], f"dz_{l}")
    dx1, dgmix = _carried(_norm_matmul_bwd, carry, 'mix_in', dz, w['w_in_z'], sv['x1'], 0, w['mix_norm'], dx2,
                          f"mix_in_bwd_{l}")
    grads.update(mix_norm=dgmix, w_in=_matmul_tn(sv['hb'], dz, f"mix_dw_in_{l}"))
    return _ffn_grads(grads, sv['x0'], dx1, sv['a1'], sv['b1'], w, 'ffn1', l, carry)


EXCHANGE_PARTS = {
    'F1a': [['ffn1_w1', 'ffn1_w3']],
    'F1b': [['ffn1_w2']],
    'F2': [['ffn2_w1', 'ffn2_w3'], ['ffn2_w2']],
    'C': [['w_in']],
    'S': [['w_br_na', 'w_br_gla', 'w_br_mla'], ['w_out'], ['mla_w_uq'], ['mla_w_ukv'], ['gla_gf_up', 'gla_gb_up']],
}
FWD_PLAN = {'ffn1': [(0, 'C')], 'mix_in': [(0, 'S')], 'mla': [(0, 'F2')], 'na': [(1, 'F1b')], 'ffn2': [(1, 'F1a')]}
BWD_PLAN = {'mla': [(0, 'F2')], 'mix_in': [(0, 'S')], 'ffn1': [(0, 'C')], 'ffn2': [(1, 'F1a'), (1, 'F1b')]}
EDGE_PARTS = ['F1a', 'F1b']


def _carried(fn, carry, key, *args):
    if key in carry:
        make_srcs, broadcast, deliver = carry[key]
        outs, landed = fn(*args, exch=(make_srcs(), broadcast))
        deliver(landed)
        return outs
    return fn(*args)


def kernel(x, ffn1_norm, ffn1_w1, ffn1_w3, ffn1_w2, mix_norm, w_in, na_q_norm, na_k_norm, na_rpb, gla_gf_up, gla_gf_bias, gla_gb_up, gla_gb_bias, gla_out_norm, mla_cq_norm, mla_ckv_norm, mla_w_uq, mla_w_ukv, mla_q_norm, mla_k_norm, w_br_na, w_br_gla, w_br_mla, w_out, ffn2_norm, ffn2_w1, ffn2_w3, ffn2_w2, loss_target, m_ffn1_norm, m_ffn1_w1, m_ffn1_w3, m_ffn1_w2, m_mix_norm, m_w_in, m_na_q_norm, m_na_k_norm, m_na_rpb, m_gla_gf_up, m_gla_gf_bias, m_gla_gb_up, m_gla_gb_bias, m_gla_out_norm, m_mla_cq_norm, m_mla_ckv_norm, m_mla_w_uq, m_mla_w_ukv, m_mla_q_norm, m_mla_k_norm, m_w_br_na, m_w_br_gla, m_w_br_mla, m_w_out, m_ffn2_norm, m_ffn2_w1, m_ffn2_w3, m_ffn2_w2, v_ffn1_norm, v_ffn1_w1, v_ffn1_w3, v_ffn1_w2, v_mix_norm, v_w_in, v_na_q_norm, v_na_k_norm, v_na_rpb, v_gla_gf_up, v_gla_gf_bias, v_gla_gb_up, v_gla_gb_bias, v_gla_out_norm, v_mla_cq_norm, v_mla_ckv_norm, v_mla_w_uq, v_mla_w_ukv, v_mla_q_norm, v_mla_k_norm, v_w_br_na, v_w_br_gla, v_w_br_mla, v_w_out, v_ffn2_norm, v_ffn2_w1, v_ffn2_w3, v_ffn2_w2):
    given = dict(locals())
    P = {n: given[n] for n in WEIGHTS}
    M = {n: given['m_' + n] for n in WEIGHTS}
    V = {n: given['v_' + n] for n in WEIGHTS}
    xs = x[0]
    T, D = xs.shape
    L = ffn1_norm.shape[0]
    lay, zp, d_in = _z_layout(D)

    def arrays_of(items):
        return [(l, names) for l, p in items for names in EXCHANGE_PARTS[p]]

    ws = [{n: P[n][l][None, :] for n in REPLICATED if n != 'na_rpb'} for l in range(L)]

    def gather_srcs(items):
        return [jnp.stack([P[n][l].astype(BF16) for n in names]) for l, names in arrays_of(items)]

    def take_gathered(items, res):
        for (l, names), g in zip(arrays_of(items), res):
            for k, n in enumerate(names):
                blk = g[:, k]
                if n == 'w_in':
                    ws[l]['w_in_z'] = _z_cols_from_blocks(blk, lay)
                    continue
                if n in COL_SHARDED:
                    full = jnp.concatenate([blk[d] for d in range(N_DEV)], axis=1)
                else:
                    full = blk.reshape(N_DEV * blk.shape[1], blk.shape[2])
                if n == 'gla_gf_up':
                    ws[l]['up_f'] = jnp.zeros((LANE, GLA_QK_W), BF16).at[:GLA_RANK].set(full)
                elif n == 'gla_gb_up':
                    ws[l]['up_b'] = jnp.zeros((LANE, GLA_QK_W), BF16).at[GLA_RANK:2 * GLA_RANK].set(full)
                else:
                    ws[l][n] = full

    def plan(table, l, make_srcs, broadcast, deliver):
        carry = {}
        for key, items in table.items():
            items = [(l + off, p) for off, p in items if l + off < L]
            if items:
                carry[key] = (functools.partial(make_srcs, items), broadcast, functools.partial(deliver, items))
        return carry

    ctab, stab = _rope_tables(T)
    na_bias, onehot = _na_bias_tables(na_rpb)

    edge = [(0, p) for p in EDGE_PARTS]
    take_gathered(edge, _exchange(gather_srcs(edge), True, "gather_weights_edge"))
    saved = []
    h = xs
    for l in range(L):
        h, sv = _layer_fwd(h, ws[l], l, lay, (ctab, stab, na_bias[l]), plan(FWD_PLAN, l, gather_srcs, True, take_gathered))
        saved.append(sv)
    dh, loss_local = _loss_head(h, loss_target[0], "loss_head")
    loss = lax.psum(loss_local[0, 0], ("x", "y", "c"))

    layer_grads = [dict() for _ in range(L)]

    def scatter_srcs(items):
        def blocks(l, n):
            g = layer_grads[l][n]
            if n == 'w_in':
                return _blocks_from_z_cols(g, lay, d_in).astype(BF16)
            if n in COL_SHARDED:
                c = g.shape[1] // N_DEV
                return jnp.stack([g[:, d * c:(d + 1) * c] for d in range(N_DEV)]).astype(BF16)
            return g.reshape(N_DEV, -1, g.shape[1]).astype(BF16)
        return [jnp.stack([blocks(l, n) for n in names], axis=1) for l, names in arrays_of(items)]

    landed = [dict() for _ in range(L)]

    def take_landed(items, res):
        for (l, names), r in zip(arrays_of(items), res):
            for k, n in enumerate(names):
                landed[l][n] = (r, k)

    for l in reversed(range(L)):
        dh = _layer_bwd(layer_grads[l], dh, saved[l], ws[l], l, lay, (ctab, stab, na_bias[l], onehot),
                        plan(BWD_PLAN, l, scatter_srcs, False, take_landed))
    grad_x = dh[None]
    take_landed(edge, _exchange(scatter_srcs(edge), False, "scatter_grads_edge"))

    out = {}
    for n in SHARDED:
        out[n] = _adamw_sharded([landed[l][n][0] for l in range(L)], landed[0][n][1], P[n], M[n], V[n], f"adamw_{n}")

    rep = [n for n in REPLICATED]
    rep_shapes = [tuple(P[n].shape) for n in rep]
    rep_partial = _pack([jnp.stack([layer_grads[l][n].reshape(P[n].shape[1:]) for l in range(L)]) for n in rep])
    (rep_landed,) = _exchange([rep_partial], True, "gather_small_grads")
    packed = _adamw_replicated(rep_landed, _pack([P[n] for n in rep]), _pack([M[n] for n in rep]),
                               _pack([V[n] for n in rep]), "adamw_replicated")
    for kind, pk in enumerate(packed):
        for n, arr in zip(rep, _unpack(pk, rep_shapes)):
            out.setdefault(n, [None] * 4)[kind] = arr

    res = [loss, grad_x]
    for kind in range(4):
        res += [out[n][kind] for n in WEIGHTS]
    return tuple(res)
```

```python
import functools

import numpy as np
import jax
import jax.numpy as jnp
from jax import lax
from jax.experimental import pallas as pl
from jax.experimental.pallas import tpu as pltpu

F32 = jnp.float32
BF16 = jnp.bfloat16
SDS = jax.ShapeDtypeStruct
HI = lax.Precision.HIGHEST

N_DEV = 8
DEPTH = 4
EPS = 1e-6
LANE = 128
VMEM_LIMIT = 56 * 1024 * 1024

GRID_W = 64
NA_HEADS, NA_HD, NA_WIN_R, NA_WIN_C = 8, 64, 8, 16
GLA_HEADS, GLA_DK, GLA_DV, GLA_RANK, GLA_TAU, GLA_CHUNK = 4, 64, 128, 16, 16.0, 64
MLA_HEADS, MLA_QR, MLA_KVR, MLA_NOPE, MLA_ROPE, MLA_V = 4, 256, 256, 128, 64, 128
MLA_QK = MLA_NOPE + MLA_ROPE
ROPE_THETA = 10000.0
NA_W = NA_HEADS * NA_HD
GLA_QK_W = GLA_HEADS * GLA_DK
GLA_V_W = GLA_HEADS * GLA_DV
MLA_V_W = MLA_HEADS * MLA_V

ADAM_LR, ADAM_B1, ADAM_B2, ADAM_EPS, ADAM_WD, ADAM_STEP = 0.001, 0.9, 0.999, 1e-08, 0.01, 10

WEIGHTS = ['ffn1_norm', 'ffn1_w1', 'ffn1_w3', 'ffn1_w2', 'mix_norm', 'w_in', 'na_q_norm', 'na_k_norm', 'na_rpb',
           'gla_gf_up', 'gla_gf_bias', 'gla_gb_up', 'gla_gb_bias', 'gla_out_norm', 'mla_cq_norm', 'mla_ckv_norm',
           'mla_w_uq', 'mla_w_ukv', 'mla_q_norm', 'mla_k_norm', 'w_br_na', 'w_br_gla', 'w_br_mla', 'w_out',
           'ffn2_norm', 'ffn2_w1', 'ffn2_w3', 'ffn2_w2']
COL_SHARDED = ['ffn1_w1', 'ffn1_w3', 'w_in', 'gla_gf_up', 'gla_gb_up', 'mla_w_uq', 'mla_w_ukv', 'w_br_na', 'w_br_gla',
               'w_br_mla', 'ffn2_w1', 'ffn2_w3']
ROW_SHARDED = ['ffn1_w2', 'w_out', 'ffn2_w2']
SHARDED = [n for n in WEIGHTS if n in COL_SHARDED or n in ROW_SHARDED]
REPLICATED = [n for n in WEIGHTS if n not in SHARDED]

NT = (((1,), (1,)), ((), ()))
TN = (((0,), (0,)), ((), ()))


def _tile(n, pref, mult=8):
    best = None
    for t in range(mult, min(n, pref) + 1, mult):
        if n % t == 0:
            best = t
    return best if best is not None else n


def _params(n_axes):
    return pltpu.CompilerParams(dimension_semantics=("arbitrary",) * n_axes, vmem_limit_bytes=VMEM_LIMIT)


def _dot(a, b, dims=None, precision=None):
    if dims is None:
        return jnp.dot(a, b, preferred_element_type=F32, precision=precision)
    return lax.dot_general(a, b, dims, preferred_element_type=F32, precision=precision)


def _sigmoid(x):
    return 1.0 / (1.0 + jnp.exp(-x))


def _rstd(x):
    return lax.rsqrt(jnp.mean(x * x, axis=-1, keepdims=True) + EPS)


def _rms_bwd(dy, x, g, r):
    xh = x * r
    dyg = dy * g
    dx = r * (dyg - xh * jnp.mean(dyg * xh, axis=-1, keepdims=True))
    return dx, jnp.sum(dy * xh, axis=0, keepdims=True)


def _z_layout(d_model):
    own = {}
    off = 0
    for name, w in [('na_q', NA_W), ('na_k', NA_W), ('na_v', NA_W), ('gq', GLA_QK_W), ('gk', GLA_QK_W),
                    ('gv', GLA_V_W), ('gr', GLA_V_W), ('gfl', GLA_RANK), ('gbl', GLA_RANK), ('c_q', MLA_QR),
                    ('c_kv', MLA_KVR), ('k_rope', MLA_ROPE), ('gates', 3 * d_model)]:
        own[name] = (off, w)
        off += w
    order = [('gates', 3 * d_model), ('na_q', NA_W), ('na_k', NA_W), ('na_v', NA_W), ('gv', GLA_V_W), ('gr', GLA_V_W),
             ('gq', GLA_QK_W), ('gk', GLA_QK_W), ('c_q', MLA_QR), ('c_kv', MLA_KVR), ('glow', LANE), ('k_rope', LANE)]
    lay = {}
    z = 0
    for name, w in order:
        assert z % w == 0, (name, z, w)
        if name == 'glow':
            lay[name] = (z, w, own['gfl'][0], 2 * GLA_RANK)
        else:
            lay[name] = (z, w, own[name][0], own[name][1])
        z += w
    return lay, z, off


def _z_cols_from_blocks(blocks, lay):
    cs = blocks.shape[2]
    parts = []
    for zo, zw, oo, ow in lay.values():
        a = oo
        while a < oo + ow:
            b = min(oo + ow, (a // cs + 1) * cs)
            parts.append(blocks[a // cs, :, a % cs:a % cs + (b - a)])
            a = b
        if zw > ow:
            parts.append(jnp.zeros((blocks.shape[1], zw - ow), blocks.dtype))
    return jnp.concatenate(parts, axis=1)


def _blocks_from_z_cols(wz, lay, d_in):
    cs = d_in // N_DEV
    own = sorted(lay.values(), key=lambda t: t[2])
    out = []
    for k in range(N_DEV):
        parts = []
        for zo, zw, oo, ow in own:
            a, b = max(oo, k * cs), min(oo + ow, (k + 1) * cs)
            if a < b:
                parts.append(wz[:, zo + a - oo:zo + b - oo])
        out.append(jnp.concatenate(parts, axis=1))
    return jnp.stack(out)


def _ffn_fwd(x, g, w1, w3, w2, name, exch=None):
    T, D = x.shape
    Fd = w1.shape[1]
    tm, tf = _tile(T, 1024), _tile(Fd, 256, LANE)
    nj = Fd // tf

    def body(x_ref, g_ref, w1_ref, w3_ref, w2_ref, o_ref, a_ref, b_ref, h_scr, acc):
        j = pl.program_id(1)

        @pl.when(j == 0)
        def _():
            xv = x_ref[...]
            h_scr[...] = (xv * _rstd(xv) * g_ref[...]).astype(BF16)
            acc[...] = jnp.zeros_like(acc)

        h = h_scr[...]
        a = _dot(h, w1_ref[...])
        b = _dot(h, w3_ref[...])
        a_ref[...] = a.astype(BF16)
        b_ref[...] = b.astype(BF16)
        u = a * _sigmoid(a) * b
        acc[...] += _dot(u.astype(BF16), w2_ref[...])

        @pl.when(j == nj - 1)
        def _():
            o_ref[...] = x_ref[...] + 0.5 * acc[...]

    return _call(
        body, name=name, grid=(T // tm, nj),
        in_specs=[pl.BlockSpec((tm, D), lambda i, j: (i, 0)), pl.BlockSpec((1, D), lambda i, j: (0, 0)),
                  pl.BlockSpec((D, tf), lambda i, j: (0, j)), pl.BlockSpec((D, tf), lambda i, j: (0, j)),
                  pl.BlockSpec((tf, D), lambda i, j: (j, 0))],
        out_specs=[pl.BlockSpec((tm, D), lambda i, j: (i, 0)), pl.BlockSpec((tm, tf), lambda i, j: (i, j)),
                   pl.BlockSpec((tm, tf), lambda i, j: (i, j))],
        out_shape=[SDS((T, D), F32), SDS((T, Fd), BF16), SDS((T, Fd), BF16)],
        scratch_shapes=[pltpu.VMEM((tm, D), BF16), pltpu.VMEM((tm, D), F32)],
        args=(x, g, w1, w3, w2), exch=exch)


def _ffn_bwd(x, g, dy, a, b, w1, w3, w2, name, exch=None):
    T, D = x.shape
    Fd = w1.shape[1]
    tm, tf = _tile(T, 512), _tile(Fd, 256, LANE)
    nj = Fd // tf

    def body(x_ref, g_ref, dy_ref, a_ref, b_ref, w1_ref, w3_ref, w2_ref,
             dx_ref, da_ref, db_ref, u_ref, hb_ref, dyb_ref, dg_ref, dh_acc):
        i, j = pl.program_id(0), pl.program_id(1)

        @pl.when(j == 0)
        def _():
            xv = x_ref[...]
            hb_ref[...] = (xv * _rstd(xv) * g_ref[...]).astype(BF16)
            dyb_ref[...] = (0.5 * dy_ref[...]).astype(BF16)
            dh_acc[...] = jnp.zeros_like(dh_acc)

        @pl.when((i == 0) & (j == 0))
        def _():
            dg_ref[...] = jnp.zeros_like(dg_ref)

        du = _dot(dyb_ref[...], w2_ref[...], NT)
        av = a_ref[...].astype(F32)
        bv = b_ref[...].astype(F32)
        s = _sigmoid(av)
        sl = av * s
        da = (du * bv * (s * (1.0 + av * (1.0 - s)))).astype(BF16)
        db = (du * sl).astype(BF16)
        da_ref[...] = da
        db_ref[...] = db
        u_ref[...] = (sl * bv).astype(BF16)
        dh_acc[...] += _dot(da, w1_ref[...], NT) + _dot(db, w3_ref[...], NT)

        @pl.when(j == nj - 1)
        def _():
            xv = x_ref[...]
            dxn, dg = _rms_bwd(dh_acc[...], xv, g_ref[...], _rstd(xv))
            dx_ref[...] = dy_ref[...] + dxn
            dg_ref[...] += dg

    row = lambda i, j: (i, 0)
    return _call(
        body, name=name, grid=(T // tm, nj),
        in_specs=[pl.BlockSpec((tm, D), row), pl.BlockSpec((1, D), lambda i, j: (0, 0)), pl.BlockSpec((tm, D), row),
                  pl.BlockSpec((tm, tf), lambda i, j: (i, j)), pl.BlockSpec((tm, tf), lambda i, j: (i, j)),
                  pl.BlockSpec((D, tf), lambda i, j: (0, j)), pl.BlockSpec((D, tf), lambda i, j: (0, j)),
                  pl.BlockSpec((tf, D), lambda i, j: (j, 0))],
        out_specs=[pl.BlockSpec((tm, D), row), pl.BlockSpec((tm, tf), lambda i, j: (i, j)),
                   pl.BlockSpec((tm, tf), lambda i, j: (i, j)), pl.BlockSpec((tm, tf), lambda i, j: (i, j)),
                   pl.BlockSpec((tm, D), row), pl.BlockSpec((tm, D), row), pl.BlockSpec((1, D), lambda i, j: (0, 0))],
        out_shape=[SDS((T, D), F32), SDS((T, Fd), BF16), SDS((T, Fd), BF16), SDS((T, Fd), BF16),
                   SDS((T, D), BF16), SDS((T, D), BF16), SDS((1, D), F32)],
        scratch_shapes=[pltpu.VMEM((tm, D), F32)],
        args=(x, g, dy, a, b, w1, w3, w2), exch=exch)


def _matmul_tn(a, b, name, out_dtype=BF16):
    T, K = a.shape
    N = b.shape[1]
    tk = _tile(K, 1408, LANE)
    tn = _tile(N, 1408, LANE)
    tt = _tile(T, 1024)
    nt = T // tt

    def body(a_ref, b_ref, o_ref, acc):
        t = pl.program_id(2)

        @pl.when(t == 0)
        def _():
            acc[...] = jnp.zeros_like(acc)

        acc[...] += _dot(a_ref[...], b_ref[...], TN)

        @pl.when(t == nt - 1)
        def _():
            o_ref[...] = acc[...].astype(out_dtype)

    return pl.pallas_call(
        body, name=name, grid=(K // tk, N // tn, nt),
        in_specs=[pl.BlockSpec((tt, tk), lambda k, n, t: (t, k)), pl.BlockSpec((tt, tn), lambda k, n, t: (t, n))],
        out_specs=pl.BlockSpec((tk, tn), lambda k, n, t: (k, n)),
        out_shape=SDS((K, N), out_dtype),
        scratch_shapes=[pltpu.VMEM((tk, tn), F32)],
        compiler_params=_params(3),
    )(a, b)


def _norm_matmul(x, xcol, g, w, name, exch=None):
    T = x.shape[0]
    D = g.shape[1]
    N = w.shape[1]
    tm, tn = _tile(T, 1024), _tile(N, 768, LANE)

    def body(x_ref, g_ref, w_ref, z_ref, hb_ref):
        @pl.when(pl.program_id(1) == 0)
        def _():
            xv = x_ref[...]
            hb_ref[...] = (xv * _rstd(xv) * g_ref[...]).astype(BF16)

        z_ref[...] = _dot(hb_ref[...], w_ref[...])

    return _call(
        body, name=name, grid=(T // tm, N // tn),
        in_specs=[pl.BlockSpec((tm, D), lambda i, j: (i, xcol)), pl.BlockSpec((1, D), lambda i, j: (0, 0)),
                  pl.BlockSpec((D, tn), lambda i, j: (0, j))],
        out_specs=[pl.BlockSpec((tm, tn), lambda i, j: (i, j)), pl.BlockSpec((tm, D), lambda i, j: (i, 0))],
        out_shape=[SDS((T, N), F32), SDS((T, D), BF16)],
        args=(x, g, w), exch=exch)


def _norm_matmul_bwd(dz, w, x, xcol, g, resid, name, exch=None):
    T, N = dz.shape
    D = g.shape[1]
    tm, tn = _tile(T, 1024), _tile(N, 768, LANE)
    nj = N // tn
    has_resid = resid is not None

    def body(*refs):
        if has_resid:
            dz_ref, w_ref, x_ref, g_ref, r_ref, dx_ref, dg_ref, acc = refs
        else:
            dz_ref, w_ref, x_ref, g_ref, dx_ref, dg_ref, acc = refs
        i, j = pl.program_id(0), pl.program_id(1)

        @pl.when(j == 0)
        def _():
            acc[...] = jnp.zeros_like(acc)

        @pl.when((i == 0) & (j == 0))
        def _():
            dg_ref[...] = jnp.zeros_like(dg_ref)

        acc[...] += _dot(dz_ref[...], w_ref[...], NT)

        @pl.when(j == nj - 1)
        def _():
            xv = x_ref[...]
            dxn, dg = _rms_bwd(acc[...], xv, g_ref[...], _rstd(xv))
            dx_ref[...] = dxn + r_ref[...] if has_resid else dxn
            dg_ref[...] += dg

    in_specs = [pl.BlockSpec((tm, tn), lambda i, j: (i, j)), pl.BlockSpec((D, tn), lambda i, j: (0, j)),
                pl.BlockSpec((tm, D), lambda i, j: (i, xcol)), pl.BlockSpec((1, D), lambda i, j: (0, 0))]
    args = [dz, w, x, g]
    if has_resid:
        in_specs.append(pl.BlockSpec((tm, D), lambda i, j: (i, 0)))
        args.append(resid)
    return _call(
        body, name=name, grid=(T // tm, nj), in_specs=in_specs,
        out_specs=[pl.BlockSpec((tm, D), lambda i, j: (i, 0)), pl.BlockSpec((1, D), lambda i, j: (0, 0))],
        out_shape=[SDS((T, D), F32), SDS((1, D), F32)],
        scratch_shapes=[pltpu.VMEM((tm, D), F32)],
        args=args, exch=exch)


def _na_bias_tables(rpb):
    L = rpb.shape[0]
    qc = np.arange(GRID_W)[:, None]
    kc = np.arange(GRID_W)[None, :]
    dc = np.clip(kc - qc + NA_WIN_C - 1, 0, 2 * NA_WIN_C - 2)
    c0 = np.clip(qc - NA_WIN_C // 2, 0, GRID_W - NA_WIN_C)
    ok = (kc >= c0) & (kc < c0 + NA_WIN_C)
    onehot = (dc.reshape(-1)[None, :] == np.arange(2 * NA_WIN_C - 1)[:, None]).astype(np.float32)
    t1 = jnp.stack([rpb[:, :, c:c + NA_WIN_R, :] for c in range(NA_WIN_R)], axis=2)
    full = jnp.dot(t1.reshape(-1, 2 * NA_WIN_C - 1), jnp.asarray(onehot), precision=HI)
    full = full.reshape(L, NA_HEADS, NA_WIN_R, NA_WIN_R, GRID_W, GRID_W)
    full = jnp.where(jnp.asarray(ok)[None, None, None, None], full, -1e30)
    return jnp.transpose(full, (0, 1, 2, 4, 3, 5)).reshape(L * NA_HEADS, NA_WIN_R, GRID_W, NA_WIN_R * GRID_W), onehot


def _na_bias_tables_bwd(db, onehot):
    d = db.reshape(NA_HEADS, NA_WIN_R, GRID_W, NA_WIN_R, GRID_W)
    d = jnp.transpose(d, (0, 1, 3, 2, 4)).reshape(-1, GRID_W * GRID_W)
    t1 = jnp.dot(d, jnp.asarray(onehot.T), precision=HI).reshape(NA_HEADS, NA_WIN_R, NA_WIN_R, 2 * NA_WIN_C - 1)
    out = jnp.zeros((NA_HEADS, 2 * NA_WIN_R - 1, 2 * NA_WIN_C - 1), F32)
    for c in range(NA_WIN_R):
        out = out.at[:, c:c + NA_WIN_R, :].add(t1[:, c])
    return out


def _na_windows(bi, rb, rows):
    wsl, cls = [], []
    for i in range(rb):
        r = bi * rb + i
        r0 = jnp.clip(r - NA_WIN_R // 2, 0, rows - NA_WIN_R)
        wsl.append(pl.ds(pl.multiple_of(r0 * GRID_W, GRID_W), NA_WIN_R * GRID_W))
        cls.append(r0 - r + NA_WIN_R - 1)
    return wsl, cls


def _na_fwd(q, k, v, bias, layer, gq, gk, name, exch=None):
    H, T, d = q.shape
    rows = T // GRID_W
    rb = _tile(rows, 8, 1)
    win = NA_WIN_R * GRID_W
    scale = d ** -0.5

    def body(q_ref, k_ref, v_ref, b_ref, gq_ref, gk_ref, o_ref, qn, kn, vb):
        qv, kv = q_ref[0], k_ref[0]
        qn[...] = (qv * _rstd(qv) * gq_ref[...] * scale).astype(BF16)
        kn[...] = (kv * _rstd(kv) * gk_ref[...]).astype(BF16)
        vb[...] = v_ref[0].astype(BF16)

        def block(bi, c):
            wsl, cls = _na_windows(bi, rb, rows)
            qsl = pl.ds(pl.multiple_of(bi * (rb * GRID_W), rb * GRID_W), rb * GRID_W)
            kw = jnp.stack([kn[w, :] for w in wsl])
            vw = jnp.stack([vb[w, :] for w in wsl])
            s = _bdot(qn[qsl, :].reshape(rb, GRID_W, d), kw, 2, 2) + jnp.stack([b_ref[0, c_] for c_ in cls])
            p = jnp.exp(s - jnp.max(s, axis=-1, keepdims=True))
            p = p / jnp.sum(p, axis=-1, keepdims=True)
            o_ref[0, qsl, :] = _bdot(p.astype(BF16), vw, 2, 1).reshape(rb * GRID_W, d)
            return c

        lax.fori_loop(0, rows // rb, block, 0)

    hm = pl.BlockSpec((1, T, d), lambda h: (h, 0, 0))
    gs = pl.BlockSpec((1, d), lambda h: (0, 0))
    return _call(
        body, name=name, grid=(H,),
        in_specs=[hm, hm, hm, pl.BlockSpec((1, NA_WIN_R, GRID_W, win), lambda h: (layer * H + h, 0, 0, 0)), gs, gs],
        out_specs=[hm], out_shape=[SDS((H, T, d), F32)],
        scratch_shapes=[pltpu.VMEM((T, d), BF16)] * 3,
        args=(q, k, v, bias, gq, gk), exch=exch)


def _na_bwd(q, k, v, do, bias, layer, gq, gk, name):
    H, T, d = q.shape
    rows = T // GRID_W
    rb = _tile(rows, 8, 1)
    win = NA_WIN_R * GRID_W
    scale = d ** -0.5

    def body(q_ref, k_ref, v_ref, do_ref, b_ref, gq_ref, gk_ref,
             dq_ref, dk_ref, dv_ref, db_ref, dgq_ref, dgk_ref, qn, kn, vb, dob, dqn, dkn):
        h = pl.program_id(0)
        qv, kv = q_ref[0], k_ref[0]
        rq, rk = _rstd(qv), _rstd(kv)
        qn[...] = (qv * rq * gq_ref[...] * scale).astype(BF16)
        kn[...] = (kv * rk * gk_ref[...]).astype(BF16)
        vb[...] = v_ref[0].astype(BF16)
        dob[...] = do_ref[0].astype(BF16)
        dkn[...] = jnp.zeros_like(dkn)
        dv_ref[...] = jnp.zeros_like(dv_ref)
        db_ref[...] = jnp.zeros_like(db_ref)

        @pl.when(h == 0)
        def _():
            dgq_ref[...] = jnp.zeros_like(dgq_ref)
            dgk_ref[...] = jnp.zeros_like(dgk_ref)

        def block(bi, c):
            wsl, cls = _na_windows(bi, rb, rows)
            qsl = pl.ds(pl.multiple_of(bi * (rb * GRID_W), rb * GRID_W), rb * GRID_W)
            qs = qn[qsl, :].reshape(rb, GRID_W, d)
            dos = dob[qsl, :].reshape(rb, GRID_W, d)
            kw = jnp.stack([kn[w, :] for w in wsl])
            vw = jnp.stack([vb[w, :] for w in wsl])
            s = _bdot(qs, kw, 2, 2) + jnp.stack([b_ref[0, c_] for c_ in cls])
            p = jnp.exp(s - jnp.max(s, axis=-1, keepdims=True))
            p = p / jnp.sum(p, axis=-1, keepdims=True)
            dp = _bdot(dos, vw, 2, 2)
            ds = p * (dp - jnp.sum(dp * p, axis=-1, keepdims=True))
            dsb = ds.astype(BF16)
            dqn[qsl, :] = _bdot(dsb, kw, 2, 1).reshape(rb * GRID_W, d)
            dvw = _bdot(p.astype(BF16), dos, 1, 1)
            dkw = _bdot(dsb, qs, 1, 1)
            for i in range(rb):
                db_ref[0, cls[i]] += ds[i]
                dv_ref[0, wsl[i], :] += dvw[i]
                dkn[wsl[i], :] += dkw[i]
            return c

        lax.fori_loop(0, rows // rb, block, 0)
        dq, dgq = _rms_bwd(dqn[...] * scale, qv, gq_ref[...], rq)
        dk, dgk = _rms_bwd(dkn[...], kv, gk_ref[...], rk)
        dq_ref[0] = dq
        dk_ref[0] = dk
        dgq_ref[...] += dgq
        dgk_ref[...] += dgk

    hm = pl.BlockSpec((1, T, d), lambda h: (h, 0, 0))
    gs = pl.BlockSpec((1, d), lambda h: (0, 0))
    bs = pl.BlockSpec((1, NA_WIN_R, GRID_W, win), lambda h: (h, 0, 0, 0))
    return pl.pallas_call(
        body, name=name, grid=(H,),
        in_specs=[hm, hm, hm, hm, pl.BlockSpec((1, NA_WIN_R, GRID_W, win), lambda h: (layer * H + h, 0, 0, 0)), gs, gs],
        out_specs=[hm, hm, hm, bs, gs, gs],
        out_shape=[SDS((H, T, d), F32)] * 3 + [SDS((H, NA_WIN_R, GRID_W, win), F32), SDS((1, d), F32), SDS((1, d), F32)],
        scratch_shapes=[pltpu.VMEM((T, d), BF16)] * 4 + [pltpu.VMEM((T, d), F32)] * 2,
        compiler_params=_params(1),
    )(q, k, v, do, bias, gq, gk)


def _gla_gate_fwd(z, col, upf, upb, bf, bb, name):
    T = z.shape[0]
    W = upf.shape[1]
    tm = _tile(T, 1024)

    def body(z_ref, upf_ref, upb_ref, bf_ref, bb_ref, g_ref):
        seg = z_ref[...].astype(BF16)
        for rev, (up_ref, b_ref) in enumerate(((upf_ref, bf_ref), (upb_ref, bb_ref))):
            pre = _dot(seg, up_ref[...]) + b_ref[...]
            g = (jnp.minimum(pre, 0.0) - jnp.log(1.0 + jnp.exp(-jnp.abs(pre)))) * (1.0 / GLA_TAU)
            for h in range(GLA_HEADS):
                g_ref[rev, h] = g[:, h * GLA_DK:(h + 1) * GLA_DK]

    full = lambda shape: pl.BlockSpec(shape, lambda i: (0, 0))
    return pl.pallas_call(
        body, name=name, grid=(T // tm,),
        in_specs=[pl.BlockSpec((tm, LANE), lambda i: (i, col)), full((LANE, W)), full((LANE, W)), full((1, W)), full((1, W))],
        out_specs=pl.BlockSpec((2, GLA_HEADS, tm, GLA_DK), lambda i: (0, 0, i, 0)),
        out_shape=SDS((2, GLA_HEADS, T, GLA_DK), F32),
        compiler_params=_params(1),
    )(z, upf, upb, bf, bb)


def _gla_gate_bwd(z, col, upf, upb, bf, bb, dg, name):
    T = z.shape[0]
    W = upf.shape[1]
    tm = _tile(T, 1024)

    def body(z_ref, upf_ref, upb_ref, bf_ref, bb_ref, dg_ref, dseg_ref, dupf_ref, dupb_ref, dbf_ref, dbb_ref):
        @pl.when(pl.program_id(0) == 0)
        def _():
            for r in (dupf_ref, dupb_ref, dbf_ref, dbb_ref):
                r[...] = jnp.zeros_like(r)

        seg = z_ref[...].astype(BF16)
        dseg = jnp.zeros(dseg_ref.shape, F32)
        for rev, (up_ref, b_ref, dup_ref, db_ref) in enumerate(((upf_ref, bf_ref, dupf_ref, dbf_ref),
                                                               (upb_ref, bb_ref, dupb_ref, dbb_ref))):
            pre = _dot(seg, up_ref[...]) + b_ref[...]
            dgv = jnp.concatenate([dg_ref[rev, h] for h in range(GLA_HEADS)], axis=1)
            dpre = dgv * (1.0 / GLA_TAU) * (1.0 - _sigmoid(pre))
            dpb = dpre.astype(BF16)
            dseg = dseg + _dot(dpb, up_ref[...], NT)
            dup_ref[...] += _dot(seg, dpb, TN)
            db_ref[...] += jnp.sum(dpre, axis=0, keepdims=True)
        dseg_ref[...] = dseg

    full = lambda shape: pl.BlockSpec(shape, lambda i: (0, 0))
    return pl.pallas_call(
        body, name=name, grid=(T // tm,),
        in_specs=[pl.BlockSpec((tm, LANE), lambda i: (i, col)), full((LANE, W)), full((LANE, W)), full((1, W)), full((1, W)),
                  pl.BlockSpec((2, GLA_HEADS, tm, GLA_DK), lambda i: (0, 0, i, 0))],
        out_specs=[pl.BlockSpec((tm, LANE), lambda i: (i, 0)), full((LANE, W)), full((LANE, W)), full((1, W)), full((1, W))],
        out_shape=[SDS((T, LANE), F32), SDS((LANE, W), F32), SDS((LANE, W), F32), SDS((1, W), F32), SDS((1, W), F32)],
        compiler_params=_params(1),
    )(z, upf, upb, bf, bb, dg)


def _bdot(a, b, ca, cb, precision=None):
    return lax.dot_general(a, b, (((ca,), (cb,)), ((0,), (0,))), preferred_element_type=F32, precision=precision)


def _gla_chunk_terms(q_ref, k_ref, v_ref, g_ref, rev, tok, n, C):
    dk, dv = q_ref.shape[-1], v_ref.shape[-1]
    q = q_ref[0, tok, :].reshape(n, C, dk) * (dk ** -0.5)
    k = k_ref[0, tok, :].reshape(n, C, dk)
    v = v_ref[0, tok, :].reshape(n, C, dv)
    g = g_ref[rev, 0, tok, :].reshape(n, C, dk)
    ii = lax.broadcasted_iota(jnp.int32, (C, C), 0)
    jj = lax.broadcasted_iota(jnp.int32, (C, C), 1)
    tri = jnp.broadcast_to(((ii <= jj) if rev else (ii >= jj)).astype(F32), (n, C, C))
    amask = ((jj > ii) if rev else (ii >= jj))[None]
    b = _bdot(tri, g, 2, 1, HI)
    bl = jnp.sum(g, axis=1, keepdims=True)
    eb = jnp.exp(b)
    enb = jnp.exp(-b)
    eend = jnp.exp(bl - b)
    return q, k, v, tri, amask, bl, eb, enb, eend


def _gla_segments(T, C):
    n = T // C
    ns = _tile(n, 16, 1)

    def tok(s):
        return pl.ds(pl.multiple_of(s * (ns * C), ns * C), ns * C)

    def chunks(s):
        return pl.ds(pl.multiple_of(s * ns, ns), ns)

    return n, ns, n // ns, tok, chunks


def _gla_fwd(q, k, v, g, name):
    H, T, dk = q.shape
    dv = v.shape[-1]
    C = GLA_CHUNK
    n, ns, nseg, seg_tok, seg_chunks = _gla_segments(T, C)

    def body(q_ref, k_ref, v_ref, g_ref, o_ref, upd_scr, dec_scr, st_scr):
        for rev in (0, 1):
            def intra(s, c):
                tok, ch = seg_tok(s), seg_chunks(s)
                q, k, v, tri, amask, bl, eb, enb, eend = _gla_chunk_terms(q_ref, k_ref, v_ref, g_ref, rev, tok, ns, C)
                vb = v.astype(BF16)
                a = jnp.where(amask, _bdot((q * eb).astype(BF16), (k * enb).astype(BF16), 2, 2), 0.0).astype(BF16)
                o = _bdot(a, vb, 2, 1).reshape(ns * C, dv)
                if rev:
                    o_ref[0, tok, :] += o
                else:
                    o_ref[0, tok, :] = o
                upd_scr[ch] = _bdot(vb, (k * eend).astype(BF16), 1, 1)
                dec_scr[ch] = jnp.exp(bl)
                return c

            lax.fori_loop(0, nseg, intra, 0)

            def step(t, st):
                i = n - 1 - t if rev else t
                st_scr[i] = st
                return st * dec_scr[i] + upd_scr[i]

            lax.fori_loop(0, n, step, jnp.zeros((dv, dk), F32))

            def inter(s, c):
                tok, ch = seg_tok(s), seg_chunks(s)
                q, k, v, tri, amask, bl, eb, enb, eend = _gla_chunk_terms(q_ref, k_ref, v_ref, g_ref, rev, tok, ns, C)
                o_ref[0, tok, :] += _bdot((q * eb).astype(BF16), st_scr[ch].astype(BF16), 2, 2).reshape(ns * C, dv)
                return c

            lax.fori_loop(0, nseg, inter, 0)

    sk = pl.BlockSpec((1, T, dk), lambda h: (h, 0, 0))
    sv = pl.BlockSpec((1, T, dv), lambda h: (h, 0, 0))
    sg = pl.BlockSpec((2, 1, T, dk), lambda h: (0, h, 0, 0))
    return pl.pallas_call(
        body, name=name, grid=(H,), in_specs=[sk, sk, sv, sg], out_specs=sv,
        out_shape=SDS((H, T, dv), F32),
        scratch_shapes=[pltpu.VMEM((n, dv, dk), F32), pltpu.VMEM((n, 1, dk), F32), pltpu.VMEM((n, dv, dk), F32)],
        compiler_params=_params(1),
    )(q, k, v, g)


def _gla_bwd(q, k, v, g, do, name):
    H, T, dk = q.shape
    dv = v.shape[-1]
    C = GLA_CHUNK
    n, ns, nseg, seg_tok, seg_chunks = _gla_segments(T, C)

    def one_scan(rev, q_ref, k_ref, v_ref, g_ref, do_ref, dq_ref, dk_ref, dv_ref, dg_ref, upd_scr, dec_scr, st_scr, gn_scr,
                 rn_scr):
        def first(s, c):
            tok, ch = seg_tok(s), seg_chunks(s)
            q, k, v, tri, amask, bl, eb, enb, eend = _gla_chunk_terms(q_ref, k_ref, v_ref, g_ref, rev, tok, ns, C)
            dob = do_ref[0, tok, :].reshape(ns, C, dv).astype(BF16)
            upd_scr[ch] = _bdot(v.astype(BF16), (k * eend).astype(BF16), 1, 1)
            dec_scr[ch] = jnp.exp(bl)
            gn_scr[ch] = _bdot(dob, (q * eb).astype(BF16), 1, 1)
            return c

        lax.fori_loop(0, nseg, first, 0)

        def fstep(t, st):
            i = n - 1 - t if rev else t
            st_scr[i] = st
            return st * dec_scr[i] + upd_scr[i]

        lax.fori_loop(0, n, fstep, jnp.zeros((dv, dk), F32))

        def bstep(t, r):
            i = t if rev else n - 1 - t
            rn_scr[i] = r
            return gn_scr[i] + r * dec_scr[i]

        lax.fori_loop(0, n, bstep, jnp.zeros((dv, dk), F32))

        def second(s, c):
            tok, ch = seg_tok(s), seg_chunks(s)
            q, k, v, tri, amask, bl, eb, enb, eend = _gla_chunk_terms(q_ref, k_ref, v_ref, g_ref, rev, tok, ns, C)
            qe, ke, kend = q * eb, k * enb, k * eend
            qeb, keb, kendb, vb = qe.astype(BF16), ke.astype(BF16), kend.astype(BF16), v.astype(BF16)
            dob = do_ref[0, tok, :].reshape(ns, C, dv).astype(BF16)
            a = jnp.where(amask, _bdot(qeb, keb, 2, 2), 0.0).astype(BF16)
            st = st_scr[ch]
            rn = rn_scr[ch]
            rnb = rn.astype(BF16)
            da = jnp.where(amask, _bdot(dob, vb, 2, 2), 0.0).astype(BF16)
            dvv = _bdot(a, dob, 1, 1) + _bdot(kendb, rnb, 2, 2)
            dqe = _bdot(da, keb, 2, 1) + _bdot(dob, st.astype(BF16), 2, 1)
            dke = _bdot(da, qeb, 1, 1)
            dkend = _bdot(vb, rnb, 2, 1)
            ddec = jnp.sum(rn * st, axis=1, keepdims=True)
            dbl = ddec * jnp.exp(bl) + jnp.sum(dkend * kend, axis=1, keepdims=True)
            db = dqe * qe - dke * ke - dkend * kend
            dg = _bdot(tri, db, 1, 1, HI) + dbl
            dqv = (dqe * eb * (dk ** -0.5)).reshape(ns * C, dk)
            dkv = (dke * enb + dkend * eend).reshape(ns * C, dk)
            if rev:
                dq_ref[0, tok, :] += dqv
                dk_ref[0, tok, :] += dkv
                dv_ref[0, tok, :] += dvv.reshape(ns * C, dv)
            else:
                dq_ref[0, tok, :] = dqv
                dk_ref[0, tok, :] = dkv
                dv_ref[0, tok, :] = dvv.reshape(ns * C, dv)
            dg_ref[rev, 0, tok, :] = dg.reshape(ns * C, dk)
            return c

        lax.fori_loop(0, nseg, second, 0)

    def body(*refs):
        one_scan(0, *refs)
        one_scan(1, *refs)

    sk = pl.BlockSpec((1, T, dk), lambda h: (h, 0, 0), pipeline_mode=pl.Buffered(1))
    sv = pl.BlockSpec((1, T, dv), lambda h: (h, 0, 0), pipeline_mode=pl.Buffered(1))
    sg = pl.BlockSpec((2, 1, T, dk), lambda h: (0, h, 0, 0), pipeline_mode=pl.Buffered(1))
    st_shape = pltpu.VMEM((n, dv, dk), F32)
    return pl.pallas_call(
        body, name=name, grid=(H,), in_specs=[sk, sk, sv, sg, sv], out_specs=[sk, sk, sv, sg],
        out_shape=[SDS((H, T, dk), F32), SDS((H, T, dk), F32), SDS((H, T, dv), F32), SDS((2, H, T, dk), F32)],
        scratch_shapes=[st_shape, pltpu.VMEM((n, 1, dk), F32), st_shape, st_shape, st_shape],
        compiler_params=_params(1),
    )(q, k, v, g, do)


def _rope_tables(T):
    half = MLA_ROPE // 2
    inv = ROPE_THETA ** (-jnp.arange(half, dtype=F32) / half)
    ang = jnp.arange(T, dtype=F32)[:, None] * inv[None, :]
    cos, sin = jnp.cos(ang), jnp.sin(ang)
    ctab = jnp.concatenate([jnp.ones((T, MLA_NOPE), F32), cos, cos], axis=1)
    stab = jnp.concatenate([jnp.zeros((T, MLA_NOPE), F32), -sin, sin], axis=1)
    return ctab, stab


def _swap_rope_halves(x):
    half = MLA_ROPE // 2
    return jnp.concatenate([x[:, :MLA_NOPE], x[:, MLA_NOPE + half:], x[:, MLA_NOPE:MLA_NOPE + half]], axis=1)


def _mla_prep_fwd(x, g, ctab, stab, name):
    H, T, d = x.shape
    tm = _tile(T, 1024)

    def body(x_ref, g_ref, c_ref, s_ref, o_ref):
        xv = x_ref[0]
        xn = xv * _rstd(xv) * g_ref[...]
        o_ref[0] = (xn * c_ref[...] + _swap_rope_halves(xn) * s_ref[...]).astype(BF16)

    tab = pl.BlockSpec((tm, d), lambda h, i: (i, 0))
    return pl.pallas_call(
        body, name=name, grid=(H, T // tm),
        in_specs=[pl.BlockSpec((1, tm, d), lambda h, i: (h, i, 0)), pl.BlockSpec((1, d), lambda h, i: (0, 0)), tab, tab],
        out_specs=pl.BlockSpec((1, tm, d), lambda h, i: (h, i, 0)),
        out_shape=SDS((H, T, d), BF16),
        compiler_params=_params(2),
    )(x, g, ctab, stab)


def _mla_prep_bwd(x, g, ctab, stab, dy, name):
    H, T, d = x.shape
    tm = _tile(T, 1024)

    def body(x_ref, g_ref, c_ref, s_ref, dy_ref, dx_ref, dg_ref):
        @pl.when((pl.program_id(0) == 0) & (pl.program_id(1) == 0))
        def _():
            dg_ref[...] = jnp.zeros_like(dg_ref)

        xv = x_ref[0]
        dyv = dy_ref[0]
        dxn = dyv * c_ref[...] + _swap_rope_halves(dyv * s_ref[...])
        dx, dg = _rms_bwd(dxn, xv, g_ref[...], _rstd(xv))
        dx_ref[0] = dx
        dg_ref[...] += dg

    tab = pl.BlockSpec((tm, d), lambda h, i: (i, 0))
    blk = pl.BlockSpec((1, tm, d), lambda h, i: (h, i, 0))
    gs = pl.BlockSpec((1, d), lambda h, i: (0, 0))
    return pl.pallas_call(
        body, name=name, grid=(H, T // tm),
        in_specs=[blk, gs, tab, tab, blk],
        out_specs=[blk, gs], out_shape=[SDS((H, T, d), F32), SDS((1, d), F32)],
        compiler_params=_params(2),
    )(x, g, ctab, stab, dy)


def _mla_attn_fwd(q, k, v, name, exch=None):
    H, T, d = q.shape
    dv = v.shape[-1]
    tq = _tile(T, 256)
    scale = d ** -0.5

    def body(q_ref, k_ref, v_ref, o_ref, lse_ref):
        s = _dot(q_ref[0], k_ref[0], NT) * scale
        m = jnp.max(s, axis=-1, keepdims=True)
        p = jnp.exp(s - m)
        l = jnp.sum(p, axis=-1, keepdims=True)
        o_ref[0] = _dot((p / l).astype(BF16), v_ref[0])
        lse_ref[0] = _col_to_row(m + jnp.log(l))

    return _call(
        body, name=name, grid=(H, T // tq),
        in_specs=[pl.BlockSpec((1, tq, d), lambda h, i: (h, i, 0)), pl.BlockSpec((1, T, d), lambda h, i: (h, 0, 0)),
                  pl.BlockSpec((1, T, dv), lambda h, i: (h, 0, 0))],
        out_specs=[pl.BlockSpec((1, tq, dv), lambda h, i: (h, i, 0)), pl.BlockSpec((1, 1, tq), lambda h, i: (h, 0, i))],
        out_shape=[SDS((H, T, dv), F32), SDS((H, 1, T), F32)],
        args=(q, k, v), exch=exch)


def _col_to_row(col):
    m = col.shape[0]
    eye = lax.broadcasted_iota(jnp.int32, (m, m), 0) == lax.broadcasted_iota(jnp.int32, (m, m), 1)
    return jnp.sum(jnp.where(eye, col, 0.0), axis=0, keepdims=True)


def _mla_attn_bwd(q, k, v, o, lse, do, name, exch=None):
    H, T, d = q.shape
    dv = v.shape[-1]
    tq = _tile(T, 256)
    scale = d ** -0.5

    def body(q_ref, k_ref, v_ref, o_ref, lse_ref, do_ref, dq_ref, dk_ref, dv_ref):
        @pl.when(pl.program_id(1) == 0)
        def _():
            dk_ref[...] = jnp.zeros_like(dk_ref)
            dv_ref[...] = jnp.zeros_like(dv_ref)

        qv, kv = q_ref[0], k_ref[0]
        dov = do_ref[0]
        dob = dov.astype(BF16)
        delta_row = _col_to_row(jnp.sum(dov * o_ref[0], axis=-1, keepdims=True))
        pt = jnp.exp(_dot(kv, qv, NT) * scale - lse_ref[0])
        dst = (pt * (_dot(v_ref[0], dob, NT) - delta_row) * scale).astype(BF16)
        dv_ref[0] += _dot(pt.astype(BF16), dob)
        dk_ref[0] += _dot(dst, qv)
        dq_ref[0] = _dot(dst, kv, TN)

    qb = pl.BlockSpec((1, tq, d), lambda h, i: (h, i, 0))
    kb = pl.BlockSpec((1, T, d), lambda h, i: (h, 0, 0))
    vb = pl.BlockSpec((1, T, dv), lambda h, i: (h, 0, 0))
    ob = pl.BlockSpec((1, tq, dv), lambda h, i: (h, i, 0))
    return _call(
        body, name=name, grid=(H, T // tq),
        in_specs=[qb, kb, vb, ob, pl.BlockSpec((1, 1, tq), lambda h, i: (h, 0, i)), ob],
        out_specs=[qb, kb, vb],
        out_shape=[SDS((H, T, d), F32), SDS((H, T, d), F32), SDS((H, T, dv), F32)],
        args=(q, k, v, o, lse, do), exch=exch)


def _silu_parts(x):
    s = _sigmoid(x)
    return x * s, s * (1.0 + x * (1.0 - s))


def _mix_out_fwd(x1, y_na, o_gla, y_mla, z, gr_col, gnorm, wbr, w_out, name):
    T, D = x1.shape
    W = y_na.shape[0] * y_na.shape[2]
    tm = _tile(T, 512)

    def body(x_ref, na_ref, gla_ref, mla_ref, gt_ref, gr_ref, gn_ref, w0, w1, w2, wo_ref, o_ref):
        gn = gn_ref[...]
        nrm = jnp.concatenate([gla_ref[h] * _rstd(gla_ref[h]) * gn for h in range(GLA_HEADS)], axis=1)
        y_gla = nrm * _silu_parts(gr_ref[...])[0]
        y_na = jnp.concatenate([na_ref[h] for h in range(NA_HEADS)], axis=1)
        y_mla = jnp.concatenate([mla_ref[h] for h in range(MLA_HEADS)], axis=1)
        mixed = jnp.zeros((tm, D), F32)
        for i, (y, w_ref) in enumerate(((y_na, w0), (y_gla, w1), (y_mla, w2))):
            mixed = mixed + _sigmoid(gt_ref[:, i * D:(i + 1) * D]) * _dot(y.astype(BF16), w_ref[...])
        o_ref[...] = x_ref[...] + _dot(mixed.astype(BF16), wo_ref[...])

    tok = lambda w: pl.BlockSpec((tm, w), lambda i: (i, 0))
    hm = lambda a: pl.BlockSpec((a.shape[0], tm, a.shape[2]), lambda i: (0, i, 0))
    full = lambda shape: pl.BlockSpec(shape, lambda i: (0, 0))
    return pl.pallas_call(
        body, name=name, grid=(T // tm,),
        in_specs=[tok(D), hm(y_na), hm(o_gla), hm(y_mla), tok(3 * D), pl.BlockSpec((tm, W), lambda i: (i, gr_col)),
                  full((1, GLA_DV)), full((W, D)), full((W, D)), full((W, D)), full((D, D))],
        out_specs=tok(D), out_shape=SDS((T, D), F32),
        compiler_params=_params(1),
    )(x1, y_na, o_gla, y_mla, z, z, gnorm, wbr[0], wbr[1], wbr[2], w_out)


def _mix_out_bwd(dx2, y_na, o_gla, y_mla, z, gr_col, gnorm, wbr, w_out, name):
    T, D = dx2.shape
    W = y_na.shape[0] * y_na.shape[2]
    tm = _tile(T, 256)

    def body(dx_ref, na_ref, gla_ref, mla_ref, gt_ref, gr_ref, gn_ref, w0, w1, w2, wo_ref,
             dna_ref, dgla_ref, dmla_ref, dgr_ref, dgt_ref, dw0, dw1, dw2, dwo_ref, dgn_ref):
        @pl.when(pl.program_id(0) == 0)
        def _():
            for r in (dw0, dw1, dw2, dwo_ref, dgn_ref):
                r[...] = jnp.zeros_like(r)

        gn = gn_ref[...]
        grv = gr_ref[...]
        sil, dsil = _silu_parts(grv)
        rs = [_rstd(gla_ref[h]) for h in range(GLA_HEADS)]
        nrm = jnp.concatenate([gla_ref[h] * rs[h] * gn for h in range(GLA_HEADS)], axis=1)
        ys = (jnp.concatenate([na_ref[h] for h in range(NA_HEADS)], axis=1).astype(BF16), (nrm * sil).astype(BF16),
              jnp.concatenate([mla_ref[h] for h in range(MLA_HEADS)], axis=1).astype(BF16))
        dxb = dx_ref[...].astype(BF16)
        dmixed = _dot(dxb, wo_ref[...], NT)
        mixed = jnp.zeros((tm, D), F32)
        dys = []
        for i, (y, w_ref, dw_ref) in enumerate(zip(ys, (w0, w1, w2), (dw0, dw1, dw2))):
            gt = _sigmoid(gt_ref[:, i * D:(i + 1) * D])
            p = _dot(y, w_ref[...])
            mixed = mixed + gt * p
            dp = (dmixed * gt).astype(BF16)
            dgt_ref[:, i * D:(i + 1) * D] = (dmixed * p * gt * (1.0 - gt)).astype(BF16)
            dys.append(_dot(dp, w_ref[...], NT))
            dw_ref[...] += _dot(y, dp, TN)
        dwo_ref[...] += _dot(mixed.astype(BF16), dxb, TN)
        for h in range(NA_HEADS):
            dna_ref[h] = dys[0][:, h * NA_HD:(h + 1) * NA_HD]
        for h in range(MLA_HEADS):
            dmla_ref[h] = dys[2][:, h * MLA_V:(h + 1) * MLA_V]
        dgr_ref[...] = dys[1] * nrm * dsil
        dn = dys[1] * sil
        dgn = jnp.zeros((1, GLA_DV), F32)
        for h in range(GLA_HEADS):
            dxh, dgh = _rms_bwd(dn[:, h * GLA_DV:(h + 1) * GLA_DV], gla_ref[h], gn, rs[h])
            dgla_ref[h] = dxh
            dgn = dgn + dgh
        dgn_ref[...] += dgn

    tok = lambda w: pl.BlockSpec((tm, w), lambda i: (i, 0))
    hm = lambda a: pl.BlockSpec((a.shape[0], tm, a.shape[2]), lambda i: (0, i, 0))
    full = lambda shape: pl.BlockSpec(shape, lambda i: (0, 0))
    return pl.pallas_call(
        body, name=name, grid=(T // tm,),
        in_specs=[tok(D), hm(y_na), hm(o_gla), hm(y_mla), tok(3 * D), pl.BlockSpec((tm, W), lambda i: (i, gr_col)),
                  full((1, GLA_DV)), full((W, D)), full((W, D)), full((W, D)), full((D, D))],
        out_specs=[hm(y_na), hm(o_gla), hm(y_mla), tok(W), tok(3 * D), full((W, D)), full((W, D)), full((W, D)),
                   full((D, D)), full((1, GLA_DV))],
        out_shape=[SDS(y_na.shape, F32), SDS(o_gla.shape, F32), SDS(y_mla.shape, F32), SDS((T, W), F32),
                   SDS((T, 3 * D), BF16)] + [SDS((W, D), F32)] * 3 + [SDS((D, D), F32), SDS((1, GLA_DV), F32)],
        compiler_params=_params(1),
    )(dx2, y_na, o_gla, y_mla, z, z, gnorm, wbr[0], wbr[1], wbr[2], w_out)


def _split_z(z, lay, name):
    T = z.shape[0]
    tm = _tile(T, 512)
    na0, gv0, gq0 = lay['na_q'][0], lay['gv'][0], lay['gq'][0]
    assert na0 % (3 * NA_W) == 0 and gv0 % GLA_V_W == 0 and gq0 % (2 * GLA_QK_W) == 0
    assert lay['na_k'][0] == na0 + NA_W and lay['na_v'][0] == na0 + 2 * NA_W and lay['gk'][0] == gq0 + GLA_QK_W

    def body(na_ref, gv_ref, gqk_ref, q_ref, k_ref, v_ref, gq_ref, gk_ref, gvo_ref):
        for j, o_ref in enumerate((q_ref, k_ref, v_ref)):
            for h in range(NA_HEADS):
                o_ref[h] = na_ref[:, j * NA_W + h * NA_HD:j * NA_W + (h + 1) * NA_HD]
        for j, o_ref in enumerate((gq_ref, gk_ref)):
            for h in range(GLA_HEADS):
                o_ref[h] = gqk_ref[:, j * GLA_QK_W + h * GLA_DK:j * GLA_QK_W + (h + 1) * GLA_DK]
        for h in range(GLA_HEADS):
            gvo_ref[h] = gv_ref[:, h * GLA_DV:(h + 1) * GLA_DV]

    hm = lambda H, d: pl.BlockSpec((H, tm, d), lambda i: (0, i, 0))
    return pl.pallas_call(
        body, name=name, grid=(T // tm,),
        in_specs=[pl.BlockSpec((tm, 3 * NA_W), lambda i: (i, na0 // (3 * NA_W))),
                  pl.BlockSpec((tm, GLA_V_W), lambda i: (i, gv0 // GLA_V_W)),
                  pl.BlockSpec((tm, 2 * GLA_QK_W), lambda i: (i, gq0 // (2 * GLA_QK_W)))],
        out_specs=[hm(NA_HEADS, NA_HD)] * 3 + [hm(GLA_HEADS, GLA_DK)] * 2 + [hm(GLA_HEADS, GLA_DV)],
        out_shape=[SDS((NA_HEADS, T, NA_HD), F32)] * 3 + [SDS((GLA_HEADS, T, GLA_DK), F32)] * 2
        + [SDS((GLA_HEADS, T, GLA_DV), F32)],
        compiler_params=_params(1),
    )(z, z, z)


def _assemble_dz(dgates, na_dqkv, gla_dqk, gla_dv, dgr, dcq, dckv, dglow, dk_raw, lay, zp, name):
    T = dgr.shape[0]
    tm = _tile(T, 256)

    def body(dgt_ref, nq_ref, nk_ref, nv_ref, gq_ref, gk_ref, gv_ref, dgr_ref, dcq_ref, dckv_ref, dglow_ref, dkr_ref, dz_ref):
        def put(seg, off, val):
            a = lay[seg][0] + off
            dz_ref[:, a:a + val.shape[1]] = val.astype(BF16)

        put('gates', 0, dgt_ref[...])
        for seg, ref in (('na_q', nq_ref), ('na_k', nk_ref), ('na_v', nv_ref)):
            for h in range(NA_HEADS):
                put(seg, h * NA_HD, ref[h])
        for seg, ref in (('gq', gq_ref), ('gk', gk_ref)):
            for h in range(GLA_HEADS):
                put(seg, h * GLA_DK, ref[h])
        for h in range(GLA_HEADS):
            put('gv', h * GLA_DV, gv_ref[h])
        put('gr', 0, dgr_ref[...])
        put('c_q', 0, dcq_ref[...])
        put('c_kv', 0, dckv_ref[...])
        put('glow', 0, dglow_ref[...])
        kr = dkr_ref[0][:, MLA_NOPE:]
        for h in range(1, MLA_HEADS):
            kr = kr + dkr_ref[h][:, MLA_NOPE:]
        put('k_rope', 0, jnp.concatenate([kr, jnp.zeros((tm, LANE - MLA_ROPE), F32)], axis=1))

    tok = lambda a: pl.BlockSpec((tm, a.shape[1]), lambda i: (i, 0))
    hm = lambda a: pl.BlockSpec((a.shape[0], tm, a.shape[2]), lambda i: (0, i, 0))
    args = (dgates, *na_dqkv, *gla_dqk, gla_dv, dgr, dcq, dckv, dglow, dk_raw)
    return pl.pallas_call(
        body, name=name, grid=(T // tm,),
        in_specs=[tok(dgates)] + [hm(a) for a in (*na_dqkv, *gla_dqk, gla_dv)] + [tok(dgr), tok(dcq), tok(dckv), tok(dglow),
                                                                                 hm(dk_raw)],
        out_specs=pl.BlockSpec((tm, zp), lambda i: (i, 0)), out_shape=SDS((T, zp), BF16),
        compiler_params=_params(1),
    )(*args)


def _loss_head(y, target, name):
    T, D = y.shape
    tm = _tile(T, 1024)

    def body(y_ref, t_ref, dy_ref, l_ref):
        @pl.when(pl.program_id(0) == 0)
        def _():
            l_ref[...] = jnp.zeros_like(l_ref)

        e = y_ref[...] - t_ref[...]
        dy_ref[...] = e * (1.0 / D)
        l_ref[...] += 0.5 * jnp.sum(jnp.mean(e * e, axis=-1, keepdims=True), axis=0, keepdims=True)

    tok = pl.BlockSpec((tm, D), lambda i: (i, 0))
    return pl.pallas_call(
        body, name=name, grid=(T // tm,), in_specs=[tok, tok],
        out_specs=[tok, pl.BlockSpec((1, 1), lambda i: (0, 0))],
        out_shape=[SDS((T, D), F32), SDS((1, 1), F32)],
        compiler_params=_params(1),
    )(y, target)


def _exchange_copies(src, dst, send_sems, recv_sems, local_sems, broadcast, with_recvs=True):
    n = len(src)
    x, y, c = lax.axis_index("x"), lax.axis_index("y"), lax.axis_index("c")
    me = 4 * x + 2 * y + c
    local = [pltpu.make_async_copy(src[a] if broadcast else src[a].at[me], dst[a].at[me], local_sems.at[a])
             for a in range(n)]
    sends, recvs = [], []
    for d in range(1, N_DEV):
        px = 1 - x if d & 4 else x
        py = 1 - y if d & 2 else y
        pc = 1 - c if d & 1 else c
        peer = 4 * px + 2 * py + pc
        for a in range(n):
            for out, slot in ((sends, me), (recvs, peer)) if with_recvs else ((sends, me),):
                out.append(pltpu.make_async_remote_copy(
                    src_ref=src[a] if broadcast else src[a].at[peer], dst_ref=dst[a].at[slot],
                    send_sem=send_sems.at[a, d - 1], recv_sem=recv_sems.at[a, d - 1],
                    device_id=(px, py, pc), device_id_type=pl.DeviceIdType.MESH))
    return local, sends, recvs


def _exchange_start(*refs_and_mode):
    local, sends, _ = _exchange_copies(*refs_and_mode, with_recvs=False)
    for cp in local + sends:
        cp.start()


def _exchange_wait(*refs_and_mode):
    local, sends, recvs = _exchange_copies(*refs_and_mode)
    for cp in recvs:
        cp.wait_recv()
    for cp in sends:
        cp.wait_send()
    for cp in local:
        cp.wait()


def _exchange_shapes(srcs, broadcast):
    n = len(srcs)
    out_shape = [SDS((N_DEV,) + (tuple(s.shape) if broadcast else tuple(s.shape[1:])), s.dtype) for s in srcs]
    sems = [pltpu.SemaphoreType.DMA((n, N_DEV - 1)), pltpu.SemaphoreType.DMA((n, N_DEV - 1)),
            pltpu.SemaphoreType.DMA((n,))]
    return out_shape, sems


def _exchange(srcs, broadcast, name):
    n = len(srcs)
    out_shape, sems = _exchange_shapes(srcs, broadcast)

    def body(*refs):
        mode = (refs[:n], refs[n:2 * n]) + tuple(refs[2 * n:]) + (broadcast,)
        _exchange_start(*mode)
        _exchange_wait(*mode)

    hbm = pl.BlockSpec(memory_space=pltpu.HBM)
    return pl.pallas_call(body, name=name, in_specs=[hbm] * n, out_specs=[hbm] * n, out_shape=out_shape,
                          scratch_shapes=sems)(*srcs)


def _call(body, *, name, grid, in_specs, out_specs, out_shape, args, scratch_shapes=(), exch=None):
    params = _params(len(grid))
    if exch is None:
        return pl.pallas_call(body, name=name, grid=grid, in_specs=in_specs, out_specs=out_specs, out_shape=out_shape,
                              scratch_shapes=list(scratch_shapes), compiler_params=params)(*args)
    srcs, broadcast = exch
    n, n_in, n_out, n_scr = len(srcs), len(args), len(out_shape), len(scratch_shapes)
    x_shape, sems = _exchange_shapes(srcs, broadcast)

    def carrier(*refs):
        ins, x_src = refs[:n_in], refs[n_in:n_in + n]
        outs, x_dst = refs[n_in + n:n_in + n + n_out], refs[n_in + n + n_out:n_in + 2 * n + n_out]
        scr = refs[n_in + 2 * n + n_out:n_in + 2 * n + n_out + n_scr]
        mode = (x_src, x_dst) + tuple(refs[n_in + 2 * n + n_out + n_scr:]) + (broadcast,)
        ids = [pl.program_id(a) for a in range(len(grid))]
        first = functools.reduce(jnp.logical_and, [i == 0 for i in ids])
        last = functools.reduce(jnp.logical_and, [i == g - 1 for i, g in zip(ids, grid)])
        pl.when(first)(lambda: _exchange_start(*mode))
        body(*ins, *outs, *scr)
        pl.when(last)(lambda: _exchange_wait(*mode))

    hbm = pl.BlockSpec(memory_space=pltpu.HBM)
    res = pl.pallas_call(carrier, name=name, grid=grid, in_specs=list(in_specs) + [hbm] * n,
                         out_specs=list(out_specs) + [hbm] * n, out_shape=list(out_shape) + x_shape,
                         scratch_shapes=list(scratch_shapes) + sems, compiler_params=params)(*args, *srcs)
    return res[:n_out], res[n_out:]


def _adam_math(w, g, m, v):
    m = ADAM_B1 * m + (1.0 - ADAM_B1) * g
    v = ADAM_B2 * v + (1.0 - ADAM_B2) * (g * g)
    m_hat = m / (1.0 - ADAM_B1 ** ADAM_STEP)
    v_hat = v / (1.0 - ADAM_B2 ** ADAM_STEP)
    delta = -ADAM_LR * (m_hat / (jnp.sqrt(v_hat) + ADAM_EPS) + ADAM_WD * w)
    return delta, m, v


def _adamw_sharded(landed, k, w, m, v, name):
    L, r, c = w.shape
    lanes = -(-c // LANE) * LANE
    rb = _tile(r, max(16, (1 << 21) // (N_DEV * lanes * 2)), 16)

    def body(*refs):
        l_refs = refs[:L]
        w_ref, m_ref, v_ref, g_ref, d_ref, nm_ref, nv_ref = refs[L:]
        for j in range(L):
            @pl.when(pl.program_id(0) == j)
            def _(j=j):
                g = l_refs[j][0, 0].astype(F32)
                for s in range(1, N_DEV):
                    g = g + l_refs[j][s, 0].astype(F32)
                delta, nm, nv = _adam_math(w_ref[0], g, m_ref[0], v_ref[0])
                g_ref[0] = g
                d_ref[0] = delta
                nm_ref[0] = nm
                nv_ref[0] = nv

    blk = pl.BlockSpec((1, rb, c), lambda l, i: (l, i, 0))
    land = [pl.BlockSpec((N_DEV, 1, rb, c), lambda l, i, j=j: (0, k, jnp.where(l == j, i, 0), 0)) for j in range(L)]
    return pl.pallas_call(
        body, name=name, grid=(L, r // rb),
        in_specs=land + [blk, blk, blk],
        out_specs=[blk] * 4, out_shape=[SDS((L, r, c), F32)] * 4,
        compiler_params=_params(2),
    )(*landed, w, m, v)


def _adamw_replicated(landed, w, m, v, name):
    R = w.shape[0]

    def body(l_ref, w_ref, m_ref, v_ref, g_ref, d_ref, nm_ref, nv_ref):
        g = l_ref[0]
        for s in range(1, N_DEV):
            g = g + l_ref[s]
        delta, nm, nv = _adam_math(w_ref[...], g, m_ref[...], v_ref[...])
        g_ref[...] = g
        d_ref[...] = delta
        nm_ref[...] = nm
        nv_ref[...] = nv

    blk = pl.BlockSpec((R, LANE), lambda i: (0, 0))
    return pl.pallas_call(
        body, name=name, grid=(1,),
        in_specs=[pl.BlockSpec((N_DEV, R, LANE), lambda i: (0, 0, 0)), blk, blk, blk],
        out_specs=[blk] * 4, out_shape=[SDS((R, LANE), F32)] * 4,
        compiler_params=_params(1),
    )(landed, w, m, v)


def _heads(x, h):
    T = x.shape[0]
    return jnp.transpose(x.reshape(T, h, -1), (1, 0, 2))


def _unheads(x):
    h, T, d = x.shape
    return jnp.transpose(x, (1, 0, 2)).reshape(T, h * d)


def _pack(parts):
    flat = jnp.concatenate([p.reshape(-1) for p in parts])
    rows = -(-flat.shape[0] // (8 * LANE)) * 8
    return jnp.pad(flat, (0, rows * LANE - flat.shape[0])).reshape(rows, LANE)


def _unpack(packed, shapes):
    flat = packed.reshape(-1)
    out, off = [], 0
    for s in shapes:
        size = int(np.prod(s))
        out.append(flat[off:off + size].reshape(s))
        off += size
    return out


def _layer_fwd(x0, w, l, lay, tabs, carry):
    T, D = x0.shape
    ctab, stab, na_bias = tabs
    col = lambda name: lay[name][0] // lay[name][1]
    sv = {'x0': x0}
    x1, sv['a1'], sv['b1'] = _carried(_ffn_fwd, carry, 'ffn1', x0, w['ffn1_norm'], w['ffn1_w1'], w['ffn1_w3'],
                                      w['ffn1_w2'], f"ffn1_fwd_{l}")
    z, sv['hb'] = _carried(_norm_matmul, carry, 'mix_in', x1, 0, w['mix_norm'], w['w_in_z'], f"mix_in_{l}")
    seg = lambda name: z[:, lay[name][0]:lay[name][0] + lay[name][3]]
    sv['na_q'], sv['na_k'], sv['na_v'], sv['gq'], sv['gk'], sv['gv'] = _split_z(z, lay, f"split_z_{l}")
    y_na = _carried(_na_fwd, carry, 'na', sv['na_q'], sv['na_k'], sv['na_v'], na_bias, l, w['na_q_norm'], w['na_k_norm'],
                    f"na_fwd_{l}")[0]
    sv['gg'] = _gla_gate_fwd(z, col('glow'), w['up_f'], w['up_b'], w['gla_gf_bias'], w['gla_gb_bias'], f"gla_gate_{l}")
    o_gla = _gla_fwd(sv['gq'], sv['gk'], sv['gv'], sv['gg'], f"gla_fwd_{l}")
    q_raw, sv['hq'] = _norm_matmul(z, col('c_q'), w['mla_cq_norm'], w['mla_w_uq'], f"mla_uq_{l}")
    kv_raw, sv['hkv'] = _norm_matmul(z, col('c_kv'), w['mla_ckv_norm'], w['mla_w_ukv'], f"mla_ukv_{l}")
    kv_h = _heads(kv_raw, MLA_HEADS)
    k_rope = jnp.broadcast_to(seg('k_rope')[None], (MLA_HEADS, T, MLA_ROPE))
    sv['mq_raw'] = _heads(q_raw, MLA_HEADS)
    sv['mk_raw'] = jnp.concatenate([kv_h[..., :MLA_NOPE], k_rope], axis=-1)
    sv['mq'] = _mla_prep_fwd(sv['mq_raw'], w['mla_q_norm'], ctab, stab, f"mla_qprep_{l}")
    sv['mk'] = _mla_prep_fwd(sv['mk_raw'], w['mla_k_norm'], ctab, stab, f"mla_kprep_{l}")
    sv['mv'] = kv_h[..., MLA_NOPE:].astype(BF16)
    y_mla, sv['lse'] = _carried(_mla_attn_fwd, carry, 'mla', sv['mq'], sv['mk'], sv['mv'], f"mla_attn_{l}")
    w_br = (w['w_br_na'], w['w_br_gla'], w['w_br_mla'])
    x2 = _mix_out_fwd(x1, y_na, o_gla, y_mla, z, col('gr'), w['gla_out_norm'], w_br, w['w_out'], f"mix_out_{l}")
    sv.update(x1=x1, z=z, y_na=y_na, o_gla=o_gla, y_mla=y_mla, x2=x2)
    x3, sv['a2'], sv['b2'] = _carried(_ffn_fwd, carry, 'ffn2', x2, w['ffn2_norm'], w['ffn2_w1'], w['ffn2_w3'],
                                      w['ffn2_w2'], f"ffn2_fwd_{l}")
    return x3, sv


def _ffn_grads(grads, x, dy, a, b, w, pre, l, carry):
    dx, da, db, u, hb, dyb, dg = _carried(_ffn_bwd, carry, pre, x, w[pre + '_norm'], dy, a, b, w[pre + '_w1'],
                                          w[pre + '_w3'], w[pre + '_w2'], f"{pre}_bwd_{l}")
    grads.update({pre + '_norm': dg,
                  pre + '_w1': _matmul_tn(hb, da, f"{pre}_dw1_{l}"),
                  pre + '_w3': _matmul_tn(hb, db, f"{pre}_dw3_{l}"),
                  pre + '_w2': _matmul_tn(u, dyb, f"{pre}_dw2_{l}")})
    return dx


def _layer_bwd(grads, dx3, sv, w, l, lay, tabs, carry):
    T, D = dx3.shape
    ctab, stab, na_bias, onehot = tabs
    col = lambda name: lay[name][0] // lay[name][1]
    z = sv['z']
    dx2 = _ffn_grads(grads, sv['x2'], dx3, sv['a2'], sv['b2'], w, 'ffn2', l, carry)
    w_br = (w['w_br_na'], w['w_br_gla'], w['w_br_mla'])
    (dy_na, do_gla, dy_mla, dgr, dgates, dw0, dw1, dw2, dwo, dgn) = _mix_out_bwd(
        dx2, sv['y_na'], sv['o_gla'], sv['y_mla'], z, col('gr'), w['gla_out_norm'], w_br, w['w_out'], f"mix_out_bwd_{l}")
    grads.update(w_br_na=dw0, w_br_gla=dw1, w_br_mla=dw2, w_out=dwo, gla_out_norm=dgn)
    dq, dk, dv, dbias, dgq, dgk = _na_bwd(sv['na_q'], sv['na_k'], sv['na_v'], dy_na, na_bias, l,
                                          w['na_q_norm'], w['na_k_norm'], f"na_bwd_{l}")
    grads.update(na_q_norm=dgq, na_k_norm=dgk, na_rpb=_na_bias_tables_bwd(dbias, onehot))
    gdq, gdk, gdv, gdg = _gla_bwd(sv['gq'], sv['gk'], sv['gv'], sv['gg'], do_gla, f"gla_bwd_{l}")
    dglow, dupf, dupb, dbf, dbb = _gla_gate_bwd(z, col('glow'), w['up_f'], w['up_b'], w['gla_gf_bias'], w['gla_gb_bias'],
                                                gdg, f"gla_gate_bwd_{l}")
    grads.update(gla_gf_up=dupf[:GLA_RANK], gla_gb_up=dupb[GLA_RANK:2 * GLA_RANK], gla_gf_bias=dbf, gla_gb_bias=dbb)
    mdq, mdk, mdv = _carried(_mla_attn_bwd, carry, 'mla', sv['mq'], sv['mk'], sv['mv'], sv['y_mla'], sv['lse'], dy_mla,
                             f"mla_attn_bwd_{l}")
    dq_raw, dgqn = _mla_prep_bwd(sv['mq_raw'], w['mla_q_norm'], ctab, stab, mdq, f"mla_qprep_bwd_{l}")
    dk_raw, dgkn = _mla_prep_bwd(sv['mk_raw'], w['mla_k_norm'], ctab, stab, mdk, f"mla_kprep_bwd_{l}")
    dq_raw = _unheads(dq_raw).astype(BF16)
    dkv_raw = _unheads(jnp.concatenate([dk_raw[..., :MLA_NOPE], mdv], axis=-1)).astype(BF16)
    dcq, dgcq = _norm_matmul_bwd(dq_raw, w['mla_w_uq'], z, col('c_q'), w['mla_cq_norm'], None, f"mla_uq_bwd_{l}")
    dckv, dgckv = _norm_matmul_bwd(dkv_raw, w['mla_w_ukv'], z, col('c_kv'), w['mla_ckv_norm'], None, f"mla_ukv_bwd_{l}")
    grads.update(mla_q_norm=dgqn, mla_k_norm=dgkn, mla_cq_norm=dgcq, mla_ckv_norm=dgckv,
                 mla_w_uq=_matmul_tn(sv['hq'], dq_raw, f"mla_duq_{l}", F32),
                 mla_w_ukv=_matmul_tn(sv['hkv'], dkv_raw, f"mla_dukv_{l}", F32))
    dz = _assemble_dz(dgates, (dq, dk, dv), (gdq, gdk), gdv, dgr, dcq, dckv, dglow, dk_raw, lay, z.shape[1], f"dz_{l}")
    dx1, dgmix = _carried(_norm_matmul_bwd, carry, 'mix_in', dz, w['w_in_z'], sv['x1'], 0, w['mix_norm'], dx2,
                          f"mix_in_bwd_{l}")
    grads.update(mix_norm=dgmix, w_in=_matmul_tn(sv['hb'], dz, f"mix_dw_in_{l}"))
    return _ffn_grads(grads, sv['x0'], dx1, sv['a1'], sv['b1'], w, 'ffn1', l, carry)


EXCHANGE_PARTS = {
    'F1a': [['ffn1_w1', 'ffn1_w3']],
    'F1b': [['ffn1_w2']],
    'F2': [['ffn2_w1', 'ffn2_w3'], ['ffn2_w2']],
    'C': [['w_in']],
    'S': [['w_br_na', 'w_br_gla', 'w_br_mla'], ['w_out'], ['mla_w_uq'], ['mla_w_ukv'], ['gla_gf_up', 'gla_gb_up']],
}
FWD_PLAN = {'ffn1': [(0, 'C')], 'mix_in': [(0, 'S')], 'mla': [(0, 'F2')], 'na': [(1, 'F1b')], 'ffn2': [(1, 'F1a')]}
BWD_PLAN = {'mla': [(0, 'F2')], 'mix_in': [(0, 'S')], 'ffn1': [(0, 'C')], 'ffn2': [(1, 'F1a'), (1, 'F1b')]}
EDGE_PARTS = ['F1a', 'F1b']


def _carried(fn, carry, key, *args):
    if key in carry:
        make_srcs, broadcast, deliver = carry[key]
        outs, landed = fn(*args, exch=(make_srcs(), broadcast))
        deliver(landed)
        return outs
    return fn(*args)


def kernel(x, ffn1_norm, ffn1_w1, ffn1_w3, ffn1_w2, mix_norm, w_in, na_q_norm, na_k_norm, na_rpb, gla_gf_up, gla_gf_bias, gla_gb_up, gla_gb_bias, gla_out_norm, mla_cq_norm, mla_ckv_norm, mla_w_uq, mla_w_ukv, mla_q_norm, mla_k_norm, w_br_na, w_br_gla, w_br_mla, w_out, ffn2_norm, ffn2_w1, ffn2_w3, ffn2_w2, loss_target, m_ffn1_norm, m_ffn1_w1, m_ffn1_w3, m_ffn1_w2, m_mix_norm, m_w_in, m_na_q_norm, m_na_k_norm, m_na_rpb, m_gla_gf_up, m_gla_gf_bias, m_gla_gb_up, m_gla_gb_bias, m_gla_out_norm, m_mla_cq_norm, m_mla_ckv_norm, m_mla_w_uq, m_mla_w_ukv, m_mla_q_norm, m_mla_k_norm, m_w_br_na, m_w_br_gla, m_w_br_mla, m_w_out, m_ffn2_norm, m_ffn2_w1, m_ffn2_w3, m_ffn2_w2, v_ffn1_norm, v_ffn1_w1, v_ffn1_w3, v_ffn1_w2, v_mix_norm, v_w_in, v_na_q_norm, v_na_k_norm, v_na_rpb, v_gla_gf_up, v_gla_gf_bias, v_gla_gb_up, v_gla_gb_bias, v_gla_out_norm, v_mla_cq_norm, v_mla_ckv_norm, v_mla_w_uq, v_mla_w_ukv, v_mla_q_norm, v_mla_k_norm, v_w_br_na, v_w_br_gla, v_w_br_mla, v_w_out, v_ffn2_norm, v_ffn2_w1, v_ffn2_w3, v_ffn2_w2):
    given = dict(locals())
    P = {n: given[n] for n in WEIGHTS}
    M = {n: given['m_' + n] for n in WEIGHTS}
    V = {n: given['v_' + n] for n in WEIGHTS}
    xs = x[0]
    T, D = xs.shape
    L = ffn1_norm.shape[0]
    lay, zp, d_in = _z_layout(D)

    def arrays_of(items):
        return [(l, names) for l, p in items for names in EXCHANGE_PARTS[p]]

    ws = [{n: P[n][l][None, :] for n in REPLICATED if n != 'na_rpb'} for l in range(L)]

    def gather_srcs(items):
        return [jnp.stack([P[n][l].astype(BF16) for n in names]) for l, names in arrays_of(items)]

    def take_gathered(items, res):
        for (l, names), g in zip(arrays_of(items), res):
            for k, n in enumerate(names):
                blk = g[:, k]
                if n == 'w_in':
                    ws[l]['w_in_z'] = _z_cols_from_blocks(blk, lay)
                    continue
                if n in COL_SHARDED:
                    full = jnp.concatenate([blk[d] for d in range(N_DEV)], axis=1)
                else:
                    full = blk.reshape(N_DEV * blk.shape[1], blk.shape[2])
                if n == 'gla_gf_up':
                    ws[l]['up_f'] = jnp.zeros((LANE, GLA_QK_W), BF16).at[:GLA_RANK].set(full)
                elif n == 'gla_gb_up':
                    ws[l]['up_b'] = jnp.zeros((LANE, GLA_QK_W), BF16).at[GLA_RANK:2 * GLA_RANK].set(full)
                else:
                    ws[l][n] = full

    def plan(table, l, make_srcs, broadcast, deliver):
        carry = {}
        for key, items in table.items():
            items = [(l + off, p) for off, p in items if l + off < L]
            if items:
                carry[key] = (functools.partial(make_srcs, items), broadcast, functools.partial(deliver, items))
        return carry

    ctab, stab = _rope_tables(T)
    na_bias, onehot = _na_bias_tables(na_rpb)

    edge = [(0, p) for p in EDGE_PARTS]
    take_gathered(edge, _exchange(gather_srcs(edge), True, "gather_weights_edge"))
    saved = []
    h = xs
    for l in range(L):
        h, sv = _layer_fwd(h, ws[l], l, lay, (ctab, stab, na_bias), plan(FWD_PLAN, l, gather_srcs, True, take_gathered))
        saved.append(sv)
    dh, loss_local = _loss_head(h, loss_target[0], "loss_head")
    loss = lax.psum(loss_local[0, 0], ("x", "y", "c"))

    layer_grads = [dict() for _ in range(L)]

    def scatter_srcs(items):
        def blocks(l, n):
            g = layer_grads[l][n]
            if n == 'w_in':
                return _blocks_from_z_cols(g, lay, d_in).astype(BF16)
            if n in COL_SHARDED:
                c = g.shape[1] // N_DEV
                return jnp.stack([g[:, d * c:(d + 1) * c] for d in range(N_DEV)]).astype(BF16)
            return g.reshape(N_DEV, -1, g.shape[1]).astype(BF16)
        return [jnp.stack([blocks(l, n) for n in names], axis=1) for l, names in arrays_of(items)]

    landed = [dict() for _ in range(L)]

    def take_landed(items, res):
        for (l, names), r in zip(arrays_of(items), res):
            for k, n in enumerate(names):
                landed[l][n] = (r, k)

    for l in reversed(range(L)):
        dh = _layer_bwd(layer_grads[l], dh, saved[l], ws[l], l, lay, (ctab, stab, na_bias, onehot),
                        plan(BWD_PLAN, l, scatter_srcs, False, take_landed))
    grad_x = dh[None]
    take_landed(edge, _exchange(scatter_srcs(edge), False, "scatter_grads_edge"))

    out = {}
    for n in SHARDED:
        out[n] = _adamw_sharded([landed[l][n][0] for l in range(L)], landed[0][n][1], P[n], M[n], V[n], f"adamw_{n}")

    rep = [n for n in REPLICATED]
    rep_shapes = [tuple(P[n].shape) for n in rep]
    rep_partial = _pack([jnp.stack([layer_grads[l][n].reshape(P[n].shape[1:]) for l in range(L)]) for n in rep])
    (rep_landed,) = _exchange([rep_partial], True, "gather_small_grads")
    packed = _adamw_replicated(rep_landed, _pack([P[n] for n in rep]), _pack([M[n] for n in rep]),
                               _pack([V[n] for n in rep]), "adamw_replicated")
    for kind, pk in enumerate(packed):
        for n, arr in zip(rep, _unpack(pk, rep_shapes)):
            out.setdefault(n, [None] * 4)[kind] = arr

    res = [loss, grad_x]
    for kind in range(4):
        res += [out[n][kind] for n in WEIGHTS]
    return tuple(res)
```

```python
import functools

import numpy as np
import jax
import jax.numpy as jnp
from jax import lax
from jax.experimental import pallas as pl
from jax.experimental.pallas import tpu as pltpu

F32 = jnp.float32
BF16 = jnp.bfloat16
SDS = jax.ShapeDtypeStruct
HI = lax.Precision.HIGHEST

N_DEV = 8
DEPTH = 4
EPS = 1e-6
LANE = 128
VMEM_LIMIT = 56 * 1024 * 1024

GRID_W = 64
NA_HEADS, NA_HD, NA_WIN_R, NA_WIN_C = 8, 64, 8, 16
GLA_HEADS, GLA_DK, GLA_DV, GLA_RANK, GLA_TAU, GLA_CHUNK = 4, 64, 128, 16, 16.0, 64
MLA_HEADS, MLA_QR, MLA_KVR, MLA_NOPE, MLA_ROPE, MLA_V = 4, 256, 256, 128, 64, 128
MLA_QK = MLA_NOPE + MLA_ROPE
ROPE_THETA = 10000.0
NA_W = NA_HEADS * NA_HD
GLA_QK_W = GLA_HEADS * GLA_DK
GLA_V_W = GLA_HEADS * GLA_DV
MLA_V_W = MLA_HEADS * MLA_V

ADAM_LR, ADAM_B1, ADAM_B2, ADAM_EPS, ADAM_WD, ADAM_STEP = 0.001, 0.9, 0.999, 1e-08, 0.01, 10

WEIGHTS = ['ffn1_norm', 'ffn1_w1', 'ffn1_w3', 'ffn1_w2', 'mix_norm', 'w_in', 'na_q_norm', 'na_k_norm', 'na_rpb',
           'gla_gf_up', 'gla_gf_bias', 'gla_gb_up', 'gla_gb_bias', 'gla_out_norm', 'mla_cq_norm', 'mla_ckv_norm',
           'mla_w_uq', 'mla_w_ukv', 'mla_q_norm', 'mla_k_norm', 'w_br_na', 'w_br_gla', 'w_br_mla', 'w_out',
           'ffn2_norm', 'ffn2_w1', 'ffn2_w3', 'ffn2_w2']
COL_SHARDED = ['ffn1_w1', 'ffn1_w3', 'w_in', 'gla_gf_up', 'gla_gb_up', 'mla_w_uq', 'mla_w_ukv', 'w_br_na', 'w_br_gla',
               'w_br_mla', 'ffn2_w1', 'ffn2_w3']
ROW_SHARDED = ['ffn1_w2', 'w_out', 'ffn2_w2']
SHARDED = [n for n in WEIGHTS if n in COL_SHARDED or n in ROW_SHARDED]
REPLICATED = [n for n in WEIGHTS if n not in SHARDED]

NT = (((1,), (1,)), ((), ()))
TN = (((0,), (0,)), ((), ()))


def _tile(n, pref, mult=8):
    best = None
    for t in range(mult, min(n, pref) + 1, mult):
        if n % t == 0:
            best = t
    return best if best is not None else n


def _params(n_axes):
    return pltpu.CompilerParams(dimension_semantics=("arbitrary",) * n_axes, vmem_limit_bytes=VMEM_LIMIT)


def _dot(a, b, dims=None, precision=None):
    if dims is None:
        return jnp.dot(a, b, preferred_element_type=F32, precision=precision)
    return lax.dot_general(a, b, dims, preferred_element_type=F32, precision=precision)


def _sigmoid(x):
    return 1.0 / (1.0 + jnp.exp(-x))


def _rstd(x):
    return lax.rsqrt(jnp.mean(x * x, axis=-1, keepdims=True) + EPS)


def _rms_bwd(dy, x, g, r):
    xh = x * r
    dyg = dy * g
    dx = r * (dyg - xh * jnp.mean(dyg * xh, axis=-1, keepdims=True))
    return dx, jnp.sum(dy * xh, axis=0, keepdims=True)


def _z_layout(d_model):
    own = {}
    off = 0
    for name, w in [('na_q', NA_W), ('na_k', NA_W), ('na_v', NA_W), ('gq', GLA_QK_W), ('gk', GLA_QK_W),
                    ('gv', GLA_V_W), ('gr', GLA_V_W), ('gfl', GLA_RANK), ('gbl', GLA_RANK), ('c_q', MLA_QR),
                    ('c_kv', MLA_KVR), ('k_rope', MLA_ROPE), ('gates', 3 * d_model)]:
        own[name] = (off, w)
        off += w
    order = [('gates', 3 * d_model), ('na_q', NA_W), ('na_k', NA_W), ('na_v', NA_W), ('gv', GLA_V_W), ('gr', GLA_V_W),
             ('gq', GLA_QK_W), ('gk', GLA_QK_W), ('c_q', MLA_QR), ('c_kv', MLA_KVR), ('glow', LANE), ('k_rope', LANE)]
    lay = {}
    z = 0
    for name, w in order:
        assert z % w == 0, (name, z, w)
        if name == 'glow':
            lay[name] = (z, w, own['gfl'][0], 2 * GLA_RANK)
        else:
            lay[name] = (z, w, own[name][0], own[name][1])
        z += w
    return lay, z, off


def _z_cols_from_blocks(blocks, lay):
    cs = blocks.shape[2]
    parts = []
    for zo, zw, oo, ow in lay.values():
        a = oo
        while a < oo + ow:
            b = min(oo + ow, (a // cs + 1) * cs)
            parts.append(blocks[a // cs, :, a % cs:a % cs + (b - a)])
            a = b
        if zw > ow:
            parts.append(jnp.zeros((blocks.shape[1], zw - ow), blocks.dtype))
    return jnp.concatenate(parts, axis=1)


def _blocks_from_z_cols(wz, lay, d_in):
    cs = d_in // N_DEV
    own = sorted(lay.values(), key=lambda t: t[2])
    out = []
    for k in range(N_DEV):
        parts = []
        for zo, zw, oo, ow in own:
            a, b = max(oo, k * cs), min(oo + ow, (k + 1) * cs)
            if a < b:
                parts.append(wz[:, zo + a - oo:zo + b - oo])
        out.append(jnp.concatenate(parts, axis=1))
    return jnp.stack(out)


def _ffn_fwd(x, g, w1, w3, w2, name, exch=None):
    T, D = x.shape
    Fd = w1.shape[1]
    tm, tf = _tile(T, 1024), _tile(Fd, 256, LANE)
    nj = Fd // tf

    def body(x_ref, g_ref, w1_ref, w3_ref, w2_ref, o_ref, a_ref, b_ref, h_scr, acc):
        j = pl.program_id(1)

        @pl.when(j == 0)
        def _():
            xv = x_ref[...]
            h_scr[...] = (xv * _rstd(xv) * g_ref[...]).astype(BF16)
            acc[...] = jnp.zeros_like(acc)

        h = h_scr[...]
        a = _dot(h, w1_ref[...])
        b = _dot(h, w3_ref[...])
        a_ref[...] = a.astype(BF16)
        b_ref[...] = b.astype(BF16)
        u = a * _sigmoid(a) * b
        acc[...] += _dot(u.astype(BF16), w2_ref[...])

        @pl.when(j == nj - 1)
        def _():
            o_ref[...] = x_ref[...] + 0.5 * acc[...]

    return _call(
        body, name=name, grid=(T // tm, nj),
        in_specs=[pl.BlockSpec((tm, D), lambda i, j: (i, 0)), pl.BlockSpec((1, D), lambda i, j: (0, 0)),
                  pl.BlockSpec((D, tf), lambda i, j: (0, j)), pl.BlockSpec((D, tf), lambda i, j: (0, j)),
                  pl.BlockSpec((tf, D), lambda i, j: (j, 0))],
        out_specs=[pl.BlockSpec((tm, D), lambda i, j: (i, 0)), pl.BlockSpec((tm, tf), lambda i, j: (i, j)),
                   pl.BlockSpec((tm, tf), lambda i, j: (i, j))],
        out_shape=[SDS((T, D), F32), SDS((T, Fd), BF16), SDS((T, Fd), BF16)],
        scratch_shapes=[pltpu.VMEM((tm, D), BF16), pltpu.VMEM((tm, D), F32)],
        args=(x, g, w1, w3, w2), exch=exch)


def _ffn_bwd(x, g, dy, a, b, w1, w3, w2, name, exch=None):
    T, D = x.shape
    Fd = w1.shape[1]
    tm, tf = _tile(T, 512), _tile(Fd, 256, LANE)
    nj = Fd // tf

    def body(x_ref, g_ref, dy_ref, a_ref, b_ref, w1_ref, w3_ref, w2_ref,
             dx_ref, da_ref, db_ref, u_ref, hb_ref, dyb_ref, dg_ref, dh_acc):
        i, j = pl.program_id(0), pl.program_id(1)

        @pl.when(j == 0)
        def _():
            xv = x_ref[...]
            hb_ref[...] = (xv * _rstd(xv) * g_ref[...]).astype(BF16)
            dyb_ref[...] = (0.5 * dy_ref[...]).astype(BF16)
            dh_acc[...] = jnp.zeros_like(dh_acc)

        @pl.when((i == 0) & (j == 0))
        def _():
            dg_ref[...] = jnp.zeros_like(dg_ref)

        du = _dot(dyb_ref[...], w2_ref[...], NT)
        av = a_ref[...].astype(F32)
        bv = b_ref[...].astype(F32)
        s = _sigmoid(av)
        sl = av * s
        da = (du * bv * (s * (1.0 + av * (1.0 - s)))).astype(BF16)
        db = (du * sl).astype(BF16)
        da_ref[...] = da
        db_ref[...] = db
        u_ref[...] = (sl * bv).astype(BF16)
        dh_acc[...] += _dot(da, w1_ref[...], NT) + _dot(db, w3_ref[...], NT)

        @pl.when(j == nj - 1)
        def _():
            xv = x_ref[...]
            dxn, dg = _rms_bwd(dh_acc[...], xv, g_ref[...], _rstd(xv))
            dx_ref[...] = dy_ref[...] + dxn
            dg_ref[...] += dg

    row = lambda i, j: (i, 0)
    return _call(
        body, name=name, grid=(T // tm, nj),
        in_specs=[pl.BlockSpec((tm, D), row), pl.BlockSpec((1, D), lambda i, j: (0, 0)), pl.BlockSpec((tm, D), row),
                  pl.BlockSpec((tm, tf), lambda i, j: (i, j)), pl.BlockSpec((tm, tf), lambda i, j: (i, j)),
                  pl.BlockSpec((D, tf), lambda i, j: (0, j)), pl.BlockSpec((D, tf), lambda i, j: (0, j)),
                  pl.BlockSpec((tf, D), lambda i, j: (j, 0))],
        out_specs=[pl.BlockSpec((tm, D), row), pl.BlockSpec((tm, tf), lambda i, j: (i, j)),
                   pl.BlockSpec((tm, tf), lambda i, j: (i, j)), pl.BlockSpec((tm, tf), lambda i, j: (i, j)),
                   pl.BlockSpec((tm, D), row), pl.BlockSpec((tm, D), row), pl.BlockSpec((1, D), lambda i, j: (0, 0))],
        out_shape=[SDS((T, D), F32), SDS((T, Fd), BF16), SDS((T, Fd), BF16), SDS((T, Fd), BF16),
                   SDS((T, D), BF16), SDS((T, D), BF16), SDS((1, D), F32)],
        scratch_shapes=[pltpu.VMEM((tm, D), F32)],
        args=(x, g, dy, a, b, w1, w3, w2), exch=exch)


def _matmul_tn(a, b, name, out_dtype=BF16):
    T, K = a.shape
    N = b.shape[1]
    tk = _tile(K, 1408, LANE)
    tn = _tile(N, 1408, LANE)
    tt = _tile(T, 1024)
    nt = T // tt

    def body(a_ref, b_ref, o_ref, acc):
        t = pl.program_id(2)

        @pl.when(t == 0)
        def _():
            acc[...] = jnp.zeros_like(acc)

        acc[...] += _dot(a_ref[...], b_ref[...], TN)

        @pl.when(t == nt - 1)
        def _():
            o_ref[...] = acc[...].astype(out_dtype)

    return pl.pallas_call(
        body, name=name, grid=(K // tk, N // tn, nt),
        in_specs=[pl.BlockSpec((tt, tk), lambda k, n, t: (t, k)), pl.BlockSpec((tt, tn), lambda k, n, t: (t, n))],
        out_specs=pl.BlockSpec((tk, tn), lambda k, n, t: (k, n)),
        out_shape=SDS((K, N), out_dtype),
        scratch_shapes=[pltpu.VMEM((tk, tn), F32)],
        compiler_params=_params(3),
    )(a, b)


def _norm_matmul(x, xcol, g, w, name, exch=None):
    T = x.shape[0]
    D = g.shape[1]
    N = w.shape[1]
    tm, tn = _tile(T, 1024), _tile(N, 768, LANE)

    def body(x_ref, g_ref, w_ref, z_ref, hb_ref):
        @pl.when(pl.program_id(1) == 0)
        def _():
            xv = x_ref[...]
            hb_ref[...] = (xv * _rstd(xv) * g_ref[...]).astype(BF16)

        z_ref[...] = _dot(hb_ref[...], w_ref[...])

    return _call(
        body, name=name, grid=(T // tm, N // tn),
        in_specs=[pl.BlockSpec((tm, D), lambda i, j: (i, xcol)), pl.BlockSpec((1, D), lambda i, j: (0, 0)),
                  pl.BlockSpec((D, tn), lambda i, j: (0, j))],
        out_specs=[pl.BlockSpec((tm, tn), lambda i, j: (i, j)), pl.BlockSpec((tm, D), lambda i, j: (i, 0))],
        out_shape=[SDS((T, N), F32), SDS((T, D), BF16)],
        args=(x, g, w), exch=exch)


def _norm_matmul_bwd(dz, w, x, xcol, g, resid, name, exch=None):
    T, N = dz.shape
    D = g.shape[1]
    tm, tn = _tile(T, 1024), _tile(N, 768, LANE)
    nj = N // tn
    has_resid = resid is not None

    def body(*refs):
        if has_resid:
            dz_ref, w_ref, x_ref, g_ref, r_ref, dx_ref, dg_ref, acc = refs
        else:
            dz_ref, w_ref, x_ref, g_ref, dx_ref, dg_ref, acc = refs
        i, j = pl.program_id(0), pl.program_id(1)

        @pl.when(j == 0)
        def _():
            acc[...] = jnp.zeros_like(acc)

        @pl.when((i == 0) & (j == 0))
        def _():
            dg_ref[...] = jnp.zeros_like(dg_ref)

        acc[...] += _dot(dz_ref[...], w_ref[...], NT)

        @pl.when(j == nj - 1)
        def _():
            xv = x_ref[...]
            dxn, dg = _rms_bwd(acc[...], xv, g_ref[...], _rstd(xv))
            dx_ref[...] = dxn + r_ref[...] if has_resid else dxn
            dg_ref[...] += dg

    in_specs = [pl.BlockSpec((tm, tn), lambda i, j: (i, j)), pl.BlockSpec((D, tn), lambda i, j: (0, j)),
                pl.BlockSpec((tm, D), lambda i, j: (i, xcol)), pl.BlockSpec((1, D), lambda i, j: (0, 0))]
    args = [dz, w, x, g]
    if has_resid:
        in_specs.append(pl.BlockSpec((tm, D), lambda i, j: (i, 0)))
        args.append(resid)
    return _call(
        body, name=name, grid=(T // tm, nj), in_specs=in_specs,
        out_specs=[pl.BlockSpec((tm, D), lambda i, j: (i, 0)), pl.BlockSpec((1, D), lambda i, j: (0, 0))],
        out_shape=[SDS((T, D), F32), SDS((1, D), F32)],
        scratch_shapes=[pltpu.VMEM((tm, D), F32)],
        args=args, exch=exch)


def _na_bias_tables(rpb):
    L = rpb.shape[0]
    qc = np.arange(GRID_W)[:, None]
    kc = np.arange(GRID_W)[None, :]
    dc = np.clip(kc - qc + NA_WIN_C - 1, 0, 2 * NA_WIN_C - 2)
    c0 = np.clip(qc - NA_WIN_C // 2, 0, GRID_W - NA_WIN_C)
    ok = (kc >= c0) & (kc < c0 + NA_WIN_C)
    onehot = (dc.reshape(-1)[None, :] == np.arange(2 * NA_WIN_C - 1)[:, None]).astype(np.float32)
    t1 = jnp.stack([rpb[:, :, c:c + NA_WIN_R, :] for c in range(NA_WIN_R)], axis=2)
    full = jnp.dot(t1.reshape(-1, 2 * NA_WIN_C - 1), jnp.asarray(onehot), precision=HI)
    full = full.reshape(L, NA_HEADS, NA_WIN_R, NA_WIN_R, GRID_W, GRID_W)
    full = jnp.where(jnp.asarray(ok)[None, None, None, None], full, -1e30)
    return jnp.transpose(full, (0, 1, 2, 4, 3, 5)).reshape(L * NA_HEADS, NA_WIN_R, GRID_W, NA_WIN_R * GRID_W), onehot


def _na_bias_tables_bwd(db, onehot):
    d = db.reshape(NA_HEADS, NA_WIN_R, GRID_W, NA_WIN_R, GRID_W)
    d = jnp.transpose(d, (0, 1, 3, 2, 4)).reshape(-1, GRID_W * GRID_W)
    t1 = jnp.dot(d, jnp.asarray(onehot.T), precision=HI).reshape(NA_HEADS, NA_WIN_R, NA_WIN_R, 2 * NA_WIN_C - 1)
    out = jnp.zeros((NA_HEADS, 2 * NA_WIN_R - 1, 2 * NA_WIN_C - 1), F32)
    for c in range(NA_WIN_R):
        out = out.at[:, c:c + NA_WIN_R, :].add(t1[:, c])
    return out


def _na_windows(bi, rb, rows):
    wsl, cls = [], []
    for i in range(rb):
        r = bi * rb + i
        r0 = jnp.clip(r - NA_WIN_R // 2, 0, rows - NA_WIN_R)
        wsl.append(pl.ds(pl.multiple_of(r0 * GRID_W, GRID_W), NA_WIN_R * GRID_W))
        cls.append(r0 - r + NA_WIN_R - 1)
    return wsl, cls


def _na_fwd(q, k, v, bias, layer, gq, gk, name, exch=None):
    H, T, d = q.shape
    rows = T // GRID_W
    rb = _tile(rows, 8, 1)
    win = NA_WIN_R * GRID_W
    scale = d ** -0.5

    def body(q_ref, k_ref, v_ref, b_ref, gq_ref, gk_ref, o_ref, qn, kn, vb):
        qv, kv = q_ref[0], k_ref[0]
        qn[...] = (qv * _rstd(qv) * gq_ref[...] * scale).astype(BF16)
        kn[...] = (kv * _rstd(kv) * gk_ref[...]).astype(BF16)
        vb[...] = v_ref[0].astype(BF16)

        def block(bi, c):
            wsl, cls = _na_windows(bi, rb, rows)
            qsl = pl.ds(pl.multiple_of(bi * (rb * GRID_W), rb * GRID_W), rb * GRID_W)
            kw = jnp.stack([kn[w, :] for w in wsl])
            vw = jnp.stack([vb[w, :] for w in wsl])
            s = _bdot(qn[qsl, :].reshape(rb, GRID_W, d), kw, 2, 2) + jnp.stack([b_ref[0, c_] for c_ in cls])
            p = jnp.exp(s - jnp.max(s, axis=-1, keepdims=True))
            p = p / jnp.sum(p, axis=-1, keepdims=True)
            o_ref[0, qsl, :] = _bdot(p.astype(BF16), vw, 2, 1).reshape(rb * GRID_W, d)
            return c

        lax.fori_loop(0, rows // rb, block, 0)

    hm = pl.BlockSpec((1, T, d), lambda h: (h, 0, 0))
    gs = pl.BlockSpec((1, d), lambda h: (0, 0))
    return _call(
        body, name=name, grid=(H,),
        in_specs=[hm, hm, hm, pl.BlockSpec((1, NA_WIN_R, GRID_W, win), lambda h: (layer * H + h, 0, 0, 0)), gs, gs],
        out_specs=[hm], out_shape=[SDS((H, T, d), F32)],
        scratch_shapes=[pltpu.VMEM((T, d), BF16)] * 3,
        args=(q, k, v, bias, gq, gk), exch=exch)


def _na_bwd(q, k, v, do, bias, layer, gq, gk, name):
    H, T, d = q.shape
    rows = T // GRID_W
    rb = _tile(rows, 8, 1)
    win = NA_WIN_R * GRID_W
    scale = d ** -0.5

    def body(q_ref, k_ref, v_ref, do_ref, b_ref, gq_ref, gk_ref,
             dq_ref, dk_ref, dv_ref, db_ref, dgq_ref, dgk_ref, qn, kn, vb, dob, dqn, dkn):
        h = pl.program_id(0)
        qv, kv = q_ref[0], k_ref[0]
        rq, rk = _rstd(qv), _rstd(kv)
        qn[...] = (qv * rq * gq_ref[...] * scale).astype(BF16)
        kn[...] = (kv * rk * gk_ref[...]).astype(BF16)
        vb[...] = v_ref[0].astype(BF16)
        dob[...] = do_ref[0].astype(BF16)
        dkn[...] = jnp.zeros_like(dkn)
        dv_ref[...] = jnp.zeros_like(dv_ref)
        db_ref[...] = jnp.zeros_like(db_ref)

        @pl.when(h == 0)
        def _():
            dgq_ref[...] = jnp.zeros_like(dgq_ref)
            dgk_ref[...] = jnp.zeros_like(dgk_ref)

        def block(bi, c):
            wsl, cls = _na_windows(bi, rb, rows)
            qsl = pl.ds(pl.multiple_of(bi * (rb * GRID_W), rb * GRID_W), rb * GRID_W)
            qs = qn[qsl, :].reshape(rb, GRID_W, d)
            dos = dob[qsl, :].reshape(rb, GRID_W, d)
            kw = jnp.stack([kn[w, :] for w in wsl])
            vw = jnp.stack([vb[w, :] for w in wsl])
            s = _bdot(qs, kw, 2, 2) + jnp.stack([b_ref[0, c_] for c_ in cls])
            p = jnp.exp(s - jnp.max(s, axis=-1, keepdims=True))
            p = p / jnp.sum(p, axis=-1, keepdims=True)
            dp = _bdot(dos, vw, 2, 2)
            ds = p * (dp - jnp.sum(dp * p, axis=-1, keepdims=True))
            dsb = ds.astype(BF16)
            dqn[qsl, :] = _bdot(dsb, kw, 2, 1).reshape(rb * GRID_W, d)
            dvw = _bdot(p.astype(BF16), dos, 1, 1)
            dkw = _bdot(dsb, qs, 1, 1)
            for i in range(rb):
                db_ref[0, cls[i]] += ds[i]
                dv_ref[0, wsl[i], :] += dvw[i]
                dkn[wsl[i], :] += dkw[i]
            return c

        lax.fori_loop(0, rows // rb, block, 0)
        dq, dgq = _rms_bwd(dqn[...] * scale, qv, gq_ref[...], rq)
        dk, dgk = _rms_bwd(dkn[...], kv, gk_ref[...], rk)
        dq_ref[0] = dq
        dk_ref[0] = dk
        dgq_ref[...] += dgq
        dgk_ref[...] += dgk

    hm = pl.BlockSpec((1, T, d), lambda h: (h, 0, 0))
    gs = pl.BlockSpec((1, d), lambda h: (0, 0))
    bs = pl.BlockSpec((1, NA_WIN_R, GRID_W, win), lambda h: (h, 0, 0, 0))
    return pl.pallas_call(
        body, name=name, grid=(H,),
        in_specs=[hm, hm, hm, hm, pl.BlockSpec((1, NA_WIN_R, GRID_W, win), lambda h: (layer * H + h, 0, 0, 0)), gs, gs],
        out_specs=[hm, hm, hm, bs, gs, gs],
        out_shape=[SDS((H, T, d), F32)] * 3 + [SDS((H, NA_WIN_R, GRID_W, win), F32), SDS((1, d), F32), SDS((1, d), F32)],
        scratch_shapes=[pltpu.VMEM((T, d), BF16)] * 4 + [pltpu.VMEM((T, d), F32)] * 2,
        compiler_params=_params(1),
    )(q, k, v, do, bias, gq, gk)


def _gla_gate_fwd(z, col, upf, upb, bf, bb, name):
    T = z.shape[0]
    W = upf.shape[1]
    tm = _tile(T, 1024)

    def body(z_ref, upf_ref, upb_ref, bf_ref, bb_ref, g_ref):
        seg = z_ref[...].astype(BF16)
        for rev, (up_ref, b_ref) in enumerate(((upf_ref, bf_ref), (upb_ref, bb_ref))):
            pre = _dot(seg, up_ref[...]) + b_ref[...]
            g = (jnp.minimum(pre, 0.0) - jnp.log(1.0 + jnp.exp(-jnp.abs(pre)))) * (1.0 / GLA_TAU)
            for h in range(GLA_HEADS):
                g_ref[rev, h] = g[:, h * GLA_DK:(h + 1) * GLA_DK]

    full = lambda shape: pl.BlockSpec(shape, lambda i: (0, 0))
    return pl.pallas_call(
        body, name=name, grid=(T // tm,),
        in_specs=[pl.BlockSpec((tm, LANE), lambda i: (i, col)), full((LANE, W)), full((LANE, W)), full((1, W)), full((1, W))],
        out_specs=pl.BlockSpec((2, GLA_HEADS, tm, GLA_DK), lambda i: (0, 0, i, 0)),
        out_shape=SDS((2, GLA_HEADS, T, GLA_DK), F32),
        compiler_params=_params(1),
    )(z, upf, upb, bf, bb)


def _gla_gate_bwd(z, col, upf, upb, bf, bb, dg, name):
    T = z.shape[0]
    W = upf.shape[1]
    tm = _tile(T, 1024)

    def body(z_ref, upf_ref, upb_ref, bf_ref, bb_ref, dg_ref, dseg_ref, dupf_ref, dupb_ref, dbf_ref, dbb_ref):
        @pl.when(pl.program_id(0) == 0)
        def _():
            for r in (dupf_ref, dupb_ref, dbf_ref, dbb_ref):
                r[...] = jnp.zeros_like(r)

        seg = z_ref[...].astype(BF16)
        dseg = jnp.zeros(dseg_ref.shape, F32)
        for rev, (up_ref, b_ref, dup_ref, db_ref) in enumerate(((upf_ref, bf_ref, dupf_ref, dbf_ref),
                                                               (upb_ref, bb_ref, dupb_ref, dbb_ref))):
            pre = _dot(seg, up_ref[...]) + b_ref[...]
            dgv = jnp.concatenate([dg_ref[rev, h] for h in range(GLA_HEADS)], axis=1)
            dpre = dgv * (1.0 / GLA_TAU) * (1.0 - _sigmoid(pre))
            dpb = dpre.astype(BF16)
            dseg = dseg + _dot(dpb, up_ref[...], NT)
            dup_ref[...] += _dot(seg, dpb, TN)
            db_ref[...] += jnp.sum(dpre, axis=0, keepdims=True)
        dseg_ref[...] = dseg

    full = lambda shape: pl.BlockSpec(shape, lambda i: (0, 0))
    return pl.pallas_call(
        body, name=name, grid=(T // tm,),
        in_specs=[pl.BlockSpec((tm, LANE), lambda i: (i, col)), full((LANE, W)), full((LANE, W)), full((1, W)), full((1, W)),
                  pl.BlockSpec((2, GLA_HEADS, tm, GLA_DK), lambda i: (0, 0, i, 0))],
        out_specs=[pl.BlockSpec((tm, LANE), lambda i: (i, 0)), full((LANE, W)), full((LANE, W)), full((1, W)), full((1, W))],
        out_shape=[SDS((T, LANE), F32), SDS((LANE, W), F32), SDS((LANE, W), F32), SDS((1, W), F32), SDS((1, W), F32)],
        compiler_params=_params(1),
    )(z, upf, upb, bf, bb, dg)


def _bdot(a, b, ca, cb, precision=None):
    return lax.dot_general(a, b, (((ca,), (cb,)), ((0,), (0,))), preferred_element_type=F32, precision=precision)


def _gla_chunk_terms(q_ref, k_ref, v_ref, g_ref, rev, tok, n, C):
    dk, dv = q_ref.shape[-1], v_ref.shape[-1]
    q = q_ref[0, tok, :].reshape(n, C, dk) * (dk ** -0.5)
    k = k_ref[0, tok, :].reshape(n, C, dk)
    v = v_ref[0, tok, :].reshape(n, C, dv)
    g = g_ref[rev, 0, tok, :].reshape(n, C, dk)
    ii = lax.broadcasted_iota(jnp.int32, (C, C), 0)
    jj = lax.broadcasted_iota(jnp.int32, (C, C), 1)
    tri = jnp.broadcast_to(((ii <= jj) if rev else (ii >= jj)).astype(F32), (n, C, C))
    amask = ((jj > ii) if rev else (ii >= jj))[None]
    b = _bdot(tri, g, 2, 1, HI)
    bl = jnp.sum(g, axis=1, keepdims=True)
    eb = jnp.exp(b)
    enb = jnp.exp(-b)
    eend = jnp.exp(bl - b)
    return q, k, v, tri, amask, bl, eb, enb, eend


def _gla_segments(T, C):
    n = T // C
    ns = _tile(n, 16, 1)

    def tok(s):
        return pl.ds(pl.multiple_of(s * (ns * C), ns * C), ns * C)

    def chunks(s):
        return pl.ds(pl.multiple_of(s * ns, ns), ns)

    return n, ns, n // ns, tok, chunks


def _gla_fwd(q, k, v, g, name):
    H, T, dk = q.shape
    dv = v.shape[-1]
    C = GLA_CHUNK
    n, ns, nseg, seg_tok, seg_chunks = _gla_segments(T, C)

    def body(q_ref, k_ref, v_ref, g_ref, o_ref, upd_scr, dec_scr, st_scr):
        for rev in (0, 1):
            def intra(s, c):
                tok, ch = seg_tok(s), seg_chunks(s)
                q, k, v, tri, amask, bl, eb, enb, eend = _gla_chunk_terms(q_ref, k_ref, v_ref, g_ref, rev, tok, ns, C)
                vb = v.astype(BF16)
                a = jnp.where(amask, _bdot((q * eb).astype(BF16), (k * enb).astype(BF16), 2, 2), 0.0).astype(BF16)
                o = _bdot(a, vb, 2, 1).reshape(ns * C, dv)
                if rev:
                    o_ref[0, tok, :] += o
                else:
                    o_ref[0, tok, :] = o
                upd_scr[ch] = _bdot(vb, (k * eend).astype(BF16), 1, 1)
                dec_scr[ch] = jnp.exp(bl)
                return c

            lax.fori_loop(0, nseg, intra, 0)

            def step(t, st):
                i = n - 1 - t if rev else t
                st_scr[i] = st
                return st * dec_scr[i] + upd_scr[i]

            lax.fori_loop(0, n, step, jnp.zeros((dv, dk), F32))

            def inter(s, c):
                tok, ch = seg_tok(s), seg_chunks(s)
                q, k, v, tri, amask, bl, eb, enb, eend = _gla_chunk_terms(q_ref, k_ref, v_ref, g_ref, rev, tok, ns, C)
                o_ref[0, tok, :] += _bdot((q * eb).astype(BF16), st_scr[ch].astype(BF16), 2, 2).reshape(ns * C, dv)
                return c

            lax.fori_loop(0, nseg, inter, 0)

    sk = pl.BlockSpec((1, T, dk), lambda h: (h, 0, 0))
    sv = pl.BlockSpec((1, T, dv), lambda h: (h, 0, 0))
    sg = pl.BlockSpec((2, 1, T, dk), lambda h: (0, h, 0, 0))
    return pl.pallas_call(
        body, name=name, grid=(H,), in_specs=[sk, sk, sv, sg], out_specs=sv,
        out_shape=SDS((H, T, dv), F32),
        scratch_shapes=[pltpu.VMEM((n, dv, dk), F32), pltpu.VMEM((n, 1, dk), F32), pltpu.VMEM((n, dv, dk), F32)],
        compiler_params=_params(1),
    )(q, k, v, g)


def _gla_bwd(q, k, v, g, do, name):
    H, T, dk = q.shape
    dv = v.shape[-1]
    C = GLA_CHUNK
    n, ns, nseg, seg_tok, seg_chunks = _gla_segments(T, C)

    def one_scan(rev, q_ref, k_ref, v_ref, g_ref, do_ref, dq_ref, dk_ref, dv_ref, dg_ref, upd_scr, dec_scr, st_scr, gn_scr,
                 rn_scr):
        def first(s, c):
            tok, ch = seg_tok(s), seg_chunks(s)
            q, k, v, tri, amask, bl, eb, enb, eend = _gla_chunk_terms(q_ref, k_ref, v_ref, g_ref, rev, tok, ns, C)
            dob = do_ref[0, tok, :].reshape(ns, C, dv).astype(BF16)
            upd_scr[ch] = _bdot(v.astype(BF16), (k * eend).astype(BF16), 1, 1)
            dec_scr[ch] = jnp.exp(bl)
            gn_scr[ch] = _bdot(dob, (q * eb).astype(BF16), 1, 1)
            return c

        lax.fori_loop(0, nseg, first, 0)

        def fstep(t, st):
            i = n - 1 - t if rev else t
            st_scr[i] = st
            return st * dec_scr[i] + upd_scr[i]

        lax.fori_loop(0, n, fstep, jnp.zeros((dv, dk), F32))

        def bstep(t, r):
            i = t if rev else n - 1 - t
            rn_scr[i] = r
            return gn_scr[i] + r * dec_scr[i]

        lax.fori_loop(0, n, bstep, jnp.zeros((dv, dk), F32))

        def second(s, c):
            tok, ch = seg_tok(s), seg_chunks(s)
            q, k, v, tri, amask, bl, eb, enb, eend = _gla_chunk_terms(q_ref, k_ref, v_ref, g_ref, rev, tok, ns, C)
            qe, ke, kend = q * eb, k * enb, k * eend
            qeb, keb, kendb, vb = qe.astype(BF16), ke.astype(BF16), kend.astype(BF16), v.astype(BF16)
            dob = do_ref[0, tok, :].reshape(ns, C, dv).astype(BF16)
            a = jnp.where(amask, _bdot(qeb, keb, 2, 2), 0.0).astype(BF16)
            st = st_scr[ch]
            rn = rn_scr[ch]
            rnb = rn.astype(BF16)
            da = jnp.where(amask, _bdot(dob, vb, 2, 2), 0.0).astype(BF16)
            dvv = _bdot(a, dob, 1, 1) + _bdot(kendb, rnb, 2, 2)
            dqe = _bdot(da, keb, 2, 1) + _bdot(dob, st.astype(BF16), 2, 1)
            dke = _bdot(da, qeb, 1, 1)
            dkend = _bdot(vb, rnb, 2, 1)
            ddec = jnp.sum(rn * st, axis=1, keepdims=True)
            dbl = ddec * jnp.exp(bl) + jnp.sum(dkend * kend, axis=1, keepdims=True)
            db = dqe * qe - dke * ke - dkend * kend
            dg = _bdot(tri, db, 1, 1, HI) + dbl
            dqv = (dqe * eb * (dk ** -0.5)).reshape(ns * C, dk)
            dkv = (dke * enb + dkend * eend).reshape(ns * C, dk)
            if rev:
                dq_ref[0, tok, :] += dqv
                dk_ref[0, tok, :] += dkv
                dv_ref[0, tok, :] += dvv.reshape(ns * C, dv)
            else:
                dq_ref[0, tok, :] = dqv
                dk_ref[0, tok, :] = dkv
                dv_ref[0, tok, :] = dvv.reshape(ns * C, dv)
            dg_ref[rev, 0, tok, :] = dg.reshape(ns * C, dk)
            return c

        lax.fori_loop(0, nseg, second, 0)

    def body(*refs):
        one_scan(0, *refs)
        one_scan(1, *refs)

    sk = pl.BlockSpec((1, T, dk), lambda h: (h, 0, 0), pipeline_mode=pl.Buffered(1))
    sv = pl.BlockSpec((1, T, dv), lambda h: (h, 0, 0), pipeline_mode=pl.Buffered(1))
    sg = pl.BlockSpec((2, 1, T, dk), lambda h: (0, h, 0, 0), pipeline_mode=pl.Buffered(1))
    st_shape = pltpu.VMEM((n, dv, dk), F32)
    return pl.pallas_call(
        body, name=name, grid=(H,), in_specs=[sk, sk, sv, sg, sv], out_specs=[sk, sk, sv, sg],
        out_shape=[SDS((H, T, dk), F32), SDS((H, T, dk), F32), SDS((H, T, dv), F32), SDS((2, H, T, dk), F32)],
        scratch_shapes=[st_shape, pltpu.VMEM((n, 1, dk), F32), st_shape, st_shape, st_shape],
        compiler_params=_params(1),
    )(q, k, v, g, do)


def _rope_tables(T):
    half = MLA_ROPE // 2
    inv = ROPE_THETA ** (-jnp.arange(half, dtype=F32) / half)
    ang = jnp.arange(T, dtype=F32)[:, None] * inv[None, :]
    cos, sin = jnp.cos(ang), jnp.sin(ang)
    ctab = jnp.concatenate([jnp.ones((T, MLA_NOPE), F32), cos, cos], axis=1)
    stab = jnp.concatenate([jnp.zeros((T, MLA_NOPE), F32), -sin, sin], axis=1)
    return ctab, stab


def _swap_rope_halves(x):
    half = MLA_ROPE // 2
    return jnp.concatenate([x[:, :MLA_NOPE], x[:, MLA_NOPE + half:], x[:, MLA_NOPE:MLA_NOPE + half]], axis=1)


def _mla_prep_fwd(x, g, ctab, stab, name):
    H, T, d = x.shape
    tm = _tile(T, 1024)

    def body(x_ref, g_ref, c_ref, s_ref, o_ref):
        xv = x_ref[0]
        xn = xv * _rstd(xv) * g_ref[...]
        o_ref[0] = (xn * c_ref[...] + _swap_rope_halves(xn) * s_ref[...]).astype(BF16)

    tab = pl.BlockSpec((tm, d), lambda h, i: (i, 0))
    return pl.pallas_call(
        body, name=name, grid=(H, T // tm),
        in_specs=[pl.BlockSpec((1, tm, d), lambda h, i: (h, i, 0)), pl.BlockSpec((1, d), lambda h, i: (0, 0)), tab, tab],
        out_specs=pl.BlockSpec((1, tm, d), lambda h, i: (h, i, 0)),
        out_shape=SDS((H, T, d), BF16),
        compiler_params=_params(2),
    )(x, g, ctab, stab)


def _mla_prep_bwd(x, g, ctab, stab, dy, name):
    H, T, d = x.shape
    tm = _tile(T, 1024)

    def body(x_ref, g_ref, c_ref, s_ref, dy_ref, dx_ref, dg_ref):
        @pl.when((pl.program_id(0) == 0) & (pl.program_id(1) == 0))
        def _():
            dg_ref[...] = jnp.zeros_like(dg_ref)

        xv = x_ref[0]
        dyv = dy_ref[0]
        dxn = dyv * c_ref[...] + _swap_rope_halves(dyv * s_ref[...])
        dx, dg = _rms_bwd(dxn, xv, g_ref[...], _rstd(xv))
        dx_ref[0] = dx
        dg_ref[...] += dg

    tab = pl.BlockSpec((tm, d), lambda h, i: (i, 0))
    blk = pl.BlockSpec((1, tm, d), lambda h, i: (h, i, 0))
    gs = pl.BlockSpec((1, d), lambda h, i: (0, 0))
    return pl.pallas_call(
        body, name=name, grid=(H, T // tm),
        in_specs=[blk, gs, tab, tab, blk],
        out_specs=[blk, gs], out_shape=[SDS((H, T, d), F32), SDS((1, d), F32)],
        compiler_params=_params(2),
    )(x, g, ctab, stab, dy)


def _mla_attn_fwd(q, k, v, name, exch=None):
    H, T, d = q.shape
    dv = v.shape[-1]
    tq = _tile(T, 256)
    scale = d ** -0.5

    def body(q_ref, k_ref, v_ref, o_ref, lse_ref):
        s = _dot(q_ref[0], k_ref[0], NT) * scale
        m = jnp.max(s, axis=-1, keepdims=True)
        p = jnp.exp(s - m)
        l = jnp.sum(p, axis=-1, keepdims=True)
        o_ref[0] = _dot((p / l).astype(BF16), v_ref[0])
        lse_ref[0] = _col_to_row(m + jnp.log(l))

    return _call(
        body, name=name, grid=(H, T // tq),
        in_specs=[pl.BlockSpec((1, tq, d), lambda h, i: (h, i, 0)), pl.BlockSpec((1, T, d), lambda h, i: (h, 0, 0)),
                  pl.BlockSpec((1, T, dv), lambda h, i: (h, 0, 0))],
        out_specs=[pl.BlockSpec((1, tq, dv), lambda h, i: (h, i, 0)), pl.BlockSpec((1, 1, tq), lambda h, i: (h, 0, i))],
        out_shape=[SDS((H, T, dv), F32), SDS((H, 1, T), F32)],
        args=(q, k, v), exch=exch)


def _col_to_row(col):
    m = col.shape[0]
    eye = lax.broadcasted_iota(jnp.int32, (m, m), 0) == lax.broadcasted_iota(jnp.int32, (m, m), 1)
    return jnp.sum(jnp.where(eye, col, 0.0), axis=0, keepdims=True)


def _mla_attn_bwd(q, k, v, o, lse, do, name, exch=None):
    H, T, d = q.shape
    dv = v.shape[-1]
    tq = _tile(T, 256)
    scale = d ** -0.5

    def body(q_ref, k_ref, v_ref, o_ref, lse_ref, do_ref, dq_ref, dk_ref, dv_ref):
        @pl.when(pl.program_id(1) == 0)
        def _():
            dk_ref[...] = jnp.zeros_like(dk_ref)
            dv_ref[...] = jnp.zeros_like(dv_ref)

        qv, kv = q_ref[0], k_ref[0]
        dov = do_ref[0]
        dob = dov.astype(BF16)
        delta_row = _col_to_row(jnp.sum(dov * o_ref[0], axis=-1, keepdims=True))
        pt = jnp.exp(_dot(kv, qv, NT) * scale - lse_ref[0])
        dst = (pt * (_dot(v_ref[0], dob, NT) - delta_row) * scale).astype(BF16)
        dv_ref[0] += _dot(pt.astype(BF16), dob)
        dk_ref[0] += _dot(dst, qv)
        dq_ref[0] = _dot(dst, kv, TN)

    qb = pl.BlockSpec((1, tq, d), lambda h, i: (h, i, 0))
    kb = pl.BlockSpec((1, T, d), lambda h, i: (h, 0, 0))
    vb = pl.BlockSpec((1, T, dv), lambda h, i: (h, 0, 0))
    ob = pl.BlockSpec((1, tq, dv), lambda h, i: (h, i, 0))
    return _call(
        body, name=name, grid=(H, T // tq),
        in_specs=[qb, kb, vb, ob, pl.BlockSpec((1, 1, tq), lambda h, i: (h, 0, i)), ob],
        out_specs=[qb, kb, vb],
        out_shape=[SDS((H, T, d), F32), SDS((H, T, d), F32), SDS((H, T, dv), F32)],
        args=(q, k, v, o, lse, do), exch=exch)


def _silu_parts(x):
    s = _sigmoid(x)
    return x * s, s * (1.0 + x * (1.0 - s))


def _mix_out_fwd(x1, y_na, o_gla, y_mla, z, gr_col, gnorm, wbr, w_out, name):
    T, D = x1.shape
    W = y_na.shape[0] * y_na.shape[2]
    tm = _tile(T, 512)

    def body(x_ref, na_ref, gla_ref, mla_ref, gt_ref, gr_ref, gn_ref, w0, w1, w2, wo_ref, o_ref):
        gn = gn_ref[...]
        nrm = jnp.concatenate([gla_ref[h] * _rstd(gla_ref[h]) * gn for h in range(GLA_HEADS)], axis=1)
        y_gla = nrm * _silu_parts(gr_ref[...])[0]
        y_na = jnp.concatenate([na_ref[h] for h in range(NA_HEADS)], axis=1)
        y_mla = jnp.concatenate([mla_ref[h] for h in range(MLA_HEADS)], axis=1)
        mixed = jnp.zeros((tm, D), F32)
        for i, (y, w_ref) in enumerate(((y_na, w0), (y_gla, w1), (y_mla, w2))):
            mixed = mixed + _sigmoid(gt_ref[:, i * D:(i + 1) * D]) * _dot(y.astype(BF16), w_ref[...])
        o_ref[...] = x_ref[...] + _dot(mixed.astype(BF16), wo_ref[...])

    tok = lambda w: pl.BlockSpec((tm, w), lambda i: (i, 0))
    hm = lambda a: pl.BlockSpec((a.shape[0], tm, a.shape[2]), lambda i: (0, i, 0))
    full = lambda shape: pl.BlockSpec(shape, lambda i: (0, 0))
    return pl.pallas_call(
        body, name=name, grid=(T // tm,),
        in_specs=[tok(D), hm(y_na), hm(o_gla), hm(y_mla), tok(3 * D), pl.BlockSpec((tm, W), lambda i: (i, gr_col)),
                  full((1, GLA_DV)), full((W, D)), full((W, D)), full((W, D)), full((D, D))],
        out_specs=tok(D), out_shape=SDS((T, D), F32),
        compiler_params=_params(1),
    )(x1, y_na, o_gla, y_mla, z, z, gnorm, wbr[0], wbr[1], wbr[2], w_out)


def _mix_out_bwd(dx2, y_na, o_gla, y_mla, z, gr_col, gnorm, wbr, w_out, name):
    T, D = dx2.shape
    W = y_na.shape[0] * y_na.shape[2]
    tm = _tile(T, 256)

    def body(dx_ref, na_ref, gla_ref, mla_ref, gt_ref, gr_ref, gn_ref, w0, w1, w2, wo_ref,
             dna_ref, dgla_ref, dmla_ref, dgr_ref, dgt_ref, dw0, dw1, dw2, dwo_ref, dgn_ref):
        @pl.when(pl.program_id(0) == 0)
        def _():
            for r in (dw0, dw1, dw2, dwo_ref, dgn_ref):
                r[...] = jnp.zeros_like(r)

        gn = gn_ref[...]
        grv = gr_ref[...]
        sil, dsil = _silu_parts(grv)
        rs = [_rstd(gla_ref[h]) for h in range(GLA_HEADS)]
        nrm = jnp.concatenate([gla_ref[h] * rs[h] * gn for h in range(GLA_HEADS)], axis=1)
        ys = (jnp.concatenate([na_ref[h] for h in range(NA_HEADS)], axis=1).astype(BF16), (nrm * sil).astype(BF16),
              jnp.concatenate([mla_ref[h] for h in range(MLA_HEADS)], axis=1).astype(BF16))
        dxb = dx_ref[...].astype(BF16)
        dmixed = _dot(dxb, wo_ref[...], NT)
        mixed = jnp.zeros((tm, D), F32)
        dys = []
        for i, (y, w_ref, dw_ref) in enumerate(zip(ys, (w0, w1, w2), (dw0, dw1, dw2))):
            gt = _sigmoid(gt_ref[:, i * D:(i + 1) * D])
            p = _dot(y, w_ref[...])
            mixed = mixed + gt * p
            dp = (dmixed * gt).astype(BF16)
            dgt_ref[:, i * D:(i + 1) * D] = (dmixed * p * gt * (1.0 - gt)).astype(BF16)
            dys.append(_dot(dp, w_ref[...], NT))
            dw_ref[...] += _dot(y, dp, TN)
        dwo_ref[...] += _dot(mixed.astype(BF16), dxb, TN)
        for h in range(NA_HEADS):
            dna_ref[h] = dys[0][:, h * NA_HD:(h + 1) * NA_HD]
        for h in range(MLA_HEADS):
            dmla_ref[h] = dys[2][:, h * MLA_V:(h + 1) * MLA_V]
        dgr_ref[...] = dys[1] * nrm * dsil
        dn = dys[1] * sil
        dgn = jnp.zeros((1, GLA_DV), F32)
        for h in range(GLA_HEADS):
            dxh, dgh = _rms_bwd(dn[:, h * GLA_DV:(h + 1) * GLA_DV], gla_ref[h], gn, rs[h])
            dgla_ref[h] = dxh
            dgn = dgn + dgh
        dgn_ref[...] += dgn

    tok = lambda w: pl.BlockSpec((tm, w), lambda i: (i, 0))
    hm = lambda a: pl.BlockSpec((a.shape[0], tm, a.shape[2]), lambda i: (0, i, 0))
    full = lambda shape: pl.BlockSpec(shape, lambda i: (0, 0))
    return pl.pallas_call(
        body, name=name, grid=(T // tm,),
        in_specs=[tok(D), hm(y_na), hm(o_gla), hm(y_mla), tok(3 * D), pl.BlockSpec((tm, W), lambda i: (i, gr_col)),
                  full((1, GLA_DV)), full((W, D)), full((W, D)), full((W, D)), full((D, D))],
        out_specs=[hm(y_na), hm(o_gla), hm(y_mla), tok(W), tok(3 * D), full((W, D)), full((W, D)), full((W, D)),
                   full((D, D)), full((1, GLA_DV))],
        out_shape=[SDS(y_na.shape, F32), SDS(o_gla.shape, F32), SDS(y_mla.shape, F32), SDS((T, W), F32),
                   SDS((T, 3 * D), BF16)] + [SDS((W, D), F32)] * 3 + [SDS((D, D), F32), SDS((1, GLA_DV), F32)],
        compiler_params=_params(1),
    )(dx2, y_na, o_gla, y_mla, z, z, gnorm, wbr[0], wbr[1], wbr[2], w_out)


def _split_z(z, lay, name):
    T = z.shape[0]
    tm = _tile(T, 512)
    na0, gv0, gq0 = lay['na_q'][0], lay['gv'][0], lay['gq'][0]
    assert na0 % (3 * NA_W) == 0 and gv0 % GLA_V_W == 0 and gq0 % (2 * GLA_QK_W) == 0
    assert lay['na_k'][0] == na0 + NA_W and lay['na_v'][0] == na0 + 2 * NA_W and lay['gk'][0] == gq0 + GLA_QK_W

    def body(na_ref, gv_ref, gqk_ref, q_ref, k_ref, v_ref, gq_ref, gk_ref, gvo_ref):
        for j, o_ref in enumerate((q_ref, k_ref, v_ref)):
            for h in range(NA_HEADS):
                o_ref[h] = na_ref[:, j * NA_W + h * NA_HD:j * NA_W + (h + 1) * NA_HD]
        for j, o_ref in enumerate((gq_ref, gk_ref)):
            for h in range(GLA_HEADS):
                o_ref[h] = gqk_ref[:, j * GLA_QK_W + h * GLA_DK:j * GLA_QK_W + (h + 1) * GLA_DK]
        for h in range(GLA_HEADS):
            gvo_ref[h] = gv_ref[:, h * GLA_DV:(h + 1) * GLA_DV]

    hm = lambda H, d: pl.BlockSpec((H, tm, d), lambda i: (0, i, 0))
    return pl.pallas_call(
        body, name=name, grid=(T // tm,),
        in_specs=[pl.BlockSpec((tm, 3 * NA_W), lambda i: (i, na0 // (3 * NA_W))),
                  pl.BlockSpec((tm, GLA_V_W), lambda i: (i, gv0 // GLA_V_W)),
                  pl.BlockSpec((tm, 2 * GLA_QK_W), lambda i: (i, gq0 // (2 * GLA_QK_W)))],
        out_specs=[hm(NA_HEADS, NA_HD)] * 3 + [hm(GLA_HEADS, GLA_DK)] * 2 + [hm(GLA_HEADS, GLA_DV)],
        out_shape=[SDS((NA_HEADS, T, NA_HD), F32)] * 3 + [SDS((GLA_HEADS, T, GLA_DK), F32)] * 2
        + [SDS((GLA_HEADS, T, GLA_DV), F32)],
        compiler_params=_params(1),
    )(z, z, z)


def _assemble_dz(dgates, na_dqkv, gla_dqk, gla_dv, dgr, dcq, dckv, dglow, dk_raw, lay, zp, name):
    T = dgr.shape[0]
    tm = _tile(T, 256)

    def body(dgt_ref, nq_ref, nk_ref, nv_ref, gq_ref, gk_ref, gv_ref, dgr_ref, dcq_ref, dckv_ref, dglow_ref, dkr_ref, dz_ref):
        def put(seg, off, val):
            a = lay[seg][0] + off
            dz_ref[:, a:a + val.shape[1]] = val.astype(BF16)

        put('gates', 0, dgt_ref[...])
        for seg, ref in (('na_q', nq_ref), ('na_k', nk_ref), ('na_v', nv_ref)):
            for h in range(NA_HEADS):
                put(seg, h * NA_HD, ref[h])
        for seg, ref in (('gq', gq_ref), ('gk', gk_ref)):
            for h in range(GLA_HEADS):
                put(seg, h * GLA_DK, ref[h])
        for h in range(GLA_HEADS):
            put('gv', h * GLA_DV, gv_ref[h])
        put('gr', 0, dgr_ref[...])
        put('c_q', 0, dcq_ref[...])
        put('c_kv', 0, dckv_ref[...])
        put('glow', 0, dglow_ref[...])
        kr = dkr_ref[0][:, MLA_NOPE:]
        for h in range(1, MLA_HEADS):
            kr = kr + dkr_ref[h][:, MLA_NOPE:]
        put('k_rope', 0, jnp.concatenate([kr, jnp.zeros((tm, LANE - MLA_ROPE), F32)], axis=1))

    tok = lambda a: pl.BlockSpec((tm, a.shape[1]), lambda i: (i, 0))
    hm = lambda a: pl.BlockSpec((a.shape[0], tm, a.shape[2]), lambda i: (0, i, 0))
    args = (dgates, *na_dqkv, *gla_dqk, gla_dv, dgr, dcq, dckv, dglow, dk_raw)
    return pl.pallas_call(
        body, name=name, grid=(T // tm,),
        in_specs=[tok(dgates)] + [hm(a) for a in (*na_dqkv, *gla_dqk, gla_dv)] + [tok(dgr), tok(dcq), tok(dckv), tok(dglow),
                                                                                 hm(dk_raw)],
        out_specs=pl.BlockSpec((tm, zp), lambda i: (i, 0)), out_shape=SDS((T, zp), BF16),
        compiler_params=_params(1),
    )(*args)


def _loss_head(y, target, name):
    T, D = y.shape
    tm = _tile(T, 1024)

    def body(y_ref, t_ref, dy_ref, l_ref):
        @pl.when(pl.program_id(0) == 0)
        def _():
            l_ref[...] = jnp.zeros_like(l_ref)

        e = y_ref[...] - t_ref[...]
        dy_ref[...] = e * (1.0 / D)
        l_ref[...] += 0.5 * jnp.sum(jnp.mean(e * e, axis=-1, keepdims=True), axis=0, keepdims=True)

    tok = pl.BlockSpec((tm, D), lambda i: (i, 0))
    return pl.pallas_call(
        body, name=name, grid=(T // tm,), in_specs=[tok, tok],
        out_specs=[tok, pl.BlockSpec((1, 1), lambda i: (0, 0))],
        out_shape=[SDS((T, D), F32), SDS((1, 1), F32)],
        compiler_params=_params(1),
    )(y, target)


def _exchange_copies(src, dst, send_sems, recv_sems, local_sems, broadcast, with_recvs=True):
    n = len(src)
    x, y, c = lax.axis_index("x"), lax.axis_index("y"), lax.axis_index("c")
    me = 4 * x + 2 * y + c
    local = [pltpu.make_async_copy(src[a] if broadcast else src[a].at[me], dst[a].at[me], local_sems.at[a])
             for a in range(n)]
    sends, recvs = [], []
    for d in range(1, N_DEV):
        px = 1 - x if d & 4 else x
        py = 1 - y if d & 2 else y
        pc = 1 - c if d & 1 else c
        peer = 4 * px + 2 * py + pc
        for a in range(n):
            for out, slot in ((sends, me), (recvs, peer)) if with_recvs else ((sends, me),):
                out.append(pltpu.make_async_remote_copy(
                    src_ref=src[a] if broadcast else src[a].at[peer], dst_ref=dst[a].at[slot],
                    send_sem=send_sems.at[a, d - 1], recv_sem=recv_sems.at[a, d - 1],
                    device_id=(px, py, pc), device_id_type=pl.DeviceIdType.MESH))
    return local, sends, recvs


def _gather_copies(src, dst, send_sems, recv_sems, local_sems):
    x, y, c = lax.axis_index("x"), lax.axis_index("y"), lax.axis_index("c")
    chips = [(1 - x, y), (x, 1 - y), (1 - x, 1 - y)]
    lin = lambda px, py, pc: 4 * px + 2 * py + pc

    def copy(a, k, incoming=False):
        if k == 0:
            to, out_src, out_slot, in_slot = (x, y, 1 - c), src[a], lin(x, y, c), lin(x, y, 1 - c)
        elif k <= 3:
            to, out_src, out_slot, in_slot = (*chips[k - 1], c), src[a], lin(x, y, c), lin(*chips[k - 1], c)
        else:
            slot = lin(*chips[k - 4], c)
            to, out_src, out_slot, in_slot = (x, y, 1 - c), dst[a].at[slot], slot, lin(*chips[k - 4], 1 - c)
        return pltpu.make_async_remote_copy(
            src_ref=out_src, dst_ref=dst[a].at[in_slot if incoming else out_slot], send_sem=send_sems.at[a, k],
            recv_sem=recv_sems.at[a, k], device_id=to, device_id_type=pl.DeviceIdType.MESH)

    local = [pltpu.make_async_copy(src[a], dst[a].at[lin(x, y, c)], local_sems.at[a]) for a in range(len(src))]
    return copy, local


def _exchange_start(src, dst, send_sems, recv_sems, local_sems, broadcast):
    if broadcast:
        copy, local = _gather_copies(src, dst, send_sems, recv_sems, local_sems)
        for cp in local:
            cp.start()
        for k in range(4):
            for a in range(len(src)):
                copy(a, k).start()
        return
    local, sends, _ = _exchange_copies(src, dst, send_sems, recv_sems, local_sems, broadcast, with_recvs=False)
    for cp in local + sends:
        cp.start()


def _exchange_wait(src, dst, send_sems, recv_sems, local_sems, broadcast):
    if broadcast:
        copy, local = _gather_copies(src, dst, send_sems, recv_sems, local_sems)
        n = len(src)
        for k in range(1, 4):
            for a in range(n):
                copy(a, k, True).wait_recv()
                copy(a, k + 3).start()
        for k in (0, 4, 5, 6):
            for a in range(n):
                copy(a, k, True).wait_recv()
        for k in range(N_DEV - 1):
            for a in range(n):
                copy(a, k).wait_send()
        for cp in local:
            cp.wait()
        return
    local, sends, recvs = _exchange_copies(src, dst, send_sems, recv_sems, local_sems, broadcast)
    for cp in recvs:
        cp.wait_recv()
    for cp in sends:
        cp.wait_send()
    for cp in local:
        cp.wait()


def _exchange_shapes(srcs, broadcast):
    n = len(srcs)
    out_shape = [SDS((N_DEV,) + (tuple(s.shape) if broadcast else tuple(s.shape[1:])), s.dtype) for s in srcs]
    sems = [pltpu.SemaphoreType.DMA((n, N_DEV - 1)), pltpu.SemaphoreType.DMA((n, N_DEV - 1)),
            pltpu.SemaphoreType.DMA((n,))]
    return out_shape, sems


def _exchange(srcs, broadcast, name):
    n = len(srcs)
    out_shape, sems = _exchange_shapes(srcs, broadcast)

    def body(*refs):
        mode = (refs[:n], refs[n:2 * n]) + tuple(refs[2 * n:]) + (broadcast,)
        _exchange_start(*mode)
        _exchange_wait(*mode)

    hbm = pl.BlockSpec(memory_space=pltpu.HBM)
    return pl.pallas_call(body, name=name, in_specs=[hbm] * n, out_specs=[hbm] * n, out_shape=out_shape,
                          scratch_shapes=sems)(*srcs)


def _call(body, *, name, grid, in_specs, out_specs, out_shape, args, scratch_shapes=(), exch=None):
    params = _params(len(grid))
    if exch is None:
        return pl.pallas_call(body, name=name, grid=grid, in_specs=in_specs, out_specs=out_specs, out_shape=out_shape,
                              scratch_shapes=list(scratch_shapes), compiler_params=params)(*args)
    srcs, broadcast = exch
    n, n_in, n_out, n_scr = len(srcs), len(args), len(out_shape), len(scratch_shapes)
    x_shape, sems = _exchange_shapes(srcs, broadcast)

    def carrier(*refs):
        ins, x_src = refs[:n_in], refs[n_in:n_in + n]
        outs, x_dst = refs[n_in + n:n_in + n + n_out], refs[n_in + n + n_out:n_in + 2 * n + n_out]
        scr = refs[n_in + 2 * n + n_out:n_in + 2 * n + n_out + n_scr]
        mode = (x_src, x_dst) + tuple(refs[n_in + 2 * n + n_out + n_scr:]) + (broadcast,)
        ids = [pl.program_id(a) for a in range(len(grid))]
        first = functools.reduce(jnp.logical_and, [i == 0 for i in ids])
        last = functools.reduce(jnp.logical_and, [i == g - 1 for i, g in zip(ids, grid)])
        pl.when(first)(lambda: _exchange_start(*mode))
        body(*ins, *outs, *scr)
        pl.when(last)(lambda: _exchange_wait(*mode))

    hbm = pl.BlockSpec(memory_space=pltpu.HBM)
    res = pl.pallas_call(carrier, name=name, grid=grid, in_specs=list(in_specs) + [hbm] * n,
                         out_specs=list(out_specs) + [hbm] * n, out_shape=list(out_shape) + x_shape,
                         scratch_shapes=list(scratch_shapes) + sems, compiler_params=params)(*args, *srcs)
    return res[:n_out], res[n_out:]


def _adam_math(w, g, m, v):
    m = ADAM_B1 * m + (1.0 - ADAM_B1) * g
    v = ADAM_B2 * v + (1.0 - ADAM_B2) * (g * g)
    m_hat = m / (1.0 - ADAM_B1 ** ADAM_STEP)
    v_hat = v / (1.0 - ADAM_B2 ** ADAM_STEP)
    delta = -ADAM_LR * (m_hat / (jnp.sqrt(v_hat) + ADAM_EPS) + ADAM_WD * w)
    return delta, m, v


def _adamw_sharded(landed, k, w, m, v, name):
    L, r, c = w.shape
    lanes = -(-c // LANE) * LANE
    rb = _tile(r, max(16, (1 << 21) // (N_DEV * lanes * 2)), 16)

    def body(*refs):
        l_refs = refs[:L]
        w_ref, m_ref, v_ref, g_ref, d_ref, nm_ref, nv_ref = refs[L:]
        for j in range(L):
            @pl.when(pl.program_id(0) == j)
            def _(j=j):
                g = l_refs[j][0, 0].astype(F32)
                for s in range(1, N_DEV):
                    g = g + l_refs[j][s, 0].astype(F32)
                delta, nm, nv = _adam_math(w_ref[0], g, m_ref[0], v_ref[0])
                g_ref[0] = g
                d_ref[0] = delta
                nm_ref[0] = nm
                nv_ref[0] = nv

    blk = pl.BlockSpec((1, rb, c), lambda l, i: (l, i, 0))
    land = [pl.BlockSpec((N_DEV, 1, rb, c), lambda l, i, j=j: (0, k, jnp.where(l == j, i, 0), 0)) for j in range(L)]
    return pl.pallas_call(
        body, name=name, grid=(L, r // rb),
        in_specs=land + [blk, blk, blk],
        out_specs=[blk] * 4, out_shape=[SDS((L, r, c), F32)] * 4,
        compiler_params=_params(2),
    )(*landed, w, m, v)


def _adamw_replicated(landed, w, m, v, name):
    R = w.shape[0]

    def body(l_ref, w_ref, m_ref, v_ref, g_ref, d_ref, nm_ref, nv_ref):
        g = l_ref[0]
        for s in range(1, N_DEV):
            g = g + l_ref[s]
        delta, nm, nv = _adam_math(w_ref[...], g, m_ref[...], v_ref[...])
        g_ref[...] = g
        d_ref[...] = delta
        nm_ref[...] = nm
        nv_ref[...] = nv

    blk = pl.BlockSpec((R, LANE), lambda i: (0, 0))
    return pl.pallas_call(
        body, name=name, grid=(1,),
        in_specs=[pl.BlockSpec((N_DEV, R, LANE), lambda i: (0, 0, 0)), blk, blk, blk],
        out_specs=[blk] * 4, out_shape=[SDS((R, LANE), F32)] * 4,
        compiler_params=_params(1),
    )(landed, w, m, v)


def _heads(x, h):
    T = x.shape[0]
    return jnp.transpose(x.reshape(T, h, -1), (1, 0, 2))


def _unheads(x):
    h, T, d = x.shape
    return jnp.transpose(x, (1, 0, 2)).reshape(T, h * d)


def _pack(parts):
    flat = jnp.concatenate([p.reshape(-1) for p in parts])
    rows = -(-flat.shape[0] // (8 * LANE)) * 8
    return jnp.pad(flat, (0, rows * LANE - flat.shape[0])).reshape(rows, LANE)


def _unpack(packed, shapes):
    flat = packed.reshape(-1)
    out, off = [], 0
    for s in shapes:
        size = int(np.prod(s))
        out.append(flat[off:off + size].reshape(s))
        off += size
    return out


def _layer_fwd(x0, w, l, lay, tabs, carry):
    T, D = x0.shape
    ctab, stab, na_bias = tabs
    col = lambda name: lay[name][0] // lay[name][1]
    sv = {'x0': x0}
    x1, sv['a1'], sv['b1'] = _carried(_ffn_fwd, carry, 'ffn1', x0, w['ffn1_norm'], w['ffn1_w1'], w['ffn1_w3'],
                                      w['ffn1_w2'], f"ffn1_fwd_{l}")
    z, sv['hb'] = _carried(_norm_matmul, carry, 'mix_in', x1, 0, w['mix_norm'], w['w_in_z'], f"mix_in_{l}")
    seg = lambda name: z[:, lay[name][0]:lay[name][0] + lay[name][3]]
    sv['na_q'], sv['na_k'], sv['na_v'], sv['gq'], sv['gk'], sv['gv'] = _split_z(z, lay, f"split_z_{l}")
    y_na = _carried(_na_fwd, carry, 'na', sv['na_q'], sv['na_k'], sv['na_v'], na_bias, l, w['na_q_norm'], w['na_k_norm'],
                    f"na_fwd_{l}")[0]
    sv['gg'] = _gla_gate_fwd(z, col('glow'), w['up_f'], w['up_b'], w['gla_gf_bias'], w['gla_gb_bias'], f"gla_gate_{l}")
    o_gla = _gla_fwd(sv['gq'], sv['gk'], sv['gv'], sv['gg'], f"gla_fwd_{l}")
    q_raw, sv['hq'] = _norm_matmul(z, col('c_q'), w['mla_cq_norm'], w['mla_w_uq'], f"mla_uq_{l}")
    kv_raw, sv['hkv'] = _norm_matmul(z, col('c_kv'), w['mla_ckv_norm'], w['mla_w_ukv'], f"mla_ukv_{l}")
    kv_h = _heads(kv_raw, MLA_HEADS)
    k_rope = jnp.broadcast_to(seg('k_rope')[None], (MLA_HEADS, T, MLA_ROPE))
    sv['mq_raw'] = _heads(q_raw, MLA_HEADS)
    sv['mk_raw'] = jnp.concatenate([kv_h[..., :MLA_NOPE], k_rope], axis=-1)
    sv['mq'] = _mla_prep_fwd(sv['mq_raw'], w['mla_q_norm'], ctab, stab, f"mla_qprep_{l}")
    sv['mk'] = _mla_prep_fwd(sv['mk_raw'], w['mla_k_norm'], ctab, stab, f"mla_kprep_{l}")
    sv['mv'] = kv_h[..., MLA_NOPE:].astype(BF16)
    y_mla, sv['lse'] = _carried(_mla_attn_fwd, carry, 'mla', sv['mq'], sv['mk'], sv['mv'], f"mla_attn_{l}")
    w_br = (w['w_br_na'], w['w_br_gla'], w['w_br_mla'])
    x2 = _mix_out_fwd(x1, y_na, o_gla, y_mla, z, col('gr'), w['gla_out_norm'], w_br, w['w_out'], f"mix_out_{l}")
    sv.update(x1=x1, z=z, y_na=y_na, o_gla=o_gla, y_mla=y_mla, x2=x2)
    x3, sv['a2'], sv['b2'] = _carried(_ffn_fwd, carry, 'ffn2', x2, w['ffn2_norm'], w['ffn2_w1'], w['ffn2_w3'],
                                      w['ffn2_w2'], f"ffn2_fwd_{l}")
    return x3, sv


def _ffn_grads(grads, x, dy, a, b, w, pre, l, carry):
    dx, da, db, u, hb, dyb, dg = _carried(_ffn_bwd, carry, pre, x, w[pre + '_norm'], dy, a, b, w[pre + '_w1'],
                                          w[pre + '_w3'], w[pre + '_w2'], f"{pre}_bwd_{l}")
    grads.update({pre + '_norm': dg,
                  pre + '_w1': _matmul_tn(hb, da, f"{pre}_dw1_{l}"),
                  pre + '_w3': _matmul_tn(hb, db, f"{pre}_dw3_{l}"),
                  pre + '_w2': _matmul_tn(u, dyb, f"{pre}_dw2_{l}")})
    return dx


def _layer_bwd(grads, dx3, sv, w, l, lay, tabs, carry):
    T, D = dx3.shape
    ctab, stab, na_bias, onehot = tabs
    col = lambda name: lay[name][0] // lay[name][1]
    z = sv['z']
    dx2 = _ffn_grads(grads, sv['x2'], dx3, sv['a2'], sv['b2'], w, 'ffn2', l, carry)
    w_br = (w['w_br_na'], w['w_br_gla'], w['w_br_mla'])
    (dy_na, do_gla, dy_mla, dgr, dgates, dw0, dw1, dw2, dwo, dgn) = _mix_out_bwd(
        dx2, sv['y_na'], sv['o_gla'], sv['y_mla'], z, col('gr'), w['gla_out_norm'], w_br, w['w_out'], f"mix_out_bwd_{l}")
    grads.update(w_br_na=dw0, w_br_gla=dw1, w_br_mla=dw2, w_out=dwo, gla_out_norm=dgn)
    dq, dk, dv, dbias, dgq, dgk = _na_bwd(sv['na_q'], sv['na_k'], sv['na_v'], dy_na, na_bias, l,
                                          w['na_q_norm'], w['na_k_norm'], f"na_bwd_{l}")
    grads.update(na_q_norm=dgq, na_k_norm=dgk, na_rpb=_na_bias_tables_bwd(dbias, onehot))
    gdq, gdk, gdv, gdg = _gla_bwd(sv['gq'], sv['gk'], sv['gv'], sv['gg'], do_gla, f"gla_bwd_{l}")
    dglow, dupf, dupb, dbf, dbb = _gla_gate_bwd(z, col('glow'), w['up_f'], w['up_b'], w['gla_gf_bias'], w['gla_gb_bias'],
                                                gdg, f"gla_gate_bwd_{l}")
    grads.update(gla_gf_up=dupf[:GLA_RANK], gla_gb_up=dupb[GLA_RANK:2 * GLA_RANK], gla_gf_bias=dbf, gla_gb_bias=dbb)
    mdq, mdk, mdv = _carried(_mla_attn_bwd, carry, 'mla', sv['mq'], sv['mk'], sv['mv'], sv['y_mla'], sv['lse'], dy_mla,
                             f"mla_attn_bwd_{l}")
    dq_raw, dgqn = _mla_prep_bwd(sv['mq_raw'], w['mla_q_norm'], ctab, stab, mdq, f"mla_qprep_bwd_{l}")
    dk_raw, dgkn = _mla_prep_bwd(sv['mk_raw'], w['mla_k_norm'], ctab, stab, mdk, f"mla_kprep_bwd_{l}")
    dq_raw = _unheads(dq_raw).astype(BF16)
    dkv_raw = _unheads(jnp.concatenate([dk_raw[..., :MLA_NOPE], mdv], axis=-1)).astype(BF16)
    dcq, dgcq = _norm_matmul_bwd(dq_raw, w['mla_w_uq'], z, col('c_q'), w['mla_cq_norm'], None, f"mla_uq_bwd_{l}")
    dckv, dgckv = _norm_matmul_bwd(dkv_raw, w['mla_w_ukv'], z, col('c_kv'), w['mla_ckv_norm'], None, f"mla_ukv_bwd_{l}")
    grads.update(mla_q_norm=dgqn, mla_k_norm=dgkn, mla_cq_norm=dgcq, mla_ckv_norm=dgckv,
                 mla_w_uq=_matmul_tn(sv['hq'], dq_raw, f"mla_duq_{l}", F32),
                 mla_w_ukv=_matmul_tn(sv['hkv'], dkv_raw, f"mla_dukv_{l}", F32))
    dz = _assemble_dz(dgates, (dq, dk, dv), (gdq, gdk), gdv, dgr, dcq, dckv, dglow, dk_raw, lay, z.shape[1], f"dz_{l}")
    dx1, dgmix = _carried(_norm_matmul_bwd, carry, 'mix_in', dz, w['w_in_z'], sv['x1'], 0, w['mix_norm'], dx2,
                          f"mix_in_bwd_{l}")
    grads.update(mix_norm=dgmix, w_in=_matmul_tn(sv['hb'], dz, f"mix_dw_in_{l}"))
    return _ffn_grads(grads, sv['x0'], dx1, sv['a1'], sv['b1'], w, 'ffn1', l, carry)


EXCHANGE_PARTS = {
    'F1a': [['ffn1_w1', 'ffn1_w3']],
    'F1b': [['ffn1_w2']],
    'F2': [['ffn2_w1', 'ffn2_w3'], ['ffn2_w2']],
    'C': [['w_in']],
    'S': [['w_br_na', 'w_br_gla', 'w_br_mla'], ['w_out'], ['mla_w_uq'], ['mla_w_ukv'], ['gla_gf_up', 'gla_gb_up']],
}
FWD_PLAN = {'ffn1': [(0, 'C')], 'mix_in': [(0, 'S')], 'mla': [(0, 'F2')], 'na': [(1, 'F1b')], 'ffn2': [(1, 'F1a')]}
BWD_PLAN = {'mla': [(0, 'F2')], 'mix_in': [(0, 'S')], 'ffn1': [(0, 'C')], 'ffn2': [(1, 'F1a'), (1, 'F1b')]}
EDGE_PARTS = ['F1a', 'F1b']


def _carried(fn, carry, key, *args):
    if key in carry:
        make_srcs, broadcast, deliver = carry[key]
        outs, landed = fn(*args, exch=(make_srcs(), broadcast))
        deliver(landed)
        return outs
    return fn(*args)


def kernel(x, ffn1_norm, ffn1_w1, ffn1_w3, ffn1_w2, mix_norm, w_in, na_q_norm, na_k_norm, na_rpb, gla_gf_up, gla_gf_bias, gla_gb_up, gla_gb_bias, gla_out_norm, mla_cq_norm, mla_ckv_norm, mla_w_uq, mla_w_ukv, mla_q_norm, mla_k_norm, w_br_na, w_br_gla, w_br_mla, w_out, ffn2_norm, ffn2_w1, ffn2_w3, ffn2_w2, loss_target, m_ffn1_norm, m_ffn1_w1, m_ffn1_w3, m_ffn1_w2, m_mix_norm, m_w_in, m_na_q_norm, m_na_k_norm, m_na_rpb, m_gla_gf_up, m_gla_gf_bias, m_gla_gb_up, m_gla_gb_bias, m_gla_out_norm, m_mla_cq_norm, m_mla_ckv_norm, m_mla_w_uq, m_mla_w_ukv, m_mla_q_norm, m_mla_k_norm, m_w_br_na, m_w_br_gla, m_w_br_mla, m_w_out, m_ffn2_norm, m_ffn2_w1, m_ffn2_w3, m_ffn2_w2, v_ffn1_norm, v_ffn1_w1, v_ffn1_w3, v_ffn1_w2, v_mix_norm, v_w_in, v_na_q_norm, v_na_k_norm, v_na_rpb, v_gla_gf_up, v_gla_gf_bias, v_gla_gb_up, v_gla_gb_bias, v_gla_out_norm, v_mla_cq_norm, v_mla_ckv_norm, v_mla_w_uq, v_mla_w_ukv, v_mla_q_norm, v_mla_k_norm, v_w_br_na, v_w_br_gla, v_w_br_mla, v_w_out, v_ffn2_norm, v_ffn2_w1, v_ffn2_w3, v_ffn2_w2):
    given = dict(locals())
    P = {n: given[n] for n in WEIGHTS}
    M = {n: given['m_' + n] for n in WEIGHTS}
    V = {n: given['v_' + n] for n in WEIGHTS}
    xs = x[0]
    T, D = xs.shape
    L = ffn1_norm.shape[0]
    lay, zp, d_in = _z_layout(D)

    def arrays_of(items):
        return [(l, names) for l, p in items for names in EXCHANGE_PARTS[p]]

    ws = [{n: P[n][l][None, :] for n in REPLICATED if n != 'na_rpb'} for l in range(L)]

    def gather_srcs(items):
        return [jnp.stack([P[n][l].astype(BF16) for n in names]) for l, names in arrays_of(items)]

    def take_gathered(items, res):
        for (l, names), g in zip(arrays_of(items), res):
            for k, n in enumerate(names):
                blk = g[:, k]
                if n == 'w_in':
                    ws[l]['w_in_z'] = _z_cols_from_blocks(blk, lay)
                    continue
                if n in COL_SHARDED:
                    full = jnp.concatenate([blk[d] for d in range(N_DEV)], axis=1)
                else:
                    full = blk.reshape(N_DEV * blk.shape[1], blk.shape[2])
                if n == 'gla_gf_up':
                    ws[l]['up_f'] = jnp.zeros((LANE, GLA_QK_W), BF16).at[:GLA_RANK].set(full)
                elif n == 'gla_gb_up':
                    ws[l]['up_b'] = jnp.zeros((LANE, GLA_QK_W), BF16).at[GLA_RANK:2 * GLA_RANK].set(full)
                else:
                    ws[l][n] = full

    def plan(table, l, make_srcs, broadcast, deliver):
        carry = {}
        for key, items in table.items():
            items = [(l + off, p) for off, p in items if l + off < L]
            if items:
                carry[key] = (functools.partial(make_srcs, items), broadcast, functools.partial(deliver, items))
        return carry

    ctab, stab = _rope_tables(T)
    na_bias, onehot = _na_bias_tables(na_rpb)

    edge = [(0, p) for p in EDGE_PARTS]
    take_gathered(edge, _exchange(gather_srcs(edge), True, "gather_weights_edge"))
    saved = []
    h = xs
    for l in range(L):
        h, sv = _layer_fwd(h, ws[l], l, lay, (ctab, stab, na_bias), plan(FWD_PLAN, l, gather_srcs, True, take_gathered))
        saved.append(sv)
    dh, loss_local = _loss_head(h, loss_target[0], "loss_head")
    loss = lax.psum(loss_local[0, 0], ("x", "y", "c"))

    layer_grads = [dict() for _ in range(L)]

    def scatter_srcs(items):
        def blocks(l, n):
            g = layer_grads[l][n]
            if n == 'w_in':
                return _blocks_from_z_cols(g, lay, d_in).astype(BF16)
            if n in COL_SHARDED:
                c = g.shape[1] // N_DEV
                return jnp.stack([g[:, d * c:(d + 1) * c] for d in range(N_DEV)]).astype(BF16)
            return g.reshape(N_DEV, -1, g.shape[1]).astype(BF16)
        return [jnp.stack([blocks(l, n) for n in names], axis=1) for l, names in arrays_of(items)]

    landed = [dict() for _ in range(L)]

    def take_landed(items, res):
        for (l, names), r in zip(arrays_of(items), res):
            for k, n in enumerate(names):
                landed[l][n] = (r, k)

    for l in reversed(range(L)):
        dh = _layer_bwd(layer_grads[l], dh, saved[l], ws[l], l, lay, (ctab, stab, na_bias, onehot),
                        plan(BWD_PLAN, l, scatter_srcs, False, take_landed))
    grad_x = dh[None]
    take_landed(edge, _exchange(scatter_srcs(edge), False, "scatter_grads_edge"))

    out = {}
    for n in SHARDED:
        out[n] = _adamw_sharded([landed[l][n][0] for l in range(L)], landed[0][n][1], P[n], M[n], V[n], f"adamw_{n}")

    rep = [n for n in REPLICATED]
    rep_shapes = [tuple(P[n].shape) for n in rep]
    rep_partial = _pack([jnp.stack([layer_grads[l][n].reshape(P[n].shape[1:]) for l in range(L)]) for n in rep])
    (rep_landed,) = _exchange([rep_partial], True, "gather_small_grads")
    packed = _adamw_replicated(rep_landed, _pack([P[n] for n in rep]), _pack([M[n] for n in rep]),
                               _pack([V[n] for n in rep]), "adamw_replicated")
    for kind, pk in enumerate(packed):
        for n, arr in zip(rep, _unpack(pk, rep_shapes)):
            out.setdefault(n, [None] * 4)[kind] = arr

    res = [loss, grad_x]
    for kind in range(4):
        res += [out[n][kind] for n in WEIGHTS]
    return tuple(res)
```

```python
import functools

import numpy as np
import jax
import jax.numpy as jnp
from jax import lax
from jax.experimental import pallas as pl
from jax.experimental.pallas import tpu as pltpu

F32 = jnp.float32
BF16 = jnp.bfloat16
SDS = jax.ShapeDtypeStruct
HI = lax.Precision.HIGHEST

N_DEV = 8
DEPTH = 4
EPS = 1e-6
LANE = 128
VMEM_LIMIT = 56 * 1024 * 1024

GRID_W = 64
NA_HEADS, NA_HD, NA_WIN_R, NA_WIN_C = 8, 64, 8, 16
GLA_HEADS, GLA_DK, GLA_DV, GLA_RANK, GLA_TAU, GLA_CHUNK = 4, 64, 128, 16, 16.0, 64
MLA_HEADS, MLA_QR, MLA_KVR, MLA_NOPE, MLA_ROPE, MLA_V = 4, 256, 256, 128, 64, 128
MLA_QK = MLA_NOPE + MLA_ROPE
ROPE_THETA = 10000.0
NA_W = NA_HEADS * NA_HD
GLA_QK_W = GLA_HEADS * GLA_DK
GLA_V_W = GLA_HEADS * GLA_DV
MLA_V_W = MLA_HEADS * MLA_V

ADAM_LR, ADAM_B1, ADAM_B2, ADAM_EPS, ADAM_WD, ADAM_STEP = 0.001, 0.9, 0.999, 1e-08, 0.01, 10

WEIGHTS = ['ffn1_norm', 'ffn1_w1', 'ffn1_w3', 'ffn1_w2', 'mix_norm', 'w_in', 'na_q_norm', 'na_k_norm', 'na_rpb',
           'gla_gf_up', 'gla_gf_bias', 'gla_gb_up', 'gla_gb_bias', 'gla_out_norm', 'mla_cq_norm', 'mla_ckv_norm',
           'mla_w_uq', 'mla_w_ukv', 'mla_q_norm', 'mla_k_norm', 'w_br_na', 'w_br_gla', 'w_br_mla', 'w_out',
           'ffn2_norm', 'ffn2_w1', 'ffn2_w3', 'ffn2_w2']
COL_SHARDED = ['ffn1_w1', 'ffn1_w3', 'w_in', 'gla_gf_up', 'gla_gb_up', 'mla_w_uq', 'mla_w_ukv', 'w_br_na', 'w_br_gla',
               'w_br_mla', 'ffn2_w1', 'ffn2_w3']
ROW_SHARDED = ['ffn1_w2', 'w_out', 'ffn2_w2']
SHARDED = [n for n in WEIGHTS if n in COL_SHARDED or n in ROW_SHARDED]
REPLICATED = [n for n in WEIGHTS if n not in SHARDED]

NT = (((1,), (1,)), ((), ()))
TN = (((0,), (0,)), ((), ()))


def _tile(n, pref, mult=8):
    best = None
    for t in range(mult, min(n, pref) + 1, mult):
        if n % t == 0:
            best = t
    return best if best is not None else n


def _params(n_axes):
    return pltpu.CompilerParams(dimension_semantics=("arbitrary",) * n_axes, vmem_limit_bytes=VMEM_LIMIT)


def _dot(a, b, dims=None, precision=None):
    if dims is None:
        return jnp.dot(a, b, preferred_element_type=F32, precision=precision)
    return lax.dot_general(a, b, dims, preferred_element_type=F32, precision=precision)


def _sigmoid(x):
    return 1.0 / (1.0 + jnp.exp(-x))


def _rstd(x):
    return lax.rsqrt(jnp.mean(x * x, axis=-1, keepdims=True) + EPS)


def _rms_bwd(dy, x, g, r):
    xh = x * r
    dyg = dy * g
    dx = r * (dyg - xh * jnp.mean(dyg * xh, axis=-1, keepdims=True))
    return dx, jnp.sum(dy * xh, axis=0, keepdims=True)


def _z_layout(d_model):
    own = {}
    off = 0
    for name, w in [('na_q', NA_W), ('na_k', NA_W), ('na_v', NA_W), ('gq', GLA_QK_W), ('gk', GLA_QK_W),
                    ('gv', GLA_V_W), ('gr', GLA_V_W), ('gfl', GLA_RANK), ('gbl', GLA_RANK), ('c_q', MLA_QR),
                    ('c_kv', MLA_KVR), ('k_rope', MLA_ROPE), ('gates', 3 * d_model)]:
        own[name] = (off, w)
        off += w
    order = [('gates', 3 * d_model), ('na_q', NA_W), ('na_k', NA_W), ('na_v', NA_W), ('gv', GLA_V_W), ('gr', GLA_V_W),
             ('gq', GLA_QK_W), ('gk', GLA_QK_W), ('c_q', MLA_QR), ('c_kv', MLA_KVR), ('glow', LANE), ('k_rope', LANE)]
    lay = {}
    z = 0
    for name, w in order:
        assert z % w == 0, (name, z, w)
        if name == 'glow':
            lay[name] = (z, w, own['gfl'][0], 2 * GLA_RANK)
        else:
            lay[name] = (z, w, own[name][0], own[name][1])
        z += w
    return lay, z, off


def _z_cols_from_blocks(blocks, lay):
    cs = blocks.shape[2]
    parts = []
    for zo, zw, oo, ow in lay.values():
        a = oo
        while a < oo + ow:
            b = min(oo + ow, (a // cs + 1) * cs)
            parts.append(blocks[a // cs, :, a % cs:a % cs + (b - a)])
            a = b
        if zw > ow:
            parts.append(jnp.zeros((blocks.shape[1], zw - ow), blocks.dtype))
    return jnp.concatenate(parts, axis=1)


def _blocks_from_z_cols(wz, lay, d_in):
    cs = d_in // N_DEV
    own = sorted(lay.values(), key=lambda t: t[2])
    out = []
    for k in range(N_DEV):
        parts = []
        for zo, zw, oo, ow in own:
            a, b = max(oo, k * cs), min(oo + ow, (k + 1) * cs)
            if a < b:
                parts.append(wz[:, zo + a - oo:zo + b - oo])
        out.append(jnp.concatenate(parts, axis=1))
    return jnp.stack(out)


def _ffn_fwd(x, g, w1, w3, w2, name, exch=None):
    T, D = x.shape
    Fd = w1.shape[1]
    tm, tf = _tile(T, 1024), _tile(Fd, 256, LANE)
    nj = Fd // tf

    def body(x_ref, g_ref, w1_ref, w3_ref, w2_ref, o_ref, a_ref, b_ref, h_scr, acc):
        j = pl.program_id(1)

        @pl.when(j == 0)
        def _():
            xv = x_ref[...]
            h_scr[...] = (xv * _rstd(xv) * g_ref[...]).astype(BF16)
            acc[...] = jnp.zeros_like(acc)

        h = h_scr[...]
        a = _dot(h, w1_ref[...])
        b = _dot(h, w3_ref[...])
        a_ref[...] = a.astype(BF16)
        b_ref[...] = b.astype(BF16)
        u = a * _sigmoid(a) * b
        acc[...] += _dot(u.astype(BF16), w2_ref[...])

        @pl.when(j == nj - 1)
        def _():
            o_ref[...] = x_ref[...] + 0.5 * acc[...]

    return _call(
        body, name=name, grid=(T // tm, nj),
        in_specs=[pl.BlockSpec((tm, D), lambda i, j: (i, 0)), pl.BlockSpec((1, D), lambda i, j: (0, 0)),
                  pl.BlockSpec((D, tf), lambda i, j: (0, j)), pl.BlockSpec((D, tf), lambda i, j: (0, j)),
                  pl.BlockSpec((tf, D), lambda i, j: (j, 0))],
        out_specs=[pl.BlockSpec((tm, D), lambda i, j: (i, 0)), pl.BlockSpec((tm, tf), lambda i, j: (i, j)),
                   pl.BlockSpec((tm, tf), lambda i, j: (i, j))],
        out_shape=[SDS((T, D), F32), SDS((T, Fd), BF16), SDS((T, Fd), BF16)],
        scratch_shapes=[pltpu.VMEM((tm, D), BF16), pltpu.VMEM((tm, D), F32)],
        args=(x, g, w1, w3, w2), exch=exch)


def _ffn_bwd(x, g, dy, a, b, w1, w3, w2, name, exch=None):
    T, D = x.shape
    Fd = w1.shape[1]
    tm, tf = _tile(T, 512), _tile(Fd, 256, LANE)
    nj = Fd // tf

    def body(x_ref, g_ref, dy_ref, a_ref, b_ref, w1_ref, w3_ref, w2_ref,
             dx_ref, da_ref, db_ref, u_ref, hb_ref, dyb_ref, dg_ref, dh_acc):
        i, j = pl.program_id(0), pl.program_id(1)

        @pl.when(j == 0)
        def _():
            xv = x_ref[...]
            hb_ref[...] = (xv * _rstd(xv) * g_ref[...]).astype(BF16)
            dyb_ref[...] = (0.5 * dy_ref[...]).astype(BF16)
            dh_acc[...] = jnp.zeros_like(dh_acc)

        @pl.when((i == 0) & (j == 0))
        def _():
            dg_ref[...] = jnp.zeros_like(dg_ref)

        du = _dot(dyb_ref[...], w2_ref[...], NT)
        av = a_ref[...].astype(F32)
        bv = b_ref[...].astype(F32)
        s = _sigmoid(av)
        sl = av * s
        da = (du * bv * (s * (1.0 + av * (1.0 - s)))).astype(BF16)
        db = (du * sl).astype(BF16)
        da_ref[...] = da
        db_ref[...] = db
        u_ref[...] = (sl * bv).astype(BF16)
        dh_acc[...] += _dot(da, w1_ref[...], NT) + _dot(db, w3_ref[...], NT)

        @pl.when(j == nj - 1)
        def _():
            xv = x_ref[...]
            dxn, dg = _rms_bwd(dh_acc[...], xv, g_ref[...], _rstd(xv))
            dx_ref[...] = dy_ref[...] + dxn
            dg_ref[...] += dg

    row = lambda i, j: (i, 0)
    return _call(
        body, name=name, grid=(T // tm, nj),
        in_specs=[pl.BlockSpec((tm, D), row), pl.BlockSpec((1, D), lambda i, j: (0, 0)), pl.BlockSpec((tm, D), row),
                  pl.BlockSpec((tm, tf), lambda i, j: (i, j)), pl.BlockSpec((tm, tf), lambda i, j: (i, j)),
                  pl.BlockSpec((D, tf), lambda i, j: (0, j)), pl.BlockSpec((D, tf), lambda i, j: (0, j)),
                  pl.BlockSpec((tf, D), lambda i, j: (j, 0))],
        out_specs=[pl.BlockSpec((tm, D), row), pl.BlockSpec((tm, tf), lambda i, j: (i, j)),
                   pl.BlockSpec((tm, tf), lambda i, j: (i, j)), pl.BlockSpec((tm, tf), lambda i, j: (i, j)),
                   pl.BlockSpec((tm, D), row), pl.BlockSpec((tm, D), row), pl.BlockSpec((1, D), lambda i, j: (0, 0))],
        out_shape=[SDS((T, D), F32), SDS((T, Fd), BF16), SDS((T, Fd), BF16), SDS((T, Fd), BF16),
                   SDS((T, D), BF16), SDS((T, D), BF16), SDS((1, D), F32)],
        scratch_shapes=[pltpu.VMEM((tm, D), F32)],
        args=(x, g, dy, a, b, w1, w3, w2), exch=exch)


def _matmul_tn(a, b, name, out_dtype=BF16):
    T, K = a.shape
    N = b.shape[1]
    tk = _tile(K, 1408, LANE)
    tn = _tile(N, 1408, LANE)
    tt = _tile(T, 1024)
    nt = T // tt

    def body(a_ref, b_ref, o_ref, acc):
        t = pl.program_id(2)

        @pl.when(t == 0)
        def _():
            acc[...] = jnp.zeros_like(acc)

        acc[...] += _dot(a_ref[...], b_ref[...], TN)

        @pl.when(t == nt - 1)
        def _():
            o_ref[...] = acc[...].astype(out_dtype)

    return pl.pallas_call(
        body, name=name, grid=(K // tk, N // tn, nt),
        in_specs=[pl.BlockSpec((tt, tk), lambda k, n, t: (t, k)), pl.BlockSpec((tt, tn), lambda k, n, t: (t, n))],
        out_specs=pl.BlockSpec((tk, tn), lambda k, n, t: (k, n)),
        out_shape=SDS((K, N), out_dtype),
        scratch_shapes=[pltpu.VMEM((tk, tn), F32)],
        compiler_params=_params(3),
    )(a, b)


def _norm_matmul(x, xcol, g, w, name, exch=None):
    T = x.shape[0]
    D = g.shape[1]
    N = w.shape[1]
    tm, tn = _tile(T, 1024), _tile(N, 768, LANE)

    def body(x_ref, g_ref, w_ref, z_ref, hb_ref):
        @pl.when(pl.program_id(1) == 0)
        def _():
            xv = x_ref[...]
            hb_ref[...] = (xv * _rstd(xv) * g_ref[...]).astype(BF16)

        z_ref[...] = _dot(hb_ref[...], w_ref[...])

    return _call(
        body, name=name, grid=(T // tm, N // tn),
        in_specs=[pl.BlockSpec((tm, D), lambda i, j: (i, xcol)), pl.BlockSpec((1, D), lambda i, j: (0, 0)),
                  pl.BlockSpec((D, tn), lambda i, j: (0, j))],
        out_specs=[pl.BlockSpec((tm, tn), lambda i, j: (i, j)), pl.BlockSpec((tm, D), lambda i, j: (i, 0))],
        out_shape=[SDS((T, N), F32), SDS((T, D), BF16)],
        args=(x, g, w), exch=exch)


def _norm_matmul_bwd(dz, w, x, xcol, g, resid, name, exch=None):
    T, N = dz.shape
    D = g.shape[1]
    tm, tn = _tile(T, 1024), _tile(N, 768, LANE)
    nj = N // tn
    has_resid = resid is not None

    def body(*refs):
        if has_resid:
            dz_ref, w_ref, x_ref, g_ref, r_ref, dx_ref, dg_ref, acc = refs
        else:
            dz_ref, w_ref, x_ref, g_ref, dx_ref, dg_ref, acc = refs
        i, j = pl.program_id(0), pl.program_id(1)

        @pl.when(j == 0)
        def _():
            acc[...] = jnp.zeros_like(acc)

        @pl.when((i == 0) & (j == 0))
        def _():
            dg_ref[...] = jnp.zeros_like(dg_ref)

        acc[...] += _dot(dz_ref[...], w_ref[...], NT)

        @pl.when(j == nj - 1)
        def _():
            xv = x_ref[...]
            dxn, dg = _rms_bwd(acc[...], xv, g_ref[...], _rstd(xv))
            dx_ref[...] = dxn + r_ref[...] if has_resid else dxn
            dg_ref[...] += dg

    in_specs = [pl.BlockSpec((tm, tn), lambda i, j: (i, j)), pl.BlockSpec((D, tn), lambda i, j: (0, j)),
                pl.BlockSpec((tm, D), lambda i, j: (i, xcol)), pl.BlockSpec((1, D), lambda i, j: (0, 0))]
    args = [dz, w, x, g]
    if has_resid:
        in_specs.append(pl.BlockSpec((tm, D), lambda i, j: (i, 0)))
        args.append(resid)
    return _call(
        body, name=name, grid=(T // tm, nj), in_specs=in_specs,
        out_specs=[pl.BlockSpec((tm, D), lambda i, j: (i, 0)), pl.BlockSpec((1, D), lambda i, j: (0, 0))],
        out_shape=[SDS((T, D), F32), SDS((1, D), F32)],
        scratch_shapes=[pltpu.VMEM((tm, D), F32)],
        args=args, exch=exch)


def _na_bias_tables(rpb):
    L = rpb.shape[0]
    qc = np.arange(GRID_W)[:, None]
    kc = np.arange(GRID_W)[None, :]
    dc = np.clip(kc - qc + NA_WIN_C - 1, 0, 2 * NA_WIN_C - 2)
    c0 = np.clip(qc - NA_WIN_C // 2, 0, GRID_W - NA_WIN_C)
    ok = (kc >= c0) & (kc < c0 + NA_WIN_C)
    onehot = (dc.reshape(-1)[None, :] == np.arange(2 * NA_WIN_C - 1)[:, None]).astype(np.float32)
    t1 = jnp.stack([rpb[:, :, c:c + NA_WIN_R, :] for c in range(NA_WIN_R)], axis=2)
    full = jnp.dot(t1.reshape(-1, 2 * NA_WIN_C - 1), jnp.asarray(onehot), precision=HI)
    full = full.reshape(L, NA_HEADS, NA_WIN_R, NA_WIN_R, GRID_W, GRID_W)
    full = jnp.where(jnp.asarray(ok)[None, None, None, None], full, -1e30)
    return jnp.transpose(full, (0, 1, 2, 4, 3, 5)).reshape(L * NA_HEADS, NA_WIN_R, GRID_W, NA_WIN_R * GRID_W), onehot


def _na_bias_tables_bwd(db, onehot):
    d = db.reshape(NA_HEADS, NA_WIN_R, GRID_W, NA_WIN_R, GRID_W)
    d = jnp.transpose(d, (0, 1, 3, 2, 4)).reshape(-1, GRID_W * GRID_W)
    t1 = jnp.dot(d, jnp.asarray(onehot.T), precision=HI).reshape(NA_HEADS, NA_WIN_R, NA_WIN_R, 2 * NA_WIN_C - 1)
    out = jnp.zeros((NA_HEADS, 2 * NA_WIN_R - 1, 2 * NA_WIN_C - 1), F32)
    for c in range(NA_WIN_R):
        out = out.at[:, c:c + NA_WIN_R, :].add(t1[:, c])
    return out


def _na_windows(bi, rb, rows):
    wsl, cls = [], []
    for i in range(rb):
        r = bi * rb + i
        r0 = jnp.clip(r - NA_WIN_R // 2, 0, rows - NA_WIN_R)
        wsl.append(pl.ds(pl.multiple_of(r0 * GRID_W, GRID_W), NA_WIN_R * GRID_W))
        cls.append(r0 - r + NA_WIN_R - 1)
    return wsl, cls


def _na_fwd(q, k, v, bias, layer, gq, gk, name, exch=None):
    H, T, d = q.shape
    rows = T // GRID_W
    rb = _tile(rows, 8, 1)
    win = NA_WIN_R * GRID_W
    scale = d ** -0.5

    def body(q_ref, k_ref, v_ref, b_ref, gq_ref, gk_ref, o_ref, qn, kn, vb):
        qv, kv = q_ref[0], k_ref[0]
        qn[...] = (qv * _rstd(qv) * gq_ref[...] * scale).astype(BF16)
        kn[...] = (kv * _rstd(kv) * gk_ref[...]).astype(BF16)
        vb[...] = v_ref[0].astype(BF16)

        def block(bi, c):
            wsl, cls = _na_windows(bi, rb, rows)
            qsl = pl.ds(pl.multiple_of(bi * (rb * GRID_W), rb * GRID_W), rb * GRID_W)
            kw = jnp.stack([kn[w, :] for w in wsl])
            vw = jnp.stack([vb[w, :] for w in wsl])
            s = _bdot(qn[qsl, :].reshape(rb, GRID_W, d), kw, 2, 2) + jnp.stack([b_ref[0, c_] for c_ in cls])
            p = jnp.exp(s - jnp.max(s, axis=-1, keepdims=True))
            p = p / jnp.sum(p, axis=-1, keepdims=True)
            o_ref[0, qsl, :] = _bdot(p.astype(BF16), vw, 2, 1).reshape(rb * GRID_W, d)
            return c

        lax.fori_loop(0, rows // rb, block, 0)

    hm = pl.BlockSpec((1, T, d), lambda h: (h, 0, 0))
    gs = pl.BlockSpec((1, d), lambda h: (0, 0))
    return _call(
        body, name=name, grid=(H,),
        in_specs=[hm, hm, hm, pl.BlockSpec((1, NA_WIN_R, GRID_W, win), lambda h: (layer * H + h, 0, 0, 0)), gs, gs],
        out_specs=[hm], out_shape=[SDS((H, T, d), F32)],
        scratch_shapes=[pltpu.VMEM((T, d), BF16)] * 3,
        args=(q, k, v, bias, gq, gk), exch=exch)


def _na_bwd(q, k, v, do, bias, layer, gq, gk, name):
    H, T, d = q.shape
    rows = T // GRID_W
    rb = _tile(rows, 8, 1)
    win = NA_WIN_R * GRID_W
    scale = d ** -0.5

    def body(q_ref, k_ref, v_ref, do_ref, b_ref, gq_ref, gk_ref,
             dq_ref, dk_ref, dv_ref, db_ref, dgq_ref, dgk_ref, qn, kn, vb, dob, dqn, dkn):
        h = pl.program_id(0)
        qv, kv = q_ref[0], k_ref[0]
        rq, rk = _rstd(qv), _rstd(kv)
        qn[...] = (qv * rq * gq_ref[...] * scale).astype(BF16)
        kn[...] = (kv * rk * gk_ref[...]).astype(BF16)
        vb[...] = v_ref[0].astype(BF16)
        dob[...] = do_ref[0].astype(BF16)
        dkn[...] = jnp.zeros_like(dkn)
        dv_ref[...] = jnp.zeros_like(dv_ref)
        db_ref[...] = jnp.zeros_like(db_ref)

        @pl.when(h == 0)
        def _():
            dgq_ref[...] = jnp.zeros_like(dgq_ref)
            dgk_ref[...] = jnp.zeros_like(dgk_ref)

        def block(bi, c):
            wsl, cls = _na_windows(bi, rb, rows)
            qsl = pl.ds(pl.multiple_of(bi * (rb * GRID_W), rb * GRID_W), rb * GRID_W)
            qs = qn[qsl, :].reshape(rb, GRID_W, d)
            dos = dob[qsl, :].reshape(rb, GRID_W, d)
            kw = jnp.stack([kn[w, :] for w in wsl])
            vw = jnp.stack([vb[w, :] for w in wsl])
            s = _bdot(qs, kw, 2, 2) + jnp.stack([b_ref[0, c_] for c_ in cls])
            p = jnp.exp(s - jnp.max(s, axis=-1, keepdims=True))
            p = p / jnp.sum(p, axis=-1, keepdims=True)
            dp = _bdot(dos, vw, 2, 2)
            ds = p * (dp - jnp.sum(dp * p, axis=-1, keepdims=True))
            dsb = ds.astype(BF16)
            dqn[qsl, :] = _bdot(dsb, kw, 2, 1).reshape(rb * GRID_W, d)
            dvw = _bdot(p.astype(BF16), dos, 1, 1)
            dkw = _bdot(dsb, qs, 1, 1)
            for i in range(rb):
                db_ref[0, cls[i]] += ds[i]
                dv_ref[0, wsl[i], :] += dvw[i]
                dkn[wsl[i], :] += dkw[i]
            return c

        lax.fori_loop(0, rows // rb, block, 0)
        dq, dgq = _rms_bwd(dqn[...] * scale, qv, gq_ref[...], rq)
        dk, dgk = _rms_bwd(dkn[...], kv, gk_ref[...], rk)
        dq_ref[0] = dq
        dk_ref[0] = dk
        dgq_ref[...] += dgq
        dgk_ref[...] += dgk

    hm = pl.BlockSpec((1, T, d), lambda h: (h, 0, 0))
    gs = pl.BlockSpec((1, d), lambda h: (0, 0))
    bs = pl.BlockSpec((1, NA_WIN_R, GRID_W, win), lambda h: (h, 0, 0, 0))
    return pl.pallas_call(
        body, name=name, grid=(H,),
        in_specs=[hm, hm, hm, hm, pl.BlockSpec((1, NA_WIN_R, GRID_W, win), lambda h: (layer * H + h, 0, 0, 0)), gs, gs],
        out_specs=[hm, hm, hm, bs, gs, gs],
        out_shape=[SDS((H, T, d), F32)] * 3 + [SDS((H, NA_WIN_R, GRID_W, win), F32), SDS((1, d), F32), SDS((1, d), F32)],
        scratch_shapes=[pltpu.VMEM((T, d), BF16)] * 4 + [pltpu.VMEM((T, d), F32)] * 2,
        compiler_params=_params(1),
    )(q, k, v, do, bias, gq, gk)


def _gla_gate_fwd(z, col, upf, upb, bf, bb, name):
    T = z.shape[0]
    W = upf.shape[1]
    tm = _tile(T, 1024)

    def body(z_ref, upf_ref, upb_ref, bf_ref, bb_ref, g_ref):
        seg = z_ref[...].astype(BF16)
        for rev, (up_ref, b_ref) in enumerate(((upf_ref, bf_ref), (upb_ref, bb_ref))):
            pre = _dot(seg, up_ref[...]) + b_ref[...]
            g = (jnp.minimum(pre, 0.0) - jnp.log(1.0 + jnp.exp(-jnp.abs(pre)))) * (1.0 / GLA_TAU)
            for h in range(GLA_HEADS):
                g_ref[rev, h] = g[:, h * GLA_DK:(h + 1) * GLA_DK]

    full = lambda shape: pl.BlockSpec(shape, lambda i: (0, 0))
    return pl.pallas_call(
        body, name=name, grid=(T // tm,),
        in_specs=[pl.BlockSpec((tm, LANE), lambda i: (i, col)), full((LANE, W)), full((LANE, W)), full((1, W)), full((1, W))],
        out_specs=pl.BlockSpec((2, GLA_HEADS, tm, GLA_DK), lambda i: (0, 0, i, 0)),
        out_shape=SDS((2, GLA_HEADS, T, GLA_DK), F32),
        compiler_params=_params(1),
    )(z, upf, upb, bf, bb)


def _gla_gate_bwd(z, col, upf, upb, bf, bb, dg, name):
    T = z.shape[0]
    W = upf.shape[1]
    tm = _tile(T, 1024)

    def body(z_ref, upf_ref, upb_ref, bf_ref, bb_ref, dg_ref, dseg_ref, dupf_ref, dupb_ref, dbf_ref, dbb_ref):
        @pl.when(pl.program_id(0) == 0)
        def _():
            for r in (dupf_ref, dupb_ref, dbf_ref, dbb_ref):
                r[...] = jnp.zeros_like(r)

        seg = z_ref[...].astype(BF16)
        dseg = jnp.zeros(dseg_ref.shape, F32)
        for rev, (up_ref, b_ref, dup_ref, db_ref) in enumerate(((upf_ref, bf_ref, dupf_ref, dbf_ref),
                                                               (upb_ref, bb_ref, dupb_ref, dbb_ref))):
            pre = _dot(seg, up_ref[...]) + b_ref[...]
            dgv = jnp.concatenate([dg_ref[rev, h] for h in range(GLA_HEADS)], axis=1)
            dpre = dgv * (1.0 / GLA_TAU) * (1.0 - _sigmoid(pre))
            dpb = dpre.astype(BF16)
            dseg = dseg + _dot(dpb, up_ref[...], NT)
            dup_ref[...] += _dot(seg, dpb, TN)
            db_ref[...] += jnp.sum(dpre, axis=0, keepdims=True)
        dseg_ref[...] = dseg

    full = lambda shape: pl.BlockSpec(shape, lambda i: (0, 0))
    return pl.pallas_call(
        body, name=name, grid=(T // tm,),
        in_specs=[pl.BlockSpec((tm, LANE), lambda i: (i, col)), full((LANE, W)), full((LANE, W)), full((1, W)), full((1, W)),
                  pl.BlockSpec((2, GLA_HEADS, tm, GLA_DK), lambda i: (0, 0, i, 0))],
        out_specs=[pl.BlockSpec((tm, LANE), lambda i: (i, 0)), full((LANE, W)), full((LANE, W)), full((1, W)), full((1, W))],
        out_shape=[SDS((T, LANE), F32), SDS((LANE, W), F32), SDS((LANE, W), F32), SDS((1, W), F32), SDS((1, W), F32)],
        compiler_params=_params(1),
    )(z, upf, upb, bf, bb, dg)


def _bdot(a, b, ca, cb, precision=None):
    return lax.dot_general(a, b, (((ca,), (cb,)), ((0,), (0,))), preferred_element_type=F32, precision=precision)


def _gla_chunk_terms(q_ref, k_ref, v_ref, g_ref, rev, tok, n, C):
    dk, dv = q_ref.shape[-1], v_ref.shape[-1]
    q = q_ref[0, tok, :].reshape(n, C, dk) * (dk ** -0.5)
    k = k_ref[0, tok, :].reshape(n, C, dk)
    v = v_ref[0, tok, :].reshape(n, C, dv)
    g = g_ref[rev, 0, tok, :].reshape(n, C, dk)
    ii = lax.broadcasted_iota(jnp.int32, (C, C), 0)
    jj = lax.broadcasted_iota(jnp.int32, (C, C), 1)
    tri = jnp.broadcast_to(((ii <= jj) if rev else (ii >= jj)).astype(F32), (n, C, C))
    amask = ((jj > ii) if rev else (ii >= jj))[None]
    b = _bdot(tri, g, 2, 1, HI)
    bl = jnp.sum(g, axis=1, keepdims=True)
    eb = jnp.exp(b)
    enb = jnp.exp(-b)
    eend = jnp.exp(bl - b)
    return q, k, v, tri, amask, bl, eb, enb, eend


def _gla_segments(T, C):
    n = T // C
    ns = _tile(n, 16, 1)

    def tok(s):
        return pl.ds(pl.multiple_of(s * (ns * C), ns * C), ns * C)

    def chunks(s):
        return pl.ds(pl.multiple_of(s * ns, ns), ns)

    return n, ns, n // ns, tok, chunks


def _gla_fwd(q, k, v, g, name):
    H, T, dk = q.shape
    dv = v.shape[-1]
    C = GLA_CHUNK
    n, ns, nseg, seg_tok, seg_chunks = _gla_segments(T, C)

    def body(q_ref, k_ref, v_ref, g_ref, o_ref, upd_scr, dec_scr, st_scr):
        for rev in (0, 1):
            def intra(s, c):
                tok, ch = seg_tok(s), seg_chunks(s)
                q, k, v, tri, amask, bl, eb, enb, eend = _gla_chunk_terms(q_ref, k_ref, v_ref, g_ref, rev, tok, ns, C)
                vb = v.astype(BF16)
                a = jnp.where(amask, _bdot((q * eb).astype(BF16), (k * enb).astype(BF16), 2, 2), 0.0).astype(BF16)
                o = _bdot(a, vb, 2, 1).reshape(ns * C, dv)
                if rev:
                    o_ref[0, tok, :] += o
                else:
                    o_ref[0, tok, :] = o
                upd_scr[ch] = _bdot(vb, (k * eend).astype(BF16), 1, 1)
                dec_scr[ch] = jnp.exp(bl)
                return c

            lax.fori_loop(0, nseg, intra, 0)

            def step(t, st):
                i = n - 1 - t if rev else t
                st_scr[i] = st
                return st * dec_scr[i] + upd_scr[i]

            lax.fori_loop(0, n, step, jnp.zeros((dv, dk), F32))

            def inter(s, c):
                tok, ch = seg_tok(s), seg_chunks(s)
                q, k, v, tri, amask, bl, eb, enb, eend = _gla_chunk_terms(q_ref, k_ref, v_ref, g_ref, rev, tok, ns, C)
                o_ref[0, tok, :] += _bdot((q * eb).astype(BF16), st_scr[ch].astype(BF16), 2, 2).reshape(ns * C, dv)
                return c

            lax.fori_loop(0, nseg, inter, 0)

    sk = pl.BlockSpec((1, T, dk), lambda h: (h, 0, 0))
    sv = pl.BlockSpec((1, T, dv), lambda h: (h, 0, 0))
    sg = pl.BlockSpec((2, 1, T, dk), lambda h: (0, h, 0, 0))
    return pl.pallas_call(
        body, name=name, grid=(H,), in_specs=[sk, sk, sv, sg], out_specs=sv,
        out_shape=SDS((H, T, dv), F32),
        scratch_shapes=[pltpu.VMEM((n, dv, dk), F32), pltpu.VMEM((n, 1, dk), F32), pltpu.VMEM((n, dv, dk), F32)],
        compiler_params=_params(1),
    )(q, k, v, g)


def _gla_bwd(q, k, v, g, do, name):
    H, T, dk = q.shape
    dv = v.shape[-1]
    C = GLA_CHUNK
    n, ns, nseg, seg_tok, seg_chunks = _gla_segments(T, C)

    def one_scan(rev, q_ref, k_ref, v_ref, g_ref, do_ref, dq_ref, dk_ref, dv_ref, dg_ref, upd_scr, dec_scr, st_scr, gn_scr,
                 rn_scr):
        def first(s, c):
            tok, ch = seg_tok(s), seg_chunks(s)
            q, k, v, tri, amask, bl, eb, enb, eend = _gla_chunk_terms(q_ref, k_ref, v_ref, g_ref, rev, tok, ns, C)
            dob = do_ref[0, tok, :].reshape(ns, C, dv).astype(BF16)
            upd_scr[ch] = _bdot(v.astype(BF16), (k * eend).astype(BF16), 1, 1)
            dec_scr[ch] = jnp.exp(bl)
            gn_scr[ch] = _bdot(dob, (q * eb).astype(BF16), 1, 1)
            return c

        lax.fori_loop(0, nseg, first, 0)

        def fstep(t, st):
            i = n - 1 - t if rev else t
            st_scr[i] = st
            return st * dec_scr[i] + upd_scr[i]

        lax.fori_loop(0, n, fstep, jnp.zeros((dv, dk), F32))

        def bstep(t, r):
            i = t if rev else n - 1 - t
            rn_scr[i] = r
            return gn_scr[i] + r * dec_scr[i]

        lax.fori_loop(0, n, bstep, jnp.zeros((dv, dk), F32))

        def second(s, c):
            tok, ch = seg_tok(s), seg_chunks(s)
            q, k, v, tri, amask, bl, eb, enb, eend = _gla_chunk_terms(q_ref, k_ref, v_ref, g_ref, rev, tok, ns, C)
            qe, ke, kend = q * eb, k * enb, k * eend
            qeb, keb, kendb, vb = qe.astype(BF16), ke.astype(BF16), kend.astype(BF16), v.astype(BF16)
            dob = do_ref[0, tok, :].reshape(ns, C, dv).astype(BF16)
            a = jnp.where(amask, _bdot(qeb, keb, 2, 2), 0.0).astype(BF16)
            st = st_scr[ch]
            rn = rn_scr[ch]
            rnb = rn.astype(BF16)
            da = jnp.where(amask, _bdot(dob, vb, 2, 2), 0.0).astype(BF16)
            dvv = _bdot(a, dob, 1, 1) + _bdot(kendb, rnb, 2, 2)
            dqe = _bdot(da, keb, 2, 1) + _bdot(dob, st.astype(BF16), 2, 1)
            dke = _bdot(da, qeb, 1, 1)
            dkend = _bdot(vb, rnb, 2, 1)
            ddec = jnp.sum(rn * st, axis=1, keepdims=True)
            dbl = ddec * jnp.exp(bl) + jnp.sum(dkend * kend, axis=1, keepdims=True)
            db = dqe * qe - dke * ke - dkend * kend
            dg = _bdot(tri, db, 1, 1, HI) + dbl
            dqv = (dqe * eb * (dk ** -0.5)).reshape(ns * C, dk)
            dkv = (dke * enb + dkend * eend).reshape(ns * C, dk)
            if rev:
                dq_ref[0, tok, :] += dqv
                dk_ref[0, tok, :] += dkv
                dv_ref[0, tok, :] += dvv.reshape(ns * C, dv)
            else:
                dq_ref[0, tok, :] = dqv
                dk_ref[0, tok, :] = dkv
                dv_ref[0, tok, :] = dvv.reshape(ns * C, dv)
            dg_ref[rev, 0, tok, :] = dg.reshape(ns * C, dk)
            return c

        lax.fori_loop(0, nseg, second, 0)

    def body(*refs):
        one_scan(0, *refs)
        one_scan(1, *refs)

    sk = pl.BlockSpec((1, T, dk), lambda h: (h, 0, 0), pipeline_mode=pl.Buffered(1))
    sv = pl.BlockSpec((1, T, dv), lambda h: (h, 0, 0), pipeline_mode=pl.Buffered(1))
    sg = pl.BlockSpec((2, 1, T, dk), lambda h: (0, h, 0, 0), pipeline_mode=pl.Buffered(1))
    st_shape = pltpu.VMEM((n, dv, dk), F32)
    return pl.pallas_call(
        body, name=name, grid=(H,), in_specs=[sk, sk, sv, sg, sv], out_specs=[sk, sk, sv, sg],
        out_shape=[SDS((H, T, dk), F32), SDS((H, T, dk), F32), SDS((H, T, dv), F32), SDS((2, H, T, dk), F32)],
        scratch_shapes=[st_shape, pltpu.VMEM((n, 1, dk), F32), st_shape, st_shape, st_shape],
        compiler_params=_params(1),
    )(q, k, v, g, do)


def _rope_tables(T):
    half = MLA_ROPE // 2
    inv = ROPE_THETA ** (-jnp.arange(half, dtype=F32) / half)
    ang = jnp.arange(T, dtype=F32)[:, None] * inv[None, :]
    cos, sin = jnp.cos(ang), jnp.sin(ang)
    ctab = jnp.concatenate([jnp.ones((T, MLA_NOPE), F32), cos, cos], axis=1)
    stab = jnp.concatenate([jnp.zeros((T, MLA_NOPE), F32), -sin, sin], axis=1)
    return ctab, stab


def _swap_rope_halves(x):
    half = MLA_ROPE // 2
    return jnp.concatenate([x[:, :MLA_NOPE], x[:, MLA_NOPE + half:], x[:, MLA_NOPE:MLA_NOPE + half]], axis=1)


def _norm_rope(x, g, c, s):
    xn = x * _rstd(x) * g
    return xn * c + _swap_rope_halves(xn) * s


def _norm_rope_bwd(dy, x, g, c, s):
    return _rms_bwd(dy * c + _swap_rope_halves(dy * s), x, g, _rstd(x))


def _mla_qprep_fwd(q_raw, g, ctab, stab, name):
    T, d = q_raw.shape[0], MLA_QK
    H = q_raw.shape[1] // d
    tm = _tile(T, 1024)

    def body(x_ref, g_ref, c_ref, s_ref, o_ref):
        for h in range(H):
            o_ref[h] = _norm_rope(x_ref[:, h * d:(h + 1) * d], g_ref[...], c_ref[...], s_ref[...]).astype(BF16)

    tab = pl.BlockSpec((tm, d), lambda i: (i, 0))
    return pl.pallas_call(
        body, name=name, grid=(T // tm,),
        in_specs=[pl.BlockSpec((tm, H * d), lambda i: (i, 0)), pl.BlockSpec((1, d), lambda i: (0, 0)), tab, tab],
        out_specs=pl.BlockSpec((H, tm, d), lambda i: (0, i, 0)), out_shape=SDS((H, T, d), BF16),
        compiler_params=_params(1),
    )(q_raw, g, ctab, stab)


def _mla_qprep_bwd(q_raw, g, ctab, stab, dy, name):
    T, d = q_raw.shape[0], MLA_QK
    H = q_raw.shape[1] // d
    tm = _tile(T, 1024)

    def body(x_ref, g_ref, c_ref, s_ref, dy_ref, dx_ref, dg_ref):
        @pl.when(pl.program_id(0) == 0)
        def _():
            dg_ref[...] = jnp.zeros_like(dg_ref)

        dg = jnp.zeros((1, d), F32)
        for h in range(H):
            dx, dgh = _norm_rope_bwd(dy_ref[h], x_ref[:, h * d:(h + 1) * d], g_ref[...], c_ref[...], s_ref[...])
            dx_ref[:, h * d:(h + 1) * d] = dx.astype(BF16)
            dg = dg + dgh
        dg_ref[...] += dg

    tab = pl.BlockSpec((tm, d), lambda i: (i, 0))
    tok = pl.BlockSpec((tm, H * d), lambda i: (i, 0))
    gs = pl.BlockSpec((1, d), lambda i: (0, 0))
    return pl.pallas_call(
        body, name=name, grid=(T // tm,),
        in_specs=[tok, gs, tab, tab, pl.BlockSpec((H, tm, d), lambda i: (0, i, 0))],
        out_specs=[tok, gs], out_shape=[SDS((T, H * d), BF16), SDS((1, d), F32)],
        compiler_params=_params(1),
    )(q_raw, g, ctab, stab, dy)


def _mla_kprep_fwd(kv_raw, z, kr_col, g, ctab, stab, name):
    T, d, hw = kv_raw.shape[0], MLA_QK, MLA_NOPE + MLA_V
    H = kv_raw.shape[1] // hw
    tm = _tile(T, 1024)

    def body(kv_ref, kr_ref, g_ref, c_ref, s_ref, k_ref, v_ref):
        kr = kr_ref[:, :MLA_ROPE]
        for h in range(H):
            x = jnp.concatenate([kv_ref[:, h * hw:h * hw + MLA_NOPE], kr], axis=1)
            k_ref[h] = _norm_rope(x, g_ref[...], c_ref[...], s_ref[...]).astype(BF16)
            v_ref[h] = kv_ref[:, h * hw + MLA_NOPE:(h + 1) * hw].astype(BF16)

    tab = pl.BlockSpec((tm, d), lambda i: (i, 0))
    return pl.pallas_call(
        body, name=name, grid=(T // tm,),
        in_specs=[pl.BlockSpec((tm, H * hw), lambda i: (i, 0)), pl.BlockSpec((tm, LANE), lambda i: (i, kr_col)),
                  pl.BlockSpec((1, d), lambda i: (0, 0)), tab, tab],
        out_specs=[pl.BlockSpec((H, tm, d), lambda i: (0, i, 0)), pl.BlockSpec((H, tm, MLA_V), lambda i: (0, i, 0))],
        out_shape=[SDS((H, T, d), BF16), SDS((H, T, MLA_V), BF16)],
        compiler_params=_params(1),
    )(kv_raw, z, g, ctab, stab)


def _mla_kprep_bwd(kv_raw, z, kr_col, g, ctab, stab, dk, dv, name):
    T, d, hw = kv_raw.shape[0], MLA_QK, MLA_NOPE + MLA_V
    H = kv_raw.shape[1] // hw
    tm = _tile(T, 1024)

    def body(kv_ref, kr_ref, g_ref, c_ref, s_ref, dk_ref, dv_ref, dkv_ref, dkr_ref, dg_ref):
        @pl.when(pl.program_id(0) == 0)
        def _():
            dg_ref[...] = jnp.zeros_like(dg_ref)

        kr = kr_ref[:, :MLA_ROPE]
        dg = jnp.zeros((1, d), F32)
        dkr = jnp.zeros((tm, MLA_ROPE), F32)
        for h in range(H):
            x = jnp.concatenate([kv_ref[:, h * hw:h * hw + MLA_NOPE], kr], axis=1)
            dx, dgh = _norm_rope_bwd(dk_ref[h], x, g_ref[...], c_ref[...], s_ref[...])
            dkv_ref[:, h * hw:h * hw + MLA_NOPE] = dx[:, :MLA_NOPE].astype(BF16)
            dkv_ref[:, h * hw + MLA_NOPE:(h + 1) * hw] = dv_ref[h].astype(BF16)
            dkr = dkr + dx[:, MLA_NOPE:]
            dg = dg + dgh
        dkr_ref[...] = jnp.concatenate([dkr, jnp.zeros((tm, LANE - MLA_ROPE), F32)], axis=1)
        dg_ref[...] += dg

    tab = pl.BlockSpec((tm, d), lambda i: (i, 0))
    tok = pl.BlockSpec((tm, H * hw), lambda i: (i, 0))
    gs = pl.BlockSpec((1, d), lambda i: (0, 0))
    return pl.pallas_call(
        body, name=name, grid=(T // tm,),
        in_specs=[tok, pl.BlockSpec((tm, LANE), lambda i: (i, kr_col)), gs, tab, tab,
                  pl.BlockSpec((H, tm, d), lambda i: (0, i, 0)), pl.BlockSpec((H, tm, MLA_V), lambda i: (0, i, 0))],
        out_specs=[tok, pl.BlockSpec((tm, LANE), lambda i: (i, 0)), gs],
        out_shape=[SDS((T, H * hw), BF16), SDS((T, LANE), F32), SDS((1, d), F32)],
        compiler_params=_params(1),
    )(kv_raw, z, g, ctab, stab, dk, dv)


def _mla_attn_fwd(q, k, v, name, exch=None):
    H, T, d = q.shape
    dv = v.shape[-1]
    tq = _tile(T, 256)
    scale = d ** -0.5

    def body(q_ref, k_ref, v_ref, o_ref, lse_ref):
        s = _dot(q_ref[0], k_ref[0], NT) * scale
        m = jnp.max(s, axis=-1, keepdims=True)
        p = jnp.exp(s - m)
        l = jnp.sum(p, axis=-1, keepdims=True)
        o_ref[0] = _dot((p / l).astype(BF16), v_ref[0])
        lse_ref[0] = _col_to_row(m + jnp.log(l))

    return _call(
        body, name=name, grid=(H, T // tq),
        in_specs=[pl.BlockSpec((1, tq, d), lambda h, i: (h, i, 0)), pl.BlockSpec((1, T, d), lambda h, i: (h, 0, 0)),
                  pl.BlockSpec((1, T, dv), lambda h, i: (h, 0, 0))],
        out_specs=[pl.BlockSpec((1, tq, dv), lambda h, i: (h, i, 0)), pl.BlockSpec((1, 1, tq), lambda h, i: (h, 0, i))],
        out_shape=[SDS((H, T, dv), F32), SDS((H, 1, T), F32)],
        args=(q, k, v), exch=exch)


def _col_to_row(col):
    m = col.shape[0]
    eye = lax.broadcasted_iota(jnp.int32, (m, m), 0) == lax.broadcasted_iota(jnp.int32, (m, m), 1)
    return jnp.sum(jnp.where(eye, col, 0.0), axis=0, keepdims=True)


def _mla_attn_bwd(q, k, v, o, lse, do, name, exch=None):
    H, T, d = q.shape
    dv = v.shape[-1]
    tq = _tile(T, 256)
    scale = d ** -0.5

    def body(q_ref, k_ref, v_ref, o_ref, lse_ref, do_ref, dq_ref, dk_ref, dv_ref):
        @pl.when(pl.program_id(1) == 0)
        def _():
            dk_ref[...] = jnp.zeros_like(dk_ref)
            dv_ref[...] = jnp.zeros_like(dv_ref)

        qv, kv = q_ref[0], k_ref[0]
        dov = do_ref[0]
        dob = dov.astype(BF16)
        delta_row = _col_to_row(jnp.sum(dov * o_ref[0], axis=-1, keepdims=True))
        pt = jnp.exp(_dot(kv, qv, NT) * scale - lse_ref[0])
        dst = (pt * (_dot(v_ref[0], dob, NT) - delta_row) * scale).astype(BF16)
        dv_ref[0] += _dot(pt.astype(BF16), dob)
        dk_ref[0] += _dot(dst, qv)
        dq_ref[0] = _dot(dst, kv, TN)

    qb = pl.BlockSpec((1, tq, d), lambda h, i: (h, i, 0))
    kb = pl.BlockSpec((1, T, d), lambda h, i: (h, 0, 0))
    vb = pl.BlockSpec((1, T, dv), lambda h, i: (h, 0, 0))
    ob = pl.BlockSpec((1, tq, dv), lambda h, i: (h, i, 0))
    return _call(
        body, name=name, grid=(H, T // tq),
        in_specs=[qb, kb, vb, ob, pl.BlockSpec((1, 1, tq), lambda h, i: (h, 0, i)), ob],
        out_specs=[qb, kb, vb],
        out_shape=[SDS((H, T, d), F32), SDS((H, T, d), F32), SDS((H, T, dv), F32)],
        args=(q, k, v, o, lse, do), exch=exch)


def _silu_parts(x):
    s = _sigmoid(x)
    return x * s, s * (1.0 + x * (1.0 - s))


def _mix_out_fwd(x1, y_na, o_gla, y_mla, z, gr_col, gnorm, wbr, w_out, name):
    T, D = x1.shape
    W = y_na.shape[0] * y_na.shape[2]
    tm = _tile(T, 512)

    def body(x_ref, na_ref, gla_ref, mla_ref, gt_ref, gr_ref, gn_ref, w0, w1, w2, wo_ref, o_ref):
        gn = gn_ref[...]
        nrm = jnp.concatenate([gla_ref[h] * _rstd(gla_ref[h]) * gn for h in range(GLA_HEADS)], axis=1)
        y_gla = nrm * _silu_parts(gr_ref[...])[0]
        y_na = jnp.concatenate([na_ref[h] for h in range(NA_HEADS)], axis=1)
        y_mla = jnp.concatenate([mla_ref[h] for h in range(MLA_HEADS)], axis=1)
        mixed = jnp.zeros((tm, D), F32)
        for i, (y, w_ref) in enumerate(((y_na, w0), (y_gla, w1), (y_mla, w2))):
            mixed = mixed + _sigmoid(gt_ref[:, i * D:(i + 1) * D]) * _dot(y.astype(BF16), w_ref[...])
        o_ref[...] = x_ref[...] + _dot(mixed.astype(BF16), wo_ref[...])

    tok = lambda w: pl.BlockSpec((tm, w), lambda i: (i, 0))
    hm = lambda a: pl.BlockSpec((a.shape[0], tm, a.shape[2]), lambda i: (0, i, 0))
    full = lambda shape: pl.BlockSpec(shape, lambda i: (0, 0))
    return pl.pallas_call(
        body, name=name, grid=(T // tm,),
        in_specs=[tok(D), hm(y_na), hm(o_gla), hm(y_mla), tok(3 * D), pl.BlockSpec((tm, W), lambda i: (i, gr_col)),
                  full((1, GLA_DV)), full((W, D)), full((W, D)), full((W, D)), full((D, D))],
        out_specs=tok(D), out_shape=SDS((T, D), F32),
        compiler_params=_params(1),
    )(x1, y_na, o_gla, y_mla, z, z, gnorm, wbr[0], wbr[1], wbr[2], w_out)


def _mix_out_bwd(dx2, y_na, o_gla, y_mla, z, gr_col, gnorm, wbr, w_out, name):
    T, D = dx2.shape
    W = y_na.shape[0] * y_na.shape[2]
    tm = _tile(T, 256)

    def body(dx_ref, na_ref, gla_ref, mla_ref, gt_ref, gr_ref, gn_ref, w0, w1, w2, wo_ref,
             dna_ref, dgla_ref, dmla_ref, dgr_ref, dgt_ref, dw0, dw1, dw2, dwo_ref, dgn_ref):
        @pl.when(pl.program_id(0) == 0)
        def _():
            for r in (dw0, dw1, dw2, dwo_ref, dgn_ref):
                r[...] = jnp.zeros_like(r)

        gn = gn_ref[...]
        grv = gr_ref[...]
        sil, dsil = _silu_parts(grv)
        rs = [_rstd(gla_ref[h]) for h in range(GLA_HEADS)]
        nrm = jnp.concatenate([gla_ref[h] * rs[h] * gn for h in range(GLA_HEADS)], axis=1)
        ys = (jnp.concatenate([na_ref[h] for h in range(NA_HEADS)], axis=1).astype(BF16), (nrm * sil).astype(BF16),
              jnp.concatenate([mla_ref[h] for h in range(MLA_HEADS)], axis=1).astype(BF16))
        dxb = dx_ref[...].astype(BF16)
        dmixed = _dot(dxb, wo_ref[...], NT)
        mixed = jnp.zeros((tm, D), F32)
        dys = []
        for i, (y, w_ref, dw_ref) in enumerate(zip(ys, (w0, w1, w2), (dw0, dw1, dw2))):
            gt = _sigmoid(gt_ref[:, i * D:(i + 1) * D])
            p = _dot(y, w_ref[...])
            mixed = mixed + gt * p
            dp = (dmixed * gt).astype(BF16)
            dgt_ref[:, i * D:(i + 1) * D] = (dmixed * p * gt * (1.0 - gt)).astype(BF16)
            dys.append(_dot(dp, w_ref[...], NT))
            dw_ref[...] += _dot(y, dp, TN)
        dwo_ref[...] += _dot(mixed.astype(BF16), dxb, TN)
        for h in range(NA_HEADS):
            dna_ref[h] = dys[0][:, h * NA_HD:(h + 1) * NA_HD]
        for h in range(MLA_HEADS):
            dmla_ref[h] = dys[2][:, h * MLA_V:(h + 1) * MLA_V]
        dgr_ref[...] = dys[1] * nrm * dsil
        dn = dys[1] * sil
        dgn = jnp.zeros((1, GLA_DV), F32)
        for h in range(GLA_HEADS):
            dxh, dgh = _rms_bwd(dn[:, h * GLA_DV:(h + 1) * GLA_DV], gla_ref[h], gn, rs[h])
            dgla_ref[h] = dxh
            dgn = dgn + dgh
        dgn_ref[...] += dgn

    tok = lambda w: pl.BlockSpec((tm, w), lambda i: (i, 0))
    hm = lambda a: pl.BlockSpec((a.shape[0], tm, a.shape[2]), lambda i: (0, i, 0))
    full = lambda shape: pl.BlockSpec(shape, lambda i: (0, 0))
    return pl.pallas_call(
        body, name=name, grid=(T // tm,),
        in_specs=[tok(D), hm(y_na), hm(o_gla), hm(y_mla), tok(3 * D), pl.BlockSpec((tm, W), lambda i: (i, gr_col)),
                  full((1, GLA_DV)), full((W, D)), full((W, D)), full((W, D)), full((D, D))],
        out_specs=[hm(y_na), hm(o_gla), hm(y_mla), tok(W), tok(3 * D), full((W, D)), full((W, D)), full((W, D)),
                   full((D, D)), full((1, GLA_DV))],
        out_shape=[SDS(y_na.shape, F32), SDS(o_gla.shape, F32), SDS(y_mla.shape, F32), SDS((T, W), F32),
                   SDS((T, 3 * D), BF16)] + [SDS((W, D), F32)] * 3 + [SDS((D, D), F32), SDS((1, GLA_DV), F32)],
        compiler_params=_params(1),
    )(dx2, y_na, o_gla, y_mla, z, z, gnorm, wbr[0], wbr[1], wbr[2], w_out)


def _split_z(z, lay, name):
    T = z.shape[0]
    tm = _tile(T, 512)
    na0, gv0, gq0 = lay['na_q'][0], lay['gv'][0], lay['gq'][0]
    assert na0 % (3 * NA_W) == 0 and gv0 % GLA_V_W == 0 and gq0 % (2 * GLA_QK_W) == 0
    assert lay['na_k'][0] == na0 + NA_W and lay['na_v'][0] == na0 + 2 * NA_W and lay['gk'][0] == gq0 + GLA_QK_W

    def body(na_ref, gv_ref, gqk_ref, q_ref, k_ref, v_ref, gq_ref, gk_ref, gvo_ref):
        for j, o_ref in enumerate((q_ref, k_ref, v_ref)):
            for h in range(NA_HEADS):
                o_ref[h] = na_ref[:, j * NA_W + h * NA_HD:j * NA_W + (h + 1) * NA_HD]
        for j, o_ref in enumerate((gq_ref, gk_ref)):
            for h in range(GLA_HEADS):
                o_ref[h] = gqk_ref[:, j * GLA_QK_W + h * GLA_DK:j * GLA_QK_W + (h + 1) * GLA_DK]
        for h in range(GLA_HEADS):
            gvo_ref[h] = gv_ref[:, h * GLA_DV:(h + 1) * GLA_DV]

    hm = lambda H, d: pl.BlockSpec((H, tm, d), lambda i: (0, i, 0))
    return pl.pallas_call(
        body, name=name, grid=(T // tm,),
        in_specs=[pl.BlockSpec((tm, 3 * NA_W), lambda i: (i, na0 // (3 * NA_W))),
                  pl.BlockSpec((tm, GLA_V_W), lambda i: (i, gv0 // GLA_V_W)),
                  pl.BlockSpec((tm, 2 * GLA_QK_W), lambda i: (i, gq0 // (2 * GLA_QK_W)))],
        out_specs=[hm(NA_HEADS, NA_HD)] * 3 + [hm(GLA_HEADS, GLA_DK)] * 2 + [hm(GLA_HEADS, GLA_DV)],
        out_shape=[SDS((NA_HEADS, T, NA_HD), F32)] * 3 + [SDS((GLA_HEADS, T, GLA_DK), F32)] * 2
        + [SDS((GLA_HEADS, T, GLA_DV), F32)],
        compiler_params=_params(1),
    )(z, z, z)


def _assemble_dz(dgates, na_dqkv, gla_dqk, gla_dv, dgr, dcq, dckv, dglow, dkr, lay, zp, name):
    T = dgr.shape[0]
    tm = _tile(T, 256)

    def body(dgt_ref, nq_ref, nk_ref, nv_ref, gq_ref, gk_ref, gv_ref, dgr_ref, dcq_ref, dckv_ref, dglow_ref, dkr_ref, dz_ref):
        def put(seg, off, val):
            a = lay[seg][0] + off
            dz_ref[:, a:a + val.shape[1]] = val.astype(BF16)

        put('gates', 0, dgt_ref[...])
        for seg, ref in (('na_q', nq_ref), ('na_k', nk_ref), ('na_v', nv_ref)):
            for h in range(NA_HEADS):
                put(seg, h * NA_HD, ref[h])
        for seg, ref in (('gq', gq_ref), ('gk', gk_ref)):
            for h in range(GLA_HEADS):
                put(seg, h * GLA_DK, ref[h])
        for h in range(GLA_HEADS):
            put('gv', h * GLA_DV, gv_ref[h])
        put('gr', 0, dgr_ref[...])
        put('c_q', 0, dcq_ref[...])
        put('c_kv', 0, dckv_ref[...])
        put('glow', 0, dglow_ref[...])
        put('k_rope', 0, dkr_ref[...])

    tok = lambda a: pl.BlockSpec((tm, a.shape[1]), lambda i: (i, 0))
    hm = lambda a: pl.BlockSpec((a.shape[0], tm, a.shape[2]), lambda i: (0, i, 0))
    args = (dgates, *na_dqkv, *gla_dqk, gla_dv, dgr, dcq, dckv, dglow, dkr)
    return pl.pallas_call(
        body, name=name, grid=(T // tm,),
        in_specs=[tok(dgates)] + [hm(a) for a in (*na_dqkv, *gla_dqk, gla_dv)] + [tok(dgr), tok(dcq), tok(dckv), tok(dglow),
                                                                                 tok(dkr)],
        out_specs=pl.BlockSpec((tm, zp), lambda i: (i, 0)), out_shape=SDS((T, zp), BF16),
        compiler_params=_params(1),
    )(*args)


def _loss_head(y, target, name):
    T, D = y.shape
    tm = _tile(T, 1024)

    def body(y_ref, t_ref, dy_ref, l_ref):
        @pl.when(pl.program_id(0) == 0)
        def _():
            l_ref[...] = jnp.zeros_like(l_ref)

        e = y_ref[...] - t_ref[...]
        dy_ref[...] = e * (1.0 / D)
        l_ref[...] += 0.5 * jnp.sum(jnp.mean(e * e, axis=-1, keepdims=True), axis=0, keepdims=True)

    tok = pl.BlockSpec((tm, D), lambda i: (i, 0))
    return pl.pallas_call(
        body, name=name, grid=(T // tm,), in_specs=[tok, tok],
        out_specs=[tok, pl.BlockSpec((1, 1), lambda i: (0, 0))],
        out_shape=[SDS((T, D), F32), SDS((1, 1), F32)],
        compiler_params=_params(1),
    )(y, target)


def _exchange_copies(src, dst, send_sems, recv_sems, local_sems, broadcast, with_recvs=True):
    n = len(src)
    x, y, c = lax.axis_index("x"), lax.axis_index("y"), lax.axis_index("c")
    me = 4 * x + 2 * y + c
    local = [pltpu.make_async_copy(src[a] if broadcast else src[a].at[me], dst[a].at[me], local_sems.at[a])
             for a in range(n)]
    sends, recvs = [], []
    for d in range(1, N_DEV):
        px = 1 - x if d & 4 else x
        py = 1 - y if d & 2 else y
        pc = 1 - c if d & 1 else c
        peer = 4 * px + 2 * py + pc
        for a in range(n):
            for out, slot in ((sends, me), (recvs, peer)) if with_recvs else ((sends, me),):
                out.append(pltpu.make_async_remote_copy(
                    src_ref=src[a] if broadcast else src[a].at[peer], dst_ref=dst[a].at[slot],
                    send_sem=send_sems.at[a, d - 1], recv_sem=recv_sems.at[a, d - 1],
                    device_id=(px, py, pc), device_id_type=pl.DeviceIdType.MESH))
    return local, sends, recvs


def _gather_copies(src, dst, send_sems, recv_sems, local_sems, with_local=True):
    x, y, c = lax.axis_index("x"), lax.axis_index("y"), lax.axis_index("c")
    chips = [(1 - x, y), (x, 1 - y), (1 - x, 1 - y)]
    lin = lambda px, py, pc: 4 * px + 2 * py + pc

    def copy(a, k, incoming=False):
        if k == 0:
            to, out_src, out_slot, in_slot = (x, y, 1 - c), src[a], lin(x, y, c), lin(x, y, 1 - c)
        elif k <= 3:
            to, out_src, out_slot, in_slot = (*chips[k - 1], c), src[a], lin(x, y, c), lin(*chips[k - 1], c)
        else:
            slot = lin(*chips[k - 4], c)
            to, out_src, out_slot, in_slot = (x, y, 1 - c), dst[a].at[slot], slot, lin(*chips[k - 4], 1 - c)
        return pltpu.make_async_remote_copy(
            src_ref=out_src, dst_ref=dst[a].at[in_slot if incoming else out_slot], send_sem=send_sems.at[a, k],
            recv_sem=recv_sems.at[a, k], device_id=to, device_id_type=pl.DeviceIdType.MESH)

    local = [pltpu.make_async_copy(src[a], dst[a].at[lin(x, y, c)], local_sems.at[a])
             for a in range(len(src) if with_local else 0)]
    return copy, local


def _exchange_start(src, dst, send_sems, recv_sems, local_sems, broadcast):
    if broadcast:
        copy, local = _gather_copies(src, dst, send_sems, recv_sems, local_sems)
        for cp in local:
            cp.start()
        for k in range(4):
            for a in range(len(src)):
                copy(a, k).start()
        return
    local, sends, _ = _exchange_copies(src, dst, send_sems, recv_sems, local_sems, broadcast, with_recvs=False)
    for cp in local + sends:
        cp.start()


def _exchange_pass_on(src, dst, send_sems, recv_sems, local_sems, broadcast):
    if broadcast:
        copy, _ = _gather_copies(src, dst, send_sems, recv_sems, local_sems, with_local=False)
        for k in range(1, 4):
            for a in range(len(src)):
                copy(a, k, True).wait_recv()
                copy(a, k + 3).start()


def _exchange_wait(src, dst, send_sems, recv_sems, local_sems, broadcast):
    if broadcast:
        copy, local = _gather_copies(src, dst, send_sems, recv_sems, local_sems)
        n = len(src)
        for k in (0, 4, 5, 6):
            for a in range(n):
                copy(a, k, True).wait_recv()
        for k in range(N_DEV - 1):
            for a in range(n):
                copy(a, k).wait_send()
        for cp in local:
            cp.wait()
        return
    local, sends, recvs = _exchange_copies(src, dst, send_sems, recv_sems, local_sems, broadcast)
    for cp in recvs:
        cp.wait_recv()
    for cp in sends:
        cp.wait_send()
    for cp in local:
        cp.wait()


def _exchange_shapes(srcs, broadcast):
    n = len(srcs)
    out_shape = [SDS((N_DEV,) + (tuple(s.shape) if broadcast else tuple(s.shape[1:])), s.dtype) for s in srcs]
    sems = [pltpu.SemaphoreType.DMA((n, N_DEV - 1)), pltpu.SemaphoreType.DMA((n, N_DEV - 1)),
            pltpu.SemaphoreType.DMA((n,))]
    return out_shape, sems


def _exchange(srcs, broadcast, name):
    n = len(srcs)
    out_shape, sems = _exchange_shapes(srcs, broadcast)

    def body(*refs):
        mode = (refs[:n], refs[n:2 * n]) + tuple(refs[2 * n:]) + (broadcast,)
        _exchange_start(*mode)
        _exchange_pass_on(*mode)
        _exchange_wait(*mode)

    hbm = pl.BlockSpec(memory_space=pltpu.HBM)
    return pl.pallas_call(body, name=name, in_specs=[hbm] * n, out_specs=[hbm] * n, out_shape=out_shape,
                          scratch_shapes=sems)(*srcs)


def _call(body, *, name, grid, in_specs, out_specs, out_shape, args, scratch_shapes=(), exch=None):
    params = _params(len(grid))
    if exch is None:
        return pl.pallas_call(body, name=name, grid=grid, in_specs=in_specs, out_specs=out_specs, out_shape=out_shape,
                              scratch_shapes=list(scratch_shapes), compiler_params=params)(*args)
    srcs, broadcast = exch
    n, n_in, n_out, n_scr = len(srcs), len(args), len(out_shape), len(scratch_shapes)
    x_shape, sems = _exchange_shapes(srcs, broadcast)

    def carrier(*refs):
        ins, x_src = refs[:n_in], refs[n_in:n_in + n]
        outs, x_dst = refs[n_in + n:n_in + n + n_out], refs[n_in + n + n_out:n_in + 2 * n + n_out]
        scr = refs[n_in + 2 * n + n_out:n_in + 2 * n + n_out + n_scr]
        mode = (x_src, x_dst) + tuple(refs[n_in + 2 * n + n_out + n_scr:]) + (broadcast,)
        step = 0
        for a, g in enumerate(grid):
            step = step * g + pl.program_id(a)
        steps = int(np.prod(grid))
        pl.when(step == 0)(lambda: _exchange_start(*mode))
        body(*ins, *outs, *scr)
        pl.when(step == (3 * steps) // 4)(lambda: _exchange_pass_on(*mode))
        pl.when(step == steps - 1)(lambda: _exchange_wait(*mode))

    hbm = pl.BlockSpec(memory_space=pltpu.HBM)
    res = pl.pallas_call(carrier, name=name, grid=grid, in_specs=list(in_specs) + [hbm] * n,
                         out_specs=list(out_specs) + [hbm] * n, out_shape=list(out_shape) + x_shape,
                         scratch_shapes=list(scratch_shapes) + sems, compiler_params=params)(*args, *srcs)
    return res[:n_out], res[n_out:]


def _adam_math(w, g, m, v):
    m = ADAM_B1 * m + (1.0 - ADAM_B1) * g
    v = ADAM_B2 * v + (1.0 - ADAM_B2) * (g * g)
    m_hat = m / (1.0 - ADAM_B1 ** ADAM_STEP)
    v_hat = v / (1.0 - ADAM_B2 ** ADAM_STEP)
    delta = -ADAM_LR * (m_hat / (jnp.sqrt(v_hat) + ADAM_EPS) + ADAM_WD * w)
    return delta, m, v


def _adamw_sharded(landed, k, w, m, v, name):
    L, r, c = w.shape
    lanes = -(-c // LANE) * LANE
    rb = _tile(r, max(16, (1 << 21) // (N_DEV * lanes * 2)), 16)

    def body(*refs):
        l_refs = refs[:L]
        w_ref, m_ref, v_ref, g_ref, d_ref, nm_ref, nv_ref = refs[L:]
        for j in range(L):
            @pl.when(pl.program_id(0) == j)
            def _(j=j):
                g = l_refs[j][0, 0].astype(F32)
                for s in range(1, N_DEV):
                    g = g + l_refs[j][s, 0].astype(F32)
                delta, nm, nv = _adam_math(w_ref[0], g, m_ref[0], v_ref[0])
                g_ref[0] = g
                d_ref[0] = delta
                nm_ref[0] = nm
                nv_ref[0] = nv

    blk = pl.BlockSpec((1, rb, c), lambda l, i: (l, i, 0))
    land = [pl.BlockSpec((N_DEV, 1, rb, c), lambda l, i, j=j: (0, k, jnp.where(l == j, i, 0), 0)) for j in range(L)]
    return pl.pallas_call(
        body, name=name, grid=(L, r // rb),
        in_specs=land + [blk, blk, blk],
        out_specs=[blk] * 4, out_shape=[SDS((L, r, c), F32)] * 4,
        compiler_params=_params(2),
    )(*landed, w, m, v)


def _adamw_replicated(landed, w, m, v, name):
    R = w.shape[0]

    def body(l_ref, w_ref, m_ref, v_ref, g_ref, d_ref, nm_ref, nv_ref):
        g = l_ref[0]
        for s in range(1, N_DEV):
            g = g + l_ref[s]
        delta, nm, nv = _adam_math(w_ref[...], g, m_ref[...], v_ref[...])
        g_ref[...] = g
        d_ref[...] = delta
        nm_ref[...] = nm
        nv_ref[...] = nv

    blk = pl.BlockSpec((R, LANE), lambda i: (0, 0))
    return pl.pallas_call(
        body, name=name, grid=(1,),
        in_specs=[pl.BlockSpec((N_DEV, R, LANE), lambda i: (0, 0, 0)), blk, blk, blk],
        out_specs=[blk] * 4, out_shape=[SDS((R, LANE), F32)] * 4,
        compiler_params=_params(1),
    )(landed, w, m, v)


def _pack(parts):
    flat = jnp.concatenate([p.reshape(-1) for p in parts])
    rows = -(-flat.shape[0] // (8 * LANE)) * 8
    return jnp.pad(flat, (0, rows * LANE - flat.shape[0])).reshape(rows, LANE)


def _unpack(packed, shapes):
    flat = packed.reshape(-1)
    out, off = [], 0
    for s in shapes:
        size = int(np.prod(s))
        out.append(flat[off:off + size].reshape(s))
        off += size
    return out


def _layer_fwd(x0, w, l, lay, tabs, carry):
    T, D = x0.shape
    ctab, stab, na_bias = tabs
    col = lambda name: lay[name][0] // lay[name][1]
    sv = {'x0': x0}
    x1, sv['a1'], sv['b1'] = _carried(_ffn_fwd, carry, 'ffn1', x0, w['ffn1_norm'], w['ffn1_w1'], w['ffn1_w3'],
                                      w['ffn1_w2'], f"ffn1_fwd_{l}")
    z, sv['hb'] = _carried(_norm_matmul, carry, 'mix_in', x1, 0, w['mix_norm'], w['w_in_z'], f"mix_in_{l}")
    sv['na_q'], sv['na_k'], sv['na_v'], sv['gq'], sv['gk'], sv['gv'] = _split_z(z, lay, f"split_z_{l}")
    y_na = _carried(_na_fwd, carry, 'na', sv['na_q'], sv['na_k'], sv['na_v'], na_bias, l, w['na_q_norm'], w['na_k_norm'],
                    f"na_fwd_{l}")[0]
    sv['gg'] = _gla_gate_fwd(z, col('glow'), w['up_f'], w['up_b'], w['gla_gf_bias'], w['gla_gb_bias'], f"gla_gate_{l}")
    o_gla = _gla_fwd(sv['gq'], sv['gk'], sv['gv'], sv['gg'], f"gla_fwd_{l}")
    sv['q_raw'], sv['hq'] = _norm_matmul(z, col('c_q'), w['mla_cq_norm'], w['mla_w_uq'], f"mla_uq_{l}")
    sv['kv_raw'], sv['hkv'] = _norm_matmul(z, col('c_kv'), w['mla_ckv_norm'], w['mla_w_ukv'], f"mla_ukv_{l}")
    sv['mq'] = _mla_qprep_fwd(sv['q_raw'], w['mla_q_norm'], ctab, stab, f"mla_qprep_{l}")
    sv['mk'], sv['mv'] = _mla_kprep_fwd(sv['kv_raw'], z, col('k_rope'), w['mla_k_norm'], ctab, stab, f"mla_kprep_{l}")
    y_mla, sv['lse'] = _carried(_mla_attn_fwd, carry, 'mla', sv['mq'], sv['mk'], sv['mv'], f"mla_attn_{l}")
    w_br = (w['w_br_na'], w['w_br_gla'], w['w_br_mla'])
    x2 = _mix_out_fwd(x1, y_na, o_gla, y_mla, z, col('gr'), w['gla_out_norm'], w_br, w['w_out'], f"mix_out_{l}")
    sv.update(x1=x1, z=z, y_na=y_na, o_gla=o_gla, y_mla=y_mla, x2=x2)
    x3, sv['a2'], sv['b2'] = _carried(_ffn_fwd, carry, 'ffn2', x2, w['ffn2_norm'], w['ffn2_w1'], w['ffn2_w3'],
                                      w['ffn2_w2'], f"ffn2_fwd_{l}")
    return x3, sv


def _ffn_grads(grads, x, dy, a, b, w, pre, l, carry):
    dx, da, db, u, hb, dyb, dg = _carried(_ffn_bwd, carry, pre, x, w[pre + '_norm'], dy, a, b, w[pre + '_w1'],
                                          w[pre + '_w3'], w[pre + '_w2'], f"{pre}_bwd_{l}")
    grads.update({pre + '_norm': dg,
                  pre + '_w1': _matmul_tn(hb, da, f"{pre}_dw1_{l}"),
                  pre + '_w3': _matmul_tn(hb, db, f"{pre}_dw3_{l}"),
                  pre + '_w2': _matmul_tn(u, dyb, f"{pre}_dw2_{l}")})
    return dx


def _layer_bwd(grads, dx3, sv, w, l, lay, tabs, carry):
    T, D = dx3.shape
    ctab, stab, na_bias, onehot = tabs
    col = lambda name: lay[name][0] // lay[name][1]
    z = sv['z']
    dx2 = _ffn_grads(grads, sv['x2'], dx3, sv['a2'], sv['b2'], w, 'ffn2', l, carry)
    w_br = (w['w_br_na'], w['w_br_gla'], w['w_br_mla'])
    (dy_na, do_gla, dy_mla, dgr, dgates, dw0, dw1, dw2, dwo, dgn) = _mix_out_bwd(
        dx2, sv['y_na'], sv['o_gla'], sv['y_mla'], z, col('gr'), w['gla_out_norm'], w_br, w['w_out'], f"mix_out_bwd_{l}")
    grads.update(w_br_na=dw0, w_br_gla=dw1, w_br_mla=dw2, w_out=dwo, gla_out_norm=dgn)
    dq, dk, dv, dbias, dgq, dgk = _na_bwd(sv['na_q'], sv['na_k'], sv['na_v'], dy_na, na_bias, l,
                                          w['na_q_norm'], w['na_k_norm'], f"na_bwd_{l}")
    grads.update(na_q_norm=dgq, na_k_norm=dgk, na_rpb=_na_bias_tables_bwd(dbias, onehot))
    gdq, gdk, gdv, gdg = _gla_bwd(sv['gq'], sv['gk'], sv['gv'], sv['gg'], do_gla, f"gla_bwd_{l}")
    dglow, dupf, dupb, dbf, dbb = _gla_gate_bwd(z, col('glow'), w['up_f'], w['up_b'], w['gla_gf_bias'], w['gla_gb_bias'],
                                                gdg, f"gla_gate_bwd_{l}")
    grads.update(gla_gf_up=dupf[:GLA_RANK], gla_gb_up=dupb[GLA_RANK:2 * GLA_RANK], gla_gf_bias=dbf, gla_gb_bias=dbb)
    mdq, mdk, mdv = _carried(_mla_attn_bwd, carry, 'mla', sv['mq'], sv['mk'], sv['mv'], sv['y_mla'], sv['lse'], dy_mla,
                             f"mla_attn_bwd_{l}")
    dq_raw, dgqn = _mla_qprep_bwd(sv['q_raw'], w['mla_q_norm'], ctab, stab, mdq, f"mla_qprep_bwd_{l}")
    dkv_raw, dkr, dgkn = _mla_kprep_bwd(sv['kv_raw'], z, col('k_rope'), w['mla_k_norm'], ctab, stab, mdk, mdv,
                                        f"mla_kprep_bwd_{l}")
    dcq, dgcq = _norm_matmul_bwd(dq_raw, w['mla_w_uq'], z, col('c_q'), w['mla_cq_norm'], None, f"mla_uq_bwd_{l}")
    dckv, dgckv = _norm_matmul_bwd(dkv_raw, w['mla_w_ukv'], z, col('c_kv'), w['mla_ckv_norm'], None, f"mla_ukv_bwd_{l}")
    grads.update(mla_q_norm=dgqn, mla_k_norm=dgkn, mla_cq_norm=dgcq, mla_ckv_norm=dgckv,
                 mla_w_uq=_matmul_tn(sv['hq'], dq_raw, f"mla_duq_{l}", F32),
                 mla_w_ukv=_matmul_tn(sv['hkv'], dkv_raw, f"mla_dukv_{l}", F32))
    dz = _assemble_dz(dgates, (dq, dk, dv), (gdq, gdk), gdv, dgr, dcq, dckv, dglow, dkr, lay, z.shape[1], f"dz_{l}")
    dx1, dgmix = _carried(_norm_matmul_bwd, carry, 'mix_in', dz, w['w_in_z'], sv['x1'], 0, w['mix_norm'], dx2,
                          f"mix_in_bwd_{l}")
    grads.update(mix_norm=dgmix, w_in=_matmul_tn(sv['hb'], dz, f"mix_dw_in_{l}"))
    return _ffn_grads(grads, sv['x0'], dx1, sv['a1'], sv['b1'], w, 'ffn1', l, carry)


EXCHANGE_PARTS = {
    'F1a': [['ffn1_w1', 'ffn1_w3']],
    'F1b': [['ffn1_w2']],
    'F2': [['ffn2_w1', 'ffn2_w3'], ['ffn2_w2']],
    'C': [['w_in']],
    'S': [['w_br_na', 'w_br_gla', 'w_br_mla'], ['w_out'], ['mla_w_uq'], ['mla_w_ukv'], ['gla_gf_up', 'gla_gb_up']],
}
FWD_PLAN = {'ffn1': [(0, 'C')], 'mix_in': [(0, 'S')], 'mla': [(0, 'F2')], 'na': [(1, 'F1b')], 'ffn2': [(1, 'F1a')]}
BWD_PLAN = {'mla': [(0, 'F2')], 'mix_in': [(0, 'S')], 'ffn1': [(0, 'C')], 'ffn2': [(1, 'F1a'), (1, 'F1b')]}
EDGE_PARTS = ['F1a', 'F1b']


def _carried(fn, carry, key, *args):
    if key in carry:
        make_srcs, broadcast, deliver = carry[key]
        outs, landed = fn(*args, exch=(make_srcs(), broadcast))
        deliver(landed)
        return outs
    return fn(*args)


def kernel(x, ffn1_norm, ffn1_w1, ffn1_w3, ffn1_w2, mix_norm, w_in, na_q_norm, na_k_norm, na_rpb, gla_gf_up, gla_gf_bias, gla_gb_up, gla_gb_bias, gla_out_norm, mla_cq_norm, mla_ckv_norm, mla_w_uq, mla_w_ukv, mla_q_norm, mla_k_norm, w_br_na, w_br_gla, w_br_mla, w_out, ffn2_norm, ffn2_w1, ffn2_w3, ffn2_w2, loss_target, m_ffn1_norm, m_ffn1_w1, m_ffn1_w3, m_ffn1_w2, m_mix_norm, m_w_in, m_na_q_norm, m_na_k_norm, m_na_rpb, m_gla_gf_up, m_gla_gf_bias, m_gla_gb_up, m_gla_gb_bias, m_gla_out_norm, m_mla_cq_norm, m_mla_ckv_norm, m_mla_w_uq, m_mla_w_ukv, m_mla_q_norm, m_mla_k_norm, m_w_br_na, m_w_br_gla, m_w_br_mla, m_w_out, m_ffn2_norm, m_ffn2_w1, m_ffn2_w3, m_ffn2_w2, v_ffn1_norm, v_ffn1_w1, v_ffn1_w3, v_ffn1_w2, v_mix_norm, v_w_in, v_na_q_norm, v_na_k_norm, v_na_rpb, v_gla_gf_up, v_gla_gf_bias, v_gla_gb_up, v_gla_gb_bias, v_gla_out_norm, v_mla_cq_norm, v_mla_ckv_norm, v_mla_w_uq, v_mla_w_ukv, v_mla_q_norm, v_mla_k_norm, v_w_br_na, v_w_br_gla, v_w_br_mla, v_w_out, v_ffn2_norm, v_ffn2_w1, v_ffn2_w3, v_ffn2_w2):
    given = dict(locals())
    P = {n: given[n] for n in WEIGHTS}
    M = {n: given['m_' + n] for n in WEIGHTS}
    V = {n: given['v_' + n] for n in WEIGHTS}
    xs = x[0]
    T, D = xs.shape
    L = ffn1_norm.shape[0]
    lay, zp, d_in = _z_layout(D)

    def arrays_of(items):
        return [(l, names) for l, p in items for names in EXCHANGE_PARTS[p]]

    ws = [{n: P[n][l][None, :] for n in REPLICATED if n != 'na_rpb'} for l in range(L)]

    def gather_srcs(items):
        return [jnp.stack([P[n][l].astype(BF16) for n in names]) for l, names in arrays_of(items)]

    def take_gathered(items, res):
        for (l, names), g in zip(arrays_of(items), res):
            for k, n in enumerate(names):
                blk = g[:, k]
                if n == 'w_in':
                    ws[l]['w_in_z'] = _z_cols_from_blocks(blk, lay)
                    continue
                if n in COL_SHARDED:
                    full = jnp.concatenate([blk[d] for d in range(N_DEV)], axis=1)
                else:
                    full = blk.reshape(N_DEV * blk.shape[1], blk.shape[2])
                if n == 'gla_gf_up':
                    ws[l]['up_f'] = jnp.zeros((LANE, GLA_QK_W), BF16).at[:GLA_RANK].set(full)
                elif n == 'gla_gb_up':
                    ws[l]['up_b'] = jnp.zeros((LANE, GLA_QK_W), BF16).at[GLA_RANK:2 * GLA_RANK].set(full)
                else:
                    ws[l][n] = full

    def plan(table, l, make_srcs, broadcast, deliver):
        carry = {}
        for key, items in table.items():
            items = [(l + off, p) for off, p in items if l + off < L]
            if items:
                carry[key] = (functools.partial(make_srcs, items), broadcast, functools.partial(deliver, items))
        return carry

    ctab, stab = _rope_tables(T)
    na_bias, onehot = _na_bias_tables(na_rpb)

    edge = [(0, p) for p in EDGE_PARTS]
    take_gathered(edge, _exchange(gather_srcs(edge), True, "gather_weights_edge"))
    saved = []
    h = xs
    for l in range(L):
        h, sv = _layer_fwd(h, ws[l], l, lay, (ctab, stab, na_bias), plan(FWD_PLAN, l, gather_srcs, True, take_gathered))
        saved.append(sv)
    dh, loss_local = _loss_head(h, loss_target[0], "loss_head")
    loss = lax.psum(loss_local[0, 0], ("x", "y", "c"))

    layer_grads = [dict() for _ in range(L)]

    def scatter_srcs(items):
        def blocks(l, n):
            g = layer_grads[l][n]
            if n == 'w_in':
                return _blocks_from_z_cols(g, lay, d_in).astype(BF16)
            if n in COL_SHARDED:
                c = g.shape[1] // N_DEV
                return jnp.stack([g[:, d * c:(d + 1) * c] for d in range(N_DEV)]).astype(BF16)
            return g.reshape(N_DEV, -1, g.shape[1]).astype(BF16)
        return [jnp.stack([blocks(l, n) for n in names], axis=1) for l, names in arrays_of(items)]

    landed = [dict() for _ in range(L)]

    def take_landed(items, res):
        for (l, names), r in zip(arrays_of(items), res):
            for k, n in enumerate(names):
                landed[l][n] = (r, k)

    for l in reversed(range(L)):
        dh = _layer_bwd(layer_grads[l], dh, saved[l], ws[l], l, lay, (ctab, stab, na_bias, onehot),
                        plan(BWD_PLAN, l, scatter_srcs, False, take_landed))
    grad_x = dh[None]
    take_landed(edge, _exchange(scatter_srcs(edge), False, "scatter_grads_edge"))

    out = {}
    for n in SHARDED:
        out[n] = _adamw_sharded([landed[l][n][0] for l in range(L)], landed[0][n][1], P[n], M[n], V[n], f"adamw_{n}")

    rep = [n for n in REPLICATED]
    rep_shapes = [tuple(P[n].shape) for n in rep]
    rep_partial = _pack([jnp.stack([layer_grads[l][n].reshape(P[n].shape[1:]) for l in range(L)]) for n in rep])
    (rep_landed,) = _exchange([rep_partial], True, "gather_small_grads")
    packed = _adamw_replicated(rep_landed, _pack([P[n] for n in rep]), _pack([M[n] for n in rep]),
                               _pack([V[n] for n in rep]), "adamw_replicated")
    for kind, pk in enumerate(packed):
        for n, arr in zip(rep, _unpack(pk, rep_shapes)):
            out.setdefault(n, [None] * 4)[kind] = arr

    res = [loss, grad_x]
    for kind in range(4):
        res += [out[n][kind] for n in WEIGHTS]
    return tuple(res)
```

```python
import functools

import numpy as np
import jax
import jax.numpy as jnp
from jax import lax
from jax.experimental import pallas as pl
from jax.experimental.pallas import tpu as pltpu

F32 = jnp.float32
BF16 = jnp.bfloat16
SDS = jax.ShapeDtypeStruct
HI = lax.Precision.HIGHEST

N_DEV = 8
DEPTH = 4
EPS = 1e-6
LANE = 128
VMEM_LIMIT = 56 * 1024 * 1024

GRID_W = 64
NA_HEADS, NA_HD, NA_WIN_R, NA_WIN_C = 8, 64, 8, 16
GLA_HEADS, GLA_DK, GLA_DV, GLA_RANK, GLA_TAU, GLA_CHUNK = 4, 64, 128, 16, 16.0, 64
MLA_HEADS, MLA_QR, MLA_KVR, MLA_NOPE, MLA_ROPE, MLA_V = 4, 256, 256, 128, 64, 128
MLA_QK = MLA_NOPE + MLA_ROPE
ROPE_THETA = 10000.0
NA_W = NA_HEADS * NA_HD
GLA_QK_W = GLA_HEADS * GLA_DK
GLA_V_W = GLA_HEADS * GLA_DV
MLA_V_W = MLA_HEADS * MLA_V

ADAM_LR, ADAM_B1, ADAM_B2, ADAM_EPS, ADAM_WD, ADAM_STEP = 0.001, 0.9, 0.999, 1e-08, 0.01, 10

WEIGHTS = ['ffn1_norm', 'ffn1_w1', 'ffn1_w3', 'ffn1_w2', 'mix_norm', 'w_in', 'na_q_norm', 'na_k_norm', 'na_rpb',
           'gla_gf_up', 'gla_gf_bias', 'gla_gb_up', 'gla_gb_bias', 'gla_out_norm', 'mla_cq_norm', 'mla_ckv_norm',
           'mla_w_uq', 'mla_w_ukv', 'mla_q_norm', 'mla_k_norm', 'w_br_na', 'w_br_gla', 'w_br_mla', 'w_out',
           'ffn2_norm', 'ffn2_w1', 'ffn2_w3', 'ffn2_w2']
COL_SHARDED = ['ffn1_w1', 'ffn1_w3', 'w_in', 'gla_gf_up', 'gla_gb_up', 'mla_w_uq', 'mla_w_ukv', 'w_br_na', 'w_br_gla',
               'w_br_mla', 'ffn2_w1', 'ffn2_w3']
ROW_SHARDED = ['ffn1_w2', 'w_out', 'ffn2_w2']
SHARDED = [n for n in WEIGHTS if n in COL_SHARDED or n in ROW_SHARDED]
REPLICATED = [n for n in WEIGHTS if n not in SHARDED]

NT = (((1,), (1,)), ((), ()))
TN = (((0,), (0,)), ((), ()))


def _tile(n, pref, mult=8):
    best = None
    for t in range(mult, min(n, pref) + 1, mult):
        if n % t == 0:
            best = t
    return best if best is not None else n


def _params(n_axes):
    return pltpu.CompilerParams(dimension_semantics=("arbitrary",) * n_axes, vmem_limit_bytes=VMEM_LIMIT)


def _dot(a, b, dims=None, precision=None):
    if dims is None:
        return jnp.dot(a, b, preferred_element_type=F32, precision=precision)
    return lax.dot_general(a, b, dims, preferred_element_type=F32, precision=precision)


def _sigmoid(x):
    return 1.0 / (1.0 + jnp.exp(-x))


def _rstd(x):
    return lax.rsqrt(jnp.mean(x * x, axis=-1, keepdims=True) + EPS)


def _rms_bwd(dy, x, g, r):
    xh = x * r
    dyg = dy * g
    dx = r * (dyg - xh * jnp.mean(dyg * xh, axis=-1, keepdims=True))
    return dx, jnp.sum(dy * xh, axis=0, keepdims=True)


def _z_layout(d_model):
    own = {}
    off = 0
    for name, w in [('na_q', NA_W), ('na_k', NA_W), ('na_v', NA_W), ('gq', GLA_QK_W), ('gk', GLA_QK_W),
                    ('gv', GLA_V_W), ('gr', GLA_V_W), ('gfl', GLA_RANK), ('gbl', GLA_RANK), ('c_q', MLA_QR),
                    ('c_kv', MLA_KVR), ('k_rope', MLA_ROPE), ('gates', 3 * d_model)]:
        own[name] = (off, w)
        off += w
    order = [('gates', 3 * d_model), ('na_q', NA_W), ('na_k', NA_W), ('na_v', NA_W), ('gv', GLA_V_W), ('gr', GLA_V_W),
             ('gq', GLA_QK_W), ('gk', GLA_QK_W), ('c_q', MLA_QR), ('c_kv', MLA_KVR), ('glow', LANE), ('k_rope', LANE)]
    lay = {}
    z = 0
    for name, w in order:
        assert z % w == 0, (name, z, w)
        if name == 'glow':
            lay[name] = (z, w, own['gfl'][0], 2 * GLA_RANK)
        else:
            lay[name] = (z, w, own[name][0], own[name][1])
        z += w
    return lay, z, off


def _z_cols_from_blocks(blocks, lay):
    cs = blocks.shape[2]
    parts = []
    for zo, zw, oo, ow in lay.values():
        a = oo
        while a < oo + ow:
            b = min(oo + ow, (a // cs + 1) * cs)
            parts.append(blocks[a // cs, :, a % cs:a % cs + (b - a)])
            a = b
        if zw > ow:
            parts.append(jnp.zeros((blocks.shape[1], zw - ow), blocks.dtype))
    return jnp.concatenate(parts, axis=1)


def _blocks_from_z_cols(wz, lay, d_in):
    cs = d_in // N_DEV
    own = sorted(lay.values(), key=lambda t: t[2])
    out = []
    for k in range(N_DEV):
        parts = []
        for zo, zw, oo, ow in own:
            a, b = max(oo, k * cs), min(oo + ow, (k + 1) * cs)
            if a < b:
                parts.append(wz[:, zo + a - oo:zo + b - oo])
        out.append(jnp.concatenate(parts, axis=1))
    return jnp.stack(out)


def _ffn_fwd(x, g, w1, w3, w2, name, exch=None):
    T, D = x.shape
    Fd = w1.shape[1]
    tm, tf = _tile(T, 1024), _tile(Fd, 256, LANE)
    nj = Fd // tf

    def body(x_ref, g_ref, w1_ref, w3_ref, w2_ref, o_ref, a_ref, b_ref, h_scr, acc):
        j = pl.program_id(1)

        @pl.when(j == 0)
        def _():
            xv = x_ref[...]
            h_scr[...] = (xv * _rstd(xv) * g_ref[...]).astype(BF16)
            acc[...] = jnp.zeros_like(acc)

        h = h_scr[...]
        a = _dot(h, w1_ref[...])
        b = _dot(h, w3_ref[...])
        a_ref[...] = a.astype(BF16)
        b_ref[...] = b.astype(BF16)
        u = a * _sigmoid(a) * b
        acc[...] += _dot(u.astype(BF16), w2_ref[...])

        @pl.when(j == nj - 1)
        def _():
            o_ref[...] = x_ref[...] + 0.5 * acc[...]

    return _call(
        body, name=name, grid=(T // tm, nj),
        in_specs=[pl.BlockSpec((tm, D), lambda i, j: (i, 0)), pl.BlockSpec((1, D), lambda i, j: (0, 0)),
                  pl.BlockSpec((D, tf), lambda i, j: (0, j)), pl.BlockSpec((D, tf), lambda i, j: (0, j)),
                  pl.BlockSpec((tf, D), lambda i, j: (j, 0))],
        out_specs=[pl.BlockSpec((tm, D), lambda i, j: (i, 0)), pl.BlockSpec((tm, tf), lambda i, j: (i, j)),
                   pl.BlockSpec((tm, tf), lambda i, j: (i, j))],
        out_shape=[SDS((T, D), F32), SDS((T, Fd), BF16), SDS((T, Fd), BF16)],
        scratch_shapes=[pltpu.VMEM((tm, D), BF16), pltpu.VMEM((tm, D), F32)],
        args=(x, g, w1, w3, w2), exch=exch)


def _ffn_bwd(x, g, dy, a, b, w1, w3, w2, name, exch=None):
    T, D = x.shape
    Fd = w1.shape[1]
    tm, tf = _tile(T, 1024), _tile(Fd, 256, LANE)
    nj = Fd // tf

    def body(x_ref, g_ref, dy_ref, a_ref, b_ref, w1_ref, w3_ref, w2_ref,
             dx_ref, da_ref, db_ref, u_ref, hb_ref, dyb_ref, dg_ref, dh_acc):
        i, j = pl.program_id(0), pl.program_id(1)

        @pl.when(j == 0)
        def _():
            xv = x_ref[...]
            hb_ref[...] = (xv * _rstd(xv) * g_ref[...]).astype(BF16)
            dyb_ref[...] = (0.5 * dy_ref[...]).astype(BF16)
            dh_acc[...] = jnp.zeros_like(dh_acc)

        @pl.when((i == 0) & (j == 0))
        def _():
            dg_ref[...] = jnp.zeros_like(dg_ref)

        du = _dot(dyb_ref[...], w2_ref[...], NT)
        av = a_ref[...].astype(F32)
        bv = b_ref[...].astype(F32)
        s = _sigmoid(av)
        sl = av * s
        da = (du * bv * (s * (1.0 + av * (1.0 - s)))).astype(BF16)
        db = (du * sl).astype(BF16)
        da_ref[...] = da
        db_ref[...] = db
        u_ref[...] = (sl * bv).astype(BF16)
        dh_acc[...] += _dot(da, w1_ref[...], NT) + _dot(db, w3_ref[...], NT)

        @pl.when(j == nj - 1)
        def _():
            xv = x_ref[...]
            dxn, dg = _rms_bwd(dh_acc[...], xv, g_ref[...], _rstd(xv))
            dx_ref[...] = dy_ref[...] + dxn
            dg_ref[...] += dg

    row = lambda i, j: (i, 0)
    return _call(
        body, name=name, grid=(T // tm, nj),
        in_specs=[pl.BlockSpec((tm, D), row), pl.BlockSpec((1, D), lambda i, j: (0, 0)), pl.BlockSpec((tm, D), row),
                  pl.BlockSpec((tm, tf), lambda i, j: (i, j)), pl.BlockSpec((tm, tf), lambda i, j: (i, j)),
                  pl.BlockSpec((D, tf), lambda i, j: (0, j)), pl.BlockSpec((D, tf), lambda i, j: (0, j)),
                  pl.BlockSpec((tf, D), lambda i, j: (j, 0))],
        out_specs=[pl.BlockSpec((tm, D), row), pl.BlockSpec((tm, tf), lambda i, j: (i, j)),
                   pl.BlockSpec((tm, tf), lambda i, j: (i, j)), pl.BlockSpec((tm, tf), lambda i, j: (i, j)),
                   pl.BlockSpec((tm, D), row), pl.BlockSpec((tm, D), row), pl.BlockSpec((1, D), lambda i, j: (0, 0))],
        out_shape=[SDS((T, D), F32), SDS((T, Fd), BF16), SDS((T, Fd), BF16), SDS((T, Fd), BF16),
                   SDS((T, D), BF16), SDS((T, D), BF16), SDS((1, D), F32)],
        scratch_shapes=[pltpu.VMEM((tm, D), F32)],
        args=(x, g, dy, a, b, w1, w3, w2), exch=exch)


def _matmul_tn(a, b, name, out_dtype=BF16, exch=None):
    T, K = a.shape
    N = b.shape[1]
    tk = _tile(K, 1408, LANE)
    tn = _tile(N, 1408, LANE)
    tt = _tile(T, 1024)
    nt = T // tt

    def body(a_ref, b_ref, o_ref, acc):
        t = pl.program_id(2)

        @pl.when(t == 0)
        def _():
            acc[...] = jnp.zeros_like(acc)

        acc[...] += _dot(a_ref[...], b_ref[...], TN)

        @pl.when(t == nt - 1)
        def _():
            o_ref[...] = acc[...].astype(out_dtype)

    res = _call(
        body, name=name, grid=(K // tk, N // tn, nt),
        in_specs=[pl.BlockSpec((tt, tk), lambda k, n, t: (t, k)), pl.BlockSpec((tt, tn), lambda k, n, t: (t, n))],
        out_specs=[pl.BlockSpec((tk, tn), lambda k, n, t: (k, n))],
        out_shape=[SDS((K, N), out_dtype)],
        scratch_shapes=[pltpu.VMEM((tk, tn), F32)],
        args=(a, b), exch=exch)
    return res[0] if exch is None else (res[0][0], res[1])


def _norm_matmul(x, xcol, g, w, name, exch=None):
    T = x.shape[0]
    D = g.shape[1]
    N = w.shape[1]
    tm, tn = _tile(T, 1024), _tile(N, 768, LANE)

    def body(x_ref, g_ref, w_ref, z_ref, hb_ref):
        @pl.when(pl.program_id(1) == 0)
        def _():
            xv = x_ref[...]
            hb_ref[...] = (xv * _rstd(xv) * g_ref[...]).astype(BF16)

        z_ref[...] = _dot(hb_ref[...], w_ref[...])

    return _call(
        body, name=name, grid=(T // tm, N // tn),
        in_specs=[pl.BlockSpec((tm, D), lambda i, j: (i, xcol)), pl.BlockSpec((1, D), lambda i, j: (0, 0)),
                  pl.BlockSpec((D, tn), lambda i, j: (0, j))],
        out_specs=[pl.BlockSpec((tm, tn), lambda i, j: (i, j)), pl.BlockSpec((tm, D), lambda i, j: (i, 0))],
        out_shape=[SDS((T, N), F32), SDS((T, D), BF16)],
        args=(x, g, w), exch=exch)


def _norm_matmul_bwd(dz, w, x, xcol, g, resid, name, exch=None):
    T, N = dz.shape
    D = g.shape[1]
    tm, tn = _tile(T, 1024), _tile(N, 768, LANE)
    nj = N // tn
    has_resid = resid is not None

    def body(*refs):
        if has_resid:
            dz_ref, w_ref, x_ref, g_ref, r_ref, dx_ref, dg_ref, acc = refs
        else:
            dz_ref, w_ref, x_ref, g_ref, dx_ref, dg_ref, acc = refs
        i, j = pl.program_id(0), pl.program_id(1)

        @pl.when(j == 0)
        def _():
            acc[...] = jnp.zeros_like(acc)

        @pl.when((i == 0) & (j == 0))
        def _():
            dg_ref[...] = jnp.zeros_like(dg_ref)

        acc[...] += _dot(dz_ref[...], w_ref[...], NT)

        @pl.when(j == nj - 1)
        def _():
            xv = x_ref[...]
            dxn, dg = _rms_bwd(acc[...], xv, g_ref[...], _rstd(xv))
            dx_ref[...] = dxn + r_ref[...] if has_resid else dxn
            dg_ref[...] += dg

    in_specs = [pl.BlockSpec((tm, tn), lambda i, j: (i, j)), pl.BlockSpec((D, tn), lambda i, j: (0, j)),
                pl.BlockSpec((tm, D), lambda i, j: (i, xcol)), pl.BlockSpec((1, D), lambda i, j: (0, 0))]
    args = [dz, w, x, g]
    if has_resid:
        in_specs.append(pl.BlockSpec((tm, D), lambda i, j: (i, 0)))
        args.append(resid)
    return _call(
        body, name=name, grid=(T // tm, nj), in_specs=in_specs,
        out_specs=[pl.BlockSpec((tm, D), lambda i, j: (i, 0)), pl.BlockSpec((1, D), lambda i, j: (0, 0))],
        out_shape=[SDS((T, D), F32), SDS((1, D), F32)],
        scratch_shapes=[pltpu.VMEM((tm, D), F32)],
        args=args, exch=exch)


def _na_bias_tables(rpb):
    L = rpb.shape[0]
    qc = np.arange(GRID_W)[:, None]
    kc = np.arange(GRID_W)[None, :]
    dc = np.clip(kc - qc + NA_WIN_C - 1, 0, 2 * NA_WIN_C - 2)
    c0 = np.clip(qc - NA_WIN_C // 2, 0, GRID_W - NA_WIN_C)
    ok = (kc >= c0) & (kc < c0 + NA_WIN_C)
    onehot = (dc.reshape(-1)[None, :] == np.arange(2 * NA_WIN_C - 1)[:, None]).astype(np.float32)
    t1 = jnp.stack([rpb[:, :, c:c + NA_WIN_R, :] for c in range(NA_WIN_R)], axis=2)
    full = jnp.dot(t1.reshape(-1, 2 * NA_WIN_C - 1), jnp.asarray(onehot), precision=HI)
    full = full.reshape(L, NA_HEADS, NA_WIN_R, NA_WIN_R, GRID_W, GRID_W)
    full = jnp.where(jnp.asarray(ok)[None, None, None, None], full, -1e30)
    return jnp.transpose(full, (0, 1, 2, 4, 3, 5)).reshape(L * NA_HEADS, NA_WIN_R, GRID_W, NA_WIN_R * GRID_W), onehot


def _na_bias_tables_bwd(db, onehot):
    d = db.reshape(NA_HEADS, NA_WIN_R, GRID_W, NA_WIN_R, GRID_W)
    d = jnp.transpose(d, (0, 1, 3, 2, 4)).reshape(-1, GRID_W * GRID_W)
    t1 = jnp.dot(d, jnp.asarray(onehot.T), precision=HI).reshape(NA_HEADS, NA_WIN_R, NA_WIN_R, 2 * NA_WIN_C - 1)
    out = jnp.zeros((NA_HEADS, 2 * NA_WIN_R - 1, 2 * NA_WIN_C - 1), F32)
    for c in range(NA_WIN_R):
        out = out.at[:, c:c + NA_WIN_R, :].add(t1[:, c])
    return out


def _na_windows(bi, rb, rows):
    wsl, cls = [], []
    for i in range(rb):
        r = bi * rb + i
        r0 = jnp.clip(r - NA_WIN_R // 2, 0, rows - NA_WIN_R)
        wsl.append(pl.ds(pl.multiple_of(r0 * GRID_W, GRID_W), NA_WIN_R * GRID_W))
        cls.append(r0 - r + NA_WIN_R - 1)
    return wsl, cls


def _na_fwd(q, k, v, bias, layer, gq, gk, name, exch=None):
    H, T, d = q.shape
    rows = T // GRID_W
    rb = _tile(rows, 8, 1)
    win = NA_WIN_R * GRID_W
    scale = d ** -0.5

    def body(q_ref, k_ref, v_ref, b_ref, gq_ref, gk_ref, o_ref, qn, kn, vb):
        qv, kv = q_ref[0], k_ref[0]
        qn[...] = (qv * _rstd(qv) * gq_ref[...] * scale).astype(BF16)
        kn[...] = (kv * _rstd(kv) * gk_ref[...]).astype(BF16)
        vb[...] = v_ref[0].astype(BF16)

        def block(bi, c):
            wsl, cls = _na_windows(bi, rb, rows)
            qsl = pl.ds(pl.multiple_of(bi * (rb * GRID_W), rb * GRID_W), rb * GRID_W)
            kw = jnp.stack([kn[w, :] for w in wsl])
            vw = jnp.stack([vb[w, :] for w in wsl])
            s = _bdot(qn[qsl, :].reshape(rb, GRID_W, d), kw, 2, 2) + jnp.stack([b_ref[0, c_] for c_ in cls])
            p = jnp.exp(s - jnp.max(s, axis=-1, keepdims=True))
            p = p / jnp.sum(p, axis=-1, keepdims=True)
            o_ref[0, qsl, :] = _bdot(p.astype(BF16), vw, 2, 1).reshape(rb * GRID_W, d)
            return c

        lax.fori_loop(0, rows // rb, block, 0)

    hm = pl.BlockSpec((1, T, d), lambda h: (h, 0, 0))
    gs = pl.BlockSpec((1, d), lambda h: (0, 0))
    return _call(
        body, name=name, grid=(H,),
        in_specs=[hm, hm, hm, pl.BlockSpec((1, NA_WIN_R, GRID_W, win), lambda h: (layer * H + h, 0, 0, 0)), gs, gs],
        out_specs=[hm], out_shape=[SDS((H, T, d), F32)],
        scratch_shapes=[pltpu.VMEM((T, d), BF16)] * 3,
        args=(q, k, v, bias, gq, gk), exch=exch)


def _na_bwd(q, k, v, do, bias, layer, gq, gk, name):
    H, T, d = q.shape
    rows = T // GRID_W
    rb = _tile(rows, 8, 1)
    win = NA_WIN_R * GRID_W
    scale = d ** -0.5

    def body(q_ref, k_ref, v_ref, do_ref, b_ref, gq_ref, gk_ref,
             dq_ref, dk_ref, dv_ref, db_ref, dgq_ref, dgk_ref, qn, kn, vb, dob, dqn, dkn):
        h = pl.program_id(0)
        qv, kv = q_ref[0], k_ref[0]
        rq, rk = _rstd(qv), _rstd(kv)
        qn[...] = (qv * rq * gq_ref[...] * scale).astype(BF16)
        kn[...] = (kv * rk * gk_ref[...]).astype(BF16)
        vb[...] = v_ref[0].astype(BF16)
        dob[...] = do_ref[0].astype(BF16)
        dkn[...] = jnp.zeros_like(dkn)
        dv_ref[...] = jnp.zeros_like(dv_ref)
        db_ref[...] = jnp.zeros_like(db_ref)

        @pl.when(h == 0)
        def _():
            dgq_ref[...] = jnp.zeros_like(dgq_ref)
            dgk_ref[...] = jnp.zeros_like(dgk_ref)

        def block(bi, c):
            wsl, cls = _na_windows(bi, rb, rows)
            qsl = pl.ds(pl.multiple_of(bi * (rb * GRID_W), rb * GRID_W), rb * GRID_W)
            qs = qn[qsl, :].reshape(rb, GRID_W, d)
            dos = dob[qsl, :].reshape(rb, GRID_W, d)
            kw = jnp.stack([kn[w, :] for w in wsl])
            vw = jnp.stack([vb[w, :] for w in wsl])
            s = _bdot(qs, kw, 2, 2) + jnp.stack([b_ref[0, c_] for c_ in cls])
            p = jnp.exp(s - jnp.max(s, axis=-1, keepdims=True))
            p = p / jnp.sum(p, axis=-1, keepdims=True)
            dp = _bdot(dos, vw, 2, 2)
            ds = p * (dp - jnp.sum(dp * p, axis=-1, keepdims=True))
            dsb = ds.astype(BF16)
            dqn[qsl, :] = _bdot(dsb, kw, 2, 1).reshape(rb * GRID_W, d)
            dvw = _bdot(p.astype(BF16), dos, 1, 1)
            dkw = _bdot(dsb, qs, 1, 1)
            for i in range(rb):
                db_ref[0, cls[i]] += ds[i]
                dv_ref[0, wsl[i], :] += dvw[i]
                dkn[wsl[i], :] += dkw[i]
            return c

        lax.fori_loop(0, rows // rb, block, 0)
        dq, dgq = _rms_bwd(dqn[...] * scale, qv, gq_ref[...], rq)
        dk, dgk = _rms_bwd(dkn[...], kv, gk_ref[...], rk)
        dq_ref[0] = dq
        dk_ref[0] = dk
        dgq_ref[...] += dgq
        dgk_ref[...] += dgk

    hm = pl.BlockSpec((1, T, d), lambda h: (h, 0, 0))
    gs = pl.BlockSpec((1, d), lambda h: (0, 0))
    bs = pl.BlockSpec((1, NA_WIN_R, GRID_W, win), lambda h: (h, 0, 0, 0))
    return pl.pallas_call(
        body, name=name, grid=(H,),
        in_specs=[hm, hm, hm, hm, pl.BlockSpec((1, NA_WIN_R, GRID_W, win), lambda h: (layer * H + h, 0, 0, 0)), gs, gs],
        out_specs=[hm, hm, hm, bs, gs, gs],
        out_shape=[SDS((H, T, d), F32)] * 3 + [SDS((H, NA_WIN_R, GRID_W, win), F32), SDS((1, d), F32), SDS((1, d), F32)],
        scratch_shapes=[pltpu.VMEM((T, d), BF16)] * 4 + [pltpu.VMEM((T, d), F32)] * 2,
        compiler_params=_params(1),
    )(q, k, v, do, bias, gq, gk)


def _gla_gate_fwd(z, col, upf, upb, bf, bb, name):
    T = z.shape[0]
    W = upf.shape[1]
    tm = _tile(T, 1024)

    def body(z_ref, upf_ref, upb_ref, bf_ref, bb_ref, g_ref):
        seg = z_ref[...].astype(BF16)
        for rev, (up_ref, b_ref) in enumerate(((upf_ref, bf_ref), (upb_ref, bb_ref))):
            pre = _dot(seg, up_ref[...]) + b_ref[...]
            g = (jnp.minimum(pre, 0.0) - jnp.log(1.0 + jnp.exp(-jnp.abs(pre)))) * (1.0 / GLA_TAU)
            for h in range(GLA_HEADS):
                g_ref[rev, h] = g[:, h * GLA_DK:(h + 1) * GLA_DK]

    full = lambda shape: pl.BlockSpec(shape, lambda i: (0, 0))
    return pl.pallas_call(
        body, name=name, grid=(T // tm,),
        in_specs=[pl.BlockSpec((tm, LANE), lambda i: (i, col)), full((LANE, W)), full((LANE, W)), full((1, W)), full((1, W))],
        out_specs=pl.BlockSpec((2, GLA_HEADS, tm, GLA_DK), lambda i: (0, 0, i, 0)),
        out_shape=SDS((2, GLA_HEADS, T, GLA_DK), F32),
        compiler_params=_params(1),
    )(z, upf, upb, bf, bb)


def _gla_gate_bwd(z, col, upf, upb, bf, bb, dg, name):
    T = z.shape[0]
    W = upf.shape[1]
    tm = _tile(T, 1024)

    def body(z_ref, upf_ref, upb_ref, bf_ref, bb_ref, dg_ref, dseg_ref, dupf_ref, dupb_ref, dbf_ref, dbb_ref):
        @pl.when(pl.program_id(0) == 0)
        def _():
            for r in (dupf_ref, dupb_ref, dbf_ref, dbb_ref):
                r[...] = jnp.zeros_like(r)

        seg = z_ref[...].astype(BF16)
        dseg = jnp.zeros(dseg_ref.shape, F32)
        for rev, (up_ref, b_ref, dup_ref, db_ref) in enumerate(((upf_ref, bf_ref, dupf_ref, dbf_ref),
                                                               (upb_ref, bb_ref, dupb_ref, dbb_ref))):
            pre = _dot(seg, up_ref[...]) + b_ref[...]
            dgv = jnp.concatenate([dg_ref[rev, h] for h in range(GLA_HEADS)], axis=1)
            dpre = dgv * (1.0 / GLA_TAU) * (1.0 - _sigmoid(pre))
            dpb = dpre.astype(BF16)
            dseg = dseg + _dot(dpb, up_ref[...], NT)
            dup_ref[...] += _dot(seg, dpb, TN)
            db_ref[...] += jnp.sum(dpre, axis=0, keepdims=True)
        dseg_ref[...] = dseg

    full = lambda shape: pl.BlockSpec(shape, lambda i: (0, 0))
    return pl.pallas_call(
        body, name=name, grid=(T // tm,),
        in_specs=[pl.BlockSpec((tm, LANE), lambda i: (i, col)), full((LANE, W)), full((LANE, W)), full((1, W)), full((1, W)),
                  pl.BlockSpec((2, GLA_HEADS, tm, GLA_DK), lambda i: (0, 0, i, 0))],
        out_specs=[pl.BlockSpec((tm, LANE), lambda i: (i, 0)), full((LANE, W)), full((LANE, W)), full((1, W)), full((1, W))],
        out_shape=[SDS((T, LANE), F32), SDS((LANE, W), F32), SDS((LANE, W), F32), SDS((1, W), F32), SDS((1, W), F32)],
        compiler_params=_params(1),
    )(z, upf, upb, bf, bb, dg)


def _bdot(a, b, ca, cb, precision=None):
    return lax.dot_general(a, b, (((ca,), (cb,)), ((0,), (0,))), preferred_element_type=F32, precision=precision)


def _gla_chunk_terms(q_ref, k_ref, v_ref, g_ref, rev, tok, n, C):
    dk, dv = q_ref.shape[-1], v_ref.shape[-1]
    q = q_ref[0, tok, :].reshape(n, C, dk) * (dk ** -0.5)
    k = k_ref[0, tok, :].reshape(n, C, dk)
    v = v_ref[0, tok, :].reshape(n, C, dv)
    g = g_ref[rev, 0, tok, :].reshape(n, C, dk)
    ii = lax.broadcasted_iota(jnp.int32, (C, C), 0)
    jj = lax.broadcasted_iota(jnp.int32, (C, C), 1)
    tri = jnp.broadcast_to(((ii <= jj) if rev else (ii >= jj)).astype(F32), (n, C, C))
    amask = ((jj > ii) if rev else (ii >= jj))[None]
    b = _bdot(tri, g, 2, 1, HI)
    bl = jnp.sum(g, axis=1, keepdims=True)
    eb = jnp.exp(b)
    enb = jnp.exp(-b)
    eend = jnp.exp(bl - b)
    return q, k, v, tri, amask, bl, eb, enb, eend


def _gla_segments(T, C):
    n = T // C
    ns = _tile(n, 16, 1)

    def tok(s):
        return pl.ds(pl.multiple_of(s * (ns * C), ns * C), ns * C)

    def chunks(s):
        return pl.ds(pl.multiple_of(s * ns, ns), ns)

    return n, ns, n // ns, tok, chunks


def _gla_fwd(q, k, v, g, name):
    H, T, dk = q.shape
    dv = v.shape[-1]
    C = GLA_CHUNK
    n, ns, nseg, seg_tok, seg_chunks = _gla_segments(T, C)

    def body(q_ref, k_ref, v_ref, g_ref, o_ref, upd_scr, dec_scr, st_scr):
        for rev in (0, 1):
            def intra(s, c):
                tok, ch = seg_tok(s), seg_chunks(s)
                q, k, v, tri, amask, bl, eb, enb, eend = _gla_chunk_terms(q_ref, k_ref, v_ref, g_ref, rev, tok, ns, C)
                vb = v.astype(BF16)
                a = jnp.where(amask, _bdot((q * eb).astype(BF16), (k * enb).astype(BF16), 2, 2), 0.0).astype(BF16)
                o = _bdot(a, vb, 2, 1).reshape(ns * C, dv)
                if rev:
                    o_ref[0, tok, :] += o
                else:
                    o_ref[0, tok, :] = o
                upd_scr[ch] = _bdot(vb, (k * eend).astype(BF16), 1, 1)
                dec_scr[ch] = jnp.exp(bl)
                return c

            lax.fori_loop(0, nseg, intra, 0)

            def step(t, st):
                i = n - 1 - t if rev else t
                st_scr[i] = st
                return st * dec_scr[i] + upd_scr[i]

            lax.fori_loop(0, n, step, jnp.zeros((dv, dk), F32))

            def inter(s, c):
                tok, ch = seg_tok(s), seg_chunks(s)
                q, k, v, tri, amask, bl, eb, enb, eend = _gla_chunk_terms(q_ref, k_ref, v_ref, g_ref, rev, tok, ns, C)
                o_ref[0, tok, :] += _bdot((q * eb).astype(BF16), st_scr[ch].astype(BF16), 2, 2).reshape(ns * C, dv)
                return c

            lax.fori_loop(0, nseg, inter, 0)

    sk = pl.BlockSpec((1, T, dk), lambda h: (h, 0, 0))
    sv = pl.BlockSpec((1, T, dv), lambda h: (h, 0, 0))
    sg = pl.BlockSpec((2, 1, T, dk), lambda h: (0, h, 0, 0))
    return pl.pallas_call(
        body, name=name, grid=(H,), in_specs=[sk, sk, sv, sg], out_specs=sv,
        out_shape=SDS((H, T, dv), F32),
        scratch_shapes=[pltpu.VMEM((n, dv, dk), F32), pltpu.VMEM((n, 1, dk), F32), pltpu.VMEM((n, dv, dk), F32)],
        compiler_params=_params(1),
    )(q, k, v, g)


def _gla_bwd(q, k, v, g, do, name):
    H, T, dk = q.shape
    dv = v.shape[-1]
    C = GLA_CHUNK
    n, ns, nseg, seg_tok, seg_chunks = _gla_segments(T, C)

    def one_scan(rev, q_ref, k_ref, v_ref, g_ref, do_ref, dq_ref, dk_ref, dv_ref, dg_ref, upd_scr, dec_scr, st_scr, gn_scr,
                 rn_scr):
        def first(s, c):
            tok, ch = seg_tok(s), seg_chunks(s)
            q, k, v, tri, amask, bl, eb, enb, eend = _gla_chunk_terms(q_ref, k_ref, v_ref, g_ref, rev, tok, ns, C)
            dob = do_ref[0, tok, :].reshape(ns, C, dv).astype(BF16)
            upd_scr[ch] = _bdot(v.astype(BF16), (k * eend).astype(BF16), 1, 1)
            dec_scr[ch] = jnp.exp(bl)
            gn_scr[ch] = _bdot(dob, (q * eb).astype(BF16), 1, 1)
            return c

        lax.fori_loop(0, nseg, first, 0)

        def fstep(t, st):
            i = n - 1 - t if rev else t
            st_scr[i] = st
            return st * dec_scr[i] + upd_scr[i]

        lax.fori_loop(0, n, fstep, jnp.zeros((dv, dk), F32))

        def bstep(t, r):
            i = t if rev else n - 1 - t
            rn_scr[i] = r
            return gn_scr[i] + r * dec_scr[i]

        lax.fori_loop(0, n, bstep, jnp.zeros((dv, dk), F32))

        def second(s, c):
            tok, ch = seg_tok(s), seg_chunks(s)
            q, k, v, tri, amask, bl, eb, enb, eend = _gla_chunk_terms(q_ref, k_ref, v_ref, g_ref, rev, tok, ns, C)
            qe, ke, kend = q * eb, k * enb, k * eend
            qeb, keb, kendb, vb = qe.astype(BF16), ke.astype(BF16), kend.astype(BF16), v.astype(BF16)
            dob = do_ref[0, tok, :].reshape(ns, C, dv).astype(BF16)
            a = jnp.where(amask, _bdot(qeb, keb, 2, 2), 0.0).astype(BF16)
            st = st_scr[ch]
            rn = rn_scr[ch]
            rnb = rn.astype(BF16)
            da = jnp.where(amask, _bdot(dob, vb, 2, 2), 0.0).astype(BF16)
            dvv = _bdot(a, dob, 1, 1) + _bdot(kendb, rnb, 2, 2)
            dqe = _bdot(da, keb, 2, 1) + _bdot(dob, st.astype(BF16), 2, 1)
            dke = _bdot(da, qeb, 1, 1)
            dkend = _bdot(vb, rnb, 2, 1)
            ddec = jnp.sum(rn * st, axis=1, keepdims=True)
            dbl = ddec * jnp.exp(bl) + jnp.sum(dkend * kend, axis=1, keepdims=True)
            db = dqe * qe - dke * ke - dkend * kend
            dg = _bdot(tri, db, 1, 1, HI) + dbl
            dqv = (dqe * eb * (dk ** -0.5)).reshape(ns * C, dk)
            dkv = (dke * enb + dkend * eend).reshape(ns * C, dk)
            if rev:
                dq_ref[0, tok, :] += dqv
                dk_ref[0, tok, :] += dkv
                dv_ref[0, tok, :] += dvv.reshape(ns * C, dv)
            else:
                dq_ref[0, tok, :] = dqv
                dk_ref[0, tok, :] = dkv
                dv_ref[0, tok, :] = dvv.reshape(ns * C, dv)
            dg_ref[rev, 0, tok, :] = dg.reshape(ns * C, dk)
            return c

        lax.fori_loop(0, nseg, second, 0)

    def body(*refs):
        one_scan(0, *refs)
        one_scan(1, *refs)

    sk = pl.BlockSpec((1, T, dk), lambda h: (h, 0, 0), pipeline_mode=pl.Buffered(1))
    sv = pl.BlockSpec((1, T, dv), lambda h: (h, 0, 0), pipeline_mode=pl.Buffered(1))
    sg = pl.BlockSpec((2, 1, T, dk), lambda h: (0, h, 0, 0), pipeline_mode=pl.Buffered(1))
    st_shape = pltpu.VMEM((n, dv, dk), F32)
    return pl.pallas_call(
        body, name=name, grid=(H,), in_specs=[sk, sk, sv, sg, sv], out_specs=[sk, sk, sv, sg],
        out_shape=[SDS((H, T, dk), F32), SDS((H, T, dk), F32), SDS((H, T, dv), F32), SDS((2, H, T, dk), F32)],
        scratch_shapes=[st_shape, pltpu.VMEM((n, 1, dk), F32), st_shape, st_shape, st_shape],
        compiler_params=_params(1),
    )(q, k, v, g, do)


def _rope_tables(T):
    half = MLA_ROPE // 2
    inv = ROPE_THETA ** (-jnp.arange(half, dtype=F32) / half)
    ang = jnp.arange(T, dtype=F32)[:, None] * inv[None, :]
    cos, sin = jnp.cos(ang), jnp.sin(ang)
    ctab = jnp.concatenate([jnp.ones((T, MLA_NOPE), F32), cos, cos], axis=1)
    stab = jnp.concatenate([jnp.zeros((T, MLA_NOPE), F32), -sin, sin], axis=1)
    return ctab, stab


def _swap_rope_halves(x):
    half = MLA_ROPE // 2
    return jnp.concatenate([x[:, :MLA_NOPE], x[:, MLA_NOPE + half:], x[:, MLA_NOPE:MLA_NOPE + half]], axis=1)


def _norm_rope(x, g, c, s):
    xn = x * _rstd(x) * g
    return xn * c + _swap_rope_halves(xn) * s


def _norm_rope_bwd(dy, x, g, c, s):
    return _rms_bwd(dy * c + _swap_rope_halves(dy * s), x, g, _rstd(x))


def _mla_qprep_fwd(q_raw, g, ctab, stab, name):
    T, d = q_raw.shape[0], MLA_QK
    H = q_raw.shape[1] // d
    tm = _tile(T, 1024)

    def body(x_ref, g_ref, c_ref, s_ref, o_ref):
        for h in range(H):
            o_ref[h] = _norm_rope(x_ref[:, h * d:(h + 1) * d], g_ref[...], c_ref[...], s_ref[...]).astype(BF16)

    tab = pl.BlockSpec((tm, d), lambda i: (i, 0))
    return pl.pallas_call(
        body, name=name, grid=(T // tm,),
        in_specs=[pl.BlockSpec((tm, H * d), lambda i: (i, 0)), pl.BlockSpec((1, d), lambda i: (0, 0)), tab, tab],
        out_specs=pl.BlockSpec((H, tm, d), lambda i: (0, i, 0)), out_shape=SDS((H, T, d), BF16),
        compiler_params=_params(1),
    )(q_raw, g, ctab, stab)


def _mla_qprep_bwd(q_raw, g, ctab, stab, dy, name):
    T, d = q_raw.shape[0], MLA_QK
    H = q_raw.shape[1] // d
    tm = _tile(T, 1024)

    def body(x_ref, g_ref, c_ref, s_ref, dy_ref, dx_ref, dg_ref):
        @pl.when(pl.program_id(0) == 0)
        def _():
            dg_ref[...] = jnp.zeros_like(dg_ref)

        dg = jnp.zeros((1, d), F32)
        for h in range(H):
            dx, dgh = _norm_rope_bwd(dy_ref[h], x_ref[:, h * d:(h + 1) * d], g_ref[...], c_ref[...], s_ref[...])
            dx_ref[:, h * d:(h + 1) * d] = dx.astype(BF16)
            dg = dg + dgh
        dg_ref[...] += dg

    tab = pl.BlockSpec((tm, d), lambda i: (i, 0))
    tok = pl.BlockSpec((tm, H * d), lambda i: (i, 0))
    gs = pl.BlockSpec((1, d), lambda i: (0, 0))
    return pl.pallas_call(
        body, name=name, grid=(T // tm,),
        in_specs=[tok, gs, tab, tab, pl.BlockSpec((H, tm, d), lambda i: (0, i, 0))],
        out_specs=[tok, gs], out_shape=[SDS((T, H * d), BF16), SDS((1, d), F32)],
        compiler_params=_params(1),
    )(q_raw, g, ctab, stab, dy)


def _mla_kprep_fwd(kv_raw, z, kr_col, g, ctab, stab, name):
    T, d, hw = kv_raw.shape[0], MLA_QK, MLA_NOPE + MLA_V
    H = kv_raw.shape[1] // hw
    tm = _tile(T, 1024)

    def body(kv_ref, kr_ref, g_ref, c_ref, s_ref, k_ref, v_ref):
        kr = kr_ref[:, :MLA_ROPE]
        for h in range(H):
            x = jnp.concatenate([kv_ref[:, h * hw:h * hw + MLA_NOPE], kr], axis=1)
            k_ref[h] = _norm_rope(x, g_ref[...], c_ref[...], s_ref[...]).astype(BF16)
            v_ref[h] = kv_ref[:, h * hw + MLA_NOPE:(h + 1) * hw].astype(BF16)

    tab = pl.BlockSpec((tm, d), lambda i: (i, 0))
    return pl.pallas_call(
        body, name=name, grid=(T // tm,),
        in_specs=[pl.BlockSpec((tm, H * hw), lambda i: (i, 0)), pl.BlockSpec((tm, LANE), lambda i: (i, kr_col)),
                  pl.BlockSpec((1, d), lambda i: (0, 0)), tab, tab],
        out_specs=[pl.BlockSpec((H, tm, d), lambda i: (0, i, 0)), pl.BlockSpec((H, tm, MLA_V), lambda i: (0, i, 0))],
        out_shape=[SDS((H, T, d), BF16), SDS((H, T, MLA_V), BF16)],
        compiler_params=_params(1),
    )(kv_raw, z, g, ctab, stab)


def _mla_kprep_bwd(kv_raw, z, kr_col, g, ctab, stab, dk, dv, name):
    T, d, hw = kv_raw.shape[0], MLA_QK, MLA_NOPE + MLA_V
    H = kv_raw.shape[1] // hw
    tm = _tile(T, 1024)

    def body(kv_ref, kr_ref, g_ref, c_ref, s_ref, dk_ref, dv_ref, dkv_ref, dkr_ref, dg_ref):
        @pl.when(pl.program_id(0) == 0)
        def _():
            dg_ref[...] = jnp.zeros_like(dg_ref)

        kr = kr_ref[:, :MLA_ROPE]
        dg = jnp.zeros((1, d), F32)
        dkr = jnp.zeros((tm, MLA_ROPE), F32)
        for h in range(H):
            x = jnp.concatenate([kv_ref[:, h * hw:h * hw + MLA_NOPE], kr], axis=1)
            dx, dgh = _norm_rope_bwd(dk_ref[h], x, g_ref[...], c_ref[...], s_ref[...])
            dkv_ref[:, h * hw:h * hw + MLA_NOPE] = dx[:, :MLA_NOPE].astype(BF16)
            dkv_ref[:, h * hw + MLA_NOPE:(h + 1) * hw] = dv_ref[h].astype(BF16)
            dkr = dkr + dx[:, MLA_NOPE:]
            dg = dg + dgh
        dkr_ref[...] = jnp.concatenate([dkr, jnp.zeros((tm, LANE - MLA_ROPE), F32)], axis=1)
        dg_ref[...] += dg

    tab = pl.BlockSpec((tm, d), lambda i: (i, 0))
    tok = pl.BlockSpec((tm, H * hw), lambda i: (i, 0))
    gs = pl.BlockSpec((1, d), lambda i: (0, 0))
    return pl.pallas_call(
        body, name=name, grid=(T // tm,),
        in_specs=[tok, pl.BlockSpec((tm, LANE), lambda i: (i, kr_col)), gs, tab, tab,
                  pl.BlockSpec((H, tm, d), lambda i: (0, i, 0)), pl.BlockSpec((H, tm, MLA_V), lambda i: (0, i, 0))],
        out_specs=[tok, pl.BlockSpec((tm, LANE), lambda i: (i, 0)), gs],
        out_shape=[SDS((T, H * hw), BF16), SDS((T, LANE), F32), SDS((1, d), F32)],
        compiler_params=_params(1),
    )(kv_raw, z, g, ctab, stab, dk, dv)


def _mla_attn_fwd(q, k, v, name, exch=None):
    H, T, d = q.shape
    dv = v.shape[-1]
    tq = _tile(T, 256)
    scale = d ** -0.5

    def body(q_ref, k_ref, v_ref, o_ref, lse_ref):
        s = _dot(q_ref[0], k_ref[0], NT) * scale
        m = jnp.max(s, axis=-1, keepdims=True)
        p = jnp.exp(s - m)
        l = jnp.sum(p, axis=-1, keepdims=True)
        o_ref[0] = _dot((p / l).astype(BF16), v_ref[0])
        lse_ref[0] = _col_to_row(m + jnp.log(l))

    return _call(
        body, name=name, grid=(H, T // tq),
        in_specs=[pl.BlockSpec((1, tq, d), lambda h, i: (h, i, 0)), pl.BlockSpec((1, T, d), lambda h, i: (h, 0, 0)),
                  pl.BlockSpec((1, T, dv), lambda h, i: (h, 0, 0))],
        out_specs=[pl.BlockSpec((1, tq, dv), lambda h, i: (h, i, 0)), pl.BlockSpec((1, 1, tq), lambda h, i: (h, 0, i))],
        out_shape=[SDS((H, T, dv), F32), SDS((H, 1, T), F32)],
        args=(q, k, v), exch=exch)


def _col_to_row(col):
    m = col.shape[0]
    eye = lax.broadcasted_iota(jnp.int32, (m, m), 0) == lax.broadcasted_iota(jnp.int32, (m, m), 1)
    return jnp.sum(jnp.where(eye, col, 0.0), axis=0, keepdims=True)


def _mla_attn_bwd(q, k, v, o, lse, do, name, exch=None):
    H, T, d = q.shape
    dv = v.shape[-1]
    tq = _tile(T, 256)
    scale = d ** -0.5

    def body(q_ref, k_ref, v_ref, o_ref, lse_ref, do_ref, dq_ref, dk_ref, dv_ref):
        @pl.when(pl.program_id(1) == 0)
        def _():
            dk_ref[...] = jnp.zeros_like(dk_ref)
            dv_ref[...] = jnp.zeros_like(dv_ref)

        qv, kv = q_ref[0], k_ref[0]
        dov = do_ref[0]
        dob = dov.astype(BF16)
        delta_row = _col_to_row(jnp.sum(dov * o_ref[0], axis=-1, keepdims=True))
        pt = jnp.exp(_dot(kv, qv, NT) * scale - lse_ref[0])
        dst = (pt * (_dot(v_ref[0], dob, NT) - delta_row) * scale).astype(BF16)
        dv_ref[0] += _dot(pt.astype(BF16), dob)
        dk_ref[0] += _dot(dst, qv)
        dq_ref[0] = _dot(dst, kv, TN)

    qb = pl.BlockSpec((1, tq, d), lambda h, i: (h, i, 0))
    kb = pl.BlockSpec((1, T, d), lambda h, i: (h, 0, 0))
    vb = pl.BlockSpec((1, T, dv), lambda h, i: (h, 0, 0))
    ob = pl.BlockSpec((1, tq, dv), lambda h, i: (h, i, 0))
    return _call(
        body, name=name, grid=(H, T // tq),
        in_specs=[qb, kb, vb, ob, pl.BlockSpec((1, 1, tq), lambda h, i: (h, 0, i)), ob],
        out_specs=[qb, kb, vb],
        out_shape=[SDS((H, T, d), F32), SDS((H, T, d), F32), SDS((H, T, dv), F32)],
        args=(q, k, v, o, lse, do), exch=exch)


def _silu_parts(x):
    s = _sigmoid(x)
    return x * s, s * (1.0 + x * (1.0 - s))


def _mix_out_fwd(x1, y_na, o_gla, y_mla, z, gr_col, gnorm, wbr, w_out, name):
    T, D = x1.shape
    W = y_na.shape[0] * y_na.shape[2]
    tm = _tile(T, 512)

    def body(x_ref, na_ref, gla_ref, mla_ref, gt_ref, gr_ref, gn_ref, w0, w1, w2, wo_ref, o_ref):
        gn = gn_ref[...]
        nrm = jnp.concatenate([gla_ref[h] * _rstd(gla_ref[h]) * gn for h in range(GLA_HEADS)], axis=1)
        y_gla = nrm * _silu_parts(gr_ref[...])[0]
        y_na = jnp.concatenate([na_ref[h] for h in range(NA_HEADS)], axis=1)
        y_mla = jnp.concatenate([mla_ref[h] for h in range(MLA_HEADS)], axis=1)
        mixed = jnp.zeros((tm, D), F32)
        for i, (y, w_ref) in enumerate(((y_na, w0), (y_gla, w1), (y_mla, w2))):
            mixed = mixed + _sigmoid(gt_ref[:, i * D:(i + 1) * D]) * _dot(y.astype(BF16), w_ref[...])
        o_ref[...] = x_ref[...] + _dot(mixed.astype(BF16), wo_ref[...])

    tok = lambda w: pl.BlockSpec((tm, w), lambda i: (i, 0))
    hm = lambda a: pl.BlockSpec((a.shape[0], tm, a.shape[2]), lambda i: (0, i, 0))
    full = lambda shape: pl.BlockSpec(shape, lambda i: (0, 0))
    return pl.pallas_call(
        body, name=name, grid=(T // tm,),
        in_specs=[tok(D), hm(y_na), hm(o_gla), hm(y_mla), tok(3 * D), pl.BlockSpec((tm, W), lambda i: (i, gr_col)),
                  full((1, GLA_DV)), full((W, D)), full((W, D)), full((W, D)), full((D, D))],
        out_specs=tok(D), out_shape=SDS((T, D), F32),
        compiler_params=_params(1),
    )(x1, y_na, o_gla, y_mla, z, z, gnorm, wbr[0], wbr[1], wbr[2], w_out)


def _mix_out_bwd(dx2, y_na, o_gla, y_mla, z, gr_col, gnorm, wbr, w_out, name):
    T, D = dx2.shape
    W = y_na.shape[0] * y_na.shape[2]
    tm = _tile(T, 256)

    def body(dx_ref, na_ref, gla_ref, mla_ref, gt_ref, gr_ref, gn_ref, w0, w1, w2, wo_ref,
             dna_ref, dgla_ref, dmla_ref, dgr_ref, dgt_ref, dw0, dw1, dw2, dwo_ref, dgn_ref):
        @pl.when(pl.program_id(0) == 0)
        def _():
            for r in (dw0, dw1, dw2, dwo_ref, dgn_ref):
                r[...] = jnp.zeros_like(r)

        gn = gn_ref[...]
        grv = gr_ref[...]
        sil, dsil = _silu_parts(grv)
        rs = [_rstd(gla_ref[h]) for h in range(GLA_HEADS)]
        nrm = jnp.concatenate([gla_ref[h] * rs[h] * gn for h in range(GLA_HEADS)], axis=1)
        ys = (jnp.concatenate([na_ref[h] for h in range(NA_HEADS)], axis=1).astype(BF16), (nrm * sil).astype(BF16),
              jnp.concatenate([mla_ref[h] for h in range(MLA_HEADS)], axis=1).astype(BF16))
        dxb = dx_ref[...].astype(BF16)
        dmixed = _dot(dxb, wo_ref[...], NT)
        mixed = jnp.zeros((tm, D), F32)
        dys = []
        for i, (y, w_ref, dw_ref) in enumerate(zip(ys, (w0, w1, w2), (dw0, dw1, dw2))):
            gt = _sigmoid(gt_ref[:, i * D:(i + 1) * D])
            p = _dot(y, w_ref[...])
            mixed = mixed + gt * p
            dp = (dmixed * gt).astype(BF16)
            dgt_ref[:, i * D:(i + 1) * D] = (dmixed * p * gt * (1.0 - gt)).astype(BF16)
            dys.append(_dot(dp, w_ref[...], NT))
            dw_ref[...] += _dot(y, dp, TN)
        dwo_ref[...] += _dot(mixed.astype(BF16), dxb, TN)
        for h in range(NA_HEADS):
            dna_ref[h] = dys[0][:, h * NA_HD:(h + 1) * NA_HD]
        for h in range(MLA_HEADS):
            dmla_ref[h] = dys[2][:, h * MLA_V:(h + 1) * MLA_V]
        dgr_ref[...] = dys[1] * nrm * dsil
        dn = dys[1] * sil
        dgn = jnp.zeros((1, GLA_DV), F32)
        for h in range(GLA_HEADS):
            dxh, dgh = _rms_bwd(dn[:, h * GLA_DV:(h + 1) * GLA_DV], gla_ref[h], gn, rs[h])
            dgla_ref[h] = dxh
            dgn = dgn + dgh
        dgn_ref[...] += dgn

    tok = lambda w: pl.BlockSpec((tm, w), lambda i: (i, 0))
    hm = lambda a: pl.BlockSpec((a.shape[0], tm, a.shape[2]), lambda i: (0, i, 0))
    full = lambda shape: pl.BlockSpec(shape, lambda i: (0, 0))
    return pl.pallas_call(
        body, name=name, grid=(T // tm,),
        in_specs=[tok(D), hm(y_na), hm(o_gla), hm(y_mla), tok(3 * D), pl.BlockSpec((tm, W), lambda i: (i, gr_col)),
                  full((1, GLA_DV)), full((W, D)), full((W, D)), full((W, D)), full((D, D))],
        out_specs=[hm(y_na), hm(o_gla), hm(y_mla), tok(W), tok(3 * D), full((W, D)), full((W, D)), full((W, D)),
                   full((D, D)), full((1, GLA_DV))],
        out_shape=[SDS(y_na.shape, F32), SDS(o_gla.shape, F32), SDS(y_mla.shape, F32), SDS((T, W), F32),
                   SDS((T, 3 * D), BF16)] + [SDS((W, D), F32)] * 3 + [SDS((D, D), F32), SDS((1, GLA_DV), F32)],
        compiler_params=_params(1),
    )(dx2, y_na, o_gla, y_mla, z, z, gnorm, wbr[0], wbr[1], wbr[2], w_out)


def _split_z(z, lay, name):
    T = z.shape[0]
    tm = _tile(T, 512)
    na0, gv0, gq0 = lay['na_q'][0], lay['gv'][0], lay['gq'][0]
    assert na0 % (3 * NA_W) == 0 and gv0 % GLA_V_W == 0 and gq0 % (2 * GLA_QK_W) == 0
    assert lay['na_k'][0] == na0 + NA_W and lay['na_v'][0] == na0 + 2 * NA_W and lay['gk'][0] == gq0 + GLA_QK_W

    def body(na_ref, gv_ref, gqk_ref, q_ref, k_ref, v_ref, gq_ref, gk_ref, gvo_ref):
        for j, o_ref in enumerate((q_ref, k_ref, v_ref)):
            for h in range(NA_HEADS):
                o_ref[h] = na_ref[:, j * NA_W + h * NA_HD:j * NA_W + (h + 1) * NA_HD]
        for j, o_ref in enumerate((gq_ref, gk_ref)):
            for h in range(GLA_HEADS):
                o_ref[h] = gqk_ref[:, j * GLA_QK_W + h * GLA_DK:j * GLA_QK_W + (h + 1) * GLA_DK]
        for h in range(GLA_HEADS):
            gvo_ref[h] = gv_ref[:, h * GLA_DV:(h + 1) * GLA_DV]

    hm = lambda H, d: pl.BlockSpec((H, tm, d), lambda i: (0, i, 0))
    return pl.pallas_call(
        body, name=name, grid=(T // tm,),
        in_specs=[pl.BlockSpec((tm, 3 * NA_W), lambda i: (i, na0 // (3 * NA_W))),
                  pl.BlockSpec((tm, GLA_V_W), lambda i: (i, gv0 // GLA_V_W)),
                  pl.BlockSpec((tm, 2 * GLA_QK_W), lambda i: (i, gq0 // (2 * GLA_QK_W)))],
        out_specs=[hm(NA_HEADS, NA_HD)] * 3 + [hm(GLA_HEADS, GLA_DK)] * 2 + [hm(GLA_HEADS, GLA_DV)],
        out_shape=[SDS((NA_HEADS, T, NA_HD), F32)] * 3 + [SDS((GLA_HEADS, T, GLA_DK), F32)] * 2
        + [SDS((GLA_HEADS, T, GLA_DV), F32)],
        compiler_params=_params(1),
    )(z, z, z)


def _assemble_dz(dgates, na_dqkv, gla_dqk, gla_dv, dgr, dcq, dckv, dglow, dkr, lay, zp, name):
    T = dgr.shape[0]
    tm = _tile(T, 256)

    def body(dgt_ref, nq_ref, nk_ref, nv_ref, gq_ref, gk_ref, gv_ref, dgr_ref, dcq_ref, dckv_ref, dglow_ref, dkr_ref, dz_ref):
        def put(seg, off, val):
            a = lay[seg][0] + off
            dz_ref[:, a:a + val.shape[1]] = val.astype(BF16)

        put('gates', 0, dgt_ref[...])
        for seg, ref in (('na_q', nq_ref), ('na_k', nk_ref), ('na_v', nv_ref)):
            for h in range(NA_HEADS):
                put(seg, h * NA_HD, ref[h])
        for seg, ref in (('gq', gq_ref), ('gk', gk_ref)):
            for h in range(GLA_HEADS):
                put(seg, h * GLA_DK, ref[h])
        for h in range(GLA_HEADS):
            put('gv', h * GLA_DV, gv_ref[h])
        put('gr', 0, dgr_ref[...])
        put('c_q', 0, dcq_ref[...])
        put('c_kv', 0, dckv_ref[...])
        put('glow', 0, dglow_ref[...])
        put('k_rope', 0, dkr_ref[...])

    tok = lambda a: pl.BlockSpec((tm, a.shape[1]), lambda i: (i, 0))
    hm = lambda a: pl.BlockSpec((a.shape[0], tm, a.shape[2]), lambda i: (0, i, 0))
    args = (dgates, *na_dqkv, *gla_dqk, gla_dv, dgr, dcq, dckv, dglow, dkr)
    return pl.pallas_call(
        body, name=name, grid=(T // tm,),
        in_specs=[tok(dgates)] + [hm(a) for a in (*na_dqkv, *gla_dqk, gla_dv)] + [tok(dgr), tok(dcq), tok(dckv), tok(dglow),
                                                                                 tok(dkr)],
        out_specs=pl.BlockSpec((tm, zp), lambda i: (i, 0)), out_shape=SDS((T, zp), BF16),
        compiler_params=_params(1),
    )(*args)


def _loss_head(y, target, name):
    T, D = y.shape
    tm = _tile(T, 1024)

    def body(y_ref, t_ref, dy_ref, l_ref):
        @pl.when(pl.program_id(0) == 0)
        def _():
            l_ref[...] = jnp.zeros_like(l_ref)

        e = y_ref[...] - t_ref[...]
        dy_ref[...] = e * (1.0 / D)
        l_ref[...] += 0.5 * jnp.sum(jnp.mean(e * e, axis=-1, keepdims=True), axis=0, keepdims=True)

    tok = pl.BlockSpec((tm, D), lambda i: (i, 0))
    return pl.pallas_call(
        body, name=name, grid=(T // tm,), in_specs=[tok, tok],
        out_specs=[tok, pl.BlockSpec((1, 1), lambda i: (0, 0))],
        out_shape=[SDS((T, D), F32), SDS((1, 1), F32)],
        compiler_params=_params(1),
    )(y, target)


def _exchange_copies(src, dst, send_sems, recv_sems, local_sems, broadcast, with_recvs=True):
    n = len(src)
    x, y, c = lax.axis_index("x"), lax.axis_index("y"), lax.axis_index("c")
    me = 4 * x + 2 * y + c
    local = [pltpu.make_async_copy(src[a] if broadcast else src[a].at[me], dst[a].at[me], local_sems.at[a])
             for a in range(n)]
    sends, recvs = [], []
    for d in range(1, N_DEV):
        px = 1 - x if d & 4 else x
        py = 1 - y if d & 2 else y
        pc = 1 - c if d & 1 else c
        peer = 4 * px + 2 * py + pc
        for a in range(n):
            for out, slot in ((sends, me), (recvs, peer)) if with_recvs else ((sends, me),):
                out.append(pltpu.make_async_remote_copy(
                    src_ref=src[a] if broadcast else src[a].at[peer], dst_ref=dst[a].at[slot],
                    send_sem=send_sems.at[a, d - 1], recv_sem=recv_sems.at[a, d - 1],
                    device_id=(px, py, pc), device_id_type=pl.DeviceIdType.MESH))
    return local, sends, recvs


def _gather_copies(src, dst, send_sems, recv_sems, local_sems, with_local=True):
    x, y, c = lax.axis_index("x"), lax.axis_index("y"), lax.axis_index("c")
    chips = [(1 - x, y), (x, 1 - y), (1 - x, 1 - y)]
    lin = lambda px, py, pc: 4 * px + 2 * py + pc

    def copy(a, k, incoming=False):
        if k == 0:
            to, out_src, out_slot, in_slot = (x, y, 1 - c), src[a], lin(x, y, c), lin(x, y, 1 - c)
        elif k <= 3:
            to, out_src, out_slot, in_slot = (*chips[k - 1], c), src[a], lin(x, y, c), lin(*chips[k - 1], c)
        else:
            slot = lin(*chips[k - 4], c)
            to, out_src, out_slot, in_slot = (x, y, 1 - c), dst[a].at[slot], slot, lin(*chips[k - 4], 1 - c)
        return pltpu.make_async_remote_copy(
            src_ref=out_src, dst_ref=dst[a].at[in_slot if incoming else out_slot], send_sem=send_sems.at[a, k],
            recv_sem=recv_sems.at[a, k], device_id=to, device_id_type=pl.DeviceIdType.MESH)

    local = [pltpu.make_async_copy(src[a], dst[a].at[lin(x, y, c)], local_sems.at[a])
             for a in range(len(src) if with_local else 0)]
    return copy, local


def _exchange_start(src, dst, send_sems, recv_sems, local_sems, broadcast):
    if broadcast:
        copy, local = _gather_copies(src, dst, send_sems, recv_sems, local_sems)
        for cp in local:
            cp.start()
        for k in range(4):
            for a in range(len(src)):
                copy(a, k).start()
        return
    local, sends, _ = _exchange_copies(src, dst, send_sems, recv_sems, local_sems, broadcast, with_recvs=False)
    for cp in local + sends:
        cp.start()


def _exchange_pass_on(src, dst, send_sems, recv_sems, local_sems, broadcast):
    if broadcast:
        copy, _ = _gather_copies(src, dst, send_sems, recv_sems, local_sems, with_local=False)
        for k in range(1, 4):
            for a in range(len(src)):
                copy(a, k, True).wait_recv()
                copy(a, k + 3).start()


def _exchange_wait(src, dst, send_sems, recv_sems, local_sems, broadcast):
    if broadcast:
        copy, local = _gather_copies(src, dst, send_sems, recv_sems, local_sems)
        n = len(src)
        for k in (0, 4, 5, 6):
            for a in range(n):
                copy(a, k, True).wait_recv()
        for k in range(N_DEV - 1):
            for a in range(n):
                copy(a, k).wait_send()
        for cp in local:
            cp.wait()
        return
    local, sends, recvs = _exchange_copies(src, dst, send_sems, recv_sems, local_sems, broadcast)
    for cp in recvs:
        cp.wait_recv()
    for cp in sends:
        cp.wait_send()
    for cp in local:
        cp.wait()


def _exchange_shapes(srcs, broadcast):
    n = len(srcs)
    out_shape = [SDS((N_DEV,) + (tuple(s.shape) if broadcast else tuple(s.shape[1:])), s.dtype) for s in srcs]
    sems = [pltpu.SemaphoreType.DMA((n, N_DEV - 1)), pltpu.SemaphoreType.DMA((n, N_DEV - 1)),
            pltpu.SemaphoreType.DMA((n,))]
    return out_shape, sems


def _exchange(srcs, broadcast, name):
    n = len(srcs)
    out_shape, sems = _exchange_shapes(srcs, broadcast)

    def body(*refs):
        mode = (refs[:n], refs[n:2 * n]) + tuple(refs[2 * n:]) + (broadcast,)
        _exchange_start(*mode)
        _exchange_pass_on(*mode)
        _exchange_wait(*mode)

    hbm = pl.BlockSpec(memory_space=pltpu.HBM)
    return pl.pallas_call(body, name=name, in_specs=[hbm] * n, out_specs=[hbm] * n, out_shape=out_shape,
                          scratch_shapes=sems)(*srcs)


def _call(body, *, name, grid, in_specs, out_specs, out_shape, args, scratch_shapes=(), exch=None):
    params = _params(len(grid))
    if exch is None:
        return pl.pallas_call(body, name=name, grid=grid, in_specs=in_specs, out_specs=out_specs, out_shape=out_shape,
                              scratch_shapes=list(scratch_shapes), compiler_params=params)(*args)
    srcs, broadcast = exch
    n, n_in, n_out, n_scr = len(srcs), len(args), len(out_shape), len(scratch_shapes)
    x_shape, sems = _exchange_shapes(srcs, broadcast)

    def carrier(*refs):
        ins, x_src = refs[:n_in], refs[n_in:n_in + n]
        outs, x_dst = refs[n_in + n:n_in + n + n_out], refs[n_in + n + n_out:n_in + 2 * n + n_out]
        scr = refs[n_in + 2 * n + n_out:n_in + 2 * n + n_out + n_scr]
        mode = (x_src, x_dst) + tuple(refs[n_in + 2 * n + n_out + n_scr:]) + (broadcast,)
        step = 0
        for a, g in enumerate(grid):
            step = step * g + pl.program_id(a)
        steps = int(np.prod(grid))
        pl.when(step == 0)(lambda: _exchange_start(*mode))
        body(*ins, *outs, *scr)
        pl.when(step == (3 * steps) // 4)(lambda: _exchange_pass_on(*mode))
        pl.when(step == steps - 1)(lambda: _exchange_wait(*mode))

    hbm = pl.BlockSpec(memory_space=pltpu.HBM)
    res = pl.pallas_call(carrier, name=name, grid=grid, in_specs=list(in_specs) + [hbm] * n,
                         out_specs=list(out_specs) + [hbm] * n, out_shape=list(out_shape) + x_shape,
                         scratch_shapes=list(scratch_shapes) + sems, compiler_params=params)(*args, *srcs)
    return res[:n_out], res[n_out:]


def _adam_math(w, g, m, v):
    m = ADAM_B1 * m + (1.0 - ADAM_B1) * g
    v = ADAM_B2 * v + (1.0 - ADAM_B2) * (g * g)
    m_hat = m / (1.0 - ADAM_B1 ** ADAM_STEP)
    v_hat = v / (1.0 - ADAM_B2 ** ADAM_STEP)
    delta = -ADAM_LR * (m_hat / (jnp.sqrt(v_hat) + ADAM_EPS) + ADAM_WD * w)
    return delta, m, v


def _adamw_sharded(landed, k, w, m, v, name):
    L, r, c = w.shape
    lanes = -(-c // LANE) * LANE
    rb = _tile(r, max(16, (1 << 21) // (N_DEV * lanes * 2)), 16)

    def body(*refs):
        l_refs = refs[:L]
        w_ref, m_ref, v_ref, g_ref, d_ref, nm_ref, nv_ref = refs[L:]
        for j in range(L):
            @pl.when(pl.program_id(0) == j)
            def _(j=j):
                g = l_refs[j][0, 0].astype(F32)
                for s in range(1, N_DEV):
                    g = g + l_refs[j][s, 0].astype(F32)
                delta, nm, nv = _adam_math(w_ref[0], g, m_ref[0], v_ref[0])
                g_ref[0] = g
                d_ref[0] = delta
                nm_ref[0] = nm
                nv_ref[0] = nv

    blk = pl.BlockSpec((1, rb, c), lambda l, i: (l, i, 0))
    land = [pl.BlockSpec((N_DEV, 1, rb, c), lambda l, i, j=j: (0, k, jnp.where(l == j, i, 0), 0)) for j in range(L)]
    return pl.pallas_call(
        body, name=name, grid=(L, r // rb),
        in_specs=land + [blk, blk, blk],
        out_specs=[blk] * 4, out_shape=[SDS((L, r, c), F32)] * 4,
        compiler_params=_params(2),
    )(*landed, w, m, v)


def _adamw_replicated(landed, w, m, v, name):
    R = w.shape[0]

    def body(l_ref, w_ref, m_ref, v_ref, g_ref, d_ref, nm_ref, nv_ref):
        g = l_ref[0]
        for s in range(1, N_DEV):
            g = g + l_ref[s]
        delta, nm, nv = _adam_math(w_ref[...], g, m_ref[...], v_ref[...])
        g_ref[...] = g
        d_ref[...] = delta
        nm_ref[...] = nm
        nv_ref[...] = nv

    blk = pl.BlockSpec((R, LANE), lambda i: (0, 0))
    return pl.pallas_call(
        body, name=name, grid=(1,),
        in_specs=[pl.BlockSpec((N_DEV, R, LANE), lambda i: (0, 0, 0)), blk, blk, blk],
        out_specs=[blk] * 4, out_shape=[SDS((R, LANE), F32)] * 4,
        compiler_params=_params(1),
    )(landed, w, m, v)


def _pack(parts):
    flat = jnp.concatenate([p.reshape(-1) for p in parts])
    rows = -(-flat.shape[0] // (8 * LANE)) * 8
    return jnp.pad(flat, (0, rows * LANE - flat.shape[0])).reshape(rows, LANE)


def _unpack(packed, shapes):
    flat = packed.reshape(-1)
    out, off = [], 0
    for s in shapes:
        size = int(np.prod(s))
        out.append(flat[off:off + size].reshape(s))
        off += size
    return out


def _layer_fwd(x0, w, l, lay, tabs, carry):
    T, D = x0.shape
    ctab, stab, na_bias = tabs
    col = lambda name: lay[name][0] // lay[name][1]
    sv = {'x0': x0}
    x1, sv['a1'], sv['b1'] = _carried(_ffn_fwd, carry, 'ffn1', x0, w['ffn1_norm'], w['ffn1_w1'], w['ffn1_w3'],
                                      w['ffn1_w2'], f"ffn1_fwd_{l}")
    z, sv['hb'] = _carried(_norm_matmul, carry, 'mix_in', x1, 0, w['mix_norm'], w['w_in_z'], f"mix_in_{l}")
    sv['na_q'], sv['na_k'], sv['na_v'], sv['gq'], sv['gk'], sv['gv'] = _split_z(z, lay, f"split_z_{l}")
    y_na = _carried(_na_fwd, carry, 'na', sv['na_q'], sv['na_k'], sv['na_v'], na_bias, l, w['na_q_norm'], w['na_k_norm'],
                    f"na_fwd_{l}")[0]
    sv['gg'] = _gla_gate_fwd(z, col('glow'), w['up_f'], w['up_b'], w['gla_gf_bias'], w['gla_gb_bias'], f"gla_gate_{l}")
    o_gla = _gla_fwd(sv['gq'], sv['gk'], sv['gv'], sv['gg'], f"gla_fwd_{l}")
    sv['q_raw'], sv['hq'] = _norm_matmul(z, col('c_q'), w['mla_cq_norm'], w['mla_w_uq'], f"mla_uq_{l}")
    sv['kv_raw'], sv['hkv'] = _norm_matmul(z, col('c_kv'), w['mla_ckv_norm'], w['mla_w_ukv'], f"mla_ukv_{l}")
    sv['mq'] = _mla_qprep_fwd(sv['q_raw'], w['mla_q_norm'], ctab, stab, f"mla_qprep_{l}")
    sv['mk'], sv['mv'] = _mla_kprep_fwd(sv['kv_raw'], z, col('k_rope'), w['mla_k_norm'], ctab, stab, f"mla_kprep_{l}")
    y_mla, sv['lse'] = _carried(_mla_attn_fwd, carry, 'mla', sv['mq'], sv['mk'], sv['mv'], f"mla_attn_{l}")
    w_br = (w['w_br_na'], w['w_br_gla'], w['w_br_mla'])
    x2 = _mix_out_fwd(x1, y_na, o_gla, y_mla, z, col('gr'), w['gla_out_norm'], w_br, w['w_out'], f"mix_out_{l}")
    sv.update(x1=x1, z=z, y_na=y_na, o_gla=o_gla, y_mla=y_mla, x2=x2)
    x3, sv['a2'], sv['b2'] = _carried(_ffn_fwd, carry, 'ffn2', x2, w['ffn2_norm'], w['ffn2_w1'], w['ffn2_w3'],
                                      w['ffn2_w2'], f"ffn2_fwd_{l}")
    return x3, sv


def _ffn_grads(grads, x, dy, a, b, w, pre, l, carry):
    dx, da, db, u, hb, dyb, dg = _carried(_ffn_bwd, carry, pre, x, w[pre + '_norm'], dy, a, b, w[pre + '_w1'],
                                          w[pre + '_w3'], w[pre + '_w2'], f"{pre}_bwd_{l}")
    grads[pre + '_norm'] = dg
    grads[pre + '_w1'] = _matmul_tn(hb, da, f"{pre}_dw1_{l}")
    grads[pre + '_w3'] = _carried(_matmul_tn, carry, pre + '_dw3', hb, db, f"{pre}_dw3_{l}")
    grads[pre + '_w2'] = _carried(_matmul_tn, carry, pre + '_dw2', u, dyb, f"{pre}_dw2_{l}")
    return dx


def _layer_bwd(grads, dx3, sv, w, l, lay, tabs, carry):
    T, D = dx3.shape
    ctab, stab, na_bias, onehot = tabs
    col = lambda name: lay[name][0] // lay[name][1]
    z = sv['z']
    dx2 = _ffn_grads(grads, sv['x2'], dx3, sv['a2'], sv['b2'], w, 'ffn2', l, carry)
    w_br = (w['w_br_na'], w['w_br_gla'], w['w_br_mla'])
    (dy_na, do_gla, dy_mla, dgr, dgates, dw0, dw1, dw2, dwo, dgn) = _mix_out_bwd(
        dx2, sv['y_na'], sv['o_gla'], sv['y_mla'], z, col('gr'), w['gla_out_norm'], w_br, w['w_out'], f"mix_out_bwd_{l}")
    grads.update(w_br_na=dw0, w_br_gla=dw1, w_br_mla=dw2, w_out=dwo, gla_out_norm=dgn)
    dq, dk, dv, dbias, dgq, dgk = _na_bwd(sv['na_q'], sv['na_k'], sv['na_v'], dy_na, na_bias, l,
                                          w['na_q_norm'], w['na_k_norm'], f"na_bwd_{l}")
    grads.update(na_q_norm=dgq, na_k_norm=dgk, na_rpb=_na_bias_tables_bwd(dbias, onehot))
    gdq, gdk, gdv, gdg = _gla_bwd(sv['gq'], sv['gk'], sv['gv'], sv['gg'], do_gla, f"gla_bwd_{l}")
    dglow, dupf, dupb, dbf, dbb = _gla_gate_bwd(z, col('glow'), w['up_f'], w['up_b'], w['gla_gf_bias'], w['gla_gb_bias'],
                                                gdg, f"gla_gate_bwd_{l}")
    grads.update(gla_gf_up=dupf[:GLA_RANK], gla_gb_up=dupb[GLA_RANK:2 * GLA_RANK], gla_gf_bias=dbf, gla_gb_bias=dbb)
    mdq, mdk, mdv = _carried(_mla_attn_bwd, carry, 'mla', sv['mq'], sv['mk'], sv['mv'], sv['y_mla'], sv['lse'], dy_mla,
                             f"mla_attn_bwd_{l}")
    dq_raw, dgqn = _mla_qprep_bwd(sv['q_raw'], w['mla_q_norm'], ctab, stab, mdq, f"mla_qprep_bwd_{l}")
    dkv_raw, dkr, dgkn = _mla_kprep_bwd(sv['kv_raw'], z, col('k_rope'), w['mla_k_norm'], ctab, stab, mdk, mdv,
                                        f"mla_kprep_bwd_{l}")
    dcq, dgcq = _norm_matmul_bwd(dq_raw, w['mla_w_uq'], z, col('c_q'), w['mla_cq_norm'], None, f"mla_uq_bwd_{l}")
    dckv, dgckv = _norm_matmul_bwd(dkv_raw, w['mla_w_ukv'], z, col('c_kv'), w['mla_ckv_norm'], None, f"mla_ukv_bwd_{l}")
    grads.update(mla_q_norm=dgqn, mla_k_norm=dgkn, mla_cq_norm=dgcq, mla_ckv_norm=dgckv,
                 mla_w_uq=_matmul_tn(sv['hq'], dq_raw, f"mla_duq_{l}", F32),
                 mla_w_ukv=_matmul_tn(sv['hkv'], dkv_raw, f"mla_dukv_{l}", F32))
    dz = _assemble_dz(dgates, (dq, dk, dv), (gdq, gdk), gdv, dgr, dcq, dckv, dglow, dkr, lay, z.shape[1], f"dz_{l}")
    dx1, dgmix = _carried(_norm_matmul_bwd, carry, 'mix_in', dz, w['w_in_z'], sv['x1'], 0, w['mix_norm'], dx2,
                          f"mix_in_bwd_{l}")
    grads.update(mix_norm=dgmix, w_in=_matmul_tn(sv['hb'], dz, f"mix_dw_in_{l}"))
    return _ffn_grads(grads, sv['x0'], dx1, sv['a1'], sv['b1'], w, 'ffn1', l, carry)


EXCHANGE_PARTS = {
    'F1w1': [['ffn1_w1']],
    'F1w3': [['ffn1_w3']],
    'F1b': [['ffn1_w2']],
    'F2': [['ffn2_w1', 'ffn2_w3'], ['ffn2_w2']],
    'C': [['w_in']],
    'S': [['w_br_na', 'w_br_gla', 'w_br_mla'], ['w_out'], ['mla_w_uq'], ['mla_w_ukv'], ['gla_gf_up', 'gla_gb_up']],
}
FWD_PLAN = {'ffn1': [(0, 'C')], 'mix_in': [(0, 'S')], 'mla': [(0, 'F2')], 'na': [(1, 'F1b')],
            'ffn2': [(1, 'F1w1'), (1, 'F1w3')]}
BWD_PLAN = {'mla': [(0, 'F2')], 'mix_in': [(0, 'S')], 'ffn1': [(0, 'C')], 'ffn2': [(1, 'F1w1'), (1, 'F1w3'), (1, 'F1b')]}
BWD_PLAN_LAYER0 = {'ffn1_dw3': [(0, 'F1w1')], 'ffn1_dw2': [(0, 'F1w3')]}
FWD_EDGE = ['F1w1', 'F1w3', 'F1b']
BWD_EDGE = ['F1b']


def _carried(fn, carry, key, *args):
    if key in carry:
        make_srcs, broadcast, deliver = carry[key]
        outs, landed = fn(*args, exch=(make_srcs(), broadcast))
        deliver(landed)
        return outs
    return fn(*args)


def kernel(x, ffn1_norm, ffn1_w1, ffn1_w3, ffn1_w2, mix_norm, w_in, na_q_norm, na_k_norm, na_rpb, gla_gf_up, gla_gf_bias, gla_gb_up, gla_gb_bias, gla_out_norm, mla_cq_norm, mla_ckv_norm, mla_w_uq, mla_w_ukv, mla_q_norm, mla_k_norm, w_br_na, w_br_gla, w_br_mla, w_out, ffn2_norm, ffn2_w1, ffn2_w3, ffn2_w2, loss_target, m_ffn1_norm, m_ffn1_w1, m_ffn1_w3, m_ffn1_w2, m_mix_norm, m_w_in, m_na_q_norm, m_na_k_norm, m_na_rpb, m_gla_gf_up, m_gla_gf_bias, m_gla_gb_up, m_gla_gb_bias, m_gla_out_norm, m_mla_cq_norm, m_mla_ckv_norm, m_mla_w_uq, m_mla_w_ukv, m_mla_q_norm, m_mla_k_norm, m_w_br_na, m_w_br_gla, m_w_br_mla, m_w_out, m_ffn2_norm, m_ffn2_w1, m_ffn2_w3, m_ffn2_w2, v_ffn1_norm, v_ffn1_w1, v_ffn1_w3, v_ffn1_w2, v_mix_norm, v_w_in, v_na_q_norm, v_na_k_norm, v_na_rpb, v_gla_gf_up, v_gla_gf_bias, v_gla_gb_up, v_gla_gb_bias, v_gla_out_norm, v_mla_cq_norm, v_mla_ckv_norm, v_mla_w_uq, v_mla_w_ukv, v_mla_q_norm, v_mla_k_norm, v_w_br_na, v_w_br_gla, v_w_br_mla, v_w_out, v_ffn2_norm, v_ffn2_w1, v_ffn2_w3, v_ffn2_w2):
    given = dict(locals())
    P = {n: given[n] for n in WEIGHTS}
    M = {n: given['m_' + n] for n in WEIGHTS}
    V = {n: given['v_' + n] for n in WEIGHTS}
    xs = x[0]
    T, D = xs.shape
    L = ffn1_norm.shape[0]
    lay, zp, d_in = _z_layout(D)

    def arrays_of(items):
        return [(l, names) for l, p in items for names in EXCHANGE_PARTS[p]]

    ws = [{n: P[n][l][None, :] for n in REPLICATED if n != 'na_rpb'} for l in range(L)]

    def gather_srcs(items):
        return [jnp.stack([P[n][l].astype(BF16) for n in names]) for l, names in arrays_of(items)]

    def take_gathered(items, res):
        for (l, names), g in zip(arrays_of(items), res):
            for k, n in enumerate(names):
                blk = g[:, k]
                if n == 'w_in':
                    ws[l]['w_in_z'] = _z_cols_from_blocks(blk, lay)
                    continue
                if n in COL_SHARDED:
                    full = jnp.concatenate([blk[d] for d in range(N_DEV)], axis=1)
                else:
                    full = blk.reshape(N_DEV * blk.shape[1], blk.shape[2])
                if n == 'gla_gf_up':
                    ws[l]['up_f'] = jnp.zeros((LANE, GLA_QK_W), BF16).at[:GLA_RANK].set(full)
                elif n == 'gla_gb_up':
                    ws[l]['up_b'] = jnp.zeros((LANE, GLA_QK_W), BF16).at[GLA_RANK:2 * GLA_RANK].set(full)
                else:
                    ws[l][n] = full

    def plan(table, l, make_srcs, broadcast, deliver):
        carry = {}
        for key, items in table.items():
            items = [(l + off, p) for off, p in items if l + off < L]
            if items:
                carry[key] = (functools.partial(make_srcs, items), broadcast, functools.partial(deliver, items))
        return carry

    ctab, stab = _rope_tables(T)
    na_bias, onehot = _na_bias_tables(na_rpb)

    edge = [(0, p) for p in FWD_EDGE]
    take_gathered(edge, _exchange(gather_srcs(edge), True, "gather_weights_edge"))
    saved = []
    h = xs
    for l in range(L):
        h, sv = _layer_fwd(h, ws[l], l, lay, (ctab, stab, na_bias), plan(FWD_PLAN, l, gather_srcs, True, take_gathered))
        saved.append(sv)
    dh, loss_local = _loss_head(h, loss_target[0], "loss_head")
    loss = lax.psum(loss_local[0, 0], ("x", "y", "c"))

    layer_grads = [dict() for _ in range(L)]

    def scatter_srcs(items):
        def blocks(l, n):
            g = layer_grads[l][n]
            if n == 'w_in':
                return _blocks_from_z_cols(g, lay, d_in).astype(BF16)
            if n in COL_SHARDED:
                c = g.shape[1] // N_DEV
                return jnp.stack([g[:, d * c:(d + 1) * c] for d in range(N_DEV)]).astype(BF16)
            return g.reshape(N_DEV, -1, g.shape[1]).astype(BF16)
        return [jnp.stack([blocks(l, n) for n in names], axis=1) for l, names in arrays_of(items)]

    landed = [dict() for _ in range(L)]

    def take_landed(items, res):
        for (l, names), r in zip(arrays_of(items), res):
            for k, n in enumerate(names):
                landed[l][n] = (r, k)

    for l in reversed(range(L)):
        table = {**BWD_PLAN, **BWD_PLAN_LAYER0} if l == 0 else BWD_PLAN
        dh = _layer_bwd(layer_grads[l], dh, saved[l], ws[l], l, lay, (ctab, stab, na_bias, onehot),
                        plan(table, l, scatter_srcs, False, take_landed))
    grad_x = dh[None]
    edge = [(0, p) for p in BWD_EDGE]
    take_landed(edge, _exchange(scatter_srcs(edge), False, "scatter_grads_edge"))

    out = {}
    for n in SHARDED:
        out[n] = _adamw_sharded([landed[l][n][0] for l in range(L)], landed[0][n][1], P[n], M[n], V[n], f"adamw_{n}")

    rep = [n for n in REPLICATED]
    rep_shapes = [tuple(P[n].shape) for n in rep]
    rep_partial = _pack([jnp.stack([layer_grads[l][n].reshape(P[n].shape[1:]) for l in range(L)]) for n in rep])
    (rep_landed,) = _exchange([rep_partial], True, "gather_small_grads")
    packed = _adamw_replicated(rep_landed, _pack([P[n] for n in rep]), _pack([M[n] for n in rep]),
                               _pack([V[n] for n in rep]), "adamw_replicated")
    for kind, pk in enumerate(packed):
        for n, arr in zip(rep, _unpack(pk, rep_shapes)):
            out.setdefault(n, [None] * 4)[kind] = arr

    res = [loss, grad_x]
    for kind in range(4):
        res += [out[n][kind] for n in WEIGHTS]
    return tuple(res)
```

```python
import functools

import numpy as np
import jax
import jax.numpy as jnp
from jax import lax
from jax.experimental import pallas as pl
from jax.experimental.pallas import tpu as pltpu

F32 = jnp.float32
BF16 = jnp.bfloat16
SDS = jax.ShapeDtypeStruct
HI = lax.Precision.HIGHEST

N_DEV = 8
DEPTH = 4
EPS = 1e-6
LANE = 128
VMEM_LIMIT = 56 * 1024 * 1024

GRID_W = 64
NA_HEADS, NA_HD, NA_WIN_R, NA_WIN_C = 8, 64, 8, 16
GLA_HEADS, GLA_DK, GLA_DV, GLA_RANK, GLA_TAU, GLA_CHUNK = 4, 64, 128, 16, 16.0, 64
MLA_HEADS, MLA_QR, MLA_KVR, MLA_NOPE, MLA_ROPE, MLA_V = 4, 256, 256, 128, 64, 128
MLA_QK = MLA_NOPE + MLA_ROPE
ROPE_THETA = 10000.0
NA_W = NA_HEADS * NA_HD
GLA_QK_W = GLA_HEADS * GLA_DK
GLA_V_W = GLA_HEADS * GLA_DV
MLA_V_W = MLA_HEADS * MLA_V

ADAM_LR, ADAM_B1, ADAM_B2, ADAM_EPS, ADAM_WD, ADAM_STEP = 0.001, 0.9, 0.999, 1e-08, 0.01, 10

WEIGHTS = ['ffn1_norm', 'ffn1_w1', 'ffn1_w3', 'ffn1_w2', 'mix_norm', 'w_in', 'na_q_norm', 'na_k_norm', 'na_rpb',
           'gla_gf_up', 'gla_gf_bias', 'gla_gb_up', 'gla_gb_bias', 'gla_out_norm', 'mla_cq_norm', 'mla_ckv_norm',
           'mla_w_uq', 'mla_w_ukv', 'mla_q_norm', 'mla_k_norm', 'w_br_na', 'w_br_gla', 'w_br_mla', 'w_out',
           'ffn2_norm', 'ffn2_w1', 'ffn2_w3', 'ffn2_w2']
COL_SHARDED = ['ffn1_w1', 'ffn1_w3', 'w_in', 'gla_gf_up', 'gla_gb_up', 'mla_w_uq', 'mla_w_ukv', 'w_br_na', 'w_br_gla',
               'w_br_mla', 'ffn2_w1', 'ffn2_w3']
ROW_SHARDED = ['ffn1_w2', 'w_out', 'ffn2_w2']
SHARDED = [n for n in WEIGHTS if n in COL_SHARDED or n in ROW_SHARDED]
REPLICATED = [n for n in WEIGHTS if n not in SHARDED]

NT = (((1,), (1,)), ((), ()))
TN = (((0,), (0,)), ((), ()))


def _tile(n, pref, mult=8):
    best = None
    for t in range(mult, min(n, pref) + 1, mult):
        if n % t == 0:
            best = t
    return best if best is not None else n


def _params(n_axes):
    return pltpu.CompilerParams(dimension_semantics=("arbitrary",) * n_axes, vmem_limit_bytes=VMEM_LIMIT)


def _dot(a, b, dims=None, precision=None):
    if dims is None:
        return jnp.dot(a, b, preferred_element_type=F32, precision=precision)
    return lax.dot_general(a, b, dims, preferred_element_type=F32, precision=precision)


def _sigmoid(x):
    return 1.0 / (1.0 + jnp.exp(-x))


def _rstd(x):
    return lax.rsqrt(jnp.mean(x * x, axis=-1, keepdims=True) + EPS)


def _rms_bwd(dy, x, g, r):
    xh = x * r
    dyg = dy * g
    dx = r * (dyg - xh * jnp.mean(dyg * xh, axis=-1, keepdims=True))
    return dx, jnp.sum(dy * xh, axis=0, keepdims=True)


def _z_layout(d_model):
    own = {}
    off = 0
    for name, w in [('na_q', NA_W), ('na_k', NA_W), ('na_v', NA_W), ('gq', GLA_QK_W), ('gk', GLA_QK_W),
                    ('gv', GLA_V_W), ('gr', GLA_V_W), ('gfl', GLA_RANK), ('gbl', GLA_RANK), ('c_q', MLA_QR),
                    ('c_kv', MLA_KVR), ('k_rope', MLA_ROPE), ('gates', 3 * d_model)]:
        own[name] = (off, w)
        off += w
    order = [('gates', 3 * d_model), ('na_q', NA_W), ('na_k', NA_W), ('na_v', NA_W), ('gv', GLA_V_W), ('gr', GLA_V_W),
             ('gq', GLA_QK_W), ('gk', GLA_QK_W), ('c_q', MLA_QR), ('c_kv', MLA_KVR), ('glow', LANE), ('k_rope', LANE)]
    lay = {}
    z = 0
    for name, w in order:
        assert z % w == 0, (name, z, w)
        if name == 'glow':
            lay[name] = (z, w, own['gfl'][0], 2 * GLA_RANK)
        else:
            lay[name] = (z, w, own[name][0], own[name][1])
        z += w
    return lay, z, off


def _z_cols_from_blocks(blocks, lay):
    cs = blocks.shape[2]
    parts = []
    for zo, zw, oo, ow in lay.values():
        a = oo
        while a < oo + ow:
            b = min(oo + ow, (a // cs + 1) * cs)
            parts.append(blocks[a // cs, :, a % cs:a % cs + (b - a)])
            a = b
        if zw > ow:
            parts.append(jnp.zeros((blocks.shape[1], zw - ow), blocks.dtype))
    return jnp.concatenate(parts, axis=1)


def _blocks_from_z_cols(wz, lay, d_in):
    cs = d_in // N_DEV
    own = sorted(lay.values(), key=lambda t: t[2])
    out = []
    for k in range(N_DEV):
        parts = []
        for zo, zw, oo, ow in own:
            a, b = max(oo, k * cs), min(oo + ow, (k + 1) * cs)
            if a < b:
                parts.append(wz[:, zo + a - oo:zo + b - oo])
        out.append(jnp.concatenate(parts, axis=1))
    return jnp.stack(out)


def _ffn_fwd(x, g, w1, w3, w2, name, exch=None):
    T, D = x.shape
    Fd = w1.shape[1]
    tm, tf = _tile(T, 1024), _tile(Fd, 256, LANE)
    nj = Fd // tf

    def body(x_ref, g_ref, w1_ref, w3_ref, w2_ref, o_ref, a_ref, b_ref, h_scr, acc):
        j = pl.program_id(1)

        @pl.when(j == 0)
        def _():
            xv = x_ref[...]
            h_scr[...] = (xv * _rstd(xv) * g_ref[...]).astype(BF16)
            acc[...] = jnp.zeros_like(acc)

        h = h_scr[...]
        a = _dot(h, w1_ref[...])
        b = _dot(h, w3_ref[...])
        a_ref[...] = a.astype(BF16)
        b_ref[...] = b.astype(BF16)
        u = a * _sigmoid(a) * b
        acc[...] += _dot(u.astype(BF16), w2_ref[...])

        @pl.when(j == nj - 1)
        def _():
            o_ref[...] = x_ref[...] + 0.5 * acc[...]

    return _call(
        body, name=name, grid=(T // tm, nj),
        in_specs=[pl.BlockSpec((tm, D), lambda i, j: (i, 0)), pl.BlockSpec((1, D), lambda i, j: (0, 0)),
                  pl.BlockSpec((D, tf), lambda i, j: (0, j)), pl.BlockSpec((D, tf), lambda i, j: (0, j)),
                  pl.BlockSpec((tf, D), lambda i, j: (j, 0))],
        out_specs=[pl.BlockSpec((tm, D), lambda i, j: (i, 0)), pl.BlockSpec((tm, tf), lambda i, j: (i, j)),
                   pl.BlockSpec((tm, tf), lambda i, j: (i, j))],
        out_shape=[SDS((T, D), F32), SDS((T, Fd), BF16), SDS((T, Fd), BF16)],
        scratch_shapes=[pltpu.VMEM((tm, D), BF16), pltpu.VMEM((tm, D), F32)],
        args=(x, g, w1, w3, w2), exch=exch)


def _ffn_bwd(x, g, dy, a, b, w1, w3, w2, name, exch=None):
    T, D = x.shape
    Fd = w1.shape[1]
    tm, tf = _tile(T, 1024), _tile(Fd, 256, LANE)
    nj = Fd // tf

    def body(x_ref, g_ref, dy_ref, a_ref, b_ref, w1_ref, w3_ref, w2_ref,
             dx_ref, da_ref, db_ref, u_ref, hb_ref, dyb_ref, dg_ref, dh_acc):
        i, j = pl.program_id(0), pl.program_id(1)

        @pl.when(j == 0)
        def _():
            xv = x_ref[...]
            hb_ref[...] = (xv * _rstd(xv) * g_ref[...]).astype(BF16)
            dyb_ref[...] = (0.5 * dy_ref[...]).astype(BF16)
            dh_acc[...] = jnp.zeros_like(dh_acc)

        @pl.when((i == 0) & (j == 0))
        def _():
            dg_ref[...] = jnp.zeros_like(dg_ref)

        du = _dot(dyb_ref[...], w2_ref[...], NT)
        av = a_ref[...].astype(F32)
        bv = b_ref[...].astype(F32)
        s = _sigmoid(av)
        sl = av * s
        da = (du * bv * (s * (1.0 + av * (1.0 - s)))).astype(BF16)
        db = (du * sl).astype(BF16)
        da_ref[...] = da
        db_ref[...] = db
        u_ref[...] = (sl * bv).astype(BF16)
        dh_acc[...] += _dot(da, w1_ref[...], NT) + _dot(db, w3_ref[...], NT)

        @pl.when(j == nj - 1)
        def _():
            xv = x_ref[...]
            dxn, dg = _rms_bwd(dh_acc[...], xv, g_ref[...], _rstd(xv))
            dx_ref[...] = dy_ref[...] + dxn
            dg_ref[...] += dg

    row = lambda i, j: (i, 0)
    return _call(
        body, name=name, grid=(T // tm, nj),
        in_specs=[pl.BlockSpec((tm, D), row), pl.BlockSpec((1, D), lambda i, j: (0, 0)), pl.BlockSpec((tm, D), row),
                  pl.BlockSpec((tm, tf), lambda i, j: (i, j)), pl.BlockSpec((tm, tf), lambda i, j: (i, j)),
                  pl.BlockSpec((D, tf), lambda i, j: (0, j)), pl.BlockSpec((D, tf), lambda i, j: (0, j)),
                  pl.BlockSpec((tf, D), lambda i, j: (j, 0))],
        out_specs=[pl.BlockSpec((tm, D), row), pl.BlockSpec((tm, tf), lambda i, j: (i, j)),
                   pl.BlockSpec((tm, tf), lambda i, j: (i, j)), pl.BlockSpec((tm, tf), lambda i, j: (i, j)),
                   pl.BlockSpec((tm, D), row), pl.BlockSpec((tm, D), row), pl.BlockSpec((1, D), lambda i, j: (0, 0))],
        out_shape=[SDS((T, D), F32), SDS((T, Fd), BF16), SDS((T, Fd), BF16), SDS((T, Fd), BF16),
                   SDS((T, D), BF16), SDS((T, D), BF16), SDS((1, D), F32)],
        scratch_shapes=[pltpu.VMEM((tm, D), F32)],
        args=(x, g, dy, a, b, w1, w3, w2), exch=exch)


def _matmul_tn(a, b, name, out_dtype=BF16, exch=None):
    T, K = a.shape
    N = b.shape[1]
    tk = _tile(K, 1408, LANE)
    tn = _tile(N, 1408, LANE)
    tt = _tile(T, 1024)
    nt = T // tt

    def body(a_ref, b_ref, o_ref, acc):
        t = pl.program_id(2)

        @pl.when(t == 0)
        def _():
            acc[...] = jnp.zeros_like(acc)

        acc[...] += _dot(a_ref[...], b_ref[...], TN)

        @pl.when(t == nt - 1)
        def _():
            o_ref[...] = acc[...].astype(out_dtype)

    res = _call(
        body, name=name, grid=(K // tk, N // tn, nt),
        in_specs=[pl.BlockSpec((tt, tk), lambda k, n, t: (t, k)), pl.BlockSpec((tt, tn), lambda k, n, t: (t, n))],
        out_specs=[pl.BlockSpec((tk, tn), lambda k, n, t: (k, n))],
        out_shape=[SDS((K, N), out_dtype)],
        scratch_shapes=[pltpu.VMEM((tk, tn), F32)],
        args=(a, b), exch=exch)
    return res[0] if exch is None else (res[0][0], res[1])


def _norm_matmul(x, xcol, g, w, name, exch=None):
    T = x.shape[0]
    D = g.shape[1]
    N = w.shape[1]
    tm, tn = _tile(T, 1024), _tile(N, 768, LANE)

    def body(x_ref, g_ref, w_ref, z_ref, hb_ref):
        @pl.when(pl.program_id(1) == 0)
        def _():
            xv = x_ref[...]
            hb_ref[...] = (xv * _rstd(xv) * g_ref[...]).astype(BF16)

        z_ref[...] = _dot(hb_ref[...], w_ref[...])

    return _call(
        body, name=name, grid=(T // tm, N // tn),
        in_specs=[pl.BlockSpec((tm, D), lambda i, j: (i, xcol)), pl.BlockSpec((1, D), lambda i, j: (0, 0)),
                  pl.BlockSpec((D, tn), lambda i, j: (0, j))],
        out_specs=[pl.BlockSpec((tm, tn), lambda i, j: (i, j)), pl.BlockSpec((tm, D), lambda i, j: (i, 0))],
        out_shape=[SDS((T, N), F32), SDS((T, D), BF16)],
        args=(x, g, w), exch=exch)


def _norm_matmul_bwd(dz, w, x, xcol, g, resid, name, exch=None):
    T, N = dz.shape
    D = g.shape[1]
    tm, tn = _tile(T, 1024), _tile(N, 768, LANE)
    nj = N // tn
    has_resid = resid is not None

    def body(*refs):
        if has_resid:
            dz_ref, w_ref, x_ref, g_ref, r_ref, dx_ref, dg_ref, acc = refs
        else:
            dz_ref, w_ref, x_ref, g_ref, dx_ref, dg_ref, acc = refs
        i, j = pl.program_id(0), pl.program_id(1)

        @pl.when(j == 0)
        def _():
            acc[...] = jnp.zeros_like(acc)

        @pl.when((i == 0) & (j == 0))
        def _():
            dg_ref[...] = jnp.zeros_like(dg_ref)

        acc[...] += _dot(dz_ref[...], w_ref[...], NT)

        @pl.when(j == nj - 1)
        def _():
            xv = x_ref[...]
            dxn, dg = _rms_bwd(acc[...], xv, g_ref[...], _rstd(xv))
            dx_ref[...] = dxn + r_ref[...] if has_resid else dxn
            dg_ref[...] += dg

    in_specs = [pl.BlockSpec((tm, tn), lambda i, j: (i, j)), pl.BlockSpec((D, tn), lambda i, j: (0, j)),
                pl.BlockSpec((tm, D), lambda i, j: (i, xcol)), pl.BlockSpec((1, D), lambda i, j: (0, 0))]
    args = [dz, w, x, g]
    if has_resid:
        in_specs.append(pl.BlockSpec((tm, D), lambda i, j: (i, 0)))
        args.append(resid)
    return _call(
        body, name=name, grid=(T // tm, nj), in_specs=in_specs,
        out_specs=[pl.BlockSpec((tm, D), lambda i, j: (i, 0)), pl.BlockSpec((1, D), lambda i, j: (0, 0))],
        out_shape=[SDS((T, D), F32), SDS((1, D), F32)],
        scratch_shapes=[pltpu.VMEM((tm, D), F32)],
        args=args, exch=exch)


def _na_bias_tables(rpb):
    L = rpb.shape[0]
    qc = np.arange(GRID_W)[:, None]
    kc = np.arange(GRID_W)[None, :]
    dc = np.clip(kc - qc + NA_WIN_C - 1, 0, 2 * NA_WIN_C - 2)
    c0 = np.clip(qc - NA_WIN_C // 2, 0, GRID_W - NA_WIN_C)
    ok = (kc >= c0) & (kc < c0 + NA_WIN_C)
    onehot = (dc.reshape(-1)[None, :] == np.arange(2 * NA_WIN_C - 1)[:, None]).astype(np.float32)
    t1 = jnp.stack([rpb[:, :, c:c + NA_WIN_R, :] for c in range(NA_WIN_R)], axis=2)
    full = jnp.dot(t1.reshape(-1, 2 * NA_WIN_C - 1), jnp.asarray(onehot), precision=HI)
    full = full.reshape(L, NA_HEADS, NA_WIN_R, NA_WIN_R, GRID_W, GRID_W)
    full = jnp.where(jnp.asarray(ok)[None, None, None, None], full, -1e30)
    return jnp.transpose(full, (0, 1, 2, 4, 3, 5)).reshape(L * NA_HEADS, NA_WIN_R, GRID_W, NA_WIN_R * GRID_W), onehot


def _na_bias_tables_bwd(db, onehot):
    d = db.reshape(NA_HEADS, NA_WIN_R, GRID_W, NA_WIN_R, GRID_W)
    d = jnp.transpose(d, (0, 1, 3, 2, 4)).reshape(-1, GRID_W * GRID_W)
    t1 = jnp.dot(d, jnp.asarray(onehot.T), precision=HI).reshape(NA_HEADS, NA_WIN_R, NA_WIN_R, 2 * NA_WIN_C - 1)
    out = jnp.zeros((NA_HEADS, 2 * NA_WIN_R - 1, 2 * NA_WIN_C - 1), F32)
    for c in range(NA_WIN_R):
        out = out.at[:, c:c + NA_WIN_R, :].add(t1[:, c])
    return out


def _na_windows(bi, rb, rows):
    wsl, cls = [], []
    for i in range(rb):
        r = bi * rb + i
        r0 = jnp.clip(r - NA_WIN_R // 2, 0, rows - NA_WIN_R)
        wsl.append(pl.ds(pl.multiple_of(r0 * GRID_W, GRID_W), NA_WIN_R * GRID_W))
        cls.append(r0 - r + NA_WIN_R - 1)
    return wsl, cls


def _na_fwd(q, k, v, bias, layer, gq, gk, name, exch=None):
    H, T, d = q.shape
    rows = T // GRID_W
    rb = _tile(rows, 8, 1)
    win = NA_WIN_R * GRID_W
    scale = d ** -0.5

    def body(q_ref, k_ref, v_ref, b_ref, gq_ref, gk_ref, o_ref, qn, kn, vb):
        qv, kv = q_ref[0], k_ref[0]
        qn[...] = (qv * _rstd(qv) * gq_ref[...] * scale).astype(BF16)
        kn[...] = (kv * _rstd(kv) * gk_ref[...]).astype(BF16)
        vb[...] = v_ref[0].astype(BF16)

        def block(bi, c):
            wsl, cls = _na_windows(bi, rb, rows)
            qsl = pl.ds(pl.multiple_of(bi * (rb * GRID_W), rb * GRID_W), rb * GRID_W)
            kw = jnp.stack([kn[w, :] for w in wsl])
            vw = jnp.stack([vb[w, :] for w in wsl])
            s = _bdot(qn[qsl, :].reshape(rb, GRID_W, d), kw, 2, 2) + jnp.stack([b_ref[0, c_] for c_ in cls])
            p = jnp.exp(s - jnp.max(s, axis=-1, keepdims=True))
            p = p / jnp.sum(p, axis=-1, keepdims=True)
            o_ref[0, qsl, :] = _bdot(p.astype(BF16), vw, 2, 1).reshape(rb * GRID_W, d)
            return c

        lax.fori_loop(0, rows // rb, block, 0)

    hm = pl.BlockSpec((1, T, d), lambda h: (h, 0, 0))
    gs = pl.BlockSpec((1, d), lambda h: (0, 0))
    return _call(
        body, name=name, grid=(H,),
        in_specs=[hm, hm, hm, pl.BlockSpec((1, NA_WIN_R, GRID_W, win), lambda h: (layer * H + h, 0, 0, 0)), gs, gs],
        out_specs=[hm], out_shape=[SDS((H, T, d), F32)],
        scratch_shapes=[pltpu.VMEM((T, d), BF16)] * 3,
        args=(q, k, v, bias, gq, gk), exch=exch)


def _na_bwd(q, k, v, do, bias, layer, gq, gk, name):
    H, T, d = q.shape
    rows = T // GRID_W
    rb = _tile(rows, 8, 1)
    win = NA_WIN_R * GRID_W
    scale = d ** -0.5

    def body(q_ref, k_ref, v_ref, do_ref, b_ref, gq_ref, gk_ref,
             dq_ref, dk_ref, dv_ref, db_ref, dgq_ref, dgk_ref, qn, kn, vb, dob, dqn, dkn):
        h = pl.program_id(0)
        qv, kv = q_ref[0], k_ref[0]
        rq, rk = _rstd(qv), _rstd(kv)
        qn[...] = (qv * rq * gq_ref[...] * scale).astype(BF16)
        kn[...] = (kv * rk * gk_ref[...]).astype(BF16)
        vb[...] = v_ref[0].astype(BF16)
        dob[...] = do_ref[0].astype(BF16)
        dkn[...] = jnp.zeros_like(dkn)
        dv_ref[...] = jnp.zeros_like(dv_ref)
        db_ref[...] = jnp.zeros_like(db_ref)

        @pl.when(h == 0)
        def _():
            dgq_ref[...] = jnp.zeros_like(dgq_ref)
            dgk_ref[...] = jnp.zeros_like(dgk_ref)

        def block(bi, c):
            wsl, cls = _na_windows(bi, rb, rows)
            qsl = pl.ds(pl.multiple_of(bi * (rb * GRID_W), rb * GRID_W), rb * GRID_W)
            qs = qn[qsl, :].reshape(rb, GRID_W, d)
            dos = dob[qsl, :].reshape(rb, GRID_W, d)
            kw = jnp.stack([kn[w, :] for w in wsl])
            vw = jnp.stack([vb[w, :] for w in wsl])
            s = _bdot(qs, kw, 2, 2) + jnp.stack([b_ref[0, c_] for c_ in cls])
            p = jnp.exp(s - jnp.max(s, axis=-1, keepdims=True))
            p = p / jnp.sum(p, axis=-1, keepdims=True)
            dp = _bdot(dos, vw, 2, 2)
            ds = p * (dp - jnp.sum(dp * p, axis=-1, keepdims=True))
            dsb = ds.astype(BF16)
            dqn[qsl, :] = _bdot(dsb, kw, 2, 1).reshape(rb * GRID_W, d)
            dvw = _bdot(p.astype(BF16), dos, 1, 1)
            dkw = _bdot(dsb, qs, 1, 1)
            for i in range(rb):
                db_ref[0, cls[i]] += ds[i]
                dv_ref[0, wsl[i], :] += dvw[i]
                dkn[wsl[i], :] += dkw[i]
            return c

        lax.fori_loop(0, rows // rb, block, 0)
        dq, dgq = _rms_bwd(dqn[...] * scale, qv, gq_ref[...], rq)
        dk, dgk = _rms_bwd(dkn[...], kv, gk_ref[...], rk)
        dq_ref[0] = dq
        dk_ref[0] = dk
        dgq_ref[...] += dgq
        dgk_ref[...] += dgk

    hm = pl.BlockSpec((1, T, d), lambda h: (h, 0, 0))
    gs = pl.BlockSpec((1, d), lambda h: (0, 0))
    bs = pl.BlockSpec((1, NA_WIN_R, GRID_W, win), lambda h: (h, 0, 0, 0))
    return pl.pallas_call(
        body, name=name, grid=(H,),
        in_specs=[hm, hm, hm, hm, pl.BlockSpec((1, NA_WIN_R, GRID_W, win), lambda h: (layer * H + h, 0, 0, 0)), gs, gs],
        out_specs=[hm, hm, hm, bs, gs, gs],
        out_shape=[SDS((H, T, d), F32)] * 3 + [SDS((H, NA_WIN_R, GRID_W, win), F32), SDS((1, d), F32), SDS((1, d), F32)],
        scratch_shapes=[pltpu.VMEM((T, d), BF16)] * 4 + [pltpu.VMEM((T, d), F32)] * 2,
        compiler_params=_params(1),
    )(q, k, v, do, bias, gq, gk)


def _gla_gate_fwd(z, col, upf, upb, bf, bb, name):
    T = z.shape[0]
    W = upf.shape[1]
    tm = _tile(T, 1024)

    def body(z_ref, upf_ref, upb_ref, bf_ref, bb_ref, g_ref):
        seg = z_ref[...].astype(BF16)
        for rev, (up_ref, b_ref) in enumerate(((upf_ref, bf_ref), (upb_ref, bb_ref))):
            pre = _dot(seg, up_ref[...]) + b_ref[...]
            g = (jnp.minimum(pre, 0.0) - jnp.log(1.0 + jnp.exp(-jnp.abs(pre)))) * (1.0 / GLA_TAU)
            for h in range(GLA_HEADS):
                g_ref[rev, h] = g[:, h * GLA_DK:(h + 1) * GLA_DK]

    full = lambda shape: pl.BlockSpec(shape, lambda i: (0, 0))
    return pl.pallas_call(
        body, name=name, grid=(T // tm,),
        in_specs=[pl.BlockSpec((tm, LANE), lambda i: (i, col)), full((LANE, W)), full((LANE, W)), full((1, W)), full((1, W))],
        out_specs=pl.BlockSpec((2, GLA_HEADS, tm, GLA_DK), lambda i: (0, 0, i, 0)),
        out_shape=SDS((2, GLA_HEADS, T, GLA_DK), F32),
        compiler_params=_params(1),
    )(z, upf, upb, bf, bb)


def _gla_gate_bwd(z, col, upf, upb, bf, bb, dg, name):
    T = z.shape[0]
    W = upf.shape[1]
    tm = _tile(T, 1024)

    def body(z_ref, upf_ref, upb_ref, bf_ref, bb_ref, dg_ref, dseg_ref, dupf_ref, dupb_ref, dbf_ref, dbb_ref):
        @pl.when(pl.program_id(0) == 0)
        def _():
            for r in (dupf_ref, dupb_ref, dbf_ref, dbb_ref):
                r[...] = jnp.zeros_like(r)

        seg = z_ref[...].astype(BF16)
        dseg = jnp.zeros(dseg_ref.shape, F32)
        for rev, (up_ref, b_ref, dup_ref, db_ref) in enumerate(((upf_ref, bf_ref, dupf_ref, dbf_ref),
                                                               (upb_ref, bb_ref, dupb_ref, dbb_ref))):
            pre = _dot(seg, up_ref[...]) + b_ref[...]
            dgv = jnp.concatenate([dg_ref[rev, h] for h in range(GLA_HEADS)], axis=1)
            dpre = dgv * (1.0 / GLA_TAU) * (1.0 - _sigmoid(pre))
            dpb = dpre.astype(BF16)
            dseg = dseg + _dot(dpb, up_ref[...], NT)
            dup_ref[...] += _dot(seg, dpb, TN)
            db_ref[...] += jnp.sum(dpre, axis=0, keepdims=True)
        dseg_ref[...] = dseg

    full = lambda shape: pl.BlockSpec(shape, lambda i: (0, 0))
    return pl.pallas_call(
        body, name=name, grid=(T // tm,),
        in_specs=[pl.BlockSpec((tm, LANE), lambda i: (i, col)), full((LANE, W)), full((LANE, W)), full((1, W)), full((1, W)),
                  pl.BlockSpec((2, GLA_HEADS, tm, GLA_DK), lambda i: (0, 0, i, 0))],
        out_specs=[pl.BlockSpec((tm, LANE), lambda i: (i, 0)), full((LANE, W)), full((LANE, W)), full((1, W)), full((1, W))],
        out_shape=[SDS((T, LANE), F32), SDS((LANE, W), F32), SDS((LANE, W), F32), SDS((1, W), F32), SDS((1, W), F32)],
        compiler_params=_params(1),
    )(z, upf, upb, bf, bb, dg)


def _bdot(a, b, ca, cb, precision=None):
    return lax.dot_general(a, b, (((ca,), (cb,)), ((0,), (0,))), preferred_element_type=F32, precision=precision)


def _gla_chunk_terms(q_ref, k_ref, v_ref, g_ref, rev, tok, n, C):
    dk, dv = q_ref.shape[-1], v_ref.shape[-1]
    q = q_ref[0, tok, :].reshape(n, C, dk) * (dk ** -0.5)
    k = k_ref[0, tok, :].reshape(n, C, dk)
    v = v_ref[0, tok, :].reshape(n, C, dv)
    g = g_ref[rev, 0, tok, :].reshape(n, C, dk)
    ii = lax.broadcasted_iota(jnp.int32, (C, C), 0)
    jj = lax.broadcasted_iota(jnp.int32, (C, C), 1)
    tri = jnp.broadcast_to(((ii <= jj) if rev else (ii >= jj)).astype(F32), (n, C, C))
    amask = ((jj > ii) if rev else (ii >= jj))[None]
    b = _bdot(tri, g, 2, 1, HI)
    bl = jnp.sum(g, axis=1, keepdims=True)
    eb = jnp.exp(b)
    enb = jnp.exp(-b)
    eend = jnp.exp(bl - b)
    return q, k, v, tri, amask, bl, eb, enb, eend


def _gla_segments(T, C):
    n = T // C
    ns = _tile(n, 16, 1)

    def tok(s):
        return pl.ds(pl.multiple_of(s * (ns * C), ns * C), ns * C)

    def chunks(s):
        return pl.ds(pl.multiple_of(s * ns, ns), ns)

    return n, ns, n // ns, tok, chunks


def _gla_fwd(q, k, v, g, name):
    H, T, dk = q.shape
    dv = v.shape[-1]
    C = GLA_CHUNK
    n, ns, nseg, seg_tok, seg_chunks = _gla_segments(T, C)

    def body(q_ref, k_ref, v_ref, g_ref, o_ref, upd_scr, dec_scr, st_scr):
        for rev in (0, 1):
            def intra(s, c):
                tok, ch = seg_tok(s), seg_chunks(s)
                q, k, v, tri, amask, bl, eb, enb, eend = _gla_chunk_terms(q_ref, k_ref, v_ref, g_ref, rev, tok, ns, C)
                vb = v.astype(BF16)
                a = jnp.where(amask, _bdot((q * eb).astype(BF16), (k * enb).astype(BF16), 2, 2), 0.0).astype(BF16)
                o = _bdot(a, vb, 2, 1).reshape(ns * C, dv)
                if rev:
                    o_ref[0, tok, :] += o
                else:
                    o_ref[0, tok, :] = o
                upd_scr[ch] = _bdot(vb, (k * eend).astype(BF16), 1, 1)
                dec_scr[ch] = jnp.exp(bl)
                return c

            lax.fori_loop(0, nseg, intra, 0)

            def step(t, st):
                i = n - 1 - t if rev else t
                st_scr[i] = st
                return st * dec_scr[i] + upd_scr[i]

            lax.fori_loop(0, n, step, jnp.zeros((dv, dk), F32))

            def inter(s, c):
                tok, ch = seg_tok(s), seg_chunks(s)
                q, k, v, tri, amask, bl, eb, enb, eend = _gla_chunk_terms(q_ref, k_ref, v_ref, g_ref, rev, tok, ns, C)
                o_ref[0, tok, :] += _bdot((q * eb).astype(BF16), st_scr[ch].astype(BF16), 2, 2).reshape(ns * C, dv)
                return c

            lax.fori_loop(0, nseg, inter, 0)

    sk = pl.BlockSpec((1, T, dk), lambda h: (h, 0, 0))
    sv = pl.BlockSpec((1, T, dv), lambda h: (h, 0, 0))
    sg = pl.BlockSpec((2, 1, T, dk), lambda h: (0, h, 0, 0))
    return pl.pallas_call(
        body, name=name, grid=(H,), in_specs=[sk, sk, sv, sg], out_specs=sv,
        out_shape=SDS((H, T, dv), F32),
        scratch_shapes=[pltpu.VMEM((n, dv, dk), F32), pltpu.VMEM((n, 1, dk), F32), pltpu.VMEM((n, dv, dk), F32)],
        compiler_params=_params(1),
    )(q, k, v, g)


def _gla_bwd(q, k, v, g, do, name):
    H, T, dk = q.shape
    dv = v.shape[-1]
    C = GLA_CHUNK
    n, ns, nseg, seg_tok, seg_chunks = _gla_segments(T, C)

    def one_scan(rev, q_ref, k_ref, v_ref, g_ref, do_ref, dq_ref, dk_ref, dv_ref, dg_ref, upd_scr, dec_scr, st_scr, gn_scr,
                 rn_scr):
        def first(s, c):
            tok, ch = seg_tok(s), seg_chunks(s)
            q, k, v, tri, amask, bl, eb, enb, eend = _gla_chunk_terms(q_ref, k_ref, v_ref, g_ref, rev, tok, ns, C)
            dob = do_ref[0, tok, :].reshape(ns, C, dv).astype(BF16)
            upd_scr[ch] = _bdot(v.astype(BF16), (k * eend).astype(BF16), 1, 1)
            dec_scr[ch] = jnp.exp(bl)
            gn_scr[ch] = _bdot(dob, (q * eb).astype(BF16), 1, 1)
            return c

        lax.fori_loop(0, nseg, first, 0)

        def fstep(t, st):
            i = n - 1 - t if rev else t
            st_scr[i] = st
            return st * dec_scr[i] + upd_scr[i]

        lax.fori_loop(0, n, fstep, jnp.zeros((dv, dk), F32))

        def bstep(t, r):
            i = t if rev else n - 1 - t
            rn_scr[i] = r
            return gn_scr[i] + r * dec_scr[i]

        lax.fori_loop(0, n, bstep, jnp.zeros((dv, dk), F32))

        def second(s, c):
            tok, ch = seg_tok(s), seg_chunks(s)
            q, k, v, tri, amask, bl, eb, enb, eend = _gla_chunk_terms(q_ref, k_ref, v_ref, g_ref, rev, tok, ns, C)
            qe, ke, kend = q * eb, k * enb, k * eend
            qeb, keb, kendb, vb = qe.astype(BF16), ke.astype(BF16), kend.astype(BF16), v.astype(BF16)
            dob = do_ref[0, tok, :].reshape(ns, C, dv).astype(BF16)
            a = jnp.where(amask, _bdot(qeb, keb, 2, 2), 0.0).astype(BF16)
            st = st_scr[ch]
            rn = rn_scr[ch]
            rnb = rn.astype(BF16)
            da = jnp.where(amask, _bdot(dob, vb, 2, 2), 0.0).astype(BF16)
            dvv = _bdot(a, dob, 1, 1) + _bdot(kendb, rnb, 2, 2)
            dqe = _bdot(da, keb, 2, 1) + _bdot(dob, st.astype(BF16), 2, 1)
            dke = _bdot(da, qeb, 1, 1)
            dkend = _bdot(vb, rnb, 2, 1)
            ddec = jnp.sum(rn * st, axis=1, keepdims=True)
            dbl = ddec * jnp.exp(bl) + jnp.sum(dkend * kend, axis=1, keepdims=True)
            db = dqe * qe - dke * ke - dkend * kend
            dg = _bdot(tri, db, 1, 1, HI) + dbl
            dqv = (dqe * eb * (dk ** -0.5)).reshape(ns * C, dk)
            dkv = (dke * enb + dkend * eend).reshape(ns * C, dk)
            if rev:
                dq_ref[0, tok, :] += dqv
                dk_ref[0, tok, :] += dkv
                dv_ref[0, tok, :] += dvv.reshape(ns * C, dv)
            else:
                dq_ref[0, tok, :] = dqv
                dk_ref[0, tok, :] = dkv
                dv_ref[0, tok, :] = dvv.reshape(ns * C, dv)
            dg_ref[rev, 0, tok, :] = dg.reshape(ns * C, dk)
            return c

        lax.fori_loop(0, nseg, second, 0)

    def body(*refs):
        one_scan(0, *refs)
        one_scan(1, *refs)

    sk = pl.BlockSpec((1, T, dk), lambda h: (h, 0, 0), pipeline_mode=pl.Buffered(1))
    sv = pl.BlockSpec((1, T, dv), lambda h: (h, 0, 0), pipeline_mode=pl.Buffered(1))
    sg = pl.BlockSpec((2, 1, T, dk), lambda h: (0, h, 0, 0), pipeline_mode=pl.Buffered(1))
    st_shape = pltpu.VMEM((n, dv, dk), F32)
    return pl.pallas_call(
        body, name=name, grid=(H,), in_specs=[sk, sk, sv, sg, sv], out_specs=[sk, sk, sv, sg],
        out_shape=[SDS((H, T, dk), F32), SDS((H, T, dk), F32), SDS((H, T, dv), F32), SDS((2, H, T, dk), F32)],
        scratch_shapes=[st_shape, pltpu.VMEM((n, 1, dk), F32), st_shape, st_shape, st_shape],
        compiler_params=_params(1),
    )(q, k, v, g, do)


def _rope_tables(T):
    half = MLA_ROPE // 2
    inv = ROPE_THETA ** (-jnp.arange(half, dtype=F32) / half)
    ang = jnp.arange(T, dtype=F32)[:, None] * inv[None, :]
    cos, sin = jnp.cos(ang), jnp.sin(ang)
    ctab = jnp.concatenate([jnp.ones((T, MLA_NOPE), F32), cos, cos], axis=1)
    stab = jnp.concatenate([jnp.zeros((T, MLA_NOPE), F32), -sin, sin], axis=1)
    return ctab, stab


def _swap_rope_halves(x):
    half = MLA_ROPE // 2
    return jnp.concatenate([x[:, :MLA_NOPE], x[:, MLA_NOPE + half:], x[:, MLA_NOPE:MLA_NOPE + half]], axis=1)


def _norm_rope(x, g, c, s):
    xn = x * _rstd(x) * g
    return xn * c + _swap_rope_halves(xn) * s


def _norm_rope_bwd(dy, x, g, c, s):
    return _rms_bwd(dy * c + _swap_rope_halves(dy * s), x, g, _rstd(x))


def _mla_qprep_fwd(q_raw, g, ctab, stab, name):
    T, d = q_raw.shape[0], MLA_QK
    H = q_raw.shape[1] // d
    tm = _tile(T, 1024)

    def body(x_ref, g_ref, c_ref, s_ref, o_ref):
        for h in range(H):
            o_ref[h] = _norm_rope(x_ref[:, h * d:(h + 1) * d], g_ref[...], c_ref[...], s_ref[...]).astype(BF16)

    tab = pl.BlockSpec((tm, d), lambda i: (i, 0))
    return pl.pallas_call(
        body, name=name, grid=(T // tm,),
        in_specs=[pl.BlockSpec((tm, H * d), lambda i: (i, 0)), pl.BlockSpec((1, d), lambda i: (0, 0)), tab, tab],
        out_specs=pl.BlockSpec((H, tm, d), lambda i: (0, i, 0)), out_shape=SDS((H, T, d), BF16),
        compiler_params=_params(1),
    )(q_raw, g, ctab, stab)


def _mla_qprep_bwd(q_raw, g, ctab, stab, dy, name):
    T, d = q_raw.shape[0], MLA_QK
    H = q_raw.shape[1] // d
    tm = _tile(T, 1024)

    def body(x_ref, g_ref, c_ref, s_ref, dy_ref, dx_ref, dg_ref):
        @pl.when(pl.program_id(0) == 0)
        def _():
            dg_ref[...] = jnp.zeros_like(dg_ref)

        dg = jnp.zeros((1, d), F32)
        for h in range(H):
            dx, dgh = _norm_rope_bwd(dy_ref[h], x_ref[:, h * d:(h + 1) * d], g_ref[...], c_ref[...], s_ref[...])
            dx_ref[:, h * d:(h + 1) * d] = dx.astype(BF16)
            dg = dg + dgh
        dg_ref[...] += dg

    tab = pl.BlockSpec((tm, d), lambda i: (i, 0))
    tok = pl.BlockSpec((tm, H * d), lambda i: (i, 0))
    gs = pl.BlockSpec((1, d), lambda i: (0, 0))
    return pl.pallas_call(
        body, name=name, grid=(T // tm,),
        in_specs=[tok, gs, tab, tab, pl.BlockSpec((H, tm, d), lambda i: (0, i, 0))],
        out_specs=[tok, gs], out_shape=[SDS((T, H * d), BF16), SDS((1, d), F32)],
        compiler_params=_params(1),
    )(q_raw, g, ctab, stab, dy)


def _mla_kprep_fwd(kv_raw, z, kr_col, g, ctab, stab, name):
    T, d, hw = kv_raw.shape[0], MLA_QK, MLA_NOPE + MLA_V
    H = kv_raw.shape[1] // hw
    tm = _tile(T, 1024)

    def body(kv_ref, kr_ref, g_ref, c_ref, s_ref, k_ref, v_ref):
        kr = kr_ref[:, :MLA_ROPE]
        for h in range(H):
            x = jnp.concatenate([kv_ref[:, h * hw:h * hw + MLA_NOPE], kr], axis=1)
            k_ref[h] = _norm_rope(x, g_ref[...], c_ref[...], s_ref[...]).astype(BF16)
            v_ref[h] = kv_ref[:, h * hw + MLA_NOPE:(h + 1) * hw].astype(BF16)

    tab = pl.BlockSpec((tm, d), lambda i: (i, 0))
    return pl.pallas_call(
        body, name=name, grid=(T // tm,),
        in_specs=[pl.BlockSpec((tm, H * hw), lambda i: (i, 0)), pl.BlockSpec((tm, LANE), lambda i: (i, kr_col)),
                  pl.BlockSpec((1, d), lambda i: (0, 0)), tab, tab],
        out_specs=[pl.BlockSpec((H, tm, d), lambda i: (0, i, 0)), pl.BlockSpec((H, tm, MLA_V), lambda i: (0, i, 0))],
        out_shape=[SDS((H, T, d), BF16), SDS((H, T, MLA_V), BF16)],
        compiler_params=_params(1),
    )(kv_raw, z, g, ctab, stab)


def _mla_kprep_bwd(kv_raw, z, kr_col, g, ctab, stab, dk, dv, name):
    T, d, hw = kv_raw.shape[0], MLA_QK, MLA_NOPE + MLA_V
    H = kv_raw.shape[1] // hw
    tm = _tile(T, 1024)

    def body(kv_ref, kr_ref, g_ref, c_ref, s_ref, dk_ref, dv_ref, dkv_ref, dkr_ref, dg_ref):
        @pl.when(pl.program_id(0) == 0)
        def _():
            dg_ref[...] = jnp.zeros_like(dg_ref)

        kr = kr_ref[:, :MLA_ROPE]
        dg = jnp.zeros((1, d), F32)
        dkr = jnp.zeros((tm, MLA_ROPE), F32)
        for h in range(H):
            x = jnp.concatenate([kv_ref[:, h * hw:h * hw + MLA_NOPE], kr], axis=1)
            dx, dgh = _norm_rope_bwd(dk_ref[h], x, g_ref[...], c_ref[...], s_ref[...])
            dkv_ref[:, h * hw:h * hw + MLA_NOPE] = dx[:, :MLA_NOPE].astype(BF16)
            dkv_ref[:, h * hw + MLA_NOPE:(h + 1) * hw] = dv_ref[h].astype(BF16)
            dkr = dkr + dx[:, MLA_NOPE:]
            dg = dg + dgh
        dkr_ref[...] = jnp.concatenate([dkr, jnp.zeros((tm, LANE - MLA_ROPE), F32)], axis=1)
        dg_ref[...] += dg

    tab = pl.BlockSpec((tm, d), lambda i: (i, 0))
    tok = pl.BlockSpec((tm, H * hw), lambda i: (i, 0))
    gs = pl.BlockSpec((1, d), lambda i: (0, 0))
    return pl.pallas_call(
        body, name=name, grid=(T // tm,),
        in_specs=[tok, pl.BlockSpec((tm, LANE), lambda i: (i, kr_col)), gs, tab, tab,
                  pl.BlockSpec((H, tm, d), lambda i: (0, i, 0)), pl.BlockSpec((H, tm, MLA_V), lambda i: (0, i, 0))],
        out_specs=[tok, pl.BlockSpec((tm, LANE), lambda i: (i, 0)), gs],
        out_shape=[SDS((T, H * hw), BF16), SDS((T, LANE), F32), SDS((1, d), F32)],
        compiler_params=_params(1),
    )(kv_raw, z, g, ctab, stab, dk, dv)


def _mla_attn_fwd(q, k, v, name, exch=None):
    H, T, d = q.shape
    dv = v.shape[-1]
    tq = _tile(T, 512)
    scale = d ** -0.5

    def body(q_ref, k_ref, v_ref, o_ref, lse_ref):
        s = _dot(q_ref[0], k_ref[0], NT) * scale
        m = jnp.max(s, axis=-1, keepdims=True)
        p = jnp.exp(s - m)
        l = jnp.sum(p, axis=-1, keepdims=True)
        o_ref[0] = _dot((p / l).astype(BF16), v_ref[0])
        lse_ref[0] = _col_to_row(m + jnp.log(l))

    return _call(
        body, name=name, grid=(H, T // tq),
        in_specs=[pl.BlockSpec((1, tq, d), lambda h, i: (h, i, 0)), pl.BlockSpec((1, T, d), lambda h, i: (h, 0, 0)),
                  pl.BlockSpec((1, T, dv), lambda h, i: (h, 0, 0))],
        out_specs=[pl.BlockSpec((1, tq, dv), lambda h, i: (h, i, 0)), pl.BlockSpec((1, 1, tq), lambda h, i: (h, 0, i))],
        out_shape=[SDS((H, T, dv), F32), SDS((H, 1, T), F32)],
        args=(q, k, v), exch=exch)


def _col_to_row(col):
    m = col.shape[0]
    eye = lax.broadcasted_iota(jnp.int32, (m, m), 0) == lax.broadcasted_iota(jnp.int32, (m, m), 1)
    return jnp.sum(jnp.where(eye, col, 0.0), axis=0, keepdims=True)


def _mla_attn_bwd(q, k, v, o, lse, do, name, exch=None):
    H, T, d = q.shape
    dv = v.shape[-1]
    tq = _tile(T, 256)
    scale = d ** -0.5

    def body(q_ref, k_ref, v_ref, o_ref, lse_ref, do_ref, dq_ref, dk_ref, dv_ref):
        @pl.when(pl.program_id(1) == 0)
        def _():
            dk_ref[...] = jnp.zeros_like(dk_ref)
            dv_ref[...] = jnp.zeros_like(dv_ref)

        qv, kv = q_ref[0], k_ref[0]
        dov = do_ref[0]
        dob = dov.astype(BF16)
        delta_row = _col_to_row(jnp.sum(dov * o_ref[0], axis=-1, keepdims=True))
        pt = jnp.exp(_dot(kv, qv, NT) * scale - lse_ref[0])
        dst = (pt * (_dot(v_ref[0], dob, NT) - delta_row) * scale).astype(BF16)
        dv_ref[0] += _dot(pt.astype(BF16), dob)
        dk_ref[0] += _dot(dst, qv)
        dq_ref[0] = _dot(dst, kv, TN)

    qb = pl.BlockSpec((1, tq, d), lambda h, i: (h, i, 0))
    kb = pl.BlockSpec((1, T, d), lambda h, i: (h, 0, 0))
    vb = pl.BlockSpec((1, T, dv), lambda h, i: (h, 0, 0))
    ob = pl.BlockSpec((1, tq, dv), lambda h, i: (h, i, 0))
    return _call(
        body, name=name, grid=(H, T // tq),
        in_specs=[qb, kb, vb, ob, pl.BlockSpec((1, 1, tq), lambda h, i: (h, 0, i)), ob],
        out_specs=[qb, kb, vb],
        out_shape=[SDS((H, T, d), F32), SDS((H, T, d), F32), SDS((H, T, dv), F32)],
        args=(q, k, v, o, lse, do), exch=exch)


def _silu_parts(x):
    s = _sigmoid(x)
    return x * s, s * (1.0 + x * (1.0 - s))


def _mix_out_fwd(x1, y_na, o_gla, y_mla, z, gr_col, gnorm, wbr, w_out, name):
    T, D = x1.shape
    W = y_na.shape[0] * y_na.shape[2]
    tm = _tile(T, 512)

    def body(x_ref, na_ref, gla_ref, mla_ref, gt_ref, gr_ref, gn_ref, w0, w1, w2, wo_ref, o_ref):
        gn = gn_ref[...]
        nrm = jnp.concatenate([gla_ref[h] * _rstd(gla_ref[h]) * gn for h in range(GLA_HEADS)], axis=1)
        y_gla = nrm * _silu_parts(gr_ref[...])[0]
        y_na = jnp.concatenate([na_ref[h] for h in range(NA_HEADS)], axis=1)
        y_mla = jnp.concatenate([mla_ref[h] for h in range(MLA_HEADS)], axis=1)
        mixed = jnp.zeros((tm, D), F32)
        for i, (y, w_ref) in enumerate(((y_na, w0), (y_gla, w1), (y_mla, w2))):
            mixed = mixed + _sigmoid(gt_ref[:, i * D:(i + 1) * D]) * _dot(y.astype(BF16), w_ref[...])
        o_ref[...] = x_ref[...] + _dot(mixed.astype(BF16), wo_ref[...])

    tok = lambda w: pl.BlockSpec((tm, w), lambda i: (i, 0))
    hm = lambda a: pl.BlockSpec((a.shape[0], tm, a.shape[2]), lambda i: (0, i, 0))
    full = lambda shape: pl.BlockSpec(shape, lambda i: (0, 0))
    return pl.pallas_call(
        body, name=name, grid=(T // tm,),
        in_specs=[tok(D), hm(y_na), hm(o_gla), hm(y_mla), tok(3 * D), pl.BlockSpec((tm, W), lambda i: (i, gr_col)),
                  full((1, GLA_DV)), full((W, D)), full((W, D)), full((W, D)), full((D, D))],
        out_specs=tok(D), out_shape=SDS((T, D), F32),
        compiler_params=_params(1),
    )(x1, y_na, o_gla, y_mla, z, z, gnorm, wbr[0], wbr[1], wbr[2], w_out)


def _mix_out_bwd(dx2, y_na, o_gla, y_mla, z, gr_col, gnorm, wbr, w_out, name):
    T, D = dx2.shape
    W = y_na.shape[0] * y_na.shape[2]
    tm = _tile(T, 256)

    def body(dx_ref, na_ref, gla_ref, mla_ref, gt_ref, gr_ref, gn_ref, w0, w1, w2, wo_ref,
             dna_ref, dgla_ref, dmla_ref, dgr_ref, dgt_ref, dw0, dw1, dw2, dwo_ref, dgn_ref):
        @pl.when(pl.program_id(0) == 0)
        def _():
            for r in (dw0, dw1, dw2, dwo_ref, dgn_ref):
                r[...] = jnp.zeros_like(r)

        gn = gn_ref[...]
        grv = gr_ref[...]
        sil, dsil = _silu_parts(grv)
        rs = [_rstd(gla_ref[h]) for h in range(GLA_HEADS)]
        nrm = jnp.concatenate([gla_ref[h] * rs[h] * gn for h in range(GLA_HEADS)], axis=1)
        ys = (jnp.concatenate([na_ref[h] for h in range(NA_HEADS)], axis=1).astype(BF16), (nrm * sil).astype(BF16),
              jnp.concatenate([mla_ref[h] for h in range(MLA_HEADS)], axis=1).astype(BF16))
        dxb = dx_ref[...].astype(BF16)
        dmixed = _dot(dxb, wo_ref[...], NT)
        mixed = jnp.zeros((tm, D), F32)
        dys = []
        for i, (y, w_ref, dw_ref) in enumerate(zip(ys, (w0, w1, w2), (dw0, dw1, dw2))):
            gt = _sigmoid(gt_ref[:, i * D:(i + 1) * D])
            p = _dot(y, w_ref[...])
            mixed = mixed + gt * p
            dp = (dmixed * gt).astype(BF16)
            dgt_ref[:, i * D:(i + 1) * D] = (dmixed * p * gt * (1.0 - gt)).astype(BF16)
            dys.append(_dot(dp, w_ref[...], NT))
            dw_ref[...] += _dot(y, dp, TN)
        dwo_ref[...] += _dot(mixed.astype(BF16), dxb, TN)
        for h in range(NA_HEADS):
            dna_ref[h] = dys[0][:, h * NA_HD:(h + 1) * NA_HD]
        for h in range(MLA_HEADS):
            dmla_ref[h] = dys[2][:, h * MLA_V:(h + 1) * MLA_V]
        dgr_ref[...] = dys[1] * nrm * dsil
        dn = dys[1] * sil
        dgn = jnp.zeros((1, GLA_DV), F32)
        for h in range(GLA_HEADS):
            dxh, dgh = _rms_bwd(dn[:, h * GLA_DV:(h + 1) * GLA_DV], gla_ref[h], gn, rs[h])
            dgla_ref[h] = dxh
            dgn = dgn + dgh
        dgn_ref[...] += dgn

    tok = lambda w: pl.BlockSpec((tm, w), lambda i: (i, 0))
    hm = lambda a: pl.BlockSpec((a.shape[0], tm, a.shape[2]), lambda i: (0, i, 0))
    full = lambda shape: pl.BlockSpec(shape, lambda i: (0, 0))
    return pl.pallas_call(
        body, name=name, grid=(T // tm,),
        in_specs=[tok(D), hm(y_na), hm(o_gla), hm(y_mla), tok(3 * D), pl.BlockSpec((tm, W), lambda i: (i, gr_col)),
                  full((1, GLA_DV)), full((W, D)), full((W, D)), full((W, D)), full((D, D))],
        out_specs=[hm(y_na), hm(o_gla), hm(y_mla), tok(W), tok(3 * D), full((W, D)), full((W, D)), full((W, D)),
                   full((D, D)), full((1, GLA_DV))],
        out_shape=[SDS(y_na.shape, F32), SDS(o_gla.shape, F32), SDS(y_mla.shape, F32), SDS((T, W), F32),
                   SDS((T, 3 * D), BF16)] + [SDS((W, D), F32)] * 3 + [SDS((D, D), F32), SDS((1, GLA_DV), F32)],
        compiler_params=_params(1),
    )(dx2, y_na, o_gla, y_mla, z, z, gnorm, wbr[0], wbr[1], wbr[2], w_out)


def _split_z(z, lay, name):
    T = z.shape[0]
    tm = _tile(T, 512)
    na0, gv0, gq0 = lay['na_q'][0], lay['gv'][0], lay['gq'][0]
    assert na0 % (3 * NA_W) == 0 and gv0 % GLA_V_W == 0 and gq0 % (2 * GLA_QK_W) == 0
    assert lay['na_k'][0] == na0 + NA_W and lay['na_v'][0] == na0 + 2 * NA_W and lay['gk'][0] == gq0 + GLA_QK_W

    def body(na_ref, gv_ref, gqk_ref, q_ref, k_ref, v_ref, gq_ref, gk_ref, gvo_ref):
        for j, o_ref in enumerate((q_ref, k_ref, v_ref)):
            for h in range(NA_HEADS):
                o_ref[h] = na_ref[:, j * NA_W + h * NA_HD:j * NA_W + (h + 1) * NA_HD]
        for j, o_ref in enumerate((gq_ref, gk_ref)):
            for h in range(GLA_HEADS):
                o_ref[h] = gqk_ref[:, j * GLA_QK_W + h * GLA_DK:j * GLA_QK_W + (h + 1) * GLA_DK]
        for h in range(GLA_HEADS):
            gvo_ref[h] = gv_ref[:, h * GLA_DV:(h + 1) * GLA_DV]

    hm = lambda H, d: pl.BlockSpec((H, tm, d), lambda i: (0, i, 0))
    return pl.pallas_call(
        body, name=name, grid=(T // tm,),
        in_specs=[pl.BlockSpec((tm, 3 * NA_W), lambda i: (i, na0 // (3 * NA_W))),
                  pl.BlockSpec((tm, GLA_V_W), lambda i: (i, gv0 // GLA_V_W)),
                  pl.BlockSpec((tm, 2 * GLA_QK_W), lambda i: (i, gq0 // (2 * GLA_QK_W)))],
        out_specs=[hm(NA_HEADS, NA_HD)] * 3 + [hm(GLA_HEADS, GLA_DK)] * 2 + [hm(GLA_HEADS, GLA_DV)],
        out_shape=[SDS((NA_HEADS, T, NA_HD), F32)] * 3 + [SDS((GLA_HEADS, T, GLA_DK), F32)] * 2
        + [SDS((GLA_HEADS, T, GLA_DV), F32)],
        compiler_params=_params(1),
    )(z, z, z)


def _assemble_dz(dgates, na_dqkv, gla_dqk, gla_dv, dgr, dcq, dckv, dglow, dkr, lay, zp, name):
    T = dgr.shape[0]
    tm = _tile(T, 256)

    def body(dgt_ref, nq_ref, nk_ref, nv_ref, gq_ref, gk_ref, gv_ref, dgr_ref, dcq_ref, dckv_ref, dglow_ref, dkr_ref, dz_ref):
        def put(seg, off, val):
            a = lay[seg][0] + off
            dz_ref[:, a:a + val.shape[1]] = val.astype(BF16)

        put('gates', 0, dgt_ref[...])
        for seg, ref in (('na_q', nq_ref), ('na_k', nk_ref), ('na_v', nv_ref)):
            for h in range(NA_HEADS):
                put(seg, h * NA_HD, ref[h])
        for seg, ref in (('gq', gq_ref), ('gk', gk_ref)):
            for h in range(GLA_HEADS):
                put(seg, h * GLA_DK, ref[h])
        for h in range(GLA_HEADS):
            put('gv', h * GLA_DV, gv_ref[h])
        put('gr', 0, dgr_ref[...])
        put('c_q', 0, dcq_ref[...])
        put('c_kv', 0, dckv_ref[...])
        put('glow', 0, dglow_ref[...])
        put('k_rope', 0, dkr_ref[...])

    tok = lambda a: pl.BlockSpec((tm, a.shape[1]), lambda i: (i, 0))
    hm = lambda a: pl.BlockSpec((a.shape[0], tm, a.shape[2]), lambda i: (0, i, 0))
    args = (dgates, *na_dqkv, *gla_dqk, gla_dv, dgr, dcq, dckv, dglow, dkr)
    return pl.pallas_call(
        body, name=name, grid=(T // tm,),
        in_specs=[tok(dgates)] + [hm(a) for a in (*na_dqkv, *gla_dqk, gla_dv)] + [tok(dgr), tok(dcq), tok(dckv), tok(dglow),
                                                                                 tok(dkr)],
        out_specs=pl.BlockSpec((tm, zp), lambda i: (i, 0)), out_shape=SDS((T, zp), BF16),
        compiler_params=_params(1),
    )(*args)


def _loss_head(y, target, name):
    T, D = y.shape
    tm = _tile(T, 1024)

    def body(y_ref, t_ref, dy_ref, l_ref):
        @pl.when(pl.program_id(0) == 0)
        def _():
            l_ref[...] = jnp.zeros_like(l_ref)

        e = y_ref[...] - t_ref[...]
        dy_ref[...] = e * (1.0 / D)
        l_ref[...] += 0.5 * jnp.sum(jnp.mean(e * e, axis=-1, keepdims=True), axis=0, keepdims=True)

    tok = pl.BlockSpec((tm, D), lambda i: (i, 0))
    return pl.pallas_call(
        body, name=name, grid=(T // tm,), in_specs=[tok, tok],
        out_specs=[tok, pl.BlockSpec((1, 1), lambda i: (0, 0))],
        out_shape=[SDS((T, D), F32), SDS((1, 1), F32)],
        compiler_params=_params(1),
    )(y, target)


def _exchange_copies(src, dst, send_sems, recv_sems, local_sems, broadcast, with_recvs=True):
    n = len(src)
    x, y, c = lax.axis_index("x"), lax.axis_index("y"), lax.axis_index("c")
    me = 4 * x + 2 * y + c
    local = [pltpu.make_async_copy(src[a] if broadcast else src[a].at[me], dst[a].at[me], local_sems.at[a])
             for a in range(n)]
    sends, recvs = [], []
    for d in range(1, N_DEV):
        px = 1 - x if d & 4 else x
        py = 1 - y if d & 2 else y
        pc = 1 - c if d & 1 else c
        peer = 4 * px + 2 * py + pc
        for a in range(n):
            for out, slot in ((sends, me), (recvs, peer)) if with_recvs else ((sends, me),):
                out.append(pltpu.make_async_remote_copy(
                    src_ref=src[a] if broadcast else src[a].at[peer], dst_ref=dst[a].at[slot],
                    send_sem=send_sems.at[a, d - 1], recv_sem=recv_sems.at[a, d - 1],
                    device_id=(px, py, pc), device_id_type=pl.DeviceIdType.MESH))
    return local, sends, recvs


def _gather_copies(src, dst, send_sems, recv_sems, local_sems, with_local=True):
    x, y, c = lax.axis_index("x"), lax.axis_index("y"), lax.axis_index("c")
    chips = [(1 - x, y), (x, 1 - y), (1 - x, 1 - y)]
    lin = lambda px, py, pc: 4 * px + 2 * py + pc

    def copy(a, k, incoming=False):
        if k == 0:
            to, out_src, out_slot, in_slot = (x, y, 1 - c), src[a], lin(x, y, c), lin(x, y, 1 - c)
        elif k <= 3:
            to, out_src, out_slot, in_slot = (*chips[k - 1], c), src[a], lin(x, y, c), lin(*chips[k - 1], c)
        else:
            slot = lin(*chips[k - 4], c)
            to, out_src, out_slot, in_slot = (x, y, 1 - c), dst[a].at[slot], slot, lin(*chips[k - 4], 1 - c)
        return pltpu.make_async_remote_copy(
            src_ref=out_src, dst_ref=dst[a].at[in_slot if incoming else out_slot], send_sem=send_sems.at[a, k],
            recv_sem=recv_sems.at[a, k], device_id=to, device_id_type=pl.DeviceIdType.MESH)

    local = [pltpu.make_async_copy(src[a], dst[a].at[lin(x, y, c)], local_sems.at[a])
             for a in range(len(src) if with_local else 0)]
    return copy, local


def _exchange_start(src, dst, send_sems, recv_sems, local_sems, broadcast):
    if broadcast:
        copy, local = _gather_copies(src, dst, send_sems, recv_sems, local_sems)
        for cp in local:
            cp.start()
        for k in range(4):
            for a in range(len(src)):
                copy(a, k).start()
        return
    local, sends, _ = _exchange_copies(src, dst, send_sems, recv_sems, local_sems, broadcast, with_recvs=False)
    for cp in local + sends:
        cp.start()


def _exchange_pass_on(src, dst, send_sems, recv_sems, local_sems, broadcast):
    if broadcast:
        copy, _ = _gather_copies(src, dst, send_sems, recv_sems, local_sems, with_local=False)
        for k in range(1, 4):
            for a in range(len(src)):
                copy(a, k, True).wait_recv()
                copy(a, k + 3).start()


def _exchange_wait(src, dst, send_sems, recv_sems, local_sems, broadcast):
    if broadcast:
        copy, local = _gather_copies(src, dst, send_sems, recv_sems, local_sems)
        n = len(src)
        for k in (0, 4, 5, 6):
            for a in range(n):
                copy(a, k, True).wait_recv()
        for k in range(N_DEV - 1):
            for a in range(n):
                copy(a, k).wait_send()
        for cp in local:
            cp.wait()
        return
    local, sends, recvs = _exchange_copies(src, dst, send_sems, recv_sems, local_sems, broadcast)
    for cp in recvs:
        cp.wait_recv()
    for cp in sends:
        cp.wait_send()
    for cp in local:
        cp.wait()


def _exchange_shapes(srcs, broadcast):
    n = len(srcs)
    out_shape = [SDS((N_DEV,) + (tuple(s.shape) if broadcast else tuple(s.shape[1:])), s.dtype) for s in srcs]
    sems = [pltpu.SemaphoreType.DMA((n, N_DEV - 1)), pltpu.SemaphoreType.DMA((n, N_DEV - 1)),
            pltpu.SemaphoreType.DMA((n,))]
    return out_shape, sems


def _exchange(srcs, broadcast, name):
    n = len(srcs)
    out_shape, sems = _exchange_shapes(srcs, broadcast)

    def body(*refs):
        mode = (refs[:n], refs[n:2 * n]) + tuple(refs[2 * n:]) + (broadcast,)
        _exchange_start(*mode)
        _exchange_pass_on(*mode)
        _exchange_wait(*mode)

    hbm = pl.BlockSpec(memory_space=pltpu.HBM)
    return pl.pallas_call(body, name=name, in_specs=[hbm] * n, out_specs=[hbm] * n, out_shape=out_shape,
                          scratch_shapes=sems)(*srcs)


def _call(body, *, name, grid, in_specs, out_specs, out_shape, args, scratch_shapes=(), exch=None):
    params = _params(len(grid))
    if exch is None:
        return pl.pallas_call(body, name=name, grid=grid, in_specs=in_specs, out_specs=out_specs, out_shape=out_shape,
                              scratch_shapes=list(scratch_shapes), compiler_params=params)(*args)
    srcs, broadcast = exch
    n, n_in, n_out, n_scr = len(srcs), len(args), len(out_shape), len(scratch_shapes)
    x_shape, sems = _exchange_shapes(srcs, broadcast)

    def carrier(*refs):
        ins, x_src = refs[:n_in], refs[n_in:n_in + n]
        outs, x_dst = refs[n_in + n:n_in + n + n_out], refs[n_in + n + n_out:n_in + 2 * n + n_out]
        scr = refs[n_in + 2 * n + n_out:n_in + 2 * n + n_out + n_scr]
        mode = (x_src, x_dst) + tuple(refs[n_in + 2 * n + n_out + n_scr:]) + (broadcast,)
        step = 0
        for a, g in enumerate(grid):
            step = step * g + pl.program_id(a)
        steps = int(np.prod(grid))
        pl.when(step == 0)(lambda: _exchange_start(*mode))
        body(*ins, *outs, *scr)
        pl.when(step == (3 * steps) // 4)(lambda: _exchange_pass_on(*mode))
        pl.when(step == steps - 1)(lambda: _exchange_wait(*mode))

    hbm = pl.BlockSpec(memory_space=pltpu.HBM)
    res = pl.pallas_call(carrier, name=name, grid=grid, in_specs=list(in_specs) + [hbm] * n,
                         out_specs=list(out_specs) + [hbm] * n, out_shape=list(out_shape) + x_shape,
                         scratch_shapes=list(scratch_shapes) + sems, compiler_params=params)(*args, *srcs)
    return res[:n_out], res[n_out:]


def _adam_math(w, g, m, v):
    m = ADAM_B1 * m + (1.0 - ADAM_B1) * g
    v = ADAM_B2 * v + (1.0 - ADAM_B2) * (g * g)
    m_hat = m / (1.0 - ADAM_B1 ** ADAM_STEP)
    v_hat = v / (1.0 - ADAM_B2 ** ADAM_STEP)
    delta = -ADAM_LR * (m_hat / (jnp.sqrt(v_hat) + ADAM_EPS) + ADAM_WD * w)
    return delta, m, v


def _adamw_sharded(landed, k, w, m, v, name):
    L, r, c = w.shape
    lanes = -(-c // LANE) * LANE
    rb = _tile(r, max(16, (1 << 22) // (N_DEV * lanes * 2)), 16)

    def body(*refs):
        l_refs = refs[:L]
        w_ref, m_ref, v_ref, g_ref, d_ref, nm_ref, nv_ref = refs[L:]
        for j in range(L):
            @pl.when(pl.program_id(0) == j)
            def _(j=j):
                g = l_refs[j][0, 0].astype(F32)
                for s in range(1, N_DEV):
                    g = g + l_refs[j][s, 0].astype(F32)
                delta, nm, nv = _adam_math(w_ref[0], g, m_ref[0], v_ref[0])
                g_ref[0] = g
                d_ref[0] = delta
                nm_ref[0] = nm
                nv_ref[0] = nv

    blk = pl.BlockSpec((1, rb, c), lambda l, i: (l, i, 0))
    land = [pl.BlockSpec((N_DEV, 1, rb, c), lambda l, i, j=j: (0, k, jnp.where(l == j, i, 0), 0)) for j in range(L)]
    return pl.pallas_call(
        body, name=name, grid=(L, r // rb),
        in_specs=land + [blk, blk, blk],
        out_specs=[blk] * 4, out_shape=[SDS((L, r, c), F32)] * 4,
        compiler_params=_params(2),
    )(*landed, w, m, v)


def _adamw_replicated(landed, w, m, v, name):
    R = w.shape[0]

    def body(l_ref, w_ref, m_ref, v_ref, g_ref, d_ref, nm_ref, nv_ref):
        g = l_ref[0]
        for s in range(1, N_DEV):
            g = g + l_ref[s]
        delta, nm, nv = _adam_math(w_ref[...], g, m_ref[...], v_ref[...])
        g_ref[...] = g
        d_ref[...] = delta
        nm_ref[...] = nm
        nv_ref[...] = nv

    blk = pl.BlockSpec((R, LANE), lambda i: (0, 0))
    return pl.pallas_call(
        body, name=name, grid=(1,),
        in_specs=[pl.BlockSpec((N_DEV, R, LANE), lambda i: (0, 0, 0)), blk, blk, blk],
        out_specs=[blk] * 4, out_shape=[SDS((R, LANE), F32)] * 4,
        compiler_params=_params(1),
    )(landed, w, m, v)


def _pack(parts):
    flat = jnp.concatenate([p.reshape(-1) for p in parts])
    rows = -(-flat.shape[0] // (8 * LANE)) * 8
    return jnp.pad(flat, (0, rows * LANE - flat.shape[0])).reshape(rows, LANE)


def _unpack(packed, shapes):
    flat = packed.reshape(-1)
    out, off = [], 0
    for s in shapes:
        size = int(np.prod(s))
        out.append(flat[off:off + size].reshape(s))
        off += size
    return out


def _layer_fwd(x0, w, l, lay, tabs, carry):
    T, D = x0.shape
    ctab, stab, na_bias = tabs
    col = lambda name: lay[name][0] // lay[name][1]
    sv = {'x0': x0}
    x1, sv['a1'], sv['b1'] = _carried(_ffn_fwd, carry, 'ffn1', x0, w['ffn1_norm'], w['ffn1_w1'], w['ffn1_w3'],
                                      w['ffn1_w2'], f"ffn1_fwd_{l}")
    z, sv['hb'] = _carried(_norm_matmul, carry, 'mix_in', x1, 0, w['mix_norm'], w['w_in_z'], f"mix_in_{l}")
    sv['na_q'], sv['na_k'], sv['na_v'], sv['gq'], sv['gk'], sv['gv'] = _split_z(z, lay, f"split_z_{l}")
    y_na = _carried(_na_fwd, carry, 'na', sv['na_q'], sv['na_k'], sv['na_v'], na_bias, l, w['na_q_norm'], w['na_k_norm'],
                    f"na_fwd_{l}")[0]
    sv['gg'] = _gla_gate_fwd(z, col('glow'), w['up_f'], w['up_b'], w['gla_gf_bias'], w['gla_gb_bias'], f"gla_gate_{l}")
    o_gla = _gla_fwd(sv['gq'], sv['gk'], sv['gv'], sv['gg'], f"gla_fwd_{l}")
    sv['q_raw'], sv['hq'] = _norm_matmul(z, col('c_q'), w['mla_cq_norm'], w['mla_w_uq'], f"mla_uq_{l}")
    sv['kv_raw'], sv['hkv'] = _norm_matmul(z, col('c_kv'), w['mla_ckv_norm'], w['mla_w_ukv'], f"mla_ukv_{l}")
    sv['mq'] = _mla_qprep_fwd(sv['q_raw'], w['mla_q_norm'], ctab, stab, f"mla_qprep_{l}")
    sv['mk'], sv['mv'] = _mla_kprep_fwd(sv['kv_raw'], z, col('k_rope'), w['mla_k_norm'], ctab, stab, f"mla_kprep_{l}")
    y_mla, sv['lse'] = _carried(_mla_attn_fwd, carry, 'mla', sv['mq'], sv['mk'], sv['mv'], f"mla_attn_{l}")
    w_br = (w['w_br_na'], w['w_br_gla'], w['w_br_mla'])
    x2 = _mix_out_fwd(x1, y_na, o_gla, y_mla, z, col('gr'), w['gla_out_norm'], w_br, w['w_out'], f"mix_out_{l}")
    sv.update(x1=x1, z=z, y_na=y_na, o_gla=o_gla, y_mla=y_mla, x2=x2)
    x3, sv['a2'], sv['b2'] = _carried(_ffn_fwd, carry, 'ffn2', x2, w['ffn2_norm'], w['ffn2_w1'], w['ffn2_w3'],
                                      w['ffn2_w2'], f"ffn2_fwd_{l}")
    return x3, sv


def _ffn_grads(grads, x, dy, a, b, w, pre, l, carry):
    dx, da, db, u, hb, dyb, dg = _carried(_ffn_bwd, carry, pre, x, w[pre + '_norm'], dy, a, b, w[pre + '_w1'],
                                          w[pre + '_w3'], w[pre + '_w2'], f"{pre}_bwd_{l}")
    grads[pre + '_norm'] = dg
    grads[pre + '_w1'] = _matmul_tn(hb, da, f"{pre}_dw1_{l}")
    grads[pre + '_w3'] = _carried(_matmul_tn, carry, pre + '_dw3', hb, db, f"{pre}_dw3_{l}")
    grads[pre + '_w2'] = _carried(_matmul_tn, carry, pre + '_dw2', u, dyb, f"{pre}_dw2_{l}")
    return dx


def _layer_bwd(grads, dx3, sv, w, l, lay, tabs, carry):
    T, D = dx3.shape
    ctab, stab, na_bias, onehot = tabs
    col = lambda name: lay[name][0] // lay[name][1]
    z = sv['z']
    dx2 = _ffn_grads(grads, sv['x2'], dx3, sv['a2'], sv['b2'], w, 'ffn2', l, carry)
    w_br = (w['w_br_na'], w['w_br_gla'], w['w_br_mla'])
    (dy_na, do_gla, dy_mla, dgr, dgates, dw0, dw1, dw2, dwo, dgn) = _mix_out_bwd(
        dx2, sv['y_na'], sv['o_gla'], sv['y_mla'], z, col('gr'), w['gla_out_norm'], w_br, w['w_out'], f"mix_out_bwd_{l}")
    grads.update(w_br_na=dw0, w_br_gla=dw1, w_br_mla=dw2, w_out=dwo, gla_out_norm=dgn)
    dq, dk, dv, dbias, dgq, dgk = _na_bwd(sv['na_q'], sv['na_k'], sv['na_v'], dy_na, na_bias, l,
                                          w['na_q_norm'], w['na_k_norm'], f"na_bwd_{l}")
    grads.update(na_q_norm=dgq, na_k_norm=dgk, na_rpb=_na_bias_tables_bwd(dbias, onehot))
    gdq, gdk, gdv, gdg = _gla_bwd(sv['gq'], sv['gk'], sv['gv'], sv['gg'], do_gla, f"gla_bwd_{l}")
    dglow, dupf, dupb, dbf, dbb = _gla_gate_bwd(z, col('glow'), w['up_f'], w['up_b'], w['gla_gf_bias'], w['gla_gb_bias'],
                                                gdg, f"gla_gate_bwd_{l}")
    grads.update(gla_gf_up=dupf[:GLA_RANK], gla_gb_up=dupb[GLA_RANK:2 * GLA_RANK], gla_gf_bias=dbf, gla_gb_bias=dbb)
    mdq, mdk, mdv = _carried(_mla_attn_bwd, carry, 'mla', sv['mq'], sv['mk'], sv['mv'], sv['y_mla'], sv['lse'], dy_mla,
                             f"mla_attn_bwd_{l}")
    dq_raw, dgqn = _mla_qprep_bwd(sv['q_raw'], w['mla_q_norm'], ctab, stab, mdq, f"mla_qprep_bwd_{l}")
    dkv_raw, dkr, dgkn = _mla_kprep_bwd(sv['kv_raw'], z, col('k_rope'), w['mla_k_norm'], ctab, stab, mdk, mdv,
                                        f"mla_kprep_bwd_{l}")
    dcq, dgcq = _norm_matmul_bwd(dq_raw, w['mla_w_uq'], z, col('c_q'), w['mla_cq_norm'], None, f"mla_uq_bwd_{l}")
    dckv, dgckv = _norm_matmul_bwd(dkv_raw, w['mla_w_ukv'], z, col('c_kv'), w['mla_ckv_norm'], None, f"mla_ukv_bwd_{l}")
    grads.update(mla_q_norm=dgqn, mla_k_norm=dgkn, mla_cq_norm=dgcq, mla_ckv_norm=dgckv,
                 mla_w_uq=_matmul_tn(sv['hq'], dq_raw, f"mla_duq_{l}", F32),
                 mla_w_ukv=_matmul_tn(sv['hkv'], dkv_raw, f"mla_dukv_{l}", F32))
    dz = _assemble_dz(dgates, (dq, dk, dv), (gdq, gdk), gdv, dgr, dcq, dckv, dglow, dkr, lay, z.shape[1], f"dz_{l}")
    dx1, dgmix = _carried(_norm_matmul_bwd, carry, 'mix_in', dz, w['w_in_z'], sv['x1'], 0, w['mix_norm'], dx2,
                          f"mix_in_bwd_{l}")
    grads.update(mix_norm=dgmix, w_in=_matmul_tn(sv['hb'], dz, f"mix_dw_in_{l}"))
    return _ffn_grads(grads, sv['x0'], dx1, sv['a1'], sv['b1'], w, 'ffn1', l, carry)


EXCHANGE_PARTS = {
    'F1w1': [['ffn1_w1']],
    'F1w3': [['ffn1_w3']],
    'F1b': [['ffn1_w2']],
    'F2': [['ffn2_w1', 'ffn2_w3'], ['ffn2_w2']],
    'C': [['w_in']],
    'S': [['w_br_na', 'w_br_gla', 'w_br_mla'], ['w_out'], ['mla_w_uq'], ['mla_w_ukv'], ['gla_gf_up', 'gla_gb_up']],
}
FWD_PLAN = {'ffn1': [(0, 'C')], 'mix_in': [(0, 'S')], 'mla': [(0, 'F2')], 'na': [(1, 'F1b')],
            'ffn2': [(1, 'F1w1'), (1, 'F1w3')]}
BWD_PLAN = {'mla': [(0, 'F2')], 'mix_in': [(0, 'S')], 'ffn1': [(0, 'C')], 'ffn2': [(1, 'F1w1'), (1, 'F1w3'), (1, 'F1b')]}
BWD_PLAN_LAYER0 = {'ffn1_dw3': [(0, 'F1w1')], 'ffn1_dw2': [(0, 'F1w3')]}
FWD_EDGE = ['F1w1', 'F1w3', 'F1b']
BWD_EDGE = ['F1b']


def _carried(fn, carry, key, *args):
    if key in carry:
        make_srcs, broadcast, deliver = carry[key]
        outs, landed = fn(*args, exch=(make_srcs(), broadcast))
        deliver(landed)
        return outs
    return fn(*args)


def kernel(x, ffn1_norm, ffn1_w1, ffn1_w3, ffn1_w2, mix_norm, w_in, na_q_norm, na_k_norm, na_rpb, gla_gf_up, gla_gf_bias, gla_gb_up, gla_gb_bias, gla_out_norm, mla_cq_norm, mla_ckv_norm, mla_w_uq, mla_w_ukv, mla_q_norm, mla_k_norm, w_br_na, w_br_gla, w_br_mla, w_out, ffn2_norm, ffn2_w1, ffn2_w3, ffn2_w2, loss_target, m_ffn1_norm, m_ffn1_w1, m_ffn1_w3, m_ffn1_w2, m_mix_norm, m_w_in, m_na_q_norm, m_na_k_norm, m_na_rpb, m_gla_gf_up, m_gla_gf_bias, m_gla_gb_up, m_gla_gb_bias, m_gla_out_norm, m_mla_cq_norm, m_mla_ckv_norm, m_mla_w_uq, m_mla_w_ukv, m_mla_q_norm, m_mla_k_norm, m_w_br_na, m_w_br_gla, m_w_br_mla, m_w_out, m_ffn2_norm, m_ffn2_w1, m_ffn2_w3, m_ffn2_w2, v_ffn1_norm, v_ffn1_w1, v_ffn1_w3, v_ffn1_w2, v_mix_norm, v_w_in, v_na_q_norm, v_na_k_norm, v_na_rpb, v_gla_gf_up, v_gla_gf_bias, v_gla_gb_up, v_gla_gb_bias, v_gla_out_norm, v_mla_cq_norm, v_mla_ckv_norm, v_mla_w_uq, v_mla_w_ukv, v_mla_q_norm, v_mla_k_norm, v_w_br_na, v_w_br_gla, v_w_br_mla, v_w_out, v_ffn2_norm, v_ffn2_w1, v_ffn2_w3, v_ffn2_w2):
    given = dict(locals())
    P = {n: given[n] for n in WEIGHTS}
    M = {n: given['m_' + n] for n in WEIGHTS}
    V = {n: given['v_' + n] for n in WEIGHTS}
    xs = x[0]
    T, D = xs.shape
    L = ffn1_norm.shape[0]
    lay, zp, d_in = _z_layout(D)

    def arrays_of(items):
        return [(l, names) for l, p in items for names in EXCHANGE_PARTS[p]]

    ws = [{n: P[n][l][None, :] for n in REPLICATED if n != 'na_rpb'} for l in range(L)]

    def gather_srcs(items):
        return [jnp.stack([P[n][l].astype(BF16) for n in names]) for l, names in arrays_of(items)]

    def take_gathered(items, res):
        for (l, names), g in zip(arrays_of(items), res):
            for k, n in enumerate(names):
                blk = g[:, k]
                if n == 'w_in':
                    ws[l]['w_in_z'] = _z_cols_from_blocks(blk, lay)
                    continue
                if n in COL_SHARDED:
                    full = jnp.concatenate([blk[d] for d in range(N_DEV)], axis=1)
                else:
                    full = blk.reshape(N_DEV * blk.shape[1], blk.shape[2])
                if n == 'gla_gf_up':
                    ws[l]['up_f'] = jnp.zeros((LANE, GLA_QK_W), BF16).at[:GLA_RANK].set(full)
                elif n == 'gla_gb_up':
                    ws[l]['up_b'] = jnp.zeros((LANE, GLA_QK_W), BF16).at[GLA_RANK:2 * GLA_RANK].set(full)
                else:
                    ws[l][n] = full

    def plan(table, l, make_srcs, broadcast, deliver):
        carry = {}
        for key, items in table.items():
            items = [(l + off, p) for off, p in items if l + off < L]
            if items:
                carry[key] = (functools.partial(make_srcs, items), broadcast, functools.partial(deliver, items))
        return carry

    ctab, stab = _rope_tables(T)
    na_bias, onehot = _na_bias_tables(na_rpb)

    edge = [(0, p) for p in FWD_EDGE]
    take_gathered(edge, _exchange(gather_srcs(edge), True, "gather_weights_edge"))
    saved = []
    h = xs
    for l in range(L):
        h, sv = _layer_fwd(h, ws[l], l, lay, (ctab, stab, na_bias), plan(FWD_PLAN, l, gather_srcs, True, take_gathered))
        saved.append(sv)
    dh, loss_local = _loss_head(h, loss_target[0], "loss_head")
    loss = lax.psum(loss_local[0, 0], ("x", "y", "c"))

    layer_grads = [dict() for _ in range(L)]

    def scatter_srcs(items):
        def blocks(l, n):
            g = layer_grads[l][n]
            if n == 'w_in':
                return _blocks_from_z_cols(g, lay, d_in).astype(BF16)
            if n in COL_SHARDED:
                c = g.shape[1] // N_DEV
                return jnp.stack([g[:, d * c:(d + 1) * c] for d in range(N_DEV)]).astype(BF16)
            return g.reshape(N_DEV, -1, g.shape[1]).astype(BF16)
        return [jnp.stack([blocks(l, n) for n in names], axis=1) for l, names in arrays_of(items)]

    landed = [dict() for _ in range(L)]

    def take_landed(items, res):
        for (l, names), r in zip(arrays_of(items), res):
            for k, n in enumerate(names):
                landed[l][n] = (r, k)

    for l in reversed(range(L)):
        table = {**BWD_PLAN, **BWD_PLAN_LAYER0} if l == 0 else BWD_PLAN
        dh = _layer_bwd(layer_grads[l], dh, saved[l], ws[l], l, lay, (ctab, stab, na_bias, onehot),
                        plan(table, l, scatter_srcs, False, take_landed))
    grad_x = dh[None]
    edge = [(0, p) for p in BWD_EDGE]
    take_landed(edge, _exchange(scatter_srcs(edge), False, "scatter_grads_edge"))

    out = {}
    for n in SHARDED:
        out[n] = _adamw_sharded([landed[l][n][0] for l in range(L)], landed[0][n][1], P[n], M[n], V[n], f"adamw_{n}")

    rep = [n for n in REPLICATED]
    rep_shapes = [tuple(P[n].shape) for n in rep]
    rep_partial = _pack([jnp.stack([layer_grads[l][n].reshape(P[n].shape[1:]) for l in range(L)]) for n in rep])
    (rep_landed,) = _exchange([rep_partial], True, "gather_small_grads")
    packed = _adamw_replicated(rep_landed, _pack([P[n] for n in rep]), _pack([M[n] for n in rep]),
                               _pack([V[n] for n in rep]), "adamw_replicated")
    for kind, pk in enumerate(packed):
        for n, arr in zip(rep, _unpack(pk, rep_shapes)):
            out.setdefault(n, [None] * 4)[kind] = arr

    res = [loss, grad_x]
    for kind in range(4):
        res += [out[n][kind] for n in WEIGHTS]
    return tuple(res)
```

```python
import functools

import numpy as np
import jax
import jax.numpy as jnp
from jax import lax
from jax.experimental import pallas as pl
from jax.experimental.pallas import tpu as pltpu

F32 = jnp.float32
BF16 = jnp.bfloat16
SDS = jax.ShapeDtypeStruct
HI = lax.Precision.HIGHEST

N_DEV = 8
DEPTH = 4
EPS = 1e-6
LANE = 128
VMEM_LIMIT = 56 * 1024 * 1024

GRID_W = 64
NA_HEADS, NA_HD, NA_WIN_R, NA_WIN_C = 8, 64, 8, 16
GLA_HEADS, GLA_DK, GLA_DV, GLA_RANK, GLA_TAU, GLA_CHUNK = 4, 64, 128, 16, 16.0, 64
MLA_HEADS, MLA_QR, MLA_KVR, MLA_NOPE, MLA_ROPE, MLA_V = 4, 256, 256, 128, 64, 128
MLA_QK = MLA_NOPE + MLA_ROPE
ROPE_THETA = 10000.0
NA_W = NA_HEADS * NA_HD
GLA_QK_W = GLA_HEADS * GLA_DK
GLA_V_W = GLA_HEADS * GLA_DV
MLA_V_W = MLA_HEADS * MLA_V

ADAM_LR, ADAM_B1, ADAM_B2, ADAM_EPS, ADAM_WD, ADAM_STEP = 0.001, 0.9, 0.999, 1e-08, 0.01, 10

WEIGHTS = ['ffn1_norm', 'ffn1_w1', 'ffn1_w3', 'ffn1_w2', 'mix_norm', 'w_in', 'na_q_norm', 'na_k_norm', 'na_rpb',
           'gla_gf_up', 'gla_gf_bias', 'gla_gb_up', 'gla_gb_bias', 'gla_out_norm', 'mla_cq_norm', 'mla_ckv_norm',
           'mla_w_uq', 'mla_w_ukv', 'mla_q_norm', 'mla_k_norm', 'w_br_na', 'w_br_gla', 'w_br_mla', 'w_out',
           'ffn2_norm', 'ffn2_w1', 'ffn2_w3', 'ffn2_w2']
COL_SHARDED = ['ffn1_w1', 'ffn1_w3', 'w_in', 'gla_gf_up', 'gla_gb_up', 'mla_w_uq', 'mla_w_ukv', 'w_br_na', 'w_br_gla',
               'w_br_mla', 'ffn2_w1', 'ffn2_w3']
ROW_SHARDED = ['ffn1_w2', 'w_out', 'ffn2_w2']
SHARDED = [n for n in WEIGHTS if n in COL_SHARDED or n in ROW_SHARDED]
REPLICATED = [n for n in WEIGHTS if n not in SHARDED]

NT = (((1,), (1,)), ((), ()))
TN = (((0,), (0,)), ((), ()))


def _tile(n, pref, mult=8):
    best = None
    for t in range(mult, min(n, pref) + 1, mult):
        if n % t == 0:
            best = t
    return best if best is not None else n


def _params(n_axes):
    return pltpu.CompilerParams(dimension_semantics=("arbitrary",) * n_axes, vmem_limit_bytes=VMEM_LIMIT)


def _dot(a, b, dims=None, precision=None):
    if dims is None:
        return jnp.dot(a, b, preferred_element_type=F32, precision=precision)
    return lax.dot_general(a, b, dims, preferred_element_type=F32, precision=precision)


def _sigmoid(x):
    return 1.0 / (1.0 + jnp.exp(-x))


def _rstd(x):
    return lax.rsqrt(jnp.mean(x * x, axis=-1, keepdims=True) + EPS)


def _rms_bwd(dy, x, g, r):
    xh = x * r
    dyg = dy * g
    dx = r * (dyg - xh * jnp.mean(dyg * xh, axis=-1, keepdims=True))
    return dx, jnp.sum(dy * xh, axis=0, keepdims=True)


def _z_layout(d_model):
    own = {}
    off = 0
    for name, w in [('na_q', NA_W), ('na_k', NA_W), ('na_v', NA_W), ('gq', GLA_QK_W), ('gk', GLA_QK_W),
                    ('gv', GLA_V_W), ('gr', GLA_V_W), ('gfl', GLA_RANK), ('gbl', GLA_RANK), ('c_q', MLA_QR),
                    ('c_kv', MLA_KVR), ('k_rope', MLA_ROPE), ('gates', 3 * d_model)]:
        own[name] = (off, w)
        off += w
    order = [('gates', 3 * d_model), ('na_q', NA_W), ('na_k', NA_W), ('na_v', NA_W), ('gv', GLA_V_W), ('gr', GLA_V_W),
             ('gq', GLA_QK_W), ('gk', GLA_QK_W), ('c_q', MLA_QR), ('c_kv', MLA_KVR), ('glow', LANE), ('k_rope', LANE)]
    lay = {}
    z = 0
    for name, w in order:
        assert z % w == 0, (name, z, w)
        if name == 'glow':
            lay[name] = (z, w, own['gfl'][0], 2 * GLA_RANK)
        else:
            lay[name] = (z, w, own[name][0], own[name][1])
        z += w
    return lay, z, off


def _z_cols_from_blocks(blocks, lay):
    cs = blocks.shape[2]
    parts = []
    for zo, zw, oo, ow in lay.values():
        a = oo
        while a < oo + ow:
            b = min(oo + ow, (a // cs + 1) * cs)
            parts.append(blocks[a // cs, :, a % cs:a % cs + (b - a)])
            a = b
        if zw > ow:
            parts.append(jnp.zeros((blocks.shape[1], zw - ow), blocks.dtype))
    return jnp.concatenate(parts, axis=1)


def _blocks_from_z_cols(wz, lay, d_in):
    cs = d_in // N_DEV
    own = sorted(lay.values(), key=lambda t: t[2])
    out = []
    for k in range(N_DEV):
        parts = []
        for zo, zw, oo, ow in own:
            a, b = max(oo, k * cs), min(oo + ow, (k + 1) * cs)
            if a < b:
                parts.append(wz[:, zo + a - oo:zo + b - oo])
        out.append(jnp.concatenate(parts, axis=1))
    return jnp.stack(out)


def _ffn_fwd(x, g, w1, w3, w2, name, exch=None):
    T, D = x.shape
    Fd = w1.shape[1]
    tm, tf = _tile(T, 1024), _tile(Fd, 256, LANE)
    nj = Fd // tf

    def body(x_ref, g_ref, w1_ref, w3_ref, w2_ref, o_ref, a_ref, b_ref, h_scr, acc):
        j = pl.program_id(1)

        @pl.when(j == 0)
        def _():
            xv = x_ref[...]
            h_scr[...] = (xv * _rstd(xv) * g_ref[...]).astype(BF16)
            acc[...] = jnp.zeros_like(acc)

        h = h_scr[...]
        a = _dot(h, w1_ref[...])
        b = _dot(h, w3_ref[...])
        a_ref[...] = a.astype(BF16)
        b_ref[...] = b.astype(BF16)
        u = a * _sigmoid(a) * b
        acc[...] += _dot(u.astype(BF16), w2_ref[...])

        @pl.when(j == nj - 1)
        def _():
            o_ref[...] = x_ref[...] + 0.5 * acc[...]

    return _call(
        body, name=name, grid=(T // tm, nj),
        in_specs=[pl.BlockSpec((tm, D), lambda i, j: (i, 0)), pl.BlockSpec((1, D), lambda i, j: (0, 0)),
                  pl.BlockSpec((D, tf), lambda i, j: (0, j)), pl.BlockSpec((D, tf), lambda i, j: (0, j)),
                  pl.BlockSpec((tf, D), lambda i, j: (j, 0))],
        out_specs=[pl.BlockSpec((tm, D), lambda i, j: (i, 0)), pl.BlockSpec((tm, tf), lambda i, j: (i, j)),
                   pl.BlockSpec((tm, tf), lambda i, j: (i, j))],
        out_shape=[SDS((T, D), F32), SDS((T, Fd), BF16), SDS((T, Fd), BF16)],
        scratch_shapes=[pltpu.VMEM((tm, D), BF16), pltpu.VMEM((tm, D), F32)],
        args=(x, g, w1, w3, w2), exch=exch)


def _ffn_bwd(x, g, dy, a, b, w1, w3, w2, name, exch=None):
    T, D = x.shape
    Fd = w1.shape[1]
    tm, tf = _tile(T, 1024), _tile(Fd, 256, LANE)
    nj = Fd // tf

    def body(x_ref, g_ref, dy_ref, a_ref, b_ref, w1_ref, w3_ref, w2_ref,
             dx_ref, da_ref, db_ref, u_ref, hb_ref, dyb_ref, dg_ref, dh_acc):
        i, j = pl.program_id(0), pl.program_id(1)

        @pl.when(j == 0)
        def _():
            xv = x_ref[...]
            hb_ref[...] = (xv * _rstd(xv) * g_ref[...]).astype(BF16)
            dyb_ref[...] = (0.5 * dy_ref[...]).astype(BF16)
            dh_acc[...] = jnp.zeros_like(dh_acc)

        @pl.when((i == 0) & (j == 0))
        def _():
            dg_ref[...] = jnp.zeros_like(dg_ref)

        du = _dot(dyb_ref[...], w2_ref[...], NT)
        av = a_ref[...].astype(F32)
        bv = b_ref[...].astype(F32)
        s = _sigmoid(av)
        sl = av * s
        da = (du * bv * (s * (1.0 + av * (1.0 - s)))).astype(BF16)
        db = (du * sl).astype(BF16)
        da_ref[...] = da
        db_ref[...] = db
        u_ref[...] = (sl * bv).astype(BF16)
        dh_acc[...] += _dot(da, w1_ref[...], NT) + _dot(db, w3_ref[...], NT)

        @pl.when(j == nj - 1)
        def _():
            xv = x_ref[...]
            dxn, dg = _rms_bwd(dh_acc[...], xv, g_ref[...], _rstd(xv))
            dx_ref[...] = dy_ref[...] + dxn
            dg_ref[...] += dg

    row = lambda i, j: (i, 0)
    return _call(
        body, name=name, grid=(T // tm, nj),
        in_specs=[pl.BlockSpec((tm, D), row), pl.BlockSpec((1, D), lambda i, j: (0, 0)), pl.BlockSpec((tm, D), row),
                  pl.BlockSpec((tm, tf), lambda i, j: (i, j)), pl.BlockSpec((tm, tf), lambda i, j: (i, j)),
                  pl.BlockSpec((D, tf), lambda i, j: (0, j)), pl.BlockSpec((D, tf), lambda i, j: (0, j)),
                  pl.BlockSpec((tf, D), lambda i, j: (j, 0))],
        out_specs=[pl.BlockSpec((tm, D), row), pl.BlockSpec((tm, tf), lambda i, j: (i, j)),
                   pl.BlockSpec((tm, tf), lambda i, j: (i, j)), pl.BlockSpec((tm, tf), lambda i, j: (i, j)),
                   pl.BlockSpec((tm, D), row), pl.BlockSpec((tm, D), row), pl.BlockSpec((1, D), lambda i, j: (0, 0))],
        out_shape=[SDS((T, D), F32), SDS((T, Fd), BF16), SDS((T, Fd), BF16), SDS((T, Fd), BF16),
                   SDS((T, D), BF16), SDS((T, D), BF16), SDS((1, D), F32)],
        scratch_shapes=[pltpu.VMEM((tm, D), F32)],
        args=(x, g, dy, a, b, w1, w3, w2), exch=exch)


def _matmul_tn(a, b, name, out_dtype=BF16, exch=None):
    T, K = a.shape
    N = b.shape[1]
    tk = _tile(K, 1408, LANE)
    tn = _tile(N, 1408, LANE)
    tt = _tile(T, 1024)
    nt = T // tt

    def body(a_ref, b_ref, o_ref, acc):
        t = pl.program_id(2)

        @pl.when(t == 0)
        def _():
            acc[...] = jnp.zeros_like(acc)

        acc[...] += _dot(a_ref[...], b_ref[...], TN)

        @pl.when(t == nt - 1)
        def _():
            o_ref[...] = acc[...].astype(out_dtype)

    res = _call(
        body, name=name, grid=(K // tk, N // tn, nt),
        in_specs=[pl.BlockSpec((tt, tk), lambda k, n, t: (t, k)), pl.BlockSpec((tt, tn), lambda k, n, t: (t, n))],
        out_specs=[pl.BlockSpec((tk, tn), lambda k, n, t: (k, n))],
        out_shape=[SDS((K, N), out_dtype)],
        scratch_shapes=[pltpu.VMEM((tk, tn), F32)],
        args=(a, b), exch=exch)
    return res[0] if exch is None else (res[0][0], res[1])


def _norm_matmul(x, xcol, g, w, name, exch=None):
    T = x.shape[0]
    D = g.shape[1]
    N = w.shape[1]
    tm, tn = _tile(T, 1024), _tile(N, 768, LANE)

    def body(x_ref, g_ref, w_ref, z_ref, hb_ref):
        @pl.when(pl.program_id(1) == 0)
        def _():
            xv = x_ref[...]
            hb_ref[...] = (xv * _rstd(xv) * g_ref[...]).astype(BF16)

        z_ref[...] = _dot(hb_ref[...], w_ref[...])

    return _call(
        body, name=name, grid=(T // tm, N // tn),
        in_specs=[pl.BlockSpec((tm, D), lambda i, j: (i, xcol)), pl.BlockSpec((1, D), lambda i, j: (0, 0)),
                  pl.BlockSpec((D, tn), lambda i, j: (0, j))],
        out_specs=[pl.BlockSpec((tm, tn), lambda i, j: (i, j)), pl.BlockSpec((tm, D), lambda i, j: (i, 0))],
        out_shape=[SDS((T, N), F32), SDS((T, D), BF16)],
        args=(x, g, w), exch=exch)


def _norm_matmul_bwd(dz, w, x, xcol, g, resid, name, exch=None):
    T, N = dz.shape
    D = g.shape[1]
    tm, tn = _tile(T, 1024), _tile(N, 768, LANE)
    nj = N // tn
    has_resid = resid is not None

    def body(*refs):
        if has_resid:
            dz_ref, w_ref, x_ref, g_ref, r_ref, dx_ref, dg_ref, acc = refs
        else:
            dz_ref, w_ref, x_ref, g_ref, dx_ref, dg_ref, acc = refs
        i, j = pl.program_id(0), pl.program_id(1)

        @pl.when(j == 0)
        def _():
            acc[...] = jnp.zeros_like(acc)

        @pl.when((i == 0) & (j == 0))
        def _():
            dg_ref[...] = jnp.zeros_like(dg_ref)

        acc[...] += _dot(dz_ref[...], w_ref[...], NT)

        @pl.when(j == nj - 1)
        def _():
            xv = x_ref[...]
            dxn, dg = _rms_bwd(acc[...], xv, g_ref[...], _rstd(xv))
            dx_ref[...] = dxn + r_ref[...] if has_resid else dxn
            dg_ref[...] += dg

    in_specs = [pl.BlockSpec((tm, tn), lambda i, j: (i, j)), pl.BlockSpec((D, tn), lambda i, j: (0, j)),
                pl.BlockSpec((tm, D), lambda i, j: (i, xcol)), pl.BlockSpec((1, D), lambda i, j: (0, 0))]
    args = [dz, w, x, g]
    if has_resid:
        in_specs.append(pl.BlockSpec((tm, D), lambda i, j: (i, 0)))
        args.append(resid)
    return _call(
        body, name=name, grid=(T // tm, nj), in_specs=in_specs,
        out_specs=[pl.BlockSpec((tm, D), lambda i, j: (i, 0)), pl.BlockSpec((1, D), lambda i, j: (0, 0))],
        out_shape=[SDS((T, D), F32), SDS((1, D), F32)],
        scratch_shapes=[pltpu.VMEM((tm, D), F32)],
        args=args, exch=exch)


def _na_bias_tables(rpb):
    L = rpb.shape[0]
    qc = np.arange(GRID_W)[:, None]
    kc = np.arange(GRID_W)[None, :]
    dc = np.clip(kc - qc + NA_WIN_C - 1, 0, 2 * NA_WIN_C - 2)
    c0 = np.clip(qc - NA_WIN_C // 2, 0, GRID_W - NA_WIN_C)
    ok = (kc >= c0) & (kc < c0 + NA_WIN_C)
    onehot = (dc.reshape(-1)[None, :] == np.arange(2 * NA_WIN_C - 1)[:, None]).astype(np.float32)
    t1 = jnp.stack([rpb[:, :, c:c + NA_WIN_R, :] for c in range(NA_WIN_R)], axis=2)
    full = jnp.dot(t1.reshape(-1, 2 * NA_WIN_C - 1), jnp.asarray(onehot), precision=HI)
    full = full.reshape(L, NA_HEADS, NA_WIN_R, NA_WIN_R, GRID_W, GRID_W)
    full = jnp.where(jnp.asarray(ok)[None, None, None, None], full, -1e30)
    return jnp.transpose(full, (0, 1, 2, 4, 3, 5)).reshape(L * NA_HEADS, NA_WIN_R, GRID_W, NA_WIN_R * GRID_W), onehot


def _na_bias_tables_bwd(db, onehot):
    d = db.reshape(NA_HEADS, NA_WIN_R, GRID_W, NA_WIN_R, GRID_W)
    d = jnp.transpose(d, (0, 1, 3, 2, 4)).reshape(-1, GRID_W * GRID_W)
    t1 = jnp.dot(d, jnp.asarray(onehot.T), precision=HI).reshape(NA_HEADS, NA_WIN_R, NA_WIN_R, 2 * NA_WIN_C - 1)
    out = jnp.zeros((NA_HEADS, 2 * NA_WIN_R - 1, 2 * NA_WIN_C - 1), F32)
    for c in range(NA_WIN_R):
        out = out.at[:, c:c + NA_WIN_R, :].add(t1[:, c])
    return out


def _na_windows(bi, rb, rows):
    wsl, cls = [], []
    for i in range(rb):
        r = bi * rb + i
        r0 = jnp.clip(r - NA_WIN_R // 2, 0, rows - NA_WIN_R)
        wsl.append(pl.ds(pl.multiple_of(r0 * GRID_W, GRID_W), NA_WIN_R * GRID_W))
        cls.append(r0 - r + NA_WIN_R - 1)
    return wsl, cls


def _na_fwd(q, k, v, bias, layer, gq, gk, name, exch=None):
    H, T, d = q.shape
    rows = T // GRID_W
    rb = _tile(rows, 8, 1)
    win = NA_WIN_R * GRID_W
    scale = d ** -0.5

    def body(q_ref, k_ref, v_ref, b_ref, gq_ref, gk_ref, o_ref, qn, kn, vb):
        qv, kv = q_ref[0], k_ref[0]
        qn[...] = (qv * _rstd(qv) * gq_ref[...] * scale).astype(BF16)
        kn[...] = (kv * _rstd(kv) * gk_ref[...]).astype(BF16)
        vb[...] = v_ref[0].astype(BF16)

        def block(bi, c):
            wsl, cls = _na_windows(bi, rb, rows)
            qsl = pl.ds(pl.multiple_of(bi * (rb * GRID_W), rb * GRID_W), rb * GRID_W)
            kw = jnp.stack([kn[w, :] for w in wsl])
            vw = jnp.stack([vb[w, :] for w in wsl])
            s = _bdot(qn[qsl, :].reshape(rb, GRID_W, d), kw, 2, 2) + jnp.stack([b_ref[0, c_] for c_ in cls])
            p = jnp.exp(s - jnp.max(s, axis=-1, keepdims=True))
            p = p / jnp.sum(p, axis=-1, keepdims=True)
            o_ref[0, qsl, :] = _bdot(p.astype(BF16), vw, 2, 1).reshape(rb * GRID_W, d)
            return c

        lax.fori_loop(0, rows // rb, block, 0)

    hm = pl.BlockSpec((1, T, d), lambda h: (h, 0, 0))
    gs = pl.BlockSpec((1, d), lambda h: (0, 0))
    return _call(
        body, name=name, grid=(H,),
        in_specs=[hm, hm, hm, pl.BlockSpec((1, NA_WIN_R, GRID_W, win), lambda h: (layer * H + h, 0, 0, 0)), gs, gs],
        out_specs=[hm], out_shape=[SDS((H, T, d), F32)],
        scratch_shapes=[pltpu.VMEM((T, d), BF16)] * 3,
        args=(q, k, v, bias, gq, gk), exch=exch)


def _na_bwd(q, k, v, do, bias, layer, gq, gk, name, exch=None):
    H, T, d = q.shape
    rows = T // GRID_W
    rb = _tile(rows, 8, 1)
    win = NA_WIN_R * GRID_W
    scale = d ** -0.5

    def body(q_ref, k_ref, v_ref, do_ref, b_ref, gq_ref, gk_ref,
             dq_ref, dk_ref, dv_ref, db_ref, dgq_ref, dgk_ref, qn, kn, vb, dob, dqn, dkn):
        h = pl.program_id(0)
        qv, kv = q_ref[0], k_ref[0]
        rq, rk = _rstd(qv), _rstd(kv)
        qn[...] = (qv * rq * gq_ref[...] * scale).astype(BF16)
        kn[...] = (kv * rk * gk_ref[...]).astype(BF16)
        vb[...] = v_ref[0].astype(BF16)
        dob[...] = do_ref[0].astype(BF16)
        dkn[...] = jnp.zeros_like(dkn)
        dv_ref[...] = jnp.zeros_like(dv_ref)
        db_ref[...] = jnp.zeros_like(db_ref)

        @pl.when(h == 0)
        def _():
            dgq_ref[...] = jnp.zeros_like(dgq_ref)
            dgk_ref[...] = jnp.zeros_like(dgk_ref)

        def block(bi, c):
            wsl, cls = _na_windows(bi, rb, rows)
            qsl = pl.ds(pl.multiple_of(bi * (rb * GRID_W), rb * GRID_W), rb * GRID_W)
            qs = qn[qsl, :].reshape(rb, GRID_W, d)
            dos = dob[qsl, :].reshape(rb, GRID_W, d)
            kw = jnp.stack([kn[w, :] for w in wsl])
            vw = jnp.stack([vb[w, :] for w in wsl])
            s = _bdot(qs, kw, 2, 2) + jnp.stack([b_ref[0, c_] for c_ in cls])
            p = jnp.exp(s - jnp.max(s, axis=-1, keepdims=True))
            p = p / jnp.sum(p, axis=-1, keepdims=True)
            dp = _bdot(dos, vw, 2, 2)
            ds = p * (dp - jnp.sum(dp * p, axis=-1, keepdims=True))
            dsb = ds.astype(BF16)
            dqn[qsl, :] = _bdot(dsb, kw, 2, 1).reshape(rb * GRID_W, d)
            dvw = _bdot(p.astype(BF16), dos, 1, 1)
            dkw = _bdot(dsb, qs, 1, 1)
            for i in range(rb):
                db_ref[0, cls[i]] += ds[i]
                dv_ref[0, wsl[i], :] += dvw[i]
                dkn[wsl[i], :] += dkw[i]
            return c

        lax.fori_loop(0, rows // rb, block, 0)
        dq, dgq = _rms_bwd(dqn[...] * scale, qv, gq_ref[...], rq)
        dk, dgk = _rms_bwd(dkn[...], kv, gk_ref[...], rk)
        dq_ref[0] = dq
        dk_ref[0] = dk
        dgq_ref[...] += dgq
        dgk_ref[...] += dgk

    hm = pl.BlockSpec((1, T, d), lambda h: (h, 0, 0))
    gs = pl.BlockSpec((1, d), lambda h: (0, 0))
    bs = pl.BlockSpec((1, NA_WIN_R, GRID_W, win), lambda h: (h, 0, 0, 0))
    return _call(
        body, name=name, grid=(H,),
        in_specs=[hm, hm, hm, hm, pl.BlockSpec((1, NA_WIN_R, GRID_W, win), lambda h: (layer * H + h, 0, 0, 0)), gs, gs],
        out_specs=[hm, hm, hm, bs, gs, gs],
        out_shape=[SDS((H, T, d), F32)] * 3 + [SDS((H, NA_WIN_R, GRID_W, win), F32), SDS((1, d), F32), SDS((1, d), F32)],
        scratch_shapes=[pltpu.VMEM((T, d), BF16)] * 4 + [pltpu.VMEM((T, d), F32)] * 2,
        args=(q, k, v, do, bias, gq, gk), exch=exch)


def _gla_gate_fwd(z, col, upf, upb, bf, bb, name):
    T = z.shape[0]
    W = upf.shape[1]
    tm = _tile(T, 1024)

    def body(z_ref, upf_ref, upb_ref, bf_ref, bb_ref, g_ref):
        seg = z_ref[...].astype(BF16)
        for rev, (up_ref, b_ref) in enumerate(((upf_ref, bf_ref), (upb_ref, bb_ref))):
            pre = _dot(seg, up_ref[...]) + b_ref[...]
            g = (jnp.minimum(pre, 0.0) - jnp.log(1.0 + jnp.exp(-jnp.abs(pre)))) * (1.0 / GLA_TAU)
            for h in range(GLA_HEADS):
                g_ref[rev, h] = g[:, h * GLA_DK:(h + 1) * GLA_DK]

    full = lambda shape: pl.BlockSpec(shape, lambda i: (0, 0))
    return pl.pallas_call(
        body, name=name, grid=(T // tm,),
        in_specs=[pl.BlockSpec((tm, LANE), lambda i: (i, col)), full((LANE, W)), full((LANE, W)), full((1, W)), full((1, W))],
        out_specs=pl.BlockSpec((2, GLA_HEADS, tm, GLA_DK), lambda i: (0, 0, i, 0)),
        out_shape=SDS((2, GLA_HEADS, T, GLA_DK), F32),
        compiler_params=_params(1),
    )(z, upf, upb, bf, bb)


def _gla_gate_bwd(z, col, upf, upb, bf, bb, dg, name):
    T = z.shape[0]
    W = upf.shape[1]
    tm = _tile(T, 1024)

    def body(z_ref, upf_ref, upb_ref, bf_ref, bb_ref, dg_ref, dseg_ref, dupf_ref, dupb_ref, dbf_ref, dbb_ref):
        @pl.when(pl.program_id(0) == 0)
        def _():
            for r in (dupf_ref, dupb_ref, dbf_ref, dbb_ref):
                r[...] = jnp.zeros_like(r)

        seg = z_ref[...].astype(BF16)
        dseg = jnp.zeros(dseg_ref.shape, F32)
        for rev, (up_ref, b_ref, dup_ref, db_ref) in enumerate(((upf_ref, bf_ref, dupf_ref, dbf_ref),
                                                               (upb_ref, bb_ref, dupb_ref, dbb_ref))):
            pre = _dot(seg, up_ref[...]) + b_ref[...]
            dgv = jnp.concatenate([dg_ref[rev, h] for h in range(GLA_HEADS)], axis=1)
            dpre = dgv * (1.0 / GLA_TAU) * (1.0 - _sigmoid(pre))
            dpb = dpre.astype(BF16)
            dseg = dseg + _dot(dpb, up_ref[...], NT)
            dup_ref[...] += _dot(seg, dpb, TN)
            db_ref[...] += jnp.sum(dpre, axis=0, keepdims=True)
        dseg_ref[...] = dseg

    full = lambda shape: pl.BlockSpec(shape, lambda i: (0, 0))
    return pl.pallas_call(
        body, name=name, grid=(T // tm,),
        in_specs=[pl.BlockSpec((tm, LANE), lambda i: (i, col)), full((LANE, W)), full((LANE, W)), full((1, W)), full((1, W)),
                  pl.BlockSpec((2, GLA_HEADS, tm, GLA_DK), lambda i: (0, 0, i, 0))],
        out_specs=[pl.BlockSpec((tm, LANE), lambda i: (i, 0)), full((LANE, W)), full((LANE, W)), full((1, W)), full((1, W))],
        out_shape=[SDS((T, LANE), F32), SDS((LANE, W), F32), SDS((LANE, W), F32), SDS((1, W), F32), SDS((1, W), F32)],
        compiler_params=_params(1),
    )(z, upf, upb, bf, bb, dg)


def _bdot(a, b, ca, cb, precision=None):
    return lax.dot_general(a, b, (((ca,), (cb,)), ((0,), (0,))), preferred_element_type=F32, precision=precision)


def _gla_chunk_terms(q_ref, k_ref, v_ref, g_ref, rev, tok, n, C):
    dk, dv = q_ref.shape[-1], v_ref.shape[-1]
    q = q_ref[0, tok, :].reshape(n, C, dk) * (dk ** -0.5)
    k = k_ref[0, tok, :].reshape(n, C, dk)
    v = v_ref[0, tok, :].reshape(n, C, dv)
    g = g_ref[rev, 0, tok, :].reshape(n, C, dk)
    ii = lax.broadcasted_iota(jnp.int32, (C, C), 0)
    jj = lax.broadcasted_iota(jnp.int32, (C, C), 1)
    tri = jnp.broadcast_to(((ii <= jj) if rev else (ii >= jj)).astype(F32), (n, C, C))
    amask = ((jj > ii) if rev else (ii >= jj))[None]
    b = _bdot(tri, g, 2, 1, HI)
    bl = jnp.sum(g, axis=1, keepdims=True)
    eb = jnp.exp(b)
    enb = jnp.exp(-b)
    eend = jnp.exp(bl - b)
    return q, k, v, tri, amask, bl, eb, enb, eend


def _gla_segments(T, C):
    n = T // C
    ns = _tile(n, 16, 1)

    def tok(s):
        return pl.ds(pl.multiple_of(s * (ns * C), ns * C), ns * C)

    def chunks(s):
        return pl.ds(pl.multiple_of(s * ns, ns), ns)

    return n, ns, n // ns, tok, chunks


def _gla_fwd(q, k, v, g, name):
    H, T, dk = q.shape
    dv = v.shape[-1]
    C = GLA_CHUNK
    n, ns, nseg, seg_tok, seg_chunks = _gla_segments(T, C)

    def body(q_ref, k_ref, v_ref, g_ref, o_ref, upd_scr, dec_scr, st_scr):
        for rev in (0, 1):
            def intra(s, c):
                tok, ch = seg_tok(s), seg_chunks(s)
                q, k, v, tri, amask, bl, eb, enb, eend = _gla_chunk_terms(q_ref, k_ref, v_ref, g_ref, rev, tok, ns, C)
                vb = v.astype(BF16)
                a = jnp.where(amask, _bdot((q * eb).astype(BF16), (k * enb).astype(BF16), 2, 2), 0.0).astype(BF16)
                o = _bdot(a, vb, 2, 1).reshape(ns * C, dv)
                if rev:
                    o_ref[0, tok, :] += o
                else:
                    o_ref[0, tok, :] = o
                upd_scr[ch] = _bdot(vb, (k * eend).astype(BF16), 1, 1)
                dec_scr[ch] = jnp.exp(bl)
                return c

            lax.fori_loop(0, nseg, intra, 0)

            def step(t, st):
                i = n - 1 - t if rev else t
                st_scr[i] = st
                return st * dec_scr[i] + upd_scr[i]

            lax.fori_loop(0, n, step, jnp.zeros((dv, dk), F32))

            def inter(s, c):
                tok, ch = seg_tok(s), seg_chunks(s)
                q, k, v, tri, amask, bl, eb, enb, eend = _gla_chunk_terms(q_ref, k_ref, v_ref, g_ref, rev, tok, ns, C)
                o_ref[0, tok, :] += _bdot((q * eb).astype(BF16), st_scr[ch].astype(BF16), 2, 2).reshape(ns * C, dv)
                return c

            lax.fori_loop(0, nseg, inter, 0)

    sk = pl.BlockSpec((1, T, dk), lambda h: (h, 0, 0))
    sv = pl.BlockSpec((1, T, dv), lambda h: (h, 0, 0))
    sg = pl.BlockSpec((2, 1, T, dk), lambda h: (0, h, 0, 0))
    return pl.pallas_call(
        body, name=name, grid=(H,), in_specs=[sk, sk, sv, sg], out_specs=sv,
        out_shape=SDS((H, T, dv), F32),
        scratch_shapes=[pltpu.VMEM((n, dv, dk), F32), pltpu.VMEM((n, 1, dk), F32), pltpu.VMEM((n, dv, dk), F32)],
        compiler_params=_params(1),
    )(q, k, v, g)


def _gla_bwd(q, k, v, g, do, name):
    H, T, dk = q.shape
    dv = v.shape[-1]
    C = GLA_CHUNK
    n, ns, nseg, seg_tok, seg_chunks = _gla_segments(T, C)

    def one_scan(rev, q_ref, k_ref, v_ref, g_ref, do_ref, dq_ref, dk_ref, dv_ref, dg_ref, upd_scr, dec_scr, st_scr, gn_scr,
                 rn_scr):
        def first(s, c):
            tok, ch = seg_tok(s), seg_chunks(s)
            q, k, v, tri, amask, bl, eb, enb, eend = _gla_chunk_terms(q_ref, k_ref, v_ref, g_ref, rev, tok, ns, C)
            dob = do_ref[0, tok, :].reshape(ns, C, dv).astype(BF16)
            upd_scr[ch] = _bdot(v.astype(BF16), (k * eend).astype(BF16), 1, 1)
            dec_scr[ch] = jnp.exp(bl)
            gn_scr[ch] = _bdot(dob, (q * eb).astype(BF16), 1, 1)
            return c

        lax.fori_loop(0, nseg, first, 0)

        def fstep(t, st):
            i = n - 1 - t if rev else t
            st_scr[i] = st
            return st * dec_scr[i] + upd_scr[i]

        lax.fori_loop(0, n, fstep, jnp.zeros((dv, dk), F32))

        def bstep(t, r):
            i = t if rev else n - 1 - t
            rn_scr[i] = r
            return gn_scr[i] + r * dec_scr[i]

        lax.fori_loop(0, n, bstep, jnp.zeros((dv, dk), F32))

        def second(s, c):
            tok, ch = seg_tok(s), seg_chunks(s)
            q, k, v, tri, amask, bl, eb, enb, eend = _gla_chunk_terms(q_ref, k_ref, v_ref, g_ref, rev, tok, ns, C)
            qe, ke, kend = q * eb, k * enb, k * eend
            qeb, keb, kendb, vb = qe.astype(BF16), ke.astype(BF16), kend.astype(BF16), v.astype(BF16)
            dob = do_ref[0, tok, :].reshape(ns, C, dv).astype(BF16)
            a = jnp.where(amask, _bdot(qeb, keb, 2, 2), 0.0).astype(BF16)
            st = st_scr[ch]
            rn = rn_scr[ch]
            rnb = rn.astype(BF16)
            da = jnp.where(amask, _bdot(dob, vb, 2, 2), 0.0).astype(BF16)
            dvv = _bdot(a, dob, 1, 1) + _bdot(kendb, rnb, 2, 2)
            dqe = _bdot(da, keb, 2, 1) + _bdot(dob, st.astype(BF16), 2, 1)
            dke = _bdot(da, qeb, 1, 1)
            dkend = _bdot(vb, rnb, 2, 1)
            ddec = jnp.sum(rn * st, axis=1, keepdims=True)
            dbl = ddec * jnp.exp(bl) + jnp.sum(dkend * kend, axis=1, keepdims=True)
            db = dqe * qe - dke * ke - dkend * kend
            dg = _bdot(tri, db, 1, 1, HI) + dbl
            dqv = (dqe * eb * (dk ** -0.5)).reshape(ns * C, dk)
            dkv = (dke * enb + dkend * eend).reshape(ns * C, dk)
            if rev:
                dq_ref[0, tok, :] += dqv
                dk_ref[0, tok, :] += dkv
                dv_ref[0, tok, :] += dvv.reshape(ns * C, dv)
            else:
                dq_ref[0, tok, :] = dqv
                dk_ref[0, tok, :] = dkv
                dv_ref[0, tok, :] = dvv.reshape(ns * C, dv)
            dg_ref[rev, 0, tok, :] = dg.reshape(ns * C, dk)
            return c

        lax.fori_loop(0, nseg, second, 0)

    def body(*refs):
        one_scan(0, *refs)
        one_scan(1, *refs)

    sk = pl.BlockSpec((1, T, dk), lambda h: (h, 0, 0), pipeline_mode=pl.Buffered(1))
    sv = pl.BlockSpec((1, T, dv), lambda h: (h, 0, 0), pipeline_mode=pl.Buffered(1))
    sg = pl.BlockSpec((2, 1, T, dk), lambda h: (0, h, 0, 0), pipeline_mode=pl.Buffered(1))
    st_shape = pltpu.VMEM((n, dv, dk), F32)
    return pl.pallas_call(
        body, name=name, grid=(H,), in_specs=[sk, sk, sv, sg, sv], out_specs=[sk, sk, sv, sg],
        out_shape=[SDS((H, T, dk), F32), SDS((H, T, dk), F32), SDS((H, T, dv), F32), SDS((2, H, T, dk), F32)],
        scratch_shapes=[st_shape, pltpu.VMEM((n, 1, dk), F32), st_shape, st_shape, st_shape],
        compiler_params=_params(1),
    )(q, k, v, g, do)


def _rope_tables(T):
    half = MLA_ROPE // 2
    inv = ROPE_THETA ** (-jnp.arange(half, dtype=F32) / half)
    ang = jnp.arange(T, dtype=F32)[:, None] * inv[None, :]
    cos, sin = jnp.cos(ang), jnp.sin(ang)
    ctab = jnp.concatenate([jnp.ones((T, MLA_NOPE), F32), cos, cos], axis=1)
    stab = jnp.concatenate([jnp.zeros((T, MLA_NOPE), F32), -sin, sin], axis=1)
    return ctab, stab


def _swap_rope_halves(x):
    half = MLA_ROPE // 2
    return jnp.concatenate([x[:, :MLA_NOPE], x[:, MLA_NOPE + half:], x[:, MLA_NOPE:MLA_NOPE + half]], axis=1)


def _norm_rope(x, g, c, s):
    xn = x * _rstd(x) * g
    return xn * c + _swap_rope_halves(xn) * s


def _norm_rope_bwd(dy, x, g, c, s):
    return _rms_bwd(dy * c + _swap_rope_halves(dy * s), x, g, _rstd(x))


def _mla_qprep_fwd(q_raw, g, ctab, stab, name):
    T, d = q_raw.shape[0], MLA_QK
    H = q_raw.shape[1] // d
    tm = _tile(T, 1024)

    def body(x_ref, g_ref, c_ref, s_ref, o_ref):
        for h in range(H):
            o_ref[h] = _norm_rope(x_ref[:, h * d:(h + 1) * d], g_ref[...], c_ref[...], s_ref[...]).astype(BF16)

    tab = pl.BlockSpec((tm, d), lambda i: (i, 0))
    return pl.pallas_call(
        body, name=name, grid=(T // tm,),
        in_specs=[pl.BlockSpec((tm, H * d), lambda i: (i, 0)), pl.BlockSpec((1, d), lambda i: (0, 0)), tab, tab],
        out_specs=pl.BlockSpec((H, tm, d), lambda i: (0, i, 0)), out_shape=SDS((H, T, d), BF16),
        compiler_params=_params(1),
    )(q_raw, g, ctab, stab)


def _mla_qprep_bwd(q_raw, g, ctab, stab, dy, name):
    T, d = q_raw.shape[0], MLA_QK
    H = q_raw.shape[1] // d
    tm = _tile(T, 1024)

    def body(x_ref, g_ref, c_ref, s_ref, dy_ref, dx_ref, dg_ref):
        @pl.when(pl.program_id(0) == 0)
        def _():
            dg_ref[...] = jnp.zeros_like(dg_ref)

        dg = jnp.zeros((1, d), F32)
        for h in range(H):
            dx, dgh = _norm_rope_bwd(dy_ref[h], x_ref[:, h * d:(h + 1) * d], g_ref[...], c_ref[...], s_ref[...])
            dx_ref[:, h * d:(h + 1) * d] = dx.astype(BF16)
            dg = dg + dgh
        dg_ref[...] += dg

    tab = pl.BlockSpec((tm, d), lambda i: (i, 0))
    tok = pl.BlockSpec((tm, H * d), lambda i: (i, 0))
    gs = pl.BlockSpec((1, d), lambda i: (0, 0))
    return pl.pallas_call(
        body, name=name, grid=(T // tm,),
        in_specs=[tok, gs, tab, tab, pl.BlockSpec((H, tm, d), lambda i: (0, i, 0))],
        out_specs=[tok, gs], out_shape=[SDS((T, H * d), BF16), SDS((1, d), F32)],
        compiler_params=_params(1),
    )(q_raw, g, ctab, stab, dy)


def _mla_kprep_fwd(kv_raw, z, kr_col, g, ctab, stab, name):
    T, d, hw = kv_raw.shape[0], MLA_QK, MLA_NOPE + MLA_V
    H = kv_raw.shape[1] // hw
    tm = _tile(T, 1024)

    def body(kv_ref, kr_ref, g_ref, c_ref, s_ref, k_ref, v_ref):
        kr = kr_ref[:, :MLA_ROPE]
        for h in range(H):
            x = jnp.concatenate([kv_ref[:, h * hw:h * hw + MLA_NOPE], kr], axis=1)
            k_ref[h] = _norm_rope(x, g_ref[...], c_ref[...], s_ref[...]).astype(BF16)
            v_ref[h] = kv_ref[:, h * hw + MLA_NOPE:(h + 1) * hw].astype(BF16)

    tab = pl.BlockSpec((tm, d), lambda i: (i, 0))
    return pl.pallas_call(
        body, name=name, grid=(T // tm,),
        in_specs=[pl.BlockSpec((tm, H * hw), lambda i: (i, 0)), pl.BlockSpec((tm, LANE), lambda i: (i, kr_col)),
                  pl.BlockSpec((1, d), lambda i: (0, 0)), tab, tab],
        out_specs=[pl.BlockSpec((H, tm, d), lambda i: (0, i, 0)), pl.BlockSpec((H, tm, MLA_V), lambda i: (0, i, 0))],
        out_shape=[SDS((H, T, d), BF16), SDS((H, T, MLA_V), BF16)],
        compiler_params=_params(1),
    )(kv_raw, z, g, ctab, stab)


def _mla_kprep_bwd(kv_raw, z, kr_col, g, ctab, stab, dk, dv, name):
    T, d, hw = kv_raw.shape[0], MLA_QK, MLA_NOPE + MLA_V
    H = kv_raw.shape[1] // hw
    tm = _tile(T, 1024)

    def body(kv_ref, kr_ref, g_ref, c_ref, s_ref, dk_ref, dv_ref, dkv_ref, dkr_ref, dg_ref):
        @pl.when(pl.program_id(0) == 0)
        def _():
            dg_ref[...] = jnp.zeros_like(dg_ref)

        kr = kr_ref[:, :MLA_ROPE]
        dg = jnp.zeros((1, d), F32)
        dkr = jnp.zeros((tm, MLA_ROPE), F32)
        for h in range(H):
            x = jnp.concatenate([kv_ref[:, h * hw:h * hw + MLA_NOPE], kr], axis=1)
            dx, dgh = _norm_rope_bwd(dk_ref[h], x, g_ref[...], c_ref[...], s_ref[...])
            dkv_ref[:, h * hw:h * hw + MLA_NOPE] = dx[:, :MLA_NOPE].astype(BF16)
            dkv_ref[:, h * hw + MLA_NOPE:(h + 1) * hw] = dv_ref[h].astype(BF16)
            dkr = dkr + dx[:, MLA_NOPE:]
            dg = dg + dgh
        dkr_ref[...] = jnp.concatenate([dkr, jnp.zeros((tm, LANE - MLA_ROPE), F32)], axis=1)
        dg_ref[...] += dg

    tab = pl.BlockSpec((tm, d), lambda i: (i, 0))
    tok = pl.BlockSpec((tm, H * hw), lambda i: (i, 0))
    gs = pl.BlockSpec((1, d), lambda i: (0, 0))
    return pl.pallas_call(
        body, name=name, grid=(T // tm,),
        in_specs=[tok, pl.BlockSpec((tm, LANE), lambda i: (i, kr_col)), gs, tab, tab,
                  pl.BlockSpec((H, tm, d), lambda i: (0, i, 0)), pl.BlockSpec((H, tm, MLA_V), lambda i: (0, i, 0))],
        out_specs=[tok, pl.BlockSpec((tm, LANE), lambda i: (i, 0)), gs],
        out_shape=[SDS((T, H * hw), BF16), SDS((T, LANE), F32), SDS((1, d), F32)],
        compiler_params=_params(1),
    )(kv_raw, z, g, ctab, stab, dk, dv)


def _mla_attn_fwd(q, k, v, name, exch=None):
    H, T, d = q.shape
    dv = v.shape[-1]
    tq = _tile(T, 256)
    scale = d ** -0.5

    def body(q_ref, k_ref, v_ref, o_ref, lse_ref):
        s = _dot(q_ref[0], k_ref[0], NT) * scale
        m = jnp.max(s, axis=-1, keepdims=True)
        p = jnp.exp(s - m)
        l = jnp.sum(p, axis=-1, keepdims=True)
        o_ref[0] = _dot((p / l).astype(BF16), v_ref[0])
        lse_ref[0] = _col_to_row(m + jnp.log(l))

    return _call(
        body, name=name, grid=(H, T // tq),
        in_specs=[pl.BlockSpec((1, tq, d), lambda h, i: (h, i, 0)), pl.BlockSpec((1, T, d), lambda h, i: (h, 0, 0)),
                  pl.BlockSpec((1, T, dv), lambda h, i: (h, 0, 0))],
        out_specs=[pl.BlockSpec((1, tq, dv), lambda h, i: (h, i, 0)), pl.BlockSpec((1, 1, tq), lambda h, i: (h, 0, i))],
        out_shape=[SDS((H, T, dv), F32), SDS((H, 1, T), F32)],
        args=(q, k, v), exch=exch)


def _col_to_row(col):
    m = col.shape[0]
    eye = lax.broadcasted_iota(jnp.int32, (m, m), 0) == lax.broadcasted_iota(jnp.int32, (m, m), 1)
    return jnp.sum(jnp.where(eye, col, 0.0), axis=0, keepdims=True)


def _mla_attn_bwd(q, k, v, o, lse, do, name, exch=None):
    H, T, d = q.shape
    dv = v.shape[-1]
    tq = _tile(T, 256)
    scale = d ** -0.5

    def body(q_ref, k_ref, v_ref, o_ref, lse_ref, do_ref, dq_ref, dk_ref, dv_ref):
        @pl.when(pl.program_id(1) == 0)
        def _():
            dk_ref[...] = jnp.zeros_like(dk_ref)
            dv_ref[...] = jnp.zeros_like(dv_ref)

        qv, kv = q_ref[0], k_ref[0]
        dov = do_ref[0]
        dob = dov.astype(BF16)
        delta_row = _col_to_row(jnp.sum(dov * o_ref[0], axis=-1, keepdims=True))
        pt = jnp.exp(_dot(kv, qv, NT) * scale - lse_ref[0])
        dst = (pt * (_dot(v_ref[0], dob, NT) - delta_row) * scale).astype(BF16)
        dv_ref[0] += _dot(pt.astype(BF16), dob)
        dk_ref[0] += _dot(dst, qv)
        dq_ref[0] = _dot(dst, kv, TN)

    qb = pl.BlockSpec((1, tq, d), lambda h, i: (h, i, 0))
    kb = pl.BlockSpec((1, T, d), lambda h, i: (h, 0, 0))
    vb = pl.BlockSpec((1, T, dv), lambda h, i: (h, 0, 0))
    ob = pl.BlockSpec((1, tq, dv), lambda h, i: (h, i, 0))
    return _call(
        body, name=name, grid=(H, T // tq),
        in_specs=[qb, kb, vb, ob, pl.BlockSpec((1, 1, tq), lambda h, i: (h, 0, i)), ob],
        out_specs=[qb, kb, vb],
        out_shape=[SDS((H, T, d), F32), SDS((H, T, d), F32), SDS((H, T, dv), F32)],
        args=(q, k, v, o, lse, do), exch=exch)


def _silu_parts(x):
    s = _sigmoid(x)
    return x * s, s * (1.0 + x * (1.0 - s))


def _mix_out_fwd(x1, y_na, o_gla, y_mla, z, gr_col, gnorm, wbr, w_out, name):
    T, D = x1.shape
    W = y_na.shape[0] * y_na.shape[2]
    tm = _tile(T, 512)

    def body(x_ref, na_ref, gla_ref, mla_ref, gt_ref, gr_ref, gn_ref, w0, w1, w2, wo_ref, o_ref):
        gn = gn_ref[...]
        nrm = jnp.concatenate([gla_ref[h] * _rstd(gla_ref[h]) * gn for h in range(GLA_HEADS)], axis=1)
        y_gla = nrm * _silu_parts(gr_ref[...])[0]
        y_na = jnp.concatenate([na_ref[h] for h in range(NA_HEADS)], axis=1)
        y_mla = jnp.concatenate([mla_ref[h] for h in range(MLA_HEADS)], axis=1)
        mixed = jnp.zeros((tm, D), F32)
        for i, (y, w_ref) in enumerate(((y_na, w0), (y_gla, w1), (y_mla, w2))):
            mixed = mixed + _sigmoid(gt_ref[:, i * D:(i + 1) * D]) * _dot(y.astype(BF16), w_ref[...])
        o_ref[...] = x_ref[...] + _dot(mixed.astype(BF16), wo_ref[...])

    tok = lambda w: pl.BlockSpec((tm, w), lambda i: (i, 0))
    hm = lambda a: pl.BlockSpec((a.shape[0], tm, a.shape[2]), lambda i: (0, i, 0))
    full = lambda shape: pl.BlockSpec(shape, lambda i: (0, 0))
    return pl.pallas_call(
        body, name=name, grid=(T // tm,),
        in_specs=[tok(D), hm(y_na), hm(o_gla), hm(y_mla), tok(3 * D), pl.BlockSpec((tm, W), lambda i: (i, gr_col)),
                  full((1, GLA_DV)), full((W, D)), full((W, D)), full((W, D)), full((D, D))],
        out_specs=tok(D), out_shape=SDS((T, D), F32),
        compiler_params=_params(1),
    )(x1, y_na, o_gla, y_mla, z, z, gnorm, wbr[0], wbr[1], wbr[2], w_out)


def _mix_out_bwd(dx2, y_na, o_gla, y_mla, z, gr_col, gnorm, wbr, w_out, name):
    T, D = dx2.shape
    W = y_na.shape[0] * y_na.shape[2]
    tm = _tile(T, 256)

    def body(dx_ref, na_ref, gla_ref, mla_ref, gt_ref, gr_ref, gn_ref, w0, w1, w2, wo_ref,
             dna_ref, dgla_ref, dmla_ref, dgr_ref, dgt_ref, dw0, dw1, dw2, dwo_ref, dgn_ref):
        @pl.when(pl.program_id(0) == 0)
        def _():
            for r in (dw0, dw1, dw2, dwo_ref, dgn_ref):
                r[...] = jnp.zeros_like(r)

        gn = gn_ref[...]
        grv = gr_ref[...]
        sil, dsil = _silu_parts(grv)
        rs = [_rstd(gla_ref[h]) for h in range(GLA_HEADS)]
        nrm = jnp.concatenate([gla_ref[h] * rs[h] * gn for h in range(GLA_HEADS)], axis=1)
        ys = (jnp.concatenate([na_ref[h] for h in range(NA_HEADS)], axis=1).astype(BF16), (nrm * sil).astype(BF16),
              jnp.concatenate([mla_ref[h] for h in range(MLA_HEADS)], axis=1).astype(BF16))
        dxb = dx_ref[...].astype(BF16)
        dmixed = _dot(dxb, wo_ref[...], NT)
        mixed = jnp.zeros((tm, D), F32)
        dys = []
        for i, (y, w_ref, dw_ref) in enumerate(zip(ys, (w0, w1, w2), (dw0, dw1, dw2))):
            gt = _sigmoid(gt_ref[:, i * D:(i + 1) * D])
            p = _dot(y, w_ref[...])
            mixed = mixed + gt * p
            dp = (dmixed * gt).astype(BF16)
            dgt_ref[:, i * D:(i + 1) * D] = (dmixed * p * gt * (1.0 - gt)).astype(BF16)
            dys.append(_dot(dp, w_ref[...], NT))
            dw_ref[...] += _dot(y, dp, TN)
        dwo_ref[...] += _dot(mixed.astype(BF16), dxb, TN)
        for h in range(NA_HEADS):
            dna_ref[h] = dys[0][:, h * NA_HD:(h + 1) * NA_HD]
        for h in range(MLA_HEADS):
            dmla_ref[h] = dys[2][:, h * MLA_V:(h + 1) * MLA_V]
        dgr_ref[...] = dys[1] * nrm * dsil
        dn = dys[1] * sil
        dgn = jnp.zeros((1, GLA_DV), F32)
        for h in range(GLA_HEADS):
            dxh, dgh = _rms_bwd(dn[:, h * GLA_DV:(h + 1) * GLA_DV], gla_ref[h], gn, rs[h])
            dgla_ref[h] = dxh
            dgn = dgn + dgh
        dgn_ref[...] += dgn

    tok = lambda w: pl.BlockSpec((tm, w), lambda i: (i, 0))
    hm = lambda a: pl.BlockSpec((a.shape[0], tm, a.shape[2]), lambda i: (0, i, 0))
    full = lambda shape: pl.BlockSpec(shape, lambda i: (0, 0))
    return pl.pallas_call(
        body, name=name, grid=(T // tm,),
        in_specs=[tok(D), hm(y_na), hm(o_gla), hm(y_mla), tok(3 * D), pl.BlockSpec((tm, W), lambda i: (i, gr_col)),
                  full((1, GLA_DV)), full((W, D)), full((W, D)), full((W, D)), full((D, D))],
        out_specs=[hm(y_na), hm(o_gla), hm(y_mla), tok(W), tok(3 * D), full((W, D)), full((W, D)), full((W, D)),
                   full((D, D)), full((1, GLA_DV))],
        out_shape=[SDS(y_na.shape, F32), SDS(o_gla.shape, F32), SDS(y_mla.shape, F32), SDS((T, W), F32),
                   SDS((T, 3 * D), BF16)] + [SDS((W, D), F32)] * 3 + [SDS((D, D), F32), SDS((1, GLA_DV), F32)],
        compiler_params=_params(1),
    )(dx2, y_na, o_gla, y_mla, z, z, gnorm, wbr[0], wbr[1], wbr[2], w_out)


def _split_z(z, lay, name):
    T = z.shape[0]
    tm = _tile(T, 512)
    na0, gv0, gq0 = lay['na_q'][0], lay['gv'][0], lay['gq'][0]
    assert na0 % (3 * NA_W) == 0 and gv0 % GLA_V_W == 0 and gq0 % (2 * GLA_QK_W) == 0
    assert lay['na_k'][0] == na0 + NA_W and lay['na_v'][0] == na0 + 2 * NA_W and lay['gk'][0] == gq0 + GLA_QK_W

    def body(na_ref, gv_ref, gqk_ref, q_ref, k_ref, v_ref, gq_ref, gk_ref, gvo_ref):
        for j, o_ref in enumerate((q_ref, k_ref, v_ref)):
            for h in range(NA_HEADS):
                o_ref[h] = na_ref[:, j * NA_W + h * NA_HD:j * NA_W + (h + 1) * NA_HD]
        for j, o_ref in enumerate((gq_ref, gk_ref)):
            for h in range(GLA_HEADS):
                o_ref[h] = gqk_ref[:, j * GLA_QK_W + h * GLA_DK:j * GLA_QK_W + (h + 1) * GLA_DK]
        for h in range(GLA_HEADS):
            gvo_ref[h] = gv_ref[:, h * GLA_DV:(h + 1) * GLA_DV]

    hm = lambda H, d: pl.BlockSpec((H, tm, d), lambda i: (0, i, 0))
    return pl.pallas_call(
        body, name=name, grid=(T // tm,),
        in_specs=[pl.BlockSpec((tm, 3 * NA_W), lambda i: (i, na0 // (3 * NA_W))),
                  pl.BlockSpec((tm, GLA_V_W), lambda i: (i, gv0 // GLA_V_W)),
                  pl.BlockSpec((tm, 2 * GLA_QK_W), lambda i: (i, gq0 // (2 * GLA_QK_W)))],
        out_specs=[hm(NA_HEADS, NA_HD)] * 3 + [hm(GLA_HEADS, GLA_DK)] * 2 + [hm(GLA_HEADS, GLA_DV)],
        out_shape=[SDS((NA_HEADS, T, NA_HD), F32)] * 3 + [SDS((GLA_HEADS, T, GLA_DK), F32)] * 2
        + [SDS((GLA_HEADS, T, GLA_DV), F32)],
        compiler_params=_params(1),
    )(z, z, z)


def _assemble_dz(dgates, na_dqkv, gla_dqk, gla_dv, dgr, dcq, dckv, dglow, dkr, lay, zp, name):
    T = dgr.shape[0]
    tm = _tile(T, 256)

    def body(dgt_ref, nq_ref, nk_ref, nv_ref, gq_ref, gk_ref, gv_ref, dgr_ref, dcq_ref, dckv_ref, dglow_ref, dkr_ref, dz_ref):
        def put(seg, off, val):
            a = lay[seg][0] + off
            dz_ref[:, a:a + val.shape[1]] = val.astype(BF16)

        put('gates', 0, dgt_ref[...])
        for seg, ref in (('na_q', nq_ref), ('na_k', nk_ref), ('na_v', nv_ref)):
            for h in range(NA_HEADS):
                put(seg, h * NA_HD, ref[h])
        for seg, ref in (('gq', gq_ref), ('gk', gk_ref)):
            for h in range(GLA_HEADS):
                put(seg, h * GLA_DK, ref[h])
        for h in range(GLA_HEADS):
            put('gv', h * GLA_DV, gv_ref[h])
        put('gr', 0, dgr_ref[...])
        put('c_q', 0, dcq_ref[...])
        put('c_kv', 0, dckv_ref[...])
        put('glow', 0, dglow_ref[...])
        put('k_rope', 0, dkr_ref[...])

    tok = lambda a: pl.BlockSpec((tm, a.shape[1]), lambda i: (i, 0))
    hm = lambda a: pl.BlockSpec((a.shape[0], tm, a.shape[2]), lambda i: (0, i, 0))
    args = (dgates, *na_dqkv, *gla_dqk, gla_dv, dgr, dcq, dckv, dglow, dkr)
    return pl.pallas_call(
        body, name=name, grid=(T // tm,),
        in_specs=[tok(dgates)] + [hm(a) for a in (*na_dqkv, *gla_dqk, gla_dv)] + [tok(dgr), tok(dcq), tok(dckv), tok(dglow),
                                                                                 tok(dkr)],
        out_specs=pl.BlockSpec((tm, zp), lambda i: (i, 0)), out_shape=SDS((T, zp), BF16),
        compiler_params=_params(1),
    )(*args)


def _loss_head(y, target, name):
    T, D = y.shape
    tm = _tile(T, 1024)

    def body(y_ref, t_ref, dy_ref, l_ref):
        @pl.when(pl.program_id(0) == 0)
        def _():
            l_ref[...] = jnp.zeros_like(l_ref)

        e = y_ref[...] - t_ref[...]
        dy_ref[...] = e * (1.0 / D)
        l_ref[...] += 0.5 * jnp.sum(jnp.mean(e * e, axis=-1, keepdims=True), axis=0, keepdims=True)

    tok = pl.BlockSpec((tm, D), lambda i: (i, 0))
    return pl.pallas_call(
        body, name=name, grid=(T // tm,), in_specs=[tok, tok],
        out_specs=[tok, pl.BlockSpec((1, 1), lambda i: (0, 0))],
        out_shape=[SDS((T, D), F32), SDS((1, 1), F32)],
        compiler_params=_params(1),
    )(y, target)


def _exchange_copies(src, dst, send_sems, recv_sems, local_sems, broadcast, with_recvs=True):
    n = len(src)
    x, y, c = lax.axis_index("x"), lax.axis_index("y"), lax.axis_index("c")
    me = 4 * x + 2 * y + c
    local = [pltpu.make_async_copy(src[a] if broadcast else src[a].at[me], dst[a].at[me], local_sems.at[a])
             for a in range(n)]
    sends, recvs = [], []
    for d in range(1, N_DEV):
        px = 1 - x if d & 4 else x
        py = 1 - y if d & 2 else y
        pc = 1 - c if d & 1 else c
        peer = 4 * px + 2 * py + pc
        for a in range(n):
            for out, slot in ((sends, me), (recvs, peer)) if with_recvs else ((sends, me),):
                out.append(pltpu.make_async_remote_copy(
                    src_ref=src[a] if broadcast else src[a].at[peer], dst_ref=dst[a].at[slot],
                    send_sem=send_sems.at[a, d - 1], recv_sem=recv_sems.at[a, d - 1],
                    device_id=(px, py, pc), device_id_type=pl.DeviceIdType.MESH))
    return local, sends, recvs


def _gather_copies(src, dst, send_sems, recv_sems, local_sems, with_local=True):
    x, y, c = lax.axis_index("x"), lax.axis_index("y"), lax.axis_index("c")
    chips = [(1 - x, y), (x, 1 - y), (1 - x, 1 - y)]
    lin = lambda px, py, pc: 4 * px + 2 * py + pc

    def copy(a, k, incoming=False):
        if k == 0:
            to, out_src, out_slot, in_slot = (x, y, 1 - c), src[a], lin(x, y, c), lin(x, y, 1 - c)
        elif k <= 3:
            to, out_src, out_slot, in_slot = (*chips[k - 1], c), src[a], lin(x, y, c), lin(*chips[k - 1], c)
        else:
            slot = lin(*chips[k - 4], c)
            to, out_src, out_slot, in_slot = (x, y, 1 - c), dst[a].at[slot], slot, lin(*chips[k - 4], 1 - c)
        return pltpu.make_async_remote_copy(
            src_ref=out_src, dst_ref=dst[a].at[in_slot if incoming else out_slot], send_sem=send_sems.at[a, k],
            recv_sem=recv_sems.at[a, k], device_id=to, device_id_type=pl.DeviceIdType.MESH)

    local = [pltpu.make_async_copy(src[a], dst[a].at[lin(x, y, c)], local_sems.at[a])
             for a in range(len(src) if with_local else 0)]
    return copy, local


def _exchange_start(src, dst, send_sems, recv_sems, local_sems, broadcast):
    if broadcast:
        copy, local = _gather_copies(src, dst, send_sems, recv_sems, local_sems)
        for cp in local:
            cp.start()
        for k in range(4):
            for a in range(len(src)):
                copy(a, k).start()
        return
    local, sends, _ = _exchange_copies(src, dst, send_sems, recv_sems, local_sems, broadcast, with_recvs=False)
    for cp in local + sends:
        cp.start()


def _exchange_pass_on(src, dst, send_sems, recv_sems, local_sems, broadcast):
    if broadcast:
        copy, _ = _gather_copies(src, dst, send_sems, recv_sems, local_sems, with_local=False)
        for k in range(1, 4):
            for a in range(len(src)):
                copy(a, k, True).wait_recv()
                copy(a, k + 3).start()


def _exchange_wait(src, dst, send_sems, recv_sems, local_sems, broadcast):
    if broadcast:
        copy, local = _gather_copies(src, dst, send_sems, recv_sems, local_sems)
        n = len(src)
        for k in (0, 4, 5, 6):
            for a in range(n):
                copy(a, k, True).wait_recv()
        for k in range(N_DEV - 1):
            for a in range(n):
                copy(a, k).wait_send()
        for cp in local:
            cp.wait()
        return
    local, sends, recvs = _exchange_copies(src, dst, send_sems, recv_sems, local_sems, broadcast)
    for cp in recvs:
        cp.wait_recv()
    for cp in sends:
        cp.wait_send()
    for cp in local:
        cp.wait()


def _exchange_shapes(srcs, broadcast):
    n = len(srcs)
    out_shape = [SDS((N_DEV,) + (tuple(s.shape) if broadcast else tuple(s.shape[1:])), s.dtype) for s in srcs]
    sems = [pltpu.SemaphoreType.DMA((n, N_DEV - 1)), pltpu.SemaphoreType.DMA((n, N_DEV - 1)),
            pltpu.SemaphoreType.DMA((n,))]
    return out_shape, sems


def _exchange(srcs, broadcast, name):
    n = len(srcs)
    out_shape, sems = _exchange_shapes(srcs, broadcast)

    def body(*refs):
        mode = (refs[:n], refs[n:2 * n]) + tuple(refs[2 * n:]) + (broadcast,)
        _exchange_start(*mode)
        _exchange_pass_on(*mode)
        _exchange_wait(*mode)

    hbm = pl.BlockSpec(memory_space=pltpu.HBM)
    return pl.pallas_call(body, name=name, in_specs=[hbm] * n, out_specs=[hbm] * n, out_shape=out_shape,
                          scratch_shapes=sems)(*srcs)


def _call(body, *, name, grid, in_specs, out_specs, out_shape, args, scratch_shapes=(), exch=None):
    params = _params(len(grid))
    if exch is None:
        return pl.pallas_call(body, name=name, grid=grid, in_specs=in_specs, out_specs=out_specs, out_shape=out_shape,
                              scratch_shapes=list(scratch_shapes), compiler_params=params)(*args)
    srcs, broadcast = exch
    n, n_in, n_out, n_scr = len(srcs), len(args), len(out_shape), len(scratch_shapes)
    x_shape, sems = _exchange_shapes(srcs, broadcast)

    def carrier(*refs):
        ins, x_src = refs[:n_in], refs[n_in:n_in + n]
        outs, x_dst = refs[n_in + n:n_in + n + n_out], refs[n_in + n + n_out:n_in + 2 * n + n_out]
        scr = refs[n_in + 2 * n + n_out:n_in + 2 * n + n_out + n_scr]
        mode = (x_src, x_dst) + tuple(refs[n_in + 2 * n + n_out + n_scr:]) + (broadcast,)
        step = 0
        for a, g in enumerate(grid):
            step = step * g + pl.program_id(a)
        steps = int(np.prod(grid))
        pl.when(step == 0)(lambda: _exchange_start(*mode))
        body(*ins, *outs, *scr)
        pl.when(step == (3 * steps) // 4)(lambda: _exchange_pass_on(*mode))
        pl.when(step == steps - 1)(lambda: _exchange_wait(*mode))

    hbm = pl.BlockSpec(memory_space=pltpu.HBM)
    res = pl.pallas_call(carrier, name=name, grid=grid, in_specs=list(in_specs) + [hbm] * n,
                         out_specs=list(out_specs) + [hbm] * n, out_shape=list(out_shape) + x_shape,
                         scratch_shapes=list(scratch_shapes) + sems, compiler_params=params)(*args, *srcs)
    return res[:n_out], res[n_out:]


def _adam_math(w, g, m, v):
    m = ADAM_B1 * m + (1.0 - ADAM_B1) * g
    v = ADAM_B2 * v + (1.0 - ADAM_B2) * (g * g)
    m_hat = m / (1.0 - ADAM_B1 ** ADAM_STEP)
    v_hat = v / (1.0 - ADAM_B2 ** ADAM_STEP)
    delta = -ADAM_LR * (m_hat / (jnp.sqrt(v_hat) + ADAM_EPS) + ADAM_WD * w)
    return delta, m, v


def _adamw_sharded(landed, k, w, m, v, name):
    L, r, c = w.shape
    lanes = -(-c // LANE) * LANE
    rb = _tile(r, max(16, (1 << 22) // (N_DEV * lanes * 2)), 16)

    def body(*refs):
        l_refs = refs[:L]
        w_ref, m_ref, v_ref, g_ref, d_ref, nm_ref, nv_ref = refs[L:]
        for j in range(L):
            @pl.when(pl.program_id(0) == j)
            def _(j=j):
                g = l_refs[j][0, 0].astype(F32)
                for s in range(1, N_DEV):
                    g = g + l_refs[j][s, 0].astype(F32)
                delta, nm, nv = _adam_math(w_ref[0], g, m_ref[0], v_ref[0])
                g_ref[0] = g
                d_ref[0] = delta
                nm_ref[0] = nm
                nv_ref[0] = nv

    blk = pl.BlockSpec((1, rb, c), lambda l, i: (l, i, 0))
    land = [pl.BlockSpec((N_DEV, 1, rb, c), lambda l, i, j=j: (0, k, jnp.where(l == j, i, 0), 0)) for j in range(L)]
    return pl.pallas_call(
        body, name=name, grid=(L, r // rb),
        in_specs=land + [blk, blk, blk],
        out_specs=[blk] * 4, out_shape=[SDS((L, r, c), F32)] * 4,
        compiler_params=_params(2),
    )(*landed, w, m, v)


def _adamw_replicated(landed, w, m, v, name):
    R = w.shape[0]

    def body(l_ref, w_ref, m_ref, v_ref, g_ref, d_ref, nm_ref, nv_ref):
        g = l_ref[0]
        for s in range(1, N_DEV):
            g = g + l_ref[s]
        delta, nm, nv = _adam_math(w_ref[...], g, m_ref[...], v_ref[...])
        g_ref[...] = g
        d_ref[...] = delta
        nm_ref[...] = nm
        nv_ref[...] = nv

    blk = pl.BlockSpec((R, LANE), lambda i: (0, 0))
    return pl.pallas_call(
        body, name=name, grid=(1,),
        in_specs=[pl.BlockSpec((N_DEV, R, LANE), lambda i: (0, 0, 0)), blk, blk, blk],
        out_specs=[blk] * 4, out_shape=[SDS((R, LANE), F32)] * 4,
        compiler_params=_params(1),
    )(landed, w, m, v)


def _pack(parts):
    flat = jnp.concatenate([p.reshape(-1) for p in parts])
    rows = -(-flat.shape[0] // (8 * LANE)) * 8
    return jnp.pad(flat, (0, rows * LANE - flat.shape[0])).reshape(rows, LANE)


def _unpack(packed, shapes):
    flat = packed.reshape(-1)
    out, off = [], 0
    for s in shapes:
        size = int(np.prod(s))
        out.append(flat[off:off + size].reshape(s))
        off += size
    return out


def _layer_fwd(x0, w, l, lay, tabs, carry):
    T, D = x0.shape
    ctab, stab, na_bias = tabs
    col = lambda name: lay[name][0] // lay[name][1]
    sv = {'x0': x0}
    x1, sv['a1'], sv['b1'] = _carried(_ffn_fwd, carry, 'ffn1', x0, w['ffn1_norm'], w['ffn1_w1'], w['ffn1_w3'],
                                      w['ffn1_w2'], f"ffn1_fwd_{l}")
    z, sv['hb'] = _carried(_norm_matmul, carry, 'mix_in', x1, 0, w['mix_norm'], w['w_in_z'], f"mix_in_{l}")
    sv['na_q'], sv['na_k'], sv['na_v'], sv['gq'], sv['gk'], sv['gv'] = _split_z(z, lay, f"split_z_{l}")
    y_na = _carried(_na_fwd, carry, 'na', sv['na_q'], sv['na_k'], sv['na_v'], na_bias, l, w['na_q_norm'], w['na_k_norm'],
                    f"na_fwd_{l}")[0]
    sv['gg'] = _gla_gate_fwd(z, col('glow'), w['up_f'], w['up_b'], w['gla_gf_bias'], w['gla_gb_bias'], f"gla_gate_{l}")
    o_gla = _gla_fwd(sv['gq'], sv['gk'], sv['gv'], sv['gg'], f"gla_fwd_{l}")
    sv['q_raw'], sv['hq'] = _norm_matmul(z, col('c_q'), w['mla_cq_norm'], w['mla_w_uq'], f"mla_uq_{l}")
    sv['kv_raw'], sv['hkv'] = _norm_matmul(z, col('c_kv'), w['mla_ckv_norm'], w['mla_w_ukv'], f"mla_ukv_{l}")
    sv['mq'] = _mla_qprep_fwd(sv['q_raw'], w['mla_q_norm'], ctab, stab, f"mla_qprep_{l}")
    sv['mk'], sv['mv'] = _mla_kprep_fwd(sv['kv_raw'], z, col('k_rope'), w['mla_k_norm'], ctab, stab, f"mla_kprep_{l}")
    y_mla, sv['lse'] = _carried(_mla_attn_fwd, carry, 'mla', sv['mq'], sv['mk'], sv['mv'], f"mla_attn_{l}")
    w_br = (w['w_br_na'], w['w_br_gla'], w['w_br_mla'])
    x2 = _mix_out_fwd(x1, y_na, o_gla, y_mla, z, col('gr'), w['gla_out_norm'], w_br, w['w_out'], f"mix_out_{l}")
    sv.update(x1=x1, z=z, y_na=y_na, o_gla=o_gla, y_mla=y_mla, x2=x2)
    x3, sv['a2'], sv['b2'] = _carried(_ffn_fwd, carry, 'ffn2', x2, w['ffn2_norm'], w['ffn2_w1'], w['ffn2_w3'],
                                      w['ffn2_w2'], f"ffn2_fwd_{l}")
    return x3, sv


def _ffn_grads(grads, x, dy, a, b, w, pre, l, carry):
    dx, da, db, u, hb, dyb, dg = _carried(_ffn_bwd, carry, pre, x, w[pre + '_norm'], dy, a, b, w[pre + '_w1'],
                                          w[pre + '_w3'], w[pre + '_w2'], f"{pre}_bwd_{l}")
    grads[pre + '_norm'] = dg
    grads[pre + '_w1'] = _matmul_tn(hb, da, f"{pre}_dw1_{l}")
    grads[pre + '_w3'] = _carried(_matmul_tn, carry, pre + '_dw3', hb, db, f"{pre}_dw3_{l}")
    grads[pre + '_w2'] = _carried(_matmul_tn, carry, pre + '_dw2', u, dyb, f"{pre}_dw2_{l}")
    return dx


def _layer_bwd(grads, dx3, sv, w, l, lay, tabs, carry):
    T, D = dx3.shape
    ctab, stab, na_bias, onehot = tabs
    col = lambda name: lay[name][0] // lay[name][1]
    z = sv['z']
    dx2 = _ffn_grads(grads, sv['x2'], dx3, sv['a2'], sv['b2'], w, 'ffn2', l, carry)
    w_br = (w['w_br_na'], w['w_br_gla'], w['w_br_mla'])
    (dy_na, do_gla, dy_mla, dgr, dgates, dw0, dw1, dw2, dwo, dgn) = _mix_out_bwd(
        dx2, sv['y_na'], sv['o_gla'], sv['y_mla'], z, col('gr'), w['gla_out_norm'], w_br, w['w_out'], f"mix_out_bwd_{l}")
    grads.update(w_br_na=dw0, w_br_gla=dw1, w_br_mla=dw2, w_out=dwo, gla_out_norm=dgn)
    dq, dk, dv, dbias, dgq, dgk = _carried(_na_bwd, carry, 'na', sv['na_q'], sv['na_k'], sv['na_v'], dy_na, na_bias, l,
                                          w['na_q_norm'], w['na_k_norm'], f"na_bwd_{l}")
    grads.update(na_q_norm=dgq, na_k_norm=dgk, na_rpb=_na_bias_tables_bwd(dbias, onehot))
    gdq, gdk, gdv, gdg = _gla_bwd(sv['gq'], sv['gk'], sv['gv'], sv['gg'], do_gla, f"gla_bwd_{l}")
    dglow, dupf, dupb, dbf, dbb = _gla_gate_bwd(z, col('glow'), w['up_f'], w['up_b'], w['gla_gf_bias'], w['gla_gb_bias'],
                                                gdg, f"gla_gate_bwd_{l}")
    grads.update(gla_gf_up=dupf[:GLA_RANK], gla_gb_up=dupb[GLA_RANK:2 * GLA_RANK], gla_gf_bias=dbf, gla_gb_bias=dbb)
    mdq, mdk, mdv = _carried(_mla_attn_bwd, carry, 'mla', sv['mq'], sv['mk'], sv['mv'], sv['y_mla'], sv['lse'], dy_mla,
                             f"mla_attn_bwd_{l}")
    dq_raw, dgqn = _mla_qprep_bwd(sv['q_raw'], w['mla_q_norm'], ctab, stab, mdq, f"mla_qprep_bwd_{l}")
    dkv_raw, dkr, dgkn = _mla_kprep_bwd(sv['kv_raw'], z, col('k_rope'), w['mla_k_norm'], ctab, stab, mdk, mdv,
                                        f"mla_kprep_bwd_{l}")
    dcq, dgcq = _norm_matmul_bwd(dq_raw, w['mla_w_uq'], z, col('c_q'), w['mla_cq_norm'], None, f"mla_uq_bwd_{l}")
    dckv, dgckv = _norm_matmul_bwd(dkv_raw, w['mla_w_ukv'], z, col('c_kv'), w['mla_ckv_norm'], None, f"mla_ukv_bwd_{l}")
    grads.update(mla_q_norm=dgqn, mla_k_norm=dgkn, mla_cq_norm=dgcq, mla_ckv_norm=dgckv,
                 mla_w_uq=_matmul_tn(sv['hq'], dq_raw, f"mla_duq_{l}", F32),
                 mla_w_ukv=_matmul_tn(sv['hkv'], dkv_raw, f"mla_dukv_{l}", F32))
    dz = _assemble_dz(dgates, (dq, dk, dv), (gdq, gdk), gdv, dgr, dcq, dckv, dglow, dkr, lay, z.shape[1], f"dz_{l}")
    dx1, dgmix = _carried(_norm_matmul_bwd, carry, 'mix_in', dz, w['w_in_z'], sv['x1'], 0, w['mix_norm'], dx2,
                          f"mix_in_bwd_{l}")
    grads.update(mix_norm=dgmix, w_in=_matmul_tn(sv['hb'], dz, f"mix_dw_in_{l}"))
    return _ffn_grads(grads, sv['x0'], dx1, sv['a1'], sv['b1'], w, 'ffn1', l, carry)


EXCHANGE_PARTS = {
    'F1w1': [['ffn1_w1']],
    'F1w3': [['ffn1_w3']],
    'F1b': [['ffn1_w2']],
    'F2': [['ffn2_w1', 'ffn2_w3'], ['ffn2_w2']],
    'C': [['w_in']],
    'S': [['w_br_na', 'w_br_gla', 'w_br_mla'], ['w_out'], ['mla_w_uq'], ['mla_w_ukv'], ['gla_gf_up', 'gla_gb_up']],
}
FWD_PLAN = {'ffn1': [(0, 'C')], 'mix_in': [(0, 'S')], 'mla': [(0, 'F2')], 'na': [(1, 'F1b')],
            'ffn2': [(1, 'F1w1'), (1, 'F1w3')]}
BWD_PLAN = {'mla': [(0, 'F2')], 'mix_in': [(0, 'S')], 'ffn1': [(0, 'C')], 'ffn2': [(1, 'F1w1'), (1, 'F1w3')],
            'na': [(1, 'F1b')]}
BWD_PLAN_LAYER0 = {'ffn1_dw3': [(0, 'F1w1')], 'ffn1_dw2': [(0, 'F1w3')]}
FWD_EDGE = ['F1w1', 'F1w3', 'F1b']
BWD_EDGE = ['F1b']


def _carried(fn, carry, key, *args):
    if key in carry:
        make_srcs, broadcast, deliver = carry[key]
        outs, landed = fn(*args, exch=(make_srcs(), broadcast))
        deliver(landed)
        return outs
    return fn(*args)


def kernel(x, ffn1_norm, ffn1_w1, ffn1_w3, ffn1_w2, mix_norm, w_in, na_q_norm, na_k_norm, na_rpb, gla_gf_up, gla_gf_bias, gla_gb_up, gla_gb_bias, gla_out_norm, mla_cq_norm, mla_ckv_norm, mla_w_uq, mla_w_ukv, mla_q_norm, mla_k_norm, w_br_na, w_br_gla, w_br_mla, w_out, ffn2_norm, ffn2_w1, ffn2_w3, ffn2_w2, loss_target, m_ffn1_norm, m_ffn1_w1, m_ffn1_w3, m_ffn1_w2, m_mix_norm, m_w_in, m_na_q_norm, m_na_k_norm, m_na_rpb, m_gla_gf_up, m_gla_gf_bias, m_gla_gb_up, m_gla_gb_bias, m_gla_out_norm, m_mla_cq_norm, m_mla_ckv_norm, m_mla_w_uq, m_mla_w_ukv, m_mla_q_norm, m_mla_k_norm, m_w_br_na, m_w_br_gla, m_w_br_mla, m_w_out, m_ffn2_norm, m_ffn2_w1, m_ffn2_w3, m_ffn2_w2, v_ffn1_norm, v_ffn1_w1, v_ffn1_w3, v_ffn1_w2, v_mix_norm, v_w_in, v_na_q_norm, v_na_k_norm, v_na_rpb, v_gla_gf_up, v_gla_gf_bias, v_gla_gb_up, v_gla_gb_bias, v_gla_out_norm, v_mla_cq_norm, v_mla_ckv_norm, v_mla_w_uq, v_mla_w_ukv, v_mla_q_norm, v_mla_k_norm, v_w_br_na, v_w_br_gla, v_w_br_mla, v_w_out, v_ffn2_norm, v_ffn2_w1, v_ffn2_w3, v_ffn2_w2):
    given = dict(locals())
    P = {n: given[n] for n in WEIGHTS}
    M = {n: given['m_' + n] for n in WEIGHTS}
    V = {n: given['v_' + n] for n in WEIGHTS}
    xs = x[0]
    T, D = xs.shape
    L = ffn1_norm.shape[0]
    lay, zp, d_in = _z_layout(D)

    def arrays_of(items):
        return [(l, names) for l, p in items for names in EXCHANGE_PARTS[p]]

    ws = [{n: P[n][l][None, :] for n in REPLICATED if n != 'na_rpb'} for l in range(L)]

    def gather_srcs(items):
        return [jnp.stack([P[n][l].astype(BF16) for n in names]) for l, names in arrays_of(items)]

    def take_gathered(items, res):
        for (l, names), g in zip(arrays_of(items), res):
            for k, n in enumerate(names):
                blk = g[:, k]
                if n == 'w_in':
                    ws[l]['w_in_z'] = _z_cols_from_blocks(blk, lay)
                    continue
                if n in COL_SHARDED:
                    full = jnp.concatenate([blk[d] for d in range(N_DEV)], axis=1)
                else:
                    full = blk.reshape(N_DEV * blk.shape[1], blk.shape[2])
                if n == 'gla_gf_up':
                    ws[l]['up_f'] = jnp.zeros((LANE, GLA_QK_W), BF16).at[:GLA_RANK].set(full)
                elif n == 'gla_gb_up':
                    ws[l]['up_b'] = jnp.zeros((LANE, GLA_QK_W), BF16).at[GLA_RANK:2 * GLA_RANK].set(full)
                else:
                    ws[l][n] = full

    def plan(table, l, make_srcs, broadcast, deliver):
        carry = {}
        for key, items in table.items():
            items = [(l + off, p) for off, p in items if l + off < L]
            if items:
                carry[key] = (functools.partial(make_srcs, items), broadcast, functools.partial(deliver, items))
        return carry

    ctab, stab = _rope_tables(T)
    na_bias, onehot = _na_bias_tables(na_rpb)

    edge = [(0, p) for p in FWD_EDGE]
    take_gathered(edge, _exchange(gather_srcs(edge), True, "gather_weights_edge"))
    saved = []
    h = xs
    for l in range(L):
        h, sv = _layer_fwd(h, ws[l], l, lay, (ctab, stab, na_bias), plan(FWD_PLAN, l, gather_srcs, True, take_gathered))
        saved.append(sv)
    dh, loss_local = _loss_head(h, loss_target[0], "loss_head")
    loss = lax.psum(loss_local[0, 0], ("x", "y", "c"))

    layer_grads = [dict() for _ in range(L)]

    def scatter_srcs(items):
        def blocks(l, n):
            g = layer_grads[l][n]
            if n == 'w_in':
                return _blocks_from_z_cols(g, lay, d_in).astype(BF16)
            if n in COL_SHARDED:
                c = g.shape[1] // N_DEV
                return jnp.stack([g[:, d * c:(d + 1) * c] for d in range(N_DEV)]).astype(BF16)
            return g.reshape(N_DEV, -1, g.shape[1]).astype(BF16)
        return [jnp.stack([blocks(l, n) for n in names], axis=1) for l, names in arrays_of(items)]

    landed = [dict() for _ in range(L)]

    def take_landed(items, res):
        for (l, names), r in zip(arrays_of(items), res):
            for k, n in enumerate(names):
                landed[l][n] = (r, k)

    for l in reversed(range(L)):
        table = {**BWD_PLAN, **BWD_PLAN_LAYER0} if l == 0 else BWD_PLAN
        dh = _layer_bwd(layer_grads[l], dh, saved[l], ws[l], l, lay, (ctab, stab, na_bias, onehot),
                        plan(table, l, scatter_srcs, False, take_landed))
    grad_x = dh[None]
    edge = [(0, p) for p in BWD_EDGE]
    take_landed(edge, _exchange(scatter_srcs(edge), False, "scatter_grads_edge"))

    out = {}
    for n in SHARDED:
        out[n] = _adamw_sharded([landed[l][n][0] for l in range(L)], landed[0][n][1], P[n], M[n], V[n], f"adamw_{n}")

    rep = [n for n in REPLICATED]
    rep_shapes = [tuple(P[n].shape) for n in rep]
    rep_partial = _pack([jnp.stack([layer_grads[l][n].reshape(P[n].shape[1:]) for l in range(L)]) for n in rep])
    (rep_landed,) = _exchange([rep_partial], True, "gather_small_grads")
    packed = _adamw_replicated(rep_landed, _pack([P[n] for n in rep]), _pack([M[n] for n in rep]),
                               _pack([V[n] for n in rep]), "adamw_replicated")
    for kind, pk in enumerate(packed):
        for n, arr in zip(rep, _unpack(pk, rep_shapes)):
            out.setdefault(n, [None] * 4)[kind] = arr

    res = [loss, grad_x]
    for kind in range(4):
        res += [out[n][kind] for n in WEIGHTS]
    return tuple(res)
```

```python
import functools

import numpy as np
import jax
import jax.numpy as jnp
from jax import lax
from jax.experimental import pallas as pl
from jax.experimental.pallas import tpu as pltpu

F32 = jnp.float32
BF16 = jnp.bfloat16
SDS = jax.ShapeDtypeStruct
HI = lax.Precision.HIGHEST

N_DEV = 8
DEPTH = 4
EPS = 1e-6
LANE = 128
VMEM_LIMIT = 56 * 1024 * 1024

GRID_W = 64
NA_HEADS, NA_HD, NA_WIN_R, NA_WIN_C = 8, 64, 8, 16
GLA_HEADS, GLA_DK, GLA_DV, GLA_RANK, GLA_TAU, GLA_CHUNK = 4, 64, 128, 16, 16.0, 64
MLA_HEADS, MLA_QR, MLA_KVR, MLA_NOPE, MLA_ROPE, MLA_V = 4, 256, 256, 128, 64, 128
MLA_QK = MLA_NOPE + MLA_ROPE
ROPE_THETA = 10000.0
NA_W = NA_HEADS * NA_HD
GLA_QK_W = GLA_HEADS * GLA_DK
GLA_V_W = GLA_HEADS * GLA_DV
MLA_V_W = MLA_HEADS * MLA_V

ADAM_LR, ADAM_B1, ADAM_B2, ADAM_EPS, ADAM_WD, ADAM_STEP = 0.001, 0.9, 0.999, 1e-08, 0.01, 10

WEIGHTS = ['ffn1_norm', 'ffn1_w1', 'ffn1_w3', 'ffn1_w2', 'mix_norm', 'w_in', 'na_q_norm', 'na_k_norm', 'na_rpb',
           'gla_gf_up', 'gla_gf_bias', 'gla_gb_up', 'gla_gb_bias', 'gla_out_norm', 'mla_cq_norm', 'mla_ckv_norm',
           'mla_w_uq', 'mla_w_ukv', 'mla_q_norm', 'mla_k_norm', 'w_br_na', 'w_br_gla', 'w_br_mla', 'w_out',
           'ffn2_norm', 'ffn2_w1', 'ffn2_w3', 'ffn2_w2']
COL_SHARDED = ['ffn1_w1', 'ffn1_w3', 'w_in', 'gla_gf_up', 'gla_gb_up', 'mla_w_uq', 'mla_w_ukv', 'w_br_na', 'w_br_gla',
               'w_br_mla', 'ffn2_w1', 'ffn2_w3']
ROW_SHARDED = ['ffn1_w2', 'w_out', 'ffn2_w2']
SHARDED = [n for n in WEIGHTS if n in COL_SHARDED or n in ROW_SHARDED]
REPLICATED = [n for n in WEIGHTS if n not in SHARDED]

NT = (((1,), (1,)), ((), ()))
TN = (((0,), (0,)), ((), ()))


def _tile(n, pref, mult=8):
    best = None
    for t in range(mult, min(n, pref) + 1, mult):
        if n % t == 0:
            best = t
    return best if best is not None else n


def _params(n_axes):
    return pltpu.CompilerParams(dimension_semantics=("arbitrary",) * n_axes, vmem_limit_bytes=VMEM_LIMIT)


def _dot(a, b, dims=None, precision=None):
    if dims is None:
        return jnp.dot(a, b, preferred_element_type=F32, precision=precision)
    return lax.dot_general(a, b, dims, preferred_element_type=F32, precision=precision)


def _sigmoid(x):
    return 1.0 / (1.0 + jnp.exp(-x))


def _rstd(x):
    return lax.rsqrt(jnp.mean(x * x, axis=-1, keepdims=True) + EPS)


def _rms_bwd(dy, x, g, r):
    xh = x * r
    dyg = dy * g
    dx = r * (dyg - xh * jnp.mean(dyg * xh, axis=-1, keepdims=True))
    return dx, jnp.sum(dy * xh, axis=0, keepdims=True)


def _z_layout(d_model):
    own = {}
    off = 0
    for name, w in [('na_q', NA_W), ('na_k', NA_W), ('na_v', NA_W), ('gq', GLA_QK_W), ('gk', GLA_QK_W),
                    ('gv', GLA_V_W), ('gr', GLA_V_W), ('gfl', GLA_RANK), ('gbl', GLA_RANK), ('c_q', MLA_QR),
                    ('c_kv', MLA_KVR), ('k_rope', MLA_ROPE), ('gates', 3 * d_model)]:
        own[name] = (off, w)
        off += w
    order = [('gates', 3 * d_model), ('na_q', NA_W), ('na_k', NA_W), ('na_v', NA_W), ('gv', GLA_V_W), ('gr', GLA_V_W),
             ('gq', GLA_QK_W), ('gk', GLA_QK_W), ('c_q', MLA_QR), ('c_kv', MLA_KVR), ('glow', LANE), ('k_rope', LANE)]
    lay = {}
    z = 0
    for name, w in order:
        assert z % w == 0, (name, z, w)
        if name == 'glow':
            lay[name] = (z, w, own['gfl'][0], 2 * GLA_RANK)
        else:
            lay[name] = (z, w, own[name][0], own[name][1])
        z += w
    return lay, z, off


def _z_cols_from_blocks(blocks, lay):
    cs = blocks.shape[2]
    parts = []
    for zo, zw, oo, ow in lay.values():
        a = oo
        while a < oo + ow:
            b = min(oo + ow, (a // cs + 1) * cs)
            parts.append(blocks[a // cs, :, a % cs:a % cs + (b - a)])
            a = b
        if zw > ow:
            parts.append(jnp.zeros((blocks.shape[1], zw - ow), blocks.dtype))
    return jnp.concatenate(parts, axis=1)


def _blocks_from_z_cols(wz, lay, d_in):
    cs = d_in // N_DEV
    own = sorted(lay.values(), key=lambda t: t[2])
    out = []
    for k in range(N_DEV):
        parts = []
        for zo, zw, oo, ow in own:
            a, b = max(oo, k * cs), min(oo + ow, (k + 1) * cs)
            if a < b:
                parts.append(wz[:, zo + a - oo:zo + b - oo])
        out.append(jnp.concatenate(parts, axis=1))
    return jnp.stack(out)


def _ffn_fwd(x, g, w1, w3, w2, name, exch=None):
    T, D = x.shape
    Fd = w1.shape[1]
    tm, tf = _tile(T, 1024), _tile(Fd, 256, LANE)
    nj = Fd // tf

    def body(x_ref, g_ref, w1_ref, w3_ref, w2_ref, o_ref, a_ref, b_ref, h_scr, acc):
        j = pl.program_id(1)

        @pl.when(j == 0)
        def _():
            xv = x_ref[...]
            h_scr[...] = (xv * _rstd(xv) * g_ref[...]).astype(BF16)
            acc[...] = jnp.zeros_like(acc)

        h = h_scr[...]
        a = _dot(h, w1_ref[...])
        b = _dot(h, w3_ref[...])
        a_ref[...] = a.astype(BF16)
        b_ref[...] = b.astype(BF16)
        u = a * _sigmoid(a) * b
        acc[...] += _dot(u.astype(BF16), w2_ref[...])

        @pl.when(j == nj - 1)
        def _():
            o_ref[...] = x_ref[...] + 0.5 * acc[...]

    return _call(
        body, name=name, grid=(T // tm, nj),
        in_specs=[pl.BlockSpec((tm, D), lambda i, j: (i, 0)), pl.BlockSpec((1, D), lambda i, j: (0, 0)),
                  pl.BlockSpec((D, tf), lambda i, j: (0, j)), pl.BlockSpec((D, tf), lambda i, j: (0, j)),
                  pl.BlockSpec((tf, D), lambda i, j: (j, 0))],
        out_specs=[pl.BlockSpec((tm, D), lambda i, j: (i, 0)), pl.BlockSpec((tm, tf), lambda i, j: (i, j)),
                   pl.BlockSpec((tm, tf), lambda i, j: (i, j))],
        out_shape=[SDS((T, D), F32), SDS((T, Fd), BF16), SDS((T, Fd), BF16)],
        scratch_shapes=[pltpu.VMEM((tm, D), BF16), pltpu.VMEM((tm, D), F32)],
        args=(x, g, w1, w3, w2), exch=exch)


def _ffn_bwd(x, g, dy, a, b, w1, w3, w2, name, exch=None):
    T, D = x.shape
    Fd = w1.shape[1]
    tm, tf = _tile(T, 1024), _tile(Fd, 256, LANE)
    nj = Fd // tf

    def body(x_ref, g_ref, dy_ref, a_ref, b_ref, w1_ref, w3_ref, w2_ref,
             dx_ref, da_ref, db_ref, u_ref, hb_ref, dyb_ref, dg_ref, dh_acc):
        i, j = pl.program_id(0), pl.program_id(1)

        @pl.when(j == 0)
        def _():
            xv = x_ref[...]
            hb_ref[...] = (xv * _rstd(xv) * g_ref[...]).astype(BF16)
            dyb_ref[...] = (0.5 * dy_ref[...]).astype(BF16)
            dh_acc[...] = jnp.zeros_like(dh_acc)

        @pl.when((i == 0) & (j == 0))
        def _():
            dg_ref[...] = jnp.zeros_like(dg_ref)

        du = _dot(dyb_ref[...], w2_ref[...], NT)
        av = a_ref[...].astype(F32)
        bv = b_ref[...].astype(F32)
        s = _sigmoid(av)
        sl = av * s
        da = (du * bv * (s * (1.0 + av * (1.0 - s)))).astype(BF16)
        db = (du * sl).astype(BF16)
        da_ref[...] = da
        db_ref[...] = db
        u_ref[...] = (sl * bv).astype(BF16)
        dh_acc[...] += _dot(da, w1_ref[...], NT) + _dot(db, w3_ref[...], NT)

        @pl.when(j == nj - 1)
        def _():
            xv = x_ref[...]
            dxn, dg = _rms_bwd(dh_acc[...], xv, g_ref[...], _rstd(xv))
            dx_ref[...] = dy_ref[...] + dxn
            dg_ref[...] += dg

    row = lambda i, j: (i, 0)
    return _call(
        body, name=name, grid=(T // tm, nj),
        in_specs=[pl.BlockSpec((tm, D), row), pl.BlockSpec((1, D), lambda i, j: (0, 0)), pl.BlockSpec((tm, D), row),
                  pl.BlockSpec((tm, tf), lambda i, j: (i, j)), pl.BlockSpec((tm, tf), lambda i, j: (i, j)),
                  pl.BlockSpec((D, tf), lambda i, j: (0, j)), pl.BlockSpec((D, tf), lambda i, j: (0, j)),
                  pl.BlockSpec((tf, D), lambda i, j: (j, 0))],
        out_specs=[pl.BlockSpec((tm, D), row), pl.BlockSpec((tm, tf), lambda i, j: (i, j)),
                   pl.BlockSpec((tm, tf), lambda i, j: (i, j)), pl.BlockSpec((tm, tf), lambda i, j: (i, j)),
                   pl.BlockSpec((tm, D), row), pl.BlockSpec((tm, D), row), pl.BlockSpec((1, D), lambda i, j: (0, 0))],
        out_shape=[SDS((T, D), F32), SDS((T, Fd), BF16), SDS((T, Fd), BF16), SDS((T, Fd), BF16),
                   SDS((T, D), BF16), SDS((T, D), BF16), SDS((1, D), F32)],
        scratch_shapes=[pltpu.VMEM((tm, D), F32)],
        args=(x, g, dy, a, b, w1, w3, w2), exch=exch)


def _matmul_tn(a, b, name, out_dtype=BF16, exch=None):
    T, K = a.shape
    N = b.shape[1]
    tk = _tile(K, 1408, LANE)
    tn = _tile(N, 1408, LANE)
    tt = _tile(T, 1024)
    nt = T // tt

    def body(a_ref, b_ref, o_ref, acc):
        t = pl.program_id(2)

        @pl.when(t == 0)
        def _():
            acc[...] = jnp.zeros_like(acc)

        acc[...] += _dot(a_ref[...], b_ref[...], TN)

        @pl.when(t == nt - 1)
        def _():
            o_ref[...] = acc[...].astype(out_dtype)

    res = _call(
        body, name=name, grid=(K // tk, N // tn, nt),
        in_specs=[pl.BlockSpec((tt, tk), lambda k, n, t: (t, k)), pl.BlockSpec((tt, tn), lambda k, n, t: (t, n))],
        out_specs=[pl.BlockSpec((tk, tn), lambda k, n, t: (k, n))],
        out_shape=[SDS((K, N), out_dtype)],
        scratch_shapes=[pltpu.VMEM((tk, tn), F32)],
        args=(a, b), exch=exch)
    return res[0] if exch is None else (res[0][0], res[1])


def _norm_matmul(x, xcol, g, w, name, exch=None):
    T = x.shape[0]
    D = g.shape[1]
    N = w.shape[1]
    tm, tn = _tile(T, 1024), _tile(N, 768, LANE)

    def body(x_ref, g_ref, w_ref, z_ref, hb_ref):
        @pl.when(pl.program_id(1) == 0)
        def _():
            xv = x_ref[...]
            hb_ref[...] = (xv * _rstd(xv) * g_ref[...]).astype(BF16)

        z_ref[...] = _dot(hb_ref[...], w_ref[...])

    return _call(
        body, name=name, grid=(T // tm, N // tn),
        in_specs=[pl.BlockSpec((tm, D), lambda i, j: (i, xcol)), pl.BlockSpec((1, D), lambda i, j: (0, 0)),
                  pl.BlockSpec((D, tn), lambda i, j: (0, j))],
        out_specs=[pl.BlockSpec((tm, tn), lambda i, j: (i, j)), pl.BlockSpec((tm, D), lambda i, j: (i, 0))],
        out_shape=[SDS((T, N), F32), SDS((T, D), BF16)],
        args=(x, g, w), exch=exch)


def _norm_matmul_bwd(dz, w, x, xcol, g, resid, name, exch=None):
    T, N = dz.shape
    D = g.shape[1]
    tm, tn = _tile(T, 1024), _tile(N, 768, LANE)
    nj = N // tn
    has_resid = resid is not None

    def body(*refs):
        if has_resid:
            dz_ref, w_ref, x_ref, g_ref, r_ref, dx_ref, dg_ref, acc = refs
        else:
            dz_ref, w_ref, x_ref, g_ref, dx_ref, dg_ref, acc = refs
        i, j = pl.program_id(0), pl.program_id(1)

        @pl.when(j == 0)
        def _():
            acc[...] = jnp.zeros_like(acc)

        @pl.when((i == 0) & (j == 0))
        def _():
            dg_ref[...] = jnp.zeros_like(dg_ref)

        acc[...] += _dot(dz_ref[...], w_ref[...], NT)

        @pl.when(j == nj - 1)
        def _():
            xv = x_ref[...]
            dxn, dg = _rms_bwd(acc[...], xv, g_ref[...], _rstd(xv))
            dx_ref[...] = dxn + r_ref[...] if has_resid else dxn
            dg_ref[...] += dg

    in_specs = [pl.BlockSpec((tm, tn), lambda i, j: (i, j)), pl.BlockSpec((D, tn), lambda i, j: (0, j)),
                pl.BlockSpec((tm, D), lambda i, j: (i, xcol)), pl.BlockSpec((1, D), lambda i, j: (0, 0))]
    args = [dz, w, x, g]
    if has_resid:
        in_specs.append(pl.BlockSpec((tm, D), lambda i, j: (i, 0)))
        args.append(resid)
    return _call(
        body, name=name, grid=(T // tm, nj), in_specs=in_specs,
        out_specs=[pl.BlockSpec((tm, D), lambda i, j: (i, 0)), pl.BlockSpec((1, D), lambda i, j: (0, 0))],
        out_shape=[SDS((T, D), F32), SDS((1, D), F32)],
        scratch_shapes=[pltpu.VMEM((tm, D), F32)],
        args=args, exch=exch)


def _na_bias_tables(rpb):
    L = rpb.shape[0]
    qc = np.arange(GRID_W)[:, None]
    kc = np.arange(GRID_W)[None, :]
    dc = np.clip(kc - qc + NA_WIN_C - 1, 0, 2 * NA_WIN_C - 2)
    c0 = np.clip(qc - NA_WIN_C // 2, 0, GRID_W - NA_WIN_C)
    ok = (kc >= c0) & (kc < c0 + NA_WIN_C)
    onehot = (dc.reshape(-1)[None, :] == np.arange(2 * NA_WIN_C - 1)[:, None]).astype(np.float32)
    t1 = jnp.stack([rpb[:, :, c:c + NA_WIN_R, :] for c in range(NA_WIN_R)], axis=2)
    full = jnp.dot(t1.reshape(-1, 2 * NA_WIN_C - 1), jnp.asarray(onehot), precision=HI)
    full = full.reshape(L, NA_HEADS, NA_WIN_R, NA_WIN_R, GRID_W, GRID_W)
    full = jnp.where(jnp.asarray(ok)[None, None, None, None], full, -1e30)
    return jnp.transpose(full, (0, 1, 2, 4, 3, 5)).reshape(L * NA_HEADS, NA_WIN_R, GRID_W, NA_WIN_R * GRID_W), onehot


def _na_bias_tables_bwd(db, onehot):
    d = db.reshape(NA_HEADS, NA_WIN_R, GRID_W, NA_WIN_R, GRID_W)
    d = jnp.transpose(d, (0, 1, 3, 2, 4)).reshape(-1, GRID_W * GRID_W)
    t1 = jnp.dot(d, jnp.asarray(onehot.T), precision=HI).reshape(NA_HEADS, NA_WIN_R, NA_WIN_R, 2 * NA_WIN_C - 1)
    out = jnp.zeros((NA_HEADS, 2 * NA_WIN_R - 1, 2 * NA_WIN_C - 1), F32)
    for c in range(NA_WIN_R):
        out = out.at[:, c:c + NA_WIN_R, :].add(t1[:, c])
    return out


def _na_windows(bi, rb, rows):
    wsl, cls = [], []
    for i in range(rb):
        r = bi * rb + i
        r0 = jnp.clip(r - NA_WIN_R // 2, 0, rows - NA_WIN_R)
        wsl.append(pl.ds(pl.multiple_of(r0 * GRID_W, GRID_W), NA_WIN_R * GRID_W))
        cls.append(r0 - r + NA_WIN_R - 1)
    return wsl, cls


def _na_fwd(q, k, v, bias, layer, gq, gk, name, exch=None):
    H, T, d = q.shape
    rows = T // GRID_W
    rb = _tile(rows, 8, 1)
    win = NA_WIN_R * GRID_W
    scale = d ** -0.5

    def body(q_ref, k_ref, v_ref, b_ref, gq_ref, gk_ref, o_ref, qn, kn, vb):
        qv, kv = q_ref[0], k_ref[0]
        qn[...] = (qv * _rstd(qv) * gq_ref[...] * scale).astype(BF16)
        kn[...] = (kv * _rstd(kv) * gk_ref[...]).astype(BF16)
        vb[...] = v_ref[0].astype(BF16)

        def block(bi, c):
            wsl, cls = _na_windows(bi, rb, rows)
            qsl = pl.ds(pl.multiple_of(bi * (rb * GRID_W), rb * GRID_W), rb * GRID_W)
            kw = jnp.stack([kn[w, :] for w in wsl])
            vw = jnp.stack([vb[w, :] for w in wsl])
            s = _bdot(qn[qsl, :].reshape(rb, GRID_W, d), kw, 2, 2) + jnp.stack([b_ref[0, c_] for c_ in cls])
            p = jnp.exp(s - jnp.max(s, axis=-1, keepdims=True))
            p = p / jnp.sum(p, axis=-1, keepdims=True)
            o_ref[0, qsl, :] = _bdot(p.astype(BF16), vw, 2, 1).reshape(rb * GRID_W, d)
            return c

        lax.fori_loop(0, rows // rb, block, 0)

    hm = pl.BlockSpec((1, T, d), lambda h: (h, 0, 0))
    gs = pl.BlockSpec((1, d), lambda h: (0, 0))
    return _call(
        body, name=name, grid=(H,),
        in_specs=[hm, hm, hm, pl.BlockSpec((1, NA_WIN_R, GRID_W, win), lambda h: (layer * H + h, 0, 0, 0)), gs, gs],
        out_specs=[hm], out_shape=[SDS((H, T, d), F32)],
        scratch_shapes=[pltpu.VMEM((T, d), BF16)] * 3,
        args=(q, k, v, bias, gq, gk), exch=exch)


def _na_bwd(q, k, v, do, bias, layer, gq, gk, name, exch=None):
    H, T, d = q.shape
    rows = T // GRID_W
    rb = _tile(rows, 8, 1)
    win = NA_WIN_R * GRID_W
    scale = d ** -0.5

    def body(q_ref, k_ref, v_ref, do_ref, b_ref, gq_ref, gk_ref,
             dq_ref, dk_ref, dv_ref, db_ref, dgq_ref, dgk_ref, qn, kn, vb, dob, dqn, dkn):
        h = pl.program_id(0)
        qv, kv = q_ref[0], k_ref[0]
        rq, rk = _rstd(qv), _rstd(kv)
        qn[...] = (qv * rq * gq_ref[...] * scale).astype(BF16)
        kn[...] = (kv * rk * gk_ref[...]).astype(BF16)
        vb[...] = v_ref[0].astype(BF16)
        dob[...] = do_ref[0].astype(BF16)
        dkn[...] = jnp.zeros_like(dkn)
        dv_ref[...] = jnp.zeros_like(dv_ref)
        db_ref[...] = jnp.zeros_like(db_ref)

        @pl.when(h == 0)
        def _():
            dgq_ref[...] = jnp.zeros_like(dgq_ref)
            dgk_ref[...] = jnp.zeros_like(dgk_ref)

        def block(bi, c):
            wsl, cls = _na_windows(bi, rb, rows)
            qsl = pl.ds(pl.multiple_of(bi * (rb * GRID_W), rb * GRID_W), rb * GRID_W)
            qs = qn[qsl, :].reshape(rb, GRID_W, d)
            dos = dob[qsl, :].reshape(rb, GRID_W, d)
            kw = jnp.stack([kn[w, :] for w in wsl])
            vw = jnp.stack([vb[w, :] for w in wsl])
            s = _bdot(qs, kw, 2, 2) + jnp.stack([b_ref[0, c_] for c_ in cls])
            p = jnp.exp(s - jnp.max(s, axis=-1, keepdims=True))
            p = p / jnp.sum(p, axis=-1, keepdims=True)
            dp = _bdot(dos, vw, 2, 2)
            ds = p * (dp - jnp.sum(dp * p, axis=-1, keepdims=True))
            dsb = ds.astype(BF16)
            dqn[qsl, :] = _bdot(dsb, kw, 2, 1).reshape(rb * GRID_W, d)
            dvw = _bdot(p.astype(BF16), dos, 1, 1)
            dkw = _bdot(dsb, qs, 1, 1)
            for i in range(rb):
                db_ref[0, cls[i]] += ds[i]
                dv_ref[0, wsl[i], :] += dvw[i]
                dkn[wsl[i], :] += dkw[i]
            return c

        lax.fori_loop(0, rows // rb, block, 0)
        dq, dgq = _rms_bwd(dqn[...] * scale, qv, gq_ref[...], rq)
        dk, dgk = _rms_bwd(dkn[...], kv, gk_ref[...], rk)
        dq_ref[0] = dq
        dk_ref[0] = dk
        dgq_ref[...] += dgq
        dgk_ref[...] += dgk

    hm = pl.BlockSpec((1, T, d), lambda h: (h, 0, 0))
    gs = pl.BlockSpec((1, d), lambda h: (0, 0))
    bs = pl.BlockSpec((1, NA_WIN_R, GRID_W, win), lambda h: (h, 0, 0, 0))
    return _call(
        body, name=name, grid=(H,),
        in_specs=[hm, hm, hm, hm, pl.BlockSpec((1, NA_WIN_R, GRID_W, win), lambda h: (layer * H + h, 0, 0, 0)), gs, gs],
        out_specs=[hm, hm, hm, bs, gs, gs],
        out_shape=[SDS((H, T, d), F32)] * 3 + [SDS((H, NA_WIN_R, GRID_W, win), F32), SDS((1, d), F32), SDS((1, d), F32)],
        scratch_shapes=[pltpu.VMEM((T, d), BF16)] * 4 + [pltpu.VMEM((T, d), F32)] * 2,
        args=(q, k, v, do, bias, gq, gk), exch=exch)


def _gla_gate_fwd(z, col, upf, upb, bf, bb, name):
    T = z.shape[0]
    W = upf.shape[1]
    tm = _tile(T, 1024)

    def body(z_ref, upf_ref, upb_ref, bf_ref, bb_ref, g_ref):
        seg = z_ref[...].astype(BF16)
        for rev, (up_ref, b_ref) in enumerate(((upf_ref, bf_ref), (upb_ref, bb_ref))):
            pre = _dot(seg, up_ref[...]) + b_ref[...]
            g = (jnp.minimum(pre, 0.0) - jnp.log(1.0 + jnp.exp(-jnp.abs(pre)))) * (1.0 / GLA_TAU)
            for h in range(GLA_HEADS):
                g_ref[rev, h] = g[:, h * GLA_DK:(h + 1) * GLA_DK]

    full = lambda shape: pl.BlockSpec(shape, lambda i: (0, 0))
    return pl.pallas_call(
        body, name=name, grid=(T // tm,),
        in_specs=[pl.BlockSpec((tm, LANE), lambda i: (i, col)), full((LANE, W)), full((LANE, W)), full((1, W)), full((1, W))],
        out_specs=pl.BlockSpec((2, GLA_HEADS, tm, GLA_DK), lambda i: (0, 0, i, 0)),
        out_shape=SDS((2, GLA_HEADS, T, GLA_DK), F32),
        compiler_params=_params(1),
    )(z, upf, upb, bf, bb)


def _gla_gate_bwd(z, col, upf, upb, bf, bb, dg, name):
    T = z.shape[0]
    W = upf.shape[1]
    tm = _tile(T, 1024)

    def body(z_ref, upf_ref, upb_ref, bf_ref, bb_ref, dg_ref, dseg_ref, dupf_ref, dupb_ref, dbf_ref, dbb_ref):
        @pl.when(pl.program_id(0) == 0)
        def _():
            for r in (dupf_ref, dupb_ref, dbf_ref, dbb_ref):
                r[...] = jnp.zeros_like(r)

        seg = z_ref[...].astype(BF16)
        dseg = jnp.zeros(dseg_ref.shape, F32)
        for rev, (up_ref, b_ref, dup_ref, db_ref) in enumerate(((upf_ref, bf_ref, dupf_ref, dbf_ref),
                                                               (upb_ref, bb_ref, dupb_ref, dbb_ref))):
            pre = _dot(seg, up_ref[...]) + b_ref[...]
            dgv = jnp.concatenate([dg_ref[rev, h] for h in range(GLA_HEADS)], axis=1)
            dpre = dgv * (1.0 / GLA_TAU) * (1.0 - _sigmoid(pre))
            dpb = dpre.astype(BF16)
            dseg = dseg + _dot(dpb, up_ref[...], NT)
            dup_ref[...] += _dot(seg, dpb, TN)
            db_ref[...] += jnp.sum(dpre, axis=0, keepdims=True)
        dseg_ref[...] = dseg

    full = lambda shape: pl.BlockSpec(shape, lambda i: (0, 0))
    return pl.pallas_call(
        body, name=name, grid=(T // tm,),
        in_specs=[pl.BlockSpec((tm, LANE), lambda i: (i, col)), full((LANE, W)), full((LANE, W)), full((1, W)), full((1, W)),
                  pl.BlockSpec((2, GLA_HEADS, tm, GLA_DK), lambda i: (0, 0, i, 0))],
        out_specs=[pl.BlockSpec((tm, LANE), lambda i: (i, 0)), full((LANE, W)), full((LANE, W)), full((1, W)), full((1, W))],
        out_shape=[SDS((T, LANE), F32), SDS((LANE, W), F32), SDS((LANE, W), F32), SDS((1, W), F32), SDS((1, W), F32)],
        compiler_params=_params(1),
    )(z, upf, upb, bf, bb, dg)


def _bdot(a, b, ca, cb, precision=None):
    return lax.dot_general(a, b, (((ca,), (cb,)), ((0,), (0,))), preferred_element_type=F32, precision=precision)


def _gla_chunk_terms(q_ref, k_ref, v_ref, g_ref, rev, tok, n, C):
    dk, dv = q_ref.shape[-1], v_ref.shape[-1]
    q = q_ref[0, tok, :].reshape(n, C, dk) * (dk ** -0.5)
    k = k_ref[0, tok, :].reshape(n, C, dk)
    v = v_ref[0, tok, :].reshape(n, C, dv)
    g = g_ref[rev, 0, tok, :].reshape(n, C, dk)
    ii = lax.broadcasted_iota(jnp.int32, (C, C), 0)
    jj = lax.broadcasted_iota(jnp.int32, (C, C), 1)
    tri = jnp.broadcast_to(((ii <= jj) if rev else (ii >= jj)).astype(F32), (n, C, C))
    amask = ((jj > ii) if rev else (ii >= jj))[None]
    b = _bdot(tri, g, 2, 1, HI)
    bl = jnp.sum(g, axis=1, keepdims=True)
    eb = jnp.exp(b)
    enb = jnp.exp(-b)
    eend = jnp.exp(bl - b)
    return q, k, v, tri, amask, bl, eb, enb, eend


def _gla_segments(T, C):
    n = T // C
    ns = _tile(n, 16, 1)

    def tok(s):
        return pl.ds(pl.multiple_of(s * (ns * C), ns * C), ns * C)

    def chunks(s):
        return pl.ds(pl.multiple_of(s * ns, ns), ns)

    return n, ns, n // ns, tok, chunks


def _gla_fwd(q, k, v, g, name):
    H, T, dk = q.shape
    dv = v.shape[-1]
    C = GLA_CHUNK
    n, ns, nseg, seg_tok, seg_chunks = _gla_segments(T, C)

    def body(q_ref, k_ref, v_ref, g_ref, o_ref, upd_scr, dec_scr, st_scr):
        for rev in (0, 1):
            def intra(s, c):
                tok, ch = seg_tok(s), seg_chunks(s)
                q, k, v, tri, amask, bl, eb, enb, eend = _gla_chunk_terms(q_ref, k_ref, v_ref, g_ref, rev, tok, ns, C)
                vb = v.astype(BF16)
                a = jnp.where(amask, _bdot((q * eb).astype(BF16), (k * enb).astype(BF16), 2, 2), 0.0).astype(BF16)
                o = _bdot(a, vb, 2, 1).reshape(ns * C, dv)
                if rev:
                    o_ref[0, tok, :] += o
                else:
                    o_ref[0, tok, :] = o
                upd_scr[ch] = _bdot(vb, (k * eend).astype(BF16), 1, 1)
                dec_scr[ch] = jnp.exp(bl)
                return c

            lax.fori_loop(0, nseg, intra, 0)

            def step(t, st):
                i = n - 1 - t if rev else t
                st_scr[i] = st
                return st * dec_scr[i] + upd_scr[i]

            lax.fori_loop(0, n, step, jnp.zeros((dv, dk), F32))

            def inter(s, c):
                tok, ch = seg_tok(s), seg_chunks(s)
                q, k, v, tri, amask, bl, eb, enb, eend = _gla_chunk_terms(q_ref, k_ref, v_ref, g_ref, rev, tok, ns, C)
                o_ref[0, tok, :] += _bdot((q * eb).astype(BF16), st_scr[ch].astype(BF16), 2, 2).reshape(ns * C, dv)
                return c

            lax.fori_loop(0, nseg, inter, 0)

    sk = pl.BlockSpec((1, T, dk), lambda h: (h, 0, 0))
    sv = pl.BlockSpec((1, T, dv), lambda h: (h, 0, 0))
    sg = pl.BlockSpec((2, 1, T, dk), lambda h: (0, h, 0, 0))
    return pl.pallas_call(
        body, name=name, grid=(H,), in_specs=[sk, sk, sv, sg], out_specs=sv,
        out_shape=SDS((H, T, dv), F32),
        scratch_shapes=[pltpu.VMEM((n, dv, dk), F32), pltpu.VMEM((n, 1, dk), F32), pltpu.VMEM((n, dv, dk), F32)],
        compiler_params=_params(1),
    )(q, k, v, g)


def _gla_bwd(q, k, v, g, do, name):
    H, T, dk = q.shape
    dv = v.shape[-1]
    C = GLA_CHUNK
    n, ns, nseg, seg_tok, seg_chunks = _gla_segments(T, C)

    def one_scan(rev, q_ref, k_ref, v_ref, g_ref, do_ref, dq_ref, dk_ref, dv_ref, dg_ref, upd_scr, dec_scr, st_scr, gn_scr,
                 rn_scr):
        def first(s, c):
            tok, ch = seg_tok(s), seg_chunks(s)
            q, k, v, tri, amask, bl, eb, enb, eend = _gla_chunk_terms(q_ref, k_ref, v_ref, g_ref, rev, tok, ns, C)
            dob = do_ref[0, tok, :].reshape(ns, C, dv).astype(BF16)
            upd_scr[ch] = _bdot(v.astype(BF16), (k * eend).astype(BF16), 1, 1)
            dec_scr[ch] = jnp.exp(bl)
            gn_scr[ch] = _bdot(dob, (q * eb).astype(BF16), 1, 1)
            return c

        lax.fori_loop(0, nseg, first, 0)

        def fstep(t, st):
            i = n - 1 - t if rev else t
            st_scr[i] = st
            return st * dec_scr[i] + upd_scr[i]

        lax.fori_loop(0, n, fstep, jnp.zeros((dv, dk), F32))

        def bstep(t, r):
            i = t if rev else n - 1 - t
            rn_scr[i] = r
            return gn_scr[i] + r * dec_scr[i]

        lax.fori_loop(0, n, bstep, jnp.zeros((dv, dk), F32))

        def second(s, c):
            tok, ch = seg_tok(s), seg_chunks(s)
            q, k, v, tri, amask, bl, eb, enb, eend = _gla_chunk_terms(q_ref, k_ref, v_ref, g_ref, rev, tok, ns, C)
            qe, ke, kend = q * eb, k * enb, k * eend
            qeb, keb, kendb, vb = qe.astype(BF16), ke.astype(BF16), kend.astype(BF16), v.astype(BF16)
            dob = do_ref[0, tok, :].reshape(ns, C, dv).astype(BF16)
            a = jnp.where(amask, _bdot(qeb, keb, 2, 2), 0.0).astype(BF16)
            st = st_scr[ch]
            rn = rn_scr[ch]
            rnb = rn.astype(BF16)
            da = jnp.where(amask, _bdot(dob, vb, 2, 2), 0.0).astype(BF16)
            dvv = _bdot(a, dob, 1, 1) + _bdot(kendb, rnb, 2, 2)
            dqe = _bdot(da, keb, 2, 1) + _bdot(dob, st.astype(BF16), 2, 1)
            dke = _bdot(da, qeb, 1, 1)
            dkend = _bdot(vb, rnb, 2, 1)
            ddec = jnp.sum(rn * st, axis=1, keepdims=True)
            dbl = ddec * jnp.exp(bl) + jnp.sum(dkend * kend, axis=1, keepdims=True)
            db = dqe * qe - dke * ke - dkend * kend
            dg = _bdot(tri, db, 1, 1, HI) + dbl
            dqv = (dqe * eb * (dk ** -0.5)).reshape(ns * C, dk)
            dkv = (dke * enb + dkend * eend).reshape(ns * C, dk)
            if rev:
                dq_ref[0, tok, :] += dqv
                dk_ref[0, tok, :] += dkv
                dv_ref[0, tok, :] += dvv.reshape(ns * C, dv)
            else:
                dq_ref[0, tok, :] = dqv
                dk_ref[0, tok, :] = dkv
                dv_ref[0, tok, :] = dvv.reshape(ns * C, dv)
            dg_ref[rev, 0, tok, :] = dg.reshape(ns * C, dk)
            return c

        lax.fori_loop(0, nseg, second, 0)

    def body(*refs):
        one_scan(0, *refs)
        one_scan(1, *refs)

    sk = pl.BlockSpec((1, T, dk), lambda h: (h, 0, 0), pipeline_mode=pl.Buffered(1))
    sv = pl.BlockSpec((1, T, dv), lambda h: (h, 0, 0), pipeline_mode=pl.Buffered(1))
    sg = pl.BlockSpec((2, 1, T, dk), lambda h: (0, h, 0, 0), pipeline_mode=pl.Buffered(1))
    st_shape = pltpu.VMEM((n, dv, dk), F32)
    return pl.pallas_call(
        body, name=name, grid=(H,), in_specs=[sk, sk, sv, sg, sv], out_specs=[sk, sk, sv, sg],
        out_shape=[SDS((H, T, dk), F32), SDS((H, T, dk), F32), SDS((H, T, dv), F32), SDS((2, H, T, dk), F32)],
        scratch_shapes=[st_shape, pltpu.VMEM((n, 1, dk), F32), st_shape, st_shape, st_shape],
        compiler_params=_params(1),
    )(q, k, v, g, do)


def _rope_tables(T):
    half = MLA_ROPE // 2
    inv = ROPE_THETA ** (-jnp.arange(half, dtype=F32) / half)
    ang = jnp.arange(T, dtype=F32)[:, None] * inv[None, :]
    cos, sin = jnp.cos(ang), jnp.sin(ang)
    ctab = jnp.concatenate([jnp.ones((T, MLA_NOPE), F32), cos, cos], axis=1)
    stab = jnp.concatenate([jnp.zeros((T, MLA_NOPE), F32), -sin, sin], axis=1)
    return ctab, stab


def _swap_rope_halves(x):
    half = MLA_ROPE // 2
    return jnp.concatenate([x[:, :MLA_NOPE], x[:, MLA_NOPE + half:], x[:, MLA_NOPE:MLA_NOPE + half]], axis=1)


def _norm_rope(x, g, c, s):
    xn = x * _rstd(x) * g
    return xn * c + _swap_rope_halves(xn) * s


def _norm_rope_bwd(dy, x, g, c, s):
    return _rms_bwd(dy * c + _swap_rope_halves(dy * s), x, g, _rstd(x))


def _mla_qprep_fwd(q_raw, g, ctab, stab, name):
    T, d = q_raw.shape[0], MLA_QK
    H = q_raw.shape[1] // d
    tm = _tile(T, 1024)

    def body(x_ref, g_ref, c_ref, s_ref, o_ref):
        for h in range(H):
            o_ref[h] = _norm_rope(x_ref[:, h * d:(h + 1) * d], g_ref[...], c_ref[...], s_ref[...]).astype(BF16)

    tab = pl.BlockSpec((tm, d), lambda i: (i, 0))
    return pl.pallas_call(
        body, name=name, grid=(T // tm,),
        in_specs=[pl.BlockSpec((tm, H * d), lambda i: (i, 0)), pl.BlockSpec((1, d), lambda i: (0, 0)), tab, tab],
        out_specs=pl.BlockSpec((H, tm, d), lambda i: (0, i, 0)), out_shape=SDS((H, T, d), BF16),
        compiler_params=_params(1),
    )(q_raw, g, ctab, stab)


def _mla_qprep_bwd(q_raw, g, ctab, stab, dy, name):
    T, d = q_raw.shape[0], MLA_QK
    H = q_raw.shape[1] // d
    tm = _tile(T, 1024)

    def body(x_ref, g_ref, c_ref, s_ref, dy_ref, dx_ref, dg_ref):
        @pl.when(pl.program_id(0) == 0)
        def _():
            dg_ref[...] = jnp.zeros_like(dg_ref)

        dg = jnp.zeros((1, d), F32)
        for h in range(H):
            dx, dgh = _norm_rope_bwd(dy_ref[h], x_ref[:, h * d:(h + 1) * d], g_ref[...], c_ref[...], s_ref[...])
            dx_ref[:, h * d:(h + 1) * d] = dx.astype(BF16)
            dg = dg + dgh
        dg_ref[...] += dg

    tab = pl.BlockSpec((tm, d), lambda i: (i, 0))
    tok = pl.BlockSpec((tm, H * d), lambda i: (i, 0))
    gs = pl.BlockSpec((1, d), lambda i: (0, 0))
    return pl.pallas_call(
        body, name=name, grid=(T // tm,),
        in_specs=[tok, gs, tab, tab, pl.BlockSpec((H, tm, d), lambda i: (0, i, 0))],
        out_specs=[tok, gs], out_shape=[SDS((T, H * d), BF16), SDS((1, d), F32)],
        compiler_params=_params(1),
    )(q_raw, g, ctab, stab, dy)


def _mla_kprep_fwd(kv_raw, z, kr_col, g, ctab, stab, name):
    T, d, hw = kv_raw.shape[0], MLA_QK, MLA_NOPE + MLA_V
    H = kv_raw.shape[1] // hw
    tm = _tile(T, 1024)

    def body(kv_ref, kr_ref, g_ref, c_ref, s_ref, k_ref, v_ref):
        kr = kr_ref[:, :MLA_ROPE]
        for h in range(H):
            x = jnp.concatenate([kv_ref[:, h * hw:h * hw + MLA_NOPE], kr], axis=1)
            k_ref[h] = _norm_rope(x, g_ref[...], c_ref[...], s_ref[...]).astype(BF16)
            v_ref[h] = kv_ref[:, h * hw + MLA_NOPE:(h + 1) * hw].astype(BF16)

    tab = pl.BlockSpec((tm, d), lambda i: (i, 0))
    return pl.pallas_call(
        body, name=name, grid=(T // tm,),
        in_specs=[pl.BlockSpec((tm, H * hw), lambda i: (i, 0)), pl.BlockSpec((tm, LANE), lambda i: (i, kr_col)),
                  pl.BlockSpec((1, d), lambda i: (0, 0)), tab, tab],
        out_specs=[pl.BlockSpec((H, tm, d), lambda i: (0, i, 0)), pl.BlockSpec((H, tm, MLA_V), lambda i: (0, i, 0))],
        out_shape=[SDS((H, T, d), BF16), SDS((H, T, MLA_V), BF16)],
        compiler_params=_params(1),
    )(kv_raw, z, g, ctab, stab)


def _mla_kprep_bwd(kv_raw, z, kr_col, g, ctab, stab, dk, dv, name):
    T, d, hw = kv_raw.shape[0], MLA_QK, MLA_NOPE + MLA_V
    H = kv_raw.shape[1] // hw
    tm = _tile(T, 1024)

    def body(kv_ref, kr_ref, g_ref, c_ref, s_ref, dk_ref, dv_ref, dkv_ref, dkr_ref, dg_ref):
        @pl.when(pl.program_id(0) == 0)
        def _():
            dg_ref[...] = jnp.zeros_like(dg_ref)

        kr = kr_ref[:, :MLA_ROPE]
        dg = jnp.zeros((1, d), F32)
        dkr = jnp.zeros((tm, MLA_ROPE), F32)
        for h in range(H):
            x = jnp.concatenate([kv_ref[:, h * hw:h * hw + MLA_NOPE], kr], axis=1)
            dx, dgh = _norm_rope_bwd(dk_ref[h], x, g_ref[...], c_ref[...], s_ref[...])
            dkv_ref[:, h * hw:h * hw + MLA_NOPE] = dx[:, :MLA_NOPE].astype(BF16)
            dkv_ref[:, h * hw + MLA_NOPE:(h + 1) * hw] = dv_ref[h].astype(BF16)
            dkr = dkr + dx[:, MLA_NOPE:]
            dg = dg + dgh
        dkr_ref[...] = jnp.concatenate([dkr, jnp.zeros((tm, LANE - MLA_ROPE), F32)], axis=1)
        dg_ref[...] += dg

    tab = pl.BlockSpec((tm, d), lambda i: (i, 0))
    tok = pl.BlockSpec((tm, H * hw), lambda i: (i, 0))
    gs = pl.BlockSpec((1, d), lambda i: (0, 0))
    return pl.pallas_call(
        body, name=name, grid=(T // tm,),
        in_specs=[tok, pl.BlockSpec((tm, LANE), lambda i: (i, kr_col)), gs, tab, tab,
                  pl.BlockSpec((H, tm, d), lambda i: (0, i, 0)), pl.BlockSpec((H, tm, MLA_V), lambda i: (0, i, 0))],
        out_specs=[tok, pl.BlockSpec((tm, LANE), lambda i: (i, 0)), gs],
        out_shape=[SDS((T, H * hw), BF16), SDS((T, LANE), F32), SDS((1, d), F32)],
        compiler_params=_params(1),
    )(kv_raw, z, g, ctab, stab, dk, dv)


def _mla_attn_fwd(q, k, v, name, exch=None):
    H, T, d = q.shape
    dv = v.shape[-1]
    tq = _tile(T, 256)
    scale = d ** -0.5

    def body(q_ref, k_ref, v_ref, o_ref, lse_ref):
        s = _dot(q_ref[0], k_ref[0], NT) * scale
        m = jnp.max(s, axis=-1, keepdims=True)
        p = jnp.exp(s - m)
        l = jnp.sum(p, axis=-1, keepdims=True)
        o_ref[0] = _dot(p.astype(BF16), v_ref[0]) / l
        lse_ref[0] = _col_to_row(m + jnp.log(l))

    return _call(
        body, name=name, grid=(H, T // tq),
        in_specs=[pl.BlockSpec((1, tq, d), lambda h, i: (h, i, 0)), pl.BlockSpec((1, T, d), lambda h, i: (h, 0, 0)),
                  pl.BlockSpec((1, T, dv), lambda h, i: (h, 0, 0))],
        out_specs=[pl.BlockSpec((1, tq, dv), lambda h, i: (h, i, 0)), pl.BlockSpec((1, 1, tq), lambda h, i: (h, 0, i))],
        out_shape=[SDS((H, T, dv), F32), SDS((H, 1, T), F32)],
        args=(q, k, v), exch=exch)


def _col_to_row(col):
    m = col.shape[0]
    eye = lax.broadcasted_iota(jnp.int32, (m, m), 0) == lax.broadcasted_iota(jnp.int32, (m, m), 1)
    return jnp.sum(jnp.where(eye, col, 0.0), axis=0, keepdims=True)


def _mla_attn_bwd(q, k, v, o, lse, do, name, exch=None):
    H, T, d = q.shape
    dv = v.shape[-1]
    tq = _tile(T, 256)
    scale = d ** -0.5

    def body(q_ref, k_ref, v_ref, o_ref, lse_ref, do_ref, dq_ref, dk_ref, dv_ref):
        @pl.when(pl.program_id(1) == 0)
        def _():
            dk_ref[...] = jnp.zeros_like(dk_ref)
            dv_ref[...] = jnp.zeros_like(dv_ref)

        qv, kv = q_ref[0], k_ref[0]
        dov = do_ref[0]
        dob = dov.astype(BF16)
        delta_row = _col_to_row(jnp.sum(dov * o_ref[0], axis=-1, keepdims=True))
        pt = jnp.exp(_dot(kv, qv, NT) * scale - lse_ref[0])
        dst = (pt * (_dot(v_ref[0], dob, NT) - delta_row) * scale).astype(BF16)
        dv_ref[0] += _dot(pt.astype(BF16), dob)
        dk_ref[0] += _dot(dst, qv)
        dq_ref[0] = _dot(dst, kv, TN)

    qb = pl.BlockSpec((1, tq, d), lambda h, i: (h, i, 0))
    kb = pl.BlockSpec((1, T, d), lambda h, i: (h, 0, 0))
    vb = pl.BlockSpec((1, T, dv), lambda h, i: (h, 0, 0))
    ob = pl.BlockSpec((1, tq, dv), lambda h, i: (h, i, 0))
    return _call(
        body, name=name, grid=(H, T // tq),
        in_specs=[qb, kb, vb, ob, pl.BlockSpec((1, 1, tq), lambda h, i: (h, 0, i)), ob],
        out_specs=[qb, kb, vb],
        out_shape=[SDS((H, T, d), F32), SDS((H, T, d), F32), SDS((H, T, dv), F32)],
        args=(q, k, v, o, lse, do), exch=exch)


def _silu_parts(x):
    s = _sigmoid(x)
    return x * s, s * (1.0 + x * (1.0 - s))


def _mix_out_fwd(x1, y_na, o_gla, y_mla, z, gr_col, gnorm, wbr, w_out, name):
    T, D = x1.shape
    W = y_na.shape[0] * y_na.shape[2]
    tm = _tile(T, 512)

    def body(x_ref, na_ref, gla_ref, mla_ref, gt_ref, gr_ref, gn_ref, w0, w1, w2, wo_ref, o_ref):
        gn = gn_ref[...]
        nrm = jnp.concatenate([gla_ref[h] * _rstd(gla_ref[h]) * gn for h in range(GLA_HEADS)], axis=1)
        y_gla = nrm * _silu_parts(gr_ref[...])[0]
        y_na = jnp.concatenate([na_ref[h] for h in range(NA_HEADS)], axis=1)
        y_mla = jnp.concatenate([mla_ref[h] for h in range(MLA_HEADS)], axis=1)
        mixed = jnp.zeros((tm, D), F32)
        for i, (y, w_ref) in enumerate(((y_na, w0), (y_gla, w1), (y_mla, w2))):
            mixed = mixed + _sigmoid(gt_ref[:, i * D:(i + 1) * D]) * _dot(y.astype(BF16), w_ref[...])
        o_ref[...] = x_ref[...] + _dot(mixed.astype(BF16), wo_ref[...])

    tok = lambda w: pl.BlockSpec((tm, w), lambda i: (i, 0))
    hm = lambda a: pl.BlockSpec((a.shape[0], tm, a.shape[2]), lambda i: (0, i, 0))
    full = lambda shape: pl.BlockSpec(shape, lambda i: (0, 0))
    return pl.pallas_call(
        body, name=name, grid=(T // tm,),
        in_specs=[tok(D), hm(y_na), hm(o_gla), hm(y_mla), tok(3 * D), pl.BlockSpec((tm, W), lambda i: (i, gr_col)),
                  full((1, GLA_DV)), full((W, D)), full((W, D)), full((W, D)), full((D, D))],
        out_specs=tok(D), out_shape=SDS((T, D), F32),
        compiler_params=_params(1),
    )(x1, y_na, o_gla, y_mla, z, z, gnorm, wbr[0], wbr[1], wbr[2], w_out)


def _mix_out_bwd(dx2, y_na, o_gla, y_mla, z, gr_col, gnorm, wbr, w_out, name):
    T, D = dx2.shape
    W = y_na.shape[0] * y_na.shape[2]
    tm = _tile(T, 256)

    def body(dx_ref, na_ref, gla_ref, mla_ref, gt_ref, gr_ref, gn_ref, w0, w1, w2, wo_ref,
             dna_ref, dgla_ref, dmla_ref, dgr_ref, dgt_ref, dw0, dw1, dw2, dwo_ref, dgn_ref):
        @pl.when(pl.program_id(0) == 0)
        def _():
            for r in (dw0, dw1, dw2, dwo_ref, dgn_ref):
                r[...] = jnp.zeros_like(r)

        gn = gn_ref[...]
        grv = gr_ref[...]
        sil, dsil = _silu_parts(grv)
        rs = [_rstd(gla_ref[h]) for h in range(GLA_HEADS)]
        nrm = jnp.concatenate([gla_ref[h] * rs[h] * gn for h in range(GLA_HEADS)], axis=1)
        ys = (jnp.concatenate([na_ref[h] for h in range(NA_HEADS)], axis=1).astype(BF16), (nrm * sil).astype(BF16),
              jnp.concatenate([mla_ref[h] for h in range(MLA_HEADS)], axis=1).astype(BF16))
        dxb = dx_ref[...].astype(BF16)
        dmixed = _dot(dxb, wo_ref[...], NT)
        mixed = jnp.zeros((tm, D), F32)
        dys = []
        for i, (y, w_ref, dw_ref) in enumerate(zip(ys, (w0, w1, w2), (dw0, dw1, dw2))):
            gt = _sigmoid(gt_ref[:, i * D:(i + 1) * D])
            p = _dot(y, w_ref[...])
            mixed = mixed + gt * p
            dp = (dmixed * gt).astype(BF16)
            dgt_ref[:, i * D:(i + 1) * D] = (dmixed * p * gt * (1.0 - gt)).astype(BF16)
            dys.append(_dot(dp, w_ref[...], NT))
            dw_ref[...] += _dot(y, dp, TN)
        dwo_ref[...] += _dot(mixed.astype(BF16), dxb, TN)
        for h in range(NA_HEADS):
            dna_ref[h] = dys[0][:, h * NA_HD:(h + 1) * NA_HD]
        for h in range(MLA_HEADS):
            dmla_ref[h] = dys[2][:, h * MLA_V:(h + 1) * MLA_V]
        dgr_ref[...] = dys[1] * nrm * dsil
        dn = dys[1] * sil
        dgn = jnp.zeros((1, GLA_DV), F32)
        for h in range(GLA_HEADS):
            dxh, dgh = _rms_bwd(dn[:, h * GLA_DV:(h + 1) * GLA_DV], gla_ref[h], gn, rs[h])
            dgla_ref[h] = dxh
            dgn = dgn + dgh
        dgn_ref[...] += dgn

    tok = lambda w: pl.BlockSpec((tm, w), lambda i: (i, 0))
    hm = lambda a: pl.BlockSpec((a.shape[0], tm, a.shape[2]), lambda i: (0, i, 0))
    full = lambda shape: pl.BlockSpec(shape, lambda i: (0, 0))
    return pl.pallas_call(
        body, name=name, grid=(T // tm,),
        in_specs=[tok(D), hm(y_na), hm(o_gla), hm(y_mla), tok(3 * D), pl.BlockSpec((tm, W), lambda i: (i, gr_col)),
                  full((1, GLA_DV)), full((W, D)), full((W, D)), full((W, D)), full((D, D))],
        out_specs=[hm(y_na), hm(o_gla), hm(y_mla), tok(W), tok(3 * D), full((W, D)), full((W, D)), full((W, D)),
                   full((D, D)), full((1, GLA_DV))],
        out_shape=[SDS(y_na.shape, F32), SDS(o_gla.shape, F32), SDS(y_mla.shape, F32), SDS((T, W), F32),
                   SDS((T, 3 * D), BF16)] + [SDS((W, D), F32)] * 3 + [SDS((D, D), F32), SDS((1, GLA_DV), F32)],
        compiler_params=_params(1),
    )(dx2, y_na, o_gla, y_mla, z, z, gnorm, wbr[0], wbr[1], wbr[2], w_out)


def _split_z(z, lay, name):
    T = z.shape[0]
    tm = _tile(T, 512)
    na0, gv0, gq0 = lay['na_q'][0], lay['gv'][0], lay['gq'][0]
    assert na0 % (3 * NA_W) == 0 and gv0 % GLA_V_W == 0 and gq0 % (2 * GLA_QK_W) == 0
    assert lay['na_k'][0] == na0 + NA_W and lay['na_v'][0] == na0 + 2 * NA_W and lay['gk'][0] == gq0 + GLA_QK_W

    def body(na_ref, gv_ref, gqk_ref, q_ref, k_ref, v_ref, gq_ref, gk_ref, gvo_ref):
        for j, o_ref in enumerate((q_ref, k_ref, v_ref)):
            for h in range(NA_HEADS):
                o_ref[h] = na_ref[:, j * NA_W + h * NA_HD:j * NA_W + (h + 1) * NA_HD]
        for j, o_ref in enumerate((gq_ref, gk_ref)):
            for h in range(GLA_HEADS):
                o_ref[h] = gqk_ref[:, j * GLA_QK_W + h * GLA_DK:j * GLA_QK_W + (h + 1) * GLA_DK]
        for h in range(GLA_HEADS):
            gvo_ref[h] = gv_ref[:, h * GLA_DV:(h + 1) * GLA_DV]

    hm = lambda H, d: pl.BlockSpec((H, tm, d), lambda i: (0, i, 0))
    return pl.pallas_call(
        body, name=name, grid=(T // tm,),
        in_specs=[pl.BlockSpec((tm, 3 * NA_W), lambda i: (i, na0 // (3 * NA_W))),
                  pl.BlockSpec((tm, GLA_V_W), lambda i: (i, gv0 // GLA_V_W)),
                  pl.BlockSpec((tm, 2 * GLA_QK_W), lambda i: (i, gq0 // (2 * GLA_QK_W)))],
        out_specs=[hm(NA_HEADS, NA_HD)] * 3 + [hm(GLA_HEADS, GLA_DK)] * 2 + [hm(GLA_HEADS, GLA_DV)],
        out_shape=[SDS((NA_HEADS, T, NA_HD), F32)] * 3 + [SDS((GLA_HEADS, T, GLA_DK), F32)] * 2
        + [SDS((GLA_HEADS, T, GLA_DV), F32)],
        compiler_params=_params(1),
    )(z, z, z)


def _assemble_dz(dgates, na_dqkv, gla_dqk, gla_dv, dgr, dcq, dckv, dglow, dkr, lay, zp, name):
    T = dgr.shape[0]
    tm = _tile(T, 256)

    def body(dgt_ref, nq_ref, nk_ref, nv_ref, gq_ref, gk_ref, gv_ref, dgr_ref, dcq_ref, dckv_ref, dglow_ref, dkr_ref, dz_ref):
        def put(seg, off, val):
            a = lay[seg][0] + off
            dz_ref[:, a:a + val.shape[1]] = val.astype(BF16)

        put('gates', 0, dgt_ref[...])
        for seg, ref in (('na_q', nq_ref), ('na_k', nk_ref), ('na_v', nv_ref)):
            for h in range(NA_HEADS):
                put(seg, h * NA_HD, ref[h])
        for seg, ref in (('gq', gq_ref), ('gk', gk_ref)):
            for h in range(GLA_HEADS):
                put(seg, h * GLA_DK, ref[h])
        for h in range(GLA_HEADS):
            put('gv', h * GLA_DV, gv_ref[h])
        put('gr', 0, dgr_ref[...])
        put('c_q', 0, dcq_ref[...])
        put('c_kv', 0, dckv_ref[...])
        put('glow', 0, dglow_ref[...])
        put('k_rope', 0, dkr_ref[...])

    tok = lambda a: pl.BlockSpec((tm, a.shape[1]), lambda i: (i, 0))
    hm = lambda a: pl.BlockSpec((a.shape[0], tm, a.shape[2]), lambda i: (0, i, 0))
    args = (dgates, *na_dqkv, *gla_dqk, gla_dv, dgr, dcq, dckv, dglow, dkr)
    return pl.pallas_call(
        body, name=name, grid=(T // tm,),
        in_specs=[tok(dgates)] + [hm(a) for a in (*na_dqkv, *gla_dqk, gla_dv)] + [tok(dgr), tok(dcq), tok(dckv), tok(dglow),
                                                                                 tok(dkr)],
        out_specs=pl.BlockSpec((tm, zp), lambda i: (i, 0)), out_shape=SDS((T, zp), BF16),
        compiler_params=_params(1),
    )(*args)


def _loss_head(y, target, name):
    T, D = y.shape
    tm = _tile(T, 1024)

    def body(y_ref, t_ref, dy_ref, l_ref):
        @pl.when(pl.program_id(0) == 0)
        def _():
            l_ref[...] = jnp.zeros_like(l_ref)

        e = y_ref[...] - t_ref[...]
        dy_ref[...] = e * (1.0 / D)
        l_ref[...] += 0.5 * jnp.sum(jnp.mean(e * e, axis=-1, keepdims=True), axis=0, keepdims=True)

    tok = pl.BlockSpec((tm, D), lambda i: (i, 0))
    return pl.pallas_call(
        body, name=name, grid=(T // tm,), in_specs=[tok, tok],
        out_specs=[tok, pl.BlockSpec((1, 1), lambda i: (0, 0))],
        out_shape=[SDS((T, D), F32), SDS((1, 1), F32)],
        compiler_params=_params(1),
    )(y, target)


def _exchange_copies(src, dst, send_sems, recv_sems, local_sems, broadcast, with_recvs=True):
    n = len(src)
    x, y, c = lax.axis_index("x"), lax.axis_index("y"), lax.axis_index("c")
    me = 4 * x + 2 * y + c
    local = [pltpu.make_async_copy(src[a] if broadcast else src[a].at[me], dst[a].at[me], local_sems.at[a])
             for a in range(n)]
    sends, recvs = [], []
    for d in range(1, N_DEV):
        px = 1 - x if d & 4 else x
        py = 1 - y if d & 2 else y
        pc = 1 - c if d & 1 else c
        peer = 4 * px + 2 * py + pc
        for a in range(n):
            for out, slot in ((sends, me), (recvs, peer)) if with_recvs else ((sends, me),):
                out.append(pltpu.make_async_remote_copy(
                    src_ref=src[a] if broadcast else src[a].at[peer], dst_ref=dst[a].at[slot],
                    send_sem=send_sems.at[a, d - 1], recv_sem=recv_sems.at[a, d - 1],
                    device_id=(px, py, pc), device_id_type=pl.DeviceIdType.MESH))
    return local, sends, recvs


def _gather_copies(src, dst, send_sems, recv_sems, local_sems, with_local=True):
    x, y, c = lax.axis_index("x"), lax.axis_index("y"), lax.axis_index("c")
    chips = [(1 - x, y), (x, 1 - y), (1 - x, 1 - y)]
    lin = lambda px, py, pc: 4 * px + 2 * py + pc

    def copy(a, k, incoming=False):
        if k == 0:
            to, out_src, out_slot, in_slot = (x, y, 1 - c), src[a], lin(x, y, c), lin(x, y, 1 - c)
        elif k <= 3:
            to, out_src, out_slot, in_slot = (*chips[k - 1], c), src[a], lin(x, y, c), lin(*chips[k - 1], c)
        else:
            slot = lin(*chips[k - 4], c)
            to, out_src, out_slot, in_slot = (x, y, 1 - c), dst[a].at[slot], slot, lin(*chips[k - 4], 1 - c)
        return pltpu.make_async_remote_copy(
            src_ref=out_src, dst_ref=dst[a].at[in_slot if incoming else out_slot], send_sem=send_sems.at[a, k],
            recv_sem=recv_sems.at[a, k], device_id=to, device_id_type=pl.DeviceIdType.MESH)

    local = [pltpu.make_async_copy(src[a], dst[a].at[lin(x, y, c)], local_sems.at[a])
             for a in range(len(src) if with_local else 0)]
    return copy, local


def _exchange_start(src, dst, send_sems, recv_sems, local_sems, broadcast):
    if broadcast:
        copy, local = _gather_copies(src, dst, send_sems, recv_sems, local_sems)
        for cp in local:
            cp.start()
        for k in range(4):
            for a in range(len(src)):
                copy(a, k).start()
        return
    local, sends, _ = _exchange_copies(src, dst, send_sems, recv_sems, local_sems, broadcast, with_recvs=False)
    for cp in local + sends:
        cp.start()


def _exchange_pass_on(src, dst, send_sems, recv_sems, local_sems, broadcast):
    if broadcast:
        copy, _ = _gather_copies(src, dst, send_sems, recv_sems, local_sems, with_local=False)
        for k in range(1, 4):
            for a in range(len(src)):
                copy(a, k, True).wait_recv()
                copy(a, k + 3).start()


def _exchange_wait(src, dst, send_sems, recv_sems, local_sems, broadcast):
    if broadcast:
        copy, local = _gather_copies(src, dst, send_sems, recv_sems, local_sems)
        n = len(src)
        for k in (0, 4, 5, 6):
            for a in range(n):
                copy(a, k, True).wait_recv()
        for k in range(N_DEV - 1):
            for a in range(n):
                copy(a, k).wait_send()
        for cp in local:
            cp.wait()
        return
    local, sends, recvs = _exchange_copies(src, dst, send_sems, recv_sems, local_sems, broadcast)
    for cp in recvs:
        cp.wait_recv()
    for cp in sends:
        cp.wait_send()
    for cp in local:
        cp.wait()


def _exchange_shapes(srcs, broadcast):
    n = len(srcs)
    out_shape = [SDS((N_DEV,) + (tuple(s.shape) if broadcast else tuple(s.shape[1:])), s.dtype) for s in srcs]
    sems = [pltpu.SemaphoreType.DMA((n, N_DEV - 1)), pltpu.SemaphoreType.DMA((n, N_DEV - 1)),
            pltpu.SemaphoreType.DMA((n,))]
    return out_shape, sems


def _exchange(srcs, broadcast, name):
    n = len(srcs)
    out_shape, sems = _exchange_shapes(srcs, broadcast)

    def body(*refs):
        mode = (refs[:n], refs[n:2 * n]) + tuple(refs[2 * n:]) + (broadcast,)
        _exchange_start(*mode)
        _exchange_pass_on(*mode)
        _exchange_wait(*mode)

    hbm = pl.BlockSpec(memory_space=pltpu.HBM)
    return pl.pallas_call(body, name=name, in_specs=[hbm] * n, out_specs=[hbm] * n, out_shape=out_shape,
                          scratch_shapes=sems)(*srcs)


def _call(body, *, name, grid, in_specs, out_specs, out_shape, args, scratch_shapes=(), exch=None):
    params = _params(len(grid))
    if exch is None:
        return pl.pallas_call(body, name=name, grid=grid, in_specs=in_specs, out_specs=out_specs, out_shape=out_shape,
                              scratch_shapes=list(scratch_shapes), compiler_params=params)(*args)
    srcs, broadcast = exch
    n, n_in, n_out, n_scr = len(srcs), len(args), len(out_shape), len(scratch_shapes)
    x_shape, sems = _exchange_shapes(srcs, broadcast)

    def carrier(*refs):
        ins, x_src = refs[:n_in], refs[n_in:n_in + n]
        outs, x_dst = refs[n_in + n:n_in + n + n_out], refs[n_in + n + n_out:n_in + 2 * n + n_out]
        scr = refs[n_in + 2 * n + n_out:n_in + 2 * n + n_out + n_scr]
        mode = (x_src, x_dst) + tuple(refs[n_in + 2 * n + n_out + n_scr:]) + (broadcast,)
        step = 0
        for a, g in enumerate(grid):
            step = step * g + pl.program_id(a)
        steps = int(np.prod(grid))
        pl.when(step == 0)(lambda: _exchange_start(*mode))
        body(*ins, *outs, *scr)
        pl.when(step == (3 * steps) // 4)(lambda: _exchange_pass_on(*mode))
        pl.when(step == steps - 1)(lambda: _exchange_wait(*mode))

    hbm = pl.BlockSpec(memory_space=pltpu.HBM)
    res = pl.pallas_call(carrier, name=name, grid=grid, in_specs=list(in_specs) + [hbm] * n,
                         out_specs=list(out_specs) + [hbm] * n, out_shape=list(out_shape) + x_shape,
                         scratch_shapes=list(scratch_shapes) + sems, compiler_params=params)(*args, *srcs)
    return res[:n_out], res[n_out:]


def _adam_math(w, g, m, v):
    m = ADAM_B1 * m + (1.0 - ADAM_B1) * g
    v = ADAM_B2 * v + (1.0 - ADAM_B2) * (g * g)
    m_hat = m / (1.0 - ADAM_B1 ** ADAM_STEP)
    v_hat = v / (1.0 - ADAM_B2 ** ADAM_STEP)
    delta = -ADAM_LR * (m_hat / (jnp.sqrt(v_hat) + ADAM_EPS) + ADAM_WD * w)
    return delta, m, v


def _adamw_sharded(landed, k, w, m, v, name):
    L, r, c = w.shape
    lanes = -(-c // LANE) * LANE
    rb = _tile(r, max(16, (1 << 22) // (N_DEV * lanes * 2)), 16)

    def body(*refs):
        l_refs = refs[:L]
        w_ref, m_ref, v_ref, g_ref, d_ref, nm_ref, nv_ref = refs[L:]
        for j in range(L):
            @pl.when(pl.program_id(0) == j)
            def _(j=j):
                g = l_refs[j][0, 0].astype(F32)
                for s in range(1, N_DEV):
                    g = g + l_refs[j][s, 0].astype(F32)
                delta, nm, nv = _adam_math(w_ref[0], g, m_ref[0], v_ref[0])
                g_ref[0] = g
                d_ref[0] = delta
                nm_ref[0] = nm
                nv_ref[0] = nv

    blk = pl.BlockSpec((1, rb, c), lambda l, i: (l, i, 0))
    land = [pl.BlockSpec((N_DEV, 1, rb, c), lambda l, i, j=j: (0, k, jnp.where(l == j, i, 0), 0)) for j in range(L)]
    return pl.pallas_call(
        body, name=name, grid=(L, r // rb),
        in_specs=land + [blk, blk, blk],
        out_specs=[blk] * 4, out_shape=[SDS((L, r, c), F32)] * 4,
        compiler_params=_params(2),
    )(*landed, w, m, v)


def _adamw_replicated(landed, w, m, v, name):
    R = w.shape[0]

    def body(l_ref, w_ref, m_ref, v_ref, g_ref, d_ref, nm_ref, nv_ref):
        g = l_ref[0]
        for s in range(1, N_DEV):
            g = g + l_ref[s]
        delta, nm, nv = _adam_math(w_ref[...], g, m_ref[...], v_ref[...])
        g_ref[...] = g
        d_ref[...] = delta
        nm_ref[...] = nm
        nv_ref[...] = nv

    blk = pl.BlockSpec((R, LANE), lambda i: (0, 0))
    return pl.pallas_call(
        body, name=name, grid=(1,),
        in_specs=[pl.BlockSpec((N_DEV, R, LANE), lambda i: (0, 0, 0)), blk, blk, blk],
        out_specs=[blk] * 4, out_shape=[SDS((R, LANE), F32)] * 4,
        compiler_params=_params(1),
    )(landed, w, m, v)


def _pack(parts):
    flat = jnp.concatenate([p.reshape(-1) for p in parts])
    rows = -(-flat.shape[0] // (8 * LANE)) * 8
    return jnp.pad(flat, (0, rows * LANE - flat.shape[0])).reshape(rows, LANE)


def _unpack(packed, shapes):
    flat = packed.reshape(-1)
    out, off = [], 0
    for s in shapes:
        size = int(np.prod(s))
        out.append(flat[off:off + size].reshape(s))
        off += size
    return out


def _layer_fwd(x0, w, l, lay, tabs, carry):
    T, D = x0.shape
    ctab, stab, na_bias = tabs
    col = lambda name: lay[name][0] // lay[name][1]
    sv = {'x0': x0}
    x1, sv['a1'], sv['b1'] = _carried(_ffn_fwd, carry, 'ffn1', x0, w['ffn1_norm'], w['ffn1_w1'], w['ffn1_w3'],
                                      w['ffn1_w2'], f"ffn1_fwd_{l}")
    z, sv['hb'] = _carried(_norm_matmul, carry, 'mix_in', x1, 0, w['mix_norm'], w['w_in_z'], f"mix_in_{l}")
    sv['na_q'], sv['na_k'], sv['na_v'], sv['gq'], sv['gk'], sv['gv'] = _split_z(z, lay, f"split_z_{l}")
    y_na = _carried(_na_fwd, carry, 'na', sv['na_q'], sv['na_k'], sv['na_v'], na_bias, l, w['na_q_norm'], w['na_k_norm'],
                    f"na_fwd_{l}")[0]
    sv['gg'] = _gla_gate_fwd(z, col('glow'), w['up_f'], w['up_b'], w['gla_gf_bias'], w['gla_gb_bias'], f"gla_gate_{l}")
    o_gla = _gla_fwd(sv['gq'], sv['gk'], sv['gv'], sv['gg'], f"gla_fwd_{l}")
    sv['q_raw'], sv['hq'] = _norm_matmul(z, col('c_q'), w['mla_cq_norm'], w['mla_w_uq'], f"mla_uq_{l}")
    sv['kv_raw'], sv['hkv'] = _norm_matmul(z, col('c_kv'), w['mla_ckv_norm'], w['mla_w_ukv'], f"mla_ukv_{l}")
    sv['mq'] = _mla_qprep_fwd(sv['q_raw'], w['mla_q_norm'], ctab, stab, f"mla_qprep_{l}")
    sv['mk'], sv['mv'] = _mla_kprep_fwd(sv['kv_raw'], z, col('k_rope'), w['mla_k_norm'], ctab, stab, f"mla_kprep_{l}")
    y_mla, sv['lse'] = _carried(_mla_attn_fwd, carry, 'mla', sv['mq'], sv['mk'], sv['mv'], f"mla_attn_{l}")
    w_br = (w['w_br_na'], w['w_br_gla'], w['w_br_mla'])
    x2 = _mix_out_fwd(x1, y_na, o_gla, y_mla, z, col('gr'), w['gla_out_norm'], w_br, w['w_out'], f"mix_out_{l}")
    sv.update(x1=x1, z=z, y_na=y_na, o_gla=o_gla, y_mla=y_mla, x2=x2)
    x3, sv['a2'], sv['b2'] = _carried(_ffn_fwd, carry, 'ffn2', x2, w['ffn2_norm'], w['ffn2_w1'], w['ffn2_w3'],
                                      w['ffn2_w2'], f"ffn2_fwd_{l}")
    return x3, sv


def _ffn_grads(grads, x, dy, a, b, w, pre, l, carry):
    dx, da, db, u, hb, dyb, dg = _carried(_ffn_bwd, carry, pre, x, w[pre + '_norm'], dy, a, b, w[pre + '_w1'],
                                          w[pre + '_w3'], w[pre + '_w2'], f"{pre}_bwd_{l}")
    grads[pre + '_norm'] = dg
    grads[pre + '_w1'] = _matmul_tn(hb, da, f"{pre}_dw1_{l}")
    grads[pre + '_w3'] = _carried(_matmul_tn, carry, pre + '_dw3', hb, db, f"{pre}_dw3_{l}")
    grads[pre + '_w2'] = _carried(_matmul_tn, carry, pre + '_dw2', u, dyb, f"{pre}_dw2_{l}")
    return dx


def _layer_bwd(grads, dx3, sv, w, l, lay, tabs, carry):
    T, D = dx3.shape
    ctab, stab, na_bias, onehot = tabs
    col = lambda name: lay[name][0] // lay[name][1]
    z = sv['z']
    dx2 = _ffn_grads(grads, sv['x2'], dx3, sv['a2'], sv['b2'], w, 'ffn2', l, carry)
    w_br = (w['w_br_na'], w['w_br_gla'], w['w_br_mla'])
    (dy_na, do_gla, dy_mla, dgr, dgates, dw0, dw1, dw2, dwo, dgn) = _mix_out_bwd(
        dx2, sv['y_na'], sv['o_gla'], sv['y_mla'], z, col('gr'), w['gla_out_norm'], w_br, w['w_out'], f"mix_out_bwd_{l}")
    grads.update(w_br_na=dw0, w_br_gla=dw1, w_br_mla=dw2, w_out=dwo, gla_out_norm=dgn)
    dq, dk, dv, dbias, dgq, dgk = _carried(_na_bwd, carry, 'na', sv['na_q'], sv['na_k'], sv['na_v'], dy_na, na_bias, l,
                                          w['na_q_norm'], w['na_k_norm'], f"na_bwd_{l}")
    grads.update(na_q_norm=dgq, na_k_norm=dgk, na_rpb=_na_bias_tables_bwd(dbias, onehot))
    gdq, gdk, gdv, gdg = _gla_bwd(sv['gq'], sv['gk'], sv['gv'], sv['gg'], do_gla, f"gla_bwd_{l}")
    dglow, dupf, dupb, dbf, dbb = _gla_gate_bwd(z, col('glow'), w['up_f'], w['up_b'], w['gla_gf_bias'], w['gla_gb_bias'],
                                                gdg, f"gla_gate_bwd_{l}")
    grads.update(gla_gf_up=dupf[:GLA_RANK], gla_gb_up=dupb[GLA_RANK:2 * GLA_RANK], gla_gf_bias=dbf, gla_gb_bias=dbb)
    mdq, mdk, mdv = _carried(_mla_attn_bwd, carry, 'mla', sv['mq'], sv['mk'], sv['mv'], sv['y_mla'], sv['lse'], dy_mla,
                             f"mla_attn_bwd_{l}")
    dq_raw, dgqn = _mla_qprep_bwd(sv['q_raw'], w['mla_q_norm'], ctab, stab, mdq, f"mla_qprep_bwd_{l}")
    dkv_raw, dkr, dgkn = _mla_kprep_bwd(sv['kv_raw'], z, col('k_rope'), w['mla_k_norm'], ctab, stab, mdk, mdv,
                                        f"mla_kprep_bwd_{l}")
    dcq, dgcq = _norm_matmul_bwd(dq_raw, w['mla_w_uq'], z, col('c_q'), w['mla_cq_norm'], None, f"mla_uq_bwd_{l}")
    dckv, dgckv = _norm_matmul_bwd(dkv_raw, w['mla_w_ukv'], z, col('c_kv'), w['mla_ckv_norm'], None, f"mla_ukv_bwd_{l}")
    grads.update(mla_q_norm=dgqn, mla_k_norm=dgkn, mla_cq_norm=dgcq, mla_ckv_norm=dgckv,
                 mla_w_uq=_matmul_tn(sv['hq'], dq_raw, f"mla_duq_{l}", F32),
                 mla_w_ukv=_matmul_tn(sv['hkv'], dkv_raw, f"mla_dukv_{l}", F32))
    dz = _assemble_dz(dgates, (dq, dk, dv), (gdq, gdk), gdv, dgr, dcq, dckv, dglow, dkr, lay, z.shape[1], f"dz_{l}")
    dx1, dgmix = _carried(_norm_matmul_bwd, carry, 'mix_in', dz, w['w_in_z'], sv['x1'], 0, w['mix_norm'], dx2,
                          f"mix_in_bwd_{l}")
    grads.update(mix_norm=dgmix, w_in=_matmul_tn(sv['hb'], dz, f"mix_dw_in_{l}"))
    return _ffn_grads(grads, sv['x0'], dx1, sv['a1'], sv['b1'], w, 'ffn1', l, carry)


EXCHANGE_PARTS = {
    'F1w1': [['ffn1_w1']],
    'F1w3': [['ffn1_w3']],
    'F1b': [['ffn1_w2']],
    'F2': [['ffn2_w1', 'ffn2_w3'], ['ffn2_w2']],
    'C': [['w_in']],
    'S': [['w_br_na', 'w_br_gla', 'w_br_mla'], ['w_out'], ['mla_w_uq'], ['mla_w_ukv'], ['gla_gf_up', 'gla_gb_up']],
}
FWD_PLAN = {'ffn1': [(0, 'C')], 'mix_in': [(0, 'S')], 'mla': [(0, 'F2')], 'na': [(1, 'F1b')],
            'ffn2': [(1, 'F1w1'), (1, 'F1w3')]}
BWD_PLAN = {'mla': [(0, 'F2')], 'mix_in': [(0, 'S')], 'ffn1': [(0, 'C')], 'ffn2': [(1, 'F1w1'), (1, 'F1w3')],
            'na': [(1, 'F1b')]}
BWD_PLAN_LAYER0 = {'ffn1_dw3': [(0, 'F1w1')], 'ffn1_dw2': [(0, 'F1w3')]}
FWD_EDGE = ['F1w1', 'F1w3', 'F1b']
BWD_EDGE = ['F1b']


def _carried(fn, carry, key, *args):
    if key in carry:
        make_srcs, broadcast, deliver = carry[key]
        outs, landed = fn(*args, exch=(make_srcs(), broadcast))
        deliver(landed)
        return outs
    return fn(*args)


def kernel(x, ffn1_norm, ffn1_w1, ffn1_w3, ffn1_w2, mix_norm, w_in, na_q_norm, na_k_norm, na_rpb, gla_gf_up, gla_gf_bias, gla_gb_up, gla_gb_bias, gla_out_norm, mla_cq_norm, mla_ckv_norm, mla_w_uq, mla_w_ukv, mla_q_norm, mla_k_norm, w_br_na, w_br_gla, w_br_mla, w_out, ffn2_norm, ffn2_w1, ffn2_w3, ffn2_w2, loss_target, m_ffn1_norm, m_ffn1_w1, m_ffn1_w3, m_ffn1_w2, m_mix_norm, m_w_in, m_na_q_norm, m_na_k_norm, m_na_rpb, m_gla_gf_up, m_gla_gf_bias, m_gla_gb_up, m_gla_gb_bias, m_gla_out_norm, m_mla_cq_norm, m_mla_ckv_norm, m_mla_w_uq, m_mla_w_ukv, m_mla_q_norm, m_mla_k_norm, m_w_br_na, m_w_br_gla, m_w_br_mla, m_w_out, m_ffn2_norm, m_ffn2_w1, m_ffn2_w3, m_ffn2_w2, v_ffn1_norm, v_ffn1_w1, v_ffn1_w3, v_ffn1_w2, v_mix_norm, v_w_in, v_na_q_norm, v_na_k_norm, v_na_rpb, v_gla_gf_up, v_gla_gf_bias, v_gla_gb_up, v_gla_gb_bias, v_gla_out_norm, v_mla_cq_norm, v_mla_ckv_norm, v_mla_w_uq, v_mla_w_ukv, v_mla_q_norm, v_mla_k_norm, v_w_br_na, v_w_br_gla, v_w_br_mla, v_w_out, v_ffn2_norm, v_ffn2_w1, v_ffn2_w3, v_ffn2_w2):
    given = dict(locals())
    P = {n: given[n] for n in WEIGHTS}
    M = {n: given['m_' + n] for n in WEIGHTS}
    V = {n: given['v_' + n] for n in WEIGHTS}
    xs = x[0]
    T, D = xs.shape
    L = ffn1_norm.shape[0]
    lay, zp, d_in = _z_layout(D)

    def arrays_of(items):
        return [(l, names) for l, p in items for names in EXCHANGE_PARTS[p]]

    ws = [{n: P[n][l][None, :] for n in REPLICATED if n != 'na_rpb'} for l in range(L)]

    def gather_srcs(items):
        return [jnp.stack([P[n][l].astype(BF16) for n in names]) for l, names in arrays_of(items)]

    def take_gathered(items, res):
        for (l, names), g in zip(arrays_of(items), res):
            for k, n in enumerate(names):
                blk = g[:, k]
                if n == 'w_in':
                    ws[l]['w_in_z'] = _z_cols_from_blocks(blk, lay)
                    continue
                if n in COL_SHARDED:
                    full = jnp.concatenate([blk[d] for d in range(N_DEV)], axis=1)
                else:
                    full = blk.reshape(N_DEV * blk.shape[1], blk.shape[2])
                if n == 'gla_gf_up':
                    ws[l]['up_f'] = jnp.zeros((LANE, GLA_QK_W), BF16).at[:GLA_RANK].set(full)
                elif n == 'gla_gb_up':
                    ws[l]['up_b'] = jnp.zeros((LANE, GLA_QK_W), BF16).at[GLA_RANK:2 * GLA_RANK].set(full)
                else:
                    ws[l][n] = full

    def plan(table, l, make_srcs, broadcast, deliver):
        carry = {}
        for key, items in table.items():
            items = [(l + off, p) for off, p in items if l + off < L]
            if items:
                carry[key] = (functools.partial(make_srcs, items), broadcast, functools.partial(deliver, items))
        return carry

    ctab, stab = _rope_tables(T)
    na_bias, onehot = _na_bias_tables(na_rpb)

    edge = [(0, p) for p in FWD_EDGE]
    take_gathered(edge, _exchange(gather_srcs(edge), True, "gather_weights_edge"))
    saved = []
    h = xs
    for l in range(L):
        h, sv = _layer_fwd(h, ws[l], l, lay, (ctab, stab, na_bias), plan(FWD_PLAN, l, gather_srcs, True, take_gathered))
        saved.append(sv)
    dh, loss_local = _loss_head(h, loss_target[0], "loss_head")
    loss = lax.psum(loss_local[0, 0], ("x", "y", "c"))

    layer_grads = [dict() for _ in range(L)]

    def scatter_srcs(items):
        def blocks(l, n):
            g = layer_grads[l][n]
            if n == 'w_in':
                return _blocks_from_z_cols(g, lay, d_in).astype(BF16)
            if n in COL_SHARDED:
                c = g.shape[1] // N_DEV
                return jnp.stack([g[:, d * c:(d + 1) * c] for d in range(N_DEV)]).astype(BF16)
            return g.reshape(N_DEV, -1, g.shape[1]).astype(BF16)
        return [jnp.stack([blocks(l, n) for n in names], axis=1) for l, names in arrays_of(items)]

    landed = [dict() for _ in range(L)]

    def take_landed(items, res):
        for (l, names), r in zip(arrays_of(items), res):
            for k, n in enumerate(names):
                landed[l][n] = (r, k)

    for l in reversed(range(L)):
        table = {**BWD_PLAN, **BWD_PLAN_LAYER0} if l == 0 else BWD_PLAN
        dh = _layer_bwd(layer_grads[l], dh, saved[l], ws[l], l, lay, (ctab, stab, na_bias, onehot),
                        plan(table, l, scatter_srcs, False, take_landed))
    grad_x = dh[None]
    edge = [(0, p) for p in BWD_EDGE]
    take_landed(edge, _exchange(scatter_srcs(edge), False, "scatter_grads_edge"))

    out = {}
    for n in SHARDED:
        out[n] = _adamw_sharded([landed[l][n][0] for l in range(L)], landed[0][n][1], P[n], M[n], V[n], f"adamw_{n}")

    rep = [n for n in REPLICATED]
    rep_shapes = [tuple(P[n].shape) for n in rep]
    rep_partial = _pack([jnp.stack([layer_grads[l][n].reshape(P[n].shape[1:]) for l in range(L)]) for n in rep])
    (rep_landed,) = _exchange([rep_partial], True, "gather_small_grads")
    packed = _adamw_replicated(rep_landed, _pack([P[n] for n in rep]), _pack([M[n] for n in rep]),
                               _pack([V[n] for n in rep]), "adamw_replicated")
    for kind, pk in enumerate(packed):
        for n, arr in zip(rep, _unpack(pk, rep_shapes)):
            out.setdefault(n, [None] * 4)[kind] = arr

    res = [loss, grad_x]
    for kind in range(4):
        res += [out[n][kind] for n in WEIGHTS]
    return tuple(res)
```
